```python
import jax, jax.numpy as jnp
from jax import lax
import numpy as np

D_MODEL = 1024
BATCH = 8
SEQ = 8192
DEPTH = 2

CHUNK = 64
EPS = 1e-6
HEAD_DIM = 64
H_A = D_MODEL // (2 * HEAD_DIM)
A_PREV_CHUNKS = 8
MAX_REL_DIST = 256
H_B = D_MODEL // (2 * HEAD_DIM)
H_B_KV = H_B // 4
B_WINDOW = 128
B_PREV_CHUNKS = B_WINDOW // CHUNK
ATTN_PROJ = 3 * H_A * HEAD_DIM + H_B * HEAD_DIM + 2 * H_B_KV * HEAD_DIM
D_INNER = 2 * D_MODEL
SSM_HEAD_DIM = 64
SSM_HEADS = D_INNER // SSM_HEAD_DIM
SSM_GROUPS = 4
SSM_STATE = 128
SSM_CONV = 4
SSD_CHUNK = 64
SSM_CONV_CH = D_INNER + 2 * SSM_GROUPS * SSM_STATE
SSM_PROJ = D_INNER + SSM_CONV_CH + SSM_HEADS
D_FF = ((8 * D_MODEL // 3 + 127) // 128) * 128
FFN_CONV = 3

kernel_name = "chunk_causal_hybrid_attn_ssd_convffn"


def rmsnorm(x, g):
    xf = x.astype(jnp.float32)
    y = xf * lax.rsqrt(jnp.mean(xf * xf, axis=-1, keepdims=True) + EPS)
    return (y * g.astype(jnp.float32)).astype(x.dtype)


def causal_dwconv(x, w, b):
    k = w.shape[0]
    y = lax.conv_general_dilated(
        x, w[:, None, :].astype(x.dtype), window_strides=(1,), padding=[(k - 1, 0)],
        dimension_numbers=("NWC", "WIO", "NWC"), feature_group_count=x.shape[-1])
    return y + b.astype(x.dtype)


def band_offsets(n_prev):
    band = (n_prev + 1) * CHUNK
    q_off = jnp.arange(CHUNK, dtype=jnp.int32)
    k_off = jnp.arange(band, dtype=jnp.int32) - n_prev * CHUNK
    return q_off[:, None] - k_off[None, :], k_off


def band_attention(q, k, v, n_prev, bias, sinks):
    b, s, hq, d = q.shape
    hkv = k.shape[2]
    grp = hq // hkv
    nc = s // CHUNK
    band = (n_prev + 1) * CHUNK
    pad = n_prev * CHUNK
    kp = jnp.pad(k, ((0, 0), (pad, 0), (0, 0), (0, 0)))
    vp = jnp.pad(v, ((0, 0), (pad, 0), (0, 0), (0, 0)))
    qc = jnp.moveaxis(q.reshape(b, nc, CHUNK, hkv, grp, d), 1, 0)
    _, k_off = band_offsets(n_prev)
    scale = d ** -0.5

    def one_chunk(args):
        c, qb = args
        start = c * CHUNK
        kb = lax.dynamic_slice_in_dim(kp, start, band, axis=1)
        vb = lax.dynamic_slice_in_dim(vp, start, band, axis=1)
        sc = jnp.einsum("bqkgd,bskd->bkgqs", qb, kb).astype(jnp.float32) * scale + bias
        valid = (start + k_off) >= 0
        sc = jnp.where(valid, sc, -jnp.inf)
        if sinks is None:
            p = jax.nn.softmax(sc, axis=-1)
        else:
            snk = sinks.astype(jnp.float32)[None, :, :, None, None]
            m = jnp.maximum(jnp.max(sc, axis=-1, keepdims=True), snk)
            e = jnp.exp(sc - m)
            p = e / (jnp.sum(e, axis=-1, keepdims=True) + jnp.exp(snk - m))
        return jnp.einsum("bkgqs,bskd->bqkgd", p.astype(vb.dtype), vb)

    out = lax.map(one_chunk, (jnp.arange(nc, dtype=jnp.int32), qc))
    return jnp.moveaxis(out, 0, 1).reshape(b, s, hq * d)


def attn_layer(h, w_in, w_out, relpos_table, q_norm_a, k_norm_a, q_norm_b, k_norm_b, sinks):
    b, s, _ = h.shape
    da, db, dkv = H_A * HEAD_DIM, H_B * HEAD_DIM, H_B_KV * HEAD_DIM
    cuts = [da, 2 * da, 3 * da, 3 * da + db, 3 * da + db + dkv]
    qa, ka, va, qb, kb, vb = jnp.split(h @ w_in, cuts, axis=-1)
    heads = lambda t, n: t.reshape(b, s, n, HEAD_DIM)
    qa = rmsnorm(heads(qa, H_A), q_norm_a)
    ka = rmsnorm(heads(ka, H_A), k_norm_a)
    rel_a, _ = band_offsets(A_PREV_CHUNKS)
    idx = jnp.clip(rel_a, -MAX_REL_DIST, MAX_REL_DIST) + MAX_REL_DIST
    bias_a = relpos_table.astype(jnp.float32)[:, idx][:, None]
    oa = band_attention(qa, ka, heads(va, H_A), A_PREV_CHUNKS, bias_a, None)
    qb = rmsnorm(heads(qb, H_B), q_norm_b)
    kb = rmsnorm(heads(kb, H_B_KV), k_norm_b)
    rel_b, _ = band_offsets(B_PREV_CHUNKS)
    slopes = 2.0 ** (-8.0 * jnp.arange(1, H_B + 1, dtype=jnp.float32) / H_B)
    bias_b = (-slopes[:, None, None] * jnp.abs(rel_b).astype(jnp.float32)).reshape(
        H_B_KV, H_B // H_B_KV, CHUNK, (B_PREV_CHUNKS + 1) * CHUNK)
    ob = band_attention(qb, kb, heads(vb, H_B_KV), B_PREV_CHUNKS, bias_b,
                        sinks.reshape(H_B_KV, H_B // H_B_KV))
    return jnp.concatenate([oa, ob], axis=-1) @ w_out


def ssd_scan(x, dt, a, bm, cm):
    b, s, h, p = x.shape
    g, n = bm.shape[2], bm.shape[3]
    r = h // g
    L = SSD_CHUNK
    nc = s // L

    def to_chunks(t):
        return jnp.moveaxis(t.reshape((b, nc, L) + t.shape[2:]), 1, 0)

    xc = to_chunks(x.reshape(b, s, g, r, p))
    dtc = to_chunks(dt.reshape(b, s, g, r))
    bc, cc = to_chunks(bm), to_chunks(cm)
    a = a.reshape(g, r)
    causal = jnp.tril(jnp.ones((L, L), dtype=bool))[None, :, :, None, None]

    def step(state, inp):
        xk, dtk, bk, ck = inp
        acs = jnp.cumsum(dtk * a, axis=1)
        seg = acs[:, :, None] - acs[:, None, :]
        decay = jnp.exp(jnp.where(causal, seg, -jnp.inf))
        cb = jnp.einsum("blgn,bsgn->bgls", ck, bk)
        y_intra = jnp.einsum("bgls,blsgr,bsgrp->blgrp", cb, decay, xk * dtk[..., None])
        y_state = jnp.einsum("blgn,bgrpn->blgrp", ck, state) * jnp.exp(acs)[..., None]
        last = acs[:, -1]
        w_in = jnp.exp(last[:, None] - acs) * dtk
        new_state = state * jnp.exp(last)[..., None, None] + jnp.einsum(
            "bsgn,bsgr,bsgrp->bgrpn", bk, w_in, xk)
        return new_state, y_intra + y_state

    state0 = jnp.zeros((b, g, r, p, n), jnp.float32)
    _, ys = lax.scan(step, state0, (xc, dtc, bc, cc))
    return jnp.moveaxis(ys, 0, 1).reshape(b, s, h, p)


def ssm_layer(h, w_in, conv_w, conv_b, dt_bias, a_log, d_skip, norm_w, w_out):
    b, s, _ = h.shape
    z, xbc, dt = jnp.split(h @ w_in, [D_INNER, D_INNER + SSM_CONV_CH], axis=-1)
    xbc = jax.nn.silu(causal_dwconv(xbc, conv_w, conv_b))
    xs, bm, cm = jnp.split(xbc, [D_INNER, D_INNER + SSM_GROUPS * SSM_STATE], axis=-1)
    xs = xs.reshape(b, s, SSM_HEADS, SSM_HEAD_DIM).astype(jnp.float32)
    bm = bm.reshape(b, s, SSM_GROUPS, SSM_STATE).astype(jnp.float32)
    cm = cm.reshape(b, s, SSM_GROUPS, SSM_STATE).astype(jnp.float32)
    dt = jax.nn.softplus(dt.astype(jnp.float32) + dt_bias.astype(jnp.float32))
    a = -jnp.exp(a_log.astype(jnp.float32))
    y = ssd_scan(xs, dt, a, bm, cm) + d_skip.astype(jnp.float32)[:, None] * xs
    y = y.reshape(b, s, D_INNER) * jax.nn.silu(z.astype(jnp.float32))
    yg = y.reshape(b, s, SSM_GROUPS, D_INNER // SSM_GROUPS)
    yg = yg * lax.rsqrt(jnp.mean(yg * yg, axis=-1, keepdims=True) + EPS)
    y = (yg.reshape(b, s, D_INNER) * norm_w.astype(jnp.float32)).astype(h.dtype)
    return y @ w_out


def conv_ffn(h, w_in, conv_w, conv_b, w_out):
    gate, up = jnp.split(h @ w_in, [D_FF], axis=-1)
    gate = causal_dwconv(gate, conv_w, conv_b)
    return (jax.nn.silu(gate) * up) @ w_out


def _fwd_setup_inputs(seed: int = 0) -> dict:
    key = jax.random.key(seed)
    ks = jax.random.split(key, 24)
    n_even, n_odd = (DEPTH + 1) // 2, DEPTH // 2
    nrm = lambda k, shape, scale: jax.random.normal(k, shape, jnp.float32) * scale
    dt0 = jnp.exp(jax.random.uniform(ks[15], (n_odd, SSM_HEADS), jnp.float32,
                                     np.log(1e-3), np.log(1e-1)))
    return {
        "x": nrm(ks[0], (BATCH, SEQ, D_MODEL), 1.0),
        "norm_mix": 1.0 + nrm(ks[1], (DEPTH, D_MODEL), 0.05),
        "norm_ffn": 1.0 + nrm(ks[2], (DEPTH, D_MODEL), 0.05),
        "attn_w_in": nrm(ks[3], (n_even, D_MODEL, ATTN_PROJ), D_MODEL ** -0.5),
        "attn_w_out": nrm(ks[4], (n_even, (H_A + H_B) * HEAD_DIM, D_MODEL), ((H_A + H_B) * HEAD_DIM) ** -0.5),
        "relpos_table": nrm(ks[5], (n_even, H_A, 2 * MAX_REL_DIST + 1), 0.2),
        "q_norm_a": 1.0 + nrm(ks[6], (n_even, HEAD_DIM), 0.05),
        "k_norm_a": 1.0 + nrm(ks[7], (n_even, HEAD_DIM), 0.05),
        "q_norm_b": 1.0 + nrm(ks[8], (n_even, HEAD_DIM), 0.05),
        "k_norm_b": 1.0 + nrm(ks[9], (n_even, HEAD_DIM), 0.05),
        "sinks": nrm(ks[10], (n_even, H_B), 0.5),
        "ssm_w_in": nrm(ks[11], (n_odd, D_MODEL, SSM_PROJ), D_MODEL ** -0.5),
        "ssm_conv_w": nrm(ks[12], (n_odd, SSM_CONV, SSM_CONV_CH), SSM_CONV ** -0.5),
        "ssm_conv_b": nrm(ks[13], (n_odd, SSM_CONV_CH), 0.02),
        "ssm_dt_bias": dt0 + jnp.log(-jnp.expm1(-dt0)),
        "ssm_a_log": jnp.log(jax.random.uniform(ks[14], (n_odd, SSM_HEADS), jnp.float32, 1.0, 16.0)),
        "ssm_d": 1.0 + nrm(ks[16], (n_odd, SSM_HEADS), 0.1),
        "ssm_norm": 1.0 + nrm(ks[17], (n_odd, D_INNER), 0.05),
        "ssm_w_out": nrm(ks[18], (n_odd, D_INNER, D_MODEL), D_INNER ** -0.5),
        "ffn_w_in": nrm(ks[19], (DEPTH, D_MODEL, 2 * D_FF), D_MODEL ** -0.5),
        "ffn_conv_w": nrm(ks[20], (DEPTH, FFN_CONV, D_FF), FFN_CONV ** -0.5),
        "ffn_conv_b": nrm(ks[21], (DEPTH, D_FF), 0.02),
        "ffn_w_out": nrm(ks[22], (DEPTH, D_FF, D_MODEL), D_FF ** -0.5),
    }


def _fwd_reference(x, norm_mix, norm_ffn, attn_w_in, attn_w_out, relpos_table, q_norm_a, k_norm_a,
              q_norm_b, k_norm_b, sinks, ssm_w_in, ssm_conv_w, ssm_conv_b, ssm_dt_bias, ssm_a_log,
              ssm_d, ssm_norm, ssm_w_out, ffn_w_in, ffn_conv_w, ffn_conv_b, ffn_w_out):
    for layer in range(DEPTH):
        i = layer // 2
        h = rmsnorm(x, norm_mix[layer])
        if layer % 2 == 0:
            mix = attn_layer(h, attn_w_in[i], attn_w_out[i], relpos_table[i], q_norm_a[i],
                             k_norm_a[i], q_norm_b[i], k_norm_b[i], sinks[i])
        else:
            mix = ssm_layer(h, ssm_w_in[i], ssm_conv_w[i], ssm_conv_b[i], ssm_dt_bias[i],
                            ssm_a_log[i], ssm_d[i], ssm_norm[i], ssm_w_out[i])
        x = x + mix
        h = rmsnorm(x, norm_ffn[layer])
        x = x + conv_ffn(h, ffn_w_in[layer], ffn_conv_w[layer], ffn_conv_b[layer], ffn_w_out[layer])
    return x


import jax as _jax
import jax.numpy as _jnp

TWIN_FORMAT = 'train_step'
FWD_PARAMS = ['x', 'norm_mix', 'norm_ffn', 'attn_w_in', 'attn_w_out', 'relpos_table', 'q_norm_a', 'k_norm_a', 'q_norm_b', 'k_norm_b', 'sinks', 'ssm_w_in', 'ssm_conv_w', 'ssm_conv_b', 'ssm_dt_bias', 'ssm_a_log', 'ssm_d', 'ssm_norm', 'ssm_w_out', 'ffn_w_in', 'ffn_conv_w', 'ffn_conv_b', 'ffn_w_out']
TWIN_WEIGHTS = ['norm_mix', 'norm_ffn', 'attn_w_in', 'attn_w_out', 'relpos_table', 'q_norm_a', 'k_norm_a', 'q_norm_b', 'k_norm_b', 'sinks', 'ssm_w_in', 'ssm_conv_w', 'ssm_conv_b', 'ssm_dt_bias', 'ssm_a_log', 'ssm_d', 'ssm_norm', 'ssm_w_out', 'ffn_w_in', 'ffn_conv_w', 'ffn_conv_b', 'ffn_w_out']
TWIN_DIFF_INPUT = 'x'
TWIN_INPUTS = ['x', 'norm_mix', 'norm_ffn', 'attn_w_in', 'attn_w_out', 'relpos_table', 'q_norm_a', 'k_norm_a', 'q_norm_b', 'k_norm_b', 'sinks', 'ssm_w_in', 'ssm_conv_w', 'ssm_conv_b', 'ssm_dt_bias', 'ssm_a_log', 'ssm_d', 'ssm_norm', 'ssm_w_out', 'ffn_w_in', 'ffn_conv_w', 'ffn_conv_b', 'ffn_w_out', 'loss_target', 'm_norm_mix', 'm_norm_ffn', 'm_attn_w_in', 'm_attn_w_out', 'm_relpos_table', 'm_q_norm_a', 'm_k_norm_a', 'm_q_norm_b', 'm_k_norm_b', 'm_sinks', 'm_ssm_w_in', 'm_ssm_conv_w', 'm_ssm_conv_b', 'm_ssm_dt_bias', 'm_ssm_a_log', 'm_ssm_d', 'm_ssm_norm', 'm_ssm_w_out', 'm_ffn_w_in', 'm_ffn_conv_w', 'm_ffn_conv_b', 'm_ffn_w_out', 'v_norm_mix', 'v_norm_ffn', 'v_attn_w_in', 'v_attn_w_out', 'v_relpos_table', 'v_q_norm_a', 'v_k_norm_a', 'v_q_norm_b', 'v_k_norm_b', 'v_sinks', 'v_ssm_w_in', 'v_ssm_conv_w', 'v_ssm_conv_b', 'v_ssm_dt_bias', 'v_ssm_a_log', 'v_ssm_d', 'v_ssm_norm', 'v_ssm_w_out', 'v_ffn_w_in', 'v_ffn_conv_w', 'v_ffn_conv_b', 'v_ffn_w_out']
TWIN_OUTPUTS = ['loss', 'grad_x', 'grad_norm_mix', 'grad_norm_ffn', 'grad_attn_w_in', 'grad_attn_w_out', 'grad_relpos_table', 'grad_q_norm_a', 'grad_k_norm_a', 'grad_q_norm_b', 'grad_k_norm_b', 'grad_sinks', 'grad_ssm_w_in', 'grad_ssm_conv_w', 'grad_ssm_conv_b', 'grad_ssm_dt_bias', 'grad_ssm_a_log', 'grad_ssm_d', 'grad_ssm_norm', 'grad_ssm_w_out', 'grad_ffn_w_in', 'grad_ffn_conv_w', 'grad_ffn_conv_b', 'grad_ffn_w_out', 'delta_norm_mix', 'delta_norm_ffn', 'delta_attn_w_in', 'delta_attn_w_out', 'delta_relpos_table', 'delta_q_norm_a', 'delta_k_norm_a', 'delta_q_norm_b', 'delta_k_norm_b', 'delta_sinks', 'delta_ssm_w_in', 'delta_ssm_conv_w', 'delta_ssm_conv_b', 'delta_ssm_dt_bias', 'delta_ssm_a_log', 'delta_ssm_d', 'delta_ssm_norm', 'delta_ssm_w_out', 'delta_ffn_w_in', 'delta_ffn_conv_w', 'delta_ffn_conv_b', 'delta_ffn_w_out', 'new_m_norm_mix', 'new_m_norm_ffn', 'new_m_attn_w_in', 'new_m_attn_w_out', 'new_m_relpos_table', 'new_m_q_norm_a', 'new_m_k_norm_a', 'new_m_q_norm_b', 'new_m_k_norm_b', 'new_m_sinks', 'new_m_ssm_w_in', 'new_m_ssm_conv_w', 'new_m_ssm_conv_b', 'new_m_ssm_dt_bias', 'new_m_ssm_a_log', 'new_m_ssm_d', 'new_m_ssm_norm', 'new_m_ssm_w_out', 'new_m_ffn_w_in', 'new_m_ffn_conv_w', 'new_m_ffn_conv_b', 'new_m_ffn_w_out', 'new_v_norm_mix', 'new_v_norm_ffn', 'new_v_attn_w_in', 'new_v_attn_w_out', 'new_v_relpos_table', 'new_v_q_norm_a', 'new_v_k_norm_a', 'new_v_q_norm_b', 'new_v_k_norm_b', 'new_v_sinks', 'new_v_ssm_w_in', 'new_v_ssm_conv_w', 'new_v_ssm_conv_b', 'new_v_ssm_dt_bias', 'new_v_ssm_a_log', 'new_v_ssm_d', 'new_v_ssm_norm', 'new_v_ssm_w_out', 'new_v_ffn_w_in', 'new_v_ffn_conv_w', 'new_v_ffn_conv_b', 'new_v_ffn_w_out']
TWIN_LEAF_KINDS = {'loss': 'loss', 'grad_x': 'grad_x', 'grad_norm_mix': 'grad_w', 'grad_norm_ffn': 'grad_w', 'grad_attn_w_in': 'grad_w', 'grad_attn_w_out': 'grad_w', 'grad_relpos_table': 'grad_w', 'grad_q_norm_a': 'grad_w', 'grad_k_norm_a': 'grad_w', 'grad_q_norm_b': 'grad_w', 'grad_k_norm_b': 'grad_w', 'grad_sinks': 'grad_w', 'grad_ssm_w_in': 'grad_w', 'grad_ssm_conv_w': 'grad_w', 'grad_ssm_conv_b': 'grad_w', 'grad_ssm_dt_bias': 'grad_w', 'grad_ssm_a_log': 'grad_w', 'grad_ssm_d': 'grad_w', 'grad_ssm_norm': 'grad_w', 'grad_ssm_w_out': 'grad_w', 'grad_ffn_w_in': 'grad_w', 'grad_ffn_conv_w': 'grad_w', 'grad_ffn_conv_b': 'grad_w', 'grad_ffn_w_out': 'grad_w', 'delta_norm_mix': 'delta_w', 'delta_norm_ffn': 'delta_w', 'delta_attn_w_in': 'delta_w', 'delta_attn_w_out': 'delta_w', 'delta_relpos_table': 'delta_w', 'delta_q_norm_a': 'delta_w', 'delta_k_norm_a': 'delta_w', 'delta_q_norm_b': 'delta_w', 'delta_k_norm_b': 'delta_w', 'delta_sinks': 'delta_w', 'delta_ssm_w_in': 'delta_w', 'delta_ssm_conv_w': 'delta_w', 'delta_ssm_conv_b': 'delta_w', 'delta_ssm_dt_bias': 'delta_w', 'delta_ssm_a_log': 'delta_w', 'delta_ssm_d': 'delta_w', 'delta_ssm_norm': 'delta_w', 'delta_ssm_w_out': 'delta_w', 'delta_ffn_w_in': 'delta_w', 'delta_ffn_conv_w': 'delta_w', 'delta_ffn_conv_b': 'delta_w', 'delta_ffn_w_out': 'delta_w', 'new_m_norm_mix': 'new_m', 'new_m_norm_ffn': 'new_m', 'new_m_attn_w_in': 'new_m', 'new_m_attn_w_out': 'new_m', 'new_m_relpos_table': 'new_m', 'new_m_q_norm_a': 'new_m', 'new_m_k_norm_a': 'new_m', 'new_m_q_norm_b': 'new_m', 'new_m_k_norm_b': 'new_m', 'new_m_sinks': 'new_m', 'new_m_ssm_w_in': 'new_m', 'new_m_ssm_conv_w': 'new_m', 'new_m_ssm_conv_b': 'new_m', 'new_m_ssm_dt_bias': 'new_m', 'new_m_ssm_a_log': 'new_m', 'new_m_ssm_d': 'new_m', 'new_m_ssm_norm': 'new_m', 'new_m_ssm_w_out': 'new_m', 'new_m_ffn_w_in': 'new_m', 'new_m_ffn_conv_w': 'new_m', 'new_m_ffn_conv_b': 'new_m', 'new_m_ffn_w_out': 'new_m', 'new_v_norm_mix': 'new_v', 'new_v_norm_ffn': 'new_v', 'new_v_attn_w_in': 'new_v', 'new_v_attn_w_out': 'new_v', 'new_v_relpos_table': 'new_v', 'new_v_q_norm_a': 'new_v', 'new_v_k_norm_a': 'new_v', 'new_v_q_norm_b': 'new_v', 'new_v_k_norm_b': 'new_v', 'new_v_sinks': 'new_v', 'new_v_ssm_w_in': 'new_v', 'new_v_ssm_conv_w': 'new_v', 'new_v_ssm_conv_b': 'new_v', 'new_v_ssm_dt_bias': 'new_v', 'new_v_ssm_a_log': 'new_v', 'new_v_ssm_d': 'new_v', 'new_v_ssm_norm': 'new_v', 'new_v_ssm_w_out': 'new_v', 'new_v_ffn_w_in': 'new_v', 'new_v_ffn_conv_w': 'new_v', 'new_v_ffn_conv_b': 'new_v', 'new_v_ffn_w_out': 'new_v'}


def _forward(args):
    return _fwd_reference(*[args[k] for k in FWD_PARAMS])


def _output_shape():
    out = _jax.eval_shape(lambda: _forward(_fwd_setup_inputs(0)))
    return out.shape, out.dtype

N_MICROBATCH = 1
ADAM_LR = 0.001
ADAM_B1 = 0.9
ADAM_B2 = 0.999
ADAM_EPS = 1e-08
ADAM_WD = 0.01
ADAM_STEP = 10
PER_EXAMPLE_BATCH_AXIS = {'x': 0, 'loss_target': 0}
SHARED_INPUTS = []
_WEIGHT_DTYPES = {'norm_mix': _jnp.float32, 'norm_ffn': _jnp.float32, 'attn_w_in': _jnp.float32, 'attn_w_out': _jnp.float32, 'relpos_table': _jnp.float32, 'q_norm_a': _jnp.float32, 'k_norm_a': _jnp.float32, 'q_norm_b': _jnp.float32, 'k_norm_b': _jnp.float32, 'sinks': _jnp.float32, 'ssm_w_in': _jnp.float32, 'ssm_conv_w': _jnp.float32, 'ssm_conv_b': _jnp.float32, 'ssm_dt_bias': _jnp.float32, 'ssm_a_log': _jnp.float32, 'ssm_d': _jnp.float32, 'ssm_norm': _jnp.float32, 'ssm_w_out': _jnp.float32, 'ffn_w_in': _jnp.float32, 'ffn_conv_w': _jnp.float32, 'ffn_conv_b': _jnp.float32, 'ffn_w_out': _jnp.float32}
MOMENT_SCALE = {'norm_mix': 1.090699e+00, 'norm_ffn': 5.121005e+01, 'attn_w_in': 3.019537e-01, 'attn_w_out': 3.020363e-01, 'relpos_table': 6.494546e-02, 'q_norm_a': 1.255061e+00, 'k_norm_a': 1.263492e+00, 'q_norm_b': 1.398375e+01, 'k_norm_b': 1.368901e+01, 'sinks': 2.676275e+01, 'ssm_w_in': 4.758373e-01, 'ssm_conv_w': 1.439023e+00, 'ssm_conv_b': 4.392523e+00, 'ssm_dt_bias': 2.996578e+00, 'ssm_a_log': 6.789992e+00, 'ssm_d': 7.605344e+00, 'ssm_norm': 4.534847e+01, 'ssm_w_out': 2.660146e+00, 'ffn_w_in': 4.926555e-01, 'ffn_conv_w': 5.451683e+00, 'ffn_conv_b': 6.713819e+00, 'ffn_w_out': 6.629333e-01}


def _to_microbatches(a, axis):
    t = _jnp.moveaxis(a, axis, 0)
    t = t.reshape((N_MICROBATCH, t.shape[0] // N_MICROBATCH) + t.shape[1:])
    return _jnp.moveaxis(t, 1, axis + 1)


def setup_inputs(seed: int = 0) -> dict:
    inp = _fwd_setup_inputs(seed)
    key = _jax.random.fold_in(_jax.random.key(seed), 7919)
    shape, _ = _output_shape()
    out = dict(inp)
    out["loss_target"] = _jax.random.normal(_jax.random.fold_in(key, 0), shape, _jnp.float32)
    for i, name in enumerate(TWIN_WEIGHTS):
        w = inp[name].astype(_jnp.float32)
        if MOMENT_SCALE is None:
            s = _jnp.sqrt(_jnp.mean(_jnp.square(w)) + 1e-30)
        else:
            s = MOMENT_SCALE[name]
        km, kv = _jax.random.split(_jax.random.fold_in(key, i + 1))
        out[name] = w
        out["m_" + name] = s * _jax.random.normal(km, w.shape, _jnp.float32)
        out["v_" + name] = (s * s) * _jax.random.uniform(kv, w.shape, _jnp.float32, 0.5, 1.5)
    if N_MICROBATCH > 1:
        for name, axis in PER_EXAMPLE_BATCH_AXIS.items():
            out[name] = _to_microbatches(out[name], axis)
    return {'x': out['x'], 'norm_mix': out['norm_mix'], 'norm_ffn': out['norm_ffn'], 'attn_w_in': out['attn_w_in'], 'attn_w_out': out['attn_w_out'], 'relpos_table': out['relpos_table'], 'q_norm_a': out['q_norm_a'], 'k_norm_a': out['k_norm_a'], 'q_norm_b': out['q_norm_b'], 'k_norm_b': out['k_norm_b'], 'sinks': out['sinks'], 'ssm_w_in': out['ssm_w_in'], 'ssm_conv_w': out['ssm_conv_w'], 'ssm_conv_b': out['ssm_conv_b'], 'ssm_dt_bias': out['ssm_dt_bias'], 'ssm_a_log': out['ssm_a_log'], 'ssm_d': out['ssm_d'], 'ssm_norm': out['ssm_norm'], 'ssm_w_out': out['ssm_w_out'], 'ffn_w_in': out['ffn_w_in'], 'ffn_conv_w': out['ffn_conv_w'], 'ffn_conv_b': out['ffn_conv_b'], 'ffn_w_out': out['ffn_w_out'], 'loss_target': out['loss_target'], 'm_norm_mix': out['m_norm_mix'], 'm_norm_ffn': out['m_norm_ffn'], 'm_attn_w_in': out['m_attn_w_in'], 'm_attn_w_out': out['m_attn_w_out'], 'm_relpos_table': out['m_relpos_table'], 'm_q_norm_a': out['m_q_norm_a'], 'm_k_norm_a': out['m_k_norm_a'], 'm_q_norm_b': out['m_q_norm_b'], 'm_k_norm_b': out['m_k_norm_b'], 'm_sinks': out['m_sinks'], 'm_ssm_w_in': out['m_ssm_w_in'], 'm_ssm_conv_w': out['m_ssm_conv_w'], 'm_ssm_conv_b': out['m_ssm_conv_b'], 'm_ssm_dt_bias': out['m_ssm_dt_bias'], 'm_ssm_a_log': out['m_ssm_a_log'], 'm_ssm_d': out['m_ssm_d'], 'm_ssm_norm': out['m_ssm_norm'], 'm_ssm_w_out': out['m_ssm_w_out'], 'm_ffn_w_in': out['m_ffn_w_in'], 'm_ffn_conv_w': out['m_ffn_conv_w'], 'm_ffn_conv_b': out['m_ffn_conv_b'], 'm_ffn_w_out': out['m_ffn_w_out'], 'v_norm_mix': out['v_norm_mix'], 'v_norm_ffn': out['v_norm_ffn'], 'v_attn_w_in': out['v_attn_w_in'], 'v_attn_w_out': out['v_attn_w_out'], 'v_relpos_table': out['v_relpos_table'], 'v_q_norm_a': out['v_q_norm_a'], 'v_k_norm_a': out['v_k_norm_a'], 'v_q_norm_b': out['v_q_norm_b'], 'v_k_norm_b': out['v_k_norm_b'], 'v_sinks': out['v_sinks'], 'v_ssm_w_in': out['v_ssm_w_in'], 'v_ssm_conv_w': out['v_ssm_conv_w'], 'v_ssm_conv_b': out['v_ssm_conv_b'], 'v_ssm_dt_bias': out['v_ssm_dt_bias'], 'v_ssm_a_log': out['v_ssm_a_log'], 'v_ssm_d': out['v_ssm_d'], 'v_ssm_norm': out['v_ssm_norm'], 'v_ssm_w_out': out['v_ssm_w_out'], 'v_ffn_w_in': out['v_ffn_w_in'], 'v_ffn_conv_w': out['v_ffn_conv_w'], 'v_ffn_conv_b': out['v_ffn_conv_b'], 'v_ffn_w_out': out['v_ffn_w_out']}


def _loss(weights, diff, rest, loss_target):
    with _jax.named_scope("forward"):
        args = {**rest, TWIN_DIFF_INPUT: diff, **{k: w.astype(_WEIGHT_DTYPES[k]) for k, w in weights.items()}}
        y = _forward(args)
    with _jax.named_scope("loss_head"):
        err = _jnp.square(y.astype(_jnp.float32) - loss_target)
        return 0.5 * _jnp.sum(_jnp.mean(err, axis=-1)) if err.ndim else 0.5 * err


def _adamw(w, g, m, v):
    m = ADAM_B1 * m + (1.0 - ADAM_B1) * g
    v = ADAM_B2 * v + (1.0 - ADAM_B2) * _jnp.square(g)
    m_hat = m / (1.0 - ADAM_B1 ** ADAM_STEP)
    v_hat = v / (1.0 - ADAM_B2 ** ADAM_STEP)
    delta = -ADAM_LR * (m_hat / (_jnp.sqrt(v_hat) + ADAM_EPS) + ADAM_WD * w)
    return delta, m, v


def reference(x, norm_mix, norm_ffn, attn_w_in, attn_w_out, relpos_table, q_norm_a, k_norm_a, q_norm_b, k_norm_b, sinks, ssm_w_in, ssm_conv_w, ssm_conv_b, ssm_dt_bias, ssm_a_log, ssm_d, ssm_norm, ssm_w_out, ffn_w_in, ffn_conv_w, ffn_conv_b, ffn_w_out, loss_target, m_norm_mix, m_norm_ffn, m_attn_w_in, m_attn_w_out, m_relpos_table, m_q_norm_a, m_k_norm_a, m_q_norm_b, m_k_norm_b, m_sinks, m_ssm_w_in, m_ssm_conv_w, m_ssm_conv_b, m_ssm_dt_bias, m_ssm_a_log, m_ssm_d, m_ssm_norm, m_ssm_w_out, m_ffn_w_in, m_ffn_conv_w, m_ffn_conv_b, m_ffn_w_out, v_norm_mix, v_norm_ffn, v_attn_w_in, v_attn_w_out, v_relpos_table, v_q_norm_a, v_k_norm_a, v_q_norm_b, v_k_norm_b, v_sinks, v_ssm_w_in, v_ssm_conv_w, v_ssm_conv_b, v_ssm_dt_bias, v_ssm_a_log, v_ssm_d, v_ssm_norm, v_ssm_w_out, v_ffn_w_in, v_ffn_conv_w, v_ffn_conv_b, v_ffn_w_out):
    given = dict(x=x, norm_mix=norm_mix, norm_ffn=norm_ffn, attn_w_in=attn_w_in, attn_w_out=attn_w_out, relpos_table=relpos_table, q_norm_a=q_norm_a, k_norm_a=k_norm_a, q_norm_b=q_norm_b, k_norm_b=k_norm_b, sinks=sinks, ssm_w_in=ssm_w_in, ssm_conv_w=ssm_conv_w, ssm_conv_b=ssm_conv_b, ssm_dt_bias=ssm_dt_bias, ssm_a_log=ssm_a_log, ssm_d=ssm_d, ssm_norm=ssm_norm, ssm_w_out=ssm_w_out, ffn_w_in=ffn_w_in, ffn_conv_w=ffn_conv_w, ffn_conv_b=ffn_conv_b, ffn_w_out=ffn_w_out, loss_target=loss_target, m_norm_mix=m_norm_mix, m_norm_ffn=m_norm_ffn, m_attn_w_in=m_attn_w_in, m_attn_w_out=m_attn_w_out, m_relpos_table=m_relpos_table, m_q_norm_a=m_q_norm_a, m_k_norm_a=m_k_norm_a, m_q_norm_b=m_q_norm_b, m_k_norm_b=m_k_norm_b, m_sinks=m_sinks, m_ssm_w_in=m_ssm_w_in, m_ssm_conv_w=m_ssm_conv_w, m_ssm_conv_b=m_ssm_conv_b, m_ssm_dt_bias=m_ssm_dt_bias, m_ssm_a_log=m_ssm_a_log, m_ssm_d=m_ssm_d, m_ssm_norm=m_ssm_norm, m_ssm_w_out=m_ssm_w_out, m_ffn_w_in=m_ffn_w_in, m_ffn_conv_w=m_ffn_conv_w, m_ffn_conv_b=m_ffn_conv_b, m_ffn_w_out=m_ffn_w_out, v_norm_mix=v_norm_mix, v_norm_ffn=v_norm_ffn, v_attn_w_in=v_attn_w_in, v_attn_w_out=v_attn_w_out, v_relpos_table=v_relpos_table, v_q_norm_a=v_q_norm_a, v_k_norm_a=v_k_norm_a, v_q_norm_b=v_q_norm_b, v_k_norm_b=v_k_norm_b, v_sinks=v_sinks, v_ssm_w_in=v_ssm_w_in, v_ssm_conv_w=v_ssm_conv_w, v_ssm_conv_b=v_ssm_conv_b, v_ssm_dt_bias=v_ssm_dt_bias, v_ssm_a_log=v_ssm_a_log, v_ssm_d=v_ssm_d, v_ssm_norm=v_ssm_norm, v_ssm_w_out=v_ssm_w_out, v_ffn_w_in=v_ffn_w_in, v_ffn_conv_w=v_ffn_conv_w, v_ffn_conv_b=v_ffn_conv_b, v_ffn_w_out=v_ffn_w_out)
    weights = {n: given[n] for n in TWIN_WEIGHTS}
    shared = {n: given[n] for n in SHARED_INPUTS}
    per_example = {n: given[n] for n in ['x']}
    grad_fn = _jax.value_and_grad(_loss, argnums=(0, 1))

    def one_microbatch(ex, loss_target):
        ex = dict(ex)
        diff = ex.pop(TWIN_DIFF_INPUT)
        return grad_fn(weights, diff, {**shared, **ex}, loss_target)

    if N_MICROBATCH == 1:
        loss, (grad_w, grad_x) = one_microbatch(per_example, given["loss_target"])
    else:
        def body(carry, xs):
            loss_sum, grad_sum = carry
            l_k, (gw_k, gx_k) = one_microbatch(xs[0], xs[1])
            with _jax.named_scope("update"):
                return (loss_sum + l_k, _jax.tree.map(_jnp.add, grad_sum, gw_k)), gx_k

        init = (_jnp.zeros((), _jnp.float32), _jax.tree.map(_jnp.zeros_like, weights))
        (loss, grad_w), grad_x = _jax.lax.scan(body, init, (per_example, given["loss_target"]))
    with _jax.named_scope("update"):
        delta_w, new_m, new_v = {}, {}, {}
        for n in TWIN_WEIGHTS:
            delta_w[n], new_m[n], new_v[n] = _adamw(weights[n], grad_w[n], given["m_" + n], given["v_" + n])
    return (loss, grad_x, *[grad_w[n] for n in TWIN_WEIGHTS], *[delta_w[n] for n in TWIN_WEIGHTS],
            *[new_m[n] for n in TWIN_WEIGHTS], *[new_v[n] for n in TWIN_WEIGHTS])
```

```python
import functools

import numpy as np
import jax
import jax.numpy as jnp
from jax import lax
from jax.experimental import pallas as pl
from jax.experimental.pallas import tpu as pltpu

F32 = jnp.float32
BF16 = jnp.bfloat16
HI = lax.Precision.HIGHEST

D_MODEL = 1024
CHUNK = 64
EPS = 1e-6
HEAD_DIM = 64
N_HEADS = 8
A_PREV = 8
B_PREV = 2
MAX_REL = 256
D_INNER = 2048
SSM_HEADS = 32
SSM_GROUPS = 4
SSM_STATE = 128
SSM_CONV = 4
D_FF = 2816
FFN_CONV = 3
LANES = 128
SUBLANES = 8
VMEM_LIMIT = 56 * 1024 * 1024
SSD_L = 128

ADAM_LR = 0.001
ADAM_B1 = 0.9
ADAM_B2 = 0.999
ADAM_EPS = 1e-08
ADAM_WD = 0.01
ADAM_STEP = 10

MESH = pl.DeviceIdType.MESH


def _params(*sem):
    return pltpu.CompilerParams(dimension_semantics=sem, vmem_limit_bytes=VMEM_LIMIT)


def _pick(n, want):
    if n <= want:
        return n
    t = (want // LANES) * LANES
    while t >= LANES:
        if n % t == 0:
            return t
        t -= LANES
    return n


def matmul(a, b, *, mode, name, out_dtype=F32, residual=None, tm=512, tn=512, tk=512):
    if mode == "nn":
        (M, K), (K2, N) = a.shape, b.shape
    elif mode == "nt":
        (M, K), (N, K2) = a.shape, b.shape
    else:
        (K, M), (K2, N) = a.shape, b.shape
    assert K == K2, (a.shape, b.shape, mode)
    tm, tn, tk = _pick(M, tm), _pick(N, tn), _pick(K, tk)
    nk = K // tk
    dims = {"nn": (((1,), (0,)), ((), ())), "nt": (((1,), (1,)), ((), ())), "tn": (((0,), (0,)), ((), ()))}[mode]

    def body(*refs):
        if residual is None:
            a_ref, b_ref, o_ref, acc_ref = refs
            r_ref = None
        else:
            a_ref, b_ref, r_ref, o_ref, acc_ref = refs
        k = pl.program_id(2)

        @pl.when(k == 0)
        def _():
            acc_ref[...] = jnp.zeros_like(acc_ref)

        acc_ref[...] += lax.dot_general(a_ref[...].astype(BF16), b_ref[...].astype(BF16), dims,
                                        preferred_element_type=F32)

        @pl.when(k == nk - 1)
        def _():
            r = acc_ref[...]
            if r_ref is not None:
                r = r + r_ref[...].astype(F32)
            o_ref[...] = r.astype(o_ref.dtype)

    if mode == "nn":
        a_spec = pl.BlockSpec((tm, tk), lambda i, j, k: (i, k))
        b_spec = pl.BlockSpec((tk, tn), lambda i, j, k: (k, j))
    elif mode == "nt":
        a_spec = pl.BlockSpec((tm, tk), lambda i, j, k: (i, k))
        b_spec = pl.BlockSpec((tn, tk), lambda i, j, k: (j, k))
    else:
        a_spec = pl.BlockSpec((tk, tm), lambda i, j, k: (k, i))
        b_spec = pl.BlockSpec((tk, tn), lambda i, j, k: (k, j))
    o_spec = pl.BlockSpec((tm, tn), lambda i, j, k: (i, j))
    in_specs = [a_spec, b_spec] + ([o_spec] if residual is not None else [])
    args = (a, b) + ((residual,) if residual is not None else ())
    return pl.pallas_call(
        body, name=name, grid=(M // tm, N // tn, nk),
        in_specs=in_specs, out_specs=o_spec,
        out_shape=jax.ShapeDtypeStruct((M, N), out_dtype),
        scratch_shapes=[pltpu.VMEM((tm, tn), F32)],
        compiler_params=_params("parallel", "parallel", "arbitrary"),
    )(*args)


def rowwise(f, rows, params, outs, *, name, tm=256):
    S = rows[0].shape[0]
    tm = _row_tile(S, tm)
    nr, npar = len(rows), len(params)

    def body(*refs):
        vals = [r[...] for r in refs[:nr + npar]]
        res = f(*vals)
        for o_ref, r in zip(refs[nr + npar:], res):
            o_ref[...] = r.astype(o_ref.dtype)

    in_specs = [pl.BlockSpec((tm, r.shape[1]), lambda i: (i, 0)) for r in rows]
    in_specs += [pl.BlockSpec(p.shape, lambda i: (0, 0)) for p in params]
    out_specs = [pl.BlockSpec((tm, c), lambda i: (i, 0)) for c, _ in outs]
    out_shape = [jax.ShapeDtypeStruct((S, c), dt) for c, dt in outs]
    return pl.pallas_call(body, name=name, grid=(S // tm,), in_specs=in_specs, out_specs=out_specs,
                          out_shape=out_shape, compiler_params=_params("parallel"))(*rows, *params)


def rowwise_vjp(f, rows, params, cots, drow, dpar, *, name, tm=256):
    S = rows[0].shape[0]
    tm = _row_tile(S, tm)
    nr, npar, nc = len(rows), len(params), len(cots)

    def body(*refs):
        vals = [r[...] for r in refs[:nr + npar]]
        cvals = [r[...].astype(F32) for r in refs[nr + npar:nr + npar + nc]]
        o_refs = refs[nr + npar + nc:]
        want = [ri for ri, _ in drow] + [nr + pi for pi in dpar]

        def f_want(*d):
            full = list(vals)
            for k, v in zip(want, d):
                full[k] = v
            return f(*full)

        _, vjp = jax.vjp(f_want, *[vals[k] for k in want])
        grads = vjp(tuple(cvals))
        for o_ref, g in zip(o_refs[:len(drow)], grads):
            o_ref[...] = g.astype(o_ref.dtype)
        first = pl.program_id(0) == 0
        for o_ref, g in zip(o_refs[len(drow):], grads[len(drow):]):
            g = g.astype(F32)

            @pl.when(first)
            def _(o_ref=o_ref, g=g):
                o_ref[...] = g

            @pl.when(jnp.logical_not(first))
            def _(o_ref=o_ref, g=g):
                o_ref[...] += g

    in_specs = [pl.BlockSpec((tm, r.shape[1]), lambda i: (i, 0)) for r in rows]
    in_specs += [pl.BlockSpec(p.shape, lambda i: (0, 0)) for p in params]
    in_specs += [pl.BlockSpec((tm, c.shape[1]), lambda i: (i, 0)) for c in cots]
    out_specs = [pl.BlockSpec((tm, rows[ri].shape[1]), lambda i: (i, 0)) for ri, _ in drow]
    out_specs += [pl.BlockSpec(params[pi].shape, lambda i: (0, 0)) for pi in dpar]
    out_shape = [jax.ShapeDtypeStruct(rows[ri].shape, dt) for ri, dt in drow]
    out_shape += [jax.ShapeDtypeStruct(params[pi].shape, F32) for pi in dpar]
    return pl.pallas_call(body, name=name, grid=(S // tm,), in_specs=in_specs, out_specs=out_specs,
                          out_shape=out_shape, compiler_params=_params("arbitrary"))(*rows, *params, *cots)


HALO = SUBLANES


def dwconv_fwd(x, w, b, *, name, tm=256):
    S, C = x.shape
    K = w.shape[0]
    tm = min(tm, S)
    hb = tm // HALO

    def body(x_ref, halo_ref, w_ref, b_ref, y_ref, buf):
        i = pl.program_id(0)
        buf[0:HALO, :] = jnp.where(i == 0, 0.0, halo_ref[...])
        buf[HALO:HALO + tm, :] = x_ref[...]
        acc = jnp.broadcast_to(b_ref[...], (tm, C))
        for k in range(K):
            acc = acc + w_ref[k:k + 1, :] * buf[pl.ds(HALO - (K - 1) + k, tm), :]
        y_ref[...] = acc

    return pl.pallas_call(
        body, name=name, grid=(S // tm,),
        in_specs=[pl.BlockSpec((tm, C), lambda i: (i, 0)),
                  pl.BlockSpec((HALO, C), lambda i: (jnp.maximum(i * hb - 1, 0), 0)),
                  pl.BlockSpec((K, C), lambda i: (0, 0)),
                  pl.BlockSpec((1, C), lambda i: (0, 0))],
        out_specs=pl.BlockSpec((tm, C), lambda i: (i, 0)),
        out_shape=jax.ShapeDtypeStruct((S, C), F32),
        scratch_shapes=[pltpu.VMEM((HALO + tm, C), F32)],
        compiler_params=_params("parallel"),
    )(x, x, w, b)


def dwconv_bwd(x, dy, w, *, name, tm=256):
    S, C = x.shape
    K = w.shape[0]
    tm = min(tm, S)
    hb = tm // HALO
    n = S // tm

    def body(x_ref, xh_ref, dy_ref, dyh_ref, w_ref, dx_ref, dw_ref, db_ref, bx, bd):
        i = pl.program_id(0)
        bx[0:HALO, :] = jnp.where(i == 0, 0.0, xh_ref[...])
        bx[HALO:HALO + tm, :] = x_ref[...]
        dyv = dy_ref[...]
        bd[0:tm, :] = dyv
        bd[tm:tm + HALO, :] = jnp.where(i == n - 1, 0.0, dyh_ref[...])
        acc = jnp.zeros((tm, C), F32)
        for k in range(K):
            acc = acc + w_ref[k:k + 1, :] * bd[pl.ds((K - 1) - k, tm), :]
        dx_ref[...] = acc

        @pl.when(i == 0)
        def _():
            dw_ref[...] = jnp.zeros_like(dw_ref)
            db_ref[...] = jnp.zeros_like(db_ref)

        for k in range(K):
            dw_ref[k:k + 1, :] += jnp.sum(dyv * bx[pl.ds(HALO - (K - 1) + k, tm), :], axis=0, keepdims=True)
        db_ref[...] += jnp.sum(dyv, axis=0, keepdims=True)

    return pl.pallas_call(
        body, name=name, grid=(n,),
        in_specs=[pl.BlockSpec((tm, C), lambda i: (i, 0)),
                  pl.BlockSpec((HALO, C), lambda i: (jnp.maximum(i * hb - 1, 0), 0)),
                  pl.BlockSpec((tm, C), lambda i: (i, 0)),
                  pl.BlockSpec((HALO, C), lambda i: (jnp.minimum((i + 1) * hb, S // HALO - 1), 0)),
                  pl.BlockSpec((K, C), lambda i: (0, 0))],
        out_specs=[pl.BlockSpec((tm, C), lambda i: (i, 0)),
                   pl.BlockSpec((K, C), lambda i: (0, 0)),
                   pl.BlockSpec((1, C), lambda i: (0, 0))],
        out_shape=[jax.ShapeDtypeStruct((S, C), F32), jax.ShapeDtypeStruct((K, C), F32),
                   jax.ShapeDtypeStruct((1, C), F32)],
        scratch_shapes=[pltpu.VMEM((HALO + tm, C), F32), pltpu.VMEM((tm + HALO, C), F32)],
        compiler_params=_params("arbitrary"),
    )(x, x, dy, dy, w)


def _sigmoid(x):
    return 1.0 / (1.0 + jnp.exp(-x))


def _silu(x):
    return x * _sigmoid(x)


def f_rmsnorm(x, g):
    return (x * lax.rsqrt(jnp.mean(x * x, axis=-1, keepdims=True) + EPS) * g,)


def f_ffn_act(gc, up):
    return (_silu(gc) * up,)


def f_silu(x):
    return (_silu(x),)


def _group_norm(x, bd, width):
    ms = jnp.dot(x * x, bd, precision=HI, preferred_element_type=F32) * (1.0 / width)
    return x * lax.rsqrt(ms + EPS)


def f_qknorm(qkv, gqa, gka, gqb, gkb, bd512, bd128, fold, expand):
    dq = N_HEADS * HEAD_DIM
    qa, ka, va, qb = (qkv[:, i * dq:(i + 1) * dq] for i in range(4))
    kb = qkv[:, 4 * dq:4 * dq + LANES]
    vb = qkv[:, 4 * dq + LANES:4 * dq + 2 * LANES]
    tile8 = lambda g: jnp.dot(g, fold, precision=HI, preferred_element_type=F32)
    qa = _group_norm(qa, bd512, HEAD_DIM) * tile8(gqa)
    ka = _group_norm(ka, bd512, HEAD_DIM) * tile8(gka)
    qb = _group_norm(qb, bd512, HEAD_DIM) * tile8(gqb)
    kb = _group_norm(kb, bd128, HEAD_DIM) * tile8(gkb)[:, :LANES]
    kb = jnp.dot(kb, expand, precision=HI, preferred_element_type=F32)
    vb = jnp.dot(vb, expand, precision=HI, preferred_element_type=F32)
    return qa, ka, va, qb, kb, vb


def f_gate_norm(y, z, nw):
    v = y * _silu(z)
    gw = D_INNER // SSM_GROUPS
    parts = []
    for g in range(SSM_GROUPS):
        vg = v[:, g * gw:(g + 1) * gw]
        parts.append(vg * lax.rsqrt(jnp.mean(vg * vg, axis=-1, keepdims=True) + EPS))
    return (jnp.concatenate(parts, axis=-1) * nw,)


ATT_TQ = 256
_NT = (((1,), (1,)), ((), ()))
_TN = (((0,), (0,)), ((), ()))


def _attn_probs(qh, kb, bias, valid, snk):
    s = lax.dot_general(qh, kb, _NT, preferred_element_type=F32) * (HEAD_DIM ** -0.5) + bias
    s = jnp.where(valid, s, -jnp.inf)
    m = jnp.max(s, axis=1, keepdims=True)
    if snk is not None:
        m = jnp.maximum(m, snk)
    e = jnp.exp(s - m)
    den = jnp.sum(e, axis=1, keepdims=True)
    if snk is None:
        return e / den, None
    es = jnp.exp(snk - m)
    den = den + es
    return e / den, es / den


def attn_fwd(q, k, v, bias, sinks, *, n_prev, name):
    S = q.shape[0]
    band = (n_prev + 1) * CHUNK
    pad = n_prev * CHUNK
    tq = min(ATT_TQ, S)
    nj = tq // CHUNK
    has_sink = sinks is not None

    def body(*refs):
        if has_sink:
            q_ref, k_ref, v_ref, bias_ref, sink_ref, o_ref = refs
        else:
            q_ref, k_ref, v_ref, bias_ref, o_ref = refs
        i = pl.program_id(1)
        head0 = lax.broadcasted_iota(jnp.int32, (1, LANES), 1) < HEAD_DIM
        col = lax.broadcasted_iota(jnp.int32, (1, band), 1)
        for j in range(nj):
            c = i * nj + j
            start = pl.multiple_of(c * CHUNK, CHUNK)
            qp = q_ref[j * CHUNK:(j + 1) * CHUNK, :].astype(F32)
            kb = k_ref[pl.ds(start, band), :]
            vb = v_ref[pl.ds(start, band), :]
            valid = col >= (n_prev - c) * CHUNK
            outs = []
            for r in range(2):
                mh = head0 if r == 0 else jnp.logical_not(head0)
                qh = jnp.where(mh, qp, 0.0).astype(BF16)
                snk = sink_ref[0, r:r + 1, 0:1] if has_sink else None
                p, _ = _attn_probs(qh, kb, bias_ref[r], valid, snk)
                outs.append(jnp.dot(p.astype(BF16), vb, preferred_element_type=F32))
            o_ref[j * CHUNK:(j + 1) * CHUNK, :] = jnp.where(head0, outs[0], outs[1]).astype(o_ref.dtype)

    in_specs = [pl.BlockSpec((tq, LANES), lambda p, i: (i, p)),
                pl.BlockSpec((pad + S, LANES), lambda p, i: (0, p)),
                pl.BlockSpec((pad + S, LANES), lambda p, i: (0, p)),
                pl.BlockSpec((2, CHUNK, band), lambda p, i: (p, 0, 0))]
    args = [q, k, v, bias]
    if has_sink:
        in_specs.append(pl.BlockSpec((1, 2, LANES), lambda p, i: (p, 0, 0)))
        args.append(sinks)
    return pl.pallas_call(
        body, name=name, grid=(N_HEADS // 2, S // tq), in_specs=in_specs,
        out_specs=pl.BlockSpec((tq, LANES), lambda p, i: (i, p)),
        out_shape=jax.ShapeDtypeStruct((S, N_HEADS * HEAD_DIM), BF16),
        compiler_params=_params("parallel", "parallel"),
    )(*args)


def attn_bwd(q, k, v, do, bias, sinks, *, n_prev, name):
    S = q.shape[0]
    band = (n_prev + 1) * CHUNK
    pad = n_prev * CHUNK
    tq = min(ATT_TQ, S)
    nj = tq // CHUNK
    has_sink = sinks is not None
    scale = HEAD_DIM ** -0.5

    def body(*refs):
        if has_sink:
            q_ref, k_ref, v_ref, do_ref, bias_ref, sink_ref, dq_ref, dk_ref, dv_ref, db_ref, dsk_ref = refs
        else:
            q_ref, k_ref, v_ref, do_ref, bias_ref, dq_ref, dk_ref, dv_ref, db_ref = refs
        i = pl.program_id(1)

        @pl.when(i == 0)
        def _():
            dk_ref[...] = jnp.zeros_like(dk_ref)
            dv_ref[...] = jnp.zeros_like(dv_ref)
            db_ref[...] = jnp.zeros_like(db_ref)
            if has_sink:
                dsk_ref[...] = jnp.zeros_like(dsk_ref)

        head0 = lax.broadcasted_iota(jnp.int32, (1, LANES), 1) < HEAD_DIM
        col = lax.broadcasted_iota(jnp.int32, (1, band), 1)
        for j in range(nj):
            c = i * nj + j
            start = pl.multiple_of(c * CHUNK, CHUNK)
            qp = q_ref[j * CHUNK:(j + 1) * CHUNK, :].astype(F32)
            dop = do_ref[j * CHUNK:(j + 1) * CHUNK, :].astype(F32)
            kb = k_ref[pl.ds(start, band), :]
            vb = v_ref[pl.ds(start, band), :]
            valid = col >= (n_prev - c) * CHUNK
            dqs = []
            dk_acc = jnp.zeros((band, LANES), F32)
            dv_acc = jnp.zeros((band, LANES), F32)
            for r in range(2):
                mh = head0 if r == 0 else jnp.logical_not(head0)
                qh = jnp.where(mh, qp, 0.0).astype(BF16)
                doh = jnp.where(mh, dop, 0.0).astype(BF16)
                snk = sink_ref[0, r:r + 1, 0:1] if has_sink else None
                p, ps = _attn_probs(qh, kb, bias_ref[r], valid, snk)
                dp = lax.dot_general(doh, vb, _NT, preferred_element_type=F32)
                delta = jnp.sum(p * dp, axis=1, keepdims=True)
                ds = p * (dp - delta)
                db_ref[r] += ds
                if has_sink:
                    dsk = -jnp.sum(ps * delta, axis=0, keepdims=True)
                    dsk_ref[0, r:r + 1, :] += jnp.broadcast_to(dsk, (1, LANES))
                dsb = ds.astype(BF16)
                dqs.append(jnp.dot(dsb, kb, preferred_element_type=F32) * scale)
                dk_acc = dk_acc + lax.dot_general(dsb, qh, _TN, preferred_element_type=F32) * scale
                dv_acc = dv_acc + lax.dot_general(p.astype(BF16), doh, _TN, preferred_element_type=F32)
            dq_ref[j * CHUNK:(j + 1) * CHUNK, :] = jnp.where(head0, dqs[0], dqs[1])
            dk_ref[pl.ds(start, band), :] += dk_acc
            dv_ref[pl.ds(start, band), :] += dv_acc

    row_spec = pl.BlockSpec((tq, LANES), lambda p, i: (i, p))
    kv_spec = pl.BlockSpec((pad + S, LANES), lambda p, i: (0, p))
    bias_spec = pl.BlockSpec((2, CHUNK, band), lambda p, i: (p, 0, 0))
    sink_spec = pl.BlockSpec((1, 2, LANES), lambda p, i: (p, 0, 0))
    in_specs = [row_spec, kv_spec, kv_spec, row_spec, bias_spec]
    args = [q, k, v, do, bias]
    out_specs = [row_spec, kv_spec, kv_spec, bias_spec]
    W = N_HEADS * HEAD_DIM
    out_shape = [jax.ShapeDtypeStruct((S, W), F32), jax.ShapeDtypeStruct((pad + S, W), F32),
                 jax.ShapeDtypeStruct((pad + S, W), F32), jax.ShapeDtypeStruct((N_HEADS, CHUNK, band), F32)]
    if has_sink:
        in_specs.append(sink_spec)
        args.append(sinks)
        out_specs.append(sink_spec)
        out_shape.append(jax.ShapeDtypeStruct((N_HEADS // 2, 2, LANES), F32))
    return pl.pallas_call(
        body, name=name, grid=(N_HEADS // 2, S // tq), in_specs=in_specs, out_specs=out_specs,
        out_shape=out_shape, compiler_params=_params("arbitrary", "arbitrary"),
    )(*args)


HP = SSM_HEADS // 2
PAIRS_PER_GROUP = HP // SSM_GROUPS
HEADS_PER_GROUP = SSM_HEADS // SSM_GROUPS
GW = HEADS_PER_GROUP * 64


def _ssd_dt(dtraw, dtb, A, tril):
    lane = lax.broadcasted_iota(jnp.int32, (1, LANES), 1)
    u = dtraw + dtb
    eu = jnp.exp(-jnp.abs(u))
    w1 = 1.0 + eu
    l1p = jnp.where(w1 == 1.0, eu, jnp.log(w1) * eu / jnp.where(w1 == 1.0, 1.0, w1 - 1.0))
    dt = jnp.where(lane < SSM_HEADS, jnp.maximum(u, 0.0) + l1p, 0.0)
    acs = jnp.dot(tril, dt * A, precision=HI, preferred_element_type=F32)
    return u, dt, acs


def _ssd_head(acs, acsT, dt, h, trilb):
    L = acs.shape[0]
    ca = acs[:, h:h + 1]
    ra = acsT[h:h + 1, :]
    dec = jnp.exp(jnp.where(trilb, ca - ra, -jnp.inf))
    last = acs[L - 1:L, h:h + 1]
    return dec, dt[:, h:h + 1], jnp.exp(ca), jnp.exp(last - ca), last


def ssd_fwd(xbc, dtraw, dtb, A, dexp, *, name):
    S = xbc.shape[0]
    L = min(SSD_L, S)
    nc = S // L
    N = SSM_STATE

    def body(xs_ref, b_ref, c_ref, dtr_ref, dtb_ref, a_ref, d_ref, y_ref, st_out_ref, st_ref, xw_ref):
        c = pl.program_id(0)

        @pl.when(c == 0)
        def _():
            st_ref[...] = jnp.zeros_like(st_ref)

        st_out_ref[0] = st_ref[...]
        ri = lax.broadcasted_iota(jnp.int32, (L, L), 0)
        ci = lax.broadcasted_iota(jnp.int32, (L, L), 1)
        trilb = ri >= ci
        head0 = lax.broadcasted_iota(jnp.int32, (1, LANES), 1) < 64
        _, dt, acs = _ssd_dt(dtr_ref[...], dtb_ref[...], a_ref[...], trilb.astype(F32))
        acsT = acs.T
        lasts = []
        for g in range(SSM_GROUPS):
            Bg = b_ref[:, g * N:(g + 1) * N].astype(BF16)
            Cg = c_ref[:, g * N:(g + 1) * N].astype(BF16)
            CB = lax.dot_general(Cg, Bg, _NT, preferred_element_type=F32)
            Z = lax.dot_general(Cg, st_ref[g * GW:(g + 1) * GW, :].astype(BF16), _NT, preferred_element_type=F32)
            for q in range(PAIRS_PER_GROUP):
                hp = g * PAIRS_PER_GROUP + q
                sl = slice(hp * LANES, (hp + 1) * LANES)
                xs = xs_ref[:, sl]
                hd = [_ssd_head(acs, acsT, dt, 2 * hp + r, trilb) for r in range(2)]
                dtp = jnp.where(head0, hd[0][1], hd[1][1])
                eap = jnp.where(head0, hd[0][2], hd[1][2])
                wp = jnp.where(head0, hd[0][3] * hd[0][1], hd[1][3] * hd[1][1])
                xd = xs * dtp
                yi = jnp.dot((CB * hd[0][0]).astype(BF16), jnp.where(head0, xd, 0.0).astype(BF16), preferred_element_type=F32)
                yi = yi + jnp.dot((CB * hd[1][0]).astype(BF16), jnp.where(head0, 0.0, xd).astype(BF16), preferred_element_type=F32)
                y_ref[:, sl] = yi + Z[:, q * LANES:(q + 1) * LANES] * eap + d_ref[:, sl] * xs
                xw_ref[:, sl] = (xs * wp).astype(BF16)
                lasts += [hd[0][4], hd[1][4]]
        for g in range(SSM_GROUPS):
            Bg = b_ref[:, g * N:(g + 1) * N].astype(BF16)
            sn = lax.dot_general(xw_ref[:, g * GW:(g + 1) * GW], Bg, _TN, preferred_element_type=F32)
            for k in range(HEADS_PER_GROUP):
                h = g * HEADS_PER_GROUP + k
                rows = slice(h * 64, (h + 1) * 64)
                st_ref[rows, :] = st_ref[rows, :] * jnp.exp(lasts[h]) + sn[k * 64:(k + 1) * 64, :]

    return pl.pallas_call(
        body, name=name, grid=(nc,),
        in_specs=[pl.BlockSpec((L, D_INNER), lambda c: (c, 0)),
                  pl.BlockSpec((L, SSM_GROUPS * N), lambda c: (c, D_INNER // (SSM_GROUPS * N))),
                  pl.BlockSpec((L, SSM_GROUPS * N), lambda c: (c, D_INNER // (SSM_GROUPS * N) + 1)),
                  pl.BlockSpec((L, LANES), lambda c: (c, 0)),
                  pl.BlockSpec((1, LANES), lambda c: (0, 0)),
                  pl.BlockSpec((1, LANES), lambda c: (0, 0)),
                  pl.BlockSpec((1, D_INNER), lambda c: (0, 0))],
        out_specs=[pl.BlockSpec((L, D_INNER), lambda c: (c, 0)),
                   pl.BlockSpec((1, D_INNER, N), lambda c: (c, 0, 0))],
        out_shape=[jax.ShapeDtypeStruct((S, D_INNER), F32), jax.ShapeDtypeStruct((nc, D_INNER, N), F32)],
        scratch_shapes=[pltpu.VMEM((D_INNER, N), F32), pltpu.VMEM((L, D_INNER), BF16)],
        compiler_params=_params("arbitrary"),
    )(xbc, xbc, xbc, dtraw, dtb, A, dexp)


def ssd_bwd(xbc, dtraw, dtb, A, dexp, states, dy, *, name):
    S = xbc.shape[0]
    L = min(SSD_L, S)
    nc = S // L
    N = SSM_STATE

    def body(xs_ref, b_ref, c_ref, dtr_ref, dtb_ref, a_ref, d_ref, st_in_ref, dy_ref,
             dxs_ref, db_ref, dc_ref, ddtr_ref, da_ref, ddtb_ref, dd_ref, dst_ref, xw_ref, dz_ref):
        step = pl.program_id(0)

        @pl.when(step == 0)
        def _():
            dst_ref[...] = jnp.zeros_like(dst_ref)
            da_ref[...] = jnp.zeros_like(da_ref)
            ddtb_ref[...] = jnp.zeros_like(ddtb_ref)
            dd_ref[...] = jnp.zeros_like(dd_ref)

        ri = lax.broadcasted_iota(jnp.int32, (L, L), 0)
        ci = lax.broadcasted_iota(jnp.int32, (L, L), 1)
        trilb = ri >= ci
        lane = lax.broadcasted_iota(jnp.int32, (1, LANES), 1)
        sub = lax.broadcasted_iota(jnp.int32, (LANES, 1), 0)
        head0 = lane < 64
        A = a_ref[...]
        u, dt, acs = _ssd_dt(dtr_ref[...], dtb_ref[...], A, trilb.astype(F32))
        acsT = acs.T
        dacs = jnp.zeros((L, LANES), F32)
        dacsT = jnp.zeros((LANES, L), F32)
        ddt = jnp.zeros((L, LANES), F32)
        dlast = jnp.zeros((1, LANES), F32)
        lasts = []
        dcbs = []
        for g in range(SSM_GROUPS):
            Bg = b_ref[:, g * N:(g + 1) * N].astype(BF16)
            Cg = c_ref[:, g * N:(g + 1) * N].astype(BF16)
            stg = st_in_ref[0, g * GW:(g + 1) * GW, :]
            dstg = dst_ref[g * GW:(g + 1) * GW, :]
            CB = lax.dot_general(Cg, Bg, _NT, preferred_element_type=F32)
            Z = lax.dot_general(Cg, stg.astype(BF16), _NT, preferred_element_type=F32)
            U = lax.dot_general(Bg, dstg.astype(BF16), _NT, preferred_element_type=F32)
            dcb = jnp.zeros((L, L), F32)
            for q in range(PAIRS_PER_GROUP):
                hp = g * PAIRS_PER_GROUP + q
                sl = slice(hp * LANES, (hp + 1) * LANES)
                qs = slice(q * LANES, (q + 1) * LANES)
                xs = xs_ref[:, sl]
                dyp = dy_ref[:, sl]
                hd = [_ssd_head(acs, acsT, dt, 2 * hp + r, trilb) for r in range(2)]
                dtp = jnp.where(head0, hd[0][1], hd[1][1])
                eap = jnp.where(head0, hd[0][2], hd[1][2])
                wcol = [hd[r][3] * hd[r][1] for r in range(2)]
                wp = jnp.where(head0, wcol[0], wcol[1])
                xd = xs * dtp
                dxd = jnp.zeros((L, LANES), F32)
                for r in range(2):
                    h = 2 * hp + r
                    mh = head0 if r == 0 else jnp.logical_not(head0)
                    dyh = jnp.where(mh, dyp, 0.0).astype(BF16)
                    xdh = jnp.where(mh, xd, 0.0).astype(BF16)
                    G = lax.dot_general(dyh, xdh, _NT, preferred_element_type=F32)
                    dec = hd[r][0]
                    M = CB * dec
                    dcb = dcb + G * dec
                    dseg = G * M
                    dacs = dacs + jnp.where(lane == h, jnp.sum(dseg, axis=1, keepdims=True), 0.0)
                    dacsT = dacsT - jnp.where(sub == h, jnp.sum(dseg, axis=0, keepdims=True), 0.0)
                    dxd = dxd + lax.dot_general(M.astype(BF16), dyh, _TN, preferred_element_type=F32)
                Up = U[:, qs]
                dyz = dyp * Z[:, qs] * eap
                t1 = dxd * xs
                t2 = Up * xs
                for r in range(2):
                    h = 2 * hp + r
                    mh = head0 if r == 0 else jnp.logical_not(head0)
                    rs = lambda t: jnp.sum(jnp.where(mh, t, 0.0), axis=1, keepdims=True)
                    dw = rs(t2)
                    dww = dw * wcol[r]
                    dacs = dacs + jnp.where(lane == h, rs(dyz) - dww, 0.0)
                    ddt = ddt + jnp.where(lane == h, rs(t1) + dw * hd[r][3], 0.0)
                    rows = slice((h % HEADS_PER_GROUP) * 64, (h % HEADS_PER_GROUP + 1) * 64)
                    dl = jnp.sum(dww, axis=0, keepdims=True) + jnp.sum(dstg[rows, :] * stg[rows, :], keepdims=True) * jnp.exp(hd[r][4])
                    dlast = dlast + jnp.where(lane == h, dl, 0.0)
                    lasts.append(hd[r][4])
                dz_ref[:, sl] = (dyp * eap).astype(BF16)
                xw_ref[:, sl] = (xs * wp).astype(BF16)
                dxs_ref[:, sl] = dxd * dtp + d_ref[:, sl] * dyp + Up * wp
                dd_ref[:, sl] += jnp.sum(dyp * xs, axis=0, keepdims=True)
            dcbs.append(dcb)
        for g in range(SSM_GROUPS):
            Bg = b_ref[:, g * N:(g + 1) * N].astype(BF16)
            Cg = c_ref[:, g * N:(g + 1) * N].astype(BF16)
            gs = slice(g * GW, (g + 1) * GW)
            stb = st_in_ref[0, gs, :].astype(BF16)
            dstb = dst_ref[gs, :].astype(BF16)
            dcbb = dcbs[g].astype(BF16)
            dzg = dz_ref[:, gs]
            dc_ref[:, g * N:(g + 1) * N] = (jnp.dot(dzg, stb, preferred_element_type=F32)
                                            + jnp.dot(dcbb, Bg, preferred_element_type=F32))
            db_ref[:, g * N:(g + 1) * N] = (jnp.dot(xw_ref[:, gs], dstb, preferred_element_type=F32)
                                            + lax.dot_general(dcbb, Cg, _TN, preferred_element_type=F32))
            dsn = lax.dot_general(dzg, Cg, _TN, preferred_element_type=F32)
            for k in range(HEADS_PER_GROUP):
                h = g * HEADS_PER_GROUP + k
                rows = slice(h * 64, (h + 1) * 64)
                dst_ref[rows, :] = dst_ref[rows, :] * jnp.exp(lasts[h]) + dsn[k * 64:(k + 1) * 64, :]
        rowi = lax.broadcasted_iota(jnp.int32, (L, 1), 0)
        dacs = dacs + dacsT.T + jnp.where(rowi == L - 1, dlast, 0.0)
        da = jnp.dot((ci >= ri).astype(F32), dacs, precision=HI, preferred_element_type=F32)
        ddt = ddt + da * A
        da_ref[...] += jnp.sum(da * dt, axis=0, keepdims=True)
        ddtr = jnp.where(lane < SSM_HEADS, ddt * _sigmoid(u), 0.0)
        ddtr_ref[...] = ddtr
        ddtb_ref[...] += jnp.sum(ddtr, axis=0, keepdims=True)

    rev = lambda c: nc - 1 - c
    gn = SSM_GROUPS * N
    return pl.pallas_call(
        body, name=name, grid=(nc,),
        in_specs=[pl.BlockSpec((L, D_INNER), lambda c: (rev(c), 0)),
                  pl.BlockSpec((L, gn), lambda c: (rev(c), D_INNER // gn)),
                  pl.BlockSpec((L, gn), lambda c: (rev(c), D_INNER // gn + 1)),
                  pl.BlockSpec((L, LANES), lambda c: (rev(c), 0)),
                  pl.BlockSpec((1, LANES), lambda c: (0, 0)),
                  pl.BlockSpec((1, LANES), lambda c: (0, 0)),
                  pl.BlockSpec((1, D_INNER), lambda c: (0, 0)),
                  pl.BlockSpec((1, D_INNER, N), lambda c: (rev(c), 0, 0)),
                  pl.BlockSpec((L, D_INNER), lambda c: (rev(c), 0))],
        out_specs=[pl.BlockSpec((L, D_INNER), lambda c: (rev(c), 0)),
                   pl.BlockSpec((L, gn), lambda c: (rev(c), 0)),
                   pl.BlockSpec((L, gn), lambda c: (rev(c), 0)),
                   pl.BlockSpec((L, LANES), lambda c: (rev(c), 0)),
                   pl.BlockSpec((1, LANES), lambda c: (0, 0)),
                   pl.BlockSpec((1, LANES), lambda c: (0, 0)),
                   pl.BlockSpec((1, D_INNER), lambda c: (0, 0))],
        out_shape=[jax.ShapeDtypeStruct((S, D_INNER), F32), jax.ShapeDtypeStruct((S, gn), F32),
                   jax.ShapeDtypeStruct((S, gn), F32), jax.ShapeDtypeStruct((S, LANES), F32),
                   jax.ShapeDtypeStruct((1, LANES), F32), jax.ShapeDtypeStruct((1, LANES), F32),
                   jax.ShapeDtypeStruct((1, D_INNER), F32)],
        scratch_shapes=[pltpu.VMEM((D_INNER, N), F32), pltpu.VMEM((L, D_INNER), BF16), pltpu.VMEM((L, D_INNER), BF16)],
        compiler_params=_params("arbitrary"),
    )(xbc, xbc, xbc, dtraw, dtb, A, dexp, states, dy)


BAND_A = (A_PREV + 1) * CHUNK
REL_W = 640


def _relpos_onehot(q):
    u = lax.broadcasted_iota(jnp.int32, (BAND_A, 1), 0)
    idx = jnp.clip(q - u + A_PREV * CHUNK, -MAX_REL, MAX_REL) + MAX_REL
    r = lax.broadcasted_iota(jnp.int32, (1, REL_W), 1)
    return (r == idx).astype(F32)


def relpos_bias(table_pad, *, name):
    def body(t_ref, o_ref):
        oh = _relpos_onehot(pl.program_id(0))
        o_ref[0] = lax.dot_general(t_ref[...], oh, _NT, precision=HI, preferred_element_type=F32)

    return pl.pallas_call(
        body, name=name, grid=(CHUNK,),
        in_specs=[pl.BlockSpec((N_HEADS, REL_W), lambda q: (0, 0))],
        out_specs=pl.BlockSpec((1, N_HEADS, BAND_A), lambda q: (q, 0, 0)),
        out_shape=jax.ShapeDtypeStruct((CHUNK, N_HEADS, BAND_A), F32),
        compiler_params=_params("parallel"),
    )(table_pad)


def relpos_grad(dbias_t, *, name):
    def body(d_ref, o_ref):
        q = pl.program_id(0)

        @pl.when(q == 0)
        def _():
            o_ref[...] = jnp.zeros_like(o_ref)

        o_ref[...] += jnp.dot(d_ref[0], _relpos_onehot(q), precision=HI, preferred_element_type=F32)

    return pl.pallas_call(
        body, name=name, grid=(CHUNK,),
        in_specs=[pl.BlockSpec((1, N_HEADS, BAND_A), lambda q: (q, 0, 0))],
        out_specs=pl.BlockSpec((N_HEADS, REL_W), lambda q: (0, 0)),
        out_shape=jax.ShapeDtypeStruct((N_HEADS, REL_W), F32),
        compiler_params=_params("arbitrary"),
    )(dbias_t)


def loss_head(y, t, *, name, tm=256):
    S, D = y.shape
    tm = min(tm, S)

    def body(y_ref, t_ref, dy_ref, l_ref):
        e = y_ref[...] - t_ref[...]
        dy_ref[...] = e * (1.0 / D)

        @pl.when(pl.program_id(0) == 0)
        def _():
            l_ref[...] = jnp.zeros_like(l_ref)

        part = jnp.sum(jnp.sum(e * e, axis=1, keepdims=True), axis=0, keepdims=True) * (0.5 / D)
        l_ref[...] += jnp.broadcast_to(part, l_ref.shape)

    return pl.pallas_call(
        body, name=name, grid=(S // tm,),
        in_specs=[pl.BlockSpec((tm, D), lambda i: (i, 0))] * 2,
        out_specs=[pl.BlockSpec((tm, D), lambda i: (i, 0)), pl.BlockSpec((1, LANES), lambda i: (0, 0))],
        out_shape=[jax.ShapeDtypeStruct((S, D), F32), jax.ShapeDtypeStruct((1, LANES), F32)],
        compiler_params=_params("arbitrary"),
    )(y, t)


def f_adamw(w, g, m, v):
    m = ADAM_B1 * m + (1.0 - ADAM_B1) * g
    v = ADAM_B2 * v + (1.0 - ADAM_B2) * (g * g)
    m_hat = m / (1.0 - ADAM_B1 ** ADAM_STEP)
    v_hat = v / (1.0 - ADAM_B2 ** ADAM_STEP)
    delta = -ADAM_LR * (m_hat / (jnp.sqrt(v_hat) + ADAM_EPS) + ADAM_WD * w)
    return delta, m, v


def f_norm_id(x, g):
    return f_rmsnorm(x, g)[0], x


ANY = pl.BlockSpec(memory_space=pl.ANY)


def _pos():
    return lax.axis_index("x"), lax.axis_index("y"), lax.axis_index("c")


def _other_chips(x, y):
    return [(1 - x, y), (x, 1 - y), (1 - x, 1 - y)]


def gather_chips(shards, *, name):
    n = len(shards)

    def body(*refs):
        ins, outs = refs[:n], refs[n:2 * n]
        send, recv, fsend, frecv, loc = refs[2 * n:]
        x, y, c = _pos()
        me = 2 * x + y
        sib = (x, y, 1 - c)
        chips = _other_chips(x, y)
        local = []
        for i in range(n):
            cp = pltpu.make_async_copy(ins[i], outs[i].at[me], loc.at[i])
            cp.start()
            local.append(cp)
        first = []
        for i in range(n):
            for j, (px, py) in enumerate(chips):
                cp = pltpu.make_async_remote_copy(ins[i].at[c], outs[i].at[me, c], send.at[3 * i + j], recv.at[3 * i + j],
                                                  device_id=(px, py, c), device_id_type=MESH)
                cp.start()
                first.append(cp)
        passed = []
        for i in range(n):
            for j, (px, py) in enumerate(chips):
                got = outs[i].at[2 * px + py, c]
                pltpu.make_async_remote_copy(ins[i].at[c], got, send.at[3 * i + j], recv.at[3 * i + j],
                                             device_id=(px, py, c), device_id_type=MESH).wait_recv()
                cp = pltpu.make_async_remote_copy(got, got, fsend.at[3 * i + j], frecv.at[3 * i + j],
                                                  device_id=sib, device_id_type=MESH)
                cp.start()
                passed.append(cp)
        for i in range(n):
            for j, (px, py) in enumerate(chips):
                theirs = outs[i].at[2 * px + py, 1 - c]
                pltpu.make_async_remote_copy(theirs, theirs, fsend.at[3 * i + j], frecv.at[3 * i + j],
                                             device_id=sib, device_id_type=MESH).wait_recv()
        for cp in first + passed:
            cp.wait_send()
        for cp in local:
            cp.wait()

    return pl.pallas_call(
        body, name=name, in_specs=[ANY] * n, out_specs=[ANY] * n,
        out_shape=[jax.ShapeDtypeStruct((4,) + s.shape, s.dtype) for s in shards],
        scratch_shapes=[pltpu.SemaphoreType.DMA((3 * n,)), pltpu.SemaphoreType.DMA((3 * n,)),
                        pltpu.SemaphoreType.DMA((3 * n,)), pltpu.SemaphoreType.DMA((3 * n,)),
                        pltpu.SemaphoreType.DMA((n,))],
        compiler_params=pltpu.CompilerParams(has_side_effects=True),
    )(*shards)


def pair_swap_halves(gs, *, name):
    n = len(gs)

    def body(*refs):
        ins, outs = refs[:n], refs[n:2 * n]
        send, recv = refs[2 * n:]
        x, y, c = _pos()
        cps = []
        for i in range(n):
            cp = pltpu.make_async_remote_copy(ins[i].at[1 - c], outs[i], send.at[i], recv.at[i],
                                              device_id=(x, y, 1 - c), device_id_type=MESH)
            cp.start()
            cps.append(cp)
        for cp in cps:
            cp.wait()

    return pl.pallas_call(
        body, name=name, in_specs=[ANY] * n, out_specs=[ANY] * n,
        out_shape=[jax.ShapeDtypeStruct(g.shape[1:], g.dtype) for g in gs],
        scratch_shapes=[pltpu.SemaphoreType.DMA((n,)), pltpu.SemaphoreType.DMA((n,))],
        compiler_params=pltpu.CompilerParams(has_side_effects=True),
    )(*gs)


def scatter_chips(ps, *, name):
    n = len(ps)

    def body(*refs):
        ins, outs = refs[:n], refs[n:2 * n]
        send, recv = refs[2 * n:]
        x, y, c = _pos()
        cps = []
        for i in range(n):
            for j, (px, py) in enumerate(_other_chips(x, y)):
                cp = pltpu.make_async_remote_copy(ins[i].at[2 * px + py], outs[i].at[j], send.at[3 * i + j], recv.at[3 * i + j],
                                                  device_id=(px, py, c), device_id_type=MESH)
                cp.start()
                cps.append(cp)
        for cp in cps:
            cp.wait()

    return pl.pallas_call(
        body, name=name, in_specs=[ANY] * n, out_specs=[ANY] * n,
        out_shape=[jax.ShapeDtypeStruct((3,) + p.shape[1:], p.dtype) for p in ps],
        scratch_shapes=[pltpu.SemaphoreType.DMA((3 * n,)), pltpu.SemaphoreType.DMA((3 * n,))],
        compiler_params=pltpu.CompilerParams(has_side_effects=True),
    )(*ps)


def pair_share(hs, *, name):
    n = len(hs)

    def body(*refs):
        ins, outs = refs[:n], refs[n:2 * n]
        send, recv, loc = refs[2 * n:]
        x, y, c = _pos()
        cps, local = [], []
        for i in range(n):
            lc = pltpu.make_async_copy(ins[i], outs[i].at[c], loc.at[i])
            lc.start()
            local.append(lc)
            cp = pltpu.make_async_remote_copy(ins[i], outs[i].at[c], send.at[i], recv.at[i],
                                              device_id=(x, y, 1 - c), device_id_type=MESH)
            cp.start()
            cps.append(cp)
        for i in range(n):
            cps[i].wait_send()
            pltpu.make_async_remote_copy(ins[i], outs[i].at[1 - c], send.at[i], recv.at[i],
                                         device_id=(x, y, 1 - c), device_id_type=MESH).wait_recv()
        for lc in local:
            lc.wait()

    return pl.pallas_call(
        body, name=name, in_specs=[ANY] * n, out_specs=[ANY] * n,
        out_shape=[jax.ShapeDtypeStruct((2,) + h.shape, h.dtype) for h in hs],
        scratch_shapes=[pltpu.SemaphoreType.DMA((n,)), pltpu.SemaphoreType.DMA((n,)), pltpu.SemaphoreType.DMA((n,))],
        compiler_params=pltpu.CompilerParams(has_side_effects=True),
    )(*hs)


def gather_all(buf, *, name):
    def body(in_ref, out_ref, send, recv, loc):
        x, y, c = _pos()
        lid = 4 * x + 2 * y + c
        lc = pltpu.make_async_copy(in_ref, out_ref.at[lid], loc.at[0])
        lc.start()
        cps = []
        for k in range(1, 8):
            px = 1 - x if k & 4 else x
            py = 1 - y if k & 2 else y
            pc = 1 - c if k & 1 else c
            cp = pltpu.make_async_remote_copy(in_ref, out_ref.at[lid], send.at[k - 1], recv.at[k - 1],
                                              device_id=(px, py, pc), device_id_type=MESH)
            cp.start()
            cps.append((cp, 4 * px + 2 * py + pc, (px, py, pc)))
        for k, (cp, plid, peer) in enumerate(cps):
            cp.wait_send()
            pltpu.make_async_remote_copy(in_ref, out_ref.at[plid], send.at[k], recv.at[k],
                                         device_id=peer, device_id_type=MESH).wait_recv()
        lc.wait()

    return pl.pallas_call(
        body, name=name, in_specs=[ANY], out_specs=ANY,
        out_shape=jax.ShapeDtypeStruct((8,) + buf.shape, buf.dtype),
        scratch_shapes=[pltpu.SemaphoreType.DMA((7,)), pltpu.SemaphoreType.DMA((7,)), pltpu.SemaphoreType.DMA((1,))],
        compiler_params=pltpu.CompilerParams(has_side_effects=True),
    )(buf)


def sum_slots(a, *, name):
    n = a.shape[0]

    def body(a_ref, o_ref):
        acc = a_ref[0]
        for k in range(1, n):
            acc = acc + a_ref[k]
        o_ref[...] = acc

    return pl.pallas_call(body, name=name, out_shape=jax.ShapeDtypeStruct(a.shape[1:], a.dtype),
                          compiler_params=pltpu.CompilerParams(vmem_limit_bytes=VMEM_LIMIT))(a)


def _row_tile(r, want, mult=16):
    t = (min(want, r) // mult) * mult
    while t >= mult:
        if r % t == 0:
            return t
        t -= mult
    return r


def pair_add(g, r1, csel, *, name):
    _, _, r, C = g.shape
    tr = _row_tile(r, 256)

    def body(g_ref, r_ref, c_ref, p32_ref, pb_ref):
        south = c_ref[0:1, 0:1] == 0.0
        p = jnp.where(south, g_ref[0, 0], g_ref[1, 0]) + r_ref[0]
        p32_ref[0] = p
        pb_ref[0] = p.astype(BF16)

    return pl.pallas_call(
        body, name=name, grid=(4, r // tr),
        in_specs=[pl.BlockSpec((2, 1, tr, C), lambda j, t: (0, j, t, 0)), pl.BlockSpec((1, tr, C), lambda j, t: (j, t, 0)),
                  pl.BlockSpec((1, LANES), lambda j, t: (0, 0))],
        out_specs=[pl.BlockSpec((1, tr, C), lambda j, t: (j, t, 0))] * 2,
        out_shape=[jax.ShapeDtypeStruct((4, r, C), F32), jax.ShapeDtypeStruct((4, r, C), BF16)],
        compiler_params=_params("parallel", "parallel"),
    )(g, r1, csel)


def chip_add(p32, r3, msel, *, name):
    _, r, C = p32.shape
    tr = _row_tile(r, 128)

    def body(p_ref, r_ref, m_ref, o_ref):
        me = m_ref[0:1, 0:1]
        acc = jnp.where(me == 0.0, p_ref[0], jnp.where(me == 1.0, p_ref[1], jnp.where(me == 2.0, p_ref[2], p_ref[3])))
        for j in range(3):
            acc = acc + r_ref[j].astype(F32)
        o_ref[...] = acc

    return pl.pallas_call(
        body, name=name, grid=(r // tr,),
        in_specs=[pl.BlockSpec((4, tr, C), lambda t: (0, t, 0)), pl.BlockSpec((3, tr, C), lambda t: (0, t, 0)),
                  pl.BlockSpec((1, LANES), lambda t: (0, 0))],
        out_specs=pl.BlockSpec((tr, C), lambda t: (t, 0)),
        out_shape=jax.ShapeDtypeStruct((r, C), F32),
        compiler_params=_params("parallel"),
    )(p32, r3, msel)


def _consts():
    i512 = np.arange(N_HEADS * HEAD_DIM)
    i128 = np.arange(LANES)
    bd512 = (i512[:, None] // HEAD_DIM == i512[None, :] // HEAD_DIM).astype(np.float32)
    bd128 = (i128[:, None] // HEAD_DIM == i128[None, :] // HEAD_DIM).astype(np.float32)
    fold = (np.arange(HEAD_DIM)[:, None] == (i512[None, :] % HEAD_DIM)).astype(np.float32)
    grp = N_HEADS // 2 * HEAD_DIM
    expand = ((i128[:, None] // HEAD_DIM == i512[None, :] // grp)
              & (i128[:, None] % HEAD_DIM == i512[None, :] % HEAD_DIM)).astype(np.float32)
    band = (B_PREV + 1) * CHUNK
    rel = np.arange(CHUNK)[:, None] - (np.arange(band)[None, :] - B_PREV * CHUNK)
    slopes = 2.0 ** (-8.0 * np.arange(1, N_HEADS + 1, dtype=np.float32) / N_HEADS)
    bias_b = (-slopes[:, None, None] * np.abs(rel).astype(np.float32)[None]).astype(np.float32)
    return [jnp.asarray(a) for a in (bd512, bd128, fold, expand)], jnp.asarray(bias_b)


def _ffn_fwd(xin, l, W, P):
    g = P["norm_ffn"][l:l + 1]
    (h,) = rowwise(f_rmsnorm, [xin], [g], [(D_MODEL, BF16)], name=f"ffn{l}_norm")
    Wi = W["ffn_in"][l]
    gate = matmul(h, Wi[:, :D_FF], mode="nn", name=f"ffn{l}_gate")
    up = matmul(h, Wi[:, D_FF:], mode="nn", name=f"ffn{l}_up")
    gc = dwconv_fwd(gate, P["ffn_conv_w"][l], P["ffn_conv_b"][l:l + 1], name=f"ffn{l}_conv")
    (act,) = rowwise(f_ffn_act, [gc, up], [], [(D_FF, BF16)], name=f"ffn{l}_act")
    xout = matmul(act, W["ffn_out"][l], mode="nn", name=f"ffn{l}_out", residual=xin)
    return xout, (xin, h, gate, gc, up, act)


def _ffn_bwd(dxout, l, saved, W, P):
    xin, h, gate, gc, up, act = saved
    g = P["norm_ffn"][l:l + 1]
    Wi = W["ffn_in"][l]
    dact = matmul(dxout, W["ffn_out"][l], mode="nt", name=f"ffn{l}_dact")
    dWo = matmul(act, dxout, mode="tn", name=f"ffn{l}_dwout")
    dgc, dup = rowwise_vjp(f_ffn_act, [gc, up], [], [dact], [(0, F32), (1, BF16)], [], name=f"ffn{l}_dact_fn")
    dgate, dcw, dcb = dwconv_bwd(gate, dgc, P["ffn_conv_w"][l], name=f"ffn{l}_dconv")
    dh = matmul(dgate, Wi[:, :D_FF], mode="nt", name=f"ffn{l}_dh_gate")
    dh = matmul(dup, Wi[:, D_FF:], mode="nt", name=f"ffn{l}_dh_up", residual=dh)
    dWi = jnp.concatenate([matmul(h, dgate, mode="tn", name=f"ffn{l}_dw_gate"),
                           matmul(h, dup, mode="tn", name=f"ffn{l}_dw_up")], axis=1)
    dxin, dg = rowwise_vjp(f_norm_id, [xin], [g], [dh, dxout], [(0, F32)], [0], name=f"ffn{l}_dnorm")
    return dxin, dWi, dWo, dg, dcw, dcb


def local_step(x, tgt, W, P):
    qk_consts, bias_b = _consts()
    pad_rows = lambda t, n: jnp.pad(t, ((n * CHUNK, 0), (0, 0)))
    DQ = N_HEADS * HEAD_DIM

    g_mix0 = P["norm_mix"][0:1]
    (h0,) = rowwise(f_rmsnorm, [x], [g_mix0], [(D_MODEL, BF16)], name="attn_norm")
    qkv = matmul(h0, W["attn_in"], mode="nn", name="attn_qkv")
    qk_par = [P["q_norm_a"], P["k_norm_a"], P["q_norm_b"], P["k_norm_b"]] + qk_consts
    qa, ka, va, qb, kb, vb = rowwise(f_qknorm, [qkv], qk_par, [(DQ, BF16)] * 6, name="attn_qknorm")
    ka, va, kb, vb = pad_rows(ka, A_PREV), pad_rows(va, A_PREV), pad_rows(kb, B_PREV), pad_rows(vb, B_PREV)
    table = jnp.pad(P["relpos_table"], ((0, 0), (0, REL_W - (2 * MAX_REL + 1))))
    bias_a = jnp.transpose(relpos_bias(table, name="relpos_bias"), (1, 0, 2))
    sinks = jnp.broadcast_to(P["sinks"].reshape(N_HEADS // 2, 2, 1), (N_HEADS // 2, 2, LANES))
    oa = attn_fwd(qa, ka, va, bias_a, None, n_prev=A_PREV, name="attn_a")
    ob = attn_fwd(qb, kb, vb, bias_b, sinks, n_prev=B_PREV, name="attn_b")
    Wao = W["attn_out"]
    x1 = matmul(oa, Wao[:DQ], mode="nn", name="attn_out_a", residual=x)
    x1 = matmul(ob, Wao[DQ:], mode="nn", name="attn_out_b", residual=x1)
    x2, ffn0 = _ffn_fwd(x1, 0, W, P)

    g_mix1 = P["norm_mix"][1:2]
    (h2,) = rowwise(f_rmsnorm, [x2], [g_mix1], [(D_MODEL, BF16)], name="ssm_norm_in")
    Ws = W["ssm_in"]
    CC = D_INNER + 2 * SSM_GROUPS * SSM_STATE
    Wz, Wx = Ws[:, :D_INNER], Ws[:, D_INNER:D_INNER + CC]
    Wdt = jnp.pad(Ws[:, D_INNER + CC:], ((0, 0), (0, LANES - SSM_HEADS)))
    z = matmul(h2, Wz, mode="nn", name="ssm_z")
    xr = matmul(h2, Wx, mode="nn", name="ssm_xbc")
    dtraw = matmul(h2, Wdt, mode="nn", name="ssm_dt")
    xc = dwconv_fwd(xr, P["ssm_conv_w"], P["ssm_conv_b"], name="ssm_conv")
    (xbc,) = rowwise(f_silu, [xc], [], [(CC, F32)], name="ssm_silu")
    pad32 = lambda v: jnp.pad(v, ((0, 0), (0, LANES - SSM_HEADS)))
    A = pad32(-jnp.exp(P["ssm_a_log"]))
    dtb = pad32(P["ssm_dt_bias"])
    dexp = jnp.repeat(P["ssm_d"], D_INNER // SSM_HEADS, axis=1)
    y, states = ssd_fwd(xbc, dtraw, dtb, A, dexp, name="ssd_fwd")
    (y2,) = rowwise(f_gate_norm, [y, z], [P["ssm_norm"]], [(D_INNER, BF16)], name="ssm_gate_norm")
    x3 = matmul(y2, W["ssm_out"], mode="nn", name="ssm_out", residual=x2)
    x4, ffn1 = _ffn_fwd(x3, 1, W, P)

    dx4, lpart = loss_head(x4, tgt, name="loss_head")

    dx3, dWfi1, dWfo1, dgf1, dfcw1, dfcb1 = _ffn_bwd(dx4, 1, ffn1, W, P)
    dy2 = matmul(dx3, W["ssm_out"], mode="nt", name="ssm_dy")
    dWso = matmul(y2, dx3, mode="tn", name="ssm_dwout")
    dy, dz, dnw = rowwise_vjp(f_gate_norm, [y, z], [P["ssm_norm"]], [dy2], [(0, F32), (1, F32)], [0],
                              name="ssm_dgate_norm")
    dxs, dB, dC, ddtraw, dA, ddtb, dDl = ssd_bwd(xbc, dtraw, dtb, A, dexp, states, dy, name="ssd_bwd")
    dxbc = jnp.concatenate([dxs, dB, dC], axis=1)
    (dxc,) = rowwise_vjp(f_silu, [xc], [], [dxbc], [(0, F32)], [], name="ssm_dsilu")
    dxr, dscw, dscb = dwconv_bwd(xr, dxc, P["ssm_conv_w"], name="ssm_dconv")
    dh2 = matmul(dz, Wz, mode="nt", name="ssm_dh_z")
    dh2 = matmul(dxr, Wx, mode="nt", name="ssm_dh_x", residual=dh2)
    dh2 = matmul(ddtraw, Wdt, mode="nt", name="ssm_dh_dt", residual=dh2)
    dWs = jnp.concatenate([matmul(h2, dz, mode="tn", name="ssm_dw_z"),
                           matmul(h2, dxr, mode="tn", name="ssm_dw_x"),
                           matmul(h2, ddtraw, mode="tn", name="ssm_dw_dt")[:, :SSM_HEADS]], axis=1)
    dx2, dgm1 = rowwise_vjp(f_norm_id, [x2], [g_mix1], [dh2, dx3], [(0, F32)], [0], name="ssm_dnorm_in")

    dx1, dWfi0, dWfo0, dgf0, dfcw0, dfcb0 = _ffn_bwd(dx2, 0, ffn0, W, P)
    doa = matmul(dx1, Wao[:DQ], mode="nt", name="attn_do_a", out_dtype=BF16)
    dob = matmul(dx1, Wao[DQ:], mode="nt", name="attn_do_b", out_dtype=BF16)
    dWao = jnp.concatenate([matmul(oa, dx1, mode="tn", name="attn_dwout_a"),
                            matmul(ob, dx1, mode="tn", name="attn_dwout_b")], axis=0)
    dqa, dka, dva, dbias_a = attn_bwd(qa, ka, va, doa, bias_a, None, n_prev=A_PREV, name="attn_a_bwd")
    dqb, dkb, dvb, _, dsk = attn_bwd(qb, kb, vb, dob, bias_b, sinks, n_prev=B_PREV, name="attn_b_bwd")
    pa, pb = A_PREV * CHUNK, B_PREV * CHUNK
    cots = [dqa, dka[pa:], dva[pa:], dqb, dkb[pb:], dvb[pb:]]
    dqkv, dgqa, dgka, dgqb, dgkb = rowwise_vjp(f_qknorm, [qkv], qk_par, cots, [(0, BF16)], [0, 1, 2, 3],
                                               name="attn_dqknorm")
    dh0 = matmul(dqkv, W["attn_in"], mode="nt", name="attn_dh")
    dWai = matmul(h0, dqkv, mode="tn", name="attn_dwin")
    dx, dgm0 = rowwise_vjp(f_norm_id, [x], [g_mix0], [dh0, dx1], [(0, F32)], [0], name="attn_dnorm")
    dtable = relpos_grad(jnp.transpose(dbias_a, (1, 0, 2)), name="relpos_grad")[:, :2 * MAX_REL + 1]

    gW = {"attn_in": dWai, "attn_out": dWao, "ssm_in": dWs, "ssm_out": dWso,
          "ffn_in": [dWfi0, dWfi1], "ffn_out": [dWfo0, dWfo1]}
    gP = {"norm_mix": jnp.concatenate([dgm0, dgm1], axis=0),
          "norm_ffn": jnp.concatenate([dgf0, dgf1], axis=0),
          "relpos_table": dtable, "q_norm_a": dgqa, "k_norm_a": dgka, "q_norm_b": dgqb, "k_norm_b": dgkb,
          "sinks": dsk[:, :, 0].reshape(1, N_HEADS),
          "ssm_conv_w": dscw, "ssm_conv_b": dscb,
          "ssm_dt_bias": ddtb[:, :SSM_HEADS], "ssm_a_log": dA[:, :SSM_HEADS] * A[:, :SSM_HEADS],
          "ssm_d": dDl.reshape(SSM_HEADS, D_INNER // SSM_HEADS).sum(axis=1).reshape(1, SSM_HEADS),
          "ssm_norm": dnw,
          "ffn_conv_w": jnp.stack([dfcw0, dfcw1]), "ffn_conv_b": jnp.concatenate([dfcb0, dfcb1], axis=0)}
    return lpart, dx, gW, gP


WEIGHTS = ["norm_mix", "norm_ffn", "attn_w_in", "attn_w_out", "relpos_table", "q_norm_a", "k_norm_a", "q_norm_b",
           "k_norm_b", "sinks", "ssm_w_in", "ssm_conv_w", "ssm_conv_b", "ssm_dt_bias", "ssm_a_log", "ssm_d",
           "ssm_norm", "ssm_w_out", "ffn_w_in", "ffn_conv_w", "ffn_conv_b", "ffn_w_out"]
ARGS = ["x"] + WEIGHTS + ["loss_target"] + ["m_" + w for w in WEIGHTS] + ["v_" + w for w in WEIGHTS]
N_CHIPS = 4
SMALL_ROWS = 384
SMALL_ORDER = ["norm_mix", "norm_ffn", "relpos_table", "q_norm_a", "k_norm_a", "q_norm_b", "k_norm_b", "sinks",
               "ssm_dt_bias", "ssm_a_log", "ssm_d", "ffn_conv_b", "ssm_conv_w", "ssm_conv_b", "ssm_norm", "ffn_conv_w"]


def _cols_to_slabs(g):
    K, N = g.shape
    return g.reshape(2, K // 2, N_CHIPS, N // N_CHIPS).transpose(0, 2, 1, 3)


def _rows_to_slabs(g):
    R, C = g.shape
    return g.reshape(N_CHIPS, 2, R // (2 * N_CHIPS), C).transpose(1, 0, 2, 3)


def _adamw(w, g, m, v, name):
    shp = w.shape
    two = lambda a: a.reshape((-1, shp[-1]))
    outs = [(shp[-1], F32)] * 3
    d, nm, nv = rowwise(f_adamw, [two(w), two(g), two(m), two(v)], [], outs, name="adamw_" + name)
    return d.reshape(shp), nm.reshape(shp), nv.reshape(shp)


def kernel(x, norm_mix, norm_ffn, attn_w_in, attn_w_out, relpos_table, q_norm_a, k_norm_a, q_norm_b, k_norm_b, sinks, ssm_w_in, ssm_conv_w, ssm_conv_b, ssm_dt_bias, ssm_a_log, ssm_d, ssm_norm, ssm_w_out, ffn_w_in, ffn_conv_w, ffn_conv_b, ffn_w_out, loss_target, m_norm_mix, m_norm_ffn, m_attn_w_in, m_attn_w_out, m_relpos_table, m_q_norm_a, m_k_norm_a, m_q_norm_b, m_k_norm_b, m_sinks, m_ssm_w_in, m_ssm_conv_w, m_ssm_conv_b, m_ssm_dt_bias, m_ssm_a_log, m_ssm_d, m_ssm_norm, m_ssm_w_out, m_ffn_w_in, m_ffn_conv_w, m_ffn_conv_b, m_ffn_w_out, v_norm_mix, v_norm_ffn, v_attn_w_in, v_attn_w_out, v_relpos_table, v_q_norm_a, v_k_norm_a, v_q_norm_b, v_k_norm_b, v_sinks, v_ssm_w_in, v_ssm_conv_w, v_ssm_conv_b, v_ssm_dt_bias, v_ssm_a_log, v_ssm_d, v_ssm_norm, v_ssm_w_out, v_ffn_w_in, v_ffn_conv_w, v_ffn_conv_b, v_ffn_w_out):
    d = dict(zip(ARGS, (x, norm_mix, norm_ffn, attn_w_in, attn_w_out, relpos_table, q_norm_a, k_norm_a, q_norm_b, k_norm_b, sinks, ssm_w_in, ssm_conv_w, ssm_conv_b, ssm_dt_bias, ssm_a_log, ssm_d, ssm_norm, ssm_w_out, ffn_w_in, ffn_conv_w, ffn_conv_b, ffn_w_out, loss_target, m_norm_mix, m_norm_ffn, m_attn_w_in, m_attn_w_out, m_relpos_table, m_q_norm_a, m_k_norm_a, m_q_norm_b, m_k_norm_b, m_sinks, m_ssm_w_in, m_ssm_conv_w, m_ssm_conv_b, m_ssm_dt_bias, m_ssm_a_log, m_ssm_d, m_ssm_norm, m_ssm_w_out, m_ffn_w_in, m_ffn_conv_w, m_ffn_conv_b, m_ffn_w_out, v_norm_mix, v_norm_ffn, v_attn_w_in, v_attn_w_out, v_relpos_table, v_q_norm_a, v_k_norm_a, v_q_norm_b, v_k_norm_b, v_sinks, v_ssm_w_in, v_ssm_conv_w, v_ssm_conv_b, v_ssm_dt_bias, v_ssm_a_log, v_ssm_d, v_ssm_norm, v_ssm_w_out, v_ffn_w_in, v_ffn_conv_w, v_ffn_conv_b, v_ffn_w_out)))
    xi, yi, ci = _pos()
    me = 2 * xi + yi
    csel = jnp.full((1, LANES), ci, F32)
    msel = jnp.full((1, LANES), me, F32)

    halves = lambda w: w.reshape((2, -1, w.shape[-1]))
    small_sh = jnp.concatenate([d["ssm_conv_w"].reshape(-1), d["ssm_conv_b"].reshape(-1), d["ssm_norm"].reshape(-1),
                                d["ffn_conv_w"].reshape(-1)])
    n_small = small_sh.shape[0]
    small_sh = jnp.pad(small_sh, (0, 2 * 40 * LANES - n_small)).reshape(2, 40, LANES)
    shards = [halves(d["attn_w_in"][0].astype(BF16)), halves(d["attn_w_out"][0].astype(BF16)),
              halves(d["ssm_w_in"][0].astype(BF16)), halves(d["ssm_w_out"][0].astype(BF16)),
              d["ffn_w_in"].astype(BF16), d["ffn_w_out"].astype(BF16), small_sh]
    g_ai, g_ao, g_si, g_so, g_fi, g_fo, g_sm = gather_chips(shards, name="gather_weights")
    cat_cols = lambda g: jnp.concatenate([g[j].reshape((-1, g.shape[-1])) for j in range(N_CHIPS)], axis=1)
    W = {"attn_in": cat_cols(g_ai), "attn_out": g_ao.reshape(-1, D_MODEL),
         "ssm_in": cat_cols(g_si), "ssm_out": g_so.reshape(-1, D_MODEL),
         "ffn_in": [jnp.concatenate([g_fi[j, l] for j in range(N_CHIPS)], axis=1) for l in range(2)],
         "ffn_out": [g_fo[:, l].reshape(-1, D_MODEL) for l in range(2)]}
    sm = g_sm.reshape(N_CHIPS, -1)
    CC = D_INNER + 2 * SSM_GROUPS * SSM_STATE
    c4, f4 = CC // N_CHIPS, D_FF // N_CHIPS
    o1 = SSM_CONV * c4
    o2 = o1 + c4
    o3 = o2 + D_INNER // N_CHIPS
    o4 = o3 + 2 * FFN_CONV * f4
    P = {k: d[k] for k in ["norm_mix", "norm_ffn", "q_norm_a", "k_norm_a", "q_norm_b", "k_norm_b", "sinks",
                           "ssm_dt_bias", "ssm_a_log", "ssm_d", "ffn_conv_b"]}
    P["relpos_table"] = d["relpos_table"][0]
    P["ssm_conv_w"] = sm[:, :o1].reshape(N_CHIPS, SSM_CONV, c4).transpose(1, 0, 2).reshape(SSM_CONV, CC)
    P["ssm_conv_b"] = sm[:, o1:o2].reshape(1, CC)
    P["ssm_norm"] = sm[:, o2:o3].reshape(1, D_INNER)
    P["ffn_conv_w"] = sm[:, o3:o4].reshape(N_CHIPS, 2, FFN_CONV, f4).transpose(1, 2, 0, 3).reshape(2, FFN_CONV, D_FF)

    lpart, dx, gW, gP = local_step(d["x"][0], d["loss_target"][0], W, P)
    loss = lax.psum(lpart[0, 0], ("x", "y", "c"))

    slabs = [_cols_to_slabs(gW["attn_in"]), _rows_to_slabs(gW["attn_out"]), _cols_to_slabs(gW["ssm_in"]),
             _rows_to_slabs(gW["ssm_out"]), _cols_to_slabs(gW["ffn_in"][0]), _cols_to_slabs(gW["ffn_in"][1]),
             _rows_to_slabs(gW["ffn_out"][0]), _rows_to_slabs(gW["ffn_out"][1])]
    tags = ["ai", "ao", "si", "so", "fi0", "fi1", "fo0", "fo1"]
    from_sib = pair_swap_halves(slabs, name="grad_pair_swap")
    pairs = [pair_add(g, r, csel, name="grad_pair_add_" + t) for g, r, t in zip(slabs, from_sib, tags)]
    from_chips = scatter_chips([p[1] for p in pairs], name="grad_scatter")
    mine = [chip_add(p[0], r, msel, name="grad_chip_add_" + t) for p, r, t in zip(pairs, from_chips, tags)]
    full = pair_share(mine, name="grad_pair_share")
    two = lambda a: a.reshape((-1, a.shape[-1]))
    grads = {"attn_w_in": two(full[0])[None], "attn_w_out": two(full[1])[None],
             "ssm_w_in": two(full[2])[None], "ssm_w_out": two(full[3])[None],
             "ffn_w_in": jnp.stack([two(full[4]), two(full[5])]),
             "ffn_w_out": jnp.stack([two(full[6]), two(full[7])])}

    flat = jnp.concatenate([gP[k].reshape(-1) for k in SMALL_ORDER])
    flat = jnp.pad(flat, (0, SMALL_ROWS * LANES - flat.shape[0])).reshape(SMALL_ROWS, LANES)
    tot = sum_slots(gather_all(flat, name="small_gather"), name="small_sum").reshape(-1)
    off = 0
    for k in SMALL_ORDER:
        n = int(np.prod(gP[k].shape))
        g = tot[off:off + n].reshape(gP[k].shape)
        off += n
        if k == "ssm_conv_w":
            g = lax.dynamic_slice_in_dim(g, me * c4, c4, axis=1)[None]
        elif k == "ssm_conv_b":
            g = lax.dynamic_slice_in_dim(g, me * c4, c4, axis=1)
        elif k == "ssm_norm":
            g = lax.dynamic_slice_in_dim(g, me * (D_INNER // N_CHIPS), D_INNER // N_CHIPS, axis=1)
        elif k == "ffn_conv_w":
            g = lax.dynamic_slice_in_dim(g, me * f4, f4, axis=2)
        elif k == "relpos_table":
            g = g[None]
        grads[k] = g

    deltas, new_m, new_v = {}, {}, {}
    for k in WEIGHTS:
        deltas[k], new_m[k], new_v[k] = _adamw(d[k], grads[k], d["m_" + k], d["v_" + k], k)
    return (loss, dx[None], *[grads[k] for k in WEIGHTS], *[deltas[k] for k in WEIGHTS],
            *[new_m[k] for k in WEIGHTS], *[new_v[k] for k in WEIGHTS])
```

```python
import functools

import numpy as np
import jax
import jax.numpy as jnp
from jax import lax
from jax.experimental import pallas as pl
from jax.experimental.pallas import tpu as pltpu

F32 = jnp.float32
BF16 = jnp.bfloat16
HI = lax.Precision.HIGHEST

D_MODEL = 1024
CHUNK = 64
EPS = 1e-6
HEAD_DIM = 64
N_HEADS = 8
A_PREV = 8
B_PREV = 2
MAX_REL = 256
D_INNER = 2048
SSM_HEADS = 32
SSM_GROUPS = 4
SSM_STATE = 128
SSM_CONV = 4
D_FF = 2816
FFN_CONV = 3
LANES = 128
SUBLANES = 8
VMEM_LIMIT = 56 * 1024 * 1024
SSD_L = 128

ADAM_LR = 0.001
ADAM_B1 = 0.9
ADAM_B2 = 0.999
ADAM_EPS = 1e-08
ADAM_WD = 0.01
ADAM_STEP = 10

MESH = pl.DeviceIdType.MESH


def _params(*sem):
    return pltpu.CompilerParams(dimension_semantics=sem, vmem_limit_bytes=VMEM_LIMIT)


def _pick(n, want):
    if n <= want:
        return n
    t = (want // LANES) * LANES
    while t >= LANES:
        if n % t == 0:
            return t
        t -= LANES
    return n


MM_ROWS = 512
MM_COLS = 1536
MM_RED = 512


def matmul(a, b, *, mode, name, out_dtype=F32, residual=None):
    dims = {"nn": (((1,), (0,)), ((), ())), "nt": (((1,), (1,)), ((), ())), "tn": (((0,), (0,)), ((), ()))}[mode]
    if mode == "tn":
        assert residual is None and out_dtype == F32
        (K, M), (K2, N) = a.shape, b.shape
        assert K == K2, (a.shape, b.shape)
        tm, tn, tk = _pick(M, MM_COLS), _pick(N, MM_COLS), _pick(K, MM_RED)

        def body(a_ref, b_ref, o_ref):
            k = pl.program_id(2)
            p = lax.dot_general(a_ref[...].astype(BF16), b_ref[...].astype(BF16), dims, preferred_element_type=F32)

            @pl.when(k == 0)
            def _():
                o_ref[...] = p

            @pl.when(k != 0)
            def _():
                o_ref[...] += p

        return pl.pallas_call(
            body, name=name, grid=(M // tm, N // tn, K // tk),
            in_specs=[pl.BlockSpec((tk, tm), lambda i, j, k: (k, i)), pl.BlockSpec((tk, tn), lambda i, j, k: (k, j))],
            out_specs=pl.BlockSpec((tm, tn), lambda i, j, k: (i, j)),
            out_shape=jax.ShapeDtypeStruct((M, N), F32),
            compiler_params=_params("parallel", "parallel", "arbitrary"),
        )(a, b)

    if mode == "nn":
        (M, K), (K2, N) = a.shape, b.shape
    else:
        (M, K), (N, K2) = a.shape, b.shape
    assert K == K2, (a.shape, b.shape, mode)
    tm, tn = _pick(M, MM_ROWS), _pick(N, MM_COLS)

    def body(*refs):
        a_ref, b_ref = refs[:2]
        o_ref = refs[-1]
        r = lax.dot_general(a_ref[...].astype(BF16), b_ref[...].astype(BF16), dims, preferred_element_type=F32)
        if residual is not None:
            r = r + refs[2][...].astype(F32)
        o_ref[...] = r.astype(o_ref.dtype)

    a_spec = pl.BlockSpec((tm, K), lambda j, i: (i, 0))
    b_spec = pl.BlockSpec((K, tn), lambda j, i: (0, j)) if mode == "nn" else pl.BlockSpec((tn, K), lambda j, i: (j, 0))
    o_spec = pl.BlockSpec((tm, tn), lambda j, i: (i, j))
    in_specs = [a_spec, b_spec] + ([o_spec] if residual is not None else [])
    args = (a, b) + ((residual,) if residual is not None else ())
    return pl.pallas_call(
        body, name=name, grid=(N // tn, M // tm),
        in_specs=in_specs, out_specs=o_spec,
        out_shape=jax.ShapeDtypeStruct((M, N), out_dtype),
        compiler_params=_params("parallel", "parallel"),
    )(*args)


def rowwise(f, rows, params, outs, *, name, tm=256):
    S = rows[0].shape[0]
    tm = _row_tile(S, tm)
    nr, npar = len(rows), len(params)

    def body(*refs):
        vals = [r[...] for r in refs[:nr + npar]]
        res = f(*vals)
        for o_ref, r in zip(refs[nr + npar:], res):
            o_ref[...] = r.astype(o_ref.dtype)

    in_specs = [pl.BlockSpec((tm, r.shape[1]), lambda i: (i, 0)) for r in rows]
    in_specs += [pl.BlockSpec(p.shape, lambda i: (0, 0)) for p in params]
    out_specs = [pl.BlockSpec((tm, c), lambda i: (i, 0)) for c, _ in outs]
    out_shape = [jax.ShapeDtypeStruct((S, c), dt) for c, dt in outs]
    return pl.pallas_call(body, name=name, grid=(S // tm,), in_specs=in_specs, out_specs=out_specs,
                          out_shape=out_shape, compiler_params=_params("parallel"))(*rows, *params)


def rowwise_vjp(f, rows, params, cots, drow, dpar, *, name, tm=256):
    S = rows[0].shape[0]
    tm = _row_tile(S, tm)
    nr, npar, nc = len(rows), len(params), len(cots)

    def body(*refs):
        vals = [r[...] for r in refs[:nr + npar]]
        cvals = [r[...].astype(F32) for r in refs[nr + npar:nr + npar + nc]]
        o_refs = refs[nr + npar + nc:]
        want = [ri for ri, _ in drow] + [nr + pi for pi in dpar]

        def f_want(*d):
            full = list(vals)
            for k, v in zip(want, d):
                full[k] = v
            return f(*full)

        _, vjp = jax.vjp(f_want, *[vals[k] for k in want])
        grads = vjp(tuple(cvals))
        for o_ref, g in zip(o_refs[:len(drow)], grads):
            o_ref[...] = g.astype(o_ref.dtype)
        first = pl.program_id(0) == 0
        for o_ref, g in zip(o_refs[len(drow):], grads[len(drow):]):
            g = g.astype(F32)

            @pl.when(first)
            def _(o_ref=o_ref, g=g):
                o_ref[...] = g

            @pl.when(jnp.logical_not(first))
            def _(o_ref=o_ref, g=g):
                o_ref[...] += g

    in_specs = [pl.BlockSpec((tm, r.shape[1]), lambda i: (i, 0)) for r in rows]
    in_specs += [pl.BlockSpec(p.shape, lambda i: (0, 0)) for p in params]
    in_specs += [pl.BlockSpec((tm, c.shape[1]), lambda i: (i, 0)) for c in cots]
    out_specs = [pl.BlockSpec((tm, rows[ri].shape[1]), lambda i: (i, 0)) for ri, _ in drow]
    out_specs += [pl.BlockSpec(params[pi].shape, lambda i: (0, 0)) for pi in dpar]
    out_shape = [jax.ShapeDtypeStruct(rows[ri].shape, dt) for ri, dt in drow]
    out_shape += [jax.ShapeDtypeStruct(params[pi].shape, F32) for pi in dpar]
    return pl.pallas_call(body, name=name, grid=(S // tm,), in_specs=in_specs, out_specs=out_specs,
                          out_shape=out_shape, compiler_params=_params("arbitrary"))(*rows, *params, *cots)


HALO = SUBLANES


def dwconv_fwd(x, w, b, *, name, tm=256):
    S, C = x.shape
    K = w.shape[0]
    tm = min(tm, S)
    hb = tm // HALO

    def body(x_ref, halo_ref, w_ref, b_ref, y_ref, buf):
        i = pl.program_id(0)
        buf[0:HALO, :] = jnp.where(i == 0, 0.0, halo_ref[...])
        buf[HALO:HALO + tm, :] = x_ref[...]
        acc = jnp.broadcast_to(b_ref[...], (tm, C))
        for k in range(K):
            acc = acc + w_ref[k:k + 1, :] * buf[pl.ds(HALO - (K - 1) + k, tm), :]
        y_ref[...] = acc

    return pl.pallas_call(
        body, name=name, grid=(S // tm,),
        in_specs=[pl.BlockSpec((tm, C), lambda i: (i, 0)),
                  pl.BlockSpec((HALO, C), lambda i: (jnp.maximum(i * hb - 1, 0), 0)),
                  pl.BlockSpec((K, C), lambda i: (0, 0)),
                  pl.BlockSpec((1, C), lambda i: (0, 0))],
        out_specs=pl.BlockSpec((tm, C), lambda i: (i, 0)),
        out_shape=jax.ShapeDtypeStruct((S, C), F32),
        scratch_shapes=[pltpu.VMEM((HALO + tm, C), F32)],
        compiler_params=_params("parallel"),
    )(x, x, w, b)


def dwconv_bwd(x, dy, w, *, name, tm=256):
    S, C = x.shape
    K = w.shape[0]
    tm = min(tm, S)
    hb = tm // HALO
    n = S // tm

    def body(x_ref, xh_ref, dy_ref, dyh_ref, w_ref, dx_ref, dw_ref, db_ref, bx, bd):
        i = pl.program_id(0)
        bx[0:HALO, :] = jnp.where(i == 0, 0.0, xh_ref[...])
        bx[HALO:HALO + tm, :] = x_ref[...]
        dyv = dy_ref[...]
        bd[0:tm, :] = dyv
        bd[tm:tm + HALO, :] = jnp.where(i == n - 1, 0.0, dyh_ref[...])
        acc = jnp.zeros((tm, C), F32)
        for k in range(K):
            acc = acc + w_ref[k:k + 1, :] * bd[pl.ds((K - 1) - k, tm), :]
        dx_ref[...] = acc

        @pl.when(i == 0)
        def _():
            dw_ref[...] = jnp.zeros_like(dw_ref)
            db_ref[...] = jnp.zeros_like(db_ref)

        for k in range(K):
            dw_ref[k:k + 1, :] += jnp.sum(dyv * bx[pl.ds(HALO - (K - 1) + k, tm), :], axis=0, keepdims=True)
        db_ref[...] += jnp.sum(dyv, axis=0, keepdims=True)

    return pl.pallas_call(
        body, name=name, grid=(n,),
        in_specs=[pl.BlockSpec((tm, C), lambda i: (i, 0)),
                  pl.BlockSpec((HALO, C), lambda i: (jnp.maximum(i * hb - 1, 0), 0)),
                  pl.BlockSpec((tm, C), lambda i: (i, 0)),
                  pl.BlockSpec((HALO, C), lambda i: (jnp.minimum((i + 1) * hb, S // HALO - 1), 0)),
                  pl.BlockSpec((K, C), lambda i: (0, 0))],
        out_specs=[pl.BlockSpec((tm, C), lambda i: (i, 0)),
                   pl.BlockSpec((K, C), lambda i: (0, 0)),
                   pl.BlockSpec((1, C), lambda i: (0, 0))],
        out_shape=[jax.ShapeDtypeStruct((S, C), F32), jax.ShapeDtypeStruct((K, C), F32),
                   jax.ShapeDtypeStruct((1, C), F32)],
        scratch_shapes=[pltpu.VMEM((HALO + tm, C), F32), pltpu.VMEM((tm + HALO, C), F32)],
        compiler_params=_params("arbitrary"),
    )(x, x, dy, dy, w)


def _sigmoid(x):
    return 1.0 / (1.0 + jnp.exp(-x))


def _silu(x):
    return x * _sigmoid(x)


def f_rmsnorm(x, g):
    return (x * lax.rsqrt(jnp.mean(x * x, axis=-1, keepdims=True) + EPS) * g,)


def f_ffn_act(gc, up):
    return (_silu(gc) * up,)


def f_silu(x):
    return (_silu(x),)


def _group_norm(x, bd, width):
    ms = jnp.dot(x * x, bd, precision=HI, preferred_element_type=F32) * (1.0 / width)
    return x * lax.rsqrt(ms + EPS)


def f_qknorm(qkv, gqa, gka, gqb, gkb, bd512, bd128, fold, expand):
    dq = N_HEADS * HEAD_DIM
    qa, ka, va, qb = (qkv[:, i * dq:(i + 1) * dq] for i in range(4))
    kb = qkv[:, 4 * dq:4 * dq + LANES]
    vb = qkv[:, 4 * dq + LANES:4 * dq + 2 * LANES]
    tile8 = lambda g: jnp.dot(g, fold, precision=HI, preferred_element_type=F32)
    qa = _group_norm(qa, bd512, HEAD_DIM) * tile8(gqa)
    ka = _group_norm(ka, bd512, HEAD_DIM) * tile8(gka)
    qb = _group_norm(qb, bd512, HEAD_DIM) * tile8(gqb)
    kb = _group_norm(kb, bd128, HEAD_DIM) * tile8(gkb)[:, :LANES]
    kb = jnp.dot(kb, expand, precision=HI, preferred_element_type=F32)
    vb = jnp.dot(vb, expand, precision=HI, preferred_element_type=F32)
    return qa, ka, va, qb, kb, vb


def f_gate_norm(y, z, nw):
    v = y * _silu(z)
    gw = D_INNER // SSM_GROUPS
    parts = []
    for g in range(SSM_GROUPS):
        vg = v[:, g * gw:(g + 1) * gw]
        parts.append(vg * lax.rsqrt(jnp.mean(vg * vg, axis=-1, keepdims=True) + EPS))
    return (jnp.concatenate(parts, axis=-1) * nw,)


ATT_TQ = 256
_NT = (((1,), (1,)), ((), ()))
_TN = (((0,), (0,)), ((), ()))


def _attn_probs(qh, kb, bias, valid, snk):
    s = lax.dot_general(qh, kb, _NT, preferred_element_type=F32) * (HEAD_DIM ** -0.5) + bias
    s = jnp.where(valid, s, -jnp.inf)
    m = jnp.max(s, axis=1, keepdims=True)
    if snk is not None:
        m = jnp.maximum(m, snk)
    e = jnp.exp(s - m)
    den = jnp.sum(e, axis=1, keepdims=True)
    if snk is None:
        return e / den, None
    es = jnp.exp(snk - m)
    den = den + es
    return e / den, es / den


def widen_bias(bias, n_prev, nj):
    band = (n_prev + 1) * CHUNK
    wk = (nj + n_prev) * CHUNK
    rows = [jnp.pad(bias, ((0, 0), (0, 0), (j * CHUNK, wk - band - j * CHUNK)), constant_values=-jnp.inf)
            for j in range(nj)]
    return jnp.concatenate(rows, axis=1)


def fold_bias(dbw, n_prev, nj):
    band = (n_prev + 1) * CHUNK
    acc = dbw[:, :CHUNK, :band]
    for j in range(1, nj):
        acc = acc + dbw[:, j * CHUNK:(j + 1) * CHUNK, j * CHUNK:j * CHUNK + band]
    return acc


def attn_fwd(q, k, v, bias_w, sinks, *, n_prev, name):
    S = q.shape[0]
    pad = n_prev * CHUNK
    tq = min(ATT_TQ, S)
    wk = tq + pad
    assert bias_w.shape == (N_HEADS, tq, wk), bias_w.shape
    has_sink = sinks is not None

    def body(*refs):
        if has_sink:
            q_ref, k_ref, v_ref, bias_ref, sink_ref, o_ref = refs
        else:
            q_ref, k_ref, v_ref, bias_ref, o_ref = refs
        start = pl.multiple_of(pl.program_id(1) * tq, tq)
        head0 = lax.broadcasted_iota(jnp.int32, (1, LANES), 1) < HEAD_DIM
        valid = lax.broadcasted_iota(jnp.int32, (1, wk), 1) + start >= pad
        qp = q_ref[...].astype(F32)
        kb = k_ref[pl.ds(start, wk), :]
        vb = v_ref[pl.ds(start, wk), :]
        outs = []
        for r in range(2):
            mh = head0 if r == 0 else jnp.logical_not(head0)
            qh = jnp.where(mh, qp, 0.0).astype(BF16)
            snk = sink_ref[0, r:r + 1, 0:1] if has_sink else None
            p, _ = _attn_probs(qh, kb, bias_ref[r], valid, snk)
            outs.append(jnp.dot(p.astype(BF16), vb, preferred_element_type=F32))
        o_ref[...] = jnp.where(head0, outs[0], outs[1]).astype(o_ref.dtype)

    in_specs = [pl.BlockSpec((tq, LANES), lambda p, i: (i, p)),
                pl.BlockSpec((pad + S, LANES), lambda p, i: (0, p)),
                pl.BlockSpec((pad + S, LANES), lambda p, i: (0, p)),
                pl.BlockSpec((2, tq, wk), lambda p, i: (p, 0, 0))]
    args = [q, k, v, bias_w]
    if has_sink:
        in_specs.append(pl.BlockSpec((1, 2, LANES), lambda p, i: (p, 0, 0)))
        args.append(sinks)
    return pl.pallas_call(
        body, name=name, grid=(N_HEADS // 2, S // tq), in_specs=in_specs,
        out_specs=pl.BlockSpec((tq, LANES), lambda p, i: (i, p)),
        out_shape=jax.ShapeDtypeStruct((S, N_HEADS * HEAD_DIM), BF16),
        compiler_params=_params("parallel", "parallel"),
    )(*args)


def attn_bwd(q, k, v, do, bias_w, sinks, *, n_prev, name):
    S = q.shape[0]
    pad = n_prev * CHUNK
    tq = min(ATT_TQ, S)
    wk = tq + pad
    assert bias_w.shape == (N_HEADS, tq, wk), bias_w.shape
    has_sink = sinks is not None
    scale = HEAD_DIM ** -0.5

    def body(*refs):
        if has_sink:
            q_ref, k_ref, v_ref, do_ref, bias_ref, sink_ref, dq_ref, dk_ref, dv_ref, db_ref, dsk_ref = refs
        else:
            q_ref, k_ref, v_ref, do_ref, bias_ref, dq_ref, dk_ref, dv_ref, db_ref = refs
        i = pl.program_id(1)

        @pl.when(i == 0)
        def _():
            dk_ref[...] = jnp.zeros_like(dk_ref)
            dv_ref[...] = jnp.zeros_like(dv_ref)
            db_ref[...] = jnp.zeros_like(db_ref)
            if has_sink:
                dsk_ref[...] = jnp.zeros_like(dsk_ref)

        start = pl.multiple_of(i * tq, tq)
        head0 = lax.broadcasted_iota(jnp.int32, (1, LANES), 1) < HEAD_DIM
        valid = lax.broadcasted_iota(jnp.int32, (1, wk), 1) + start >= pad
        qp = q_ref[...].astype(F32)
        dop = do_ref[...].astype(F32)
        kb = k_ref[pl.ds(start, wk), :]
        vb = v_ref[pl.ds(start, wk), :]
        dqs = []
        for r in range(2):
            mh = head0 if r == 0 else jnp.logical_not(head0)
            qh = jnp.where(mh, qp, 0.0).astype(BF16)
            doh = jnp.where(mh, dop, 0.0).astype(BF16)
            snk = sink_ref[0, r:r + 1, 0:1] if has_sink else None
            p, ps = _attn_probs(qh, kb, bias_ref[r], valid, snk)
            dp = lax.dot_general(doh, vb, _NT, preferred_element_type=F32)
            delta = jnp.sum(p * dp, axis=1, keepdims=True)
            ds = p * (dp - delta)
            db_ref[r] += ds
            if has_sink:
                dsk = -jnp.sum(ps * delta, axis=0, keepdims=True)
                dsk_ref[0, r:r + 1, :] += jnp.broadcast_to(dsk, (1, LANES))
            dsb = ds.astype(BF16)
            dqs.append(jnp.dot(dsb, kb, preferred_element_type=F32) * scale)
            dk_ref[pl.ds(start, wk), :] += lax.dot_general(dsb, qh, _TN, preferred_element_type=F32) * scale
            dv_ref[pl.ds(start, wk), :] += lax.dot_general(p.astype(BF16), doh, _TN, preferred_element_type=F32)
        dq_ref[...] = jnp.where(head0, dqs[0], dqs[1])

    row_spec = pl.BlockSpec((tq, LANES), lambda p, i: (i, p))
    kv_spec = pl.BlockSpec((pad + S, LANES), lambda p, i: (0, p))
    bias_spec = pl.BlockSpec((2, tq, wk), lambda p, i: (p, 0, 0))
    sink_spec = pl.BlockSpec((1, 2, LANES), lambda p, i: (p, 0, 0))
    in_specs = [row_spec, kv_spec, kv_spec, row_spec, bias_spec]
    args = [q, k, v, do, bias_w]
    out_specs = [row_spec, kv_spec, kv_spec, bias_spec]
    W = N_HEADS * HEAD_DIM
    out_shape = [jax.ShapeDtypeStruct((S, W), F32), jax.ShapeDtypeStruct((pad + S, W), F32),
                 jax.ShapeDtypeStruct((pad + S, W), F32), jax.ShapeDtypeStruct((N_HEADS, tq, wk), F32)]
    if has_sink:
        in_specs.append(sink_spec)
        args.append(sinks)
        out_specs.append(sink_spec)
        out_shape.append(jax.ShapeDtypeStruct((N_HEADS // 2, 2, LANES), F32))
    return pl.pallas_call(
        body, name=name, grid=(N_HEADS // 2, S // tq), in_specs=in_specs, out_specs=out_specs,
        out_shape=out_shape, compiler_params=_params("arbitrary", "arbitrary"),
    )(*args)


HP = SSM_HEADS // 2
PAIRS_PER_GROUP = HP // SSM_GROUPS
HEADS_PER_GROUP = SSM_HEADS // SSM_GROUPS
GW = HEADS_PER_GROUP * 64


def _ssd_dt(dtraw, dtb, A, tril):
    lane = lax.broadcasted_iota(jnp.int32, (1, LANES), 1)
    u = dtraw + dtb
    eu = jnp.exp(-jnp.abs(u))
    w1 = 1.0 + eu
    l1p = jnp.where(w1 == 1.0, eu, jnp.log(w1) * eu / jnp.where(w1 == 1.0, 1.0, w1 - 1.0))
    dt = jnp.where(lane < SSM_HEADS, jnp.maximum(u, 0.0) + l1p, 0.0)
    acs = jnp.dot(tril, dt * A, precision=HI, preferred_element_type=F32)
    return u, dt, acs


def _ssd_head(acs, acsT, dt, h, trilb):
    L = acs.shape[0]
    ca = acs[:, h:h + 1]
    ra = acsT[h:h + 1, :]
    dec = jnp.exp(jnp.where(trilb, ca - ra, -jnp.inf))
    last = acs[L - 1:L, h:h + 1]
    return dec, dt[:, h:h + 1], jnp.exp(ca), jnp.exp(last - ca), last


def ssd_fwd(xbc, dtraw, dtb, A, dexp, *, name):
    S = xbc.shape[0]
    L = min(SSD_L, S)
    nc = S // L
    N = SSM_STATE

    def body(xs_ref, b_ref, c_ref, dtr_ref, dtb_ref, a_ref, d_ref, y_ref, st_out_ref, st_ref, xw_ref):
        c = pl.program_id(0)

        @pl.when(c == 0)
        def _():
            st_ref[...] = jnp.zeros_like(st_ref)

        st_out_ref[0] = st_ref[...]
        ri = lax.broadcasted_iota(jnp.int32, (L, L), 0)
        ci = lax.broadcasted_iota(jnp.int32, (L, L), 1)
        trilb = ri >= ci
        head0 = lax.broadcasted_iota(jnp.int32, (1, LANES), 1) < 64
        _, dt, acs = _ssd_dt(dtr_ref[...], dtb_ref[...], a_ref[...], trilb.astype(F32))
        acsT = acs.T
        lasts = []
        for g in range(SSM_GROUPS):
            Bg = b_ref[:, g * N:(g + 1) * N].astype(BF16)
            Cg = c_ref[:, g * N:(g + 1) * N].astype(BF16)
            CB = lax.dot_general(Cg, Bg, _NT, preferred_element_type=F32)
            Z = lax.dot_general(Cg, st_ref[g * GW:(g + 1) * GW, :].astype(BF16), _NT, preferred_element_type=F32)
            for q in range(PAIRS_PER_GROUP):
                hp = g * PAIRS_PER_GROUP + q
                sl = slice(hp * LANES, (hp + 1) * LANES)
                xs = xs_ref[:, sl]
                hd = [_ssd_head(acs, acsT, dt, 2 * hp + r, trilb) for r in range(2)]
                dtp = jnp.where(head0, hd[0][1], hd[1][1])
                eap = jnp.where(head0, hd[0][2], hd[1][2])
                wp = jnp.where(head0, hd[0][3] * hd[0][1], hd[1][3] * hd[1][1])
                xd = xs * dtp
                yi = jnp.dot((CB * hd[0][0]).astype(BF16), jnp.where(head0, xd, 0.0).astype(BF16), preferred_element_type=F32)
                yi = yi + jnp.dot((CB * hd[1][0]).astype(BF16), jnp.where(head0, 0.0, xd).astype(BF16), preferred_element_type=F32)
                y_ref[:, sl] = yi + Z[:, q * LANES:(q + 1) * LANES] * eap + d_ref[:, sl] * xs
                xw_ref[:, sl] = (xs * wp).astype(BF16)
                lasts += [hd[0][4], hd[1][4]]
        for g in range(SSM_GROUPS):
            Bg = b_ref[:, g * N:(g + 1) * N].astype(BF16)
            sn = lax.dot_general(xw_ref[:, g * GW:(g + 1) * GW], Bg, _TN, preferred_element_type=F32)
            for k in range(HEADS_PER_GROUP):
                h = g * HEADS_PER_GROUP + k
                rows = slice(h * 64, (h + 1) * 64)
                st_ref[rows, :] = st_ref[rows, :] * jnp.exp(lasts[h]) + sn[k * 64:(k + 1) * 64, :]

    return pl.pallas_call(
        body, name=name, grid=(nc,),
        in_specs=[pl.BlockSpec((L, D_INNER), lambda c: (c, 0)),
                  pl.BlockSpec((L, SSM_GROUPS * N), lambda c: (c, D_INNER // (SSM_GROUPS * N))),
                  pl.BlockSpec((L, SSM_GROUPS * N), lambda c: (c, D_INNER // (SSM_GROUPS * N) + 1)),
                  pl.BlockSpec((L, LANES), lambda c: (c, 0)),
                  pl.BlockSpec((1, LANES), lambda c: (0, 0)),
                  pl.BlockSpec((1, LANES), lambda c: (0, 0)),
                  pl.BlockSpec((1, D_INNER), lambda c: (0, 0))],
        out_specs=[pl.BlockSpec((L, D_INNER), lambda c: (c, 0)),
                   pl.BlockSpec((1, D_INNER, N), lambda c: (c, 0, 0))],
        out_shape=[jax.ShapeDtypeStruct((S, D_INNER), F32), jax.ShapeDtypeStruct((nc, D_INNER, N), F32)],
        scratch_shapes=[pltpu.VMEM((D_INNER, N), F32), pltpu.VMEM((L, D_INNER), BF16)],
        compiler_params=_params("arbitrary"),
    )(xbc, xbc, xbc, dtraw, dtb, A, dexp)


def ssd_bwd(xbc, dtraw, dtb, A, dexp, states, dy, *, name):
    S = xbc.shape[0]
    L = min(SSD_L, S)
    nc = S // L
    N = SSM_STATE

    def body(xs_ref, b_ref, c_ref, dtr_ref, dtb_ref, a_ref, d_ref, st_in_ref, dy_ref,
             dxs_ref, db_ref, dc_ref, ddtr_ref, da_ref, ddtb_ref, dd_ref, dst_ref, xw_ref, dz_ref):
        step = pl.program_id(0)

        @pl.when(step == 0)
        def _():
            dst_ref[...] = jnp.zeros_like(dst_ref)
            da_ref[...] = jnp.zeros_like(da_ref)
            ddtb_ref[...] = jnp.zeros_like(ddtb_ref)
            dd_ref[...] = jnp.zeros_like(dd_ref)

        ri = lax.broadcasted_iota(jnp.int32, (L, L), 0)
        ci = lax.broadcasted_iota(jnp.int32, (L, L), 1)
        trilb = ri >= ci
        lane = lax.broadcasted_iota(jnp.int32, (1, LANES), 1)
        sub = lax.broadcasted_iota(jnp.int32, (LANES, 1), 0)
        head0 = lane < 64
        A = a_ref[...]
        u, dt, acs = _ssd_dt(dtr_ref[...], dtb_ref[...], A, trilb.astype(F32))
        acsT = acs.T
        dacs = jnp.zeros((L, LANES), F32)
        dacsT = jnp.zeros((LANES, L), F32)
        ddt = jnp.zeros((L, LANES), F32)
        dlast = jnp.zeros((1, LANES), F32)
        lasts = []
        dcbs = []
        for g in range(SSM_GROUPS):
            Bg = b_ref[:, g * N:(g + 1) * N].astype(BF16)
            Cg = c_ref[:, g * N:(g + 1) * N].astype(BF16)
            stg = st_in_ref[0, g * GW:(g + 1) * GW, :]
            dstg = dst_ref[g * GW:(g + 1) * GW, :]
            CB = lax.dot_general(Cg, Bg, _NT, preferred_element_type=F32)
            Z = lax.dot_general(Cg, stg.astype(BF16), _NT, preferred_element_type=F32)
            U = lax.dot_general(Bg, dstg.astype(BF16), _NT, preferred_element_type=F32)
            dcb = jnp.zeros((L, L), F32)
            for q in range(PAIRS_PER_GROUP):
                hp = g * PAIRS_PER_GROUP + q
                sl = slice(hp * LANES, (hp + 1) * LANES)
                qs = slice(q * LANES, (q + 1) * LANES)
                xs = xs_ref[:, sl]
                dyp = dy_ref[:, sl]
                hd = [_ssd_head(acs, acsT, dt, 2 * hp + r, trilb) for r in range(2)]
                dtp = jnp.where(head0, hd[0][1], hd[1][1])
                eap = jnp.where(head0, hd[0][2], hd[1][2])
                wcol = [hd[r][3] * hd[r][1] for r in range(2)]
                wp = jnp.where(head0, wcol[0], wcol[1])
                xd = xs * dtp
                dxd = jnp.zeros((L, LANES), F32)
                for r in range(2):
                    h = 2 * hp + r
                    mh = head0 if r == 0 else jnp.logical_not(head0)
                    dyh = jnp.where(mh, dyp, 0.0).astype(BF16)
                    xdh = jnp.where(mh, xd, 0.0).astype(BF16)
                    G = lax.dot_general(dyh, xdh, _NT, preferred_element_type=F32)
                    dec = hd[r][0]
                    M = CB * dec
                    dcb = dcb + G * dec
                    dseg = G * M
                    dacs = dacs + jnp.where(lane == h, jnp.sum(dseg, axis=1, keepdims=True), 0.0)
                    dacsT = dacsT - jnp.where(sub == h, jnp.sum(dseg, axis=0, keepdims=True), 0.0)
                    dxd = dxd + lax.dot_general(M.astype(BF16), dyh, _TN, preferred_element_type=F32)
                Up = U[:, qs]
                dyz = dyp * Z[:, qs] * eap
                t1 = dxd * xs
                t2 = Up * xs
                for r in range(2):
                    h = 2 * hp + r
                    mh = head0 if r == 0 else jnp.logical_not(head0)
                    rs = lambda t: jnp.sum(jnp.where(mh, t, 0.0), axis=1, keepdims=True)
                    dw = rs(t2)
                    dww = dw * wcol[r]
                    dacs = dacs + jnp.where(lane == h, rs(dyz) - dww, 0.0)
                    ddt = ddt + jnp.where(lane == h, rs(t1) + dw * hd[r][3], 0.0)
                    rows = slice((h % HEADS_PER_GROUP) * 64, (h % HEADS_PER_GROUP + 1) * 64)
                    dl = jnp.sum(dww, axis=0, keepdims=True) + jnp.sum(dstg[rows, :] * stg[rows, :], keepdims=True) * jnp.exp(hd[r][4])
                    dlast = dlast + jnp.where(lane == h, dl, 0.0)
                    lasts.append(hd[r][4])
                dz_ref[:, sl] = (dyp * eap).astype(BF16)
                xw_ref[:, sl] = (xs * wp).astype(BF16)
                dxs_ref[:, sl] = dxd * dtp + d_ref[:, sl] * dyp + Up * wp
                dd_ref[:, sl] += jnp.sum(dyp * xs, axis=0, keepdims=True)
            dcbs.append(dcb)
        for g in range(SSM_GROUPS):
            Bg = b_ref[:, g * N:(g + 1) * N].astype(BF16)
            Cg = c_ref[:, g * N:(g + 1) * N].astype(BF16)
            gs = slice(g * GW, (g + 1) * GW)
            stb = st_in_ref[0, gs, :].astype(BF16)
            dstb = dst_ref[gs, :].astype(BF16)
            dcbb = dcbs[g].astype(BF16)
            dzg = dz_ref[:, gs]
            dc_ref[:, g * N:(g + 1) * N] = (jnp.dot(dzg, stb, preferred_element_type=F32)
                                            + jnp.dot(dcbb, Bg, preferred_element_type=F32))
            db_ref[:, g * N:(g + 1) * N] = (jnp.dot(xw_ref[:, gs], dstb, preferred_element_type=F32)
                                            + lax.dot_general(dcbb, Cg, _TN, preferred_element_type=F32))
            dsn = lax.dot_general(dzg, Cg, _TN, preferred_element_type=F32)
            for k in range(HEADS_PER_GROUP):
                h = g * HEADS_PER_GROUP + k
                rows = slice(h * 64, (h + 1) * 64)
                dst_ref[rows, :] = dst_ref[rows, :] * jnp.exp(lasts[h]) + dsn[k * 64:(k + 1) * 64, :]
        rowi = lax.broadcasted_iota(jnp.int32, (L, 1), 0)
        dacs = dacs + dacsT.T + jnp.where(rowi == L - 1, dlast, 0.0)
        da = jnp.dot((ci >= ri).astype(F32), dacs, precision=HI, preferred_element_type=F32)
        ddt = ddt + da * A
        da_ref[...] += jnp.sum(da * dt, axis=0, keepdims=True)
        ddtr = jnp.where(lane < SSM_HEADS, ddt * _sigmoid(u), 0.0)
        ddtr_ref[...] = ddtr
        ddtb_ref[...] += jnp.sum(ddtr, axis=0, keepdims=True)

    rev = lambda c: nc - 1 - c
    gn = SSM_GROUPS * N
    return pl.pallas_call(
        body, name=name, grid=(nc,),
        in_specs=[pl.BlockSpec((L, D_INNER), lambda c: (rev(c), 0)),
                  pl.BlockSpec((L, gn), lambda c: (rev(c), D_INNER // gn)),
                  pl.BlockSpec((L, gn), lambda c: (rev(c), D_INNER // gn + 1)),
                  pl.BlockSpec((L, LANES), lambda c: (rev(c), 0)),
                  pl.BlockSpec((1, LANES), lambda c: (0, 0)),
                  pl.BlockSpec((1, LANES), lambda c: (0, 0)),
                  pl.BlockSpec((1, D_INNER), lambda c: (0, 0)),
                  pl.BlockSpec((1, D_INNER, N), lambda c: (rev(c), 0, 0)),
                  pl.BlockSpec((L, D_INNER), lambda c: (rev(c), 0))],
        out_specs=[pl.BlockSpec((L, D_INNER), lambda c: (rev(c), 0)),
                   pl.BlockSpec((L, gn), lambda c: (rev(c), 0)),
                   pl.BlockSpec((L, gn), lambda c: (rev(c), 0)),
                   pl.BlockSpec((L, LANES), lambda c: (rev(c), 0)),
                   pl.BlockSpec((1, LANES), lambda c: (0, 0)),
                   pl.BlockSpec((1, LANES), lambda c: (0, 0)),
                   pl.BlockSpec((1, D_INNER), lambda c: (0, 0))],
        out_shape=[jax.ShapeDtypeStruct((S, D_INNER), F32), jax.ShapeDtypeStruct((S, gn), F32),
                   jax.ShapeDtypeStruct((S, gn), F32), jax.ShapeDtypeStruct((S, LANES), F32),
                   jax.ShapeDtypeStruct((1, LANES), F32), jax.ShapeDtypeStruct((1, LANES), F32),
                   jax.ShapeDtypeStruct((1, D_INNER), F32)],
        scratch_shapes=[pltpu.VMEM((D_INNER, N), F32), pltpu.VMEM((L, D_INNER), BF16), pltpu.VMEM((L, D_INNER), BF16)],
        compiler_params=_params("arbitrary"),
    )(xbc, xbc, xbc, dtraw, dtb, A, dexp, states, dy)


BAND_A = (A_PREV + 1) * CHUNK
REL_W = 640


def _relpos_onehot(q):
    u = lax.broadcasted_iota(jnp.int32, (BAND_A, 1), 0)
    idx = jnp.clip(q - u + A_PREV * CHUNK, -MAX_REL, MAX_REL) + MAX_REL
    r = lax.broadcasted_iota(jnp.int32, (1, REL_W), 1)
    return (r == idx).astype(F32)


def relpos_bias(table_pad, *, name):
    def body(t_ref, o_ref):
        oh = _relpos_onehot(pl.program_id(0))
        o_ref[0] = lax.dot_general(t_ref[...], oh, _NT, precision=HI, preferred_element_type=F32)

    return pl.pallas_call(
        body, name=name, grid=(CHUNK,),
        in_specs=[pl.BlockSpec((N_HEADS, REL_W), lambda q: (0, 0))],
        out_specs=pl.BlockSpec((1, N_HEADS, BAND_A), lambda q: (q, 0, 0)),
        out_shape=jax.ShapeDtypeStruct((CHUNK, N_HEADS, BAND_A), F32),
        compiler_params=_params("parallel"),
    )(table_pad)


def relpos_grad(dbias_t, *, name):
    def body(d_ref, o_ref):
        q = pl.program_id(0)

        @pl.when(q == 0)
        def _():
            o_ref[...] = jnp.zeros_like(o_ref)

        o_ref[...] += jnp.dot(d_ref[0], _relpos_onehot(q), precision=HI, preferred_element_type=F32)

    return pl.pallas_call(
        body, name=name, grid=(CHUNK,),
        in_specs=[pl.BlockSpec((1, N_HEADS, BAND_A), lambda q: (q, 0, 0))],
        out_specs=pl.BlockSpec((N_HEADS, REL_W), lambda q: (0, 0)),
        out_shape=jax.ShapeDtypeStruct((N_HEADS, REL_W), F32),
        compiler_params=_params("arbitrary"),
    )(dbias_t)


def loss_head(y, t, *, name, tm=256):
    S, D = y.shape
    tm = min(tm, S)

    def body(y_ref, t_ref, dy_ref, l_ref):
        e = y_ref[...] - t_ref[...]
        dy_ref[...] = e * (1.0 / D)

        @pl.when(pl.program_id(0) == 0)
        def _():
            l_ref[...] = jnp.zeros_like(l_ref)

        part = jnp.sum(jnp.sum(e * e, axis=1, keepdims=True), axis=0, keepdims=True) * (0.5 / D)
        l_ref[...] += jnp.broadcast_to(part, l_ref.shape)

    return pl.pallas_call(
        body, name=name, grid=(S // tm,),
        in_specs=[pl.BlockSpec((tm, D), lambda i: (i, 0))] * 2,
        out_specs=[pl.BlockSpec((tm, D), lambda i: (i, 0)), pl.BlockSpec((1, LANES), lambda i: (0, 0))],
        out_shape=[jax.ShapeDtypeStruct((S, D), F32), jax.ShapeDtypeStruct((1, LANES), F32)],
        compiler_params=_params("arbitrary"),
    )(y, t)


def f_adamw(w, g, m, v):
    m = ADAM_B1 * m + (1.0 - ADAM_B1) * g
    v = ADAM_B2 * v + (1.0 - ADAM_B2) * (g * g)
    m_hat = m / (1.0 - ADAM_B1 ** ADAM_STEP)
    v_hat = v / (1.0 - ADAM_B2 ** ADAM_STEP)
    delta = -ADAM_LR * (m_hat / (jnp.sqrt(v_hat) + ADAM_EPS) + ADAM_WD * w)
    return delta, m, v


def f_norm_id(x, g):
    return f_rmsnorm(x, g)[0], x


ANY = pl.BlockSpec(memory_space=pl.ANY)


def _pos():
    return lax.axis_index("x"), lax.axis_index("y"), lax.axis_index("c")


def _other_chips(x, y):
    return [(1 - x, y), (x, 1 - y), (1 - x, 1 - y)]


def gather_chips(shards, *, name):
    n = len(shards)

    def body(*refs):
        ins, outs = refs[:n], refs[n:2 * n]
        send, recv, fsend, frecv = refs[2 * n:]
        x, y, c = _pos()
        me = 2 * x + y
        sib = (x, y, 1 - c)
        chips = _other_chips(x, y)
        first = []
        for i in range(n):
            for j, (px, py) in enumerate(chips):
                cp = pltpu.make_async_remote_copy(ins[i].at[c], outs[i].at[me, c], send.at[3 * i + j], recv.at[3 * i + j],
                                                  device_id=(px, py, c), device_id_type=MESH)
                cp.start()
                first.append(cp)
        passed = []
        for i in range(n):
            for j, (px, py) in enumerate(chips):
                got = outs[i].at[2 * px + py, c]
                pltpu.make_async_remote_copy(ins[i].at[c], got, send.at[3 * i + j], recv.at[3 * i + j],
                                             device_id=(px, py, c), device_id_type=MESH).wait_recv()
                cp = pltpu.make_async_remote_copy(got, got, fsend.at[3 * i + j], frecv.at[3 * i + j],
                                                  device_id=sib, device_id_type=MESH)
                cp.start()
                passed.append(cp)
        for i in range(n):
            for j, (px, py) in enumerate(chips):
                theirs = outs[i].at[2 * px + py, 1 - c]
                pltpu.make_async_remote_copy(theirs, theirs, fsend.at[3 * i + j], frecv.at[3 * i + j],
                                             device_id=sib, device_id_type=MESH).wait_recv()
        for cp in first + passed:
            cp.wait_send()

    return pl.pallas_call(
        body, name=name, in_specs=[ANY] * n, out_specs=[ANY] * n,
        out_shape=[jax.ShapeDtypeStruct((4,) + s.shape, s.dtype) for s in shards],
        scratch_shapes=[pltpu.SemaphoreType.DMA((3 * n,)), pltpu.SemaphoreType.DMA((3 * n,)),
                        pltpu.SemaphoreType.DMA((3 * n,)), pltpu.SemaphoreType.DMA((3 * n,))],
        compiler_params=pltpu.CompilerParams(has_side_effects=True),
    )(*shards)


def pair_swap_halves(gs, *, name):
    n = len(gs)

    def body(*refs):
        ins, outs = refs[:n], refs[n:2 * n]
        send, recv = refs[2 * n:]
        x, y, c = _pos()
        cps = []
        for i in range(n):
            cp = pltpu.make_async_remote_copy(ins[i].at[1 - c], outs[i], send.at[i], recv.at[i],
                                              device_id=(x, y, 1 - c), device_id_type=MESH)
            cp.start()
            cps.append(cp)
        for cp in cps:
            cp.wait()

    return pl.pallas_call(
        body, name=name, in_specs=[ANY] * n, out_specs=[ANY] * n,
        out_shape=[jax.ShapeDtypeStruct(g.shape[1:], g.dtype) for g in gs],
        scratch_shapes=[pltpu.SemaphoreType.DMA((n,)), pltpu.SemaphoreType.DMA((n,))],
        compiler_params=pltpu.CompilerParams(has_side_effects=True),
    )(*gs)


def scatter_chips(ps, *, name):
    n = len(ps)

    def body(*refs):
        ins, outs = refs[:n], refs[n:2 * n]
        send, recv = refs[2 * n:]
        x, y, c = _pos()
        cps = []
        for i in range(n):
            for j, (px, py) in enumerate(_other_chips(x, y)):
                cp = pltpu.make_async_remote_copy(ins[i].at[2 * px + py], outs[i].at[j], send.at[3 * i + j], recv.at[3 * i + j],
                                                  device_id=(px, py, c), device_id_type=MESH)
                cp.start()
                cps.append(cp)
        for cp in cps:
            cp.wait()

    return pl.pallas_call(
        body, name=name, in_specs=[ANY] * n, out_specs=[ANY] * n,
        out_shape=[jax.ShapeDtypeStruct((3,) + p.shape[1:], p.dtype) for p in ps],
        scratch_shapes=[pltpu.SemaphoreType.DMA((3 * n,)), pltpu.SemaphoreType.DMA((3 * n,))],
        compiler_params=pltpu.CompilerParams(has_side_effects=True),
    )(*ps)


def pair_share(hs, *, name):
    n = len(hs)

    def body(*refs):
        ins, outs = refs[:n], refs[n:2 * n]
        send, recv = refs[2 * n:]
        x, y, c = _pos()
        cps = []
        for i in range(n):
            cp = pltpu.make_async_remote_copy(ins[i], outs[i], send.at[i], recv.at[i],
                                              device_id=(x, y, 1 - c), device_id_type=MESH)
            cp.start()
            cps.append(cp)
        for cp in cps:
            cp.wait()

    return pl.pallas_call(
        body, name=name, in_specs=[ANY] * n, out_specs=[ANY] * n,
        out_shape=[jax.ShapeDtypeStruct(h.shape, h.dtype) for h in hs],
        scratch_shapes=[pltpu.SemaphoreType.DMA((n,)), pltpu.SemaphoreType.DMA((n,))],
        compiler_params=pltpu.CompilerParams(has_side_effects=True),
    )(*hs)


def gather_all(buf, *, name):
    def body(in_ref, out_ref, send, recv, loc):
        x, y, c = _pos()
        lid = 4 * x + 2 * y + c
        lc = pltpu.make_async_copy(in_ref, out_ref.at[lid], loc.at[0])
        lc.start()
        cps = []
        for k in range(1, 8):
            px = 1 - x if k & 4 else x
            py = 1 - y if k & 2 else y
            pc = 1 - c if k & 1 else c
            cp = pltpu.make_async_remote_copy(in_ref, out_ref.at[lid], send.at[k - 1], recv.at[k - 1],
                                              device_id=(px, py, pc), device_id_type=MESH)
            cp.start()
            cps.append((cp, 4 * px + 2 * py + pc, (px, py, pc)))
        for k, (cp, plid, peer) in enumerate(cps):
            cp.wait_send()
            pltpu.make_async_remote_copy(in_ref, out_ref.at[plid], send.at[k], recv.at[k],
                                         device_id=peer, device_id_type=MESH).wait_recv()
        lc.wait()

    return pl.pallas_call(
        body, name=name, in_specs=[ANY], out_specs=ANY,
        out_shape=jax.ShapeDtypeStruct((8,) + buf.shape, buf.dtype),
        scratch_shapes=[pltpu.SemaphoreType.DMA((7,)), pltpu.SemaphoreType.DMA((7,)), pltpu.SemaphoreType.DMA((1,))],
        compiler_params=pltpu.CompilerParams(has_side_effects=True),
    )(buf)


def sum_slots(a, *, name):
    n = a.shape[0]

    def body(a_ref, o_ref):
        acc = a_ref[0]
        for k in range(1, n):
            acc = acc + a_ref[k]
        o_ref[...] = acc

    return pl.pallas_call(body, name=name, out_shape=jax.ShapeDtypeStruct(a.shape[1:], a.dtype),
                          compiler_params=pltpu.CompilerParams(vmem_limit_bytes=VMEM_LIMIT))(a)


def _row_tile(r, want, mult=16):
    t = (min(want, r) // mult) * mult
    while t >= mult:
        if r % t == 0:
            return t
        t -= mult
    return r


def pair_add(g, r1, csel, *, name):
    _, _, r, C = g.shape
    tr = _row_tile(r, 256)

    def body(g_ref, r_ref, c_ref, p32_ref, pb_ref):
        south = c_ref[0:1, 0:1] == 0.0
        p = jnp.where(south, g_ref[0, 0], g_ref[1, 0]) + r_ref[0]
        p32_ref[0] = p
        pb_ref[0] = p.astype(BF16)

    return pl.pallas_call(
        body, name=name, grid=(4, r // tr),
        in_specs=[pl.BlockSpec((2, 1, tr, C), lambda j, t: (0, j, t, 0)), pl.BlockSpec((1, tr, C), lambda j, t: (j, t, 0)),
                  pl.BlockSpec((1, LANES), lambda j, t: (0, 0))],
        out_specs=[pl.BlockSpec((1, tr, C), lambda j, t: (j, t, 0))] * 2,
        out_shape=[jax.ShapeDtypeStruct((4, r, C), F32), jax.ShapeDtypeStruct((4, r, C), BF16)],
        compiler_params=_params("parallel", "parallel"),
    )(g, r1, csel)


def chip_add(p32, r3, msel, *, name):
    _, r, C = p32.shape
    tr = _row_tile(r, 128)

    def body(p_ref, r_ref, m_ref, o_ref):
        me = m_ref[0:1, 0:1]
        acc = jnp.where(me == 0.0, p_ref[0], jnp.where(me == 1.0, p_ref[1], jnp.where(me == 2.0, p_ref[2], p_ref[3])))
        for j in range(3):
            acc = acc + r_ref[j].astype(F32)
        o_ref[...] = acc

    return pl.pallas_call(
        body, name=name, grid=(r // tr,),
        in_specs=[pl.BlockSpec((4, tr, C), lambda t: (0, t, 0)), pl.BlockSpec((3, tr, C), lambda t: (0, t, 0)),
                  pl.BlockSpec((1, LANES), lambda t: (0, 0))],
        out_specs=pl.BlockSpec((tr, C), lambda t: (t, 0)),
        out_shape=jax.ShapeDtypeStruct((r, C), F32),
        compiler_params=_params("parallel"),
    )(p32, r3, msel)


def _consts():
    i512 = np.arange(N_HEADS * HEAD_DIM)
    i128 = np.arange(LANES)
    bd512 = (i512[:, None] // HEAD_DIM == i512[None, :] // HEAD_DIM).astype(np.float32)
    bd128 = (i128[:, None] // HEAD_DIM == i128[None, :] // HEAD_DIM).astype(np.float32)
    fold = (np.arange(HEAD_DIM)[:, None] == (i512[None, :] % HEAD_DIM)).astype(np.float32)
    grp = N_HEADS // 2 * HEAD_DIM
    expand = ((i128[:, None] // HEAD_DIM == i512[None, :] // grp)
              & (i128[:, None] % HEAD_DIM == i512[None, :] % HEAD_DIM)).astype(np.float32)
    band = (B_PREV + 1) * CHUNK
    rel = np.arange(CHUNK)[:, None] - (np.arange(band)[None, :] - B_PREV * CHUNK)
    slopes = 2.0 ** (-8.0 * np.arange(1, N_HEADS + 1, dtype=np.float32) / N_HEADS)
    bias_b = (-slopes[:, None, None] * np.abs(rel).astype(np.float32)[None]).astype(np.float32)
    return [jnp.asarray(a) for a in (bd512, bd128, fold, expand)], jnp.asarray(bias_b)


def _ffn_fwd(xin, l, W, P):
    g = P["norm_ffn"][l:l + 1]
    (h,) = rowwise(f_rmsnorm, [xin], [g], [(D_MODEL, BF16)], name=f"ffn{l}_norm")
    Wi = W["ffn_in"][l]
    gate = matmul(h, Wi[:, :D_FF], mode="nn", name=f"ffn{l}_gate")
    up = matmul(h, Wi[:, D_FF:], mode="nn", name=f"ffn{l}_up")
    gc = dwconv_fwd(gate, P["ffn_conv_w"][l], P["ffn_conv_b"][l:l + 1], name=f"ffn{l}_conv")
    (act,) = rowwise(f_ffn_act, [gc, up], [], [(D_FF, BF16)], name=f"ffn{l}_act")
    xout = matmul(act, W["ffn_out"][l], mode="nn", name=f"ffn{l}_out", residual=xin)
    return xout, (xin, h, gate, gc, up, act)


def _ffn_bwd(dxout, l, saved, W, P):
    xin, h, gate, gc, up, act = saved
    g = P["norm_ffn"][l:l + 1]
    Wi = W["ffn_in"][l]
    dact = matmul(dxout, W["ffn_out"][l], mode="nt", name=f"ffn{l}_dact")
    dWo = matmul(act, dxout, mode="tn", name=f"ffn{l}_dwout")
    dgc, dup = rowwise_vjp(f_ffn_act, [gc, up], [], [dact], [(0, F32), (1, BF16)], [], name=f"ffn{l}_dact_fn")
    dgate, dcw, dcb = dwconv_bwd(gate, dgc, P["ffn_conv_w"][l], name=f"ffn{l}_dconv")
    dh = matmul(dgate, Wi[:, :D_FF], mode="nt", name=f"ffn{l}_dh_gate")
    dh = matmul(dup, Wi[:, D_FF:], mode="nt", name=f"ffn{l}_dh_up", residual=dh)
    dWi = jnp.concatenate([matmul(h, dgate, mode="tn", name=f"ffn{l}_dw_gate"),
                           matmul(h, dup, mode="tn", name=f"ffn{l}_dw_up")], axis=1)
    dxin, dg = rowwise_vjp(f_norm_id, [xin], [g], [dh, dxout], [(0, F32)], [0], name=f"ffn{l}_dnorm")
    return dxin, dWi, dWo, dg, dcw, dcb


def local_step(x, tgt, W, P):
    qk_consts, bias_b = _consts()
    pad_rows = lambda t, n: jnp.pad(t, ((n * CHUNK, 0), (0, 0)))
    DQ = N_HEADS * HEAD_DIM

    g_mix0 = P["norm_mix"][0:1]
    (h0,) = rowwise(f_rmsnorm, [x], [g_mix0], [(D_MODEL, BF16)], name="attn_norm")
    qkv = matmul(h0, W["attn_in"], mode="nn", name="attn_qkv")
    qk_par = [P["q_norm_a"], P["k_norm_a"], P["q_norm_b"], P["k_norm_b"]] + qk_consts
    qa, ka, va, qb, kb, vb = rowwise(f_qknorm, [qkv], qk_par, [(DQ, BF16)] * 6, name="attn_qknorm")
    ka, va, kb, vb = pad_rows(ka, A_PREV), pad_rows(va, A_PREV), pad_rows(kb, B_PREV), pad_rows(vb, B_PREV)
    table = jnp.pad(P["relpos_table"], ((0, 0), (0, REL_W - (2 * MAX_REL + 1))))
    nj = min(ATT_TQ, x.shape[0]) // CHUNK
    bias_a = widen_bias(jnp.transpose(relpos_bias(table, name="relpos_bias"), (1, 0, 2)), A_PREV, nj)
    bias_b = widen_bias(bias_b, B_PREV, nj)
    sinks = jnp.broadcast_to(P["sinks"].reshape(N_HEADS // 2, 2, 1), (N_HEADS // 2, 2, LANES))
    oa = attn_fwd(qa, ka, va, bias_a, None, n_prev=A_PREV, name="attn_a")
    ob = attn_fwd(qb, kb, vb, bias_b, sinks, n_prev=B_PREV, name="attn_b")
    Wao = W["attn_out"]
    x1 = matmul(oa, Wao[:DQ], mode="nn", name="attn_out_a", residual=x)
    x1 = matmul(ob, Wao[DQ:], mode="nn", name="attn_out_b", residual=x1)
    x2, ffn0 = _ffn_fwd(x1, 0, W, P)

    g_mix1 = P["norm_mix"][1:2]
    (h2,) = rowwise(f_rmsnorm, [x2], [g_mix1], [(D_MODEL, BF16)], name="ssm_norm_in")
    Ws = W["ssm_in"]
    CC = D_INNER + 2 * SSM_GROUPS * SSM_STATE
    Wz, Wx = Ws[:, :D_INNER], Ws[:, D_INNER:D_INNER + CC]
    Wdt = jnp.pad(Ws[:, D_INNER + CC:], ((0, 0), (0, LANES - SSM_HEADS)))
    z = matmul(h2, Wz, mode="nn", name="ssm_z")
    xr = matmul(h2, Wx, mode="nn", name="ssm_xbc")
    dtraw = matmul(h2, Wdt, mode="nn", name="ssm_dt")
    xc = dwconv_fwd(xr, P["ssm_conv_w"], P["ssm_conv_b"], name="ssm_conv")
    (xbc,) = rowwise(f_silu, [xc], [], [(CC, F32)], name="ssm_silu")
    pad32 = lambda v: jnp.pad(v, ((0, 0), (0, LANES - SSM_HEADS)))
    A = pad32(-jnp.exp(P["ssm_a_log"]))
    dtb = pad32(P["ssm_dt_bias"])
    dexp = jnp.repeat(P["ssm_d"], D_INNER // SSM_HEADS, axis=1)
    y, states = ssd_fwd(xbc, dtraw, dtb, A, dexp, name="ssd_fwd")
    (y2,) = rowwise(f_gate_norm, [y, z], [P["ssm_norm"]], [(D_INNER, BF16)], name="ssm_gate_norm")
    x3 = matmul(y2, W["ssm_out"], mode="nn", name="ssm_out", residual=x2)
    x4, ffn1 = _ffn_fwd(x3, 1, W, P)

    dx4, lpart = loss_head(x4, tgt, name="loss_head")

    dx3, dWfi1, dWfo1, dgf1, dfcw1, dfcb1 = _ffn_bwd(dx4, 1, ffn1, W, P)
    dy2 = matmul(dx3, W["ssm_out"], mode="nt", name="ssm_dy")
    dWso = matmul(y2, dx3, mode="tn", name="ssm_dwout")
    dy, dz, dnw = rowwise_vjp(f_gate_norm, [y, z], [P["ssm_norm"]], [dy2], [(0, F32), (1, F32)], [0],
                              name="ssm_dgate_norm")
    dxs, dB, dC, ddtraw, dA, ddtb, dDl = ssd_bwd(xbc, dtraw, dtb, A, dexp, states, dy, name="ssd_bwd")
    dxbc = jnp.concatenate([dxs, dB, dC], axis=1)
    (dxc,) = rowwise_vjp(f_silu, [xc], [], [dxbc], [(0, F32)], [], name="ssm_dsilu")
    dxr, dscw, dscb = dwconv_bwd(xr, dxc, P["ssm_conv_w"], name="ssm_dconv")
    dh2 = matmul(dz, Wz, mode="nt", name="ssm_dh_z")
    dh2 = matmul(dxr, Wx, mode="nt", name="ssm_dh_x", residual=dh2)
    dh2 = matmul(ddtraw, Wdt, mode="nt", name="ssm_dh_dt", residual=dh2)
    dWs = jnp.concatenate([matmul(h2, dz, mode="tn", name="ssm_dw_z"),
                           matmul(h2, dxr, mode="tn", name="ssm_dw_x"),
                           matmul(h2, ddtraw, mode="tn", name="ssm_dw_dt")[:, :SSM_HEADS]], axis=1)
    dx2, dgm1 = rowwise_vjp(f_norm_id, [x2], [g_mix1], [dh2, dx3], [(0, F32)], [0], name="ssm_dnorm_in")

    dx1, dWfi0, dWfo0, dgf0, dfcw0, dfcb0 = _ffn_bwd(dx2, 0, ffn0, W, P)
    doa = matmul(dx1, Wao[:DQ], mode="nt", name="attn_do_a", out_dtype=BF16)
    dob = matmul(dx1, Wao[DQ:], mode="nt", name="attn_do_b", out_dtype=BF16)
    dWao = jnp.concatenate([matmul(oa, dx1, mode="tn", name="attn_dwout_a"),
                            matmul(ob, dx1, mode="tn", name="attn_dwout_b")], axis=0)
    dqa, dka, dva, dbias_a = attn_bwd(qa, ka, va, doa, bias_a, None, n_prev=A_PREV, name="attn_a_bwd")
    dqb, dkb, dvb, _, dsk = attn_bwd(qb, kb, vb, dob, bias_b, sinks, n_prev=B_PREV, name="attn_b_bwd")
    pa, pb = A_PREV * CHUNK, B_PREV * CHUNK
    cots = [dqa, dka[pa:], dva[pa:], dqb, dkb[pb:], dvb[pb:]]
    dqkv, dgqa, dgka, dgqb, dgkb = rowwise_vjp(f_qknorm, [qkv], qk_par, cots, [(0, BF16)], [0, 1, 2, 3],
                                               name="attn_dqknorm")
    dh0 = matmul(dqkv, W["attn_in"], mode="nt", name="attn_dh")
    dWai = matmul(h0, dqkv, mode="tn", name="attn_dwin")
    dx, dgm0 = rowwise_vjp(f_norm_id, [x], [g_mix0], [dh0, dx1], [(0, F32)], [0], name="attn_dnorm")
    dbias_a = fold_bias(dbias_a, A_PREV, nj)
    dtable = relpos_grad(jnp.transpose(dbias_a, (1, 0, 2)), name="relpos_grad")[:, :2 * MAX_REL + 1]

    gW = {"attn_in": dWai, "attn_out": dWao, "ssm_in": dWs, "ssm_out": dWso,
          "ffn_in": [dWfi0, dWfi1], "ffn_out": [dWfo0, dWfo1]}
    gP = {"norm_mix": jnp.concatenate([dgm0, dgm1], axis=0),
          "norm_ffn": jnp.concatenate([dgf0, dgf1], axis=0),
          "relpos_table": dtable, "q_norm_a": dgqa, "k_norm_a": dgka, "q_norm_b": dgqb, "k_norm_b": dgkb,
          "sinks": dsk[:, :, 0].reshape(1, N_HEADS),
          "ssm_conv_w": dscw, "ssm_conv_b": dscb,
          "ssm_dt_bias": ddtb[:, :SSM_HEADS], "ssm_a_log": dA[:, :SSM_HEADS] * A[:, :SSM_HEADS],
          "ssm_d": dDl.reshape(SSM_HEADS, D_INNER // SSM_HEADS).sum(axis=1).reshape(1, SSM_HEADS),
          "ssm_norm": dnw,
          "ffn_conv_w": jnp.stack([dfcw0, dfcw1]), "ffn_conv_b": jnp.concatenate([dfcb0, dfcb1], axis=0)}
    return lpart, dx, gW, gP


WEIGHTS = ["norm_mix", "norm_ffn", "attn_w_in", "attn_w_out", "relpos_table", "q_norm_a", "k_norm_a", "q_norm_b",
           "k_norm_b", "sinks", "ssm_w_in", "ssm_conv_w", "ssm_conv_b", "ssm_dt_bias", "ssm_a_log", "ssm_d",
           "ssm_norm", "ssm_w_out", "ffn_w_in", "ffn_conv_w", "ffn_conv_b", "ffn_w_out"]
ARGS = ["x"] + WEIGHTS + ["loss_target"] + ["m_" + w for w in WEIGHTS] + ["v_" + w for w in WEIGHTS]
N_CHIPS = 4
SMALL_ROWS = 384
SMALL_ORDER = ["norm_mix", "norm_ffn", "relpos_table", "q_norm_a", "k_norm_a", "q_norm_b", "k_norm_b", "sinks",
               "ssm_dt_bias", "ssm_a_log", "ssm_d", "ffn_conv_b", "ssm_conv_w", "ssm_conv_b", "ssm_norm", "ffn_conv_w"]


def _cols_to_slabs(g):
    K, N = g.shape
    return g.reshape(2, K // 2, N_CHIPS, N // N_CHIPS).transpose(0, 2, 1, 3)


def _rows_to_slabs(g):
    R, C = g.shape
    return g.reshape(N_CHIPS, 2, R // (2 * N_CHIPS), C).transpose(1, 0, 2, 3)


def _adamw(w, g, m, v, name):
    shp = w.shape
    two = lambda a: a.reshape((-1, shp[-1]))
    outs = [(shp[-1], F32)] * 3
    d, nm, nv = rowwise(f_adamw, [two(w), two(g), two(m), two(v)], [], outs, name="adamw_" + name)
    return d.reshape(shp), nm.reshape(shp), nv.reshape(shp)


def kernel(x, norm_mix, norm_ffn, attn_w_in, attn_w_out, relpos_table, q_norm_a, k_norm_a, q_norm_b, k_norm_b, sinks, ssm_w_in, ssm_conv_w, ssm_conv_b, ssm_dt_bias, ssm_a_log, ssm_d, ssm_norm, ssm_w_out, ffn_w_in, ffn_conv_w, ffn_conv_b, ffn_w_out, loss_target, m_norm_mix, m_norm_ffn, m_attn_w_in, m_attn_w_out, m_relpos_table, m_q_norm_a, m_k_norm_a, m_q_norm_b, m_k_norm_b, m_sinks, m_ssm_w_in, m_ssm_conv_w, m_ssm_conv_b, m_ssm_dt_bias, m_ssm_a_log, m_ssm_d, m_ssm_norm, m_ssm_w_out, m_ffn_w_in, m_ffn_conv_w, m_ffn_conv_b, m_ffn_w_out, v_norm_mix, v_norm_ffn, v_attn_w_in, v_attn_w_out, v_relpos_table, v_q_norm_a, v_k_norm_a, v_q_norm_b, v_k_norm_b, v_sinks, v_ssm_w_in, v_ssm_conv_w, v_ssm_conv_b, v_ssm_dt_bias, v_ssm_a_log, v_ssm_d, v_ssm_norm, v_ssm_w_out, v_ffn_w_in, v_ffn_conv_w, v_ffn_conv_b, v_ffn_w_out):
    d = dict(zip(ARGS, (x, norm_mix, norm_ffn, attn_w_in, attn_w_out, relpos_table, q_norm_a, k_norm_a, q_norm_b, k_norm_b, sinks, ssm_w_in, ssm_conv_w, ssm_conv_b, ssm_dt_bias, ssm_a_log, ssm_d, ssm_norm, ssm_w_out, ffn_w_in, ffn_conv_w, ffn_conv_b, ffn_w_out, loss_target, m_norm_mix, m_norm_ffn, m_attn_w_in, m_attn_w_out, m_relpos_table, m_q_norm_a, m_k_norm_a, m_q_norm_b, m_k_norm_b, m_sinks, m_ssm_w_in, m_ssm_conv_w, m_ssm_conv_b, m_ssm_dt_bias, m_ssm_a_log, m_ssm_d, m_ssm_norm, m_ssm_w_out, m_ffn_w_in, m_ffn_conv_w, m_ffn_conv_b, m_ffn_w_out, v_norm_mix, v_norm_ffn, v_attn_w_in, v_attn_w_out, v_relpos_table, v_q_norm_a, v_k_norm_a, v_q_norm_b, v_k_norm_b, v_sinks, v_ssm_w_in, v_ssm_conv_w, v_ssm_conv_b, v_ssm_dt_bias, v_ssm_a_log, v_ssm_d, v_ssm_norm, v_ssm_w_out, v_ffn_w_in, v_ffn_conv_w, v_ffn_conv_b, v_ffn_w_out)))
    xi, yi, ci = _pos()
    me = 2 * xi + yi
    csel = jnp.full((1, LANES), ci, F32)
    msel = jnp.full((1, LANES), me, F32)

    halves = lambda w: w.reshape((2, -1, w.shape[-1]))
    small_sh = jnp.concatenate([d["ssm_conv_w"].reshape(-1), d["ssm_conv_b"].reshape(-1), d["ssm_norm"].reshape(-1),
                                d["ffn_conv_w"].reshape(-1)])
    n_small = small_sh.shape[0]
    small_sh = jnp.pad(small_sh, (0, 2 * 40 * LANES - n_small)).reshape(2, 40, LANES)
    shards = [halves(d["attn_w_in"][0].astype(BF16)), halves(d["attn_w_out"][0].astype(BF16)),
              halves(d["ssm_w_in"][0].astype(BF16)), halves(d["ssm_w_out"][0].astype(BF16)),
              d["ffn_w_in"].astype(BF16), d["ffn_w_out"].astype(BF16), small_sh]
    gathered = gather_chips(shards, name="gather_weights")
    g_ai, g_ao, g_si, g_so, g_fi, g_fo, g_sm = [lax.dynamic_update_slice_in_dim(g, s[None], me, axis=0)
                                                for g, s in zip(gathered, shards)]
    cat_cols = lambda g: jnp.concatenate([g[j].reshape((-1, g.shape[-1])) for j in range(N_CHIPS)], axis=1)
    W = {"attn_in": cat_cols(g_ai), "attn_out": g_ao.reshape(-1, D_MODEL),
         "ssm_in": cat_cols(g_si), "ssm_out": g_so.reshape(-1, D_MODEL),
         "ffn_in": [jnp.concatenate([g_fi[j, l] for j in range(N_CHIPS)], axis=1) for l in range(2)],
         "ffn_out": [g_fo[:, l].reshape(-1, D_MODEL) for l in range(2)]}
    sm = g_sm.reshape(N_CHIPS, -1)
    CC = D_INNER + 2 * SSM_GROUPS * SSM_STATE
    c4, f4 = CC // N_CHIPS, D_FF // N_CHIPS
    o1 = SSM_CONV * c4
    o2 = o1 + c4
    o3 = o2 + D_INNER // N_CHIPS
    o4 = o3 + 2 * FFN_CONV * f4
    P = {k: d[k] for k in ["norm_mix", "norm_ffn", "q_norm_a", "k_norm_a", "q_norm_b", "k_norm_b", "sinks",
                           "ssm_dt_bias", "ssm_a_log", "ssm_d", "ffn_conv_b"]}
    P["relpos_table"] = d["relpos_table"][0]
    P["ssm_conv_w"] = sm[:, :o1].reshape(N_CHIPS, SSM_CONV, c4).transpose(1, 0, 2).reshape(SSM_CONV, CC)
    P["ssm_conv_b"] = sm[:, o1:o2].reshape(1, CC)
    P["ssm_norm"] = sm[:, o2:o3].reshape(1, D_INNER)
    P["ffn_conv_w"] = sm[:, o3:o4].reshape(N_CHIPS, 2, FFN_CONV, f4).transpose(1, 2, 0, 3).reshape(2, FFN_CONV, D_FF)

    lpart, dx, gW, gP = local_step(d["x"][0], d["loss_target"][0], W, P)
    loss = lax.psum(lpart[0, 0], ("x", "y", "c"))

    slabs = [_cols_to_slabs(gW["attn_in"]), _rows_to_slabs(gW["attn_out"]), _cols_to_slabs(gW["ssm_in"]),
             _rows_to_slabs(gW["ssm_out"]), _cols_to_slabs(gW["ffn_in"][0]), _cols_to_slabs(gW["ffn_in"][1]),
             _rows_to_slabs(gW["ffn_out"][0]), _rows_to_slabs(gW["ffn_out"][1])]
    tags = ["ai", "ao", "si", "so", "fi0", "fi1", "fo0", "fo1"]
    from_sib = pair_swap_halves(slabs, name="grad_pair_swap")
    pairs = [pair_add(g, r, csel, name="grad_pair_add_" + t) for g, r, t in zip(slabs, from_sib, tags)]
    from_chips = scatter_chips([p[1] for p in pairs], name="grad_scatter")
    mine = [chip_add(p[0], r, msel, name="grad_chip_add_" + t) for p, r, t in zip(pairs, from_chips, tags)]
    theirs = pair_share(mine, name="grad_pair_share")
    full = [jnp.where(ci == 0, jnp.stack([a, b]), jnp.stack([b, a])) for a, b in zip(mine, theirs)]
    two = lambda a: a.reshape((-1, a.shape[-1]))
    grads = {"attn_w_in": two(full[0])[None], "attn_w_out": two(full[1])[None],
             "ssm_w_in": two(full[2])[None], "ssm_w_out": two(full[3])[None],
             "ffn_w_in": jnp.stack([two(full[4]), two(full[5])]),
             "ffn_w_out": jnp.stack([two(full[6]), two(full[7])])}

    flat = jnp.concatenate([gP[k].reshape(-1) for k in SMALL_ORDER])
    flat = jnp.pad(flat, (0, SMALL_ROWS * LANES - flat.shape[0])).reshape(SMALL_ROWS, LANES)
    tot = sum_slots(gather_all(flat, name="small_gather"), name="small_sum").reshape(-1)
    off = 0
    for k in SMALL_ORDER:
        n = int(np.prod(gP[k].shape))
        g = tot[off:off + n].reshape(gP[k].shape)
        off += n
        if k == "ssm_conv_w":
            g = lax.dynamic_slice_in_dim(g, me * c4, c4, axis=1)[None]
        elif k == "ssm_conv_b":
            g = lax.dynamic_slice_in_dim(g, me * c4, c4, axis=1)
        elif k == "ssm_norm":
            g = lax.dynamic_slice_in_dim(g, me * (D_INNER // N_CHIPS), D_INNER // N_CHIPS, axis=1)
        elif k == "ffn_conv_w":
            g = lax.dynamic_slice_in_dim(g, me * f4, f4, axis=2)
        elif k == "relpos_table":
            g = g[None]
        grads[k] = g

    deltas, new_m, new_v = {}, {}, {}
    for k in WEIGHTS:
        deltas[k], new_m[k], new_v[k] = _adamw(d[k], grads[k], d["m_" + k], d["v_" + k], k)
    return (loss, dx[None], *[grads[k] for k in WEIGHTS], *[deltas[k] for k in WEIGHTS],
            *[new_m[k] for k in WEIGHTS], *[new_v[k] for k in WEIGHTS])
```

```python
import functools

import numpy as np
import jax
import jax.numpy as jnp
from jax import lax
from jax.experimental import pallas as pl
from jax.experimental.pallas import tpu as pltpu

F32 = jnp.float32
BF16 = jnp.bfloat16
HI = lax.Precision.HIGHEST

D_MODEL = 1024
CHUNK = 64
EPS = 1e-6
HEAD_DIM = 64
N_HEADS = 8
A_PREV = 8
B_PREV = 2
MAX_REL = 256
D_INNER = 2048
SSM_HEADS = 32
SSM_GROUPS = 4
SSM_STATE = 128
SSM_CONV = 4
D_FF = 2816
FFN_CONV = 3
LANES = 128
SUBLANES = 8
VMEM_LIMIT = 56 * 1024 * 1024
SSD_L = 128

ADAM_LR = 0.001
ADAM_B1 = 0.9
ADAM_B2 = 0.999
ADAM_EPS = 1e-08
ADAM_WD = 0.01
ADAM_STEP = 10

MESH = pl.DeviceIdType.MESH


def _params(*sem):
    return pltpu.CompilerParams(dimension_semantics=sem, vmem_limit_bytes=VMEM_LIMIT)


def _pick(n, want):
    if n <= want:
        return n
    t = (want // LANES) * LANES
    while t >= LANES:
        if n % t == 0:
            return t
        t -= LANES
    return n


MM_ROWS = 512
MM_COLS = 1536
MM_RED = 512


def matmul(a, b, *, mode, name, out_dtype=F32, residual=None):
    dims = {"nn": (((1,), (0,)), ((), ())), "nt": (((1,), (1,)), ((), ())), "tn": (((0,), (0,)), ((), ()))}[mode]
    if mode == "tn":
        assert residual is None and out_dtype == F32
        (K, M), (K2, N) = a.shape, b.shape
        assert K == K2, (a.shape, b.shape)
        tm, tn, tk = _pick(M, MM_COLS), _pick(N, MM_COLS), _pick(K, MM_RED)

        def body(a_ref, b_ref, o_ref):
            k = pl.program_id(2)
            p = lax.dot_general(a_ref[...].astype(BF16), b_ref[...].astype(BF16), dims, preferred_element_type=F32)

            @pl.when(k == 0)
            def _():
                o_ref[...] = p

            @pl.when(k != 0)
            def _():
                o_ref[...] += p

        return pl.pallas_call(
            body, name=name, grid=(M // tm, N // tn, K // tk),
            in_specs=[pl.BlockSpec((tk, tm), lambda i, j, k: (k, i)), pl.BlockSpec((tk, tn), lambda i, j, k: (k, j))],
            out_specs=pl.BlockSpec((tm, tn), lambda i, j, k: (i, j)),
            out_shape=jax.ShapeDtypeStruct((M, N), F32),
            compiler_params=_params("parallel", "parallel", "arbitrary"),
        )(a, b)

    if mode == "nn":
        (M, K), (K2, N) = a.shape, b.shape
    else:
        (M, K), (N, K2) = a.shape, b.shape
    assert K == K2, (a.shape, b.shape, mode)
    tm, tn = _pick(M, MM_ROWS), _pick(N, MM_COLS)

    def body(*refs):
        a_ref, b_ref = refs[:2]
        o_ref = refs[-1]
        r = lax.dot_general(a_ref[...].astype(BF16), b_ref[...].astype(BF16), dims, preferred_element_type=F32)
        if residual is not None:
            r = r + refs[2][...].astype(F32)
        o_ref[...] = r.astype(o_ref.dtype)

    a_spec = pl.BlockSpec((tm, K), lambda j, i: (i, 0))
    b_spec = pl.BlockSpec((K, tn), lambda j, i: (0, j)) if mode == "nn" else pl.BlockSpec((tn, K), lambda j, i: (j, 0))
    o_spec = pl.BlockSpec((tm, tn), lambda j, i: (i, j))
    in_specs = [a_spec, b_spec] + ([o_spec] if residual is not None else [])
    args = (a, b) + ((residual,) if residual is not None else ())
    return pl.pallas_call(
        body, name=name, grid=(N // tn, M // tm),
        in_specs=in_specs, out_specs=o_spec,
        out_shape=jax.ShapeDtypeStruct((M, N), out_dtype),
        compiler_params=_params("parallel", "parallel"),
    )(*args)


def rowwise(f, rows, params, outs, *, name, tm=256):
    S = rows[0].shape[0]
    tm = _row_tile(S, tm)
    nr, npar = len(rows), len(params)

    def body(*refs):
        vals = [r[...] for r in refs[:nr + npar]]
        res = f(*vals)
        for o_ref, r in zip(refs[nr + npar:], res):
            o_ref[...] = r.astype(o_ref.dtype)

    in_specs = [pl.BlockSpec((tm, r.shape[1]), lambda i: (i, 0)) for r in rows]
    in_specs += [pl.BlockSpec(p.shape, lambda i: (0, 0)) for p in params]
    out_specs = [pl.BlockSpec((tm, c), lambda i: (i, 0)) for c, _ in outs]
    out_shape = [jax.ShapeDtypeStruct((S, c), dt) for c, dt in outs]
    return pl.pallas_call(body, name=name, grid=(S // tm,), in_specs=in_specs, out_specs=out_specs,
                          out_shape=out_shape, compiler_params=_params("parallel"))(*rows, *params)


def rowwise_vjp(f, rows, params, cots, drow, dpar, *, name, tm=256):
    S = rows[0].shape[0]
    tm = _row_tile(S, tm)
    nr, npar, nc = len(rows), len(params), len(cots)

    def body(*refs):
        vals = [r[...] for r in refs[:nr + npar]]
        cvals = [r[...].astype(F32) for r in refs[nr + npar:nr + npar + nc]]
        o_refs = refs[nr + npar + nc:]
        want = [ri for ri, _ in drow] + [nr + pi for pi in dpar]

        def f_want(*d):
            full = list(vals)
            for k, v in zip(want, d):
                full[k] = v
            return f(*full)

        _, vjp = jax.vjp(f_want, *[vals[k] for k in want])
        grads = vjp(tuple(cvals))
        for o_ref, g in zip(o_refs[:len(drow)], grads):
            o_ref[...] = g.astype(o_ref.dtype)
        first = pl.program_id(0) == 0
        for o_ref, g in zip(o_refs[len(drow):], grads[len(drow):]):
            g = g.astype(F32)

            @pl.when(first)
            def _(o_ref=o_ref, g=g):
                o_ref[...] = g

            @pl.when(jnp.logical_not(first))
            def _(o_ref=o_ref, g=g):
                o_ref[...] += g

    in_specs = [pl.BlockSpec((tm, r.shape[1]), lambda i: (i, 0)) for r in rows]
    in_specs += [pl.BlockSpec(p.shape, lambda i: (0, 0)) for p in params]
    in_specs += [pl.BlockSpec((tm, c.shape[1]), lambda i: (i, 0)) for c in cots]
    out_specs = [pl.BlockSpec((tm, rows[ri].shape[1]), lambda i: (i, 0)) for ri, _ in drow]
    out_specs += [pl.BlockSpec(params[pi].shape, lambda i: (0, 0)) for pi in dpar]
    out_shape = [jax.ShapeDtypeStruct(rows[ri].shape, dt) for ri, dt in drow]
    out_shape += [jax.ShapeDtypeStruct(params[pi].shape, F32) for pi in dpar]
    return pl.pallas_call(body, name=name, grid=(S // tm,), in_specs=in_specs, out_specs=out_specs,
                          out_shape=out_shape, compiler_params=_params("arbitrary"))(*rows, *params, *cots)


HALO = SUBLANES


def dwconv_fwd(x, w, b, post, extra, outs, *, name, tm=256):
    S, C = x.shape
    K = w.shape[0]
    tm = min(tm, S)
    hb = tm // HALO
    ne = len(extra)

    def body(*refs):
        x_ref, halo_ref, w_ref, b_ref = refs[:4]
        e_refs = refs[4:4 + ne]
        o_refs = refs[4 + ne:4 + ne + len(outs)]
        buf = refs[-1]
        i = pl.program_id(0)
        buf[0:HALO, :] = jnp.where(i == 0, 0.0, halo_ref[...])
        buf[HALO:HALO + tm, :] = x_ref[...]
        acc = jnp.broadcast_to(b_ref[...], (tm, C))
        for k in range(K):
            acc = acc + w_ref[k:k + 1, :] * buf[pl.ds(HALO - (K - 1) + k, tm), :]
        for o_ref, r in zip(o_refs, post(acc, *[e[...] for e in e_refs])):
            o_ref[...] = r.astype(o_ref.dtype)

    row = pl.BlockSpec((tm, C), lambda i: (i, 0))
    return pl.pallas_call(
        body, name=name, grid=(S // tm,),
        in_specs=[row,
                  pl.BlockSpec((HALO, C), lambda i: (jnp.maximum(i * hb - 1, 0), 0)),
                  pl.BlockSpec((K, C), lambda i: (0, 0)),
                  pl.BlockSpec((1, C), lambda i: (0, 0))] + [row] * ne,
        out_specs=[row] * len(outs),
        out_shape=[jax.ShapeDtypeStruct((S, C), dt) for dt in outs],
        scratch_shapes=[pltpu.VMEM((HALO + tm, C), F32)],
        compiler_params=_params("parallel"),
    )(x, x, w, b, *extra)


def dwconv_bwd(x, w, srcs, dy_fn, extra_outs, *, name, tm=256):
    S, C = x.shape
    K = w.shape[0]
    tm = min(tm, S)
    hb = tm // HALO
    n = S // tm
    ns = len(srcs)

    def body(*refs):
        x_ref, xh_ref, w_ref = refs[:3]
        s_refs = refs[3:3 + ns]
        sh_refs = refs[3 + ns:3 + 2 * ns]
        dx_ref, dw_ref, db_ref = refs[3 + 2 * ns:6 + 2 * ns]
        e_refs = refs[6 + 2 * ns:6 + 2 * ns + len(extra_outs)]
        bx, bd = refs[-2:]
        i = pl.program_id(0)
        bx[0:HALO, :] = jnp.where(i == 0, 0.0, xh_ref[...])
        bx[HALO:HALO + tm, :] = x_ref[...]
        res = dy_fn(*[r[...].astype(F32) for r in s_refs])
        dyv = res[0]
        for e_ref, r in zip(e_refs, res[1:]):
            e_ref[...] = r.astype(e_ref.dtype)
        bd[0:tm, :] = dyv
        bd[tm:tm + HALO, :] = jnp.where(i == n - 1, 0.0, dy_fn(*[r[...].astype(F32) for r in sh_refs])[0])
        acc = jnp.zeros((tm, C), F32)
        for k in range(K):
            acc = acc + w_ref[k:k + 1, :] * bd[pl.ds((K - 1) - k, tm), :]
        dx_ref[...] = acc

        @pl.when(i == 0)
        def _():
            dw_ref[...] = jnp.zeros_like(dw_ref)
            db_ref[...] = jnp.zeros_like(db_ref)

        for k in range(K):
            dw_ref[k:k + 1, :] += jnp.sum(dyv * bx[pl.ds(HALO - (K - 1) + k, tm), :], axis=0, keepdims=True)
        db_ref[...] += jnp.sum(dyv, axis=0, keepdims=True)

    row = lambda c: pl.BlockSpec((tm, c), lambda i: (i, 0))
    nxt = lambda c: pl.BlockSpec((HALO, c), lambda i: (jnp.minimum((i + 1) * hb, S // HALO - 1), 0))
    return pl.pallas_call(
        body, name=name, grid=(n,),
        in_specs=[row(C), pl.BlockSpec((HALO, C), lambda i: (jnp.maximum(i * hb - 1, 0), 0)),
                  pl.BlockSpec((K, C), lambda i: (0, 0))]
                 + [row(s.shape[1]) for s in srcs] + [nxt(s.shape[1]) for s in srcs],
        out_specs=[row(C), pl.BlockSpec((K, C), lambda i: (0, 0)), pl.BlockSpec((1, C), lambda i: (0, 0))]
                  + [row(C)] * len(extra_outs),
        out_shape=[jax.ShapeDtypeStruct((S, C), F32), jax.ShapeDtypeStruct((K, C), F32),
                   jax.ShapeDtypeStruct((1, C), F32)] + [jax.ShapeDtypeStruct((S, C), dt) for dt in extra_outs],
        scratch_shapes=[pltpu.VMEM((HALO + tm, C), F32), pltpu.VMEM((tm + HALO, C), F32)],
        compiler_params=_params("arbitrary"),
    )(x, x, w, *srcs, *srcs)


def _sigmoid(x):
    return 0.5 * jnp.tanh(0.5 * x) + 0.5


def _silu(x):
    return x * _sigmoid(x)


def _dsilu(x):
    s = _sigmoid(x)
    return s * (1.0 + x * (1.0 - s))


def f_rmsnorm(x, g):
    return (x * lax.rsqrt(jnp.mean(x * x, axis=-1, keepdims=True) + EPS) * g,)


SEL = lax.Precision.HIGH


def _group_norm(x, bd, width):
    ms = jnp.dot(x * x, bd, precision=SEL, preferred_element_type=F32) * (1.0 / width)
    return x * lax.rsqrt(ms + EPS)


def f_qknorm(qkv, gqa, gka, gqb, gkb, bd512, bd128, fold, expand):
    dq = N_HEADS * HEAD_DIM
    qa, ka, va, qb = (qkv[:, i * dq:(i + 1) * dq] for i in range(4))
    kb = qkv[:, 4 * dq:4 * dq + LANES]
    vb = qkv[:, 4 * dq + LANES:4 * dq + 2 * LANES]
    tile8 = lambda g: jnp.dot(g, fold, precision=HI, preferred_element_type=F32)
    qa = _group_norm(qa, bd512, HEAD_DIM) * tile8(gqa)
    ka = _group_norm(ka, bd512, HEAD_DIM) * tile8(gka)
    qb = _group_norm(qb, bd512, HEAD_DIM) * tile8(gqb)
    kb = _group_norm(kb, bd128, HEAD_DIM) * tile8(gkb)[:, :LANES]
    kb = jnp.dot(kb, expand, precision=SEL, preferred_element_type=F32)
    vb = jnp.dot(vb, expand, precision=SEL, preferred_element_type=F32)
    return qa, ka, va, qb, kb, vb


def f_gate_norm(y, z, nw):
    v = y * _silu(z)
    gw = D_INNER // SSM_GROUPS
    parts = []
    for g in range(SSM_GROUPS):
        vg = v[:, g * gw:(g + 1) * gw]
        parts.append(vg * lax.rsqrt(jnp.mean(vg * vg, axis=-1, keepdims=True) + EPS))
    return (jnp.concatenate(parts, axis=-1) * nw,)


ATT_TQ = 256
_NT = (((1,), (1,)), ((), ()))
_TN = (((0,), (0,)), ((), ()))


def _attn_probs(qh, kb, bias, valid, snk):
    s = lax.dot_general(qh, kb, _NT, preferred_element_type=F32) * (HEAD_DIM ** -0.5) + bias
    s = jnp.where(valid, s, -jnp.inf)
    m = jnp.max(s, axis=1, keepdims=True)
    if snk is not None:
        m = jnp.maximum(m, snk)
    e = jnp.exp(s - m)
    den = jnp.sum(e, axis=1, keepdims=True)
    if snk is None:
        return e / den, None
    es = jnp.exp(snk - m)
    den = den + es
    return e / den, es / den


def widen_bias(bias, n_prev, nj):
    band = (n_prev + 1) * CHUNK
    wk = (nj + n_prev) * CHUNK
    rows = [jnp.pad(bias, ((0, 0), (0, 0), (j * CHUNK, wk - band - j * CHUNK)), constant_values=-jnp.inf)
            for j in range(nj)]
    return jnp.concatenate(rows, axis=1)


def fold_bias(dbw, n_prev, nj):
    band = (n_prev + 1) * CHUNK
    acc = dbw[:, :CHUNK, :band]
    for j in range(1, nj):
        acc = acc + dbw[:, j * CHUNK:(j + 1) * CHUNK, j * CHUNK:j * CHUNK + band]
    return acc


def attn_fwd(q, k, v, bias_w, sinks, *, n_prev, name):
    S = q.shape[0]
    pad = n_prev * CHUNK
    tq = min(ATT_TQ, S)
    wk = tq + pad
    assert bias_w.shape == (N_HEADS, tq, wk), bias_w.shape
    has_sink = sinks is not None

    def body(*refs):
        if has_sink:
            q_ref, k_ref, v_ref, bias_ref, sink_ref, o_ref = refs
        else:
            q_ref, k_ref, v_ref, bias_ref, o_ref = refs
        start = pl.multiple_of(pl.program_id(1) * tq, tq)
        head0 = lax.broadcasted_iota(jnp.int32, (1, LANES), 1) < HEAD_DIM
        valid = lax.broadcasted_iota(jnp.int32, (1, wk), 1) + start >= pad
        qp = q_ref[...].astype(F32)
        kb = k_ref[pl.ds(start, wk), :]
        vb = v_ref[pl.ds(start, wk), :]
        outs = []
        for r in range(2):
            mh = head0 if r == 0 else jnp.logical_not(head0)
            qh = jnp.where(mh, qp, 0.0).astype(BF16)
            snk = sink_ref[0, r:r + 1, 0:1] if has_sink else None
            p, _ = _attn_probs(qh, kb, bias_ref[r], valid, snk)
            outs.append(jnp.dot(p.astype(BF16), vb, preferred_element_type=F32))
        o_ref[...] = jnp.where(head0, outs[0], outs[1]).astype(o_ref.dtype)

    in_specs = [pl.BlockSpec((tq, LANES), lambda p, i: (i, p)),
                pl.BlockSpec((pad + S, LANES), lambda p, i: (0, p)),
                pl.BlockSpec((pad + S, LANES), lambda p, i: (0, p)),
                pl.BlockSpec((2, tq, wk), lambda p, i: (p, 0, 0))]
    args = [q, k, v, bias_w]
    if has_sink:
        in_specs.append(pl.BlockSpec((1, 2, LANES), lambda p, i: (p, 0, 0)))
        args.append(sinks)
    return pl.pallas_call(
        body, name=name, grid=(N_HEADS // 2, S // tq), in_specs=in_specs,
        out_specs=pl.BlockSpec((tq, LANES), lambda p, i: (i, p)),
        out_shape=jax.ShapeDtypeStruct((S, N_HEADS * HEAD_DIM), BF16),
        compiler_params=_params("parallel", "parallel"),
    )(*args)


def attn_bwd(q, k, v, do, bias_w, sinks, *, n_prev, name):
    S = q.shape[0]
    pad = n_prev * CHUNK
    tq = min(ATT_TQ, S)
    wk = tq + pad
    assert bias_w.shape == (N_HEADS, tq, wk), bias_w.shape
    has_sink = sinks is not None
    scale = HEAD_DIM ** -0.5

    def body(*refs):
        if has_sink:
            q_ref, k_ref, v_ref, do_ref, bias_ref, sink_ref, dq_ref, dk_ref, dv_ref, db_ref, dsk_ref = refs
        else:
            q_ref, k_ref, v_ref, do_ref, bias_ref, dq_ref, dk_ref, dv_ref, db_ref = refs
        i = pl.program_id(1)

        @pl.when(i == 0)
        def _():
            dk_ref[...] = jnp.zeros_like(dk_ref)
            dv_ref[...] = jnp.zeros_like(dv_ref)
            db_ref[...] = jnp.zeros_like(db_ref)
            if has_sink:
                dsk_ref[...] = jnp.zeros_like(dsk_ref)

        start = pl.multiple_of(i * tq, tq)
        head0 = lax.broadcasted_iota(jnp.int32, (1, LANES), 1) < HEAD_DIM
        valid = lax.broadcasted_iota(jnp.int32, (1, wk), 1) + start >= pad
        qp = q_ref[...].astype(F32)
        dop = do_ref[...].astype(F32)
        kb = k_ref[pl.ds(start, wk), :]
        vb = v_ref[pl.ds(start, wk), :]
        dqs = []
        for r in range(2):
            mh = head0 if r == 0 else jnp.logical_not(head0)
            qh = jnp.where(mh, qp, 0.0).astype(BF16)
            doh = jnp.where(mh, dop, 0.0).astype(BF16)
            snk = sink_ref[0, r:r + 1, 0:1] if has_sink else None
            p, ps = _attn_probs(qh, kb, bias_ref[r], valid, snk)
            dp = lax.dot_general(doh, vb, _NT, preferred_element_type=F32)
            delta = jnp.sum(p * dp, axis=1, keepdims=True)
            ds = p * (dp - delta)
            db_ref[r] += ds
            if has_sink:
                dsk = -jnp.sum(ps * delta, axis=0, keepdims=True)
                dsk_ref[0, r:r + 1, :] += jnp.broadcast_to(dsk, (1, LANES))
            dsb = ds.astype(BF16)
            dqs.append(jnp.dot(dsb, kb, preferred_element_type=F32) * scale)
            dk_ref[pl.ds(start, wk), :] += lax.dot_general(dsb, qh, _TN, preferred_element_type=F32) * scale
            dv_ref[pl.ds(start, wk), :] += lax.dot_general(p.astype(BF16), doh, _TN, preferred_element_type=F32)
        dq_ref[...] = jnp.where(head0, dqs[0], dqs[1])

    row_spec = pl.BlockSpec((tq, LANES), lambda p, i: (i, p))
    kv_spec = pl.BlockSpec((pad + S, LANES), lambda p, i: (0, p))
    bias_spec = pl.BlockSpec((2, tq, wk), lambda p, i: (p, 0, 0))
    sink_spec = pl.BlockSpec((1, 2, LANES), lambda p, i: (p, 0, 0))
    in_specs = [row_spec, kv_spec, kv_spec, row_spec, bias_spec]
    args = [q, k, v, do, bias_w]
    out_specs = [row_spec, kv_spec, kv_spec, bias_spec]
    W = N_HEADS * HEAD_DIM
    out_shape = [jax.ShapeDtypeStruct((S, W), F32), jax.ShapeDtypeStruct((pad + S, W), F32),
                 jax.ShapeDtypeStruct((pad + S, W), F32), jax.ShapeDtypeStruct((N_HEADS, tq, wk), F32)]
    if has_sink:
        in_specs.append(sink_spec)
        args.append(sinks)
        out_specs.append(sink_spec)
        out_shape.append(jax.ShapeDtypeStruct((N_HEADS // 2, 2, LANES), F32))
    return pl.pallas_call(
        body, name=name, grid=(N_HEADS // 2, S // tq), in_specs=in_specs, out_specs=out_specs,
        out_shape=out_shape, compiler_params=_params("arbitrary", "arbitrary"),
    )(*args)


HP = SSM_HEADS // 2
PAIRS_PER_GROUP = HP // SSM_GROUPS
HEADS_PER_GROUP = SSM_HEADS // SSM_GROUPS
GW = HEADS_PER_GROUP * 64


def _ssd_dt(dtraw, dtb, A, tril):
    lane = lax.broadcasted_iota(jnp.int32, (1, LANES), 1)
    u = dtraw + dtb
    eu = jnp.exp(-jnp.abs(u))
    w1 = 1.0 + eu
    l1p = jnp.where(w1 == 1.0, eu, jnp.log(w1) * eu / jnp.where(w1 == 1.0, 1.0, w1 - 1.0))
    dt = jnp.where(lane < SSM_HEADS, jnp.maximum(u, 0.0) + l1p, 0.0)
    acs = jnp.dot(tril, dt * A, precision=HI, preferred_element_type=F32)
    return u, dt, acs


def _ssd_head(acs, acsT, dt, h, trilb):
    L = acs.shape[0]
    ca = acs[:, h:h + 1]
    ra = acsT[h:h + 1, :]
    dec = jnp.exp(jnp.where(trilb, ca - ra, -jnp.inf))
    last = acs[L - 1:L, h:h + 1]
    return dec, dt[:, h:h + 1], jnp.exp(ca), jnp.exp(last - ca), last


def ssd_fwd(xbc, dtraw, dtb, A, dexp, *, name):
    S = xbc.shape[0]
    L = min(SSD_L, S)
    nc = S // L
    N = SSM_STATE

    def body(xs_ref, b_ref, c_ref, dtr_ref, dtb_ref, a_ref, d_ref, y_ref, st_out_ref, st_ref, xw_ref):
        c = pl.program_id(0)

        @pl.when(c == 0)
        def _():
            st_ref[...] = jnp.zeros_like(st_ref)

        st_out_ref[0] = st_ref[...]
        ri = lax.broadcasted_iota(jnp.int32, (L, L), 0)
        ci = lax.broadcasted_iota(jnp.int32, (L, L), 1)
        trilb = ri >= ci
        head0 = lax.broadcasted_iota(jnp.int32, (1, LANES), 1) < 64
        _, dt, acs = _ssd_dt(dtr_ref[...], dtb_ref[...], a_ref[...], trilb.astype(F32))
        acsT = acs.T
        lasts = []
        for g in range(SSM_GROUPS):
            Bg = b_ref[:, g * N:(g + 1) * N].astype(BF16)
            Cg = c_ref[:, g * N:(g + 1) * N].astype(BF16)
            CB = lax.dot_general(Cg, Bg, _NT, preferred_element_type=F32)
            Z = lax.dot_general(Cg, st_ref[g * GW:(g + 1) * GW, :].astype(BF16), _NT, preferred_element_type=F32)
            for q in range(PAIRS_PER_GROUP):
                hp = g * PAIRS_PER_GROUP + q
                sl = slice(hp * LANES, (hp + 1) * LANES)
                xs = xs_ref[:, sl]
                hd = [_ssd_head(acs, acsT, dt, 2 * hp + r, trilb) for r in range(2)]
                dtp = jnp.where(head0, hd[0][1], hd[1][1])
                eap = jnp.where(head0, hd[0][2], hd[1][2])
                wp = jnp.where(head0, hd[0][3] * hd[0][1], hd[1][3] * hd[1][1])
                xd = xs * dtp
                yi = jnp.dot((CB * hd[0][0]).astype(BF16), jnp.where(head0, xd, 0.0).astype(BF16), preferred_element_type=F32)
                yi = yi + jnp.dot((CB * hd[1][0]).astype(BF16), jnp.where(head0, 0.0, xd).astype(BF16), preferred_element_type=F32)
                y_ref[:, sl] = yi + Z[:, q * LANES:(q + 1) * LANES] * eap + d_ref[:, sl] * xs
                xw_ref[:, sl] = (xs * wp).astype(BF16)
                lasts += [hd[0][4], hd[1][4]]
        for g in range(SSM_GROUPS):
            Bg = b_ref[:, g * N:(g + 1) * N].astype(BF16)
            sn = lax.dot_general(xw_ref[:, g * GW:(g + 1) * GW], Bg, _TN, preferred_element_type=F32)
            for k in range(HEADS_PER_GROUP):
                h = g * HEADS_PER_GROUP + k
                rows = slice(h * 64, (h + 1) * 64)
                st_ref[rows, :] = st_ref[rows, :] * jnp.exp(lasts[h]) + sn[k * 64:(k + 1) * 64, :]

    return pl.pallas_call(
        body, name=name, grid=(nc,),
        in_specs=[pl.BlockSpec((L, D_INNER), lambda c: (c, 0)),
                  pl.BlockSpec((L, SSM_GROUPS * N), lambda c: (c, D_INNER // (SSM_GROUPS * N))),
                  pl.BlockSpec((L, SSM_GROUPS * N), lambda c: (c, D_INNER // (SSM_GROUPS * N) + 1)),
                  pl.BlockSpec((L, LANES), lambda c: (c, 0)),
                  pl.BlockSpec((1, LANES), lambda c: (0, 0)),
                  pl.BlockSpec((1, LANES), lambda c: (0, 0)),
                  pl.BlockSpec((1, D_INNER), lambda c: (0, 0))],
        out_specs=[pl.BlockSpec((L, D_INNER), lambda c: (c, 0)),
                   pl.BlockSpec((1, D_INNER, N), lambda c: (c, 0, 0))],
        out_shape=[jax.ShapeDtypeStruct((S, D_INNER), F32), jax.ShapeDtypeStruct((nc, D_INNER, N), F32)],
        scratch_shapes=[pltpu.VMEM((D_INNER, N), F32), pltpu.VMEM((L, D_INNER), BF16)],
        compiler_params=_params("arbitrary"),
    )(xbc, xbc, xbc, dtraw, dtb, A, dexp)


def ssd_bwd(xbc, dtraw, dtb, A, dexp, states, dy, *, name):
    S = xbc.shape[0]
    L = min(SSD_L, S)
    nc = S // L
    N = SSM_STATE
    hw = D_INNER // SSM_HEADS
    e_np = (np.arange(LANES)[:, None] == np.arange(D_INNER)[None, :] // hw).astype(np.float32)
    e_mat, et_mat = jnp.asarray(e_np), jnp.asarray(e_np.T)

    def body(xs_ref, b_ref, c_ref, dtr_ref, dtb_ref, a_ref, d_ref, e_ref, et_ref, st_in_ref, dy_ref,
             dxs_ref, db_ref, dc_ref, ddtr_ref, da_ref, ddtb_ref, dd_ref, dst_ref, xw_ref, dz_ref, r_ref,
             dsr_ref, dsc_ref):
        step = pl.program_id(0)

        @pl.when(step == 0)
        def _():
            dst_ref[...] = jnp.zeros_like(dst_ref)
            dsr_ref[...] = jnp.zeros_like(dsr_ref)
            dsc_ref[...] = jnp.zeros_like(dsc_ref)
            da_ref[...] = jnp.zeros_like(da_ref)
            ddtb_ref[...] = jnp.zeros_like(ddtb_ref)
            dd_ref[...] = jnp.zeros_like(dd_ref)

        ri = lax.broadcasted_iota(jnp.int32, (L, L), 0)
        ci = lax.broadcasted_iota(jnp.int32, (L, L), 1)
        trilb = ri >= ci
        lane = lax.broadcasted_iota(jnp.int32, (1, LANES), 1)
        sub = lax.broadcasted_iota(jnp.int32, (LANES, 1), 0)
        head0 = lane < 64
        A = a_ref[...]
        u, dt, acs = _ssd_dt(dtr_ref[...], dtb_ref[...], A, trilb.astype(F32))
        acsT = acs.T
        last = acs[L - 1:L, :]
        elast = jnp.exp(last)
        er = jnp.exp(last - acs)
        wt = er * dt
        expand = lambda t: jnp.dot(t, e_ref[...], precision=SEL, preferred_element_type=F32)
        dte, eae, wte = expand(dt), expand(jnp.exp(acs)), expand(wt)
        dlast = jnp.zeros((1, LANES), F32)
        dcbs = []
        for g in range(SSM_GROUPS):
            Bg = b_ref[:, g * N:(g + 1) * N].astype(BF16)
            Cg = c_ref[:, g * N:(g + 1) * N].astype(BF16)
            stg = st_in_ref[0, g * GW:(g + 1) * GW, :]
            dstg = dst_ref[g * GW:(g + 1) * GW, :]
            CB = lax.dot_general(Cg, Bg, _NT, preferred_element_type=F32)
            CBT = lax.dot_general(Bg, Cg, _NT, preferred_element_type=F32)
            Z = lax.dot_general(Cg, stg.astype(BF16), _NT, preferred_element_type=F32)
            U = lax.dot_general(Bg, dstg.astype(BF16), _NT, preferred_element_type=F32)
            dcb = jnp.zeros((L, L), F32)
            for q in range(PAIRS_PER_GROUP):
                hp = g * PAIRS_PER_GROUP + q
                sl = slice(hp * LANES, (hp + 1) * LANES)
                qs = slice(q * LANES, (q + 1) * LANES)
                xs = xs_ref[:, sl]
                dyp = dy_ref[:, sl]
                dtp, eap, wp, Dp = dte[:, sl], eae[:, sl], wte[:, sl], d_ref[:, sl]
                xd = xs * dtp
                dxd = jnp.zeros((L, LANES), F32)
                for r in range(2):
                    h = 2 * hp + r
                    mh = head0 if r == 0 else jnp.logical_not(head0)
                    dyh = jnp.where(mh, dyp, 0.0).astype(BF16)
                    xdh = jnp.where(mh, xd, 0.0).astype(BF16)
                    seg = acs[:, h:h + 1] - acsT[h:h + 1, :]
                    dec = jnp.exp(jnp.where(trilb, seg, -jnp.inf))
                    decT = jnp.exp(jnp.where(ri <= ci, -seg, -jnp.inf))
                    G = lax.dot_general(dyh, xdh, _NT, preferred_element_type=F32)
                    gd = G * dec
                    dcb = dcb + gd
                    dseg = gd * CB
                    dsr_ref[:, h:h + 1] = jnp.sum(dseg, axis=1, keepdims=True)
                    dsc_ref[h:h + 1, :] = jnp.sum(dseg, axis=0, keepdims=True)
                    dxd = dxd + jnp.dot((CBT * decT).astype(BF16), dyh, preferred_element_type=F32)
                Up = U[:, qs]
                r_ref[0, :, sl] = dyp * Z[:, qs] * eap
                r_ref[1, :, sl] = dxd * xs
                r_ref[2, :, sl] = Up * xs
                dz_ref[:, sl] = (dyp * eap).astype(BF16)
                xw_ref[:, sl] = (xs * wp).astype(BF16)
                dxs_ref[:, sl] = dxd * dtp + Dp * dyp + Up * wp
                dd_ref[:, sl] += jnp.sum(dyp * xs, axis=0, keepdims=True)
            dcbs.append(dcb)
            t = dstg * stg
            for k in range(HEADS_PER_GROUP):
                dlast = dlast + jnp.where(lane == g * HEADS_PER_GROUP + k,
                                          jnp.sum(t[k * 64:(k + 1) * 64, :], keepdims=True), 0.0)
        fold = lambda k: jnp.dot(r_ref[k], et_ref[...], precision=SEL, preferred_element_type=F32)
        r1, r2, dws = fold(0), fold(1), fold(2)
        dww = dws * wt
        ddt = r2 + dws * er
        dacs = r1 - dww + dsr_ref[...] - dsc_ref[...].T
        dlast = dlast * elast + jnp.sum(dww, axis=0, keepdims=True)
        lasts = [last[:, h:h + 1] for h in range(SSM_HEADS)]
        for g in range(SSM_GROUPS):
            Bg = b_ref[:, g * N:(g + 1) * N].astype(BF16)
            Cg = c_ref[:, g * N:(g + 1) * N].astype(BF16)
            gs = slice(g * GW, (g + 1) * GW)
            stb = st_in_ref[0, gs, :].astype(BF16)
            dstb = dst_ref[gs, :].astype(BF16)
            dcbb = dcbs[g].astype(BF16)
            dzg = dz_ref[:, gs]
            dc_ref[:, g * N:(g + 1) * N] = (jnp.dot(dzg, stb, preferred_element_type=F32)
                                            + jnp.dot(dcbb, Bg, preferred_element_type=F32))
            db_ref[:, g * N:(g + 1) * N] = (jnp.dot(xw_ref[:, gs], dstb, preferred_element_type=F32)
                                            + lax.dot_general(dcbb, Cg, _TN, preferred_element_type=F32))
            dsn = lax.dot_general(dzg, Cg, _TN, preferred_element_type=F32)
            for k in range(HEADS_PER_GROUP):
                h = g * HEADS_PER_GROUP + k
                rows = slice(h * 64, (h + 1) * 64)
                dst_ref[rows, :] = dst_ref[rows, :] * jnp.exp(lasts[h]) + dsn[k * 64:(k + 1) * 64, :]
        rowi = lax.broadcasted_iota(jnp.int32, (L, 1), 0)
        dacs = dacs + jnp.where(rowi == L - 1, dlast, 0.0)
        da = jnp.dot((ci >= ri).astype(F32), dacs, precision=HI, preferred_element_type=F32)
        ddt = ddt + da * A
        da_ref[...] += jnp.sum(da * dt, axis=0, keepdims=True)
        ddtr = jnp.where(lane < SSM_HEADS, ddt * _sigmoid(u), 0.0)
        ddtr_ref[...] = ddtr
        ddtb_ref[...] += jnp.sum(ddtr, axis=0, keepdims=True)

    rev = lambda c: nc - 1 - c
    gn = SSM_GROUPS * N
    return pl.pallas_call(
        body, name=name, grid=(nc,),
        in_specs=[pl.BlockSpec((L, D_INNER), lambda c: (rev(c), 0)),
                  pl.BlockSpec((L, gn), lambda c: (rev(c), D_INNER // gn)),
                  pl.BlockSpec((L, gn), lambda c: (rev(c), D_INNER // gn + 1)),
                  pl.BlockSpec((L, LANES), lambda c: (rev(c), 0)),
                  pl.BlockSpec((1, LANES), lambda c: (0, 0)),
                  pl.BlockSpec((1, LANES), lambda c: (0, 0)),
                  pl.BlockSpec((1, D_INNER), lambda c: (0, 0)),
                  pl.BlockSpec((LANES, D_INNER), lambda c: (0, 0)),
                  pl.BlockSpec((D_INNER, LANES), lambda c: (0, 0)),
                  pl.BlockSpec((1, D_INNER, N), lambda c: (rev(c), 0, 0)),
                  pl.BlockSpec((L, D_INNER), lambda c: (rev(c), 0))],
        out_specs=[pl.BlockSpec((L, D_INNER), lambda c: (rev(c), 0)),
                   pl.BlockSpec((L, gn), lambda c: (rev(c), 0)),
                   pl.BlockSpec((L, gn), lambda c: (rev(c), 0)),
                   pl.BlockSpec((L, LANES), lambda c: (rev(c), 0)),
                   pl.BlockSpec((1, LANES), lambda c: (0, 0)),
                   pl.BlockSpec((1, LANES), lambda c: (0, 0)),
                   pl.BlockSpec((1, D_INNER), lambda c: (0, 0))],
        out_shape=[jax.ShapeDtypeStruct((S, D_INNER), F32), jax.ShapeDtypeStruct((S, gn), F32),
                   jax.ShapeDtypeStruct((S, gn), F32), jax.ShapeDtypeStruct((S, LANES), F32),
                   jax.ShapeDtypeStruct((1, LANES), F32), jax.ShapeDtypeStruct((1, LANES), F32),
                   jax.ShapeDtypeStruct((1, D_INNER), F32)],
        scratch_shapes=[pltpu.VMEM((D_INNER, N), F32), pltpu.VMEM((L, D_INNER), BF16), pltpu.VMEM((L, D_INNER), BF16),
                        pltpu.VMEM((3, L, D_INNER), F32), pltpu.VMEM((L, LANES), F32), pltpu.VMEM((LANES, L), F32)],
        compiler_params=_params("arbitrary"),
    )(xbc, xbc, xbc, dtraw, dtb, A, dexp, e_mat, et_mat, states, dy)


BAND_A = (A_PREV + 1) * CHUNK
REL_W = 640


def _relpos_onehot(q):
    u = lax.broadcasted_iota(jnp.int32, (BAND_A, 1), 0)
    idx = jnp.clip(q - u + A_PREV * CHUNK, -MAX_REL, MAX_REL) + MAX_REL
    r = lax.broadcasted_iota(jnp.int32, (1, REL_W), 1)
    return (r == idx).astype(F32)


def relpos_bias(table_pad, *, name):
    def body(t_ref, o_ref):
        oh = _relpos_onehot(pl.program_id(0))
        o_ref[0] = lax.dot_general(t_ref[...], oh, _NT, precision=HI, preferred_element_type=F32)

    return pl.pallas_call(
        body, name=name, grid=(CHUNK,),
        in_specs=[pl.BlockSpec((N_HEADS, REL_W), lambda q: (0, 0))],
        out_specs=pl.BlockSpec((1, N_HEADS, BAND_A), lambda q: (q, 0, 0)),
        out_shape=jax.ShapeDtypeStruct((CHUNK, N_HEADS, BAND_A), F32),
        compiler_params=_params("parallel"),
    )(table_pad)


def relpos_grad(dbias_t, *, name):
    def body(d_ref, o_ref):
        q = pl.program_id(0)

        @pl.when(q == 0)
        def _():
            o_ref[...] = jnp.zeros_like(o_ref)

        o_ref[...] += jnp.dot(d_ref[0], _relpos_onehot(q), precision=HI, preferred_element_type=F32)

    return pl.pallas_call(
        body, name=name, grid=(CHUNK,),
        in_specs=[pl.BlockSpec((1, N_HEADS, BAND_A), lambda q: (q, 0, 0))],
        out_specs=pl.BlockSpec((N_HEADS, REL_W), lambda q: (0, 0)),
        out_shape=jax.ShapeDtypeStruct((N_HEADS, REL_W), F32),
        compiler_params=_params("arbitrary"),
    )(dbias_t)


def loss_head(y, t, *, name, tm=256):
    S, D = y.shape
    tm = min(tm, S)

    def body(y_ref, t_ref, dy_ref, l_ref):
        e = y_ref[...] - t_ref[...]
        dy_ref[...] = e * (1.0 / D)

        @pl.when(pl.program_id(0) == 0)
        def _():
            l_ref[...] = jnp.zeros_like(l_ref)

        part = jnp.sum(jnp.sum(e * e, axis=1, keepdims=True), axis=0, keepdims=True) * (0.5 / D)
        l_ref[...] += jnp.broadcast_to(part, l_ref.shape)

    return pl.pallas_call(
        body, name=name, grid=(S // tm,),
        in_specs=[pl.BlockSpec((tm, D), lambda i: (i, 0))] * 2,
        out_specs=[pl.BlockSpec((tm, D), lambda i: (i, 0)), pl.BlockSpec((1, LANES), lambda i: (0, 0))],
        out_shape=[jax.ShapeDtypeStruct((S, D), F32), jax.ShapeDtypeStruct((1, LANES), F32)],
        compiler_params=_params("arbitrary"),
    )(y, t)


def f_adamw(w, g, m, v):
    m = ADAM_B1 * m + (1.0 - ADAM_B1) * g
    v = ADAM_B2 * v + (1.0 - ADAM_B2) * (g * g)
    m_hat = m / (1.0 - ADAM_B1 ** ADAM_STEP)
    v_hat = v / (1.0 - ADAM_B2 ** ADAM_STEP)
    delta = -ADAM_LR * (m_hat / (jnp.sqrt(v_hat) + ADAM_EPS) + ADAM_WD * w)
    return delta, m, v


def f_norm_id(x, g):
    return f_rmsnorm(x, g)[0], x


ANY = pl.BlockSpec(memory_space=pl.ANY)


def _pos():
    return lax.axis_index("x"), lax.axis_index("y"), lax.axis_index("c")


def _other_chips(x, y):
    return [(1 - x, y), (x, 1 - y), (1 - x, 1 - y)]


def gather_chips(shards, *, name):
    n = len(shards)

    def body(*refs):
        ins, outs = refs[:n], refs[n:2 * n]
        send, recv, fsend, frecv = refs[2 * n:]
        x, y, c = _pos()
        me = 2 * x + y
        sib = (x, y, 1 - c)
        chips = _other_chips(x, y)
        first = []
        for i in range(n):
            for j, (px, py) in enumerate(chips):
                cp = pltpu.make_async_remote_copy(ins[i].at[c], outs[i].at[me, c], send.at[3 * i + j], recv.at[3 * i + j],
                                                  device_id=(px, py, c), device_id_type=MESH)
                cp.start()
                first.append(cp)
        passed = []
        for i in range(n):
            for j, (px, py) in enumerate(chips):
                got = outs[i].at[2 * px + py, c]
                pltpu.make_async_remote_copy(ins[i].at[c], got, send.at[3 * i + j], recv.at[3 * i + j],
                                             device_id=(px, py, c), device_id_type=MESH).wait_recv()
                cp = pltpu.make_async_remote_copy(got, got, fsend.at[3 * i + j], frecv.at[3 * i + j],
                                                  device_id=sib, device_id_type=MESH)
                cp.start()
                passed.append(cp)
        for i in range(n):
            for j, (px, py) in enumerate(chips):
                theirs = outs[i].at[2 * px + py, 1 - c]
                pltpu.make_async_remote_copy(theirs, theirs, fsend.at[3 * i + j], frecv.at[3 * i + j],
                                             device_id=sib, device_id_type=MESH).wait_recv()
        for cp in first + passed:
            cp.wait_send()

    return pl.pallas_call(
        body, name=name, in_specs=[ANY] * n, out_specs=[ANY] * n,
        out_shape=[jax.ShapeDtypeStruct((4,) + s.shape, s.dtype) for s in shards],
        scratch_shapes=[pltpu.SemaphoreType.DMA((3 * n,)), pltpu.SemaphoreType.DMA((3 * n,)),
                        pltpu.SemaphoreType.DMA((3 * n,)), pltpu.SemaphoreType.DMA((3 * n,))],
        compiler_params=pltpu.CompilerParams(has_side_effects=True),
    )(*shards)


def pair_swap_halves(gs, *, name):
    n = len(gs)

    def body(*refs):
        ins, outs = refs[:n], refs[n:2 * n]
        send, recv = refs[2 * n:]
        x, y, c = _pos()
        cps = []
        for i in range(n):
            cp = pltpu.make_async_remote_copy(ins[i].at[1 - c], outs[i], send.at[i], recv.at[i],
                                              device_id=(x, y, 1 - c), device_id_type=MESH)
            cp.start()
            cps.append(cp)
        for cp in cps:
            cp.wait()

    return pl.pallas_call(
        body, name=name, in_specs=[ANY] * n, out_specs=[ANY] * n,
        out_shape=[jax.ShapeDtypeStruct(g.shape[1:], g.dtype) for g in gs],
        scratch_shapes=[pltpu.SemaphoreType.DMA((n,)), pltpu.SemaphoreType.DMA((n,))],
        compiler_params=pltpu.CompilerParams(has_side_effects=True),
    )(*gs)


def scatter_chips(ps, *, name):
    n = len(ps)

    def body(*refs):
        ins, outs = refs[:n], refs[n:2 * n]
        send, recv = refs[2 * n:]
        x, y, c = _pos()
        cps = []
        for i in range(n):
            for j, (px, py) in enumerate(_other_chips(x, y)):
                cp = pltpu.make_async_remote_copy(ins[i].at[2 * px + py], outs[i].at[j], send.at[3 * i + j], recv.at[3 * i + j],
                                                  device_id=(px, py, c), device_id_type=MESH)
                cp.start()
                cps.append(cp)
        for cp in cps:
            cp.wait()

    return pl.pallas_call(
        body, name=name, in_specs=[ANY] * n, out_specs=[ANY] * n,
        out_shape=[jax.ShapeDtypeStruct((3,) + p.shape[1:], p.dtype) for p in ps],
        scratch_shapes=[pltpu.SemaphoreType.DMA((3 * n,)), pltpu.SemaphoreType.DMA((3 * n,))],
        compiler_params=pltpu.CompilerParams(has_side_effects=True),
    )(*ps)


def pair_share(hs, *, name):
    n = len(hs)

    def body(*refs):
        ins, outs = refs[:n], refs[n:2 * n]
        send, recv = refs[2 * n:]
        x, y, c = _pos()
        cps = []
        for i in range(n):
            cp = pltpu.make_async_remote_copy(ins[i], outs[i], send.at[i], recv.at[i],
                                              device_id=(x, y, 1 - c), device_id_type=MESH)
            cp.start()
            cps.append(cp)
        for cp in cps:
            cp.wait()

    return pl.pallas_call(
        body, name=name, in_specs=[ANY] * n, out_specs=[ANY] * n,
        out_shape=[jax.ShapeDtypeStruct(h.shape, h.dtype) for h in hs],
        scratch_shapes=[pltpu.SemaphoreType.DMA((n,)), pltpu.SemaphoreType.DMA((n,))],
        compiler_params=pltpu.CompilerParams(has_side_effects=True),
    )(*hs)


def gather_all(buf, *, name):
    def body(in_ref, out_ref, send, recv, loc):
        x, y, c = _pos()
        lid = 4 * x + 2 * y + c
        lc = pltpu.make_async_copy(in_ref, out_ref.at[lid], loc.at[0])
        lc.start()
        cps = []
        for k in range(1, 8):
            px = 1 - x if k & 4 else x
            py = 1 - y if k & 2 else y
            pc = 1 - c if k & 1 else c
            cp = pltpu.make_async_remote_copy(in_ref, out_ref.at[lid], send.at[k - 1], recv.at[k - 1],
                                              device_id=(px, py, pc), device_id_type=MESH)
            cp.start()
            cps.append((cp, 4 * px + 2 * py + pc, (px, py, pc)))
        for k, (cp, plid, peer) in enumerate(cps):
            cp.wait_send()
            pltpu.make_async_remote_copy(in_ref, out_ref.at[plid], send.at[k], recv.at[k],
                                         device_id=peer, device_id_type=MESH).wait_recv()
        lc.wait()

    return pl.pallas_call(
        body, name=name, in_specs=[ANY], out_specs=ANY,
        out_shape=jax.ShapeDtypeStruct((8,) + buf.shape, buf.dtype),
        scratch_shapes=[pltpu.SemaphoreType.DMA((7,)), pltpu.SemaphoreType.DMA((7,)), pltpu.SemaphoreType.DMA((1,))],
        compiler_params=pltpu.CompilerParams(has_side_effects=True),
    )(buf)


def sum_slots(a, *, name):
    n = a.shape[0]

    def body(a_ref, o_ref):
        acc = a_ref[0]
        for k in range(1, n):
            acc = acc + a_ref[k]
        o_ref[...] = acc

    return pl.pallas_call(body, name=name, out_shape=jax.ShapeDtypeStruct(a.shape[1:], a.dtype),
                          compiler_params=pltpu.CompilerParams(vmem_limit_bytes=VMEM_LIMIT))(a)


def _row_tile(r, want, mult=16):
    t = (min(want, r) // mult) * mult
    while t >= mult:
        if r % t == 0:
            return t
        t -= mult
    return r


def pair_add(g, r1, csel, *, name):
    _, _, r, C = g.shape
    tr = _row_tile(r, 256)

    def body(g_ref, r_ref, c_ref, p32_ref, pb_ref):
        south = c_ref[0:1, 0:1] == 0.0
        p = jnp.where(south, g_ref[0, 0], g_ref[1, 0]) + r_ref[0]
        p32_ref[0] = p
        pb_ref[0] = p.astype(BF16)

    return pl.pallas_call(
        body, name=name, grid=(4, r // tr),
        in_specs=[pl.BlockSpec((2, 1, tr, C), lambda j, t: (0, j, t, 0)), pl.BlockSpec((1, tr, C), lambda j, t: (j, t, 0)),
                  pl.BlockSpec((1, LANES), lambda j, t: (0, 0))],
        out_specs=[pl.BlockSpec((1, tr, C), lambda j, t: (j, t, 0))] * 2,
        out_shape=[jax.ShapeDtypeStruct((4, r, C), F32), jax.ShapeDtypeStruct((4, r, C), BF16)],
        compiler_params=_params("parallel", "parallel"),
    )(g, r1, csel)


def chip_add(p32, r3, msel, *, name):
    _, r, C = p32.shape
    tr = _row_tile(r, 128)

    def body(p_ref, r_ref, m_ref, o_ref):
        me = m_ref[0:1, 0:1]
        acc = jnp.where(me == 0.0, p_ref[0], jnp.where(me == 1.0, p_ref[1], jnp.where(me == 2.0, p_ref[2], p_ref[3])))
        for j in range(3):
            acc = acc + r_ref[j].astype(F32)
        o_ref[...] = acc

    return pl.pallas_call(
        body, name=name, grid=(r // tr,),
        in_specs=[pl.BlockSpec((4, tr, C), lambda t: (0, t, 0)), pl.BlockSpec((3, tr, C), lambda t: (0, t, 0)),
                  pl.BlockSpec((1, LANES), lambda t: (0, 0))],
        out_specs=pl.BlockSpec((tr, C), lambda t: (t, 0)),
        out_shape=jax.ShapeDtypeStruct((r, C), F32),
        compiler_params=_params("parallel"),
    )(p32, r3, msel)


def _consts():
    i512 = np.arange(N_HEADS * HEAD_DIM)
    i128 = np.arange(LANES)
    bd512 = (i512[:, None] // HEAD_DIM == i512[None, :] // HEAD_DIM).astype(np.float32)
    bd128 = (i128[:, None] // HEAD_DIM == i128[None, :] // HEAD_DIM).astype(np.float32)
    fold = (np.arange(HEAD_DIM)[:, None] == (i512[None, :] % HEAD_DIM)).astype(np.float32)
    grp = N_HEADS // 2 * HEAD_DIM
    expand = ((i128[:, None] // HEAD_DIM == i512[None, :] // grp)
              & (i128[:, None] % HEAD_DIM == i512[None, :] % HEAD_DIM)).astype(np.float32)
    band = (B_PREV + 1) * CHUNK
    rel = np.arange(CHUNK)[:, None] - (np.arange(band)[None, :] - B_PREV * CHUNK)
    slopes = 2.0 ** (-8.0 * np.arange(1, N_HEADS + 1, dtype=np.float32) / N_HEADS)
    bias_b = (-slopes[:, None, None] * np.abs(rel).astype(np.float32)[None]).astype(np.float32)
    return [jnp.asarray(a) for a in (bd512, bd128, fold, expand)], jnp.asarray(bias_b)


def _ffn_fwd(xin, l, W, P):
    g = P["norm_ffn"][l:l + 1]
    (h,) = rowwise(f_rmsnorm, [xin], [g], [(D_MODEL, BF16)], name=f"ffn{l}_norm")
    Wi = W["ffn_in"][l]
    gate = matmul(h, Wi[:, :D_FF], mode="nn", name=f"ffn{l}_gate")
    up = matmul(h, Wi[:, D_FF:], mode="nn", name=f"ffn{l}_up")
    gc, act = dwconv_fwd(gate, P["ffn_conv_w"][l], P["ffn_conv_b"][l:l + 1], lambda y, u: (y, _silu(y) * u), [up],
                         [F32, BF16], name=f"ffn{l}_conv")
    xout = matmul(act, W["ffn_out"][l], mode="nn", name=f"ffn{l}_out", residual=xin)
    return xout, (xin, h, gate, gc, up, act)


def _ffn_bwd(dxout, l, saved, W, P):
    xin, h, gate, gc, up, act = saved
    g = P["norm_ffn"][l:l + 1]
    Wi = W["ffn_in"][l]
    dact = matmul(dxout, W["ffn_out"][l], mode="nt", name=f"ffn{l}_dact")
    dWo = matmul(act, dxout, mode="tn", name=f"ffn{l}_dwout")
    dgate, dcw, dcb, dup = dwconv_bwd(gate, P["ffn_conv_w"][l], [gc, up, dact],
                                      lambda c, u, da: (da * u * _dsilu(c), da * _silu(c)), [BF16],
                                      name=f"ffn{l}_dconv")
    dh = matmul(dgate, Wi[:, :D_FF], mode="nt", name=f"ffn{l}_dh_gate")
    dh = matmul(dup, Wi[:, D_FF:], mode="nt", name=f"ffn{l}_dh_up", residual=dh)
    dWi = jnp.concatenate([matmul(h, dgate, mode="tn", name=f"ffn{l}_dw_gate"),
                           matmul(h, dup, mode="tn", name=f"ffn{l}_dw_up")], axis=1)
    dxin, dg = rowwise_vjp(f_norm_id, [xin], [g], [dh, dxout], [(0, F32)], [0], name=f"ffn{l}_dnorm")
    return dxin, dWi, dWo, dg, dcw, dcb


def local_step(x, tgt, W, P):
    qk_consts, bias_b = _consts()
    pad_rows = lambda t, n: jnp.pad(t, ((n * CHUNK, 0), (0, 0)))
    DQ = N_HEADS * HEAD_DIM

    g_mix0 = P["norm_mix"][0:1]
    (h0,) = rowwise(f_rmsnorm, [x], [g_mix0], [(D_MODEL, BF16)], name="attn_norm")
    qkv = matmul(h0, W["attn_in"], mode="nn", name="attn_qkv")
    qk_par = [P["q_norm_a"], P["k_norm_a"], P["q_norm_b"], P["k_norm_b"]] + qk_consts
    qa, ka, va, qb, kb, vb = rowwise(f_qknorm, [qkv], qk_par, [(DQ, BF16)] * 6, name="attn_qknorm")
    ka, va, kb, vb = pad_rows(ka, A_PREV), pad_rows(va, A_PREV), pad_rows(kb, B_PREV), pad_rows(vb, B_PREV)
    table = jnp.pad(P["relpos_table"], ((0, 0), (0, REL_W - (2 * MAX_REL + 1))))
    nj = min(ATT_TQ, x.shape[0]) // CHUNK
    bias_a = widen_bias(jnp.transpose(relpos_bias(table, name="relpos_bias"), (1, 0, 2)), A_PREV, nj)
    bias_b = widen_bias(bias_b, B_PREV, nj)
    sinks = jnp.broadcast_to(P["sinks"].reshape(N_HEADS // 2, 2, 1), (N_HEADS // 2, 2, LANES))
    oa = attn_fwd(qa, ka, va, bias_a, None, n_prev=A_PREV, name="attn_a")
    ob = attn_fwd(qb, kb, vb, bias_b, sinks, n_prev=B_PREV, name="attn_b")
    Wao = W["attn_out"]
    x1 = matmul(oa, Wao[:DQ], mode="nn", name="attn_out_a", residual=x)
    x1 = matmul(ob, Wao[DQ:], mode="nn", name="attn_out_b", residual=x1)
    x2, ffn0 = _ffn_fwd(x1, 0, W, P)

    g_mix1 = P["norm_mix"][1:2]
    (h2,) = rowwise(f_rmsnorm, [x2], [g_mix1], [(D_MODEL, BF16)], name="ssm_norm_in")
    Ws = W["ssm_in"]
    CC = D_INNER + 2 * SSM_GROUPS * SSM_STATE
    Wz, Wx = Ws[:, :D_INNER], Ws[:, D_INNER:D_INNER + CC]
    Wdt = jnp.pad(Ws[:, D_INNER + CC:], ((0, 0), (0, LANES - SSM_HEADS)))
    z = matmul(h2, Wz, mode="nn", name="ssm_z")
    xr = matmul(h2, Wx, mode="nn", name="ssm_xbc")
    dtraw = matmul(h2, Wdt, mode="nn", name="ssm_dt")
    xc, xbc = dwconv_fwd(xr, P["ssm_conv_w"], P["ssm_conv_b"], lambda y: (y, _silu(y)), [], [F32, F32],
                         name="ssm_conv")
    pad32 = lambda v: jnp.pad(v, ((0, 0), (0, LANES - SSM_HEADS)))
    A = pad32(-jnp.exp(P["ssm_a_log"]))
    dtb = pad32(P["ssm_dt_bias"])
    dexp = jnp.repeat(P["ssm_d"], D_INNER // SSM_HEADS, axis=1)
    y, states = ssd_fwd(xbc, dtraw, dtb, A, dexp, name="ssd_fwd")
    (y2,) = rowwise(f_gate_norm, [y, z], [P["ssm_norm"]], [(D_INNER, BF16)], name="ssm_gate_norm")
    x3 = matmul(y2, W["ssm_out"], mode="nn", name="ssm_out", residual=x2)
    x4, ffn1 = _ffn_fwd(x3, 1, W, P)

    dx4, lpart = loss_head(x4, tgt, name="loss_head")

    dx3, dWfi1, dWfo1, dgf1, dfcw1, dfcb1 = _ffn_bwd(dx4, 1, ffn1, W, P)
    dy2 = matmul(dx3, W["ssm_out"], mode="nt", name="ssm_dy")
    dWso = matmul(y2, dx3, mode="tn", name="ssm_dwout")
    dy, dz, dnw = rowwise_vjp(f_gate_norm, [y, z], [P["ssm_norm"]], [dy2], [(0, F32), (1, F32)], [0],
                              name="ssm_dgate_norm")
    dxs, dB, dC, ddtraw, dA, ddtb, dDl = ssd_bwd(xbc, dtraw, dtb, A, dexp, states, dy, name="ssd_bwd")
    dxr, dscw, dscb = dwconv_bwd(xr, P["ssm_conv_w"], [xc, dxs, dB, dC],
                                 lambda c, a, b, cc: (jnp.concatenate([a, b, cc], axis=1) * _dsilu(c),), [],
                                 name="ssm_dconv")
    dh2 = matmul(dz, Wz, mode="nt", name="ssm_dh_z")
    dh2 = matmul(dxr, Wx, mode="nt", name="ssm_dh_x", residual=dh2)
    dh2 = matmul(ddtraw, Wdt, mode="nt", name="ssm_dh_dt", residual=dh2)
    dWs = jnp.concatenate([matmul(h2, dz, mode="tn", name="ssm_dw_z"),
                           matmul(h2, dxr, mode="tn", name="ssm_dw_x"),
                           matmul(h2, ddtraw, mode="tn", name="ssm_dw_dt")[:, :SSM_HEADS]], axis=1)
    dx2, dgm1 = rowwise_vjp(f_norm_id, [x2], [g_mix1], [dh2, dx3], [(0, F32)], [0], name="ssm_dnorm_in")

    dx1, dWfi0, dWfo0, dgf0, dfcw0, dfcb0 = _ffn_bwd(dx2, 0, ffn0, W, P)
    doa = matmul(dx1, Wao[:DQ], mode="nt", name="attn_do_a", out_dtype=BF16)
    dob = matmul(dx1, Wao[DQ:], mode="nt", name="attn_do_b", out_dtype=BF16)
    dWao = jnp.concatenate([matmul(oa, dx1, mode="tn", name="attn_dwout_a"),
                            matmul(ob, dx1, mode="tn", name="attn_dwout_b")], axis=0)
    dqa, dka, dva, dbias_a = attn_bwd(qa, ka, va, doa, bias_a, None, n_prev=A_PREV, name="attn_a_bwd")
    dqb, dkb, dvb, _, dsk = attn_bwd(qb, kb, vb, dob, bias_b, sinks, n_prev=B_PREV, name="attn_b_bwd")
    pa, pb = A_PREV * CHUNK, B_PREV * CHUNK
    cots = [dqa, dka[pa:], dva[pa:], dqb, dkb[pb:], dvb[pb:]]
    dqkv, dgqa, dgka, dgqb, dgkb = rowwise_vjp(f_qknorm, [qkv], qk_par, cots, [(0, BF16)], [0, 1, 2, 3],
                                               name="attn_dqknorm")
    dh0 = matmul(dqkv, W["attn_in"], mode="nt", name="attn_dh")
    dWai = matmul(h0, dqkv, mode="tn", name="attn_dwin")
    dx, dgm0 = rowwise_vjp(f_norm_id, [x], [g_mix0], [dh0, dx1], [(0, F32)], [0], name="attn_dnorm")
    dbias_a = fold_bias(dbias_a, A_PREV, nj)
    dtable = relpos_grad(jnp.transpose(dbias_a, (1, 0, 2)), name="relpos_grad")[:, :2 * MAX_REL + 1]

    gW = {"attn_in": dWai, "attn_out": dWao, "ssm_in": dWs, "ssm_out": dWso,
          "ffn_in": [dWfi0, dWfi1], "ffn_out": [dWfo0, dWfo1]}
    gP = {"norm_mix": jnp.concatenate([dgm0, dgm1], axis=0),
          "norm_ffn": jnp.concatenate([dgf0, dgf1], axis=0),
          "relpos_table": dtable, "q_norm_a": dgqa, "k_norm_a": dgka, "q_norm_b": dgqb, "k_norm_b": dgkb,
          "sinks": dsk[:, :, 0].reshape(1, N_HEADS),
          "ssm_conv_w": dscw, "ssm_conv_b": dscb,
          "ssm_dt_bias": ddtb[:, :SSM_HEADS], "ssm_a_log": dA[:, :SSM_HEADS] * A[:, :SSM_HEADS],
          "ssm_d": dDl.reshape(SSM_HEADS, D_INNER // SSM_HEADS).sum(axis=1).reshape(1, SSM_HEADS),
          "ssm_norm": dnw,
          "ffn_conv_w": jnp.stack([dfcw0, dfcw1]), "ffn_conv_b": jnp.concatenate([dfcb0, dfcb1], axis=0)}
    return lpart, dx, gW, gP


WEIGHTS = ["norm_mix", "norm_ffn", "attn_w_in", "attn_w_out", "relpos_table", "q_norm_a", "k_norm_a", "q_norm_b",
           "k_norm_b", "sinks", "ssm_w_in", "ssm_conv_w", "ssm_conv_b", "ssm_dt_bias", "ssm_a_log", "ssm_d",
           "ssm_norm", "ssm_w_out", "ffn_w_in", "ffn_conv_w", "ffn_conv_b", "ffn_w_out"]
ARGS = ["x"] + WEIGHTS + ["loss_target"] + ["m_" + w for w in WEIGHTS] + ["v_" + w for w in WEIGHTS]
N_CHIPS = 4
SMALL_ROWS = 384
SMALL_ORDER = ["norm_mix", "norm_ffn", "relpos_table", "q_norm_a", "k_norm_a", "q_norm_b", "k_norm_b", "sinks",
               "ssm_dt_bias", "ssm_a_log", "ssm_d", "ffn_conv_b", "ssm_conv_w", "ssm_conv_b", "ssm_norm", "ffn_conv_w"]


def _cols_to_slabs(g):
    K, N = g.shape
    return g.reshape(2, K // 2, N_CHIPS, N // N_CHIPS).transpose(0, 2, 1, 3)


def _rows_to_slabs(g):
    R, C = g.shape
    return g.reshape(N_CHIPS, 2, R // (2 * N_CHIPS), C).transpose(1, 0, 2, 3)


def _adamw(w, g, m, v, name):
    shp = w.shape
    two = lambda a: a.reshape((-1, shp[-1]))
    outs = [(shp[-1], F32)] * 3
    d, nm, nv = rowwise(f_adamw, [two(w), two(g), two(m), two(v)], [], outs, name="adamw_" + name)
    return d.reshape(shp), nm.reshape(shp), nv.reshape(shp)


def kernel(x, norm_mix, norm_ffn, attn_w_in, attn_w_out, relpos_table, q_norm_a, k_norm_a, q_norm_b, k_norm_b, sinks, ssm_w_in, ssm_conv_w, ssm_conv_b, ssm_dt_bias, ssm_a_log, ssm_d, ssm_norm, ssm_w_out, ffn_w_in, ffn_conv_w, ffn_conv_b, ffn_w_out, loss_target, m_norm_mix, m_norm_ffn, m_attn_w_in, m_attn_w_out, m_relpos_table, m_q_norm_a, m_k_norm_a, m_q_norm_b, m_k_norm_b, m_sinks, m_ssm_w_in, m_ssm_conv_w, m_ssm_conv_b, m_ssm_dt_bias, m_ssm_a_log, m_ssm_d, m_ssm_norm, m_ssm_w_out, m_ffn_w_in, m_ffn_conv_w, m_ffn_conv_b, m_ffn_w_out, v_norm_mix, v_norm_ffn, v_attn_w_in, v_attn_w_out, v_relpos_table, v_q_norm_a, v_k_norm_a, v_q_norm_b, v_k_norm_b, v_sinks, v_ssm_w_in, v_ssm_conv_w, v_ssm_conv_b, v_ssm_dt_bias, v_ssm_a_log, v_ssm_d, v_ssm_norm, v_ssm_w_out, v_ffn_w_in, v_ffn_conv_w, v_ffn_conv_b, v_ffn_w_out):
    d = dict(zip(ARGS, (x, norm_mix, norm_ffn, attn_w_in, attn_w_out, relpos_table, q_norm_a, k_norm_a, q_norm_b, k_norm_b, sinks, ssm_w_in, ssm_conv_w, ssm_conv_b, ssm_dt_bias, ssm_a_log, ssm_d, ssm_norm, ssm_w_out, ffn_w_in, ffn_conv_w, ffn_conv_b, ffn_w_out, loss_target, m_norm_mix, m_norm_ffn, m_attn_w_in, m_attn_w_out, m_relpos_table, m_q_norm_a, m_k_norm_a, m_q_norm_b, m_k_norm_b, m_sinks, m_ssm_w_in, m_ssm_conv_w, m_ssm_conv_b, m_ssm_dt_bias, m_ssm_a_log, m_ssm_d, m_ssm_norm, m_ssm_w_out, m_ffn_w_in, m_ffn_conv_w, m_ffn_conv_b, m_ffn_w_out, v_norm_mix, v_norm_ffn, v_attn_w_in, v_attn_w_out, v_relpos_table, v_q_norm_a, v_k_norm_a, v_q_norm_b, v_k_norm_b, v_sinks, v_ssm_w_in, v_ssm_conv_w, v_ssm_conv_b, v_ssm_dt_bias, v_ssm_a_log, v_ssm_d, v_ssm_norm, v_ssm_w_out, v_ffn_w_in, v_ffn_conv_w, v_ffn_conv_b, v_ffn_w_out)))
    xi, yi, ci = _pos()
    me = 2 * xi + yi
    csel = jnp.full((1, LANES), ci, F32)
    msel = jnp.full((1, LANES), me, F32)

    halves = lambda w: w.reshape((2, -1, w.shape[-1]))
    small_sh = jnp.concatenate([d["ssm_conv_w"].reshape(-1), d["ssm_conv_b"].reshape(-1), d["ssm_norm"].reshape(-1),
                                d["ffn_conv_w"].reshape(-1)])
    n_small = small_sh.shape[0]
    small_sh = jnp.pad(small_sh, (0, 2 * 40 * LANES - n_small)).reshape(2, 40, LANES)
    shards = [halves(d["attn_w_in"][0].astype(BF16)), halves(d["attn_w_out"][0].astype(BF16)),
              halves(d["ssm_w_in"][0].astype(BF16)), halves(d["ssm_w_out"][0].astype(BF16)),
              d["ffn_w_in"].astype(BF16), d["ffn_w_out"].astype(BF16), small_sh]
    gathered = gather_chips(shards, name="gather_weights")
    g_ai, g_ao, g_si, g_so, g_fi, g_fo, g_sm = [lax.dynamic_update_slice_in_dim(g, s[None], me, axis=0)
                                                for g, s in zip(gathered, shards)]
    cat_cols = lambda g: jnp.concatenate([g[j].reshape((-1, g.shape[-1])) for j in range(N_CHIPS)], axis=1)
    W = {"attn_in": cat_cols(g_ai), "attn_out": g_ao.reshape(-1, D_MODEL),
         "ssm_in": cat_cols(g_si), "ssm_out": g_so.reshape(-1, D_MODEL),
         "ffn_in": [jnp.concatenate([g_fi[j, l] for j in range(N_CHIPS)], axis=1) for l in range(2)],
         "ffn_out": [g_fo[:, l].reshape(-1, D_MODEL) for l in range(2)]}
    sm = g_sm.reshape(N_CHIPS, -1)
    CC = D_INNER + 2 * SSM_GROUPS * SSM_STATE
    c4, f4 = CC // N_CHIPS, D_FF // N_CHIPS
    o1 = SSM_CONV * c4
    o2 = o1 + c4
    o3 = o2 + D_INNER // N_CHIPS
    o4 = o3 + 2 * FFN_CONV * f4
    P = {k: d[k] for k in ["norm_mix", "norm_ffn", "q_norm_a", "k_norm_a", "q_norm_b", "k_norm_b", "sinks",
                           "ssm_dt_bias", "ssm_a_log", "ssm_d", "ffn_conv_b"]}
    P["relpos_table"] = d["relpos_table"][0]
    P["ssm_conv_w"] = sm[:, :o1].reshape(N_CHIPS, SSM_CONV, c4).transpose(1, 0, 2).reshape(SSM_CONV, CC)
    P["ssm_conv_b"] = sm[:, o1:o2].reshape(1, CC)
    P["ssm_norm"] = sm[:, o2:o3].reshape(1, D_INNER)
    P["ffn_conv_w"] = sm[:, o3:o4].reshape(N_CHIPS, 2, FFN_CONV, f4).transpose(1, 2, 0, 3).reshape(2, FFN_CONV, D_FF)

    lpart, dx, gW, gP = local_step(d["x"][0], d["loss_target"][0], W, P)
    loss = lax.psum(lpart[0, 0], ("x", "y", "c"))

    slabs = [_cols_to_slabs(gW["attn_in"]), _rows_to_slabs(gW["attn_out"]), _cols_to_slabs(gW["ssm_in"]),
             _rows_to_slabs(gW["ssm_out"]), _cols_to_slabs(gW["ffn_in"][0]), _cols_to_slabs(gW["ffn_in"][1]),
             _rows_to_slabs(gW["ffn_out"][0]), _rows_to_slabs(gW["ffn_out"][1])]
    tags = ["ai", "ao", "si", "so", "fi0", "fi1", "fo0", "fo1"]
    from_sib = pair_swap_halves(slabs, name="grad_pair_swap")
    pairs = [pair_add(g, r, csel, name="grad_pair_add_" + t) for g, r, t in zip(slabs, from_sib, tags)]
    from_chips = scatter_chips([p[1] for p in pairs], name="grad_scatter")
    mine = [chip_add(p[0], r, msel, name="grad_chip_add_" + t) for p, r, t in zip(pairs, from_chips, tags)]
    theirs = pair_share(mine, name="grad_pair_share")
    full = [jnp.where(ci == 0, jnp.stack([a, b]), jnp.stack([b, a])) for a, b in zip(mine, theirs)]
    two = lambda a: a.reshape((-1, a.shape[-1]))
    grads = {"attn_w_in": two(full[0])[None], "attn_w_out": two(full[1])[None],
             "ssm_w_in": two(full[2])[None], "ssm_w_out": two(full[3])[None],
             "ffn_w_in": jnp.stack([two(full[4]), two(full[5])]),
             "ffn_w_out": jnp.stack([two(full[6]), two(full[7])])}

    flat = jnp.concatenate([gP[k].reshape(-1) for k in SMALL_ORDER])
    flat = jnp.pad(flat, (0, SMALL_ROWS * LANES - flat.shape[0])).reshape(SMALL_ROWS, LANES)
    tot = sum_slots(gather_all(flat, name="small_gather"), name="small_sum").reshape(-1)
    off = 0
    for k in SMALL_ORDER:
        n = int(np.prod(gP[k].shape))
        g = tot[off:off + n].reshape(gP[k].shape)
        off += n
        if k == "ssm_conv_w":
            g = lax.dynamic_slice_in_dim(g, me * c4, c4, axis=1)[None]
        elif k == "ssm_conv_b":
            g = lax.dynamic_slice_in_dim(g, me * c4, c4, axis=1)
        elif k == "ssm_norm":
            g = lax.dynamic_slice_in_dim(g, me * (D_INNER // N_CHIPS), D_INNER // N_CHIPS, axis=1)
        elif k == "ffn_conv_w":
            g = lax.dynamic_slice_in_dim(g, me * f4, f4, axis=2)
        elif k == "relpos_table":
            g = g[None]
        grads[k] = g

    deltas, new_m, new_v = {}, {}, {}
    for k in WEIGHTS:
        deltas[k], new_m[k], new_v[k] = _adamw(d[k], grads[k], d["m_" + k], d["v_" + k], k)
    return (loss, dx[None], *[grads[k] for k in WEIGHTS], *[deltas[k] for k in WEIGHTS],
            *[new_m[k] for k in WEIGHTS], *[new_v[k] for k in WEIGHTS])
```

```python
import functools

import numpy as np
import jax
import jax.numpy as jnp
from jax import lax
from jax.experimental import pallas as pl
from jax.experimental.pallas import tpu as pltpu

F32 = jnp.float32
BF16 = jnp.bfloat16
HI = lax.Precision.HIGHEST

D_MODEL = 1024
CHUNK = 64
EPS = 1e-6
HEAD_DIM = 64
N_HEADS = 8
A_PREV = 8
B_PREV = 2
MAX_REL = 256
D_INNER = 2048
SSM_HEADS = 32
SSM_GROUPS = 4
SSM_STATE = 128
SSM_CONV = 4
D_FF = 2816
FFN_CONV = 3
LANES = 128
SUBLANES = 8
VMEM_LIMIT = 56 * 1024 * 1024
SSD_L = 128

ADAM_LR = 0.001
ADAM_B1 = 0.9
ADAM_B2 = 0.999
ADAM_EPS = 1e-08
ADAM_WD = 0.01
ADAM_STEP = 10

MESH = pl.DeviceIdType.MESH


def _params(*sem):
    return pltpu.CompilerParams(dimension_semantics=sem, vmem_limit_bytes=VMEM_LIMIT)


def _pick(n, want):
    if n <= want:
        return n
    t = (want // LANES) * LANES
    while t >= LANES:
        if n % t == 0:
            return t
        t -= LANES
    return n


MM_ROWS = 512
MM_COLS = 1536
MM_RED = 512


def matmul(a, b, *, mode, name, out_dtype=F32, residual=None):
    dims = {"nn": (((1,), (0,)), ((), ())), "nt": (((1,), (1,)), ((), ())), "tn": (((0,), (0,)), ((), ()))}[mode]
    if mode == "tn":
        assert residual is None and out_dtype == F32
        (K, M), (K2, N) = a.shape, b.shape
        assert K == K2, (a.shape, b.shape)
        tm, tn, tk = _pick(M, MM_COLS), _pick(N, MM_COLS), _pick(K, MM_RED)

        def body(a_ref, b_ref, o_ref):
            k = pl.program_id(2)
            p = lax.dot_general(a_ref[...].astype(BF16), b_ref[...].astype(BF16), dims, preferred_element_type=F32)

            @pl.when(k == 0)
            def _():
                o_ref[...] = p

            @pl.when(k != 0)
            def _():
                o_ref[...] += p

        return pl.pallas_call(
            body, name=name, grid=(M // tm, N // tn, K // tk),
            in_specs=[pl.BlockSpec((tk, tm), lambda i, j, k: (k, i)), pl.BlockSpec((tk, tn), lambda i, j, k: (k, j))],
            out_specs=pl.BlockSpec((tm, tn), lambda i, j, k: (i, j)),
            out_shape=jax.ShapeDtypeStruct((M, N), F32),
            compiler_params=_params("parallel", "parallel", "arbitrary"),
        )(a, b)

    if mode == "nn":
        (M, K), (K2, N) = a.shape, b.shape
    else:
        (M, K), (N, K2) = a.shape, b.shape
    assert K == K2, (a.shape, b.shape, mode)
    tm, tn = _pick(M, MM_ROWS), _pick(N, MM_COLS)

    def body(*refs):
        a_ref, b_ref = refs[:2]
        o_ref = refs[-1]
        r = lax.dot_general(a_ref[...].astype(BF16), b_ref[...].astype(BF16), dims, preferred_element_type=F32)
        if residual is not None:
            r = r + refs[2][...].astype(F32)
        o_ref[...] = r.astype(o_ref.dtype)

    a_spec = pl.BlockSpec((tm, K), lambda j, i: (i, 0))
    b_spec = pl.BlockSpec((K, tn), lambda j, i: (0, j)) if mode == "nn" else pl.BlockSpec((tn, K), lambda j, i: (j, 0))
    o_spec = pl.BlockSpec((tm, tn), lambda j, i: (i, j))
    in_specs = [a_spec, b_spec] + ([o_spec] if residual is not None else [])
    args = (a, b) + ((residual,) if residual is not None else ())
    return pl.pallas_call(
        body, name=name, grid=(N // tn, M // tm),
        in_specs=in_specs, out_specs=o_spec,
        out_shape=jax.ShapeDtypeStruct((M, N), out_dtype),
        compiler_params=_params("parallel", "parallel"),
    )(*args)


def rowwise(f, rows, params, outs, *, name, tm=256):
    S = rows[0].shape[0]
    tm = _row_tile(S, tm)
    nr, npar = len(rows), len(params)

    def body(*refs):
        vals = [r[...] for r in refs[:nr + npar]]
        res = f(*vals)
        for o_ref, r in zip(refs[nr + npar:], res):
            o_ref[...] = r.astype(o_ref.dtype)

    in_specs = [pl.BlockSpec((tm, r.shape[1]), lambda i: (i, 0)) for r in rows]
    in_specs += [pl.BlockSpec(p.shape, lambda i: (0, 0)) for p in params]
    out_specs = [pl.BlockSpec((tm, c), lambda i: (i, 0)) for c, _ in outs]
    out_shape = [jax.ShapeDtypeStruct((S, c), dt) for c, dt in outs]
    return pl.pallas_call(body, name=name, grid=(S // tm,), in_specs=in_specs, out_specs=out_specs,
                          out_shape=out_shape, compiler_params=_params("parallel"))(*rows, *params)


def rowwise_vjp(f, rows, params, cots, drow, dpar, *, name, tm=256):
    S = rows[0].shape[0]
    tm = _row_tile(S, tm)
    nr, npar, nc = len(rows), len(params), len(cots)

    def body(*refs):
        vals = [r[...] for r in refs[:nr + npar]]
        cvals = [r[...].astype(F32) for r in refs[nr + npar:nr + npar + nc]]
        o_refs = refs[nr + npar + nc:]
        want = [ri for ri, _ in drow] + [nr + pi for pi in dpar]

        def f_want(*d):
            full = list(vals)
            for k, v in zip(want, d):
                full[k] = v
            return f(*full)

        _, vjp = jax.vjp(f_want, *[vals[k] for k in want])
        grads = vjp(tuple(cvals))
        for o_ref, g in zip(o_refs[:len(drow)], grads):
            o_ref[...] = g.astype(o_ref.dtype)
        first = pl.program_id(0) == 0
        for o_ref, g in zip(o_refs[len(drow):], grads[len(drow):]):
            g = g.astype(F32)

            @pl.when(first)
            def _(o_ref=o_ref, g=g):
                o_ref[...] = g

            @pl.when(jnp.logical_not(first))
            def _(o_ref=o_ref, g=g):
                o_ref[...] += g

    in_specs = [pl.BlockSpec((tm, r.shape[1]), lambda i: (i, 0)) for r in rows]
    in_specs += [pl.BlockSpec(p.shape, lambda i: (0, 0)) for p in params]
    in_specs += [pl.BlockSpec((tm, c.shape[1]), lambda i: (i, 0)) for c in cots]
    out_specs = [pl.BlockSpec((tm, rows[ri].shape[1]), lambda i: (i, 0)) for ri, _ in drow]
    out_specs += [pl.BlockSpec(params[pi].shape, lambda i: (0, 0)) for pi in dpar]
    out_shape = [jax.ShapeDtypeStruct(rows[ri].shape, dt) for ri, dt in drow]
    out_shape += [jax.ShapeDtypeStruct(params[pi].shape, F32) for pi in dpar]
    return pl.pallas_call(body, name=name, grid=(S // tm,), in_specs=in_specs, out_specs=out_specs,
                          out_shape=out_shape, compiler_params=_params("arbitrary"))(*rows, *params, *cots)


HALO = SUBLANES


def dwconv_fwd(x, w, b, post, extra, outs, *, name, tm=256):
    S, C = x.shape
    K = w.shape[0]
    tm = min(tm, S)
    hb = tm // HALO
    ne = len(extra)

    def body(*refs):
        x_ref, halo_ref, w_ref, b_ref = refs[:4]
        e_refs = refs[4:4 + ne]
        o_refs = refs[4 + ne:4 + ne + len(outs)]
        buf = refs[-1]
        i = pl.program_id(0)
        buf[0:HALO, :] = jnp.where(i == 0, 0.0, halo_ref[...])
        buf[HALO:HALO + tm, :] = x_ref[...]
        for c0 in range(0, C, LANES):
            cs = slice(c0, c0 + LANES)
            acc = jnp.broadcast_to(b_ref[:, cs], (tm, LANES))
            for k in range(K):
                acc = acc + w_ref[k:k + 1, cs] * buf[pl.ds(HALO - (K - 1) + k, tm), cs]
            for o_ref, r in zip(o_refs, post(acc, *[e[:, cs] for e in e_refs])):
                o_ref[:, cs] = r.astype(o_ref.dtype)

    row = pl.BlockSpec((tm, C), lambda i: (i, 0))
    return pl.pallas_call(
        body, name=name, grid=(S // tm,),
        in_specs=[row,
                  pl.BlockSpec((HALO, C), lambda i: (jnp.maximum(i * hb - 1, 0), 0)),
                  pl.BlockSpec((K, C), lambda i: (0, 0)),
                  pl.BlockSpec((1, C), lambda i: (0, 0))] + [row] * ne,
        out_specs=[row] * len(outs),
        out_shape=[jax.ShapeDtypeStruct((S, C), dt) for dt in outs],
        scratch_shapes=[pltpu.VMEM((HALO + tm, C), F32)],
        compiler_params=_params("parallel"),
    )(x, x, w, b, *extra)


def dwconv_bwd(x, w, srcs, dy_fn, extra_outs, *, name, tm=256):
    S, C = x.shape
    K = w.shape[0]
    tm = min(tm, S)
    hb = tm // HALO
    n = S // tm
    groups = [s if isinstance(s, tuple) else (s,) for s in srcs]
    flat = [a for g in groups for a in g]
    nf = len(flat)

    def body(*refs):
        x_ref, xh_ref, w_ref = refs[:3]
        dx_ref, dw_ref, db_ref = refs[3 + 2 * nf:6 + 2 * nf]
        e_refs = refs[6 + 2 * nf:6 + 2 * nf + len(extra_outs)]
        bx, bd = refs[-2:]

        def strips(first, c0):
            out, at = [], first
            for g in groups:
                off = 0
                for a in g:
                    if off <= c0 < off + a.shape[1]:
                        out.append(refs[at][:, c0 - off:c0 - off + LANES].astype(F32))
                    off += a.shape[1]
                    at += 1
            return out

        i = pl.program_id(0)
        bx[0:HALO, :] = jnp.where(i == 0, 0.0, xh_ref[...])
        bx[HALO:HALO + tm, :] = x_ref[...]

        @pl.when(i == 0)
        def _():
            dw_ref[...] = jnp.zeros_like(dw_ref)
            db_ref[...] = jnp.zeros_like(db_ref)

        for c0 in range(0, C, LANES):
            cs = slice(c0, c0 + LANES)
            res = dy_fn(*strips(3, c0))
            dyv = res[0]
            for e_ref, r in zip(e_refs, res[1:]):
                e_ref[:, cs] = r.astype(e_ref.dtype)
            bd[0:tm, cs] = dyv
            bd[tm:tm + HALO, cs] = jnp.where(i == n - 1, 0.0, dy_fn(*strips(3 + nf, c0))[0])
            acc = jnp.zeros((tm, LANES), F32)
            for k in range(K):
                acc = acc + w_ref[k:k + 1, cs] * bd[pl.ds((K - 1) - k, tm), cs]
            dx_ref[:, cs] = acc
            for k in range(K):
                dw_ref[k:k + 1, cs] += jnp.sum(dyv * bx[pl.ds(HALO - (K - 1) + k, tm), cs], axis=0, keepdims=True)
            db_ref[:, cs] += jnp.sum(dyv, axis=0, keepdims=True)

    row = lambda c: pl.BlockSpec((tm, c), lambda i: (i, 0))
    nxt = lambda c: pl.BlockSpec((HALO, c), lambda i: (jnp.minimum((i + 1) * hb, S // HALO - 1), 0))
    return pl.pallas_call(
        body, name=name, grid=(n,),
        in_specs=[row(C), pl.BlockSpec((HALO, C), lambda i: (jnp.maximum(i * hb - 1, 0), 0)),
                  pl.BlockSpec((K, C), lambda i: (0, 0))]
                 + [row(a.shape[1]) for a in flat] + [nxt(a.shape[1]) for a in flat],
        out_specs=[row(C), pl.BlockSpec((K, C), lambda i: (0, 0)), pl.BlockSpec((1, C), lambda i: (0, 0))]
                  + [row(C)] * len(extra_outs),
        out_shape=[jax.ShapeDtypeStruct((S, C), F32), jax.ShapeDtypeStruct((K, C), F32),
                   jax.ShapeDtypeStruct((1, C), F32)] + [jax.ShapeDtypeStruct((S, C), dt) for dt in extra_outs],
        scratch_shapes=[pltpu.VMEM((HALO + tm, C), F32), pltpu.VMEM((tm + HALO, C), F32)],
        compiler_params=_params("arbitrary"),
    )(x, x, w, *flat, *flat)


def _sigmoid(x):
    return 0.5 * jnp.tanh(0.5 * x) + 0.5


def _silu(x):
    return x * _sigmoid(x)


def _dsilu(x):
    s = _sigmoid(x)
    return s * (1.0 + x * (1.0 - s))


def f_rmsnorm(x, g):
    return (x * lax.rsqrt(jnp.mean(x * x, axis=-1, keepdims=True) + EPS) * g,)


SEL = lax.Precision.HIGH


def _group_norm(x, bd, width):
    ms = jnp.dot(x * x, bd, precision=SEL, preferred_element_type=F32) * (1.0 / width)
    return x * lax.rsqrt(ms + EPS)


def f_qknorm(qkv, gqa, gka, gqb, gkb, bd512, bd128, fold, expand):
    dq = N_HEADS * HEAD_DIM
    qa, ka, va, qb = (qkv[:, i * dq:(i + 1) * dq] for i in range(4))
    kb = qkv[:, 4 * dq:4 * dq + LANES]
    vb = qkv[:, 4 * dq + LANES:4 * dq + 2 * LANES]
    tile8 = lambda g: jnp.dot(g, fold, precision=HI, preferred_element_type=F32)
    qa = _group_norm(qa, bd512, HEAD_DIM) * tile8(gqa)
    ka = _group_norm(ka, bd512, HEAD_DIM) * tile8(gka)
    qb = _group_norm(qb, bd512, HEAD_DIM) * tile8(gqb)
    kb = _group_norm(kb, bd128, HEAD_DIM) * tile8(gkb)[:, :LANES]
    kb = jnp.dot(kb, expand, precision=SEL, preferred_element_type=F32)
    vb = jnp.dot(vb, expand, precision=SEL, preferred_element_type=F32)
    return qa, ka, va, qb, kb, vb


def f_gate_norm(y, z, nw):
    v = y * _silu(z)
    gw = D_INNER // SSM_GROUPS
    parts = []
    for g in range(SSM_GROUPS):
        vg = v[:, g * gw:(g + 1) * gw]
        parts.append(vg * lax.rsqrt(jnp.mean(vg * vg, axis=-1, keepdims=True) + EPS))
    return (jnp.concatenate(parts, axis=-1) * nw,)


ATT_TQ = 256
_NT = (((1,), (1,)), ((), ()))
_TN = (((0,), (0,)), ((), ()))


def _attn_probs(qh, kb, bias, valid, snk):
    s = lax.dot_general(qh, kb, _NT, preferred_element_type=F32) * (HEAD_DIM ** -0.5) + bias
    s = jnp.where(valid, s, -jnp.inf)
    m = jnp.max(s, axis=1, keepdims=True)
    if snk is not None:
        m = jnp.maximum(m, snk)
    e = jnp.exp(s - m)
    den = jnp.sum(e, axis=1, keepdims=True)
    if snk is None:
        return e / den, None
    es = jnp.exp(snk - m)
    den = den + es
    return e / den, es / den


def widen_bias(bias, n_prev, nj):
    band = (n_prev + 1) * CHUNK
    wk = (nj + n_prev) * CHUNK
    rows = [jnp.pad(bias, ((0, 0), (0, 0), (j * CHUNK, wk - band - j * CHUNK)), constant_values=-jnp.inf)
            for j in range(nj)]
    return jnp.concatenate(rows, axis=1)


def fold_bias(dbw, n_prev, nj):
    band = (n_prev + 1) * CHUNK
    acc = dbw[:, :CHUNK, :band]
    for j in range(1, nj):
        acc = acc + dbw[:, j * CHUNK:(j + 1) * CHUNK, j * CHUNK:j * CHUNK + band]
    return acc


def attn_fwd(q, k, v, bias_w, sinks, *, n_prev, name):
    S = q.shape[0]
    pad = n_prev * CHUNK
    tq = min(ATT_TQ, S)
    wk = tq + pad
    assert bias_w.shape == (N_HEADS, tq, wk), bias_w.shape
    has_sink = sinks is not None

    def body(*refs):
        if has_sink:
            q_ref, k_ref, v_ref, bias_ref, sink_ref, o_ref = refs
        else:
            q_ref, k_ref, v_ref, bias_ref, o_ref = refs
        start = pl.multiple_of(pl.program_id(1) * tq, tq)
        head0 = lax.broadcasted_iota(jnp.int32, (1, LANES), 1) < HEAD_DIM
        valid = lax.broadcasted_iota(jnp.int32, (1, wk), 1) + start >= pad
        qp = q_ref[...].astype(F32)
        kb = k_ref[pl.ds(start, wk), :]
        vb = v_ref[pl.ds(start, wk), :]
        outs = []
        for r in range(2):
            mh = head0 if r == 0 else jnp.logical_not(head0)
            qh = jnp.where(mh, qp, 0.0).astype(BF16)
            snk = sink_ref[0, r:r + 1, 0:1] if has_sink else None
            p, _ = _attn_probs(qh, kb, bias_ref[r], valid, snk)
            outs.append(jnp.dot(p.astype(BF16), vb, preferred_element_type=F32))
        o_ref[...] = jnp.where(head0, outs[0], outs[1]).astype(o_ref.dtype)

    in_specs = [pl.BlockSpec((tq, LANES), lambda p, i: (i, p)),
                pl.BlockSpec((pad + S, LANES), lambda p, i: (0, p)),
                pl.BlockSpec((pad + S, LANES), lambda p, i: (0, p)),
                pl.BlockSpec((2, tq, wk), lambda p, i: (p, 0, 0))]
    args = [q, k, v, bias_w]
    if has_sink:
        in_specs.append(pl.BlockSpec((1, 2, LANES), lambda p, i: (p, 0, 0)))
        args.append(sinks)
    return pl.pallas_call(
        body, name=name, grid=(N_HEADS // 2, S // tq), in_specs=in_specs,
        out_specs=pl.BlockSpec((tq, LANES), lambda p, i: (i, p)),
        out_shape=jax.ShapeDtypeStruct((S, N_HEADS * HEAD_DIM), BF16),
        compiler_params=_params("parallel", "parallel"),
    )(*args)


def attn_bwd(q, k, v, do, bias_w, sinks, *, n_prev, name):
    S = q.shape[0]
    pad = n_prev * CHUNK
    tq = min(ATT_TQ, S)
    wk = tq + pad
    assert bias_w.shape == (N_HEADS, tq, wk), bias_w.shape
    has_sink = sinks is not None
    scale = HEAD_DIM ** -0.5

    def body(*refs):
        if has_sink:
            q_ref, k_ref, v_ref, do_ref, bias_ref, sink_ref, dq_ref, dk_ref, dv_ref, db_ref, dsk_ref = refs
        else:
            q_ref, k_ref, v_ref, do_ref, bias_ref, dq_ref, dk_ref, dv_ref, db_ref = refs
        i = pl.program_id(1)

        @pl.when(i == 0)
        def _():
            dk_ref[...] = jnp.zeros_like(dk_ref)
            dv_ref[...] = jnp.zeros_like(dv_ref)
            db_ref[...] = jnp.zeros_like(db_ref)
            if has_sink:
                dsk_ref[...] = jnp.zeros_like(dsk_ref)

        start = pl.multiple_of(i * tq, tq)
        head0 = lax.broadcasted_iota(jnp.int32, (1, LANES), 1) < HEAD_DIM
        valid = lax.broadcasted_iota(jnp.int32, (1, wk), 1) + start >= pad
        qp = q_ref[...].astype(F32)
        dop = do_ref[...].astype(F32)
        kb = k_ref[pl.ds(start, wk), :]
        vb = v_ref[pl.ds(start, wk), :]
        dqs = []
        for r in range(2):
            mh = head0 if r == 0 else jnp.logical_not(head0)
            qh = jnp.where(mh, qp, 0.0).astype(BF16)
            doh = jnp.where(mh, dop, 0.0).astype(BF16)
            snk = sink_ref[0, r:r + 1, 0:1] if has_sink else None
            p, ps = _attn_probs(qh, kb, bias_ref[r], valid, snk)
            dp = lax.dot_general(doh, vb, _NT, preferred_element_type=F32)
            delta = jnp.sum(p * dp, axis=1, keepdims=True)
            ds = p * (dp - delta)
            db_ref[r] += ds
            if has_sink:
                dsk = -jnp.sum(ps * delta, axis=0, keepdims=True)
                dsk_ref[0, r:r + 1, :] += jnp.broadcast_to(dsk, (1, LANES))
            dsb = ds.astype(BF16)
            dqs.append(jnp.dot(dsb, kb, preferred_element_type=F32) * scale)
            dk_ref[pl.ds(start, wk), :] += lax.dot_general(dsb, qh, _TN, preferred_element_type=F32) * scale
            dv_ref[pl.ds(start, wk), :] += lax.dot_general(p.astype(BF16), doh, _TN, preferred_element_type=F32)
        dq_ref[...] = jnp.where(head0, dqs[0], dqs[1])

    row_spec = pl.BlockSpec((tq, LANES), lambda p, i: (i, p))
    kv_spec = pl.BlockSpec((pad + S, LANES), lambda p, i: (0, p))
    bias_spec = pl.BlockSpec((2, tq, wk), lambda p, i: (p, 0, 0))
    sink_spec = pl.BlockSpec((1, 2, LANES), lambda p, i: (p, 0, 0))
    in_specs = [row_spec, kv_spec, kv_spec, row_spec, bias_spec]
    args = [q, k, v, do, bias_w]
    out_specs = [row_spec, kv_spec, kv_spec, bias_spec]
    W = N_HEADS * HEAD_DIM
    out_shape = [jax.ShapeDtypeStruct((S, W), F32), jax.ShapeDtypeStruct((pad + S, W), F32),
                 jax.ShapeDtypeStruct((pad + S, W), F32), jax.ShapeDtypeStruct((N_HEADS, tq, wk), F32)]
    if has_sink:
        in_specs.append(sink_spec)
        args.append(sinks)
        out_specs.append(sink_spec)
        out_shape.append(jax.ShapeDtypeStruct((N_HEADS // 2, 2, LANES), F32))
    return pl.pallas_call(
        body, name=name, grid=(N_HEADS // 2, S // tq), in_specs=in_specs, out_specs=out_specs,
        out_shape=out_shape, compiler_params=_params("arbitrary", "arbitrary"),
    )(*args)


HP = SSM_HEADS // 2
PAIRS_PER_GROUP = HP // SSM_GROUPS
HEADS_PER_GROUP = SSM_HEADS // SSM_GROUPS
GW = HEADS_PER_GROUP * 64


def _ssd_dt(dtraw, dtb, A, tril):
    lane = lax.broadcasted_iota(jnp.int32, (1, LANES), 1)
    u = dtraw + dtb
    eu = jnp.exp(-jnp.abs(u))
    w1 = 1.0 + eu
    l1p = jnp.where(w1 == 1.0, eu, jnp.log(w1) * eu / jnp.where(w1 == 1.0, 1.0, w1 - 1.0))
    dt = jnp.where(lane < SSM_HEADS, jnp.maximum(u, 0.0) + l1p, 0.0)
    acs = jnp.dot(tril, dt * A, precision=HI, preferred_element_type=F32)
    return u, dt, acs


def _head_expander():
    hw = D_INNER // SSM_HEADS
    return (np.arange(LANES)[:, None] == np.arange(D_INNER)[None, :] // hw).astype(np.float32)


def ssd_fwd(xbc, dtraw, dtb, A, dexp, *, name):
    S = xbc.shape[0]
    L = min(SSD_L, S)
    nc = S // L
    N = SSM_STATE
    e_mat = jnp.asarray(_head_expander())

    def body(xs_ref, b_ref, c_ref, dtr_ref, dtb_ref, a_ref, d_ref, e_ref, y_ref, st_out_ref, st_ref, xw_ref):
        c = pl.program_id(0)

        @pl.when(c == 0)
        def _():
            st_ref[...] = jnp.zeros_like(st_ref)

        st_out_ref[0] = st_ref[...]
        ri = lax.broadcasted_iota(jnp.int32, (L, L), 0)
        ci = lax.broadcasted_iota(jnp.int32, (L, L), 1)
        trilb = ri >= ci
        head0 = lax.broadcasted_iota(jnp.int32, (1, LANES), 1) < 64
        _, dt, acs = _ssd_dt(dtr_ref[...], dtb_ref[...], a_ref[...], trilb.astype(F32))
        acsT = acs.T
        last = acs[L - 1:L, :]
        expand = lambda t: jnp.dot(t, e_ref[...], precision=SEL, preferred_element_type=F32)
        dte, eae, wte = expand(dt), expand(jnp.exp(acs)), expand(jnp.exp(last - acs) * dt)
        lasts = [last[:, h:h + 1] for h in range(SSM_HEADS)]
        for g in range(SSM_GROUPS):
            Bg = b_ref[:, g * N:(g + 1) * N].astype(BF16)
            Cg = c_ref[:, g * N:(g + 1) * N].astype(BF16)
            CB = lax.dot_general(Cg, Bg, _NT, preferred_element_type=F32)
            Z = lax.dot_general(Cg, st_ref[g * GW:(g + 1) * GW, :].astype(BF16), _NT, preferred_element_type=F32)
            for q in range(PAIRS_PER_GROUP):
                hp = g * PAIRS_PER_GROUP + q
                sl = slice(hp * LANES, (hp + 1) * LANES)
                xs = xs_ref[:, sl]
                xd = xs * dte[:, sl]
                yi = jnp.zeros((L, LANES), F32)
                for r in range(2):
                    h = 2 * hp + r
                    dec = jnp.exp(jnp.where(trilb, acs[:, h:h + 1] - acsT[h:h + 1, :], -jnp.inf))
                    xdh = jnp.where(head0 if r == 0 else jnp.logical_not(head0), xd, 0.0).astype(BF16)
                    yi = yi + jnp.dot((CB * dec).astype(BF16), xdh, preferred_element_type=F32)
                y_ref[:, sl] = yi + Z[:, q * LANES:(q + 1) * LANES] * eae[:, sl] + d_ref[:, sl] * xs
                xw_ref[:, sl] = (xs * wte[:, sl]).astype(BF16)
        for g in range(SSM_GROUPS):
            Bg = b_ref[:, g * N:(g + 1) * N].astype(BF16)
            sn = lax.dot_general(xw_ref[:, g * GW:(g + 1) * GW], Bg, _TN, preferred_element_type=F32)
            for k in range(HEADS_PER_GROUP):
                h = g * HEADS_PER_GROUP + k
                rows = slice(h * 64, (h + 1) * 64)
                st_ref[rows, :] = st_ref[rows, :] * jnp.exp(lasts[h]) + sn[k * 64:(k + 1) * 64, :]

    return pl.pallas_call(
        body, name=name, grid=(nc,),
        in_specs=[pl.BlockSpec((L, D_INNER), lambda c: (c, 0)),
                  pl.BlockSpec((L, SSM_GROUPS * N), lambda c: (c, D_INNER // (SSM_GROUPS * N))),
                  pl.BlockSpec((L, SSM_GROUPS * N), lambda c: (c, D_INNER // (SSM_GROUPS * N) + 1)),
                  pl.BlockSpec((L, LANES), lambda c: (c, 0)),
                  pl.BlockSpec((1, LANES), lambda c: (0, 0)),
                  pl.BlockSpec((1, LANES), lambda c: (0, 0)),
                  pl.BlockSpec((1, D_INNER), lambda c: (0, 0)),
                  pl.BlockSpec((LANES, D_INNER), lambda c: (0, 0))],
        out_specs=[pl.BlockSpec((L, D_INNER), lambda c: (c, 0)),
                   pl.BlockSpec((1, D_INNER, N), lambda c: (c, 0, 0))],
        out_shape=[jax.ShapeDtypeStruct((S, D_INNER), F32), jax.ShapeDtypeStruct((nc, D_INNER, N), F32)],
        scratch_shapes=[pltpu.VMEM((D_INNER, N), F32), pltpu.VMEM((L, D_INNER), BF16)],
        compiler_params=_params("arbitrary"),
    )(xbc, xbc, xbc, dtraw, dtb, A, dexp, e_mat)


def ssd_bwd(xbc, dtraw, dtb, A, dexp, states, dy, *, name):
    S = xbc.shape[0]
    L = min(SSD_L, S)
    nc = S // L
    N = SSM_STATE
    e_np = _head_expander()
    e_mat, et_mat = jnp.asarray(e_np), jnp.asarray(e_np.T)

    def body(xs_ref, b_ref, c_ref, dtr_ref, dtb_ref, a_ref, d_ref, e_ref, et_ref, st_in_ref, dy_ref,
             dxs_ref, db_ref, dc_ref, ddtr_ref, da_ref, ddtb_ref, dd_ref, dst_ref, xw_ref, dz_ref, r_ref,
             dsr_ref, dsc_ref):
        step = pl.program_id(0)

        @pl.when(step == 0)
        def _():
            dst_ref[...] = jnp.zeros_like(dst_ref)
            dsr_ref[...] = jnp.zeros_like(dsr_ref)
            dsc_ref[...] = jnp.zeros_like(dsc_ref)
            da_ref[...] = jnp.zeros_like(da_ref)
            ddtb_ref[...] = jnp.zeros_like(ddtb_ref)
            dd_ref[...] = jnp.zeros_like(dd_ref)

        ri = lax.broadcasted_iota(jnp.int32, (L, L), 0)
        ci = lax.broadcasted_iota(jnp.int32, (L, L), 1)
        trilb = ri >= ci
        lane = lax.broadcasted_iota(jnp.int32, (1, LANES), 1)
        sub = lax.broadcasted_iota(jnp.int32, (LANES, 1), 0)
        head0 = lane < 64
        A = a_ref[...]
        u, dt, acs = _ssd_dt(dtr_ref[...], dtb_ref[...], A, trilb.astype(F32))
        acsT = acs.T
        last = acs[L - 1:L, :]
        elast = jnp.exp(last)
        er = jnp.exp(last - acs)
        wt = er * dt
        expand = lambda t: jnp.dot(t, e_ref[...], precision=SEL, preferred_element_type=F32)
        dte, eae, wte = expand(dt), expand(jnp.exp(acs)), expand(wt)
        dlast = jnp.zeros((1, LANES), F32)
        dcbs = []
        for g in range(SSM_GROUPS):
            Bg = b_ref[:, g * N:(g + 1) * N].astype(BF16)
            Cg = c_ref[:, g * N:(g + 1) * N].astype(BF16)
            stg = st_in_ref[0, g * GW:(g + 1) * GW, :]
            dstg = dst_ref[g * GW:(g + 1) * GW, :]
            CB = lax.dot_general(Cg, Bg, _NT, preferred_element_type=F32)
            CBT = lax.dot_general(Bg, Cg, _NT, preferred_element_type=F32)
            Z = lax.dot_general(Cg, stg.astype(BF16), _NT, preferred_element_type=F32)
            U = lax.dot_general(Bg, dstg.astype(BF16), _NT, preferred_element_type=F32)
            dcb = jnp.zeros((L, L), F32)
            for q in range(PAIRS_PER_GROUP):
                hp = g * PAIRS_PER_GROUP + q
                sl = slice(hp * LANES, (hp + 1) * LANES)
                qs = slice(q * LANES, (q + 1) * LANES)
                xs = xs_ref[:, sl]
                dyp = dy_ref[:, sl]
                dtp, eap, wp, Dp = dte[:, sl], eae[:, sl], wte[:, sl], d_ref[:, sl]
                xd = xs * dtp
                dxd = jnp.zeros((L, LANES), F32)
                for r in range(2):
                    h = 2 * hp + r
                    mh = head0 if r == 0 else jnp.logical_not(head0)
                    dyh = jnp.where(mh, dyp, 0.0).astype(BF16)
                    xdh = jnp.where(mh, xd, 0.0).astype(BF16)
                    seg = acs[:, h:h + 1] - acsT[h:h + 1, :]
                    dec = jnp.exp(jnp.where(trilb, seg, -jnp.inf))
                    decT = jnp.exp(jnp.where(ri <= ci, -seg, -jnp.inf))
                    G = lax.dot_general(dyh, xdh, _NT, preferred_element_type=F32)
                    gd = G * dec
                    dcb = dcb + gd
                    dseg = gd * CB
                    dsr_ref[:, h:h + 1] = jnp.sum(dseg, axis=1, keepdims=True)
                    dsc_ref[h:h + 1, :] = jnp.sum(dseg, axis=0, keepdims=True)
                    dxd = dxd + jnp.dot((CBT * decT).astype(BF16), dyh, preferred_element_type=F32)
                Up = U[:, qs]
                r_ref[0, :, sl] = dyp * Z[:, qs] * eap
                r_ref[1, :, sl] = dxd * xs
                r_ref[2, :, sl] = Up * xs
                dz_ref[:, sl] = (dyp * eap).astype(BF16)
                xw_ref[:, sl] = (xs * wp).astype(BF16)
                dxs_ref[:, sl] = dxd * dtp + Dp * dyp + Up * wp
                dd_ref[:, sl] += jnp.sum(dyp * xs, axis=0, keepdims=True)
            dcbs.append(dcb)
            t = dstg * stg
            for k in range(HEADS_PER_GROUP):
                dlast = dlast + jnp.where(lane == g * HEADS_PER_GROUP + k,
                                          jnp.sum(t[k * 64:(k + 1) * 64, :], keepdims=True), 0.0)
        fold = lambda k: jnp.dot(r_ref[k], et_ref[...], precision=SEL, preferred_element_type=F32)
        r1, r2, dws = fold(0), fold(1), fold(2)
        dww = dws * wt
        ddt = r2 + dws * er
        dacs = r1 - dww + dsr_ref[...] - dsc_ref[...].T
        dlast = dlast * elast + jnp.sum(dww, axis=0, keepdims=True)
        lasts = [last[:, h:h + 1] for h in range(SSM_HEADS)]
        for g in range(SSM_GROUPS):
            Bg = b_ref[:, g * N:(g + 1) * N].astype(BF16)
            Cg = c_ref[:, g * N:(g + 1) * N].astype(BF16)
            gs = slice(g * GW, (g + 1) * GW)
            stb = st_in_ref[0, gs, :].astype(BF16)
            dstb = dst_ref[gs, :].astype(BF16)
            dcbb = dcbs[g].astype(BF16)
            dzg = dz_ref[:, gs]
            dc_ref[:, g * N:(g + 1) * N] = (jnp.dot(dzg, stb, preferred_element_type=F32)
                                            + jnp.dot(dcbb, Bg, preferred_element_type=F32))
            db_ref[:, g * N:(g + 1) * N] = (jnp.dot(xw_ref[:, gs], dstb, preferred_element_type=F32)
                                            + lax.dot_general(dcbb, Cg, _TN, preferred_element_type=F32))
            dsn = lax.dot_general(dzg, Cg, _TN, preferred_element_type=F32)
            for k in range(HEADS_PER_GROUP):
                h = g * HEADS_PER_GROUP + k
                rows = slice(h * 64, (h + 1) * 64)
                dst_ref[rows, :] = dst_ref[rows, :] * jnp.exp(lasts[h]) + dsn[k * 64:(k + 1) * 64, :]
        rowi = lax.broadcasted_iota(jnp.int32, (L, 1), 0)
        dacs = dacs + jnp.where(rowi == L - 1, dlast, 0.0)
        da = jnp.dot((ci >= ri).astype(F32), dacs, precision=HI, preferred_element_type=F32)
        ddt = ddt + da * A
        da_ref[...] += jnp.sum(da * dt, axis=0, keepdims=True)
        ddtr = jnp.where(lane < SSM_HEADS, ddt * _sigmoid(u), 0.0)
        ddtr_ref[...] = ddtr
        ddtb_ref[...] += jnp.sum(ddtr, axis=0, keepdims=True)

    rev = lambda c: nc - 1 - c
    gn = SSM_GROUPS * N
    return pl.pallas_call(
        body, name=name, grid=(nc,),
        in_specs=[pl.BlockSpec((L, D_INNER), lambda c: (rev(c), 0)),
                  pl.BlockSpec((L, gn), lambda c: (rev(c), D_INNER // gn)),
                  pl.BlockSpec((L, gn), lambda c: (rev(c), D_INNER // gn + 1)),
                  pl.BlockSpec((L, LANES), lambda c: (rev(c), 0)),
                  pl.BlockSpec((1, LANES), lambda c: (0, 0)),
                  pl.BlockSpec((1, LANES), lambda c: (0, 0)),
                  pl.BlockSpec((1, D_INNER), lambda c: (0, 0)),
                  pl.BlockSpec((LANES, D_INNER), lambda c: (0, 0)),
                  pl.BlockSpec((D_INNER, LANES), lambda c: (0, 0)),
                  pl.BlockSpec((1, D_INNER, N), lambda c: (rev(c), 0, 0)),
                  pl.BlockSpec((L, D_INNER), lambda c: (rev(c), 0))],
        out_specs=[pl.BlockSpec((L, D_INNER), lambda c: (rev(c), 0)),
                   pl.BlockSpec((L, gn), lambda c: (rev(c), 0)),
                   pl.BlockSpec((L, gn), lambda c: (rev(c), 0)),
                   pl.BlockSpec((L, LANES), lambda c: (rev(c), 0)),
                   pl.BlockSpec((1, LANES), lambda c: (0, 0)),
                   pl.BlockSpec((1, LANES), lambda c: (0, 0)),
                   pl.BlockSpec((1, D_INNER), lambda c: (0, 0))],
        out_shape=[jax.ShapeDtypeStruct((S, D_INNER), F32), jax.ShapeDtypeStruct((S, gn), F32),
                   jax.ShapeDtypeStruct((S, gn), F32), jax.ShapeDtypeStruct((S, LANES), F32),
                   jax.ShapeDtypeStruct((1, LANES), F32), jax.ShapeDtypeStruct((1, LANES), F32),
                   jax.ShapeDtypeStruct((1, D_INNER), F32)],
        scratch_shapes=[pltpu.VMEM((D_INNER, N), F32), pltpu.VMEM((L, D_INNER), BF16), pltpu.VMEM((L, D_INNER), BF16),
                        pltpu.VMEM((3, L, D_INNER), F32), pltpu.VMEM((L, LANES), F32), pltpu.VMEM((LANES, L), F32)],
        compiler_params=_params("arbitrary"),
    )(xbc, xbc, xbc, dtraw, dtb, A, dexp, e_mat, et_mat, states, dy)


BAND_A = (A_PREV + 1) * CHUNK
REL_W = 640


def _relpos_onehot(q):
    u = lax.broadcasted_iota(jnp.int32, (BAND_A, 1), 0)
    idx = jnp.clip(q - u + A_PREV * CHUNK, -MAX_REL, MAX_REL) + MAX_REL
    r = lax.broadcasted_iota(jnp.int32, (1, REL_W), 1)
    return (r == idx).astype(F32)


def relpos_bias(table_pad, *, name):
    def body(t_ref, o_ref):
        oh = _relpos_onehot(pl.program_id(0))
        o_ref[0] = lax.dot_general(t_ref[...], oh, _NT, precision=HI, preferred_element_type=F32)

    return pl.pallas_call(
        body, name=name, grid=(CHUNK,),
        in_specs=[pl.BlockSpec((N_HEADS, REL_W), lambda q: (0, 0))],
        out_specs=pl.BlockSpec((1, N_HEADS, BAND_A), lambda q: (q, 0, 0)),
        out_shape=jax.ShapeDtypeStruct((CHUNK, N_HEADS, BAND_A), F32),
        compiler_params=_params("parallel"),
    )(table_pad)


def relpos_grad(dbias_t, *, name):
    def body(d_ref, o_ref):
        q = pl.program_id(0)

        @pl.when(q == 0)
        def _():
            o_ref[...] = jnp.zeros_like(o_ref)

        o_ref[...] += jnp.dot(d_ref[0], _relpos_onehot(q), precision=HI, preferred_element_type=F32)

    return pl.pallas_call(
        body, name=name, grid=(CHUNK,),
        in_specs=[pl.BlockSpec((1, N_HEADS, BAND_A), lambda q: (q, 0, 0))],
        out_specs=pl.BlockSpec((N_HEADS, REL_W), lambda q: (0, 0)),
        out_shape=jax.ShapeDtypeStruct((N_HEADS, REL_W), F32),
        compiler_params=_params("arbitrary"),
    )(dbias_t)


def loss_head(y, t, *, name, tm=256):
    S, D = y.shape
    tm = min(tm, S)

    def body(y_ref, t_ref, dy_ref, l_ref):
        e = y_ref[...] - t_ref[...]
        dy_ref[...] = e * (1.0 / D)

        @pl.when(pl.program_id(0) == 0)
        def _():
            l_ref[...] = jnp.zeros_like(l_ref)

        part = jnp.sum(jnp.sum(e * e, axis=1, keepdims=True), axis=0, keepdims=True) * (0.5 / D)
        l_ref[...] += jnp.broadcast_to(part, l_ref.shape)

    return pl.pallas_call(
        body, name=name, grid=(S // tm,),
        in_specs=[pl.BlockSpec((tm, D), lambda i: (i, 0))] * 2,
        out_specs=[pl.BlockSpec((tm, D), lambda i: (i, 0)), pl.BlockSpec((1, LANES), lambda i: (0, 0))],
        out_shape=[jax.ShapeDtypeStruct((S, D), F32), jax.ShapeDtypeStruct((1, LANES), F32)],
        compiler_params=_params("arbitrary"),
    )(y, t)


def f_adamw(w, g, m, v):
    m = ADAM_B1 * m + (1.0 - ADAM_B1) * g
    v = ADAM_B2 * v + (1.0 - ADAM_B2) * (g * g)
    m_hat = m / (1.0 - ADAM_B1 ** ADAM_STEP)
    v_hat = v / (1.0 - ADAM_B2 ** ADAM_STEP)
    delta = -ADAM_LR * (m_hat / (jnp.sqrt(v_hat) + ADAM_EPS) + ADAM_WD * w)
    return delta, m, v


def f_norm_id(x, g):
    return f_rmsnorm(x, g)[0], x


ANY = pl.BlockSpec(memory_space=pl.ANY)


def _pos():
    return lax.axis_index("x"), lax.axis_index("y"), lax.axis_index("c")


def _other_chips(x, y):
    return [(1 - x, y), (x, 1 - y), (1 - x, 1 - y)]


def gather_chips(shards, *, name):
    n = len(shards)

    def body(*refs):
        ins, outs = refs[:n], refs[n:2 * n]
        send, recv, fsend, frecv = refs[2 * n:]
        x, y, c = _pos()
        me = 2 * x + y
        sib = (x, y, 1 - c)
        chips = _other_chips(x, y)
        first = []
        for i in range(n):
            for j, (px, py) in enumerate(chips):
                cp = pltpu.make_async_remote_copy(ins[i].at[c], outs[i].at[me, c], send.at[3 * i + j], recv.at[3 * i + j],
                                                  device_id=(px, py, c), device_id_type=MESH)
                cp.start()
                first.append(cp)
        passed = []
        for i in range(n):
            for j, (px, py) in enumerate(chips):
                got = outs[i].at[2 * px + py, c]
                pltpu.make_async_remote_copy(ins[i].at[c], got, send.at[3 * i + j], recv.at[3 * i + j],
                                             device_id=(px, py, c), device_id_type=MESH).wait_recv()
                cp = pltpu.make_async_remote_copy(got, got, fsend.at[3 * i + j], frecv.at[3 * i + j],
                                                  device_id=sib, device_id_type=MESH)
                cp.start()
                passed.append(cp)
        for i in range(n):
            for j, (px, py) in enumerate(chips):
                theirs = outs[i].at[2 * px + py, 1 - c]
                pltpu.make_async_remote_copy(theirs, theirs, fsend.at[3 * i + j], frecv.at[3 * i + j],
                                             device_id=sib, device_id_type=MESH).wait_recv()
        for cp in first + passed:
            cp.wait_send()

    return pl.pallas_call(
        body, name=name, in_specs=[ANY] * n, out_specs=[ANY] * n,
        out_shape=[jax.ShapeDtypeStruct((4,) + s.shape, s.dtype) for s in shards],
        scratch_shapes=[pltpu.SemaphoreType.DMA((3 * n,)), pltpu.SemaphoreType.DMA((3 * n,)),
                        pltpu.SemaphoreType.DMA((3 * n,)), pltpu.SemaphoreType.DMA((3 * n,))],
        compiler_params=pltpu.CompilerParams(has_side_effects=True),
    )(*shards)


def pair_swap_halves(gs, *, name):
    n = len(gs)

    def body(*refs):
        ins, outs = refs[:n], refs[n:2 * n]
        send, recv = refs[2 * n:]
        x, y, c = _pos()
        cps = []
        for i in range(n):
            cp = pltpu.make_async_remote_copy(ins[i].at[1 - c], outs[i], send.at[i], recv.at[i],
                                              device_id=(x, y, 1 - c), device_id_type=MESH)
            cp.start()
            cps.append(cp)
        for cp in cps:
            cp.wait()

    return pl.pallas_call(
        body, name=name, in_specs=[ANY] * n, out_specs=[ANY] * n,
        out_shape=[jax.ShapeDtypeStruct(g.shape[1:], g.dtype) for g in gs],
        scratch_shapes=[pltpu.SemaphoreType.DMA((n,)), pltpu.SemaphoreType.DMA((n,))],
        compiler_params=pltpu.CompilerParams(has_side_effects=True),
    )(*gs)


def scatter_chips(ps, *, name):
    n = len(ps)

    def body(*refs):
        ins, outs = refs[:n], refs[n:2 * n]
        send, recv = refs[2 * n:]
        x, y, c = _pos()
        cps = []
        for i in range(n):
            for j, (px, py) in enumerate(_other_chips(x, y)):
                cp = pltpu.make_async_remote_copy(ins[i].at[2 * px + py], outs[i].at[j], send.at[3 * i + j], recv.at[3 * i + j],
                                                  device_id=(px, py, c), device_id_type=MESH)
                cp.start()
                cps.append(cp)
        for cp in cps:
            cp.wait()

    return pl.pallas_call(
        body, name=name, in_specs=[ANY] * n, out_specs=[ANY] * n,
        out_shape=[jax.ShapeDtypeStruct((3,) + p.shape[1:], p.dtype) for p in ps],
        scratch_shapes=[pltpu.SemaphoreType.DMA((3 * n,)), pltpu.SemaphoreType.DMA((3 * n,))],
        compiler_params=pltpu.CompilerParams(has_side_effects=True),
    )(*ps)


def pair_share(hs, *, name):
    n = len(hs)

    def body(*refs):
        ins, outs = refs[:n], refs[n:2 * n]
        send, recv = refs[2 * n:]
        x, y, c = _pos()
        cps = []
        for i in range(n):
            cp = pltpu.make_async_remote_copy(ins[i], outs[i], send.at[i], recv.at[i],
                                              device_id=(x, y, 1 - c), device_id_type=MESH)
            cp.start()
            cps.append(cp)
        for cp in cps:
            cp.wait()

    return pl.pallas_call(
        body, name=name, in_specs=[ANY] * n, out_specs=[ANY] * n,
        out_shape=[jax.ShapeDtypeStruct(h.shape, h.dtype) for h in hs],
        scratch_shapes=[pltpu.SemaphoreType.DMA((n,)), pltpu.SemaphoreType.DMA((n,))],
        compiler_params=pltpu.CompilerParams(has_side_effects=True),
    )(*hs)


def gather_all(buf, *, name):
    def body(in_ref, out_ref, send, recv, loc):
        x, y, c = _pos()
        lid = 4 * x + 2 * y + c
        lc = pltpu.make_async_copy(in_ref, out_ref.at[lid], loc.at[0])
        lc.start()
        cps = []
        for k in range(1, 8):
            px = 1 - x if k & 4 else x
            py = 1 - y if k & 2 else y
            pc = 1 - c if k & 1 else c
            cp = pltpu.make_async_remote_copy(in_ref, out_ref.at[lid], send.at[k - 1], recv.at[k - 1],
                                              device_id=(px, py, pc), device_id_type=MESH)
            cp.start()
            cps.append((cp, 4 * px + 2 * py + pc, (px, py, pc)))
        for k, (cp, plid, peer) in enumerate(cps):
            cp.wait_send()
            pltpu.make_async_remote_copy(in_ref, out_ref.at[plid], send.at[k], recv.at[k],
                                         device_id=peer, device_id_type=MESH).wait_recv()
        lc.wait()

    return pl.pallas_call(
        body, name=name, in_specs=[ANY], out_specs=ANY,
        out_shape=jax.ShapeDtypeStruct((8,) + buf.shape, buf.dtype),
        scratch_shapes=[pltpu.SemaphoreType.DMA((7,)), pltpu.SemaphoreType.DMA((7,)), pltpu.SemaphoreType.DMA((1,))],
        compiler_params=pltpu.CompilerParams(has_side_effects=True),
    )(buf)


def sum_slots(a, *, name):
    n = a.shape[0]

    def body(a_ref, o_ref):
        acc = a_ref[0]
        for k in range(1, n):
            acc = acc + a_ref[k]
        o_ref[...] = acc

    return pl.pallas_call(body, name=name, out_shape=jax.ShapeDtypeStruct(a.shape[1:], a.dtype),
                          compiler_params=pltpu.CompilerParams(vmem_limit_bytes=VMEM_LIMIT))(a)


def _row_tile(r, want, mult=16):
    t = (min(want, r) // mult) * mult
    while t >= mult:
        if r % t == 0:
            return t
        t -= mult
    return r


def pair_add(g, r1, csel, *, name):
    _, _, r, C = g.shape
    tr = _row_tile(r, 256)

    def body(g_ref, r_ref, c_ref, p32_ref, pb_ref):
        south = c_ref[0:1, 0:1] == 0.0
        p = jnp.where(south, g_ref[0, 0], g_ref[1, 0]) + r_ref[0]
        p32_ref[0] = p
        pb_ref[0] = p.astype(BF16)

    return pl.pallas_call(
        body, name=name, grid=(4, r // tr),
        in_specs=[pl.BlockSpec((2, 1, tr, C), lambda j, t: (0, j, t, 0)), pl.BlockSpec((1, tr, C), lambda j, t: (j, t, 0)),
                  pl.BlockSpec((1, LANES), lambda j, t: (0, 0))],
        out_specs=[pl.BlockSpec((1, tr, C), lambda j, t: (j, t, 0))] * 2,
        out_shape=[jax.ShapeDtypeStruct((4, r, C), F32), jax.ShapeDtypeStruct((4, r, C), BF16)],
        compiler_params=_params("parallel", "parallel"),
    )(g, r1, csel)


def chip_add(p32, r3, msel, *, name):
    _, r, C = p32.shape
    tr = _row_tile(r, 128)

    def body(p_ref, r_ref, m_ref, o_ref):
        me = m_ref[0:1, 0:1]
        acc = jnp.where(me == 0.0, p_ref[0], jnp.where(me == 1.0, p_ref[1], jnp.where(me == 2.0, p_ref[2], p_ref[3])))
        for j in range(3):
            acc = acc + r_ref[j].astype(F32)
        o_ref[...] = acc

    return pl.pallas_call(
        body, name=name, grid=(r // tr,),
        in_specs=[pl.BlockSpec((4, tr, C), lambda t: (0, t, 0)), pl.BlockSpec((3, tr, C), lambda t: (0, t, 0)),
                  pl.BlockSpec((1, LANES), lambda t: (0, 0))],
        out_specs=pl.BlockSpec((tr, C), lambda t: (t, 0)),
        out_shape=jax.ShapeDtypeStruct((r, C), F32),
        compiler_params=_params("parallel"),
    )(p32, r3, msel)


def _consts():
    i512 = np.arange(N_HEADS * HEAD_DIM)
    i128 = np.arange(LANES)
    bd512 = (i512[:, None] // HEAD_DIM == i512[None, :] // HEAD_DIM).astype(np.float32)
    bd128 = (i128[:, None] // HEAD_DIM == i128[None, :] // HEAD_DIM).astype(np.float32)
    fold = (np.arange(HEAD_DIM)[:, None] == (i512[None, :] % HEAD_DIM)).astype(np.float32)
    grp = N_HEADS // 2 * HEAD_DIM
    expand = ((i128[:, None] // HEAD_DIM == i512[None, :] // grp)
              & (i128[:, None] % HEAD_DIM == i512[None, :] % HEAD_DIM)).astype(np.float32)
    band = (B_PREV + 1) * CHUNK
    rel = np.arange(CHUNK)[:, None] - (np.arange(band)[None, :] - B_PREV * CHUNK)
    slopes = 2.0 ** (-8.0 * np.arange(1, N_HEADS + 1, dtype=np.float32) / N_HEADS)
    bias_b = (-slopes[:, None, None] * np.abs(rel).astype(np.float32)[None]).astype(np.float32)
    return [jnp.asarray(a) for a in (bd512, bd128, fold, expand)], jnp.asarray(bias_b)


def _ffn_fwd(xin, l, W, P):
    g = P["norm_ffn"][l:l + 1]
    (h,) = rowwise(f_rmsnorm, [xin], [g], [(D_MODEL, BF16)], name=f"ffn{l}_norm")
    Wi = W["ffn_in"][l]
    gate = matmul(h, Wi[:, :D_FF], mode="nn", name=f"ffn{l}_gate")
    up = matmul(h, Wi[:, D_FF:], mode="nn", name=f"ffn{l}_up")
    gc, act = dwconv_fwd(gate, P["ffn_conv_w"][l], P["ffn_conv_b"][l:l + 1], lambda y, u: (y, _silu(y) * u), [up],
                         [F32, BF16], name=f"ffn{l}_conv")
    xout = matmul(act, W["ffn_out"][l], mode="nn", name=f"ffn{l}_out", residual=xin)
    return xout, (xin, h, gate, gc, up, act)


def _ffn_bwd(dxout, l, saved, W, P):
    xin, h, gate, gc, up, act = saved
    g = P["norm_ffn"][l:l + 1]
    Wi = W["ffn_in"][l]
    dact = matmul(dxout, W["ffn_out"][l], mode="nt", name=f"ffn{l}_dact")
    dWo = matmul(act, dxout, mode="tn", name=f"ffn{l}_dwout")
    dgate, dcw, dcb, dup = dwconv_bwd(gate, P["ffn_conv_w"][l], [gc, up, dact],
                                      lambda c, u, da: (da * u * _dsilu(c), da * _silu(c)), [BF16],
                                      name=f"ffn{l}_dconv")
    dh = matmul(dgate, Wi[:, :D_FF], mode="nt", name=f"ffn{l}_dh_gate")
    dh = matmul(dup, Wi[:, D_FF:], mode="nt", name=f"ffn{l}_dh_up", residual=dh)
    dWi = jnp.concatenate([matmul(h, dgate, mode="tn", name=f"ffn{l}_dw_gate"),
                           matmul(h, dup, mode="tn", name=f"ffn{l}_dw_up")], axis=1)
    dxin, dg = rowwise_vjp(f_norm_id, [xin], [g], [dh, dxout], [(0, F32)], [0], name=f"ffn{l}_dnorm")
    return dxin, dWi, dWo, dg, dcw, dcb


def local_step(x, tgt, W, P):
    qk_consts, bias_b = _consts()
    pad_rows = lambda t, n: jnp.pad(t, ((n * CHUNK, 0), (0, 0)))
    DQ = N_HEADS * HEAD_DIM

    g_mix0 = P["norm_mix"][0:1]
    (h0,) = rowwise(f_rmsnorm, [x], [g_mix0], [(D_MODEL, BF16)], name="attn_norm")
    qkv = matmul(h0, W["attn_in"], mode="nn", name="attn_qkv")
    qk_par = [P["q_norm_a"], P["k_norm_a"], P["q_norm_b"], P["k_norm_b"]] + qk_consts
    qa, ka, va, qb, kb, vb = rowwise(f_qknorm, [qkv], qk_par, [(DQ, BF16)] * 6, name="attn_qknorm")
    ka, va, kb, vb = pad_rows(ka, A_PREV), pad_rows(va, A_PREV), pad_rows(kb, B_PREV), pad_rows(vb, B_PREV)
    table = jnp.pad(P["relpos_table"], ((0, 0), (0, REL_W - (2 * MAX_REL + 1))))
    nj = min(ATT_TQ, x.shape[0]) // CHUNK
    bias_a = widen_bias(jnp.transpose(relpos_bias(table, name="relpos_bias"), (1, 0, 2)), A_PREV, nj)
    bias_b = widen_bias(bias_b, B_PREV, nj)
    sinks = jnp.broadcast_to(P["sinks"].reshape(N_HEADS // 2, 2, 1), (N_HEADS // 2, 2, LANES))
    oa = attn_fwd(qa, ka, va, bias_a, None, n_prev=A_PREV, name="attn_a")
    ob = attn_fwd(qb, kb, vb, bias_b, sinks, n_prev=B_PREV, name="attn_b")
    Wao = W["attn_out"]
    x1 = matmul(oa, Wao[:DQ], mode="nn", name="attn_out_a", residual=x)
    x1 = matmul(ob, Wao[DQ:], mode="nn", name="attn_out_b", residual=x1)
    x2, ffn0 = _ffn_fwd(x1, 0, W, P)

    g_mix1 = P["norm_mix"][1:2]
    (h2,) = rowwise(f_rmsnorm, [x2], [g_mix1], [(D_MODEL, BF16)], name="ssm_norm_in")
    Ws = W["ssm_in"]
    CC = D_INNER + 2 * SSM_GROUPS * SSM_STATE
    Wz, Wx = Ws[:, :D_INNER], Ws[:, D_INNER:D_INNER + CC]
    Wdt = jnp.pad(Ws[:, D_INNER + CC:], ((0, 0), (0, LANES - SSM_HEADS)))
    z = matmul(h2, Wz, mode="nn", name="ssm_z")
    xr = matmul(h2, Wx, mode="nn", name="ssm_xbc")
    dtraw = matmul(h2, Wdt, mode="nn", name="ssm_dt")
    xc, xbc = dwconv_fwd(xr, P["ssm_conv_w"], P["ssm_conv_b"], lambda y: (y, _silu(y)), [], [F32, F32],
                         name="ssm_conv")
    pad32 = lambda v: jnp.pad(v, ((0, 0), (0, LANES - SSM_HEADS)))
    A = pad32(-jnp.exp(P["ssm_a_log"]))
    dtb = pad32(P["ssm_dt_bias"])
    dexp = jnp.repeat(P["ssm_d"], D_INNER // SSM_HEADS, axis=1)
    y, states = ssd_fwd(xbc, dtraw, dtb, A, dexp, name="ssd_fwd")
    (y2,) = rowwise(f_gate_norm, [y, z], [P["ssm_norm"]], [(D_INNER, BF16)], name="ssm_gate_norm")
    x3 = matmul(y2, W["ssm_out"], mode="nn", name="ssm_out", residual=x2)
    x4, ffn1 = _ffn_fwd(x3, 1, W, P)

    dx4, lpart = loss_head(x4, tgt, name="loss_head")

    dx3, dWfi1, dWfo1, dgf1, dfcw1, dfcb1 = _ffn_bwd(dx4, 1, ffn1, W, P)
    dy2 = matmul(dx3, W["ssm_out"], mode="nt", name="ssm_dy")
    dWso = matmul(y2, dx3, mode="tn", name="ssm_dwout")
    dy, dz, dnw = rowwise_vjp(f_gate_norm, [y, z], [P["ssm_norm"]], [dy2], [(0, F32), (1, F32)], [0],
                              name="ssm_dgate_norm")
    dxs, dB, dC, ddtraw, dA, ddtb, dDl = ssd_bwd(xbc, dtraw, dtb, A, dexp, states, dy, name="ssd_bwd")
    dxr, dscw, dscb = dwconv_bwd(xr, P["ssm_conv_w"], [xc, (dxs, dB, dC)], lambda c, g: (g * _dsilu(c),), [],
                                 name="ssm_dconv")
    dh2 = matmul(dz, Wz, mode="nt", name="ssm_dh_z")
    dh2 = matmul(dxr, Wx, mode="nt", name="ssm_dh_x", residual=dh2)
    dh2 = matmul(ddtraw, Wdt, mode="nt", name="ssm_dh_dt", residual=dh2)
    dWs = jnp.concatenate([matmul(h2, dz, mode="tn", name="ssm_dw_z"),
                           matmul(h2, dxr, mode="tn", name="ssm_dw_x"),
                           matmul(h2, ddtraw, mode="tn", name="ssm_dw_dt")[:, :SSM_HEADS]], axis=1)
    dx2, dgm1 = rowwise_vjp(f_norm_id, [x2], [g_mix1], [dh2, dx3], [(0, F32)], [0], name="ssm_dnorm_in")

    dx1, dWfi0, dWfo0, dgf0, dfcw0, dfcb0 = _ffn_bwd(dx2, 0, ffn0, W, P)
    doa = matmul(dx1, Wao[:DQ], mode="nt", name="attn_do_a", out_dtype=BF16)
    dob = matmul(dx1, Wao[DQ:], mode="nt", name="attn_do_b", out_dtype=BF16)
    dWao = jnp.concatenate([matmul(oa, dx1, mode="tn", name="attn_dwout_a"),
                            matmul(ob, dx1, mode="tn", name="attn_dwout_b")], axis=0)
    dqa, dka, dva, dbias_a = attn_bwd(qa, ka, va, doa, bias_a, None, n_prev=A_PREV, name="attn_a_bwd")
    dqb, dkb, dvb, _, dsk = attn_bwd(qb, kb, vb, dob, bias_b, sinks, n_prev=B_PREV, name="attn_b_bwd")
    pa, pb = A_PREV * CHUNK, B_PREV * CHUNK
    cots = [dqa, dka[pa:], dva[pa:], dqb, dkb[pb:], dvb[pb:]]
    dqkv, dgqa, dgka, dgqb, dgkb = rowwise_vjp(f_qknorm, [qkv], qk_par, cots, [(0, BF16)], [0, 1, 2, 3],
                                               name="attn_dqknorm")
    dh0 = matmul(dqkv, W["attn_in"], mode="nt", name="attn_dh")
    dWai = matmul(h0, dqkv, mode="tn", name="attn_dwin")
    dx, dgm0 = rowwise_vjp(f_norm_id, [x], [g_mix0], [dh0, dx1], [(0, F32)], [0], name="attn_dnorm")
    dbias_a = fold_bias(dbias_a, A_PREV, nj)
    dtable = relpos_grad(jnp.transpose(dbias_a, (1, 0, 2)), name="relpos_grad")[:, :2 * MAX_REL + 1]

    gW = {"attn_in": dWai, "attn_out": dWao, "ssm_in": dWs, "ssm_out": dWso,
          "ffn_in": [dWfi0, dWfi1], "ffn_out": [dWfo0, dWfo1]}
    gP = {"norm_mix": jnp.concatenate([dgm0, dgm1], axis=0),
          "norm_ffn": jnp.concatenate([dgf0, dgf1], axis=0),
          "relpos_table": dtable, "q_norm_a": dgqa, "k_norm_a": dgka, "q_norm_b": dgqb, "k_norm_b": dgkb,
          "sinks": dsk[:, :, 0].reshape(1, N_HEADS),
          "ssm_conv_w": dscw, "ssm_conv_b": dscb,
          "ssm_dt_bias": ddtb[:, :SSM_HEADS], "ssm_a_log": dA[:, :SSM_HEADS] * A[:, :SSM_HEADS],
          "ssm_d": dDl.reshape(SSM_HEADS, D_INNER // SSM_HEADS).sum(axis=1).reshape(1, SSM_HEADS),
          "ssm_norm": dnw,
          "ffn_conv_w": jnp.stack([dfcw0, dfcw1]), "ffn_conv_b": jnp.concatenate([dfcb0, dfcb1], axis=0)}
    return lpart, dx, gW, gP


WEIGHTS = ["norm_mix", "norm_ffn", "attn_w_in", "attn_w_out", "relpos_table", "q_norm_a", "k_norm_a", "q_norm_b",
           "k_norm_b", "sinks", "ssm_w_in", "ssm_conv_w", "ssm_conv_b", "ssm_dt_bias", "ssm_a_log", "ssm_d",
           "ssm_norm", "ssm_w_out", "ffn_w_in", "ffn_conv_w", "ffn_conv_b", "ffn_w_out"]
ARGS = ["x"] + WEIGHTS + ["loss_target"] + ["m_" + w for w in WEIGHTS] + ["v_" + w for w in WEIGHTS]
N_CHIPS = 4
SMALL_ROWS = 384
SMALL_ORDER = ["norm_mix", "norm_ffn", "relpos_table", "q_norm_a", "k_norm_a", "q_norm_b", "k_norm_b", "sinks",
               "ssm_dt_bias", "ssm_a_log", "ssm_d", "ffn_conv_b", "ssm_conv_w", "ssm_conv_b", "ssm_norm", "ffn_conv_w"]


def _cols_to_slabs(g):
    K, N = g.shape
    return g.reshape(2, K // 2, N_CHIPS, N // N_CHIPS).transpose(0, 2, 1, 3)


def _rows_to_slabs(g):
    R, C = g.shape
    return g.reshape(N_CHIPS, 2, R // (2 * N_CHIPS), C).transpose(1, 0, 2, 3)


def _adamw(w, g, m, v, name):
    shp = w.shape
    two = lambda a: a.reshape((-1, shp[-1]))
    outs = [(shp[-1], F32)] * 3
    d, nm, nv = rowwise(f_adamw, [two(w), two(g), two(m), two(v)], [], outs, name="adamw_" + name)
    return d.reshape(shp), nm.reshape(shp), nv.reshape(shp)


def kernel(x, norm_mix, norm_ffn, attn_w_in, attn_w_out, relpos_table, q_norm_a, k_norm_a, q_norm_b, k_norm_b, sinks, ssm_w_in, ssm_conv_w, ssm_conv_b, ssm_dt_bias, ssm_a_log, ssm_d, ssm_norm, ssm_w_out, ffn_w_in, ffn_conv_w, ffn_conv_b, ffn_w_out, loss_target, m_norm_mix, m_norm_ffn, m_attn_w_in, m_attn_w_out, m_relpos_table, m_q_norm_a, m_k_norm_a, m_q_norm_b, m_k_norm_b, m_sinks, m_ssm_w_in, m_ssm_conv_w, m_ssm_conv_b, m_ssm_dt_bias, m_ssm_a_log, m_ssm_d, m_ssm_norm, m_ssm_w_out, m_ffn_w_in, m_ffn_conv_w, m_ffn_conv_b, m_ffn_w_out, v_norm_mix, v_norm_ffn, v_attn_w_in, v_attn_w_out, v_relpos_table, v_q_norm_a, v_k_norm_a, v_q_norm_b, v_k_norm_b, v_sinks, v_ssm_w_in, v_ssm_conv_w, v_ssm_conv_b, v_ssm_dt_bias, v_ssm_a_log, v_ssm_d, v_ssm_norm, v_ssm_w_out, v_ffn_w_in, v_ffn_conv_w, v_ffn_conv_b, v_ffn_w_out):
    d = dict(zip(ARGS, (x, norm_mix, norm_ffn, attn_w_in, attn_w_out, relpos_table, q_norm_a, k_norm_a, q_norm_b, k_norm_b, sinks, ssm_w_in, ssm_conv_w, ssm_conv_b, ssm_dt_bias, ssm_a_log, ssm_d, ssm_norm, ssm_w_out, ffn_w_in, ffn_conv_w, ffn_conv_b, ffn_w_out, loss_target, m_norm_mix, m_norm_ffn, m_attn_w_in, m_attn_w_out, m_relpos_table, m_q_norm_a, m_k_norm_a, m_q_norm_b, m_k_norm_b, m_sinks, m_ssm_w_in, m_ssm_conv_w, m_ssm_conv_b, m_ssm_dt_bias, m_ssm_a_log, m_ssm_d, m_ssm_norm, m_ssm_w_out, m_ffn_w_in, m_ffn_conv_w, m_ffn_conv_b, m_ffn_w_out, v_norm_mix, v_norm_ffn, v_attn_w_in, v_attn_w_out, v_relpos_table, v_q_norm_a, v_k_norm_a, v_q_norm_b, v_k_norm_b, v_sinks, v_ssm_w_in, v_ssm_conv_w, v_ssm_conv_b, v_ssm_dt_bias, v_ssm_a_log, v_ssm_d, v_ssm_norm, v_ssm_w_out, v_ffn_w_in, v_ffn_conv_w, v_ffn_conv_b, v_ffn_w_out)))
    xi, yi, ci = _pos()
    me = 2 * xi + yi
    csel = jnp.full((1, LANES), ci, F32)
    msel = jnp.full((1, LANES), me, F32)

    halves = lambda w: w.reshape((2, -1, w.shape[-1]))
    small_sh = jnp.concatenate([d["ssm_conv_w"].reshape(-1), d["ssm_conv_b"].reshape(-1), d["ssm_norm"].reshape(-1),
                                d["ffn_conv_w"].reshape(-1)])
    n_small = small_sh.shape[0]
    small_sh = jnp.pad(small_sh, (0, 2 * 40 * LANES - n_small)).reshape(2, 40, LANES)
    shards = [halves(d["attn_w_in"][0].astype(BF16)), halves(d["attn_w_out"][0].astype(BF16)),
              halves(d["ssm_w_in"][0].astype(BF16)), halves(d["ssm_w_out"][0].astype(BF16)),
              d["ffn_w_in"].astype(BF16), d["ffn_w_out"].astype(BF16), small_sh]
    gathered = gather_chips(shards, name="gather_weights")
    g_ai, g_ao, g_si, g_so, g_fi, g_fo, g_sm = [lax.dynamic_update_slice_in_dim(g, s[None], me, axis=0)
                                                for g, s in zip(gathered, shards)]
    cat_cols = lambda g: jnp.concatenate([g[j].reshape((-1, g.shape[-1])) for j in range(N_CHIPS)], axis=1)
    W = {"attn_in": cat_cols(g_ai), "attn_out": g_ao.reshape(-1, D_MODEL),
         "ssm_in": cat_cols(g_si), "ssm_out": g_so.reshape(-1, D_MODEL),
         "ffn_in": [jnp.concatenate([g_fi[j, l] for j in range(N_CHIPS)], axis=1) for l in range(2)],
         "ffn_out": [g_fo[:, l].reshape(-1, D_MODEL) for l in range(2)]}
    sm = g_sm.reshape(N_CHIPS, -1)
    CC = D_INNER + 2 * SSM_GROUPS * SSM_STATE
    c4, f4 = CC // N_CHIPS, D_FF // N_CHIPS
    o1 = SSM_CONV * c4
    o2 = o1 + c4
    o3 = o2 + D_INNER // N_CHIPS
    o4 = o3 + 2 * FFN_CONV * f4
    P = {k: d[k] for k in ["norm_mix", "norm_ffn", "q_norm_a", "k_norm_a", "q_norm_b", "k_norm_b", "sinks",
                           "ssm_dt_bias", "ssm_a_log", "ssm_d", "ffn_conv_b"]}
    P["relpos_table"] = d["relpos_table"][0]
    P["ssm_conv_w"] = sm[:, :o1].reshape(N_CHIPS, SSM_CONV, c4).transpose(1, 0, 2).reshape(SSM_CONV, CC)
    P["ssm_conv_b"] = sm[:, o1:o2].reshape(1, CC)
    P["ssm_norm"] = sm[:, o2:o3].reshape(1, D_INNER)
    P["ffn_conv_w"] = sm[:, o3:o4].reshape(N_CHIPS, 2, FFN_CONV, f4).transpose(1, 2, 0, 3).reshape(2, FFN_CONV, D_FF)

    lpart, dx, gW, gP = local_step(d["x"][0], d["loss_target"][0], W, P)
    loss = lax.psum(lpart[0, 0], ("x", "y", "c"))

    slabs = [_cols_to_slabs(gW["attn_in"]), _rows_to_slabs(gW["attn_out"]), _cols_to_slabs(gW["ssm_in"]),
             _rows_to_slabs(gW["ssm_out"]), _cols_to_slabs(gW["ffn_in"][0]), _cols_to_slabs(gW["ffn_in"][1]),
             _rows_to_slabs(gW["ffn_out"][0]), _rows_to_slabs(gW["ffn_out"][1])]
    tags = ["ai", "ao", "si", "so", "fi0", "fi1", "fo0", "fo1"]
    from_sib = pair_swap_halves(slabs, name="grad_pair_swap")
    pairs = [pair_add(g, r, csel, name="grad_pair_add_" + t) for g, r, t in zip(slabs, from_sib, tags)]
    from_chips = scatter_chips([p[1] for p in pairs], name="grad_scatter")
    mine = [chip_add(p[0], r, msel, name="grad_chip_add_" + t) for p, r, t in zip(pairs, from_chips, tags)]
    theirs = pair_share(mine, name="grad_pair_share")
    full = [jnp.where(ci == 0, jnp.stack([a, b]), jnp.stack([b, a])) for a, b in zip(mine, theirs)]
    two = lambda a: a.reshape((-1, a.shape[-1]))
    grads = {"attn_w_in": two(full[0])[None], "attn_w_out": two(full[1])[None],
             "ssm_w_in": two(full[2])[None], "ssm_w_out": two(full[3])[None],
             "ffn_w_in": jnp.stack([two(full[4]), two(full[5])]),
             "ffn_w_out": jnp.stack([two(full[6]), two(full[7])])}

    flat = jnp.concatenate([gP[k].reshape(-1) for k in SMALL_ORDER])
    flat = jnp.pad(flat, (0, SMALL_ROWS * LANES - flat.shape[0])).reshape(SMALL_ROWS, LANES)
    tot = sum_slots(gather_all(flat, name="small_gather"), name="small_sum").reshape(-1)
    off = 0
    for k in SMALL_ORDER:
        n = int(np.prod(gP[k].shape))
        g = tot[off:off + n].reshape(gP[k].shape)
        off += n
        if k == "ssm_conv_w":
            g = lax.dynamic_slice_in_dim(g, me * c4, c4, axis=1)[None]
        elif k == "ssm_conv_b":
            g = lax.dynamic_slice_in_dim(g, me * c4, c4, axis=1)
        elif k == "ssm_norm":
            g = lax.dynamic_slice_in_dim(g, me * (D_INNER // N_CHIPS), D_INNER // N_CHIPS, axis=1)
        elif k == "ffn_conv_w":
            g = lax.dynamic_slice_in_dim(g, me * f4, f4, axis=2)
        elif k == "relpos_table":
            g = g[None]
        grads[k] = g

    deltas, new_m, new_v = {}, {}, {}
    for k in WEIGHTS:
        deltas[k], new_m[k], new_v[k] = _adamw(d[k], grads[k], d["m_" + k], d["v_" + k], k)
    return (loss, dx[None], *[grads[k] for k in WEIGHTS], *[deltas[k] for k in WEIGHTS],
            *[new_m[k] for k in WEIGHTS], *[new_v[k] for k in WEIGHTS])
```

```python
import functools

import numpy as np
import jax
import jax.numpy as jnp
from jax import lax
from jax.experimental import pallas as pl
from jax.experimental.pallas import tpu as pltpu

F32 = jnp.float32
BF16 = jnp.bfloat16
HI = lax.Precision.HIGHEST

D_MODEL = 1024
CHUNK = 64
EPS = 1e-6
HEAD_DIM = 64
N_HEADS = 8
A_PREV = 8
B_PREV = 2
MAX_REL = 256
D_INNER = 2048
SSM_HEADS = 32
SSM_GROUPS = 4
SSM_STATE = 128
SSM_CONV = 4
D_FF = 2816
FFN_CONV = 3
LANES = 128
SUBLANES = 8
VMEM_LIMIT = 56 * 1024 * 1024
SSD_L = 128

ADAM_LR = 0.001
ADAM_B1 = 0.9
ADAM_B2 = 0.999
ADAM_EPS = 1e-08
ADAM_WD = 0.01
ADAM_STEP = 10

MESH = pl.DeviceIdType.MESH


def _params(*sem):
    return pltpu.CompilerParams(dimension_semantics=sem, vmem_limit_bytes=VMEM_LIMIT)


def _pick(n, want):
    if n <= want:
        return n
    t = (want // LANES) * LANES
    while t >= LANES:
        if n % t == 0:
            return t
        t -= LANES
    return n


MM_ROWS = 512
MM_COLS = 1536
MM_RED = 512


def matmul(a, b, *, mode, name, out_dtype=F32, residual=None):
    dims = {"nn": (((1,), (0,)), ((), ())), "nt": (((1,), (1,)), ((), ())), "tn": (((0,), (0,)), ((), ()))}[mode]
    if mode == "tn":
        assert residual is None and out_dtype == F32
        (K, M), (K2, N) = a.shape, b.shape
        assert K == K2, (a.shape, b.shape)
        tm, tn, tk = _pick(M, MM_COLS), _pick(N, MM_COLS), _pick(K, MM_RED)

        def body(a_ref, b_ref, o_ref):
            k = pl.program_id(2)
            p = lax.dot_general(a_ref[...].astype(BF16), b_ref[...].astype(BF16), dims, preferred_element_type=F32)

            @pl.when(k == 0)
            def _():
                o_ref[...] = p

            @pl.when(k != 0)
            def _():
                o_ref[...] += p

        return pl.pallas_call(
            body, name=name, grid=(M // tm, N // tn, K // tk),
            in_specs=[pl.BlockSpec((tk, tm), lambda i, j, k: (k, i)), pl.BlockSpec((tk, tn), lambda i, j, k: (k, j))],
            out_specs=pl.BlockSpec((tm, tn), lambda i, j, k: (i, j)),
            out_shape=jax.ShapeDtypeStruct((M, N), F32),
            compiler_params=_params("parallel", "parallel", "arbitrary"),
        )(a, b)

    if mode == "nn":
        (M, K), (K2, N) = a.shape, b.shape
    else:
        (M, K), (N, K2) = a.shape, b.shape
    assert K == K2, (a.shape, b.shape, mode)
    tm, tn = _pick(M, MM_ROWS), _pick(N, MM_COLS)

    def body(*refs):
        a_ref, b_ref = refs[:2]
        o_ref = refs[-1]
        r = lax.dot_general(a_ref[...].astype(BF16), b_ref[...].astype(BF16), dims, preferred_element_type=F32)
        if residual is not None:
            r = r + refs[2][...].astype(F32)
        o_ref[...] = r.astype(o_ref.dtype)

    a_spec = pl.BlockSpec((tm, K), lambda j, i: (i, 0))
    b_spec = pl.BlockSpec((K, tn), lambda j, i: (0, j)) if mode == "nn" else pl.BlockSpec((tn, K), lambda j, i: (j, 0))
    o_spec = pl.BlockSpec((tm, tn), lambda j, i: (i, j))
    in_specs = [a_spec, b_spec] + ([o_spec] if residual is not None else [])
    args = (a, b) + ((residual,) if residual is not None else ())
    return pl.pallas_call(
        body, name=name, grid=(N // tn, M // tm),
        in_specs=in_specs, out_specs=o_spec,
        out_shape=jax.ShapeDtypeStruct((M, N), out_dtype),
        compiler_params=_params("parallel", "parallel"),
    )(*args)


def rowwise(f, rows, params, outs, *, name, tm=256):
    S = rows[0].shape[0]
    tm = _row_tile(S, tm)
    nr, npar = len(rows), len(params)

    def body(*refs):
        vals = [r[...] for r in refs[:nr + npar]]
        res = f(*vals)
        for o_ref, r in zip(refs[nr + npar:], res):
            o_ref[...] = r.astype(o_ref.dtype)

    in_specs = [pl.BlockSpec((tm, r.shape[1]), lambda i: (i, 0)) for r in rows]
    in_specs += [pl.BlockSpec(p.shape, lambda i: (0, 0)) for p in params]
    out_specs = [pl.BlockSpec((tm, c), lambda i: (i, 0)) for c, _ in outs]
    out_shape = [jax.ShapeDtypeStruct((S, c), dt) for c, dt in outs]
    return pl.pallas_call(body, name=name, grid=(S // tm,), in_specs=in_specs, out_specs=out_specs,
                          out_shape=out_shape, compiler_params=_params("parallel"))(*rows, *params)


def rowwise_vjp(f, rows, params, cots, drow, dpar, *, name, tm=256):
    S = rows[0].shape[0]
    tm = _row_tile(S, tm)
    nr, npar, nc = len(rows), len(params), len(cots)

    def body(*refs):
        vals = [r[...] for r in refs[:nr + npar]]
        cvals = [r[...].astype(F32) for r in refs[nr + npar:nr + npar + nc]]
        o_refs = refs[nr + npar + nc:]
        want = [ri for ri, _ in drow] + [nr + pi for pi in dpar]

        def f_want(*d):
            full = list(vals)
            for k, v in zip(want, d):
                full[k] = v
            return f(*full)

        _, vjp = jax.vjp(f_want, *[vals[k] for k in want])
        grads = vjp(tuple(cvals))
        for o_ref, g in zip(o_refs[:len(drow)], grads):
            o_ref[...] = g.astype(o_ref.dtype)
        first = pl.program_id(0) == 0
        for o_ref, g in zip(o_refs[len(drow):], grads[len(drow):]):
            g = g.astype(F32)

            @pl.when(first)
            def _(o_ref=o_ref, g=g):
                o_ref[...] = g

            @pl.when(jnp.logical_not(first))
            def _(o_ref=o_ref, g=g):
                o_ref[...] += g

    in_specs = [pl.BlockSpec((tm, r.shape[1]), lambda i: (i, 0)) for r in rows]
    in_specs += [pl.BlockSpec(p.shape, lambda i: (0, 0)) for p in params]
    in_specs += [pl.BlockSpec((tm, c.shape[1]), lambda i: (i, 0)) for c in cots]
    out_specs = [pl.BlockSpec((tm, rows[ri].shape[1]), lambda i: (i, 0)) for ri, _ in drow]
    out_specs += [pl.BlockSpec(params[pi].shape, lambda i: (0, 0)) for pi in dpar]
    out_shape = [jax.ShapeDtypeStruct(rows[ri].shape, dt) for ri, dt in drow]
    out_shape += [jax.ShapeDtypeStruct(params[pi].shape, F32) for pi in dpar]
    return pl.pallas_call(body, name=name, grid=(S // tm,), in_specs=in_specs, out_specs=out_specs,
                          out_shape=out_shape, compiler_params=_params("arbitrary"))(*rows, *params, *cots)


HALO = SUBLANES


def dwconv_fwd(x, w, b, post, extra, outs, *, name, tm=256):
    S, C = x.shape
    K = w.shape[0]
    tm = min(tm, S)
    hb = tm // HALO
    ne = len(extra)

    def body(*refs):
        x_ref, halo_ref, w_ref, b_ref = refs[:4]
        e_refs = refs[4:4 + ne]
        o_refs = refs[4 + ne:4 + ne + len(outs)]
        buf = refs[-1]
        i = pl.program_id(0)
        buf[0:HALO, :] = jnp.where(i == 0, 0.0, halo_ref[...])
        buf[HALO:HALO + tm, :] = x_ref[...]
        for c0 in range(0, C, LANES):
            cs = slice(c0, c0 + LANES)
            acc = jnp.broadcast_to(b_ref[:, cs], (tm, LANES))
            for k in range(K):
                acc = acc + w_ref[k:k + 1, cs] * buf[pl.ds(HALO - (K - 1) + k, tm), cs]
            for o_ref, r in zip(o_refs, post(acc, *[e[:, cs] for e in e_refs])):
                o_ref[:, cs] = r.astype(o_ref.dtype)

    row = pl.BlockSpec((tm, C), lambda i: (i, 0))
    return pl.pallas_call(
        body, name=name, grid=(S // tm,),
        in_specs=[row,
                  pl.BlockSpec((HALO, C), lambda i: (jnp.maximum(i * hb - 1, 0), 0)),
                  pl.BlockSpec((K, C), lambda i: (0, 0)),
                  pl.BlockSpec((1, C), lambda i: (0, 0))] + [row] * ne,
        out_specs=[row] * len(outs),
        out_shape=[jax.ShapeDtypeStruct((S, C), dt) for dt in outs],
        scratch_shapes=[pltpu.VMEM((HALO + tm, C), F32)],
        compiler_params=_params("parallel"),
    )(x, x, w, b, *extra)


def dwconv_bwd(x, w, srcs, dy_fn, extra_outs, *, name, tm=256):
    S, C = x.shape
    K = w.shape[0]
    tm = min(tm, S)
    hb = tm // HALO
    n = S // tm
    groups = [s if isinstance(s, tuple) else (s,) for s in srcs]
    flat = [a for g in groups for a in g]
    nf = len(flat)

    def body(*refs):
        x_ref, xh_ref, w_ref = refs[:3]
        dx_ref, dw_ref, db_ref = refs[3 + 2 * nf:6 + 2 * nf]
        e_refs = refs[6 + 2 * nf:6 + 2 * nf + len(extra_outs)]
        bx, bd = refs[-2:]

        def strips(first, c0):
            out, at = [], first
            for g in groups:
                off = 0
                for a in g:
                    if off <= c0 < off + a.shape[1]:
                        out.append(refs[at][:, c0 - off:c0 - off + LANES].astype(F32))
                    off += a.shape[1]
                    at += 1
            return out

        i = pl.program_id(0)
        bx[0:HALO, :] = jnp.where(i == 0, 0.0, xh_ref[...])
        bx[HALO:HALO + tm, :] = x_ref[...]

        @pl.when(i == 0)
        def _():
            dw_ref[...] = jnp.zeros_like(dw_ref)
            db_ref[...] = jnp.zeros_like(db_ref)

        for c0 in range(0, C, LANES):
            cs = slice(c0, c0 + LANES)
            res = dy_fn(*strips(3, c0))
            dyv = res[0]
            for e_ref, r in zip(e_refs, res[1:]):
                e_ref[:, cs] = r.astype(e_ref.dtype)
            bd[0:tm, cs] = dyv
            bd[tm:tm + HALO, cs] = jnp.where(i == n - 1, 0.0, dy_fn(*strips(3 + nf, c0))[0])
            acc = jnp.zeros((tm, LANES), F32)
            for k in range(K):
                acc = acc + w_ref[k:k + 1, cs] * bd[pl.ds((K - 1) - k, tm), cs]
            dx_ref[:, cs] = acc
            for k in range(K):
                dw_ref[k:k + 1, cs] += jnp.sum(dyv * bx[pl.ds(HALO - (K - 1) + k, tm), cs], axis=0, keepdims=True)
            db_ref[:, cs] += jnp.sum(dyv, axis=0, keepdims=True)

    row = lambda c: pl.BlockSpec((tm, c), lambda i: (i, 0))
    nxt = lambda c: pl.BlockSpec((HALO, c), lambda i: (jnp.minimum((i + 1) * hb, S // HALO - 1), 0))
    return pl.pallas_call(
        body, name=name, grid=(n,),
        in_specs=[row(C), pl.BlockSpec((HALO, C), lambda i: (jnp.maximum(i * hb - 1, 0), 0)),
                  pl.BlockSpec((K, C), lambda i: (0, 0))]
                 + [row(a.shape[1]) for a in flat] + [nxt(a.shape[1]) for a in flat],
        out_specs=[row(C), pl.BlockSpec((K, C), lambda i: (0, 0)), pl.BlockSpec((1, C), lambda i: (0, 0))]
                  + [row(C)] * len(extra_outs),
        out_shape=[jax.ShapeDtypeStruct((S, C), F32), jax.ShapeDtypeStruct((K, C), F32),
                   jax.ShapeDtypeStruct((1, C), F32)] + [jax.ShapeDtypeStruct((S, C), dt) for dt in extra_outs],
        scratch_shapes=[pltpu.VMEM((HALO + tm, C), F32), pltpu.VMEM((tm + HALO, C), F32)],
        compiler_params=_params("arbitrary"),
    )(x, x, w, *flat, *flat)


def _sigmoid(x):
    return 0.5 * jnp.tanh(0.5 * x) + 0.5


def _silu(x):
    return x * _sigmoid(x)


def _dsilu(x):
    s = _sigmoid(x)
    return s * (1.0 + x * (1.0 - s))


def f_rmsnorm(x, g):
    return (x * lax.rsqrt(jnp.mean(x * x, axis=-1, keepdims=True) + EPS) * g,)


SEL = lax.Precision.HIGH


def _group_norm(x, bd, width):
    ms = jnp.dot(x * x, bd, precision=SEL, preferred_element_type=F32) * (1.0 / width)
    return x * lax.rsqrt(ms + EPS)


def f_qknorm(qkv, gqa, gka, gqb, gkb, bd512, bd128, fold, expand):
    dq = N_HEADS * HEAD_DIM
    qa, ka, va, qb = (qkv[:, i * dq:(i + 1) * dq] for i in range(4))
    kb = qkv[:, 4 * dq:4 * dq + LANES]
    vb = qkv[:, 4 * dq + LANES:4 * dq + 2 * LANES]
    tile8 = lambda g: jnp.dot(g, fold, precision=HI, preferred_element_type=F32)
    qa = _group_norm(qa, bd512, HEAD_DIM) * tile8(gqa)
    ka = _group_norm(ka, bd512, HEAD_DIM) * tile8(gka)
    qb = _group_norm(qb, bd512, HEAD_DIM) * tile8(gqb)
    kb = _group_norm(kb, bd128, HEAD_DIM) * tile8(gkb)[:, :LANES]
    kb = jnp.dot(kb, expand, precision=SEL, preferred_element_type=F32)
    vb = jnp.dot(vb, expand, precision=SEL, preferred_element_type=F32)
    return qa, ka, va, qb, kb, vb


def f_gate_norm(y, z, nw):
    v = y * _silu(z)
    gw = D_INNER // SSM_GROUPS
    parts = []
    for g in range(SSM_GROUPS):
        vg = v[:, g * gw:(g + 1) * gw]
        parts.append(vg * lax.rsqrt(jnp.mean(vg * vg, axis=-1, keepdims=True) + EPS))
    return (jnp.concatenate(parts, axis=-1) * nw,)


ATT_TQ = 256
_NT = (((1,), (1,)), ((), ()))
_TN = (((0,), (0,)), ((), ()))


def _attn_probs(qh, kb, bias, valid, snk):
    s = lax.dot_general(qh, kb, _NT, preferred_element_type=F32) * (HEAD_DIM ** -0.5) + bias
    s = jnp.where(valid, s, -jnp.inf)
    m = jnp.max(s, axis=1, keepdims=True)
    if snk is not None:
        m = jnp.maximum(m, snk)
    e = jnp.exp(s - m)
    den = jnp.sum(e, axis=1, keepdims=True)
    if snk is None:
        return e / den, None
    es = jnp.exp(snk - m)
    den = den + es
    return e / den, es / den


def widen_bias(bias, n_prev, nj):
    band = (n_prev + 1) * CHUNK
    wk = (nj + n_prev) * CHUNK
    rows = [jnp.pad(bias, ((0, 0), (0, 0), (j * CHUNK, wk - band - j * CHUNK)), constant_values=-jnp.inf)
            for j in range(nj)]
    return jnp.concatenate(rows, axis=1)


def fold_bias(dbw, n_prev, nj):
    band = (n_prev + 1) * CHUNK
    acc = dbw[:, :CHUNK, :band]
    for j in range(1, nj):
        acc = acc + dbw[:, j * CHUNK:(j + 1) * CHUNK, j * CHUNK:j * CHUNK + band]
    return acc


def attn_fwd(q, k, v, bias_w, sinks, *, n_prev, name, rider=None):
    S = q.shape[0]
    pad = n_prev * CHUNK
    tq = min(ATT_TQ, S)
    wk = tq + pad
    assert bias_w.shape == (N_HEADS, tq, wk), bias_w.shape
    has_sink = sinks is not None

    r_in, r_out, r_shapes, r_sems, r_args = _rider_parts(rider)
    n_own = 5 if has_sink else 4
    n_p, n_i = N_HEADS // 2, S // tq

    def body(*refs):
        q_ref, k_ref, v_ref, bias_ref = refs[:4]
        sink_ref = refs[4] if has_sink else None
        o_ref = refs[n_own + len(r_in)]
        if rider is not None:
            p_id, i_id = pl.program_id(0), pl.program_id(1)
            _ride(rider, refs[n_own:n_own + len(r_in)], refs[n_own + len(r_in) + 1:n_own + len(r_in) + 1 + len(r_out)],
                  refs[n_own + len(r_in) + 1 + len(r_out):],
                  jnp.logical_and(p_id == 0, i_id == 0), jnp.logical_and(p_id == n_p // 2, i_id == 0),
                  jnp.logical_and(p_id == n_p - 1, i_id == n_i - 1))
        start = pl.multiple_of(pl.program_id(1) * tq, tq)
        head0 = lax.broadcasted_iota(jnp.int32, (1, LANES), 1) < HEAD_DIM
        valid = lax.broadcasted_iota(jnp.int32, (1, wk), 1) + start >= pad
        qp = q_ref[...].astype(F32)
        kb = k_ref[pl.ds(start, wk), :]
        vb = v_ref[pl.ds(start, wk), :]
        outs = []
        for r in range(2):
            mh = head0 if r == 0 else jnp.logical_not(head0)
            qh = jnp.where(mh, qp, 0.0).astype(BF16)
            snk = sink_ref[0, r:r + 1, 0:1] if has_sink else None
            p, _ = _attn_probs(qh, kb, bias_ref[r], valid, snk)
            outs.append(jnp.dot(p.astype(BF16), vb, preferred_element_type=F32))
        o_ref[...] = jnp.where(head0, outs[0], outs[1]).astype(o_ref.dtype)

    in_specs = [pl.BlockSpec((tq, LANES), lambda p, i: (i, p)),
                pl.BlockSpec((pad + S, LANES), lambda p, i: (0, p)),
                pl.BlockSpec((pad + S, LANES), lambda p, i: (0, p)),
                pl.BlockSpec((2, tq, wk), lambda p, i: (p, 0, 0))]
    args = [q, k, v, bias_w]
    if has_sink:
        in_specs.append(pl.BlockSpec((1, 2, LANES), lambda p, i: (p, 0, 0)))
        args.append(sinks)
    res = pl.pallas_call(
        body, name=name, grid=(n_p, n_i), in_specs=in_specs + r_in,
        out_specs=[pl.BlockSpec((tq, LANES), lambda p, i: (i, p))] + r_out,
        out_shape=[jax.ShapeDtypeStruct((S, N_HEADS * HEAD_DIM), BF16)] + r_shapes,
        scratch_shapes=r_sems,
        compiler_params=pltpu.CompilerParams(dimension_semantics=("arbitrary", "arbitrary"), vmem_limit_bytes=VMEM_LIMIT,
                                             has_side_effects=rider is not None),
    )(*args, *r_args)
    return res[0], res[1:]


def attn_bwd(q, k, v, do, bias_w, sinks, *, n_prev, name, rider=None):
    S = q.shape[0]
    pad = n_prev * CHUNK
    tq = min(ATT_TQ, S)
    wk = tq + pad
    assert bias_w.shape == (N_HEADS, tq, wk), bias_w.shape
    has_sink = sinks is not None
    scale = HEAD_DIM ** -0.5

    r_in, r_out, r_shapes, r_sems, r_args = _rider_parts(rider)
    n_own_in = 6 if has_sink else 5
    n_own_out = 5 if has_sink else 4
    n_p, n_i = N_HEADS // 2, S // tq

    def body(*refs):
        q_ref, k_ref, v_ref, do_ref, bias_ref = refs[:5]
        sink_ref = refs[5] if has_sink else None
        o0 = n_own_in + len(r_in)
        dq_ref, dk_ref, dv_ref, db_ref = refs[o0:o0 + 4]
        dsk_ref = refs[o0 + 4] if has_sink else None
        i = pl.program_id(1)
        if rider is not None:
            p_id = pl.program_id(0)
            _ride(rider, refs[n_own_in:o0], refs[o0 + n_own_out:o0 + n_own_out + len(r_out)],
                  refs[o0 + n_own_out + len(r_out):],
                  jnp.logical_and(p_id == 0, i == 0), jnp.logical_and(p_id == n_p // 2, i == 0),
                  jnp.logical_and(p_id == n_p - 1, i == n_i - 1))

        @pl.when(i == 0)
        def _():
            dk_ref[...] = jnp.zeros_like(dk_ref)
            dv_ref[...] = jnp.zeros_like(dv_ref)
            db_ref[...] = jnp.zeros_like(db_ref)
            if has_sink:
                dsk_ref[...] = jnp.zeros_like(dsk_ref)

        start = pl.multiple_of(i * tq, tq)
        head0 = lax.broadcasted_iota(jnp.int32, (1, LANES), 1) < HEAD_DIM
        valid = lax.broadcasted_iota(jnp.int32, (1, wk), 1) + start >= pad
        qp = q_ref[...].astype(F32)
        dop = do_ref[...].astype(F32)
        kb = k_ref[pl.ds(start, wk), :]
        vb = v_ref[pl.ds(start, wk), :]
        dqs = []
        for r in range(2):
            mh = head0 if r == 0 else jnp.logical_not(head0)
            qh = jnp.where(mh, qp, 0.0).astype(BF16)
            doh = jnp.where(mh, dop, 0.0).astype(BF16)
            snk = sink_ref[0, r:r + 1, 0:1] if has_sink else None
            p, ps = _attn_probs(qh, kb, bias_ref[r], valid, snk)
            dp = lax.dot_general(doh, vb, _NT, preferred_element_type=F32)
            delta = jnp.sum(p * dp, axis=1, keepdims=True)
            ds = p * (dp - delta)
            db_ref[r] += ds
            if has_sink:
                dsk = -jnp.sum(ps * delta, axis=0, keepdims=True)
                dsk_ref[0, r:r + 1, :] += jnp.broadcast_to(dsk, (1, LANES))
            dsb = ds.astype(BF16)
            dqs.append(jnp.dot(dsb, kb, preferred_element_type=F32) * scale)
            dk_ref[pl.ds(start, wk), :] += lax.dot_general(dsb, qh, _TN, preferred_element_type=F32) * scale
            dv_ref[pl.ds(start, wk), :] += lax.dot_general(p.astype(BF16), doh, _TN, preferred_element_type=F32)
        dq_ref[...] = jnp.where(head0, dqs[0], dqs[1])

    row_spec = pl.BlockSpec((tq, LANES), lambda p, i: (i, p))
    kv_spec = pl.BlockSpec((pad + S, LANES), lambda p, i: (0, p))
    bias_spec = pl.BlockSpec((2, tq, wk), lambda p, i: (p, 0, 0))
    sink_spec = pl.BlockSpec((1, 2, LANES), lambda p, i: (p, 0, 0))
    in_specs = [row_spec, kv_spec, kv_spec, row_spec, bias_spec]
    args = [q, k, v, do, bias_w]
    out_specs = [row_spec, kv_spec, kv_spec, bias_spec]
    W = N_HEADS * HEAD_DIM
    out_shape = [jax.ShapeDtypeStruct((S, W), F32), jax.ShapeDtypeStruct((pad + S, W), F32),
                 jax.ShapeDtypeStruct((pad + S, W), F32), jax.ShapeDtypeStruct((N_HEADS, tq, wk), F32)]
    if has_sink:
        in_specs.append(sink_spec)
        args.append(sinks)
        out_specs.append(sink_spec)
        out_shape.append(jax.ShapeDtypeStruct((N_HEADS // 2, 2, LANES), F32))
    res = pl.pallas_call(
        body, name=name, grid=(n_p, n_i), in_specs=in_specs + r_in, out_specs=out_specs + r_out,
        out_shape=out_shape + r_shapes, scratch_shapes=r_sems,
        compiler_params=pltpu.CompilerParams(dimension_semantics=("arbitrary", "arbitrary"), vmem_limit_bytes=VMEM_LIMIT,
                                             has_side_effects=rider is not None),
    )(*args, *r_args)
    return res[:n_own_out], res[n_own_out:]


HP = SSM_HEADS // 2
PAIRS_PER_GROUP = HP // SSM_GROUPS
HEADS_PER_GROUP = SSM_HEADS // SSM_GROUPS
GW = HEADS_PER_GROUP * 64


def _ssd_dt(dtraw, dtb, A, tril):
    lane = lax.broadcasted_iota(jnp.int32, (1, LANES), 1)
    u = dtraw + dtb
    eu = jnp.exp(-jnp.abs(u))
    w1 = 1.0 + eu
    l1p = jnp.where(w1 == 1.0, eu, jnp.log(w1) * eu / jnp.where(w1 == 1.0, 1.0, w1 - 1.0))
    dt = jnp.where(lane < SSM_HEADS, jnp.maximum(u, 0.0) + l1p, 0.0)
    acs = jnp.dot(tril, dt * A, precision=HI, preferred_element_type=F32)
    return u, dt, acs


def _head_expander():
    hw = D_INNER // SSM_HEADS
    return (np.arange(LANES)[:, None] == np.arange(D_INNER)[None, :] // hw).astype(np.float32)


def ssd_fwd(xbc, dtraw, dtb, A, dexp, *, name):
    S = xbc.shape[0]
    L = min(SSD_L, S)
    nc = S // L
    N = SSM_STATE
    e_mat = jnp.asarray(_head_expander())

    def body(xs_ref, b_ref, c_ref, dtr_ref, dtb_ref, a_ref, d_ref, e_ref, y_ref, st_out_ref, st_ref, xw_ref):
        c = pl.program_id(0)

        @pl.when(c == 0)
        def _():
            st_ref[...] = jnp.zeros_like(st_ref)

        st_out_ref[0] = st_ref[...]
        ri = lax.broadcasted_iota(jnp.int32, (L, L), 0)
        ci = lax.broadcasted_iota(jnp.int32, (L, L), 1)
        trilb = ri >= ci
        head0 = lax.broadcasted_iota(jnp.int32, (1, LANES), 1) < 64
        _, dt, acs = _ssd_dt(dtr_ref[...], dtb_ref[...], a_ref[...], trilb.astype(F32))
        acsT = acs.T
        last = acs[L - 1:L, :]
        expand = lambda t: jnp.dot(t, e_ref[...], precision=SEL, preferred_element_type=F32)
        dte, eae, wte = expand(dt), expand(jnp.exp(acs)), expand(jnp.exp(last - acs) * dt)
        lasts = [last[:, h:h + 1] for h in range(SSM_HEADS)]
        for g in range(SSM_GROUPS):
            Bg = b_ref[:, g * N:(g + 1) * N].astype(BF16)
            Cg = c_ref[:, g * N:(g + 1) * N].astype(BF16)
            CB = lax.dot_general(Cg, Bg, _NT, preferred_element_type=F32)
            Z = lax.dot_general(Cg, st_ref[g * GW:(g + 1) * GW, :].astype(BF16), _NT, preferred_element_type=F32)
            for q in range(PAIRS_PER_GROUP):
                hp = g * PAIRS_PER_GROUP + q
                sl = slice(hp * LANES, (hp + 1) * LANES)
                xs = xs_ref[:, sl]
                xd = xs * dte[:, sl]
                yi = jnp.zeros((L, LANES), F32)
                for r in range(2):
                    h = 2 * hp + r
                    dec = jnp.exp(jnp.where(trilb, acs[:, h:h + 1] - acsT[h:h + 1, :], -jnp.inf))
                    xdh = jnp.where(head0 if r == 0 else jnp.logical_not(head0), xd, 0.0).astype(BF16)
                    yi = yi + jnp.dot((CB * dec).astype(BF16), xdh, preferred_element_type=F32)
                y_ref[:, sl] = yi + Z[:, q * LANES:(q + 1) * LANES] * eae[:, sl] + d_ref[:, sl] * xs
                xw_ref[:, sl] = (xs * wte[:, sl]).astype(BF16)
        for g in range(SSM_GROUPS):
            Bg = b_ref[:, g * N:(g + 1) * N].astype(BF16)
            sn = lax.dot_general(xw_ref[:, g * GW:(g + 1) * GW], Bg, _TN, preferred_element_type=F32)
            for k in range(HEADS_PER_GROUP):
                h = g * HEADS_PER_GROUP + k
                rows = slice(h * 64, (h + 1) * 64)
                st_ref[rows, :] = st_ref[rows, :] * jnp.exp(lasts[h]) + sn[k * 64:(k + 1) * 64, :]

    return pl.pallas_call(
        body, name=name, grid=(nc,),
        in_specs=[pl.BlockSpec((L, D_INNER), lambda c: (c, 0)),
                  pl.BlockSpec((L, SSM_GROUPS * N), lambda c: (c, D_INNER // (SSM_GROUPS * N))),
                  pl.BlockSpec((L, SSM_GROUPS * N), lambda c: (c, D_INNER // (SSM_GROUPS * N) + 1)),
                  pl.BlockSpec((L, LANES), lambda c: (c, 0)),
                  pl.BlockSpec((1, LANES), lambda c: (0, 0)),
                  pl.BlockSpec((1, LANES), lambda c: (0, 0)),
                  pl.BlockSpec((1, D_INNER), lambda c: (0, 0)),
                  pl.BlockSpec((LANES, D_INNER), lambda c: (0, 0))],
        out_specs=[pl.BlockSpec((L, D_INNER), lambda c: (c, 0)),
                   pl.BlockSpec((1, D_INNER, N), lambda c: (c, 0, 0))],
        out_shape=[jax.ShapeDtypeStruct((S, D_INNER), F32), jax.ShapeDtypeStruct((nc, D_INNER, N), F32)],
        scratch_shapes=[pltpu.VMEM((D_INNER, N), F32), pltpu.VMEM((L, D_INNER), BF16)],
        compiler_params=_params("arbitrary"),
    )(xbc, xbc, xbc, dtraw, dtb, A, dexp, e_mat)


def ssd_bwd(xbc, dtraw, dtb, A, dexp, states, dy, *, name, rider=None):
    S = xbc.shape[0]
    L = min(SSD_L, S)
    nc = S // L
    N = SSM_STATE
    e_np = _head_expander()
    e_mat, et_mat = jnp.asarray(e_np), jnp.asarray(e_np.T)

    r_in, r_out, r_shapes, r_sems, r_args = _rider_parts(rider)

    def body(*refs):
        xs_ref, b_ref, c_ref, dtr_ref, dtb_ref, a_ref, d_ref, e_ref, et_ref, st_in_ref, dy_ref = refs[:11]
        o0 = 11 + len(r_in)
        dxs_ref, db_ref, dc_ref, ddtr_ref, da_ref, ddtb_ref, dd_ref = refs[o0:o0 + 7]
        s0 = o0 + 7 + len(r_out)
        dst_ref, xw_ref, dz_ref, r_ref, dsr_ref, dsc_ref = refs[s0:s0 + 6]
        step = pl.program_id(0)
        if rider is not None:
            _ride(rider, refs[11:o0], refs[o0 + 7:s0], refs[s0 + 6:], step == 0, step == nc // 2, step == nc - 1)

        @pl.when(step == 0)
        def _():
            dst_ref[...] = jnp.zeros_like(dst_ref)
            dsr_ref[...] = jnp.zeros_like(dsr_ref)
            dsc_ref[...] = jnp.zeros_like(dsc_ref)
            da_ref[...] = jnp.zeros_like(da_ref)
            ddtb_ref[...] = jnp.zeros_like(ddtb_ref)
            dd_ref[...] = jnp.zeros_like(dd_ref)

        ri = lax.broadcasted_iota(jnp.int32, (L, L), 0)
        ci = lax.broadcasted_iota(jnp.int32, (L, L), 1)
        trilb = ri >= ci
        lane = lax.broadcasted_iota(jnp.int32, (1, LANES), 1)
        sub = lax.broadcasted_iota(jnp.int32, (LANES, 1), 0)
        head0 = lane < 64
        A = a_ref[...]
        u, dt, acs = _ssd_dt(dtr_ref[...], dtb_ref[...], A, trilb.astype(F32))
        acsT = acs.T
        last = acs[L - 1:L, :]
        elast = jnp.exp(last)
        er = jnp.exp(last - acs)
        wt = er * dt
        expand = lambda t: jnp.dot(t, e_ref[...], precision=SEL, preferred_element_type=F32)
        dte, eae, wte = expand(dt), expand(jnp.exp(acs)), expand(wt)
        dlast = jnp.zeros((1, LANES), F32)
        dcbs = []
        for g in range(SSM_GROUPS):
            Bg = b_ref[:, g * N:(g + 1) * N].astype(BF16)
            Cg = c_ref[:, g * N:(g + 1) * N].astype(BF16)
            stg = st_in_ref[0, g * GW:(g + 1) * GW, :]
            dstg = dst_ref[g * GW:(g + 1) * GW, :]
            CB = lax.dot_general(Cg, Bg, _NT, preferred_element_type=F32)
            CBT = lax.dot_general(Bg, Cg, _NT, preferred_element_type=F32)
            Z = lax.dot_general(Cg, stg.astype(BF16), _NT, preferred_element_type=F32)
            U = lax.dot_general(Bg, dstg.astype(BF16), _NT, preferred_element_type=F32)
            dcb = jnp.zeros((L, L), F32)
            for q in range(PAIRS_PER_GROUP):
                hp = g * PAIRS_PER_GROUP + q
                sl = slice(hp * LANES, (hp + 1) * LANES)
                qs = slice(q * LANES, (q + 1) * LANES)
                xs = xs_ref[:, sl]
                dyp = dy_ref[:, sl]
                dtp, eap, wp, Dp = dte[:, sl], eae[:, sl], wte[:, sl], d_ref[:, sl]
                xd = xs * dtp
                dxd = jnp.zeros((L, LANES), F32)
                for r in range(2):
                    h = 2 * hp + r
                    mh = head0 if r == 0 else jnp.logical_not(head0)
                    dyh = jnp.where(mh, dyp, 0.0).astype(BF16)
                    xdh = jnp.where(mh, xd, 0.0).astype(BF16)
                    seg = acs[:, h:h + 1] - acsT[h:h + 1, :]
                    dec = jnp.exp(jnp.where(trilb, seg, -jnp.inf))
                    decT = jnp.exp(jnp.where(ri <= ci, -seg, -jnp.inf))
                    G = lax.dot_general(dyh, xdh, _NT, preferred_element_type=F32)
                    gd = G * dec
                    dcb = dcb + gd
                    dseg = gd * CB
                    dsr_ref[:, h:h + 1] = jnp.sum(dseg, axis=1, keepdims=True)
                    dsc_ref[h:h + 1, :] = jnp.sum(dseg, axis=0, keepdims=True)
                    dxd = dxd + jnp.dot((CBT * decT).astype(BF16), dyh, preferred_element_type=F32)
                Up = U[:, qs]
                r_ref[0, :, sl] = dyp * Z[:, qs] * eap
                r_ref[1, :, sl] = dxd * xs
                r_ref[2, :, sl] = Up * xs
                dz_ref[:, sl] = (dyp * eap).astype(BF16)
                xw_ref[:, sl] = (xs * wp).astype(BF16)
                dxs_ref[:, sl] = dxd * dtp + Dp * dyp + Up * wp
                dd_ref[:, sl] += jnp.sum(dyp * xs, axis=0, keepdims=True)
            dcbs.append(dcb)
            t = dstg * stg
            for k in range(HEADS_PER_GROUP):
                dlast = dlast + jnp.where(lane == g * HEADS_PER_GROUP + k,
                                          jnp.sum(t[k * 64:(k + 1) * 64, :], keepdims=True), 0.0)
        fold = lambda k: jnp.dot(r_ref[k], et_ref[...], precision=SEL, preferred_element_type=F32)
        r1, r2, dws = fold(0), fold(1), fold(2)
        dww = dws * wt
        ddt = r2 + dws * er
        dacs = r1 - dww + dsr_ref[...] - dsc_ref[...].T
        dlast = dlast * elast + jnp.sum(dww, axis=0, keepdims=True)
        lasts = [last[:, h:h + 1] for h in range(SSM_HEADS)]
        for g in range(SSM_GROUPS):
            Bg = b_ref[:, g * N:(g + 1) * N].astype(BF16)
            Cg = c_ref[:, g * N:(g + 1) * N].astype(BF16)
            gs = slice(g * GW, (g + 1) * GW)
            stb = st_in_ref[0, gs, :].astype(BF16)
            dstb = dst_ref[gs, :].astype(BF16)
            dcbb = dcbs[g].astype(BF16)
            dzg = dz_ref[:, gs]
            dc_ref[:, g * N:(g + 1) * N] = (jnp.dot(dzg, stb, preferred_element_type=F32)
                                            + jnp.dot(dcbb, Bg, preferred_element_type=F32))
            db_ref[:, g * N:(g + 1) * N] = (jnp.dot(xw_ref[:, gs], dstb, preferred_element_type=F32)
                                            + lax.dot_general(dcbb, Cg, _TN, preferred_element_type=F32))
            dsn = lax.dot_general(dzg, Cg, _TN, preferred_element_type=F32)
            for k in range(HEADS_PER_GROUP):
                h = g * HEADS_PER_GROUP + k
                rows = slice(h * 64, (h + 1) * 64)
                dst_ref[rows, :] = dst_ref[rows, :] * jnp.exp(lasts[h]) + dsn[k * 64:(k + 1) * 64, :]
        rowi = lax.broadcasted_iota(jnp.int32, (L, 1), 0)
        dacs = dacs + jnp.where(rowi == L - 1, dlast, 0.0)
        da = jnp.dot((ci >= ri).astype(F32), dacs, precision=HI, preferred_element_type=F32)
        ddt = ddt + da * A
        da_ref[...] += jnp.sum(da * dt, axis=0, keepdims=True)
        ddtr = jnp.where(lane < SSM_HEADS, ddt * _sigmoid(u), 0.0)
        ddtr_ref[...] = ddtr
        ddtb_ref[...] += jnp.sum(ddtr, axis=0, keepdims=True)

    rev = lambda c: nc - 1 - c
    gn = SSM_GROUPS * N
    res = pl.pallas_call(
        body, name=name, grid=(nc,),
        in_specs=[pl.BlockSpec((L, D_INNER), lambda c: (rev(c), 0)),
                  pl.BlockSpec((L, gn), lambda c: (rev(c), D_INNER // gn)),
                  pl.BlockSpec((L, gn), lambda c: (rev(c), D_INNER // gn + 1)),
                  pl.BlockSpec((L, LANES), lambda c: (rev(c), 0)),
                  pl.BlockSpec((1, LANES), lambda c: (0, 0)),
                  pl.BlockSpec((1, LANES), lambda c: (0, 0)),
                  pl.BlockSpec((1, D_INNER), lambda c: (0, 0)),
                  pl.BlockSpec((LANES, D_INNER), lambda c: (0, 0)),
                  pl.BlockSpec((D_INNER, LANES), lambda c: (0, 0)),
                  pl.BlockSpec((1, D_INNER, N), lambda c: (rev(c), 0, 0)),
                  pl.BlockSpec((L, D_INNER), lambda c: (rev(c), 0))] + r_in,
        out_specs=[pl.BlockSpec((L, D_INNER), lambda c: (rev(c), 0)),
                   pl.BlockSpec((L, gn), lambda c: (rev(c), 0)),
                   pl.BlockSpec((L, gn), lambda c: (rev(c), 0)),
                   pl.BlockSpec((L, LANES), lambda c: (rev(c), 0)),
                   pl.BlockSpec((1, LANES), lambda c: (0, 0)),
                   pl.BlockSpec((1, LANES), lambda c: (0, 0)),
                   pl.BlockSpec((1, D_INNER), lambda c: (0, 0))] + r_out,
        out_shape=[jax.ShapeDtypeStruct((S, D_INNER), F32), jax.ShapeDtypeStruct((S, gn), F32),
                   jax.ShapeDtypeStruct((S, gn), F32), jax.ShapeDtypeStruct((S, LANES), F32),
                   jax.ShapeDtypeStruct((1, LANES), F32), jax.ShapeDtypeStruct((1, LANES), F32),
                   jax.ShapeDtypeStruct((1, D_INNER), F32)] + r_shapes,
        scratch_shapes=[pltpu.VMEM((D_INNER, N), F32), pltpu.VMEM((L, D_INNER), BF16), pltpu.VMEM((L, D_INNER), BF16),
                        pltpu.VMEM((3, L, D_INNER), F32), pltpu.VMEM((L, LANES), F32), pltpu.VMEM((LANES, L), F32)]
                       + r_sems,
        compiler_params=pltpu.CompilerParams(dimension_semantics=("arbitrary",), vmem_limit_bytes=VMEM_LIMIT,
                                             has_side_effects=rider is not None),
    )(xbc, xbc, xbc, dtraw, dtb, A, dexp, e_mat, et_mat, states, dy, *r_args)
    return res[:7], res[7:]


BAND_A = (A_PREV + 1) * CHUNK
REL_W = 640


def _relpos_onehot(q):
    u = lax.broadcasted_iota(jnp.int32, (BAND_A, 1), 0)
    idx = jnp.clip(q - u + A_PREV * CHUNK, -MAX_REL, MAX_REL) + MAX_REL
    r = lax.broadcasted_iota(jnp.int32, (1, REL_W), 1)
    return (r == idx).astype(F32)


def relpos_bias(table_pad, *, name):
    def body(t_ref, o_ref):
        oh = _relpos_onehot(pl.program_id(0))
        o_ref[0] = lax.dot_general(t_ref[...], oh, _NT, precision=HI, preferred_element_type=F32)

    return pl.pallas_call(
        body, name=name, grid=(CHUNK,),
        in_specs=[pl.BlockSpec((N_HEADS, REL_W), lambda q: (0, 0))],
        out_specs=pl.BlockSpec((1, N_HEADS, BAND_A), lambda q: (q, 0, 0)),
        out_shape=jax.ShapeDtypeStruct((CHUNK, N_HEADS, BAND_A), F32),
        compiler_params=_params("parallel"),
    )(table_pad)


def relpos_grad(dbias_t, *, name):
    def body(d_ref, o_ref):
        q = pl.program_id(0)

        @pl.when(q == 0)
        def _():
            o_ref[...] = jnp.zeros_like(o_ref)

        o_ref[...] += jnp.dot(d_ref[0], _relpos_onehot(q), precision=HI, preferred_element_type=F32)

    return pl.pallas_call(
        body, name=name, grid=(CHUNK,),
        in_specs=[pl.BlockSpec((1, N_HEADS, BAND_A), lambda q: (q, 0, 0))],
        out_specs=pl.BlockSpec((N_HEADS, REL_W), lambda q: (0, 0)),
        out_shape=jax.ShapeDtypeStruct((N_HEADS, REL_W), F32),
        compiler_params=_params("arbitrary"),
    )(dbias_t)


def loss_head(y, t, *, name, tm=256):
    S, D = y.shape
    tm = min(tm, S)

    def body(y_ref, t_ref, dy_ref, l_ref):
        e = y_ref[...] - t_ref[...]
        dy_ref[...] = e * (1.0 / D)

        @pl.when(pl.program_id(0) == 0)
        def _():
            l_ref[...] = jnp.zeros_like(l_ref)

        part = jnp.sum(jnp.sum(e * e, axis=1, keepdims=True), axis=0, keepdims=True) * (0.5 / D)
        l_ref[...] += jnp.broadcast_to(part, l_ref.shape)

    return pl.pallas_call(
        body, name=name, grid=(S // tm,),
        in_specs=[pl.BlockSpec((tm, D), lambda i: (i, 0))] * 2,
        out_specs=[pl.BlockSpec((tm, D), lambda i: (i, 0)), pl.BlockSpec((1, LANES), lambda i: (0, 0))],
        out_shape=[jax.ShapeDtypeStruct((S, D), F32), jax.ShapeDtypeStruct((1, LANES), F32)],
        compiler_params=_params("arbitrary"),
    )(y, t)


def f_adamw(w, g, m, v):
    m = ADAM_B1 * m + (1.0 - ADAM_B1) * g
    v = ADAM_B2 * v + (1.0 - ADAM_B2) * (g * g)
    m_hat = m / (1.0 - ADAM_B1 ** ADAM_STEP)
    v_hat = v / (1.0 - ADAM_B2 ** ADAM_STEP)
    delta = -ADAM_LR * (m_hat / (jnp.sqrt(v_hat) + ADAM_EPS) + ADAM_WD * w)
    return delta, m, v


def f_norm_id(x, g):
    return f_rmsnorm(x, g)[0], x


ANY = pl.BlockSpec(memory_space=pl.ANY)


def _pos():
    return lax.axis_index("x"), lax.axis_index("y"), lax.axis_index("c")


def _other_chips(x, y):
    return [(1 - x, y), (x, 1 - y), (1 - x, 1 - y)]


class Rider:
    def __init__(self, ins, outs, sems, start, mid, finish):
        self.ins, self.outs, self.sems = list(ins), list(outs), list(sems)
        self.start, self.mid, self.finish = start, mid, finish


def _rider_parts(rider):
    if rider is None:
        return [], [], [], [], []
    return [ANY] * len(rider.ins), [ANY] * len(rider.outs), rider.outs, rider.sems, rider.ins


def _ride(rider, ins, outs, sems, first, mid, last):
    pos = _pos()

    @pl.when(first)
    def _():
        rider.start(pos, ins, outs, sems)

    if rider.mid is not None:
        @pl.when(mid)
        def _():
            rider.mid(pos, ins, outs, sems)

    @pl.when(last)
    def _():
        rider.finish(pos, ins, outs, sems)


def run_rider(rider, *, name):
    n_in, n_out = len(rider.ins), len(rider.outs)

    def body(*refs):
        ins, outs, sems = refs[:n_in], refs[n_in:n_in + n_out], refs[n_in + n_out:]
        pos = _pos()
        rider.start(pos, ins, outs, sems)
        if rider.mid is not None:
            rider.mid(pos, ins, outs, sems)
        rider.finish(pos, ins, outs, sems)

    return pl.pallas_call(
        body, name=name, in_specs=[ANY] * n_in, out_specs=[ANY] * n_out, out_shape=rider.outs,
        scratch_shapes=rider.sems, compiler_params=pltpu.CompilerParams(has_side_effects=True),
    )(*rider.ins)


def gather_rider(shards):
    n = len(shards)

    def copies(pos, ins, outs, sems):
        x, y, c = pos
        send, recv, fsend, frecv = sems
        me = 2 * x + y
        sib = (x, y, 1 - c)
        first, arrive, passed, theirs = [], [], [], []
        for i in range(n):
            for j, (px, py) in enumerate(_other_chips(x, y)):
                k = 3 * i + j
                far = dict(device_id=(px, py, c), device_id_type=MESH)
                near = dict(device_id=sib, device_id_type=MESH)
                got = outs[i].at[2 * px + py, c]
                his = outs[i].at[2 * px + py, 1 - c]
                first.append(pltpu.make_async_remote_copy(ins[i].at[c], outs[i].at[me, c], send.at[k], recv.at[k], **far))
                arrive.append(pltpu.make_async_remote_copy(ins[i].at[c], got, send.at[k], recv.at[k], **far))
                passed.append(pltpu.make_async_remote_copy(got, got, fsend.at[k], frecv.at[k], **near))
                theirs.append(pltpu.make_async_remote_copy(his, his, fsend.at[k], frecv.at[k], **near))
        return first, arrive, passed, theirs

    def start(*a):
        for cp in copies(*a)[0]:
            cp.start()

    def mid(*a):
        _, arrive, passed, _ = copies(*a)
        for got, cp in zip(arrive, passed):
            got.wait_recv()
            cp.start()

    def finish(*a):
        first, _, passed, theirs = copies(*a)
        for cp in theirs:
            cp.wait_recv()
        for cp in first + passed:
            cp.wait_send()

    return Rider(shards, [jax.ShapeDtypeStruct((4,) + s.shape, s.dtype) for s in shards],
                 [pltpu.SemaphoreType.DMA((3 * n,))] * 4, start, mid, finish)


def scatter_rider(ps):
    n = len(ps)

    def copies(pos, ins, outs, sems):
        x, y, c = pos
        send, recv = sems
        return [pltpu.make_async_remote_copy(ins[i].at[2 * px + py], outs[i].at[j], send.at[3 * i + j], recv.at[3 * i + j],
                                             device_id=(px, py, c), device_id_type=MESH)
                for i in range(n) for j, (px, py) in enumerate(_other_chips(x, y))]

    def start(*a):
        for cp in copies(*a):
            cp.start()

    def finish(*a):
        for cp in copies(*a):
            cp.wait()

    return Rider(ps, [jax.ShapeDtypeStruct((3,) + p.shape[1:], p.dtype) for p in ps],
                 [pltpu.SemaphoreType.DMA((3 * n,))] * 2, start, None, finish)


def pair_swap_halves(gs, *, name):
    n = len(gs)

    def body(*refs):
        ins, outs = refs[:n], refs[n:2 * n]
        send, recv = refs[2 * n:]
        x, y, c = _pos()
        cps = []
        for i in range(n):
            cp = pltpu.make_async_remote_copy(ins[i].at[1 - c], outs[i], send.at[i], recv.at[i],
                                              device_id=(x, y, 1 - c), device_id_type=MESH)
            cp.start()
            cps.append(cp)
        for cp in cps:
            cp.wait()

    return pl.pallas_call(
        body, name=name, in_specs=[ANY] * n, out_specs=[ANY] * n,
        out_shape=[jax.ShapeDtypeStruct(g.shape[1:], g.dtype) for g in gs],
        scratch_shapes=[pltpu.SemaphoreType.DMA((n,)), pltpu.SemaphoreType.DMA((n,))],
        compiler_params=pltpu.CompilerParams(has_side_effects=True),
    )(*gs)


def pair_share(hs, *, name):
    n = len(hs)

    def body(*refs):
        ins, outs = refs[:n], refs[n:2 * n]
        send, recv = refs[2 * n:]
        x, y, c = _pos()
        cps = []
        for i in range(n):
            cp = pltpu.make_async_remote_copy(ins[i], outs[i], send.at[i], recv.at[i],
                                              device_id=(x, y, 1 - c), device_id_type=MESH)
            cp.start()
            cps.append(cp)
        for cp in cps:
            cp.wait()

    return pl.pallas_call(
        body, name=name, in_specs=[ANY] * n, out_specs=[ANY] * n,
        out_shape=[jax.ShapeDtypeStruct(h.shape, h.dtype) for h in hs],
        scratch_shapes=[pltpu.SemaphoreType.DMA((n,)), pltpu.SemaphoreType.DMA((n,))],
        compiler_params=pltpu.CompilerParams(has_side_effects=True),
    )(*hs)


def gather_all(buf, *, name):
    def body(in_ref, out_ref, send, recv, loc):
        x, y, c = _pos()
        lid = 4 * x + 2 * y + c
        lc = pltpu.make_async_copy(in_ref, out_ref.at[lid], loc.at[0])
        lc.start()
        cps = []
        for k in range(1, 8):
            px = 1 - x if k & 4 else x
            py = 1 - y if k & 2 else y
            pc = 1 - c if k & 1 else c
            cp = pltpu.make_async_remote_copy(in_ref, out_ref.at[lid], send.at[k - 1], recv.at[k - 1],
                                              device_id=(px, py, pc), device_id_type=MESH)
            cp.start()
            cps.append((cp, 4 * px + 2 * py + pc, (px, py, pc)))
        for k, (cp, plid, peer) in enumerate(cps):
            cp.wait_send()
            pltpu.make_async_remote_copy(in_ref, out_ref.at[plid], send.at[k], recv.at[k],
                                         device_id=peer, device_id_type=MESH).wait_recv()
        lc.wait()

    return pl.pallas_call(
        body, name=name, in_specs=[ANY], out_specs=ANY,
        out_shape=jax.ShapeDtypeStruct((8,) + buf.shape, buf.dtype),
        scratch_shapes=[pltpu.SemaphoreType.DMA((7,)), pltpu.SemaphoreType.DMA((7,)), pltpu.SemaphoreType.DMA((1,))],
        compiler_params=pltpu.CompilerParams(has_side_effects=True),
    )(buf)


def sum_slots(a, *, name):
    n = a.shape[0]

    def body(a_ref, o_ref):
        acc = a_ref[0]
        for k in range(1, n):
            acc = acc + a_ref[k]
        o_ref[...] = acc

    return pl.pallas_call(body, name=name, out_shape=jax.ShapeDtypeStruct(a.shape[1:], a.dtype),
                          compiler_params=pltpu.CompilerParams(vmem_limit_bytes=VMEM_LIMIT))(a)


def _row_tile(r, want, mult=16):
    t = (min(want, r) // mult) * mult
    while t >= mult:
        if r % t == 0:
            return t
        t -= mult
    return r


def pair_add(g, r1, csel, *, name):
    _, _, r, C = g.shape
    tr = _row_tile(r, 256)

    def body(g_ref, r_ref, c_ref, p32_ref, pb_ref):
        south = c_ref[0:1, 0:1] == 0.0
        p = jnp.where(south, g_ref[0, 0], g_ref[1, 0]) + r_ref[0]
        p32_ref[0] = p
        pb_ref[0] = p.astype(BF16)

    return pl.pallas_call(
        body, name=name, grid=(4, r // tr),
        in_specs=[pl.BlockSpec((2, 1, tr, C), lambda j, t: (0, j, t, 0)), pl.BlockSpec((1, tr, C), lambda j, t: (j, t, 0)),
                  pl.BlockSpec((1, LANES), lambda j, t: (0, 0))],
        out_specs=[pl.BlockSpec((1, tr, C), lambda j, t: (j, t, 0))] * 2,
        out_shape=[jax.ShapeDtypeStruct((4, r, C), F32), jax.ShapeDtypeStruct((4, r, C), BF16)],
        compiler_params=_params("parallel", "parallel"),
    )(g, r1, csel)


def chip_add(p32, r3, msel, *, name):
    _, r, C = p32.shape
    tr = _row_tile(r, 128)

    def body(p_ref, r_ref, m_ref, o_ref):
        me = m_ref[0:1, 0:1]
        acc = jnp.where(me == 0.0, p_ref[0], jnp.where(me == 1.0, p_ref[1], jnp.where(me == 2.0, p_ref[2], p_ref[3])))
        for j in range(3):
            acc = acc + r_ref[j].astype(F32)
        o_ref[...] = acc

    return pl.pallas_call(
        body, name=name, grid=(r // tr,),
        in_specs=[pl.BlockSpec((4, tr, C), lambda t: (0, t, 0)), pl.BlockSpec((3, tr, C), lambda t: (0, t, 0)),
                  pl.BlockSpec((1, LANES), lambda t: (0, 0))],
        out_specs=pl.BlockSpec((tr, C), lambda t: (t, 0)),
        out_shape=jax.ShapeDtypeStruct((r, C), F32),
        compiler_params=_params("parallel"),
    )(p32, r3, msel)


def _consts():
    i512 = np.arange(N_HEADS * HEAD_DIM)
    i128 = np.arange(LANES)
    bd512 = (i512[:, None] // HEAD_DIM == i512[None, :] // HEAD_DIM).astype(np.float32)
    bd128 = (i128[:, None] // HEAD_DIM == i128[None, :] // HEAD_DIM).astype(np.float32)
    fold = (np.arange(HEAD_DIM)[:, None] == (i512[None, :] % HEAD_DIM)).astype(np.float32)
    grp = N_HEADS // 2 * HEAD_DIM
    expand = ((i128[:, None] // HEAD_DIM == i512[None, :] // grp)
              & (i128[:, None] % HEAD_DIM == i512[None, :] % HEAD_DIM)).astype(np.float32)
    band = (B_PREV + 1) * CHUNK
    rel = np.arange(CHUNK)[:, None] - (np.arange(band)[None, :] - B_PREV * CHUNK)
    slopes = 2.0 ** (-8.0 * np.arange(1, N_HEADS + 1, dtype=np.float32) / N_HEADS)
    bias_b = (-slopes[:, None, None] * np.abs(rel).astype(np.float32)[None]).astype(np.float32)
    return [jnp.asarray(a) for a in (bd512, bd128, fold, expand)], jnp.asarray(bias_b)


def _ffn_fwd(xin, l, W, P):
    g = P["norm_ffn"][l:l + 1]
    (h,) = rowwise(f_rmsnorm, [xin], [g], [(D_MODEL, BF16)], name=f"ffn{l}_norm")
    Wi = W["ffn_in"][l]
    gate = matmul(h, Wi[:, :D_FF], mode="nn", name=f"ffn{l}_gate")
    up = matmul(h, Wi[:, D_FF:], mode="nn", name=f"ffn{l}_up")
    gc, act = dwconv_fwd(gate, P["ffn_conv_w"][l], P["ffn_conv_b"][l:l + 1], lambda y, u: (y, _silu(y) * u), [up],
                         [F32, BF16], name=f"ffn{l}_conv")
    xout = matmul(act, W["ffn_out"][l], mode="nn", name=f"ffn{l}_out", residual=xin)
    return xout, (xin, h, gate, gc, up, act)


def _ffn_bwd(dxout, l, saved, W, P):
    xin, h, gate, gc, up, act = saved
    g = P["norm_ffn"][l:l + 1]
    Wi = W["ffn_in"][l]
    dact = matmul(dxout, W["ffn_out"][l], mode="nt", name=f"ffn{l}_dact")
    dWo = matmul(act, dxout, mode="tn", name=f"ffn{l}_dwout")
    dgate, dcw, dcb, dup = dwconv_bwd(gate, P["ffn_conv_w"][l], [gc, up, dact],
                                      lambda c, u, da: (da * u * _dsilu(c), da * _silu(c)), [BF16],
                                      name=f"ffn{l}_dconv")
    dh = matmul(dgate, Wi[:, :D_FF], mode="nt", name=f"ffn{l}_dh_gate")
    dh = matmul(dup, Wi[:, D_FF:], mode="nt", name=f"ffn{l}_dh_up", residual=dh)
    dWi = jnp.concatenate([matmul(h, dgate, mode="tn", name=f"ffn{l}_dw_gate"),
                           matmul(h, dup, mode="tn", name=f"ffn{l}_dw_up")], axis=1)
    dxin, dg = rowwise_vjp(f_norm_id, [xin], [g], [dh, dxout], [(0, F32)], [0], name=f"ffn{l}_dnorm")
    return dxin, dWi, dWo, dg, dcw, dcb


class NoComm:
    def fwd_rider(self, tag):
        return None

    def fwd_done(self, tag, outs, W, P):
        pass

    def grads(self, tag, cols, rows):
        return None

    def bwd_done(self, tag, outs):
        pass


def local_step(x, tgt, W, P, comm):
    qk_consts, bias_b = _consts()
    pad_rows = lambda t, n: jnp.pad(t, ((n * CHUNK, 0), (0, 0)))
    DQ = N_HEADS * HEAD_DIM

    g_mix0 = P["norm_mix"][0:1]
    (h0,) = rowwise(f_rmsnorm, [x], [g_mix0], [(D_MODEL, BF16)], name="attn_norm")
    qkv = matmul(h0, W["attn_in"], mode="nn", name="attn_qkv")
    qk_par = [P["q_norm_a"], P["k_norm_a"], P["q_norm_b"], P["k_norm_b"]] + qk_consts
    qa, ka, va, qb, kb, vb = rowwise(f_qknorm, [qkv], qk_par, [(DQ, BF16)] * 6, name="attn_qknorm")
    ka, va, kb, vb = pad_rows(ka, A_PREV), pad_rows(va, A_PREV), pad_rows(kb, B_PREV), pad_rows(vb, B_PREV)
    table = jnp.pad(P["relpos_table"], ((0, 0), (0, REL_W - (2 * MAX_REL + 1))))
    nj = min(ATT_TQ, x.shape[0]) // CHUNK
    bias_a = widen_bias(jnp.transpose(relpos_bias(table, name="relpos_bias"), (1, 0, 2)), A_PREV, nj)
    bias_b = widen_bias(bias_b, B_PREV, nj)
    sinks = jnp.broadcast_to(P["sinks"].reshape(N_HEADS // 2, 2, 1), (N_HEADS // 2, 2, LANES))
    oa, late = attn_fwd(qa, ka, va, bias_a, None, n_prev=A_PREV, name="attn_a", rider=comm.fwd_rider("a"))
    comm.fwd_done("a", late, W, P)
    ob, late = attn_fwd(qb, kb, vb, bias_b, sinks, n_prev=B_PREV, name="attn_b", rider=comm.fwd_rider("b"))
    comm.fwd_done("b", late, W, P)
    Wao = W["attn_out"]
    x1 = matmul(oa, Wao[:DQ], mode="nn", name="attn_out_a", residual=x)
    x1 = matmul(ob, Wao[DQ:], mode="nn", name="attn_out_b", residual=x1)
    x2, ffn0 = _ffn_fwd(x1, 0, W, P)

    g_mix1 = P["norm_mix"][1:2]
    (h2,) = rowwise(f_rmsnorm, [x2], [g_mix1], [(D_MODEL, BF16)], name="ssm_norm_in")
    Ws = W["ssm_in"]
    CC = D_INNER + 2 * SSM_GROUPS * SSM_STATE
    Wz, Wx = Ws[:, :D_INNER], Ws[:, D_INNER:D_INNER + CC]
    Wdt = jnp.pad(Ws[:, D_INNER + CC:], ((0, 0), (0, LANES - SSM_HEADS)))
    z = matmul(h2, Wz, mode="nn", name="ssm_z")
    xr = matmul(h2, Wx, mode="nn", name="ssm_xbc")
    dtraw = matmul(h2, Wdt, mode="nn", name="ssm_dt")
    xc, xbc = dwconv_fwd(xr, P["ssm_conv_w"], P["ssm_conv_b"], lambda y: (y, _silu(y)), [], [F32, F32],
                         name="ssm_conv")
    pad32 = lambda v: jnp.pad(v, ((0, 0), (0, LANES - SSM_HEADS)))
    A = pad32(-jnp.exp(P["ssm_a_log"]))
    dtb = pad32(P["ssm_dt_bias"])
    dexp = jnp.repeat(P["ssm_d"], D_INNER // SSM_HEADS, axis=1)
    y, states = ssd_fwd(xbc, dtraw, dtb, A, dexp, name="ssd_fwd")
    (y2,) = rowwise(f_gate_norm, [y, z], [P["ssm_norm"]], [(D_INNER, BF16)], name="ssm_gate_norm")
    x3 = matmul(y2, W["ssm_out"], mode="nn", name="ssm_out", residual=x2)
    x4, ffn1 = _ffn_fwd(x3, 1, W, P)

    dx4, lpart = loss_head(x4, tgt, name="loss_head")

    dx3, dWfi1, dWfo1, dgf1, dfcw1, dfcb1 = _ffn_bwd(dx4, 1, ffn1, W, P)
    out_f1 = comm.grads("f1", dWfi1, dWfo1)
    dy2 = matmul(dx3, W["ssm_out"], mode="nt", name="ssm_dy")
    dWso = matmul(y2, dx3, mode="tn", name="ssm_dwout")
    dy, dz, dnw = rowwise_vjp(f_gate_norm, [y, z], [P["ssm_norm"]], [dy2], [(0, F32), (1, F32)], [0],
                              name="ssm_dgate_norm")
    (dxs, dB, dC, ddtraw, dA, ddtb, dDl), sent = ssd_bwd(xbc, dtraw, dtb, A, dexp, states, dy, name="ssd_bwd",
                                                          rider=out_f1)
    comm.bwd_done("f1", sent)
    dxr, dscw, dscb = dwconv_bwd(xr, P["ssm_conv_w"], [xc, (dxs, dB, dC)], lambda c, g: (g * _dsilu(c),), [],
                                 name="ssm_dconv")
    dh2 = matmul(dz, Wz, mode="nt", name="ssm_dh_z")
    dh2 = matmul(dxr, Wx, mode="nt", name="ssm_dh_x", residual=dh2)
    dh2 = matmul(ddtraw, Wdt, mode="nt", name="ssm_dh_dt", residual=dh2)
    dWs = jnp.concatenate([matmul(h2, dz, mode="tn", name="ssm_dw_z"),
                           matmul(h2, dxr, mode="tn", name="ssm_dw_x"),
                           matmul(h2, ddtraw, mode="tn", name="ssm_dw_dt")[:, :SSM_HEADS]], axis=1)
    dx2, dgm1 = rowwise_vjp(f_norm_id, [x2], [g_mix1], [dh2, dx3], [(0, F32)], [0], name="ssm_dnorm_in")
    out_s = comm.grads("s", dWs, dWso)

    dx1, dWfi0, dWfo0, dgf0, dfcw0, dfcb0 = _ffn_bwd(dx2, 0, ffn0, W, P)
    out_f0 = comm.grads("f0", dWfi0, dWfo0)
    doa = matmul(dx1, Wao[:DQ], mode="nt", name="attn_do_a", out_dtype=BF16)
    dob = matmul(dx1, Wao[DQ:], mode="nt", name="attn_do_b", out_dtype=BF16)
    dWao = jnp.concatenate([matmul(oa, dx1, mode="tn", name="attn_dwout_a"),
                            matmul(ob, dx1, mode="tn", name="attn_dwout_b")], axis=0)
    (dqa, dka, dva, dbias_a), sent = attn_bwd(qa, ka, va, doa, bias_a, None, n_prev=A_PREV, name="attn_a_bwd",
                                              rider=out_s)
    comm.bwd_done("s", sent)
    (dqb, dkb, dvb, _, dsk), sent = attn_bwd(qb, kb, vb, dob, bias_b, sinks, n_prev=B_PREV, name="attn_b_bwd",
                                             rider=out_f0)
    comm.bwd_done("f0", sent)
    pa, pb = A_PREV * CHUNK, B_PREV * CHUNK
    cots = [dqa, dka[pa:], dva[pa:], dqb, dkb[pb:], dvb[pb:]]
    dqkv, dgqa, dgka, dgqb, dgkb = rowwise_vjp(f_qknorm, [qkv], qk_par, cots, [(0, BF16)], [0, 1, 2, 3],
                                               name="attn_dqknorm")
    dh0 = matmul(dqkv, W["attn_in"], mode="nt", name="attn_dh")
    dWai = matmul(h0, dqkv, mode="tn", name="attn_dwin")
    dx, dgm0 = rowwise_vjp(f_norm_id, [x], [g_mix0], [dh0, dx1], [(0, F32)], [0], name="attn_dnorm")
    dbias_a = fold_bias(dbias_a, A_PREV, nj)
    dtable = relpos_grad(jnp.transpose(dbias_a, (1, 0, 2)), name="relpos_grad")[:, :2 * MAX_REL + 1]

    gW = {"attn_in": dWai, "attn_out": dWao, "ssm_in": dWs, "ssm_out": dWso,
          "ffn_in": [dWfi0, dWfi1], "ffn_out": [dWfo0, dWfo1]}
    gP = {"norm_mix": jnp.concatenate([dgm0, dgm1], axis=0),
          "norm_ffn": jnp.concatenate([dgf0, dgf1], axis=0),
          "relpos_table": dtable, "q_norm_a": dgqa, "k_norm_a": dgka, "q_norm_b": dgqb, "k_norm_b": dgkb,
          "sinks": dsk[:, :, 0].reshape(1, N_HEADS),
          "ssm_conv_w": dscw, "ssm_conv_b": dscb,
          "ssm_dt_bias": ddtb[:, :SSM_HEADS], "ssm_a_log": dA[:, :SSM_HEADS] * A[:, :SSM_HEADS],
          "ssm_d": dDl.reshape(SSM_HEADS, D_INNER // SSM_HEADS).sum(axis=1).reshape(1, SSM_HEADS),
          "ssm_norm": dnw,
          "ffn_conv_w": jnp.stack([dfcw0, dfcw1]), "ffn_conv_b": jnp.concatenate([dfcb0, dfcb1], axis=0)}
    return lpart, dx, gW, gP


WEIGHTS = ["norm_mix", "norm_ffn", "attn_w_in", "attn_w_out", "relpos_table", "q_norm_a", "k_norm_a", "q_norm_b",
           "k_norm_b", "sinks", "ssm_w_in", "ssm_conv_w", "ssm_conv_b", "ssm_dt_bias", "ssm_a_log", "ssm_d",
           "ssm_norm", "ssm_w_out", "ffn_w_in", "ffn_conv_w", "ffn_conv_b", "ffn_w_out"]
ARGS = ["x"] + WEIGHTS + ["loss_target"] + ["m_" + w for w in WEIGHTS] + ["v_" + w for w in WEIGHTS]
N_CHIPS = 4
SMALL_ROWS = 384
SMALL_ORDER = ["norm_mix", "norm_ffn", "relpos_table", "q_norm_a", "k_norm_a", "q_norm_b", "k_norm_b", "sinks",
               "ssm_dt_bias", "ssm_a_log", "ssm_d", "ffn_conv_b", "ssm_conv_w", "ssm_conv_b", "ssm_norm", "ffn_conv_w"]


def _cols_to_slabs(g):
    K, N = g.shape
    return g.reshape(2, K // 2, N_CHIPS, N // N_CHIPS).transpose(0, 2, 1, 3)


def _rows_to_slabs(g):
    R, C = g.shape
    return g.reshape(N_CHIPS, 2, R // (2 * N_CHIPS), C).transpose(1, 0, 2, 3)


class MeshComm:
    def __init__(self, d, xi, yi, ci):
        self.d, self.ci, self.me = d, ci, 2 * xi + yi
        self.csel = jnp.full((1, LANES), ci, F32)
        self.msel = jnp.full((1, LANES), self.me, F32)
        halves = lambda w: w.reshape((2, -1, w.shape[-1]))
        small = jnp.concatenate([d[k].reshape(-1) for k in ("ssm_conv_w", "ssm_conv_b", "ssm_norm", "ffn_conv_w")])
        small = jnp.pad(small, (0, 2 * 40 * LANES - small.shape[0])).reshape(2, 40, LANES)
        self.shards = {"attn": [halves(d["attn_w_in"][0].astype(BF16)), halves(d["attn_w_out"][0].astype(BF16))],
                       "a": [d["ffn_w_in"].astype(BF16), d["ffn_w_out"].astype(BF16), small],
                       "b": [halves(d["ssm_w_in"][0].astype(BF16)), halves(d["ssm_w_out"][0].astype(BF16))]}
        self.p32, self.mine = {}, {}

    def _whole(self, tag, outs):
        return [lax.dynamic_update_slice_in_dim(g, s[None], self.me, axis=0) for g, s in zip(outs, self.shards[tag])]

    @staticmethod
    def _cat_cols(g):
        return jnp.concatenate([g[j].reshape((-1, g.shape[-1])) for j in range(N_CHIPS)], axis=1)

    def first_weights(self):
        g_ai, g_ao = self._whole("attn", run_rider(gather_rider(self.shards["attn"]), name="gather_attn"))
        return {"attn_in": self._cat_cols(g_ai), "attn_out": g_ao.reshape(-1, D_MODEL)}

    def fwd_rider(self, tag):
        return gather_rider(self.shards[tag])

    def fwd_done(self, tag, outs, W, P):
        if tag == "b":
            g_si, g_so = self._whole("b", outs)
            W["ssm_in"], W["ssm_out"] = self._cat_cols(g_si), g_so.reshape(-1, D_MODEL)
            return
        g_fi, g_fo, g_sm = self._whole("a", outs)
        W["ffn_in"] = [jnp.concatenate([g_fi[j, l] for j in range(N_CHIPS)], axis=1) for l in range(2)]
        W["ffn_out"] = [g_fo[:, l].reshape(-1, D_MODEL) for l in range(2)]
        sm = g_sm.reshape(N_CHIPS, -1)
        CC = D_INNER + 2 * SSM_GROUPS * SSM_STATE
        c4, f4 = CC // N_CHIPS, D_FF // N_CHIPS
        o1 = SSM_CONV * c4
        o2 = o1 + c4
        o3 = o2 + D_INNER // N_CHIPS
        o4 = o3 + 2 * FFN_CONV * f4
        P["ssm_conv_w"] = sm[:, :o1].reshape(N_CHIPS, SSM_CONV, c4).transpose(1, 0, 2).reshape(SSM_CONV, CC)
        P["ssm_conv_b"] = sm[:, o1:o2].reshape(1, CC)
        P["ssm_norm"] = sm[:, o2:o3].reshape(1, D_INNER)
        P["ffn_conv_w"] = sm[:, o3:o4].reshape(N_CHIPS, 2, FFN_CONV, f4).transpose(1, 2, 0, 3).reshape(2, FFN_CONV, D_FF)

    def grads(self, tag, cols, rows):
        slabs = [_cols_to_slabs(cols), _rows_to_slabs(rows)]
        from_sib = pair_swap_halves(slabs, name="grad_pair_swap_" + tag)
        pairs = [pair_add(g, r, self.csel, name=f"grad_pair_add_{tag}{i}") for i, (g, r) in enumerate(zip(slabs, from_sib))]
        self.p32[tag] = [p[0] for p in pairs]
        return scatter_rider([p[1] for p in pairs])

    def bwd_done(self, tag, outs):
        self.mine[tag] = [chip_add(p, r, self.msel, name=f"grad_chip_add_{tag}{i}")
                          for i, (p, r) in enumerate(zip(self.p32[tag], outs))]

    def finish(self, d_attn_in, d_attn_out):
        self.bwd_done("at", run_rider(self.grads("at", d_attn_in, d_attn_out), name="grad_scatter_at"))
        order = ["at", "s", "f0", "f1"]
        mine = [m for t in order for m in self.mine[t]]
        theirs = pair_share(mine, name="grad_pair_share")
        full = [jnp.where(self.ci == 0, jnp.stack([a, b]), jnp.stack([b, a])).reshape((-1, a.shape[-1]))
                for a, b in zip(mine, theirs)]
        ai, ao, si, so, fi0, fo0, fi1, fo1 = full
        return {"attn_w_in": ai[None], "attn_w_out": ao[None], "ssm_w_in": si[None], "ssm_w_out": so[None],
                "ffn_w_in": jnp.stack([fi0, fi1]), "ffn_w_out": jnp.stack([fo0, fo1])}


def _adamw(w, g, m, v, name):
    shp = w.shape
    two = lambda a: a.reshape((-1, shp[-1]))
    outs = [(shp[-1], F32)] * 3
    d, nm, nv = rowwise(f_adamw, [two(w), two(g), two(m), two(v)], [], outs, name="adamw_" + name)
    return d.reshape(shp), nm.reshape(shp), nv.reshape(shp)


def kernel(x, norm_mix, norm_ffn, attn_w_in, attn_w_out, relpos_table, q_norm_a, k_norm_a, q_norm_b, k_norm_b, sinks, ssm_w_in, ssm_conv_w, ssm_conv_b, ssm_dt_bias, ssm_a_log, ssm_d, ssm_norm, ssm_w_out, ffn_w_in, ffn_conv_w, ffn_conv_b, ffn_w_out, loss_target, m_norm_mix, m_norm_ffn, m_attn_w_in, m_attn_w_out, m_relpos_table, m_q_norm_a, m_k_norm_a, m_q_norm_b, m_k_norm_b, m_sinks, m_ssm_w_in, m_ssm_conv_w, m_ssm_conv_b, m_ssm_dt_bias, m_ssm_a_log, m_ssm_d, m_ssm_norm, m_ssm_w_out, m_ffn_w_in, m_ffn_conv_w, m_ffn_conv_b, m_ffn_w_out, v_norm_mix, v_norm_ffn, v_attn_w_in, v_attn_w_out, v_relpos_table, v_q_norm_a, v_k_norm_a, v_q_norm_b, v_k_norm_b, v_sinks, v_ssm_w_in, v_ssm_conv_w, v_ssm_conv_b, v_ssm_dt_bias, v_ssm_a_log, v_ssm_d, v_ssm_norm, v_ssm_w_out, v_ffn_w_in, v_ffn_conv_w, v_ffn_conv_b, v_ffn_w_out):
    d = dict(zip(ARGS, (x, norm_mix, norm_ffn, attn_w_in, attn_w_out, relpos_table, q_norm_a, k_norm_a, q_norm_b, k_norm_b, sinks, ssm_w_in, ssm_conv_w, ssm_conv_b, ssm_dt_bias, ssm_a_log, ssm_d, ssm_norm, ssm_w_out, ffn_w_in, ffn_conv_w, ffn_conv_b, ffn_w_out, loss_target, m_norm_mix, m_norm_ffn, m_attn_w_in, m_attn_w_out, m_relpos_table, m_q_norm_a, m_k_norm_a, m_q_norm_b, m_k_norm_b, m_sinks, m_ssm_w_in, m_ssm_conv_w, m_ssm_conv_b, m_ssm_dt_bias, m_ssm_a_log, m_ssm_d, m_ssm_norm, m_ssm_w_out, m_ffn_w_in, m_ffn_conv_w, m_ffn_conv_b, m_ffn_w_out, v_norm_mix, v_norm_ffn, v_attn_w_in, v_attn_w_out, v_relpos_table, v_q_norm_a, v_k_norm_a, v_q_norm_b, v_k_norm_b, v_sinks, v_ssm_w_in, v_ssm_conv_w, v_ssm_conv_b, v_ssm_dt_bias, v_ssm_a_log, v_ssm_d, v_ssm_norm, v_ssm_w_out, v_ffn_w_in, v_ffn_conv_w, v_ffn_conv_b, v_ffn_w_out)))
    xi, yi, ci = _pos()
    me = 2 * xi + yi
    CC = D_INNER + 2 * SSM_GROUPS * SSM_STATE
    c4, f4 = CC // N_CHIPS, D_FF // N_CHIPS

    P = {k: d[k] for k in ["norm_mix", "norm_ffn", "q_norm_a", "k_norm_a", "q_norm_b", "k_norm_b", "sinks",
                           "ssm_dt_bias", "ssm_a_log", "ssm_d", "ffn_conv_b"]}
    P["relpos_table"] = d["relpos_table"][0]
    comm = MeshComm(d, xi, yi, ci)
    W = comm.first_weights()
    lpart, dx, gW, gP = local_step(d["x"][0], d["loss_target"][0], W, P, comm)
    loss = lax.psum(lpart[0, 0], ("x", "y", "c"))
    grads = comm.finish(gW["attn_in"], gW["attn_out"])

    flat = jnp.concatenate([gP[k].reshape(-1) for k in SMALL_ORDER])
    flat = jnp.pad(flat, (0, SMALL_ROWS * LANES - flat.shape[0])).reshape(SMALL_ROWS, LANES)
    tot = sum_slots(gather_all(flat, name="small_gather"), name="small_sum").reshape(-1)
    off = 0
    for k in SMALL_ORDER:
        n = int(np.prod(gP[k].shape))
        g = tot[off:off + n].reshape(gP[k].shape)
        off += n
        if k == "ssm_conv_w":
            g = lax.dynamic_slice_in_dim(g, me * c4, c4, axis=1)[None]
        elif k == "ssm_conv_b":
            g = lax.dynamic_slice_in_dim(g, me * c4, c4, axis=1)
        elif k == "ssm_norm":
            g = lax.dynamic_slice_in_dim(g, me * (D_INNER // N_CHIPS), D_INNER // N_CHIPS, axis=1)
        elif k == "ffn_conv_w":
            g = lax.dynamic_slice_in_dim(g, me * f4, f4, axis=2)
        elif k == "relpos_table":
            g = g[None]
        grads[k] = g

    deltas, new_m, new_v = {}, {}, {}
    for k in WEIGHTS:
        deltas[k], new_m[k], new_v[k] = _adamw(d[k], grads[k], d["m_" + k], d["v_" + k], k)
    return (loss, dx[None], *[grads[k] for k in WEIGHTS], *[deltas[k] for k in WEIGHTS],
            *[new_m[k] for k in WEIGHTS], *[new_v[k] for k in WEIGHTS])
```

```python
import functools

import numpy as np
import jax
import jax.numpy as jnp
from jax import lax
from jax.experimental import pallas as pl
from jax.experimental.pallas import tpu as pltpu

F32 = jnp.float32
BF16 = jnp.bfloat16
HI = lax.Precision.HIGHEST

D_MODEL = 1024
CHUNK = 64
EPS = 1e-6
HEAD_DIM = 64
N_HEADS = 8
A_PREV = 8
B_PREV = 2
MAX_REL = 256
D_INNER = 2048
SSM_HEADS = 32
SSM_GROUPS = 4
SSM_STATE = 128
SSM_CONV = 4
D_FF = 2816
FFN_CONV = 3
LANES = 128
SUBLANES = 8
VMEM_LIMIT = 56 * 1024 * 1024
SSD_L = 128

ADAM_LR = 0.001
ADAM_B1 = 0.9
ADAM_B2 = 0.999
ADAM_EPS = 1e-08
ADAM_WD = 0.01
ADAM_STEP = 10

MESH = pl.DeviceIdType.MESH


def _params(*sem):
    return pltpu.CompilerParams(dimension_semantics=sem, vmem_limit_bytes=VMEM_LIMIT)


def _pick(n, want):
    if n <= want:
        return n
    t = (want // LANES) * LANES
    while t >= LANES:
        if n % t == 0:
            return t
        t -= LANES
    return n


MM_ROWS = 512
MM_COLS = 1536
MM_RED = 2048


def matmul(a, b, *, mode, name, out_dtype=F32, residual=None):
    dims = {"nn": (((1,), (0,)), ((), ())), "nt": (((1,), (1,)), ((), ())), "tn": (((0,), (0,)), ((), ()))}[mode]
    if mode == "tn":
        assert residual is None and out_dtype == F32
        (K, M), (K2, N) = a.shape, b.shape
        assert K == K2, (a.shape, b.shape)
        tm, tn, tk = _pick(M, MM_COLS), _pick(N, MM_COLS), _pick(K, MM_RED)

        def body(a_ref, b_ref, o_ref):
            k = pl.program_id(2)
            p = lax.dot_general(a_ref[...].astype(BF16), b_ref[...].astype(BF16), dims, preferred_element_type=F32)

            @pl.when(k == 0)
            def _():
                o_ref[...] = p

            @pl.when(k != 0)
            def _():
                o_ref[...] += p

        return pl.pallas_call(
            body, name=name, grid=(M // tm, N // tn, K // tk),
            in_specs=[pl.BlockSpec((tk, tm), lambda i, j, k: (k, i)), pl.BlockSpec((tk, tn), lambda i, j, k: (k, j))],
            out_specs=pl.BlockSpec((tm, tn), lambda i, j, k: (i, j)),
            out_shape=jax.ShapeDtypeStruct((M, N), F32),
            compiler_params=_params("parallel", "parallel", "arbitrary"),
        )(a, b)

    if mode == "nn":
        (M, K), (K2, N) = a.shape, b.shape
    else:
        (M, K), (N, K2) = a.shape, b.shape
    assert K == K2, (a.shape, b.shape, mode)
    tm, tn = _pick(M, MM_ROWS), _pick(N, MM_COLS)

    def body(*refs):
        a_ref, b_ref = refs[:2]
        o_ref = refs[-1]
        r = lax.dot_general(a_ref[...].astype(BF16), b_ref[...].astype(BF16), dims, preferred_element_type=F32)
        if residual is not None:
            r = r + refs[2][...].astype(F32)
        o_ref[...] = r.astype(o_ref.dtype)

    a_spec = pl.BlockSpec((tm, K), lambda j, i: (i, 0))
    b_spec = pl.BlockSpec((K, tn), lambda j, i: (0, j)) if mode == "nn" else pl.BlockSpec((tn, K), lambda j, i: (j, 0))
    o_spec = pl.BlockSpec((tm, tn), lambda j, i: (i, j))
    in_specs = [a_spec, b_spec] + ([o_spec] if residual is not None else [])
    args = (a, b) + ((residual,) if residual is not None else ())
    return pl.pallas_call(
        body, name=name, grid=(N // tn, M // tm),
        in_specs=in_specs, out_specs=o_spec,
        out_shape=jax.ShapeDtypeStruct((M, N), out_dtype),
        compiler_params=_params("parallel", "parallel"),
    )(*args)


def rowwise(f, rows, params, outs, *, name, tm=256):
    S = rows[0].shape[0]
    tm = _row_tile(S, tm)
    nr, npar = len(rows), len(params)

    def body(*refs):
        vals = [r[...] for r in refs[:nr + npar]]
        res = f(*vals)
        for o_ref, r in zip(refs[nr + npar:], res):
            o_ref[...] = r.astype(o_ref.dtype)

    in_specs = [pl.BlockSpec((tm, r.shape[1]), lambda i: (i, 0)) for r in rows]
    in_specs += [pl.BlockSpec(p.shape, lambda i: (0, 0)) for p in params]
    out_specs = [pl.BlockSpec((tm, c), lambda i: (i, 0)) for c, _ in outs]
    out_shape = [jax.ShapeDtypeStruct((S, c), dt) for c, dt in outs]
    return pl.pallas_call(body, name=name, grid=(S // tm,), in_specs=in_specs, out_specs=out_specs,
                          out_shape=out_shape, compiler_params=_params("parallel"))(*rows, *params)


def rowwise_vjp(f, rows, params, cots, drow, dpar, *, name, tm=256, cot_skip=None):
    S = rows[0].shape[0]
    tm = _row_tile(S, tm)
    nr, npar, nc = len(rows), len(params), len(cots)
    skip = [0] * nc if cot_skip is None else [s // tm for s in cot_skip]
    assert cot_skip is None or all(s % tm == 0 for s in cot_skip)

    def body(*refs):
        vals = [r[...] for r in refs[:nr + npar]]
        cvals = [r[...].astype(F32) for r in refs[nr + npar:nr + npar + nc]]
        o_refs = refs[nr + npar + nc:]
        want = [ri for ri, _ in drow] + [nr + pi for pi in dpar]

        def f_want(*d):
            full = list(vals)
            for k, v in zip(want, d):
                full[k] = v
            return f(*full)

        _, vjp = jax.vjp(f_want, *[vals[k] for k in want])
        grads = vjp(tuple(cvals))
        for o_ref, g in zip(o_refs[:len(drow)], grads):
            o_ref[...] = g.astype(o_ref.dtype)
        first = pl.program_id(0) == 0
        for o_ref, g in zip(o_refs[len(drow):], grads[len(drow):]):
            g = g.astype(F32)

            @pl.when(first)
            def _(o_ref=o_ref, g=g):
                o_ref[...] = g

            @pl.when(jnp.logical_not(first))
            def _(o_ref=o_ref, g=g):
                o_ref[...] += g

    in_specs = [pl.BlockSpec((tm, r.shape[1]), lambda i: (i, 0)) for r in rows]
    in_specs += [pl.BlockSpec(p.shape, lambda i: (0, 0)) for p in params]
    in_specs += [pl.BlockSpec((tm, c.shape[1]), lambda i, s=s: (i + s, 0)) for c, s in zip(cots, skip)]
    out_specs = [pl.BlockSpec((tm, rows[ri].shape[1]), lambda i: (i, 0)) for ri, _ in drow]
    out_specs += [pl.BlockSpec(params[pi].shape, lambda i: (0, 0)) for pi in dpar]
    out_shape = [jax.ShapeDtypeStruct(rows[ri].shape, dt) for ri, dt in drow]
    out_shape += [jax.ShapeDtypeStruct(params[pi].shape, F32) for pi in dpar]
    return pl.pallas_call(body, name=name, grid=(S // tm,), in_specs=in_specs, out_specs=out_specs,
                          out_shape=out_shape, compiler_params=_params("arbitrary"))(*rows, *params, *cots)


HALO = SUBLANES


def dwconv_fwd(x, w, b, post, extra, outs, *, name, tm=256):
    S, C = x.shape
    K = w.shape[0]
    tm = min(tm, S)
    hb = tm // HALO
    ne = len(extra)

    def body(*refs):
        x_ref, halo_ref, w_ref, b_ref = refs[:4]
        e_refs = refs[4:4 + ne]
        o_refs = refs[4 + ne:4 + ne + len(outs)]
        buf = refs[-1]
        i = pl.program_id(0)
        buf[0:HALO, :] = jnp.where(i == 0, 0.0, halo_ref[...])
        buf[HALO:HALO + tm, :] = x_ref[...]
        for c0 in range(0, C, LANES):
            cs = slice(c0, c0 + LANES)
            acc = jnp.broadcast_to(b_ref[:, cs], (tm, LANES))
            for k in range(K):
                acc = acc + w_ref[k:k + 1, cs] * buf[pl.ds(HALO - (K - 1) + k, tm), cs]
            for o_ref, r in zip(o_refs, post(acc, *[e[:, cs] for e in e_refs])):
                o_ref[:, cs] = r.astype(o_ref.dtype)

    row = pl.BlockSpec((tm, C), lambda i: (i, 0))
    return pl.pallas_call(
        body, name=name, grid=(S // tm,),
        in_specs=[row,
                  pl.BlockSpec((HALO, C), lambda i: (jnp.maximum(i * hb - 1, 0), 0)),
                  pl.BlockSpec((K, C), lambda i: (0, 0)),
                  pl.BlockSpec((1, C), lambda i: (0, 0))] + [row] * ne,
        out_specs=[row] * len(outs),
        out_shape=[jax.ShapeDtypeStruct((S, C), dt) for dt in outs],
        scratch_shapes=[pltpu.VMEM((HALO + tm, C), F32)],
        compiler_params=_params("parallel"),
    )(x, x, w, b, *extra)


def dwconv_bwd(x, w, srcs, dy_fn, extra_outs, *, name, tm=256):
    S, C = x.shape
    K = w.shape[0]
    tm = min(tm, S)
    hb = tm // HALO
    n = S // tm
    groups = [s if isinstance(s, tuple) else (s,) for s in srcs]
    flat = [a for g in groups for a in g]
    nf = len(flat)

    def body(*refs):
        x_ref, xh_ref, w_ref = refs[:3]
        dx_ref, dw_ref, db_ref = refs[3 + 2 * nf:6 + 2 * nf]
        e_refs = refs[6 + 2 * nf:6 + 2 * nf + len(extra_outs)]
        bx, bd = refs[-2:]

        def strips(first, c0):
            out, at = [], first
            for g in groups:
                off = 0
                for a in g:
                    if off <= c0 < off + a.shape[1]:
                        out.append(refs[at][:, c0 - off:c0 - off + LANES].astype(F32))
                    off += a.shape[1]
                    at += 1
            return out

        i = pl.program_id(0)
        bx[0:HALO, :] = jnp.where(i == 0, 0.0, xh_ref[...])
        bx[HALO:HALO + tm, :] = x_ref[...]

        @pl.when(i == 0)
        def _():
            dw_ref[...] = jnp.zeros_like(dw_ref)
            db_ref[...] = jnp.zeros_like(db_ref)

        for c0 in range(0, C, LANES):
            cs = slice(c0, c0 + LANES)
            res = dy_fn(*strips(3, c0))
            dyv = res[0]
            for e_ref, r in zip(e_refs, res[1:]):
                e_ref[:, cs] = r.astype(e_ref.dtype)
            bd[0:tm, cs] = dyv
            bd[tm:tm + HALO, cs] = jnp.where(i == n - 1, 0.0, dy_fn(*strips(3 + nf, c0))[0])
            acc = jnp.zeros((tm, LANES), F32)
            for k in range(K):
                acc = acc + w_ref[k:k + 1, cs] * bd[pl.ds((K - 1) - k, tm), cs]
            dx_ref[:, cs] = acc.astype(dx_ref.dtype)
            for k in range(K):
                dw_ref[k:k + 1, cs] += jnp.sum(dyv * bx[pl.ds(HALO - (K - 1) + k, tm), cs], axis=0, keepdims=True)
            db_ref[:, cs] += jnp.sum(dyv, axis=0, keepdims=True)

    row = lambda c: pl.BlockSpec((tm, c), lambda i: (i, 0))
    nxt = lambda c: pl.BlockSpec((HALO, c), lambda i: (jnp.minimum((i + 1) * hb, S // HALO - 1), 0))
    return pl.pallas_call(
        body, name=name, grid=(n,),
        in_specs=[row(C), pl.BlockSpec((HALO, C), lambda i: (jnp.maximum(i * hb - 1, 0), 0)),
                  pl.BlockSpec((K, C), lambda i: (0, 0))]
                 + [row(a.shape[1]) for a in flat] + [nxt(a.shape[1]) for a in flat],
        out_specs=[row(C), pl.BlockSpec((K, C), lambda i: (0, 0)), pl.BlockSpec((1, C), lambda i: (0, 0))]
                  + [row(C)] * len(extra_outs),
        out_shape=[jax.ShapeDtypeStruct((S, C), BF16), jax.ShapeDtypeStruct((K, C), F32),
                   jax.ShapeDtypeStruct((1, C), F32)] + [jax.ShapeDtypeStruct((S, C), dt) for dt in extra_outs],
        scratch_shapes=[pltpu.VMEM((HALO + tm, C), F32), pltpu.VMEM((tm + HALO, C), F32)],
        compiler_params=_params("arbitrary"),
    )(x, x, w, *flat, *flat)


def _sigmoid(x):
    return 0.5 * jnp.tanh(0.5 * x) + 0.5


def _silu(x):
    return x * _sigmoid(x)


def _dsilu(x):
    s = _sigmoid(x)
    return s * (1.0 + x * (1.0 - s))


def f_rmsnorm(x, g):
    return (x * lax.rsqrt(jnp.mean(x * x, axis=-1, keepdims=True) + EPS) * g,)


SEL = lax.Precision.HIGH


def _group_norm(x, bd, width):
    ms = jnp.dot(x * x, bd, precision=SEL, preferred_element_type=F32) * (1.0 / width)
    return x * lax.rsqrt(ms + EPS)


def f_qknorm(qkv, gqa, gka, gqb, gkb, bd512, bd128, fold, expand):
    dq = N_HEADS * HEAD_DIM
    qa, ka, va, qb = (qkv[:, i * dq:(i + 1) * dq] for i in range(4))
    kb = qkv[:, 4 * dq:4 * dq + LANES]
    vb = qkv[:, 4 * dq + LANES:4 * dq + 2 * LANES]
    tile8 = lambda g: jnp.dot(g, fold, precision=HI, preferred_element_type=F32)
    qa = _group_norm(qa, bd512, HEAD_DIM) * tile8(gqa)
    ka = _group_norm(ka, bd512, HEAD_DIM) * tile8(gka)
    qb = _group_norm(qb, bd512, HEAD_DIM) * tile8(gqb)
    kb = _group_norm(kb, bd128, HEAD_DIM) * tile8(gkb)[:, :LANES]
    kb = jnp.dot(kb, expand, precision=SEL, preferred_element_type=F32)
    vb = jnp.dot(vb, expand, precision=SEL, preferred_element_type=F32)
    return qa, ka, va, qb, kb, vb


def f_gate_norm(y, z, nw):
    v = y * _silu(z)
    gw = D_INNER // SSM_GROUPS
    parts = []
    for g in range(SSM_GROUPS):
        vg = v[:, g * gw:(g + 1) * gw]
        parts.append(vg * lax.rsqrt(jnp.mean(vg * vg, axis=-1, keepdims=True) + EPS))
    return (jnp.concatenate(parts, axis=-1) * nw,)


ATT_TQ = 256
_NT = (((1,), (1,)), ((), ()))
_TN = (((0,), (0,)), ((), ()))


def _attn_probs(qh, kb, bias, valid, snk):
    s = lax.dot_general(qh, kb, _NT, preferred_element_type=F32) * (HEAD_DIM ** -0.5) + bias
    s = jnp.where(valid, s, -jnp.inf)
    m = jnp.max(s, axis=1, keepdims=True)
    if snk is not None:
        m = jnp.maximum(m, snk)
    e = jnp.exp(s - m)
    den = jnp.sum(e, axis=1, keepdims=True)
    if snk is None:
        return e / den, None
    es = jnp.exp(snk - m)
    den = den + es
    return e / den, es / den


def widen_bias(bias, n_prev, nj):
    band = (n_prev + 1) * CHUNK
    wk = (nj + n_prev) * CHUNK
    rows = [jnp.pad(bias, ((0, 0), (0, 0), (j * CHUNK, wk - band - j * CHUNK)), constant_values=-jnp.inf)
            for j in range(nj)]
    return jnp.concatenate(rows, axis=1)


def fold_bias(dbw, n_prev, nj):
    band = (n_prev + 1) * CHUNK
    acc = dbw[:, :CHUNK, :band]
    for j in range(1, nj):
        acc = acc + dbw[:, j * CHUNK:(j + 1) * CHUNK, j * CHUNK:j * CHUNK + band]
    return acc


def attn_fwd(q, k, v, bias_w, sinks, *, n_prev, name, rider=None):
    S = q.shape[0]
    pad = n_prev * CHUNK
    tq = min(ATT_TQ, S)
    wk = tq + pad
    assert bias_w.shape == (N_HEADS, tq, wk), bias_w.shape
    has_sink = sinks is not None

    r_in, r_out, r_shapes, r_sems, r_args = _rider_parts(rider)
    n_own = 5 if has_sink else 4
    n_p, n_i = N_HEADS // 2, S // tq

    def body(*refs):
        q_ref, k_ref, v_ref, bias_ref = refs[:4]
        sink_ref = refs[4] if has_sink else None
        o_ref = refs[n_own + len(r_in)]
        if rider is not None:
            p_id, i_id = pl.program_id(0), pl.program_id(1)
            _ride(rider, refs[n_own:n_own + len(r_in)], refs[n_own + len(r_in) + 1:n_own + len(r_in) + 1 + len(r_out)],
                  refs[n_own + len(r_in) + 1 + len(r_out):],
                  jnp.logical_and(p_id == 0, i_id == 0), jnp.logical_and(p_id == n_p // 2, i_id == 0),
                  jnp.logical_and(p_id == n_p - 1, i_id == n_i - 1))
        start = pl.multiple_of(pl.program_id(1) * tq, tq)
        head0 = lax.broadcasted_iota(jnp.int32, (1, LANES), 1) < HEAD_DIM
        valid = lax.broadcasted_iota(jnp.int32, (1, wk), 1) + start >= pad
        qp = q_ref[...].astype(F32)
        kb = k_ref[pl.ds(start, wk), :]
        vb = v_ref[pl.ds(start, wk), :]
        outs = []
        for r in range(2):
            mh = head0 if r == 0 else jnp.logical_not(head0)
            qh = jnp.where(mh, qp, 0.0).astype(BF16)
            snk = sink_ref[0, r:r + 1, 0:1] if has_sink else None
            p, _ = _attn_probs(qh, kb, bias_ref[r], valid, snk)
            outs.append(jnp.dot(p.astype(BF16), vb, preferred_element_type=F32))
        o_ref[...] = jnp.where(head0, outs[0], outs[1]).astype(o_ref.dtype)

    in_specs = [pl.BlockSpec((tq, LANES), lambda p, i: (i, p)),
                pl.BlockSpec((pad + S, LANES), lambda p, i: (0, p)),
                pl.BlockSpec((pad + S, LANES), lambda p, i: (0, p)),
                pl.BlockSpec((2, tq, wk), lambda p, i: (p, 0, 0))]
    args = [q, k, v, bias_w]
    if has_sink:
        in_specs.append(pl.BlockSpec((1, 2, LANES), lambda p, i: (p, 0, 0)))
        args.append(sinks)
    res = pl.pallas_call(
        body, name=name, grid=(n_p, n_i), in_specs=in_specs + r_in,
        out_specs=[pl.BlockSpec((tq, LANES), lambda p, i: (i, p))] + r_out,
        out_shape=[jax.ShapeDtypeStruct((S, N_HEADS * HEAD_DIM), BF16)] + r_shapes,
        scratch_shapes=r_sems,
        compiler_params=pltpu.CompilerParams(dimension_semantics=("arbitrary", "arbitrary"), vmem_limit_bytes=VMEM_LIMIT,
                                             has_side_effects=rider is not None),
    )(*args, *r_args)
    return res[0], res[1:]


def attn_bwd(q, k, v, do, bias_w, sinks, *, n_prev, name, rider=None):
    S = q.shape[0]
    pad = n_prev * CHUNK
    tq = min(ATT_TQ, S)
    wk = tq + pad
    assert bias_w.shape == (N_HEADS, tq, wk), bias_w.shape
    has_sink = sinks is not None
    scale = HEAD_DIM ** -0.5

    r_in, r_out, r_shapes, r_sems, r_args = _rider_parts(rider)
    n_own_in = 6 if has_sink else 5
    n_own_out = 5 if has_sink else 4
    n_p, n_i = N_HEADS // 2, S // tq

    def body(*refs):
        q_ref, k_ref, v_ref, do_ref, bias_ref = refs[:5]
        sink_ref = refs[5] if has_sink else None
        o0 = n_own_in + len(r_in)
        dq_ref, dk_ref, dv_ref, db_ref = refs[o0:o0 + 4]
        dsk_ref = refs[o0 + 4] if has_sink else None
        i = pl.program_id(1)
        if rider is not None:
            p_id = pl.program_id(0)
            _ride(rider, refs[n_own_in:o0], refs[o0 + n_own_out:o0 + n_own_out + len(r_out)],
                  refs[o0 + n_own_out + len(r_out):],
                  jnp.logical_and(p_id == 0, i == 0), jnp.logical_and(p_id == n_p // 2, i == 0),
                  jnp.logical_and(p_id == n_p - 1, i == n_i - 1))

        @pl.when(i == 0)
        def _():
            dk_ref[...] = jnp.zeros_like(dk_ref)
            dv_ref[...] = jnp.zeros_like(dv_ref)
            db_ref[...] = jnp.zeros_like(db_ref)
            if has_sink:
                dsk_ref[...] = jnp.zeros_like(dsk_ref)

        start = pl.multiple_of(i * tq, tq)
        head0 = lax.broadcasted_iota(jnp.int32, (1, LANES), 1) < HEAD_DIM
        valid = lax.broadcasted_iota(jnp.int32, (1, wk), 1) + start >= pad
        qp = q_ref[...].astype(F32)
        dop = do_ref[...].astype(F32)
        kb = k_ref[pl.ds(start, wk), :]
        vb = v_ref[pl.ds(start, wk), :]
        dqs = []
        for r in range(2):
            mh = head0 if r == 0 else jnp.logical_not(head0)
            qh = jnp.where(mh, qp, 0.0).astype(BF16)
            doh = jnp.where(mh, dop, 0.0).astype(BF16)
            snk = sink_ref[0, r:r + 1, 0:1] if has_sink else None
            p, ps = _attn_probs(qh, kb, bias_ref[r], valid, snk)
            dp = lax.dot_general(doh, vb, _NT, preferred_element_type=F32)
            delta = jnp.sum(p * dp, axis=1, keepdims=True)
            ds = p * (dp - delta)
            db_ref[r] += ds
            if has_sink:
                dsk = -jnp.sum(ps * delta, axis=0, keepdims=True)
                dsk_ref[0, r:r + 1, :] += jnp.broadcast_to(dsk, (1, LANES))
            dsb = ds.astype(BF16)
            dqs.append(jnp.dot(dsb, kb, preferred_element_type=F32) * scale)
            dk_ref[pl.ds(start, wk), :] += lax.dot_general(dsb, qh, _TN, preferred_element_type=F32) * scale
            dv_ref[pl.ds(start, wk), :] += lax.dot_general(p.astype(BF16), doh, _TN, preferred_element_type=F32)
        dq_ref[...] = jnp.where(head0, dqs[0], dqs[1])

    row_spec = pl.BlockSpec((tq, LANES), lambda p, i: (i, p))
    kv_spec = pl.BlockSpec((pad + S, LANES), lambda p, i: (0, p))
    bias_spec = pl.BlockSpec((2, tq, wk), lambda p, i: (p, 0, 0))
    sink_spec = pl.BlockSpec((1, 2, LANES), lambda p, i: (p, 0, 0))
    in_specs = [row_spec, kv_spec, kv_spec, row_spec, bias_spec]
    args = [q, k, v, do, bias_w]
    out_specs = [row_spec, kv_spec, kv_spec, bias_spec]
    W = N_HEADS * HEAD_DIM
    out_shape = [jax.ShapeDtypeStruct((S, W), F32), jax.ShapeDtypeStruct((pad + S, W), F32),
                 jax.ShapeDtypeStruct((pad + S, W), F32), jax.ShapeDtypeStruct((N_HEADS, tq, wk), F32)]
    if has_sink:
        in_specs.append(sink_spec)
        args.append(sinks)
        out_specs.append(sink_spec)
        out_shape.append(jax.ShapeDtypeStruct((N_HEADS // 2, 2, LANES), F32))
    res = pl.pallas_call(
        body, name=name, grid=(n_p, n_i), in_specs=in_specs + r_in, out_specs=out_specs + r_out,
        out_shape=out_shape + r_shapes, scratch_shapes=r_sems,
        compiler_params=pltpu.CompilerParams(dimension_semantics=("arbitrary", "arbitrary"), vmem_limit_bytes=VMEM_LIMIT,
                                             has_side_effects=rider is not None),
    )(*args, *r_args)
    return res[:n_own_out], res[n_own_out:]


HP = SSM_HEADS // 2
PAIRS_PER_GROUP = HP // SSM_GROUPS
HEADS_PER_GROUP = SSM_HEADS // SSM_GROUPS
GW = HEADS_PER_GROUP * 64


def _ssd_dt(dtraw, dtb, A, tril):
    lane = lax.broadcasted_iota(jnp.int32, (1, LANES), 1)
    u = dtraw + dtb
    eu = jnp.exp(-jnp.abs(u))
    w1 = 1.0 + eu
    l1p = jnp.where(w1 == 1.0, eu, jnp.log(w1) * eu / jnp.where(w1 == 1.0, 1.0, w1 - 1.0))
    dt = jnp.where(lane < SSM_HEADS, jnp.maximum(u, 0.0) + l1p, 0.0)
    acs = jnp.dot(tril, dt * A, precision=HI, preferred_element_type=F32)
    return u, dt, acs


def _head_expander():
    hw = D_INNER // SSM_HEADS
    return (np.arange(LANES)[:, None] == np.arange(D_INNER)[None, :] // hw).astype(np.float32)


def ssd_fwd(xbc, dtraw, dtb, A, dexp, *, name):
    S = xbc.shape[0]
    L = min(SSD_L, S)
    nc = S // L
    N = SSM_STATE
    e_mat = jnp.asarray(_head_expander())

    def body(xs_ref, b_ref, c_ref, dtr_ref, dtb_ref, a_ref, d_ref, e_ref, y_ref, st_out_ref, st_ref, xw_ref):
        c = pl.program_id(0)

        @pl.when(c == 0)
        def _():
            st_ref[...] = jnp.zeros_like(st_ref)

        st_out_ref[0] = st_ref[...]
        ri = lax.broadcasted_iota(jnp.int32, (L, L), 0)
        ci = lax.broadcasted_iota(jnp.int32, (L, L), 1)
        trilb = ri >= ci
        head0 = lax.broadcasted_iota(jnp.int32, (1, LANES), 1) < 64
        _, dt, acs = _ssd_dt(dtr_ref[...], dtb_ref[...], a_ref[...], trilb.astype(F32))
        acsT = acs.T
        last = acs[L - 1:L, :]
        expand = lambda t: jnp.dot(t, e_ref[...], precision=SEL, preferred_element_type=F32)
        dte, eae, wte = expand(dt), expand(jnp.exp(acs)), expand(jnp.exp(last - acs) * dt)
        lasts = [last[:, h:h + 1] for h in range(SSM_HEADS)]
        for g in range(SSM_GROUPS):
            Bg = b_ref[:, g * N:(g + 1) * N].astype(BF16)
            Cg = c_ref[:, g * N:(g + 1) * N].astype(BF16)
            CB = lax.dot_general(Cg, Bg, _NT, preferred_element_type=F32)
            Z = lax.dot_general(Cg, st_ref[g * GW:(g + 1) * GW, :].astype(BF16), _NT, preferred_element_type=F32)
            for q in range(PAIRS_PER_GROUP):
                hp = g * PAIRS_PER_GROUP + q
                sl = slice(hp * LANES, (hp + 1) * LANES)
                xs = xs_ref[:, sl]
                xd = xs * dte[:, sl]
                yi = jnp.zeros((L, LANES), F32)
                for r in range(2):
                    h = 2 * hp + r
                    dec = jnp.exp(jnp.where(trilb, acs[:, h:h + 1] - acsT[h:h + 1, :], -jnp.inf))
                    xdh = jnp.where(head0 if r == 0 else jnp.logical_not(head0), xd, 0.0).astype(BF16)
                    yi = yi + jnp.dot((CB * dec).astype(BF16), xdh, preferred_element_type=F32)
                y_ref[:, sl] = yi + Z[:, q * LANES:(q + 1) * LANES] * eae[:, sl] + d_ref[:, sl] * xs
                xw_ref[:, sl] = (xs * wte[:, sl]).astype(BF16)
        for g in range(SSM_GROUPS):
            Bg = b_ref[:, g * N:(g + 1) * N].astype(BF16)
            sn = lax.dot_general(xw_ref[:, g * GW:(g + 1) * GW], Bg, _TN, preferred_element_type=F32)
            for k in range(HEADS_PER_GROUP):
                h = g * HEADS_PER_GROUP + k
                rows = slice(h * 64, (h + 1) * 64)
                st_ref[rows, :] = st_ref[rows, :] * jnp.exp(lasts[h]) + sn[k * 64:(k + 1) * 64, :]

    return pl.pallas_call(
        body, name=name, grid=(nc,),
        in_specs=[pl.BlockSpec((L, D_INNER), lambda c: (c, 0)),
                  pl.BlockSpec((L, SSM_GROUPS * N), lambda c: (c, D_INNER // (SSM_GROUPS * N))),
                  pl.BlockSpec((L, SSM_GROUPS * N), lambda c: (c, D_INNER // (SSM_GROUPS * N) + 1)),
                  pl.BlockSpec((L, LANES), lambda c: (c, 0)),
                  pl.BlockSpec((1, LANES), lambda c: (0, 0)),
                  pl.BlockSpec((1, LANES), lambda c: (0, 0)),
                  pl.BlockSpec((1, D_INNER), lambda c: (0, 0)),
                  pl.BlockSpec((LANES, D_INNER), lambda c: (0, 0))],
        out_specs=[pl.BlockSpec((L, D_INNER), lambda c: (c, 0)),
                   pl.BlockSpec((1, D_INNER, N), lambda c: (c, 0, 0))],
        out_shape=[jax.ShapeDtypeStruct((S, D_INNER), F32), jax.ShapeDtypeStruct((nc, D_INNER, N), F32)],
        scratch_shapes=[pltpu.VMEM((D_INNER, N), F32), pltpu.VMEM((L, D_INNER), BF16)],
        compiler_params=_params("arbitrary"),
    )(xbc, xbc, xbc, dtraw, dtb, A, dexp, e_mat)


def ssd_bwd(xbc, dtraw, dtb, A, dexp, states, dy, *, name, rider=None):
    S = xbc.shape[0]
    L = min(SSD_L, S)
    nc = S // L
    N = SSM_STATE
    e_np = _head_expander()
    e_mat, et_mat = jnp.asarray(e_np), jnp.asarray(e_np.T)

    r_in, r_out, r_shapes, r_sems, r_args = _rider_parts(rider)

    def body(*refs):
        xs_ref, b_ref, c_ref, dtr_ref, dtb_ref, a_ref, d_ref, e_ref, et_ref, st_in_ref, dy_ref = refs[:11]
        o0 = 11 + len(r_in)
        dxs_ref, db_ref, dc_ref, ddtr_ref, da_ref, ddtb_ref, dd_ref = refs[o0:o0 + 7]
        s0 = o0 + 7 + len(r_out)
        dst_ref, xw_ref, dz_ref, r_ref, dsr_ref, dsc_ref = refs[s0:s0 + 6]
        step = pl.program_id(0)
        if rider is not None:
            _ride(rider, refs[11:o0], refs[o0 + 7:s0], refs[s0 + 6:], step == 0, step == nc // 2, step == nc - 1)

        @pl.when(step == 0)
        def _():
            dst_ref[...] = jnp.zeros_like(dst_ref)
            dsr_ref[...] = jnp.zeros_like(dsr_ref)
            dsc_ref[...] = jnp.zeros_like(dsc_ref)
            da_ref[...] = jnp.zeros_like(da_ref)
            ddtb_ref[...] = jnp.zeros_like(ddtb_ref)
            dd_ref[...] = jnp.zeros_like(dd_ref)

        ri = lax.broadcasted_iota(jnp.int32, (L, L), 0)
        ci = lax.broadcasted_iota(jnp.int32, (L, L), 1)
        trilb = ri >= ci
        lane = lax.broadcasted_iota(jnp.int32, (1, LANES), 1)
        sub = lax.broadcasted_iota(jnp.int32, (LANES, 1), 0)
        head0 = lane < 64
        A = a_ref[...]
        u, dt, acs = _ssd_dt(dtr_ref[...], dtb_ref[...], A, trilb.astype(F32))
        acsT = acs.T
        last = acs[L - 1:L, :]
        elast = jnp.exp(last)
        er = jnp.exp(last - acs)
        wt = er * dt
        expand = lambda t: jnp.dot(t, e_ref[...], precision=SEL, preferred_element_type=F32)
        dte, eae, wte = expand(dt), expand(jnp.exp(acs)), expand(wt)
        dlast = jnp.zeros((1, LANES), F32)
        dcbs = []
        for g in range(SSM_GROUPS):
            Bg = b_ref[:, g * N:(g + 1) * N].astype(BF16)
            Cg = c_ref[:, g * N:(g + 1) * N].astype(BF16)
            stg = st_in_ref[0, g * GW:(g + 1) * GW, :]
            dstg = dst_ref[g * GW:(g + 1) * GW, :]
            CB = lax.dot_general(Cg, Bg, _NT, preferred_element_type=F32)
            CBT = lax.dot_general(Bg, Cg, _NT, preferred_element_type=F32)
            Z = lax.dot_general(Cg, stg.astype(BF16), _NT, preferred_element_type=F32)
            U = lax.dot_general(Bg, dstg.astype(BF16), _NT, preferred_element_type=F32)
            dcb = jnp.zeros((L, L), F32)
            for q in range(PAIRS_PER_GROUP):
                hp = g * PAIRS_PER_GROUP + q
                sl = slice(hp * LANES, (hp + 1) * LANES)
                qs = slice(q * LANES, (q + 1) * LANES)
                xs = xs_ref[:, sl]
                dyp = dy_ref[:, sl]
                dtp, eap, wp, Dp = dte[:, sl], eae[:, sl], wte[:, sl], d_ref[:, sl]
                xd = xs * dtp
                dxd = jnp.zeros((L, LANES), F32)
                for r in range(2):
                    h = 2 * hp + r
                    mh = head0 if r == 0 else jnp.logical_not(head0)
                    dyh = jnp.where(mh, dyp, 0.0).astype(BF16)
                    xdh = jnp.where(mh, xd, 0.0).astype(BF16)
                    seg = acs[:, h:h + 1] - acsT[h:h + 1, :]
                    dec = jnp.exp(jnp.where(trilb, seg, -jnp.inf))
                    decT = jnp.exp(jnp.where(ri <= ci, -seg, -jnp.inf))
                    G = lax.dot_general(dyh, xdh, _NT, preferred_element_type=F32)
                    gd = G * dec
                    dcb = dcb + gd
                    dseg = gd * CB
                    dsr_ref[:, h:h + 1] = jnp.sum(dseg, axis=1, keepdims=True)
                    dsc_ref[h:h + 1, :] = jnp.sum(dseg, axis=0, keepdims=True)
                    dxd = dxd + jnp.dot((CBT * decT).astype(BF16), dyh, preferred_element_type=F32)
                Up = U[:, qs]
                r_ref[0, :, sl] = dyp * Z[:, qs] * eap
                r_ref[1, :, sl] = dxd * xs
                r_ref[2, :, sl] = Up * xs
                dz_ref[:, sl] = (dyp * eap).astype(BF16)
                xw_ref[:, sl] = (xs * wp).astype(BF16)
                dxs_ref[:, sl] = dxd * dtp + Dp * dyp + Up * wp
                dd_ref[:, sl] += jnp.sum(dyp * xs, axis=0, keepdims=True)
            dcbs.append(dcb)
            t = dstg * stg
            for k in range(HEADS_PER_GROUP):
                dlast = dlast + jnp.where(lane == g * HEADS_PER_GROUP + k,
                                          jnp.sum(t[k * 64:(k + 1) * 64, :], keepdims=True), 0.0)
        fold = lambda k: jnp.dot(r_ref[k], et_ref[...], precision=SEL, preferred_element_type=F32)
        r1, r2, dws = fold(0), fold(1), fold(2)
        dww = dws * wt
        ddt = r2 + dws * er
        dacs = r1 - dww + dsr_ref[...] - dsc_ref[...].T
        dlast = dlast * elast + jnp.sum(dww, axis=0, keepdims=True)
        lasts = [last[:, h:h + 1] for h in range(SSM_HEADS)]
        for g in range(SSM_GROUPS):
            Bg = b_ref[:, g * N:(g + 1) * N].astype(BF16)
            Cg = c_ref[:, g * N:(g + 1) * N].astype(BF16)
            gs = slice(g * GW, (g + 1) * GW)
            stb = st_in_ref[0, gs, :].astype(BF16)
            dstb = dst_ref[gs, :].astype(BF16)
            dcbb = dcbs[g].astype(BF16)
            dzg = dz_ref[:, gs]
            dc_ref[:, g * N:(g + 1) * N] = (jnp.dot(dzg, stb, preferred_element_type=F32)
                                            + jnp.dot(dcbb, Bg, preferred_element_type=F32))
            db_ref[:, g * N:(g + 1) * N] = (jnp.dot(xw_ref[:, gs], dstb, preferred_element_type=F32)
                                            + lax.dot_general(dcbb, Cg, _TN, preferred_element_type=F32))
            dsn = lax.dot_general(dzg, Cg, _TN, preferred_element_type=F32)
            for k in range(HEADS_PER_GROUP):
                h = g * HEADS_PER_GROUP + k
                rows = slice(h * 64, (h + 1) * 64)
                dst_ref[rows, :] = dst_ref[rows, :] * jnp.exp(lasts[h]) + dsn[k * 64:(k + 1) * 64, :]
        rowi = lax.broadcasted_iota(jnp.int32, (L, 1), 0)
        dacs = dacs + jnp.where(rowi == L - 1, dlast, 0.0)
        da = jnp.dot((ci >= ri).astype(F32), dacs, precision=HI, preferred_element_type=F32)
        ddt = ddt + da * A
        da_ref[...] += jnp.sum(da * dt, axis=0, keepdims=True)
        ddtr = jnp.where(lane < SSM_HEADS, ddt * _sigmoid(u), 0.0)
        ddtr_ref[...] = ddtr
        ddtb_ref[...] += jnp.sum(ddtr, axis=0, keepdims=True)

    rev = lambda c: nc - 1 - c
    gn = SSM_GROUPS * N
    res = pl.pallas_call(
        body, name=name, grid=(nc,),
        in_specs=[pl.BlockSpec((L, D_INNER), lambda c: (rev(c), 0)),
                  pl.BlockSpec((L, gn), lambda c: (rev(c), D_INNER // gn)),
                  pl.BlockSpec((L, gn), lambda c: (rev(c), D_INNER // gn + 1)),
                  pl.BlockSpec((L, LANES), lambda c: (rev(c), 0)),
                  pl.BlockSpec((1, LANES), lambda c: (0, 0)),
                  pl.BlockSpec((1, LANES), lambda c: (0, 0)),
                  pl.BlockSpec((1, D_INNER), lambda c: (0, 0)),
                  pl.BlockSpec((LANES, D_INNER), lambda c: (0, 0)),
                  pl.BlockSpec((D_INNER, LANES), lambda c: (0, 0)),
                  pl.BlockSpec((1, D_INNER, N), lambda c: (rev(c), 0, 0)),
                  pl.BlockSpec((L, D_INNER), lambda c: (rev(c), 0))] + r_in,
        out_specs=[pl.BlockSpec((L, D_INNER), lambda c: (rev(c), 0)),
                   pl.BlockSpec((L, gn), lambda c: (rev(c), 0)),
                   pl.BlockSpec((L, gn), lambda c: (rev(c), 0)),
                   pl.BlockSpec((L, LANES), lambda c: (rev(c), 0)),
                   pl.BlockSpec((1, LANES), lambda c: (0, 0)),
                   pl.BlockSpec((1, LANES), lambda c: (0, 0)),
                   pl.BlockSpec((1, D_INNER), lambda c: (0, 0))] + r_out,
        out_shape=[jax.ShapeDtypeStruct((S, D_INNER), F32), jax.ShapeDtypeStruct((S, gn), F32),
                   jax.ShapeDtypeStruct((S, gn), F32), jax.ShapeDtypeStruct((S, LANES), F32),
                   jax.ShapeDtypeStruct((1, LANES), F32), jax.ShapeDtypeStruct((1, LANES), F32),
                   jax.ShapeDtypeStruct((1, D_INNER), F32)] + r_shapes,
        scratch_shapes=[pltpu.VMEM((D_INNER, N), F32), pltpu.VMEM((L, D_INNER), BF16), pltpu.VMEM((L, D_INNER), BF16),
                        pltpu.VMEM((3, L, D_INNER), F32), pltpu.VMEM((L, LANES), F32), pltpu.VMEM((LANES, L), F32)]
                       + r_sems,
        compiler_params=pltpu.CompilerParams(dimension_semantics=("arbitrary",), vmem_limit_bytes=VMEM_LIMIT,
                                             has_side_effects=rider is not None),
    )(xbc, xbc, xbc, dtraw, dtb, A, dexp, e_mat, et_mat, states, dy, *r_args)
    return res[:7], res[7:]


BAND_A = (A_PREV + 1) * CHUNK
REL_W = 640


def _relpos_onehot(q):
    u = lax.broadcasted_iota(jnp.int32, (BAND_A, 1), 0)
    idx = jnp.clip(q - u + A_PREV * CHUNK, -MAX_REL, MAX_REL) + MAX_REL
    r = lax.broadcasted_iota(jnp.int32, (1, REL_W), 1)
    return (r == idx).astype(F32)


def relpos_bias(table_pad, *, name):
    def body(t_ref, o_ref):
        oh = _relpos_onehot(pl.program_id(0))
        o_ref[0] = lax.dot_general(t_ref[...], oh, _NT, precision=HI, preferred_element_type=F32)

    return pl.pallas_call(
        body, name=name, grid=(CHUNK,),
        in_specs=[pl.BlockSpec((N_HEADS, REL_W), lambda q: (0, 0))],
        out_specs=pl.BlockSpec((1, N_HEADS, BAND_A), lambda q: (q, 0, 0)),
        out_shape=jax.ShapeDtypeStruct((CHUNK, N_HEADS, BAND_A), F32),
        compiler_params=_params("parallel"),
    )(table_pad)


def relpos_grad(dbias_t, *, name):
    def body(d_ref, o_ref):
        q = pl.program_id(0)

        @pl.when(q == 0)
        def _():
            o_ref[...] = jnp.zeros_like(o_ref)

        o_ref[...] += jnp.dot(d_ref[0], _relpos_onehot(q), precision=HI, preferred_element_type=F32)

    return pl.pallas_call(
        body, name=name, grid=(CHUNK,),
        in_specs=[pl.BlockSpec((1, N_HEADS, BAND_A), lambda q: (q, 0, 0))],
        out_specs=pl.BlockSpec((N_HEADS, REL_W), lambda q: (0, 0)),
        out_shape=jax.ShapeDtypeStruct((N_HEADS, REL_W), F32),
        compiler_params=_params("arbitrary"),
    )(dbias_t)


def loss_head(y, t, *, name, tm=256):
    S, D = y.shape
    tm = min(tm, S)

    def body(y_ref, t_ref, dy_ref, l_ref):
        e = y_ref[...] - t_ref[...]
        dy_ref[...] = e * (1.0 / D)

        @pl.when(pl.program_id(0) == 0)
        def _():
            l_ref[...] = jnp.zeros_like(l_ref)

        part = jnp.sum(jnp.sum(e * e, axis=1, keepdims=True), axis=0, keepdims=True) * (0.5 / D)
        l_ref[...] += jnp.broadcast_to(part, l_ref.shape)

    return pl.pallas_call(
        body, name=name, grid=(S // tm,),
        in_specs=[pl.BlockSpec((tm, D), lambda i: (i, 0))] * 2,
        out_specs=[pl.BlockSpec((tm, D), lambda i: (i, 0)), pl.BlockSpec((1, LANES), lambda i: (0, 0))],
        out_shape=[jax.ShapeDtypeStruct((S, D), F32), jax.ShapeDtypeStruct((1, LANES), F32)],
        compiler_params=_params("arbitrary"),
    )(y, t)


def f_adamw(w, g, m, v):
    m = ADAM_B1 * m + (1.0 - ADAM_B1) * g
    v = ADAM_B2 * v + (1.0 - ADAM_B2) * (g * g)
    m_hat = m / (1.0 - ADAM_B1 ** ADAM_STEP)
    v_hat = v / (1.0 - ADAM_B2 ** ADAM_STEP)
    delta = -ADAM_LR * (m_hat / (jnp.sqrt(v_hat) + ADAM_EPS) + ADAM_WD * w)
    return delta, m, v


def f_norm_id(x, g):
    return f_rmsnorm(x, g)[0], x


ANY = pl.BlockSpec(memory_space=pl.ANY)


def _pos():
    return lax.axis_index("x"), lax.axis_index("y"), lax.axis_index("c")


def _other_chips(x, y):
    return [(1 - x, y), (x, 1 - y), (1 - x, 1 - y)]


class Rider:
    def __init__(self, ins, outs, sems, start, mid, finish):
        self.ins, self.outs, self.sems = list(ins), list(outs), list(sems)
        self.start, self.mid, self.finish = start, mid, finish


def _rider_parts(rider):
    if rider is None:
        return [], [], [], [], []
    return [ANY] * len(rider.ins), [ANY] * len(rider.outs), rider.outs, rider.sems, rider.ins


def _ride(rider, ins, outs, sems, first, mid, last):
    pos = _pos()

    @pl.when(first)
    def _():
        rider.start(pos, ins, outs, sems)

    if rider.mid is not None:
        @pl.when(mid)
        def _():
            rider.mid(pos, ins, outs, sems)

    @pl.when(last)
    def _():
        rider.finish(pos, ins, outs, sems)


def run_rider(rider, *, name):
    n_in, n_out = len(rider.ins), len(rider.outs)

    def body(*refs):
        ins, outs, sems = refs[:n_in], refs[n_in:n_in + n_out], refs[n_in + n_out:]
        pos = _pos()
        rider.start(pos, ins, outs, sems)
        if rider.mid is not None:
            rider.mid(pos, ins, outs, sems)
        rider.finish(pos, ins, outs, sems)

    return pl.pallas_call(
        body, name=name, in_specs=[ANY] * n_in, out_specs=[ANY] * n_out, out_shape=rider.outs,
        scratch_shapes=rider.sems, compiler_params=pltpu.CompilerParams(has_side_effects=True),
    )(*rider.ins)


def gather_rider(shards):
    n = len(shards)

    def copies(pos, ins, outs, sems):
        x, y, c = pos
        send, recv, fsend, frecv = sems
        me = 2 * x + y
        sib = (x, y, 1 - c)
        first, arrive, passed, theirs = [], [], [], []
        for i in range(n):
            for j, (px, py) in enumerate(_other_chips(x, y)):
                k = 3 * i + j
                far = dict(device_id=(px, py, c), device_id_type=MESH)
                near = dict(device_id=sib, device_id_type=MESH)
                got = outs[i].at[2 * px + py, c]
                his = outs[i].at[2 * px + py, 1 - c]
                first.append(pltpu.make_async_remote_copy(ins[i].at[c], outs[i].at[me, c], send.at[k], recv.at[k], **far))
                arrive.append(pltpu.make_async_remote_copy(ins[i].at[c], got, send.at[k], recv.at[k], **far))
                passed.append(pltpu.make_async_remote_copy(got, got, fsend.at[k], frecv.at[k], **near))
                theirs.append(pltpu.make_async_remote_copy(his, his, fsend.at[k], frecv.at[k], **near))
        return first, arrive, passed, theirs

    def start(*a):
        for cp in copies(*a)[0]:
            cp.start()

    def mid(*a):
        _, arrive, passed, _ = copies(*a)
        for got, cp in zip(arrive, passed):
            got.wait_recv()
            cp.start()

    def finish(*a):
        first, _, passed, theirs = copies(*a)
        for cp in theirs:
            cp.wait_recv()
        for cp in first + passed:
            cp.wait_send()

    return Rider(shards, [jax.ShapeDtypeStruct((4,) + s.shape, s.dtype) for s in shards],
                 [pltpu.SemaphoreType.DMA((3 * n,))] * 4, start, mid, finish)


def scatter_rider(ps):
    n = len(ps)

    def copies(pos, ins, outs, sems):
        x, y, c = pos
        send, recv = sems
        return [pltpu.make_async_remote_copy(ins[i].at[2 * px + py], outs[i].at[j], send.at[3 * i + j], recv.at[3 * i + j],
                                             device_id=(px, py, c), device_id_type=MESH)
                for i in range(n) for j, (px, py) in enumerate(_other_chips(x, y))]

    def start(*a):
        for cp in copies(*a):
            cp.start()

    def finish(*a):
        for cp in copies(*a):
            cp.wait()

    return Rider(ps, [jax.ShapeDtypeStruct((3,) + p.shape[1:], p.dtype) for p in ps],
                 [pltpu.SemaphoreType.DMA((3 * n,))] * 2, start, None, finish)


def pair_swap_halves(gs, *, name):
    n = len(gs)

    def body(*refs):
        ins, outs = refs[:n], refs[n:2 * n]
        send, recv = refs[2 * n:]
        x, y, c = _pos()
        cps = []
        for i in range(n):
            cp = pltpu.make_async_remote_copy(ins[i].at[1 - c], outs[i], send.at[i], recv.at[i],
                                              device_id=(x, y, 1 - c), device_id_type=MESH)
            cp.start()
            cps.append(cp)
        for cp in cps:
            cp.wait()

    return pl.pallas_call(
        body, name=name, in_specs=[ANY] * n, out_specs=[ANY] * n,
        out_shape=[jax.ShapeDtypeStruct(g.shape[1:], g.dtype) for g in gs],
        scratch_shapes=[pltpu.SemaphoreType.DMA((n,)), pltpu.SemaphoreType.DMA((n,))],
        compiler_params=pltpu.CompilerParams(has_side_effects=True),
    )(*gs)


def pair_share(hs, *, name):
    n = len(hs)

    def body(*refs):
        ins, outs = refs[:n], refs[n:2 * n]
        send, recv = refs[2 * n:]
        x, y, c = _pos()
        cps = []
        for i in range(n):
            cp = pltpu.make_async_remote_copy(ins[i], outs[i], send.at[i], recv.at[i],
                                              device_id=(x, y, 1 - c), device_id_type=MESH)
            cp.start()
            cps.append(cp)
        for cp in cps:
            cp.wait()

    return pl.pallas_call(
        body, name=name, in_specs=[ANY] * n, out_specs=[ANY] * n,
        out_shape=[jax.ShapeDtypeStruct(h.shape, h.dtype) for h in hs],
        scratch_shapes=[pltpu.SemaphoreType.DMA((n,)), pltpu.SemaphoreType.DMA((n,))],
        compiler_params=pltpu.CompilerParams(has_side_effects=True),
    )(*hs)


def gather_all(buf, *, name):
    def body(in_ref, out_ref, send, recv, loc):
        x, y, c = _pos()
        lid = 4 * x + 2 * y + c
        lc = pltpu.make_async_copy(in_ref, out_ref.at[lid], loc.at[0])
        lc.start()
        cps = []
        for k in range(1, 8):
            px = 1 - x if k & 4 else x
            py = 1 - y if k & 2 else y
            pc = 1 - c if k & 1 else c
            cp = pltpu.make_async_remote_copy(in_ref, out_ref.at[lid], send.at[k - 1], recv.at[k - 1],
                                              device_id=(px, py, pc), device_id_type=MESH)
            cp.start()
            cps.append((cp, 4 * px + 2 * py + pc, (px, py, pc)))
        for k, (cp, plid, peer) in enumerate(cps):
            cp.wait_send()
            pltpu.make_async_remote_copy(in_ref, out_ref.at[plid], send.at[k], recv.at[k],
                                         device_id=peer, device_id_type=MESH).wait_recv()
        lc.wait()

    return pl.pallas_call(
        body, name=name, in_specs=[ANY], out_specs=ANY,
        out_shape=jax.ShapeDtypeStruct((8,) + buf.shape, buf.dtype),
        scratch_shapes=[pltpu.SemaphoreType.DMA((7,)), pltpu.SemaphoreType.DMA((7,)), pltpu.SemaphoreType.DMA((1,))],
        compiler_params=pltpu.CompilerParams(has_side_effects=True),
    )(buf)


def sum_slots(a, *, name):
    n = a.shape[0]

    def body(a_ref, o_ref):
        acc = a_ref[0]
        for k in range(1, n):
            acc = acc + a_ref[k]
        o_ref[...] = acc

    return pl.pallas_call(body, name=name, out_shape=jax.ShapeDtypeStruct(a.shape[1:], a.dtype),
                          compiler_params=pltpu.CompilerParams(vmem_limit_bytes=VMEM_LIMIT))(a)


def _row_tile(r, want, mult=16):
    t = (min(want, r) // mult) * mult
    while t >= mult:
        if r % t == 0:
            return t
        t -= mult
    return r


def pair_add(g, r1, csel, *, name):
    _, _, r, C = g.shape
    tr = _row_tile(r, 256)

    def body(g_ref, r_ref, c_ref, p32_ref, pb_ref):
        south = c_ref[0:1, 0:1] == 0.0
        p = jnp.where(south, g_ref[0, 0], g_ref[1, 0]) + r_ref[0]
        p32_ref[0] = p
        pb_ref[0] = p.astype(BF16)

    return pl.pallas_call(
        body, name=name, grid=(4, r // tr),
        in_specs=[pl.BlockSpec((2, 1, tr, C), lambda j, t: (0, j, t, 0)), pl.BlockSpec((1, tr, C), lambda j, t: (j, t, 0)),
                  pl.BlockSpec((1, LANES), lambda j, t: (0, 0))],
        out_specs=[pl.BlockSpec((1, tr, C), lambda j, t: (j, t, 0))] * 2,
        out_shape=[jax.ShapeDtypeStruct((4, r, C), F32), jax.ShapeDtypeStruct((4, r, C), BF16)],
        compiler_params=_params("parallel", "parallel"),
    )(g, r1, csel)


def chip_add(p32, r3, msel, *, name):
    _, r, C = p32.shape
    tr = _row_tile(r, 128)

    def body(p_ref, r_ref, m_ref, o_ref):
        me = m_ref[0:1, 0:1]
        acc = jnp.where(me == 0.0, p_ref[0], jnp.where(me == 1.0, p_ref[1], jnp.where(me == 2.0, p_ref[2], p_ref[3])))
        for j in range(3):
            acc = acc + r_ref[j].astype(F32)
        o_ref[...] = acc

    return pl.pallas_call(
        body, name=name, grid=(r // tr,),
        in_specs=[pl.BlockSpec((4, tr, C), lambda t: (0, t, 0)), pl.BlockSpec((3, tr, C), lambda t: (0, t, 0)),
                  pl.BlockSpec((1, LANES), lambda t: (0, 0))],
        out_specs=pl.BlockSpec((tr, C), lambda t: (t, 0)),
        out_shape=jax.ShapeDtypeStruct((r, C), F32),
        compiler_params=_params("parallel"),
    )(p32, r3, msel)


def _consts():
    i512 = np.arange(N_HEADS * HEAD_DIM)
    i128 = np.arange(LANES)
    bd512 = (i512[:, None] // HEAD_DIM == i512[None, :] // HEAD_DIM).astype(np.float32)
    bd128 = (i128[:, None] // HEAD_DIM == i128[None, :] // HEAD_DIM).astype(np.float32)
    fold = (np.arange(HEAD_DIM)[:, None] == (i512[None, :] % HEAD_DIM)).astype(np.float32)
    grp = N_HEADS // 2 * HEAD_DIM
    expand = ((i128[:, None] // HEAD_DIM == i512[None, :] // grp)
              & (i128[:, None] % HEAD_DIM == i512[None, :] % HEAD_DIM)).astype(np.float32)
    band = (B_PREV + 1) * CHUNK
    rel = np.arange(CHUNK)[:, None] - (np.arange(band)[None, :] - B_PREV * CHUNK)
    slopes = 2.0 ** (-8.0 * np.arange(1, N_HEADS + 1, dtype=np.float32) / N_HEADS)
    bias_b = (-slopes[:, None, None] * np.abs(rel).astype(np.float32)[None]).astype(np.float32)
    return [jnp.asarray(a) for a in (bd512, bd128, fold, expand)], jnp.asarray(bias_b)


def _ffn_fwd(xin, l, W, P):
    g = P["norm_ffn"][l:l + 1]
    (h,) = rowwise(f_rmsnorm, [xin], [g], [(D_MODEL, BF16)], name=f"ffn{l}_norm")
    Wi = W["ffn_in"][l]
    gate = matmul(h, Wi[:, :D_FF], mode="nn", name=f"ffn{l}_gate")
    up = matmul(h, Wi[:, D_FF:], mode="nn", name=f"ffn{l}_up")
    gc, act = dwconv_fwd(gate, P["ffn_conv_w"][l], P["ffn_conv_b"][l:l + 1], lambda y, u: (y, _silu(y) * u), [up],
                         [F32, BF16], name=f"ffn{l}_conv")
    xout = matmul(act, W["ffn_out"][l], mode="nn", name=f"ffn{l}_out", residual=xin)
    return xout, (xin, h, gate, gc, up, act)


def _ffn_bwd(dxout, l, saved, W, P):
    xin, h, gate, gc, up, act = saved
    g = P["norm_ffn"][l:l + 1]
    Wi = W["ffn_in"][l]
    dact = matmul(dxout, W["ffn_out"][l], mode="nt", name=f"ffn{l}_dact")
    dWo = matmul(act, dxout, mode="tn", name=f"ffn{l}_dwout")
    dgate, dcw, dcb, dup = dwconv_bwd(gate, P["ffn_conv_w"][l], [gc, up, dact],
                                      lambda c, u, da: (da * u * _dsilu(c), da * _silu(c)), [BF16],
                                      name=f"ffn{l}_dconv")
    dh = matmul(dgate, Wi[:, :D_FF], mode="nt", name=f"ffn{l}_dh_gate")
    dh = matmul(dup, Wi[:, D_FF:], mode="nt", name=f"ffn{l}_dh_up", residual=dh)
    dWi = jnp.concatenate([matmul(h, dgate, mode="tn", name=f"ffn{l}_dw_gate"),
                           matmul(h, dup, mode="tn", name=f"ffn{l}_dw_up")], axis=1)
    dxin, dg = rowwise_vjp(f_norm_id, [xin], [g], [dh, dxout], [(0, F32)], [0], name=f"ffn{l}_dnorm")
    return dxin, dWi, dWo, dg, dcw, dcb


class NoComm:
    def fwd_rider(self, tag):
        return None

    def fwd_done(self, tag, outs, W, P):
        pass

    def grads(self, tag, cols, rows):
        return None

    def bwd_done(self, tag, outs):
        pass


def local_step(x, tgt, W, P, comm):
    qk_consts, bias_b = _consts()
    pad_rows = lambda t, n: jnp.pad(t, ((n * CHUNK, 0), (0, 0)))
    DQ = N_HEADS * HEAD_DIM

    g_mix0 = P["norm_mix"][0:1]
    (h0,) = rowwise(f_rmsnorm, [x], [g_mix0], [(D_MODEL, BF16)], name="attn_norm")
    qkv = matmul(h0, W["attn_in"], mode="nn", name="attn_qkv")
    qk_par = [P["q_norm_a"], P["k_norm_a"], P["q_norm_b"], P["k_norm_b"]] + qk_consts
    qa, ka, va, qb, kb, vb = rowwise(f_qknorm, [qkv], qk_par, [(DQ, BF16)] * 6, name="attn_qknorm")
    ka, va, kb, vb = pad_rows(ka, A_PREV), pad_rows(va, A_PREV), pad_rows(kb, B_PREV), pad_rows(vb, B_PREV)
    table = jnp.pad(P["relpos_table"], ((0, 0), (0, REL_W - (2 * MAX_REL + 1))))
    nj = min(ATT_TQ, x.shape[0]) // CHUNK
    bias_a = widen_bias(jnp.transpose(relpos_bias(table, name="relpos_bias"), (1, 0, 2)), A_PREV, nj)
    bias_b = widen_bias(bias_b, B_PREV, nj)
    sinks = jnp.broadcast_to(P["sinks"].reshape(N_HEADS // 2, 2, 1), (N_HEADS // 2, 2, LANES))
    oa, late = attn_fwd(qa, ka, va, bias_a, None, n_prev=A_PREV, name="attn_a", rider=comm.fwd_rider("a"))
    comm.fwd_done("a", late, W, P)
    ob, late = attn_fwd(qb, kb, vb, bias_b, sinks, n_prev=B_PREV, name="attn_b", rider=comm.fwd_rider("b"))
    comm.fwd_done("b", late, W, P)
    Wao = W["attn_out"]
    x1 = matmul(oa, Wao[:DQ], mode="nn", name="attn_out_a", residual=x)
    x1 = matmul(ob, Wao[DQ:], mode="nn", name="attn_out_b", residual=x1)
    x2, ffn0 = _ffn_fwd(x1, 0, W, P)

    g_mix1 = P["norm_mix"][1:2]
    (h2,) = rowwise(f_rmsnorm, [x2], [g_mix1], [(D_MODEL, BF16)], name="ssm_norm_in")
    Ws = W["ssm_in"]
    CC = D_INNER + 2 * SSM_GROUPS * SSM_STATE
    Wz, Wx = Ws[:, :D_INNER], Ws[:, D_INNER:D_INNER + CC]
    Wdt = jnp.pad(Ws[:, D_INNER + CC:], ((0, 0), (0, LANES - SSM_HEADS)))
    z = matmul(h2, Wz, mode="nn", name="ssm_z")
    xr = matmul(h2, Wx, mode="nn", name="ssm_xbc")
    dtraw = matmul(h2, Wdt, mode="nn", name="ssm_dt")
    xc, xbc = dwconv_fwd(xr, P["ssm_conv_w"], P["ssm_conv_b"], lambda y: (y, _silu(y)), [], [F32, F32],
                         name="ssm_conv")
    pad32 = lambda v: jnp.pad(v, ((0, 0), (0, LANES - SSM_HEADS)))
    A = pad32(-jnp.exp(P["ssm_a_log"]))
    dtb = pad32(P["ssm_dt_bias"])
    dexp = jnp.repeat(P["ssm_d"], D_INNER // SSM_HEADS, axis=1)
    y, states = ssd_fwd(xbc, dtraw, dtb, A, dexp, name="ssd_fwd")
    (y2,) = rowwise(f_gate_norm, [y, z], [P["ssm_norm"]], [(D_INNER, BF16)], name="ssm_gate_norm")
    x3 = matmul(y2, W["ssm_out"], mode="nn", name="ssm_out", residual=x2)
    x4, ffn1 = _ffn_fwd(x3, 1, W, P)

    dx4, lpart = loss_head(x4, tgt, name="loss_head")

    dx3, dWfi1, dWfo1, dgf1, dfcw1, dfcb1 = _ffn_bwd(dx4, 1, ffn1, W, P)
    out_f1 = comm.grads("f1", dWfi1, dWfo1)
    dy2 = matmul(dx3, W["ssm_out"], mode="nt", name="ssm_dy")
    dWso = matmul(y2, dx3, mode="tn", name="ssm_dwout")
    dy, dz, dnw = rowwise_vjp(f_gate_norm, [y, z], [P["ssm_norm"]], [dy2], [(0, F32), (1, BF16)], [0],
                              name="ssm_dgate_norm")
    (dxs, dB, dC, ddtraw, dA, ddtb, dDl), sent = ssd_bwd(xbc, dtraw, dtb, A, dexp, states, dy, name="ssd_bwd",
                                                          rider=out_f1)
    comm.bwd_done("f1", sent)
    dxr, dscw, dscb = dwconv_bwd(xr, P["ssm_conv_w"], [xc, (dxs, dB, dC)], lambda c, g: (g * _dsilu(c),), [],
                                 name="ssm_dconv")
    dh2 = matmul(dz, Wz, mode="nt", name="ssm_dh_z")
    dh2 = matmul(dxr, Wx, mode="nt", name="ssm_dh_x", residual=dh2)
    dh2 = matmul(ddtraw, Wdt, mode="nt", name="ssm_dh_dt", residual=dh2)
    dWs = jnp.concatenate([matmul(h2, dz, mode="tn", name="ssm_dw_z"),
                           matmul(h2, dxr, mode="tn", name="ssm_dw_x"),
                           matmul(h2, ddtraw, mode="tn", name="ssm_dw_dt")[:, :SSM_HEADS]], axis=1)
    dx2, dgm1 = rowwise_vjp(f_norm_id, [x2], [g_mix1], [dh2, dx3], [(0, F32)], [0], name="ssm_dnorm_in")
    out_s = comm.grads("s", dWs, dWso)

    dx1, dWfi0, dWfo0, dgf0, dfcw0, dfcb0 = _ffn_bwd(dx2, 0, ffn0, W, P)
    out_f0 = comm.grads("f0", dWfi0, dWfo0)
    doa = matmul(dx1, Wao[:DQ], mode="nt", name="attn_do_a", out_dtype=BF16)
    dob = matmul(dx1, Wao[DQ:], mode="nt", name="attn_do_b", out_dtype=BF16)
    dWao = jnp.concatenate([matmul(oa, dx1, mode="tn", name="attn_dwout_a"),
                            matmul(ob, dx1, mode="tn", name="attn_dwout_b")], axis=0)
    (dqa, dka, dva, dbias_a), sent = attn_bwd(qa, ka, va, doa, bias_a, None, n_prev=A_PREV, name="attn_a_bwd",
                                              rider=out_s)
    comm.bwd_done("s", sent)
    (dqb, dkb, dvb, _, dsk), sent = attn_bwd(qb, kb, vb, dob, bias_b, sinks, n_prev=B_PREV, name="attn_b_bwd",
                                             rider=out_f0)
    comm.bwd_done("f0", sent)
    pa, pb = A_PREV * CHUNK, B_PREV * CHUNK
    dqkv, dgqa, dgka, dgqb, dgkb = rowwise_vjp(f_qknorm, [qkv], qk_par, [dqa, dka, dva, dqb, dkb, dvb], [(0, BF16)],
                                               [0, 1, 2, 3], name="attn_dqknorm", tm=pb,
                                               cot_skip=[0, pa, pa, 0, pb, pb])
    dh0 = matmul(dqkv, W["attn_in"], mode="nt", name="attn_dh")
    dWai = matmul(h0, dqkv, mode="tn", name="attn_dwin")
    dx, dgm0 = rowwise_vjp(f_norm_id, [x], [g_mix0], [dh0, dx1], [(0, F32)], [0], name="attn_dnorm")
    dbias_a = fold_bias(dbias_a, A_PREV, nj)
    dtable = relpos_grad(jnp.transpose(dbias_a, (1, 0, 2)), name="relpos_grad")[:, :2 * MAX_REL + 1]

    gW = {"attn_in": dWai, "attn_out": dWao, "ssm_in": dWs, "ssm_out": dWso,
          "ffn_in": [dWfi0, dWfi1], "ffn_out": [dWfo0, dWfo1]}
    gP = {"norm_mix": jnp.concatenate([dgm0, dgm1], axis=0),
          "norm_ffn": jnp.concatenate([dgf0, dgf1], axis=0),
          "relpos_table": dtable, "q_norm_a": dgqa, "k_norm_a": dgka, "q_norm_b": dgqb, "k_norm_b": dgkb,
          "sinks": dsk[:, :, 0].reshape(1, N_HEADS),
          "ssm_conv_w": dscw, "ssm_conv_b": dscb,
          "ssm_dt_bias": ddtb[:, :SSM_HEADS], "ssm_a_log": dA[:, :SSM_HEADS] * A[:, :SSM_HEADS],
          "ssm_d": dDl.reshape(SSM_HEADS, D_INNER // SSM_HEADS).sum(axis=1).reshape(1, SSM_HEADS),
          "ssm_norm": dnw,
          "ffn_conv_w": jnp.stack([dfcw0, dfcw1]), "ffn_conv_b": jnp.concatenate([dfcb0, dfcb1], axis=0)}
    return lpart, dx, gW, gP


WEIGHTS = ["norm_mix", "norm_ffn", "attn_w_in", "attn_w_out", "relpos_table", "q_norm_a", "k_norm_a", "q_norm_b",
           "k_norm_b", "sinks", "ssm_w_in", "ssm_conv_w", "ssm_conv_b", "ssm_dt_bias", "ssm_a_log", "ssm_d",
           "ssm_norm", "ssm_w_out", "ffn_w_in", "ffn_conv_w", "ffn_conv_b", "ffn_w_out"]
ARGS = ["x"] + WEIGHTS + ["loss_target"] + ["m_" + w for w in WEIGHTS] + ["v_" + w for w in WEIGHTS]
N_CHIPS = 4
SMALL_ROWS = 384
SMALL_ORDER = ["norm_mix", "norm_ffn", "relpos_table", "q_norm_a", "k_norm_a", "q_norm_b", "k_norm_b", "sinks",
               "ssm_dt_bias", "ssm_a_log", "ssm_d", "ffn_conv_b", "ssm_conv_w", "ssm_conv_b", "ssm_norm", "ffn_conv_w"]


def _cols_to_slabs(g):
    K, N = g.shape
    return g.reshape(2, K // 2, N_CHIPS, N // N_CHIPS).transpose(0, 2, 1, 3)


def _rows_to_slabs(g):
    R, C = g.shape
    return g.reshape(N_CHIPS, 2, R // (2 * N_CHIPS), C).transpose(1, 0, 2, 3)


class MeshComm:
    def __init__(self, d, xi, yi, ci):
        self.d, self.ci, self.me = d, ci, 2 * xi + yi
        self.csel = jnp.full((1, LANES), ci, F32)
        self.msel = jnp.full((1, LANES), self.me, F32)
        halves = lambda w: w.reshape((2, -1, w.shape[-1]))
        small = jnp.concatenate([d[k].reshape(-1) for k in ("ssm_conv_w", "ssm_conv_b", "ssm_norm", "ffn_conv_w")])
        small = jnp.pad(small, (0, 2 * 40 * LANES - small.shape[0])).reshape(2, 40, LANES)
        self.shards = {"attn": [halves(d["attn_w_in"][0].astype(BF16)), halves(d["attn_w_out"][0].astype(BF16))],
                       "a": [d["ffn_w_in"].astype(BF16), small],
                       "b": [d["ffn_w_out"].astype(BF16), halves(d["ssm_w_in"][0].astype(BF16)),
                             halves(d["ssm_w_out"][0].astype(BF16))]}
        self.p32, self.mine = {}, {}

    def _whole(self, tag, outs):
        return [lax.dynamic_update_slice_in_dim(g, s[None], self.me, axis=0) for g, s in zip(outs, self.shards[tag])]

    @staticmethod
    def _cat_cols(g):
        return jnp.concatenate([g[j].reshape((-1, g.shape[-1])) for j in range(N_CHIPS)], axis=1)

    def first_weights(self):
        g_ai, g_ao = self._whole("attn", run_rider(gather_rider(self.shards["attn"]), name="gather_attn"))
        return {"attn_in": self._cat_cols(g_ai), "attn_out": g_ao.reshape(-1, D_MODEL)}

    def fwd_rider(self, tag):
        return gather_rider(self.shards[tag])

    def fwd_done(self, tag, outs, W, P):
        if tag == "b":
            g_fo, g_si, g_so = self._whole("b", outs)
            W["ffn_out"] = [g_fo[:, l].reshape(-1, D_MODEL) for l in range(2)]
            W["ssm_in"], W["ssm_out"] = self._cat_cols(g_si), g_so.reshape(-1, D_MODEL)
            return
        g_fi, g_sm = self._whole("a", outs)
        W["ffn_in"] = [jnp.concatenate([g_fi[j, l] for j in range(N_CHIPS)], axis=1) for l in range(2)]
        sm = g_sm.reshape(N_CHIPS, -1)
        CC = D_INNER + 2 * SSM_GROUPS * SSM_STATE
        c4, f4 = CC // N_CHIPS, D_FF // N_CHIPS
        o1 = SSM_CONV * c4
        o2 = o1 + c4
        o3 = o2 + D_INNER // N_CHIPS
        o4 = o3 + 2 * FFN_CONV * f4
        P["ssm_conv_w"] = sm[:, :o1].reshape(N_CHIPS, SSM_CONV, c4).transpose(1, 0, 2).reshape(SSM_CONV, CC)
        P["ssm_conv_b"] = sm[:, o1:o2].reshape(1, CC)
        P["ssm_norm"] = sm[:, o2:o3].reshape(1, D_INNER)
        P["ffn_conv_w"] = sm[:, o3:o4].reshape(N_CHIPS, 2, FFN_CONV, f4).transpose(1, 2, 0, 3).reshape(2, FFN_CONV, D_FF)

    def grads(self, tag, cols, rows):
        slabs = [_cols_to_slabs(cols), _rows_to_slabs(rows)]
        from_sib = pair_swap_halves(slabs, name="grad_pair_swap_" + tag)
        pairs = [pair_add(g, r, self.csel, name=f"grad_pair_add_{tag}{i}") for i, (g, r) in enumerate(zip(slabs, from_sib))]
        self.p32[tag] = [p[0] for p in pairs]
        return scatter_rider([p[1] for p in pairs])

    def bwd_done(self, tag, outs):
        self.mine[tag] = [chip_add(p, r, self.msel, name=f"grad_chip_add_{tag}{i}")
                          for i, (p, r) in enumerate(zip(self.p32[tag], outs))]

    def finish(self, d_attn_in, d_attn_out):
        self.bwd_done("at", run_rider(self.grads("at", d_attn_in, d_attn_out), name="grad_scatter_at"))
        order = ["at", "s", "f0", "f1"]
        mine = [m for t in order for m in self.mine[t]]
        theirs = pair_share(mine, name="grad_pair_share")
        full = [jnp.where(self.ci == 0, jnp.stack([a, b]), jnp.stack([b, a])).reshape((-1, a.shape[-1]))
                for a, b in zip(mine, theirs)]
        ai, ao, si, so, fi0, fo0, fi1, fo1 = full
        return {"attn_w_in": ai[None], "attn_w_out": ao[None], "ssm_w_in": si[None], "ssm_w_out": so[None],
                "ffn_w_in": jnp.stack([fi0, fi1]), "ffn_w_out": jnp.stack([fo0, fo1])}


def _adamw(w, g, m, v, name):
    shp = w.shape
    two = lambda a: a.reshape((-1, shp[-1]))
    outs = [(shp[-1], F32)] * 3
    d, nm, nv = rowwise(f_adamw, [two(w), two(g), two(m), two(v)], [], outs, name="adamw_" + name)
    return d.reshape(shp), nm.reshape(shp), nv.reshape(shp)


def kernel(x, norm_mix, norm_ffn, attn_w_in, attn_w_out, relpos_table, q_norm_a, k_norm_a, q_norm_b, k_norm_b, sinks, ssm_w_in, ssm_conv_w, ssm_conv_b, ssm_dt_bias, ssm_a_log, ssm_d, ssm_norm, ssm_w_out, ffn_w_in, ffn_conv_w, ffn_conv_b, ffn_w_out, loss_target, m_norm_mix, m_norm_ffn, m_attn_w_in, m_attn_w_out, m_relpos_table, m_q_norm_a, m_k_norm_a, m_q_norm_b, m_k_norm_b, m_sinks, m_ssm_w_in, m_ssm_conv_w, m_ssm_conv_b, m_ssm_dt_bias, m_ssm_a_log, m_ssm_d, m_ssm_norm, m_ssm_w_out, m_ffn_w_in, m_ffn_conv_w, m_ffn_conv_b, m_ffn_w_out, v_norm_mix, v_norm_ffn, v_attn_w_in, v_attn_w_out, v_relpos_table, v_q_norm_a, v_k_norm_a, v_q_norm_b, v_k_norm_b, v_sinks, v_ssm_w_in, v_ssm_conv_w, v_ssm_conv_b, v_ssm_dt_bias, v_ssm_a_log, v_ssm_d, v_ssm_norm, v_ssm_w_out, v_ffn_w_in, v_ffn_conv_w, v_ffn_conv_b, v_ffn_w_out):
    d = dict(zip(ARGS, (x, norm_mix, norm_ffn, attn_w_in, attn_w_out, relpos_table, q_norm_a, k_norm_a, q_norm_b, k_norm_b, sinks, ssm_w_in, ssm_conv_w, ssm_conv_b, ssm_dt_bias, ssm_a_log, ssm_d, ssm_norm, ssm_w_out, ffn_w_in, ffn_conv_w, ffn_conv_b, ffn_w_out, loss_target, m_norm_mix, m_norm_ffn, m_attn_w_in, m_attn_w_out, m_relpos_table, m_q_norm_a, m_k_norm_a, m_q_norm_b, m_k_norm_b, m_sinks, m_ssm_w_in, m_ssm_conv_w, m_ssm_conv_b, m_ssm_dt_bias, m_ssm_a_log, m_ssm_d, m_ssm_norm, m_ssm_w_out, m_ffn_w_in, m_ffn_conv_w, m_ffn_conv_b, m_ffn_w_out, v_norm_mix, v_norm_ffn, v_attn_w_in, v_attn_w_out, v_relpos_table, v_q_norm_a, v_k_norm_a, v_q_norm_b, v_k_norm_b, v_sinks, v_ssm_w_in, v_ssm_conv_w, v_ssm_conv_b, v_ssm_dt_bias, v_ssm_a_log, v_ssm_d, v_ssm_norm, v_ssm_w_out, v_ffn_w_in, v_ffn_conv_w, v_ffn_conv_b, v_ffn_w_out)))
    xi, yi, ci = _pos()
    me = 2 * xi + yi
    CC = D_INNER + 2 * SSM_GROUPS * SSM_STATE
    c4, f4 = CC // N_CHIPS, D_FF // N_CHIPS

    P = {k: d[k] for k in ["norm_mix", "norm_ffn", "q_norm_a", "k_norm_a", "q_norm_b", "k_norm_b", "sinks",
                           "ssm_dt_bias", "ssm_a_log", "ssm_d", "ffn_conv_b"]}
    P["relpos_table"] = d["relpos_table"][0]
    comm = MeshComm(d, xi, yi, ci)
    W = comm.first_weights()
    lpart, dx, gW, gP = local_step(d["x"][0], d["loss_target"][0], W, P, comm)
    loss = lax.psum(lpart[0, 0], ("x", "y", "c"))
    grads = comm.finish(gW["attn_in"], gW["attn_out"])

    flat = jnp.concatenate([gP[k].reshape(-1) for k in SMALL_ORDER])
    flat = jnp.pad(flat, (0, SMALL_ROWS * LANES - flat.shape[0])).reshape(SMALL_ROWS, LANES)
    tot = sum_slots(gather_all(flat, name="small_gather"), name="small_sum").reshape(-1)
    off = 0
    for k in SMALL_ORDER:
        n = int(np.prod(gP[k].shape))
        g = tot[off:off + n].reshape(gP[k].shape)
        off += n
        if k == "ssm_conv_w":
            g = lax.dynamic_slice_in_dim(g, me * c4, c4, axis=1)[None]
        elif k == "ssm_conv_b":
            g = lax.dynamic_slice_in_dim(g, me * c4, c4, axis=1)
        elif k == "ssm_norm":
            g = lax.dynamic_slice_in_dim(g, me * (D_INNER // N_CHIPS), D_INNER // N_CHIPS, axis=1)
        elif k == "ffn_conv_w":
            g = lax.dynamic_slice_in_dim(g, me * f4, f4, axis=2)
        elif k == "relpos_table":
            g = g[None]
        grads[k] = g

    deltas, new_m, new_v = {}, {}, {}
    for k in WEIGHTS:
        deltas[k], new_m[k], new_v[k] = _adamw(d[k], grads[k], d["m_" + k], d["v_" + k], k)
    return (loss, dx[None], *[grads[k] for k in WEIGHTS], *[deltas[k] for k in WEIGHTS],
            *[new_m[k] for k in WEIGHTS], *[new_v[k] for k in WEIGHTS])
```

```python
import functools

import numpy as np
import jax
import jax.numpy as jnp
from jax import lax
from jax.experimental import pallas as pl
from jax.experimental.pallas import tpu as pltpu

F32 = jnp.float32
BF16 = jnp.bfloat16
HI = lax.Precision.HIGHEST

D_MODEL = 1024
CHUNK = 64
EPS = 1e-6
HEAD_DIM = 64
N_HEADS = 8
A_PREV = 8
B_PREV = 2
MAX_REL = 256
D_INNER = 2048
SSM_HEADS = 32
SSM_GROUPS = 4
SSM_STATE = 128
SSM_CONV = 4
D_FF = 2816
FFN_CONV = 3
LANES = 128
SUBLANES = 8
VMEM_LIMIT = 56 * 1024 * 1024
SSD_L = 128

ADAM_LR = 0.001
ADAM_B1 = 0.9
ADAM_B2 = 0.999
ADAM_EPS = 1e-08
ADAM_WD = 0.01
ADAM_STEP = 10

MESH = pl.DeviceIdType.MESH


def _params(*sem):
    return pltpu.CompilerParams(dimension_semantics=sem, vmem_limit_bytes=VMEM_LIMIT)


def _pick(n, want):
    if n <= want:
        return n
    t = (want // LANES) * LANES
    while t >= LANES:
        if n % t == 0:
            return t
        t -= LANES
    return n


MM_ROWS = 512
MM_COLS = 1536
MM_RED = 2048


def matmul(a, b, *, mode, name, out_dtype=F32, residual=None):
    dims = {"nn": (((1,), (0,)), ((), ())), "nt": (((1,), (1,)), ((), ())), "tn": (((0,), (0,)), ((), ()))}[mode]
    if mode == "tn":
        assert residual is None and out_dtype == F32
        (K, M), (K2, N) = a.shape, b.shape
        assert K == K2, (a.shape, b.shape)
        tm, tn, tk = _pick(M, MM_COLS), _pick(N, MM_COLS), _pick(K, MM_RED)

        def body(a_ref, b_ref, o_ref):
            k = pl.program_id(2)
            p = lax.dot_general(a_ref[...].astype(BF16), b_ref[...].astype(BF16), dims, preferred_element_type=F32)

            @pl.when(k == 0)
            def _():
                o_ref[...] = p

            @pl.when(k != 0)
            def _():
                o_ref[...] += p

        return pl.pallas_call(
            body, name=name, grid=(M // tm, N // tn, K // tk),
            in_specs=[pl.BlockSpec((tk, tm), lambda i, j, k: (k, i)), pl.BlockSpec((tk, tn), lambda i, j, k: (k, j))],
            out_specs=pl.BlockSpec((tm, tn), lambda i, j, k: (i, j)),
            out_shape=jax.ShapeDtypeStruct((M, N), F32),
            compiler_params=_params("parallel", "parallel", "arbitrary"),
        )(a, b)

    if mode == "nn":
        (M, K), (K2, N) = a.shape, b.shape
    else:
        (M, K), (N, K2) = a.shape, b.shape
    assert K == K2, (a.shape, b.shape, mode)
    tm, tn = _pick(M, MM_ROWS), _pick(N, MM_COLS)

    def body(*refs):
        a_ref, b_ref = refs[:2]
        o_ref = refs[-1]
        r = lax.dot_general(a_ref[...].astype(BF16), b_ref[...].astype(BF16), dims, preferred_element_type=F32)
        if residual is not None:
            r = r + refs[2][...].astype(F32)
        o_ref[...] = r.astype(o_ref.dtype)

    a_spec = pl.BlockSpec((tm, K), lambda j, i: (i, 0))
    b_spec = pl.BlockSpec((K, tn), lambda j, i: (0, j)) if mode == "nn" else pl.BlockSpec((tn, K), lambda j, i: (j, 0))
    o_spec = pl.BlockSpec((tm, tn), lambda j, i: (i, j))
    in_specs = [a_spec, b_spec] + ([o_spec] if residual is not None else [])
    args = (a, b) + ((residual,) if residual is not None else ())
    return pl.pallas_call(
        body, name=name, grid=(N // tn, M // tm),
        in_specs=in_specs, out_specs=o_spec,
        out_shape=jax.ShapeDtypeStruct((M, N), out_dtype),
        compiler_params=_params("parallel", "parallel"),
    )(*args)


def rowwise(f, rows, params, outs, *, name, tm=256):
    S = rows[0].shape[0]
    tm = _row_tile(S, tm)
    nr, npar = len(rows), len(params)

    def body(*refs):
        vals = [r[...].astype(F32) for r in refs[:nr + npar]]
        res = f(*vals)
        for o_ref, r in zip(refs[nr + npar:], res):
            o_ref[...] = r.astype(o_ref.dtype)

    in_specs = [pl.BlockSpec((tm, r.shape[1]), lambda i: (i, 0)) for r in rows]
    in_specs += [pl.BlockSpec(p.shape, lambda i: (0, 0)) for p in params]
    out_specs = [pl.BlockSpec((tm, c), lambda i: (i, 0)) for c, _ in outs]
    out_shape = [jax.ShapeDtypeStruct((S, c), dt) for c, dt in outs]
    return pl.pallas_call(body, name=name, grid=(S // tm,), in_specs=in_specs, out_specs=out_specs,
                          out_shape=out_shape, compiler_params=_params("parallel"))(*rows, *params)


def rowwise_vjp(f, rows, params, cots, drow, dpar, *, name, tm=256, cot_skip=None):
    S = rows[0].shape[0]
    tm = _row_tile(S, tm)
    nr, npar, nc = len(rows), len(params), len(cots)
    skip = [0] * nc if cot_skip is None else [s // tm for s in cot_skip]
    assert cot_skip is None or all(s % tm == 0 for s in cot_skip)

    def body(*refs):
        vals = [r[...].astype(F32) for r in refs[:nr + npar]]
        cvals = [r[...].astype(F32) for r in refs[nr + npar:nr + npar + nc]]
        o_refs = refs[nr + npar + nc:]
        want = [ri for ri, _ in drow] + [nr + pi for pi in dpar]

        def f_want(*d):
            full = list(vals)
            for k, v in zip(want, d):
                full[k] = v
            return f(*full)

        _, vjp = jax.vjp(f_want, *[vals[k] for k in want])
        grads = vjp(tuple(cvals))
        for o_ref, g in zip(o_refs[:len(drow)], grads):
            o_ref[...] = g.astype(o_ref.dtype)
        first = pl.program_id(0) == 0
        for o_ref, g in zip(o_refs[len(drow):], grads[len(drow):]):
            g = g.astype(F32)

            @pl.when(first)
            def _(o_ref=o_ref, g=g):
                o_ref[...] = g

            @pl.when(jnp.logical_not(first))
            def _(o_ref=o_ref, g=g):
                o_ref[...] += g

    in_specs = [pl.BlockSpec((tm, r.shape[1]), lambda i: (i, 0)) for r in rows]
    in_specs += [pl.BlockSpec(p.shape, lambda i: (0, 0)) for p in params]
    in_specs += [pl.BlockSpec((tm, c.shape[1]), lambda i, s=s: (i + s, 0)) for c, s in zip(cots, skip)]
    out_specs = [pl.BlockSpec((tm, rows[ri].shape[1]), lambda i: (i, 0)) for ri, _ in drow]
    out_specs += [pl.BlockSpec(params[pi].shape, lambda i: (0, 0)) for pi in dpar]
    out_shape = [jax.ShapeDtypeStruct(rows[ri].shape, dt) for ri, dt in drow]
    out_shape += [jax.ShapeDtypeStruct(params[pi].shape, F32) for pi in dpar]
    return pl.pallas_call(body, name=name, grid=(S // tm,), in_specs=in_specs, out_specs=out_specs,
                          out_shape=out_shape, compiler_params=_params("arbitrary"))(*rows, *params, *cots)


HALO = 2 * SUBLANES


def dwconv_fwd(x, w, b, post, extra, outs, *, name, tm=256):
    S, C = x.shape
    K = w.shape[0]
    tm = min(tm, S)
    hb = tm // HALO
    ne = len(extra)

    def body(*refs):
        x_ref, halo_ref, w_ref, b_ref = refs[:4]
        e_refs = refs[4:4 + ne]
        o_refs = refs[4 + ne:4 + ne + len(outs)]
        buf = refs[-1]
        i = pl.program_id(0)
        buf[0:HALO, :] = jnp.where(i == 0, 0.0, halo_ref[...].astype(F32))
        buf[HALO:HALO + tm, :] = x_ref[...].astype(F32)
        for c0 in range(0, C, LANES):
            cs = slice(c0, c0 + LANES)
            acc = jnp.broadcast_to(b_ref[:, cs], (tm, LANES))
            for k in range(K):
                acc = acc + w_ref[k:k + 1, cs] * buf[pl.ds(HALO - (K - 1) + k, tm), cs]
            for o_ref, r in zip(o_refs, post(acc, *[e[:, cs].astype(F32) for e in e_refs])):
                o_ref[:, cs] = r.astype(o_ref.dtype)

    row = pl.BlockSpec((tm, C), lambda i: (i, 0))
    return pl.pallas_call(
        body, name=name, grid=(S // tm,),
        in_specs=[row,
                  pl.BlockSpec((HALO, C), lambda i: (jnp.maximum(i * hb - 1, 0), 0)),
                  pl.BlockSpec((K, C), lambda i: (0, 0)),
                  pl.BlockSpec((1, C), lambda i: (0, 0))] + [row] * ne,
        out_specs=[row] * len(outs),
        out_shape=[jax.ShapeDtypeStruct((S, C), dt) for dt in outs],
        scratch_shapes=[pltpu.VMEM((HALO + tm, C), F32)],
        compiler_params=_params("parallel"),
    )(x, x, w, b, *extra)


def dwconv_bwd(x, w, srcs, dy_fn, extra_outs, *, name, tm=256):
    S, C = x.shape
    K = w.shape[0]
    tm = min(tm, S)
    hb = tm // HALO
    n = S // tm
    groups = [s if isinstance(s, tuple) else (s,) for s in srcs]
    flat = [a for g in groups for a in g]
    nf = len(flat)

    def body(*refs):
        x_ref, xh_ref, w_ref = refs[:3]
        dx_ref, dw_ref, db_ref = refs[3 + 2 * nf:6 + 2 * nf]
        e_refs = refs[6 + 2 * nf:6 + 2 * nf + len(extra_outs)]
        bx, bd = refs[-2:]

        def strips(first, c0):
            out, at = [], first
            for g in groups:
                off = 0
                for a in g:
                    if off <= c0 < off + a.shape[1]:
                        out.append(refs[at][:, c0 - off:c0 - off + LANES].astype(F32))
                    off += a.shape[1]
                    at += 1
            return out

        i = pl.program_id(0)
        bx[0:HALO, :] = jnp.where(i == 0, 0.0, xh_ref[...].astype(F32))
        bx[HALO:HALO + tm, :] = x_ref[...].astype(F32)

        @pl.when(i == 0)
        def _():
            dw_ref[...] = jnp.zeros_like(dw_ref)
            db_ref[...] = jnp.zeros_like(db_ref)

        for c0 in range(0, C, LANES):
            cs = slice(c0, c0 + LANES)
            res = dy_fn(*strips(3, c0))
            dyv = res[0]
            for e_ref, r in zip(e_refs, res[1:]):
                e_ref[:, cs] = r.astype(e_ref.dtype)
            bd[0:tm, cs] = dyv
            bd[tm:tm + HALO, cs] = jnp.where(i == n - 1, 0.0, dy_fn(*strips(3 + nf, c0))[0])
            acc = jnp.zeros((tm, LANES), F32)
            for k in range(K):
                acc = acc + w_ref[k:k + 1, cs] * bd[pl.ds((K - 1) - k, tm), cs]
            dx_ref[:, cs] = acc.astype(dx_ref.dtype)
            for k in range(K):
                dw_ref[k:k + 1, cs] += jnp.sum(dyv * bx[pl.ds(HALO - (K - 1) + k, tm), cs], axis=0, keepdims=True)
            db_ref[:, cs] += jnp.sum(dyv, axis=0, keepdims=True)

    row = lambda c: pl.BlockSpec((tm, c), lambda i: (i, 0))
    nxt = lambda c: pl.BlockSpec((HALO, c), lambda i: (jnp.minimum((i + 1) * hb, S // HALO - 1), 0))
    return pl.pallas_call(
        body, name=name, grid=(n,),
        in_specs=[row(C), pl.BlockSpec((HALO, C), lambda i: (jnp.maximum(i * hb - 1, 0), 0)),
                  pl.BlockSpec((K, C), lambda i: (0, 0))]
                 + [row(a.shape[1]) for a in flat] + [nxt(a.shape[1]) for a in flat],
        out_specs=[row(C), pl.BlockSpec((K, C), lambda i: (0, 0)), pl.BlockSpec((1, C), lambda i: (0, 0))]
                  + [row(C)] * len(extra_outs),
        out_shape=[jax.ShapeDtypeStruct((S, C), BF16), jax.ShapeDtypeStruct((K, C), F32),
                   jax.ShapeDtypeStruct((1, C), F32)] + [jax.ShapeDtypeStruct((S, C), dt) for dt in extra_outs],
        scratch_shapes=[pltpu.VMEM((HALO + tm, C), F32), pltpu.VMEM((tm + HALO, C), F32)],
        compiler_params=_params("arbitrary"),
    )(x, x, w, *flat, *flat)


def _sigmoid(x):
    return 0.5 * jnp.tanh(0.5 * x) + 0.5


def _silu(x):
    return x * _sigmoid(x)


def _dsilu(x):
    s = _sigmoid(x)
    return s * (1.0 + x * (1.0 - s))


def f_rmsnorm(x, g):
    return (x * lax.rsqrt(jnp.mean(x * x, axis=-1, keepdims=True) + EPS) * g,)


SEL = lax.Precision.HIGH


def _group_norm(x, bd, width):
    ms = jnp.dot(x * x, bd, precision=SEL, preferred_element_type=F32) * (1.0 / width)
    return x * lax.rsqrt(ms + EPS)


def f_qknorm(qkv, gqa, gka, gqb, gkb, bd512, bd128, fold, expand):
    dq = N_HEADS * HEAD_DIM
    qa, ka, va, qb = (qkv[:, i * dq:(i + 1) * dq] for i in range(4))
    kb = qkv[:, 4 * dq:4 * dq + LANES]
    vb = qkv[:, 4 * dq + LANES:4 * dq + 2 * LANES]
    tile8 = lambda g: jnp.dot(g, fold, precision=HI, preferred_element_type=F32)
    qa = _group_norm(qa, bd512, HEAD_DIM) * tile8(gqa)
    ka = _group_norm(ka, bd512, HEAD_DIM) * tile8(gka)
    qb = _group_norm(qb, bd512, HEAD_DIM) * tile8(gqb)
    kb = _group_norm(kb, bd128, HEAD_DIM) * tile8(gkb)[:, :LANES]
    kb = jnp.dot(kb, expand, precision=SEL, preferred_element_type=F32)
    vb = jnp.dot(vb, expand, precision=SEL, preferred_element_type=F32)
    return qa, ka, va, qb, kb, vb


def f_gate_norm(y, z, nw):
    v = y * _silu(z)
    gw = D_INNER // SSM_GROUPS
    parts = []
    for g in range(SSM_GROUPS):
        vg = v[:, g * gw:(g + 1) * gw]
        parts.append(vg * lax.rsqrt(jnp.mean(vg * vg, axis=-1, keepdims=True) + EPS))
    return (jnp.concatenate(parts, axis=-1) * nw,)


ATT_TQ = 256
_NT = (((1,), (1,)), ((), ()))
_TN = (((0,), (0,)), ((), ()))


def _attn_probs(qh, kb, bias, valid, snk):
    s = lax.dot_general(qh, kb, _NT, preferred_element_type=F32) * (HEAD_DIM ** -0.5) + bias
    s = jnp.where(valid, s, -jnp.inf)
    m = jnp.max(s, axis=1, keepdims=True)
    if snk is not None:
        m = jnp.maximum(m, snk)
    e = jnp.exp(s - m)
    den = jnp.sum(e, axis=1, keepdims=True)
    if snk is None:
        return e / den, None
    es = jnp.exp(snk - m)
    den = den + es
    return e / den, es / den


def widen_bias(bias, n_prev, nj):
    band = (n_prev + 1) * CHUNK
    wk = (nj + n_prev) * CHUNK
    rows = [jnp.pad(bias, ((0, 0), (0, 0), (j * CHUNK, wk - band - j * CHUNK)), constant_values=-jnp.inf)
            for j in range(nj)]
    return jnp.concatenate(rows, axis=1)


def fold_bias(dbw, n_prev, nj):
    band = (n_prev + 1) * CHUNK
    acc = dbw[:, :CHUNK, :band]
    for j in range(1, nj):
        acc = acc + dbw[:, j * CHUNK:(j + 1) * CHUNK, j * CHUNK:j * CHUNK + band]
    return acc


def attn_fwd(q, k, v, bias_w, sinks, *, n_prev, name, rider=None):
    S = q.shape[0]
    pad = n_prev * CHUNK
    tq = min(ATT_TQ, S)
    wk = tq + pad
    assert bias_w.shape == (N_HEADS, tq, wk), bias_w.shape
    has_sink = sinks is not None

    r_in, r_out, r_shapes, r_sems, r_args = _rider_parts(rider)
    n_own = 5 if has_sink else 4
    n_p, n_i = N_HEADS // 2, S // tq

    def body(*refs):
        q_ref, k_ref, v_ref, bias_ref = refs[:4]
        sink_ref = refs[4] if has_sink else None
        o_ref = refs[n_own + len(r_in)]
        if rider is not None:
            p_id, i_id = pl.program_id(0), pl.program_id(1)
            _ride(rider, refs[n_own:n_own + len(r_in)], refs[n_own + len(r_in) + 1:n_own + len(r_in) + 1 + len(r_out)],
                  refs[n_own + len(r_in) + 1 + len(r_out):],
                  jnp.logical_and(p_id == 0, i_id == 0), jnp.logical_and(p_id == n_p // 2, i_id == 0),
                  jnp.logical_and(p_id == n_p - 1, i_id == n_i - 1))
        start = pl.multiple_of(pl.program_id(1) * tq, tq)
        head0 = lax.broadcasted_iota(jnp.int32, (1, LANES), 1) < HEAD_DIM
        valid = lax.broadcasted_iota(jnp.int32, (1, wk), 1) + start >= pad
        qp = q_ref[...].astype(F32)
        kb = k_ref[pl.ds(start, wk), :]
        vb = v_ref[pl.ds(start, wk), :]
        outs = []
        for r in range(2):
            mh = head0 if r == 0 else jnp.logical_not(head0)
            qh = jnp.where(mh, qp, 0.0).astype(BF16)
            snk = sink_ref[0, r:r + 1, 0:1] if has_sink else None
            p, _ = _attn_probs(qh, kb, bias_ref[r], valid, snk)
            outs.append(jnp.dot(p.astype(BF16), vb, preferred_element_type=F32))
        o_ref[...] = jnp.where(head0, outs[0], outs[1]).astype(o_ref.dtype)

    in_specs = [pl.BlockSpec((tq, LANES), lambda p, i: (i, p)),
                pl.BlockSpec((pad + S, LANES), lambda p, i: (0, p)),
                pl.BlockSpec((pad + S, LANES), lambda p, i: (0, p)),
                pl.BlockSpec((2, tq, wk), lambda p, i: (p, 0, 0))]
    args = [q, k, v, bias_w]
    if has_sink:
        in_specs.append(pl.BlockSpec((1, 2, LANES), lambda p, i: (p, 0, 0)))
        args.append(sinks)
    res = pl.pallas_call(
        body, name=name, grid=(n_p, n_i), in_specs=in_specs + r_in,
        out_specs=[pl.BlockSpec((tq, LANES), lambda p, i: (i, p))] + r_out,
        out_shape=[jax.ShapeDtypeStruct((S, N_HEADS * HEAD_DIM), BF16)] + r_shapes,
        scratch_shapes=r_sems,
        compiler_params=pltpu.CompilerParams(dimension_semantics=("arbitrary", "arbitrary"), vmem_limit_bytes=VMEM_LIMIT,
                                             has_side_effects=rider is not None),
    )(*args, *r_args)
    return res[0], res[1:]


def attn_bwd(q, k, v, do, bias_w, sinks, *, n_prev, name, rider=None):
    S = q.shape[0]
    pad = n_prev * CHUNK
    tq = min(ATT_TQ, S)
    wk = tq + pad
    assert bias_w.shape == (N_HEADS, tq, wk), bias_w.shape
    has_sink = sinks is not None
    scale = HEAD_DIM ** -0.5

    r_in, r_out, r_shapes, r_sems, r_args = _rider_parts(rider)
    n_own_in = 6 if has_sink else 5
    n_own_out = 5 if has_sink else 4
    n_p, n_i = N_HEADS // 2, S // tq

    def body(*refs):
        q_ref, k_ref, v_ref, do_ref, bias_ref = refs[:5]
        sink_ref = refs[5] if has_sink else None
        o0 = n_own_in + len(r_in)
        dq_ref, dk_ref, dv_ref, db_ref = refs[o0:o0 + 4]
        dsk_ref = refs[o0 + 4] if has_sink else None
        i = pl.program_id(1)
        if rider is not None:
            p_id = pl.program_id(0)
            _ride(rider, refs[n_own_in:o0], refs[o0 + n_own_out:o0 + n_own_out + len(r_out)],
                  refs[o0 + n_own_out + len(r_out):],
                  jnp.logical_and(p_id == 0, i == 0), jnp.logical_and(p_id == n_p // 2, i == 0),
                  jnp.logical_and(p_id == n_p - 1, i == n_i - 1))

        @pl.when(i == 0)
        def _():
            dk_ref[...] = jnp.zeros_like(dk_ref)
            dv_ref[...] = jnp.zeros_like(dv_ref)
            db_ref[...] = jnp.zeros_like(db_ref)
            if has_sink:
                dsk_ref[...] = jnp.zeros_like(dsk_ref)

        start = pl.multiple_of(i * tq, tq)
        head0 = lax.broadcasted_iota(jnp.int32, (1, LANES), 1) < HEAD_DIM
        valid = lax.broadcasted_iota(jnp.int32, (1, wk), 1) + start >= pad
        qp = q_ref[...].astype(F32)
        dop = do_ref[...].astype(F32)
        kb = k_ref[pl.ds(start, wk), :]
        vb = v_ref[pl.ds(start, wk), :]
        dqs = []
        for r in range(2):
            mh = head0 if r == 0 else jnp.logical_not(head0)
            qh = jnp.where(mh, qp, 0.0).astype(BF16)
            doh = jnp.where(mh, dop, 0.0).astype(BF16)
            snk = sink_ref[0, r:r + 1, 0:1] if has_sink else None
            p, ps = _attn_probs(qh, kb, bias_ref[r], valid, snk)
            dp = lax.dot_general(doh, vb, _NT, preferred_element_type=F32)
            delta = jnp.sum(p * dp, axis=1, keepdims=True)
            ds = p * (dp - delta)
            db_ref[r] += ds
            if has_sink:
                dsk = -jnp.sum(ps * delta, axis=0, keepdims=True)
                dsk_ref[0, r:r + 1, :] += jnp.broadcast_to(dsk, (1, LANES))
            dsb = ds.astype(BF16)
            dqs.append(jnp.dot(dsb, kb, preferred_element_type=F32) * scale)
            dk_ref[pl.ds(start, wk), :] += lax.dot_general(dsb, qh, _TN, preferred_element_type=F32) * scale
            dv_ref[pl.ds(start, wk), :] += lax.dot_general(p.astype(BF16), doh, _TN, preferred_element_type=F32)
        dq_ref[...] = jnp.where(head0, dqs[0], dqs[1])

    row_spec = pl.BlockSpec((tq, LANES), lambda p, i: (i, p))
    kv_spec = pl.BlockSpec((pad + S, LANES), lambda p, i: (0, p))
    bias_spec = pl.BlockSpec((2, tq, wk), lambda p, i: (p, 0, 0))
    sink_spec = pl.BlockSpec((1, 2, LANES), lambda p, i: (p, 0, 0))
    in_specs = [row_spec, kv_spec, kv_spec, row_spec, bias_spec]
    args = [q, k, v, do, bias_w]
    out_specs = [row_spec, kv_spec, kv_spec, bias_spec]
    W = N_HEADS * HEAD_DIM
    out_shape = [jax.ShapeDtypeStruct((S, W), F32), jax.ShapeDtypeStruct((pad + S, W), F32),
                 jax.ShapeDtypeStruct((pad + S, W), F32), jax.ShapeDtypeStruct((N_HEADS, tq, wk), F32)]
    if has_sink:
        in_specs.append(sink_spec)
        args.append(sinks)
        out_specs.append(sink_spec)
        out_shape.append(jax.ShapeDtypeStruct((N_HEADS // 2, 2, LANES), F32))
    res = pl.pallas_call(
        body, name=name, grid=(n_p, n_i), in_specs=in_specs + r_in, out_specs=out_specs + r_out,
        out_shape=out_shape + r_shapes, scratch_shapes=r_sems,
        compiler_params=pltpu.CompilerParams(dimension_semantics=("arbitrary", "arbitrary"), vmem_limit_bytes=VMEM_LIMIT,
                                             has_side_effects=rider is not None),
    )(*args, *r_args)
    return res[:n_own_out], res[n_own_out:]


HP = SSM_HEADS // 2
PAIRS_PER_GROUP = HP // SSM_GROUPS
HEADS_PER_GROUP = SSM_HEADS // SSM_GROUPS
GW = HEADS_PER_GROUP * 64


def _ssd_dt(dtraw, dtb, A, tril):
    lane = lax.broadcasted_iota(jnp.int32, (1, LANES), 1)
    u = dtraw + dtb
    eu = jnp.exp(-jnp.abs(u))
    w1 = 1.0 + eu
    l1p = jnp.where(w1 == 1.0, eu, jnp.log(w1) * eu / jnp.where(w1 == 1.0, 1.0, w1 - 1.0))
    dt = jnp.where(lane < SSM_HEADS, jnp.maximum(u, 0.0) + l1p, 0.0)
    acs = jnp.dot(tril, dt * A, precision=HI, preferred_element_type=F32)
    return u, dt, acs


def _head_expander():
    hw = D_INNER // SSM_HEADS
    return (np.arange(LANES)[:, None] == np.arange(D_INNER)[None, :] // hw).astype(np.float32)


def ssd_fwd(xbc, dtraw, dtb, A, dexp, *, name):
    S = xbc.shape[0]
    L = min(SSD_L, S)
    nc = S // L
    N = SSM_STATE
    e_mat = jnp.asarray(_head_expander())

    def body(xs_ref, b_ref, c_ref, dtr_ref, dtb_ref, a_ref, d_ref, e_ref, y_ref, st_out_ref, st_ref, xw_ref):
        c = pl.program_id(0)

        @pl.when(c == 0)
        def _():
            st_ref[...] = jnp.zeros_like(st_ref)

        st_out_ref[0] = st_ref[...]
        ri = lax.broadcasted_iota(jnp.int32, (L, L), 0)
        ci = lax.broadcasted_iota(jnp.int32, (L, L), 1)
        trilb = ri >= ci
        head0 = lax.broadcasted_iota(jnp.int32, (1, LANES), 1) < 64
        _, dt, acs = _ssd_dt(dtr_ref[...], dtb_ref[...], a_ref[...], trilb.astype(F32))
        acsT = acs.T
        last = acs[L - 1:L, :]
        expand = lambda t: jnp.dot(t, e_ref[...], precision=SEL, preferred_element_type=F32)
        dte, eae, wte = expand(dt), expand(jnp.exp(acs)), expand(jnp.exp(last - acs) * dt)
        lasts = [last[:, h:h + 1] for h in range(SSM_HEADS)]
        for g in range(SSM_GROUPS):
            Bg = b_ref[:, g * N:(g + 1) * N].astype(BF16)
            Cg = c_ref[:, g * N:(g + 1) * N].astype(BF16)
            CB = lax.dot_general(Cg, Bg, _NT, preferred_element_type=F32)
            Z = lax.dot_general(Cg, st_ref[g * GW:(g + 1) * GW, :].astype(BF16), _NT, preferred_element_type=F32)
            for q in range(PAIRS_PER_GROUP):
                hp = g * PAIRS_PER_GROUP + q
                sl = slice(hp * LANES, (hp + 1) * LANES)
                xs = xs_ref[:, sl]
                xd = xs * dte[:, sl]
                yi = jnp.zeros((L, LANES), F32)
                for r in range(2):
                    h = 2 * hp + r
                    dec = jnp.exp(jnp.where(trilb, acs[:, h:h + 1] - acsT[h:h + 1, :], -jnp.inf))
                    xdh = jnp.where(head0 if r == 0 else jnp.logical_not(head0), xd, 0.0).astype(BF16)
                    yi = yi + jnp.dot((CB * dec).astype(BF16), xdh, preferred_element_type=F32)
                y_ref[:, sl] = yi + Z[:, q * LANES:(q + 1) * LANES] * eae[:, sl] + d_ref[:, sl] * xs
                xw_ref[:, sl] = (xs * wte[:, sl]).astype(BF16)
        for g in range(SSM_GROUPS):
            Bg = b_ref[:, g * N:(g + 1) * N].astype(BF16)
            sn = lax.dot_general(xw_ref[:, g * GW:(g + 1) * GW], Bg, _TN, preferred_element_type=F32)
            for k in range(HEADS_PER_GROUP):
                h = g * HEADS_PER_GROUP + k
                rows = slice(h * 64, (h + 1) * 64)
                st_ref[rows, :] = st_ref[rows, :] * jnp.exp(lasts[h]) + sn[k * 64:(k + 1) * 64, :]

    return pl.pallas_call(
        body, name=name, grid=(nc,),
        in_specs=[pl.BlockSpec((L, D_INNER), lambda c: (c, 0)),
                  pl.BlockSpec((L, SSM_GROUPS * N), lambda c: (c, D_INNER // (SSM_GROUPS * N))),
                  pl.BlockSpec((L, SSM_GROUPS * N), lambda c: (c, D_INNER // (SSM_GROUPS * N) + 1)),
                  pl.BlockSpec((L, LANES), lambda c: (c, 0)),
                  pl.BlockSpec((1, LANES), lambda c: (0, 0)),
                  pl.BlockSpec((1, LANES), lambda c: (0, 0)),
                  pl.BlockSpec((1, D_INNER), lambda c: (0, 0)),
                  pl.BlockSpec((LANES, D_INNER), lambda c: (0, 0))],
        out_specs=[pl.BlockSpec((L, D_INNER), lambda c: (c, 0)),
                   pl.BlockSpec((1, D_INNER, N), lambda c: (c, 0, 0))],
        out_shape=[jax.ShapeDtypeStruct((S, D_INNER), F32), jax.ShapeDtypeStruct((nc, D_INNER, N), F32)],
        scratch_shapes=[pltpu.VMEM((D_INNER, N), F32), pltpu.VMEM((L, D_INNER), BF16)],
        compiler_params=_params("arbitrary"),
    )(xbc, xbc, xbc, dtraw, dtb, A, dexp, e_mat)


def ssd_bwd(xbc, dtraw, dtb, A, dexp, states, dy, *, name, rider=None):
    S = xbc.shape[0]
    L = min(SSD_L, S)
    nc = S // L
    N = SSM_STATE
    e_np = _head_expander()
    e_mat, et_mat = jnp.asarray(e_np), jnp.asarray(e_np.T)

    r_in, r_out, r_shapes, r_sems, r_args = _rider_parts(rider)

    def body(*refs):
        xs_ref, b_ref, c_ref, dtr_ref, dtb_ref, a_ref, d_ref, e_ref, et_ref, st_in_ref, dy_ref = refs[:11]
        o0 = 11 + len(r_in)
        dxs_ref, db_ref, dc_ref, ddtr_ref, da_ref, ddtb_ref, dd_ref = refs[o0:o0 + 7]
        s0 = o0 + 7 + len(r_out)
        dst_ref, xw_ref, dz_ref, r_ref, dsr_ref, dsc_ref = refs[s0:s0 + 6]
        step = pl.program_id(0)
        if rider is not None:
            _ride(rider, refs[11:o0], refs[o0 + 7:s0], refs[s0 + 6:], step == 0, step == nc // 2, step == nc - 1)

        @pl.when(step == 0)
        def _():
            dst_ref[...] = jnp.zeros_like(dst_ref)
            dsr_ref[...] = jnp.zeros_like(dsr_ref)
            dsc_ref[...] = jnp.zeros_like(dsc_ref)
            da_ref[...] = jnp.zeros_like(da_ref)
            ddtb_ref[...] = jnp.zeros_like(ddtb_ref)
            dd_ref[...] = jnp.zeros_like(dd_ref)

        ri = lax.broadcasted_iota(jnp.int32, (L, L), 0)
        ci = lax.broadcasted_iota(jnp.int32, (L, L), 1)
        trilb = ri >= ci
        lane = lax.broadcasted_iota(jnp.int32, (1, LANES), 1)
        sub = lax.broadcasted_iota(jnp.int32, (LANES, 1), 0)
        head0 = lane < 64
        A = a_ref[...]
        u, dt, acs = _ssd_dt(dtr_ref[...], dtb_ref[...], A, trilb.astype(F32))
        acsT = acs.T
        last = acs[L - 1:L, :]
        elast = jnp.exp(last)
        er = jnp.exp(last - acs)
        wt = er * dt
        expand = lambda t: jnp.dot(t, e_ref[...], precision=SEL, preferred_element_type=F32)
        dte, eae, wte = expand(dt), expand(jnp.exp(acs)), expand(wt)
        dlast = jnp.zeros((1, LANES), F32)
        dcbs = []
        for g in range(SSM_GROUPS):
            Bg = b_ref[:, g * N:(g + 1) * N].astype(BF16)
            Cg = c_ref[:, g * N:(g + 1) * N].astype(BF16)
            stg = st_in_ref[0, g * GW:(g + 1) * GW, :]
            dstg = dst_ref[g * GW:(g + 1) * GW, :]
            CB = lax.dot_general(Cg, Bg, _NT, preferred_element_type=F32)
            CBT = lax.dot_general(Bg, Cg, _NT, preferred_element_type=F32)
            Z = lax.dot_general(Cg, stg.astype(BF16), _NT, preferred_element_type=F32)
            U = lax.dot_general(Bg, dstg.astype(BF16), _NT, preferred_element_type=F32)
            dcb = jnp.zeros((L, L), F32)
            for q in range(PAIRS_PER_GROUP):
                hp = g * PAIRS_PER_GROUP + q
                sl = slice(hp * LANES, (hp + 1) * LANES)
                qs = slice(q * LANES, (q + 1) * LANES)
                xs = xs_ref[:, sl]
                dyp = dy_ref[:, sl]
                dtp, eap, wp, Dp = dte[:, sl], eae[:, sl], wte[:, sl], d_ref[:, sl]
                xd = xs * dtp
                dxd = jnp.zeros((L, LANES), F32)
                for r in range(2):
                    h = 2 * hp + r
                    mh = head0 if r == 0 else jnp.logical_not(head0)
                    dyh = jnp.where(mh, dyp, 0.0).astype(BF16)
                    xdh = jnp.where(mh, xd, 0.0).astype(BF16)
                    seg = acs[:, h:h + 1] - acsT[h:h + 1, :]
                    dec = jnp.exp(jnp.where(trilb, seg, -jnp.inf))
                    decT = jnp.exp(jnp.where(ri <= ci, -seg, -jnp.inf))
                    G = lax.dot_general(dyh, xdh, _NT, preferred_element_type=F32)
                    gd = G * dec
                    dcb = dcb + gd
                    dseg = gd * CB
                    dsr_ref[:, h:h + 1] = jnp.sum(dseg, axis=1, keepdims=True)
                    dsc_ref[h:h + 1, :] = jnp.sum(dseg, axis=0, keepdims=True)
                    dxd = dxd + jnp.dot((CBT * decT).astype(BF16), dyh, preferred_element_type=F32)
                Up = U[:, qs]
                r_ref[0, :, sl] = dyp * Z[:, qs] * eap
                r_ref[1, :, sl] = dxd * xs
                r_ref[2, :, sl] = Up * xs
                dz_ref[:, sl] = (dyp * eap).astype(BF16)
                xw_ref[:, sl] = (xs * wp).astype(BF16)
                dxs_ref[:, sl] = dxd * dtp + Dp * dyp + Up * wp
                dd_ref[:, sl] += jnp.sum(dyp * xs, axis=0, keepdims=True)
            dcbs.append(dcb)
            t = dstg * stg
            for k in range(HEADS_PER_GROUP):
                dlast = dlast + jnp.where(lane == g * HEADS_PER_GROUP + k,
                                          jnp.sum(t[k * 64:(k + 1) * 64, :], keepdims=True), 0.0)
        fold = lambda k: jnp.dot(r_ref[k], et_ref[...], precision=SEL, preferred_element_type=F32)
        r1, r2, dws = fold(0), fold(1), fold(2)
        dww = dws * wt
        ddt = r2 + dws * er
        dacs = r1 - dww + dsr_ref[...] - dsc_ref[...].T
        dlast = dlast * elast + jnp.sum(dww, axis=0, keepdims=True)
        lasts = [last[:, h:h + 1] for h in range(SSM_HEADS)]
        for g in range(SSM_GROUPS):
            Bg = b_ref[:, g * N:(g + 1) * N].astype(BF16)
            Cg = c_ref[:, g * N:(g + 1) * N].astype(BF16)
            gs = slice(g * GW, (g + 1) * GW)
            stb = st_in_ref[0, gs, :].astype(BF16)
            dstb = dst_ref[gs, :].astype(BF16)
            dcbb = dcbs[g].astype(BF16)
            dzg = dz_ref[:, gs]
            dc_ref[:, g * N:(g + 1) * N] = (jnp.dot(dzg, stb, preferred_element_type=F32)
                                            + jnp.dot(dcbb, Bg, preferred_element_type=F32))
            db_ref[:, g * N:(g + 1) * N] = (jnp.dot(xw_ref[:, gs], dstb, preferred_element_type=F32)
                                            + lax.dot_general(dcbb, Cg, _TN, preferred_element_type=F32))
            dsn = lax.dot_general(dzg, Cg, _TN, preferred_element_type=F32)
            for k in range(HEADS_PER_GROUP):
                h = g * HEADS_PER_GROUP + k
                rows = slice(h * 64, (h + 1) * 64)
                dst_ref[rows, :] = dst_ref[rows, :] * jnp.exp(lasts[h]) + dsn[k * 64:(k + 1) * 64, :]
        rowi = lax.broadcasted_iota(jnp.int32, (L, 1), 0)
        dacs = dacs + jnp.where(rowi == L - 1, dlast, 0.0)
        da = jnp.dot((ci >= ri).astype(F32), dacs, precision=HI, preferred_element_type=F32)
        ddt = ddt + da * A
        da_ref[...] += jnp.sum(da * dt, axis=0, keepdims=True)
        ddtr = jnp.where(lane < SSM_HEADS, ddt * _sigmoid(u), 0.0)
        ddtr_ref[...] = ddtr
        ddtb_ref[...] += jnp.sum(ddtr, axis=0, keepdims=True)

    rev = lambda c: nc - 1 - c
    gn = SSM_GROUPS * N
    res = pl.pallas_call(
        body, name=name, grid=(nc,),
        in_specs=[pl.BlockSpec((L, D_INNER), lambda c: (rev(c), 0)),
                  pl.BlockSpec((L, gn), lambda c: (rev(c), D_INNER // gn)),
                  pl.BlockSpec((L, gn), lambda c: (rev(c), D_INNER // gn + 1)),
                  pl.BlockSpec((L, LANES), lambda c: (rev(c), 0)),
                  pl.BlockSpec((1, LANES), lambda c: (0, 0)),
                  pl.BlockSpec((1, LANES), lambda c: (0, 0)),
                  pl.BlockSpec((1, D_INNER), lambda c: (0, 0)),
                  pl.BlockSpec((LANES, D_INNER), lambda c: (0, 0)),
                  pl.BlockSpec((D_INNER, LANES), lambda c: (0, 0)),
                  pl.BlockSpec((1, D_INNER, N), lambda c: (rev(c), 0, 0)),
                  pl.BlockSpec((L, D_INNER), lambda c: (rev(c), 0))] + r_in,
        out_specs=[pl.BlockSpec((L, D_INNER), lambda c: (rev(c), 0)),
                   pl.BlockSpec((L, gn), lambda c: (rev(c), 0)),
                   pl.BlockSpec((L, gn), lambda c: (rev(c), 0)),
                   pl.BlockSpec((L, LANES), lambda c: (rev(c), 0)),
                   pl.BlockSpec((1, LANES), lambda c: (0, 0)),
                   pl.BlockSpec((1, LANES), lambda c: (0, 0)),
                   pl.BlockSpec((1, D_INNER), lambda c: (0, 0))] + r_out,
        out_shape=[jax.ShapeDtypeStruct((S, D_INNER), F32), jax.ShapeDtypeStruct((S, gn), F32),
                   jax.ShapeDtypeStruct((S, gn), F32), jax.ShapeDtypeStruct((S, LANES), F32),
                   jax.ShapeDtypeStruct((1, LANES), F32), jax.ShapeDtypeStruct((1, LANES), F32),
                   jax.ShapeDtypeStruct((1, D_INNER), F32)] + r_shapes,
        scratch_shapes=[pltpu.VMEM((D_INNER, N), F32), pltpu.VMEM((L, D_INNER), BF16), pltpu.VMEM((L, D_INNER), BF16),
                        pltpu.VMEM((3, L, D_INNER), F32), pltpu.VMEM((L, LANES), F32), pltpu.VMEM((LANES, L), F32)]
                       + r_sems,
        compiler_params=pltpu.CompilerParams(dimension_semantics=("arbitrary",), vmem_limit_bytes=VMEM_LIMIT,
                                             has_side_effects=rider is not None),
    )(xbc, xbc, xbc, dtraw, dtb, A, dexp, e_mat, et_mat, states, dy, *r_args)
    return res[:7], res[7:]


BAND_A = (A_PREV + 1) * CHUNK
REL_W = 640


def _relpos_select():
    k = np.arange(REL_W)
    rel = np.where(k < BAND_A, A_PREV * CHUNK - k, A_PREV * CHUNK - (k - REL_W))
    idx = np.clip(rel, -MAX_REL, MAX_REL) + MAX_REL
    sel = (np.arange(REL_W)[:, None] == idx[None, :]) & (k != BAND_A)[None, :]
    return sel.astype(np.float32)


def relpos_bias(table_pad, *, name):
    def body(t_ref, s_ref, o_ref):
        v = jnp.dot(t_ref[...], s_ref[...], precision=HI, preferred_element_type=F32)
        for h in range(N_HEADS):
            o_ref[h] = pltpu.roll(jnp.broadcast_to(v[h:h + 1, :], (CHUNK, REL_W)), 0, 1, stride=1, stride_axis=0)

    return pl.pallas_call(body, name=name, out_shape=jax.ShapeDtypeStruct((N_HEADS, CHUNK, REL_W), F32),
                          compiler_params=pltpu.CompilerParams(vmem_limit_bytes=VMEM_LIMIT),
                          )(table_pad, jnp.asarray(_relpos_select()))


def relpos_grad(dbias_rev, *, name):
    def body(d_ref, s_ref, o_ref):
        head = lax.broadcasted_iota(jnp.int32, (N_HEADS, 1), 0)
        dv = jnp.zeros((N_HEADS, REL_W), F32)
        for h in range(N_HEADS):
            back = pltpu.roll(d_ref[h], REL_W - (CHUNK - 1), 1, stride=1, stride_axis=0)
            dv = dv + jnp.where(head == h, jnp.sum(back, axis=0, keepdims=True), 0.0)
        o_ref[...] = lax.dot_general(dv, s_ref[...], _NT, precision=HI, preferred_element_type=F32)

    return pl.pallas_call(body, name=name, out_shape=jax.ShapeDtypeStruct((N_HEADS, REL_W), F32),
                          compiler_params=pltpu.CompilerParams(vmem_limit_bytes=VMEM_LIMIT),
                          )(dbias_rev, jnp.asarray(_relpos_select()))


def loss_head(y, t, *, name, tm=256):
    S, D = y.shape
    tm = min(tm, S)

    def body(y_ref, t_ref, dy_ref, l_ref):
        e = y_ref[...] - t_ref[...]
        dy_ref[...] = e * (1.0 / D)

        @pl.when(pl.program_id(0) == 0)
        def _():
            l_ref[...] = jnp.zeros_like(l_ref)

        part = jnp.sum(jnp.sum(e * e, axis=1, keepdims=True), axis=0, keepdims=True) * (0.5 / D)
        l_ref[...] += jnp.broadcast_to(part, l_ref.shape)

    return pl.pallas_call(
        body, name=name, grid=(S // tm,),
        in_specs=[pl.BlockSpec((tm, D), lambda i: (i, 0))] * 2,
        out_specs=[pl.BlockSpec((tm, D), lambda i: (i, 0)), pl.BlockSpec((1, LANES), lambda i: (0, 0))],
        out_shape=[jax.ShapeDtypeStruct((S, D), F32), jax.ShapeDtypeStruct((1, LANES), F32)],
        compiler_params=_params("arbitrary"),
    )(y, t)


def f_adamw(w, g, m, v):
    m = ADAM_B1 * m + (1.0 - ADAM_B1) * g
    v = ADAM_B2 * v + (1.0 - ADAM_B2) * (g * g)
    m_hat = m / (1.0 - ADAM_B1 ** ADAM_STEP)
    v_hat = v / (1.0 - ADAM_B2 ** ADAM_STEP)
    delta = -ADAM_LR * (m_hat / (jnp.sqrt(v_hat) + ADAM_EPS) + ADAM_WD * w)
    return delta, m, v


def f_norm_id(x, g):
    return f_rmsnorm(x, g)[0], x


ANY = pl.BlockSpec(memory_space=pl.ANY)


def _pos():
    return lax.axis_index("x"), lax.axis_index("y"), lax.axis_index("c")


def _other_chips(x, y):
    return [(1 - x, y), (x, 1 - y), (1 - x, 1 - y)]


class Rider:
    def __init__(self, ins, outs, sems, start, mid, finish):
        self.ins, self.outs, self.sems = list(ins), list(outs), list(sems)
        self.start, self.mid, self.finish = start, mid, finish


def _rider_parts(rider):
    if rider is None:
        return [], [], [], [], []
    return [ANY] * len(rider.ins), [ANY] * len(rider.outs), rider.outs, rider.sems, rider.ins


def _ride(rider, ins, outs, sems, first, mid, last):
    pos = _pos()

    @pl.when(first)
    def _():
        rider.start(pos, ins, outs, sems)

    if rider.mid is not None:
        @pl.when(mid)
        def _():
            rider.mid(pos, ins, outs, sems)

    @pl.when(last)
    def _():
        rider.finish(pos, ins, outs, sems)


def run_rider(rider, *, name):
    n_in, n_out = len(rider.ins), len(rider.outs)

    def body(*refs):
        ins, outs, sems = refs[:n_in], refs[n_in:n_in + n_out], refs[n_in + n_out:]
        pos = _pos()
        rider.start(pos, ins, outs, sems)
        if rider.mid is not None:
            rider.mid(pos, ins, outs, sems)
        rider.finish(pos, ins, outs, sems)

    return pl.pallas_call(
        body, name=name, in_specs=[ANY] * n_in, out_specs=[ANY] * n_out, out_shape=rider.outs,
        scratch_shapes=rider.sems, compiler_params=pltpu.CompilerParams(has_side_effects=True),
    )(*rider.ins)


def gather_rider(shards):
    n = len(shards)

    def copies(pos, ins, outs, sems):
        x, y, c = pos
        send, recv, fsend, frecv = sems
        me = 2 * x + y
        sib = (x, y, 1 - c)
        first, arrive, passed, theirs = [], [], [], []
        for i in range(n):
            for j, (px, py) in enumerate(_other_chips(x, y)):
                k = 3 * i + j
                far = dict(device_id=(px, py, c), device_id_type=MESH)
                near = dict(device_id=sib, device_id_type=MESH)
                got = outs[i].at[2 * px + py, c]
                his = outs[i].at[2 * px + py, 1 - c]
                first.append(pltpu.make_async_remote_copy(ins[i].at[c], outs[i].at[me, c], send.at[k], recv.at[k], **far))
                arrive.append(pltpu.make_async_remote_copy(ins[i].at[c], got, send.at[k], recv.at[k], **far))
                passed.append(pltpu.make_async_remote_copy(got, got, fsend.at[k], frecv.at[k], **near))
                theirs.append(pltpu.make_async_remote_copy(his, his, fsend.at[k], frecv.at[k], **near))
        return first, arrive, passed, theirs

    def start(*a):
        for cp in copies(*a)[0]:
            cp.start()

    def mid(*a):
        _, arrive, passed, _ = copies(*a)
        for got, cp in zip(arrive, passed):
            got.wait_recv()
            cp.start()

    def finish(*a):
        first, _, passed, theirs = copies(*a)
        for cp in theirs:
            cp.wait_recv()
        for cp in first + passed:
            cp.wait_send()

    return Rider(shards, [jax.ShapeDtypeStruct((4,) + s.shape, s.dtype) for s in shards],
                 [pltpu.SemaphoreType.DMA((3 * n,))] * 4, start, mid, finish)


def scatter_rider(ps):
    n = len(ps)

    def copies(pos, ins, outs, sems):
        x, y, c = pos
        send, recv = sems
        return [pltpu.make_async_remote_copy(ins[i].at[2 * px + py], outs[i].at[j], send.at[3 * i + j], recv.at[3 * i + j],
                                             device_id=(px, py, c), device_id_type=MESH)
                for i in range(n) for j, (px, py) in enumerate(_other_chips(x, y))]

    def start(*a):
        for cp in copies(*a):
            cp.start()

    def finish(*a):
        for cp in copies(*a):
            cp.wait()

    return Rider(ps, [jax.ShapeDtypeStruct((3,) + p.shape[1:], p.dtype) for p in ps],
                 [pltpu.SemaphoreType.DMA((3 * n,))] * 2, start, None, finish)


def pair_swap_halves(gs, *, name):
    n = len(gs)

    def body(*refs):
        ins, outs = refs[:n], refs[n:2 * n]
        send, recv = refs[2 * n:]
        x, y, c = _pos()
        cps = []
        for i in range(n):
            cp = pltpu.make_async_remote_copy(ins[i].at[1 - c], outs[i], send.at[i], recv.at[i],
                                              device_id=(x, y, 1 - c), device_id_type=MESH)
            cp.start()
            cps.append(cp)
        for cp in cps:
            cp.wait()

    return pl.pallas_call(
        body, name=name, in_specs=[ANY] * n, out_specs=[ANY] * n,
        out_shape=[jax.ShapeDtypeStruct(g.shape[1:], g.dtype) for g in gs],
        scratch_shapes=[pltpu.SemaphoreType.DMA((n,)), pltpu.SemaphoreType.DMA((n,))],
        compiler_params=pltpu.CompilerParams(has_side_effects=True),
    )(*gs)


def pair_share(hs, *, name):
    n = len(hs)

    def body(*refs):
        ins, outs = refs[:n], refs[n:2 * n]
        send, recv = refs[2 * n:]
        x, y, c = _pos()
        cps = []
        for i in range(n):
            cp = pltpu.make_async_remote_copy(ins[i], outs[i], send.at[i], recv.at[i],
                                              device_id=(x, y, 1 - c), device_id_type=MESH)
            cp.start()
            cps.append(cp)
        for cp in cps:
            cp.wait()

    return pl.pallas_call(
        body, name=name, in_specs=[ANY] * n, out_specs=[ANY] * n,
        out_shape=[jax.ShapeDtypeStruct(h.shape, h.dtype) for h in hs],
        scratch_shapes=[pltpu.SemaphoreType.DMA((n,)), pltpu.SemaphoreType.DMA((n,))],
        compiler_params=pltpu.CompilerParams(has_side_effects=True),
    )(*hs)


def gather_all(buf, *, name):
    def body(in_ref, out_ref, send, recv, loc):
        x, y, c = _pos()
        lid = 4 * x + 2 * y + c
        lc = pltpu.make_async_copy(in_ref, out_ref.at[lid], loc.at[0])
        lc.start()
        cps = []
        for k in range(1, 8):
            px = 1 - x if k & 4 else x
            py = 1 - y if k & 2 else y
            pc = 1 - c if k & 1 else c
            cp = pltpu.make_async_remote_copy(in_ref, out_ref.at[lid], send.at[k - 1], recv.at[k - 1],
                                              device_id=(px, py, pc), device_id_type=MESH)
            cp.start()
            cps.append((cp, 4 * px + 2 * py + pc, (px, py, pc)))
        for k, (cp, plid, peer) in enumerate(cps):
            cp.wait_send()
            pltpu.make_async_remote_copy(in_ref, out_ref.at[plid], send.at[k], recv.at[k],
                                         device_id=peer, device_id_type=MESH).wait_recv()
        lc.wait()

    return pl.pallas_call(
        body, name=name, in_specs=[ANY], out_specs=ANY,
        out_shape=jax.ShapeDtypeStruct((8,) + buf.shape, buf.dtype),
        scratch_shapes=[pltpu.SemaphoreType.DMA((7,)), pltpu.SemaphoreType.DMA((7,)), pltpu.SemaphoreType.DMA((1,))],
        compiler_params=pltpu.CompilerParams(has_side_effects=True),
    )(buf)


def sum_slots(a, *, name):
    n = a.shape[0]

    def body(a_ref, o_ref):
        acc = a_ref[0]
        for k in range(1, n):
            acc = acc + a_ref[k]
        o_ref[...] = acc

    return pl.pallas_call(body, name=name, out_shape=jax.ShapeDtypeStruct(a.shape[1:], a.dtype),
                          compiler_params=pltpu.CompilerParams(vmem_limit_bytes=VMEM_LIMIT))(a)


def _row_tile(r, want, mult=16):
    t = (min(want, r) // mult) * mult
    while t >= mult:
        if r % t == 0:
            return t
        t -= mult
    return r


def pair_add(g, r1, csel, *, name):
    _, _, r, C = g.shape
    tr = _row_tile(r, 256)

    def body(g_ref, r_ref, c_ref, p32_ref, pb_ref):
        south = c_ref[0:1, 0:1] == 0.0
        p = jnp.where(south, g_ref[0, 0], g_ref[1, 0]) + r_ref[0]
        p32_ref[0] = p
        pb_ref[0] = p.astype(BF16)

    return pl.pallas_call(
        body, name=name, grid=(4, r // tr),
        in_specs=[pl.BlockSpec((2, 1, tr, C), lambda j, t: (0, j, t, 0)), pl.BlockSpec((1, tr, C), lambda j, t: (j, t, 0)),
                  pl.BlockSpec((1, LANES), lambda j, t: (0, 0))],
        out_specs=[pl.BlockSpec((1, tr, C), lambda j, t: (j, t, 0))] * 2,
        out_shape=[jax.ShapeDtypeStruct((4, r, C), F32), jax.ShapeDtypeStruct((4, r, C), BF16)],
        compiler_params=_params("parallel", "parallel"),
    )(g, r1, csel)


def chip_add(p32, r3, msel, *, name):
    _, r, C = p32.shape
    tr = _row_tile(r, 128)

    def body(p_ref, r_ref, m_ref, o_ref):
        me = m_ref[0:1, 0:1]
        acc = jnp.where(me == 0.0, p_ref[0], jnp.where(me == 1.0, p_ref[1], jnp.where(me == 2.0, p_ref[2], p_ref[3])))
        for j in range(3):
            acc = acc + r_ref[j].astype(F32)
        o_ref[...] = acc

    return pl.pallas_call(
        body, name=name, grid=(r // tr,),
        in_specs=[pl.BlockSpec((4, tr, C), lambda t: (0, t, 0)), pl.BlockSpec((3, tr, C), lambda t: (0, t, 0)),
                  pl.BlockSpec((1, LANES), lambda t: (0, 0))],
        out_specs=pl.BlockSpec((tr, C), lambda t: (t, 0)),
        out_shape=jax.ShapeDtypeStruct((r, C), F32),
        compiler_params=_params("parallel"),
    )(p32, r3, msel)


def _consts():
    i512 = np.arange(N_HEADS * HEAD_DIM)
    i128 = np.arange(LANES)
    bd512 = (i512[:, None] // HEAD_DIM == i512[None, :] // HEAD_DIM).astype(np.float32)
    bd128 = (i128[:, None] // HEAD_DIM == i128[None, :] // HEAD_DIM).astype(np.float32)
    fold = (np.arange(HEAD_DIM)[:, None] == (i512[None, :] % HEAD_DIM)).astype(np.float32)
    grp = N_HEADS // 2 * HEAD_DIM
    expand = ((i128[:, None] // HEAD_DIM == i512[None, :] // grp)
              & (i128[:, None] % HEAD_DIM == i512[None, :] % HEAD_DIM)).astype(np.float32)
    band = (B_PREV + 1) * CHUNK
    rel = np.arange(CHUNK)[:, None] - (np.arange(band)[None, :] - B_PREV * CHUNK)
    slopes = 2.0 ** (-8.0 * np.arange(1, N_HEADS + 1, dtype=np.float32) / N_HEADS)
    bias_b = (-slopes[:, None, None] * np.abs(rel).astype(np.float32)[None]).astype(np.float32)
    return [jnp.asarray(a) for a in (bd512, bd128, fold, expand)], jnp.asarray(bias_b)


def _ffn_fwd(xin, l, W, P):
    g = P["norm_ffn"][l:l + 1]
    (h,) = rowwise(f_rmsnorm, [xin], [g], [(D_MODEL, BF16)], name=f"ffn{l}_norm")
    Wi = W["ffn_in"][l]
    gate = matmul(h, Wi[:, :D_FF], mode="nn", name=f"ffn{l}_gate", out_dtype=BF16)
    up = matmul(h, Wi[:, D_FF:], mode="nn", name=f"ffn{l}_up", out_dtype=BF16)
    gc, act = dwconv_fwd(gate, P["ffn_conv_w"][l], P["ffn_conv_b"][l:l + 1], lambda y, u: (y, _silu(y) * u), [up],
                         [BF16, BF16], name=f"ffn{l}_conv")
    xout = matmul(act, W["ffn_out"][l], mode="nn", name=f"ffn{l}_out", residual=xin)
    return xout, (xin, h, gate, gc, up, act)


def _ffn_bwd(dxout, l, saved, W, P):
    xin, h, gate, gc, up, act = saved
    g = P["norm_ffn"][l:l + 1]
    Wi = W["ffn_in"][l]
    dact = matmul(dxout, W["ffn_out"][l], mode="nt", name=f"ffn{l}_dact", out_dtype=BF16)
    dWo = matmul(act, dxout, mode="tn", name=f"ffn{l}_dwout")
    dgate, dcw, dcb, dup = dwconv_bwd(gate, P["ffn_conv_w"][l], [gc, up, dact],
                                      lambda c, u, da: (da * u * _dsilu(c), da * _silu(c)), [BF16],
                                      name=f"ffn{l}_dconv")
    dh = matmul(dgate, Wi[:, :D_FF], mode="nt", name=f"ffn{l}_dh_gate")
    dh = matmul(dup, Wi[:, D_FF:], mode="nt", name=f"ffn{l}_dh_up", residual=dh)
    dWi = jnp.concatenate([matmul(h, dgate, mode="tn", name=f"ffn{l}_dw_gate"),
                           matmul(h, dup, mode="tn", name=f"ffn{l}_dw_up")], axis=1)
    dxin, dg = rowwise_vjp(f_norm_id, [xin], [g], [dh, dxout], [(0, F32)], [0], name=f"ffn{l}_dnorm")
    return dxin, dWi, dWo, dg, dcw, dcb


class NoComm:
    def fwd_rider(self, tag):
        return None

    def fwd_done(self, tag, outs, W, P):
        pass

    def grads(self, tag, cols, rows):
        return None

    def bwd_done(self, tag, outs):
        pass


def local_step(x, tgt, W, P, comm):
    qk_consts, bias_b = _consts()
    pad_rows = lambda t, n: jnp.pad(t, ((n * CHUNK, 0), (0, 0)))
    DQ = N_HEADS * HEAD_DIM

    g_mix0 = P["norm_mix"][0:1]
    (h0,) = rowwise(f_rmsnorm, [x], [g_mix0], [(D_MODEL, BF16)], name="attn_norm")
    qkv = matmul(h0, W["attn_in"], mode="nn", name="attn_qkv")
    qk_par = [P["q_norm_a"], P["k_norm_a"], P["q_norm_b"], P["k_norm_b"]] + qk_consts
    qa, ka, va, qb, kb, vb = rowwise(f_qknorm, [qkv], qk_par, [(DQ, BF16)] * 6, name="attn_qknorm")
    ka, va, kb, vb = pad_rows(ka, A_PREV), pad_rows(va, A_PREV), pad_rows(kb, B_PREV), pad_rows(vb, B_PREV)
    table = jnp.pad(P["relpos_table"], ((0, 0), (0, REL_W - (2 * MAX_REL + 1))))
    nj = min(ATT_TQ, x.shape[0]) // CHUNK
    bias_a = widen_bias(relpos_bias(table, name="relpos_bias")[:, :, :BAND_A], A_PREV, nj)
    bias_b = widen_bias(bias_b, B_PREV, nj)
    sinks = jnp.broadcast_to(P["sinks"].reshape(N_HEADS // 2, 2, 1), (N_HEADS // 2, 2, LANES))
    oa, late = attn_fwd(qa, ka, va, bias_a, None, n_prev=A_PREV, name="attn_a", rider=comm.fwd_rider("a"))
    comm.fwd_done("a", late, W, P)
    ob, late = attn_fwd(qb, kb, vb, bias_b, sinks, n_prev=B_PREV, name="attn_b", rider=comm.fwd_rider("b"))
    comm.fwd_done("b", late, W, P)
    Wao = W["attn_out"]
    x1 = matmul(oa, Wao[:DQ], mode="nn", name="attn_out_a", residual=x)
    x1 = matmul(ob, Wao[DQ:], mode="nn", name="attn_out_b", residual=x1)
    x2, ffn0 = _ffn_fwd(x1, 0, W, P)

    g_mix1 = P["norm_mix"][1:2]
    (h2,) = rowwise(f_rmsnorm, [x2], [g_mix1], [(D_MODEL, BF16)], name="ssm_norm_in")
    Ws = W["ssm_in"]
    CC = D_INNER + 2 * SSM_GROUPS * SSM_STATE
    Wz, Wx = Ws[:, :D_INNER], Ws[:, D_INNER:D_INNER + CC]
    Wdt = jnp.pad(Ws[:, D_INNER + CC:], ((0, 0), (0, LANES - SSM_HEADS)))
    z = matmul(h2, Wz, mode="nn", name="ssm_z", out_dtype=BF16)
    xr = matmul(h2, Wx, mode="nn", name="ssm_xbc", out_dtype=BF16)
    dtraw = matmul(h2, Wdt, mode="nn", name="ssm_dt")
    xc, xbc = dwconv_fwd(xr, P["ssm_conv_w"], P["ssm_conv_b"], lambda y: (y, _silu(y)), [], [BF16, F32],
                         name="ssm_conv")
    pad32 = lambda v: jnp.pad(v, ((0, 0), (0, LANES - SSM_HEADS)))
    A = pad32(-jnp.exp(P["ssm_a_log"]))
    dtb = pad32(P["ssm_dt_bias"])
    dexp = jnp.repeat(P["ssm_d"], D_INNER // SSM_HEADS, axis=1)
    y, states = ssd_fwd(xbc, dtraw, dtb, A, dexp, name="ssd_fwd")
    (y2,) = rowwise(f_gate_norm, [y, z], [P["ssm_norm"]], [(D_INNER, BF16)], name="ssm_gate_norm")
    x3 = matmul(y2, W["ssm_out"], mode="nn", name="ssm_out", residual=x2)
    x4, ffn1 = _ffn_fwd(x3, 1, W, P)

    dx4, lpart = loss_head(x4, tgt, name="loss_head")

    dx3, dWfi1, dWfo1, dgf1, dfcw1, dfcb1 = _ffn_bwd(dx4, 1, ffn1, W, P)
    out_f1 = comm.grads("f1", dWfi1, dWfo1)
    dy2 = matmul(dx3, W["ssm_out"], mode="nt", name="ssm_dy")
    dWso = matmul(y2, dx3, mode="tn", name="ssm_dwout")
    dy, dz, dnw = rowwise_vjp(f_gate_norm, [y, z], [P["ssm_norm"]], [dy2], [(0, F32), (1, BF16)], [0],
                              name="ssm_dgate_norm")
    (dxs, dB, dC, ddtraw, dA, ddtb, dDl), sent = ssd_bwd(xbc, dtraw, dtb, A, dexp, states, dy, name="ssd_bwd",
                                                          rider=out_f1)
    comm.bwd_done("f1", sent)
    dxr, dscw, dscb = dwconv_bwd(xr, P["ssm_conv_w"], [xc, (dxs, dB, dC)], lambda c, g: (g * _dsilu(c),), [],
                                 name="ssm_dconv")
    dh2 = matmul(dz, Wz, mode="nt", name="ssm_dh_z")
    dh2 = matmul(dxr, Wx, mode="nt", name="ssm_dh_x", residual=dh2)
    dh2 = matmul(ddtraw, Wdt, mode="nt", name="ssm_dh_dt", residual=dh2)
    dWs = jnp.concatenate([matmul(h2, dz, mode="tn", name="ssm_dw_z"),
                           matmul(h2, dxr, mode="tn", name="ssm_dw_x"),
                           matmul(h2, ddtraw, mode="tn", name="ssm_dw_dt")[:, :SSM_HEADS]], axis=1)
    dx2, dgm1 = rowwise_vjp(f_norm_id, [x2], [g_mix1], [dh2, dx3], [(0, F32)], [0], name="ssm_dnorm_in")
    out_s = comm.grads("s", dWs, dWso)

    dx1, dWfi0, dWfo0, dgf0, dfcw0, dfcb0 = _ffn_bwd(dx2, 0, ffn0, W, P)
    out_f0 = comm.grads("f0", dWfi0, dWfo0)
    doa = matmul(dx1, Wao[:DQ], mode="nt", name="attn_do_a", out_dtype=BF16)
    dob = matmul(dx1, Wao[DQ:], mode="nt", name="attn_do_b", out_dtype=BF16)
    dWao = jnp.concatenate([matmul(oa, dx1, mode="tn", name="attn_dwout_a"),
                            matmul(ob, dx1, mode="tn", name="attn_dwout_b")], axis=0)
    (dqa, dka, dva, dbias_a), sent = attn_bwd(qa, ka, va, doa, bias_a, None, n_prev=A_PREV, name="attn_a_bwd",
                                              rider=out_s)
    comm.bwd_done("s", sent)
    (dqb, dkb, dvb, _, dsk), sent = attn_bwd(qb, kb, vb, dob, bias_b, sinks, n_prev=B_PREV, name="attn_b_bwd",
                                             rider=out_f0)
    comm.bwd_done("f0", sent)
    pa, pb = A_PREV * CHUNK, B_PREV * CHUNK
    dqkv, dgqa, dgka, dgqb, dgkb = rowwise_vjp(f_qknorm, [qkv], qk_par, [dqa, dka, dva, dqb, dkb[pb:], dvb[pb:]],
                                               [(0, BF16)], [0, 1, 2, 3], name="attn_dqknorm",
                                               cot_skip=[0, pa, pa, 0, 0, 0])
    dh0 = matmul(dqkv, W["attn_in"], mode="nt", name="attn_dh")
    dWai = matmul(h0, dqkv, mode="tn", name="attn_dwin")
    dx, dgm0 = rowwise_vjp(f_norm_id, [x], [g_mix0], [dh0, dx1], [(0, F32)], [0], name="attn_dnorm")
    dbias_a = fold_bias(dbias_a, A_PREV, nj)
    dbias_rev = jnp.pad(dbias_a[:, ::-1, :], ((0, 0), (0, 0), (0, REL_W - BAND_A)))
    dtable = relpos_grad(dbias_rev, name="relpos_grad")[:, :2 * MAX_REL + 1]

    gW = {"attn_in": dWai, "attn_out": dWao, "ssm_in": dWs, "ssm_out": dWso,
          "ffn_in": [dWfi0, dWfi1], "ffn_out": [dWfo0, dWfo1]}
    gP = {"norm_mix": jnp.concatenate([dgm0, dgm1], axis=0),
          "norm_ffn": jnp.concatenate([dgf0, dgf1], axis=0),
          "relpos_table": dtable, "q_norm_a": dgqa, "k_norm_a": dgka, "q_norm_b": dgqb, "k_norm_b": dgkb,
          "sinks": dsk[:, :, 0].reshape(1, N_HEADS),
          "ssm_conv_w": dscw, "ssm_conv_b": dscb,
          "ssm_dt_bias": ddtb[:, :SSM_HEADS], "ssm_a_log": dA[:, :SSM_HEADS] * A[:, :SSM_HEADS],
          "ssm_d": dDl.reshape(SSM_HEADS, D_INNER // SSM_HEADS).sum(axis=1).reshape(1, SSM_HEADS),
          "ssm_norm": dnw,
          "ffn_conv_w": jnp.stack([dfcw0, dfcw1]), "ffn_conv_b": jnp.concatenate([dfcb0, dfcb1], axis=0)}
    return lpart, dx, gW, gP


WEIGHTS = ["norm_mix", "norm_ffn", "attn_w_in", "attn_w_out", "relpos_table", "q_norm_a", "k_norm_a", "q_norm_b",
           "k_norm_b", "sinks", "ssm_w_in", "ssm_conv_w", "ssm_conv_b", "ssm_dt_bias", "ssm_a_log", "ssm_d",
           "ssm_norm", "ssm_w_out", "ffn_w_in", "ffn_conv_w", "ffn_conv_b", "ffn_w_out"]
ARGS = ["x"] + WEIGHTS + ["loss_target"] + ["m_" + w for w in WEIGHTS] + ["v_" + w for w in WEIGHTS]
N_CHIPS = 4
SMALL_ROWS = 384
SMALL_ORDER = ["norm_mix", "norm_ffn", "relpos_table", "q_norm_a", "k_norm_a", "q_norm_b", "k_norm_b", "sinks",
               "ssm_dt_bias", "ssm_a_log", "ssm_d", "ffn_conv_b", "ssm_conv_w", "ssm_conv_b", "ssm_norm", "ffn_conv_w"]


def _cols_to_slabs(g):
    K, N = g.shape
    return g.reshape(2, K // 2, N_CHIPS, N // N_CHIPS).transpose(0, 2, 1, 3)


def _rows_to_slabs(g):
    R, C = g.shape
    return g.reshape(N_CHIPS, 2, R // (2 * N_CHIPS), C).transpose(1, 0, 2, 3)


class MeshComm:
    def __init__(self, d, xi, yi, ci):
        self.d, self.ci, self.me = d, ci, 2 * xi + yi
        self.csel = jnp.full((1, LANES), ci, F32)
        self.msel = jnp.full((1, LANES), self.me, F32)
        halves = lambda w: w.reshape((2, -1, w.shape[-1]))
        small = jnp.concatenate([d[k].reshape(-1) for k in ("ssm_conv_w", "ssm_conv_b", "ssm_norm", "ffn_conv_w")])
        small = jnp.pad(small, (0, 2 * 40 * LANES - small.shape[0])).reshape(2, 40, LANES)
        self.shards = {"attn": [halves(d["attn_w_in"][0].astype(BF16)), halves(d["attn_w_out"][0].astype(BF16))],
                       "a": [d["ffn_w_in"].astype(BF16), small],
                       "b": [d["ffn_w_out"].astype(BF16), halves(d["ssm_w_in"][0].astype(BF16)),
                             halves(d["ssm_w_out"][0].astype(BF16))]}
        self.p32, self.mine = {}, {}

    def _whole(self, tag, outs):
        return [lax.dynamic_update_slice_in_dim(g, s[None], self.me, axis=0) for g, s in zip(outs, self.shards[tag])]

    @staticmethod
    def _cat_cols(g):
        return jnp.concatenate([g[j].reshape((-1, g.shape[-1])) for j in range(N_CHIPS)], axis=1)

    def first_weights(self):
        g_ai, g_ao = self._whole("attn", run_rider(gather_rider(self.shards["attn"]), name="gather_attn"))
        return {"attn_in": self._cat_cols(g_ai), "attn_out": g_ao.reshape(-1, D_MODEL)}

    def fwd_rider(self, tag):
        return gather_rider(self.shards[tag])

    def fwd_done(self, tag, outs, W, P):
        if tag == "b":
            g_fo, g_si, g_so = self._whole("b", outs)
            W["ffn_out"] = [g_fo[:, l].reshape(-1, D_MODEL) for l in range(2)]
            W["ssm_in"], W["ssm_out"] = self._cat_cols(g_si), g_so.reshape(-1, D_MODEL)
            return
        g_fi, g_sm = self._whole("a", outs)
        W["ffn_in"] = [jnp.concatenate([g_fi[j, l] for j in range(N_CHIPS)], axis=1) for l in range(2)]
        sm = g_sm.reshape(N_CHIPS, -1)
        CC = D_INNER + 2 * SSM_GROUPS * SSM_STATE
        c4, f4 = CC // N_CHIPS, D_FF // N_CHIPS
        o1 = SSM_CONV * c4
        o2 = o1 + c4
        o3 = o2 + D_INNER // N_CHIPS
        o4 = o3 + 2 * FFN_CONV * f4
        P["ssm_conv_w"] = sm[:, :o1].reshape(N_CHIPS, SSM_CONV, c4).transpose(1, 0, 2).reshape(SSM_CONV, CC)
        P["ssm_conv_b"] = sm[:, o1:o2].reshape(1, CC)
        P["ssm_norm"] = sm[:, o2:o3].reshape(1, D_INNER)
        P["ffn_conv_w"] = sm[:, o3:o4].reshape(N_CHIPS, 2, FFN_CONV, f4).transpose(1, 2, 0, 3).reshape(2, FFN_CONV, D_FF)

    def grads(self, tag, cols, rows):
        slabs = [_cols_to_slabs(cols), _rows_to_slabs(rows)]
        from_sib = pair_swap_halves(slabs, name="grad_pair_swap_" + tag)
        pairs = [pair_add(g, r, self.csel, name=f"grad_pair_add_{tag}{i}") for i, (g, r) in enumerate(zip(slabs, from_sib))]
        self.p32[tag] = [p[0] for p in pairs]
        return scatter_rider([p[1] for p in pairs])

    def bwd_done(self, tag, outs):
        self.mine[tag] = [chip_add(p, r, self.msel, name=f"grad_chip_add_{tag}{i}")
                          for i, (p, r) in enumerate(zip(self.p32[tag], outs))]

    def finish(self, d_attn_in, d_attn_out):
        self.bwd_done("at", run_rider(self.grads("at", d_attn_in, d_attn_out), name="grad_scatter_at"))
        order = ["at", "s", "f0", "f1"]
        mine = [m for t in order for m in self.mine[t]]
        theirs = pair_share(mine, name="grad_pair_share")
        full = [jnp.where(self.ci == 0, jnp.stack([a, b]), jnp.stack([b, a])).reshape((-1, a.shape[-1]))
                for a, b in zip(mine, theirs)]
        ai, ao, si, so, fi0, fo0, fi1, fo1 = full
        return {"attn_w_in": ai[None], "attn_w_out": ao[None], "ssm_w_in": si[None], "ssm_w_out": so[None],
                "ffn_w_in": jnp.stack([fi0, fi1]), "ffn_w_out": jnp.stack([fo0, fo1])}


def _adamw(w, g, m, v, name):
    shp = w.shape
    two = lambda a: a.reshape((-1, shp[-1]))
    outs = [(shp[-1], F32)] * 3
    d, nm, nv = rowwise(f_adamw, [two(w), two(g), two(m), two(v)], [], outs, name="adamw_" + name)
    return d.reshape(shp), nm.reshape(shp), nv.reshape(shp)


def kernel(x, norm_mix, norm_ffn, attn_w_in, attn_w_out, relpos_table, q_norm_a, k_norm_a, q_norm_b, k_norm_b, sinks, ssm_w_in, ssm_conv_w, ssm_conv_b, ssm_dt_bias, ssm_a_log, ssm_d, ssm_norm, ssm_w_out, ffn_w_in, ffn_conv_w, ffn_conv_b, ffn_w_out, loss_target, m_norm_mix, m_norm_ffn, m_attn_w_in, m_attn_w_out, m_relpos_table, m_q_norm_a, m_k_norm_a, m_q_norm_b, m_k_norm_b, m_sinks, m_ssm_w_in, m_ssm_conv_w, m_ssm_conv_b, m_ssm_dt_bias, m_ssm_a_log, m_ssm_d, m_ssm_norm, m_ssm_w_out, m_ffn_w_in, m_ffn_conv_w, m_ffn_conv_b, m_ffn_w_out, v_norm_mix, v_norm_ffn, v_attn_w_in, v_attn_w_out, v_relpos_table, v_q_norm_a, v_k_norm_a, v_q_norm_b, v_k_norm_b, v_sinks, v_ssm_w_in, v_ssm_conv_w, v_ssm_conv_b, v_ssm_dt_bias, v_ssm_a_log, v_ssm_d, v_ssm_norm, v_ssm_w_out, v_ffn_w_in, v_ffn_conv_w, v_ffn_conv_b, v_ffn_w_out):
    d = dict(zip(ARGS, (x, norm_mix, norm_ffn, attn_w_in, attn_w_out, relpos_table, q_norm_a, k_norm_a, q_norm_b, k_norm_b, sinks, ssm_w_in, ssm_conv_w, ssm_conv_b, ssm_dt_bias, ssm_a_log, ssm_d, ssm_norm, ssm_w_out, ffn_w_in, ffn_conv_w, ffn_conv_b, ffn_w_out, loss_target, m_norm_mix, m_norm_ffn, m_attn_w_in, m_attn_w_out, m_relpos_table, m_q_norm_a, m_k_norm_a, m_q_norm_b, m_k_norm_b, m_sinks, m_ssm_w_in, m_ssm_conv_w, m_ssm_conv_b, m_ssm_dt_bias, m_ssm_a_log, m_ssm_d, m_ssm_norm, m_ssm_w_out, m_ffn_w_in, m_ffn_conv_w, m_ffn_conv_b, m_ffn_w_out, v_norm_mix, v_norm_ffn, v_attn_w_in, v_attn_w_out, v_relpos_table, v_q_norm_a, v_k_norm_a, v_q_norm_b, v_k_norm_b, v_sinks, v_ssm_w_in, v_ssm_conv_w, v_ssm_conv_b, v_ssm_dt_bias, v_ssm_a_log, v_ssm_d, v_ssm_norm, v_ssm_w_out, v_ffn_w_in, v_ffn_conv_w, v_ffn_conv_b, v_ffn_w_out)))
    xi, yi, ci = _pos()
    me = 2 * xi + yi
    CC = D_INNER + 2 * SSM_GROUPS * SSM_STATE
    c4, f4 = CC // N_CHIPS, D_FF // N_CHIPS

    P = {k: d[k] for k in ["norm_mix", "norm_ffn", "q_norm_a", "k_norm_a", "q_norm_b", "k_norm_b", "sinks",
                           "ssm_dt_bias", "ssm_a_log", "ssm_d", "ffn_conv_b"]}
    P["relpos_table"] = d["relpos_table"][0]
    comm = MeshComm(d, xi, yi, ci)
    W = comm.first_weights()
    lpart, dx, gW, gP = local_step(d["x"][0], d["loss_target"][0], W, P, comm)
    loss = lax.psum(lpart[0, 0], ("x", "y", "c"))
    grads = comm.finish(gW["attn_in"], gW["attn_out"])

    flat = jnp.concatenate([gP[k].reshape(-1) for k in SMALL_ORDER])
    flat = jnp.pad(flat, (0, SMALL_ROWS * LANES - flat.shape[0])).reshape(SMALL_ROWS, LANES)
    tot = sum_slots(gather_all(flat, name="small_gather"), name="small_sum").reshape(-1)
    off = 0
    for k in SMALL_ORDER:
        n = int(np.prod(gP[k].shape))
        g = tot[off:off + n].reshape(gP[k].shape)
        off += n
        if k == "ssm_conv_w":
            g = lax.dynamic_slice_in_dim(g, me * c4, c4, axis=1)[None]
        elif k == "ssm_conv_b":
            g = lax.dynamic_slice_in_dim(g, me * c4, c4, axis=1)
        elif k == "ssm_norm":
            g = lax.dynamic_slice_in_dim(g, me * (D_INNER // N_CHIPS), D_INNER // N_CHIPS, axis=1)
        elif k == "ffn_conv_w":
            g = lax.dynamic_slice_in_dim(g, me * f4, f4, axis=2)
        elif k == "relpos_table":
            g = g[None]
        grads[k] = g

    deltas, new_m, new_v = {}, {}, {}
    for k in WEIGHTS:
        deltas[k], new_m[k], new_v[k] = _adamw(d[k], grads[k], d["m_" + k], d["v_" + k], k)
    return (loss, dx[None], *[grads[k] for k in WEIGHTS], *[deltas[k] for k in WEIGHTS],
            *[new_m[k] for k in WEIGHTS], *[new_v[k] for k in WEIGHTS])
```

```python
import functools

import numpy as np
import jax
import jax.numpy as jnp
from jax import lax
from jax.experimental import pallas as pl
from jax.experimental.pallas import tpu as pltpu

F32 = jnp.float32
BF16 = jnp.bfloat16
HI = lax.Precision.HIGHEST

D_MODEL = 1024
CHUNK = 64
EPS = 1e-6
HEAD_DIM = 64
N_HEADS = 8
A_PREV = 8
B_PREV = 2
MAX_REL = 256
D_INNER = 2048
SSM_HEADS = 32
SSM_GROUPS = 4
SSM_STATE = 128
SSM_CONV = 4
D_FF = 2816
FFN_CONV = 3
LANES = 128
SUBLANES = 8
VMEM_LIMIT = 56 * 1024 * 1024
SSD_L = 128

ADAM_LR = 0.001
ADAM_B1 = 0.9
ADAM_B2 = 0.999
ADAM_EPS = 1e-08
ADAM_WD = 0.01
ADAM_STEP = 10

MESH = pl.DeviceIdType.MESH


def _params(*sem):
    return pltpu.CompilerParams(dimension_semantics=sem, vmem_limit_bytes=VMEM_LIMIT)


def _pick(n, want):
    if n <= want:
        return n
    t = (want // LANES) * LANES
    while t >= LANES:
        if n % t == 0:
            return t
        t -= LANES
    return n


MM_ROWS = 512
MM_COLS = 1536
MM_RED = 2048


def matmul(a, b, *, mode, name, out_dtype=F32, residual=None):
    dims = {"nn": (((1,), (0,)), ((), ())), "nt": (((1,), (1,)), ((), ())), "tn": (((0,), (0,)), ((), ()))}[mode]
    if mode == "tn":
        assert residual is None and out_dtype == F32
        (K, M), (K2, N) = a.shape, b.shape
        assert K == K2, (a.shape, b.shape)
        tm, tn, tk = _pick(M, MM_COLS), _pick(N, MM_COLS), _pick(K, MM_RED)

        def body(a_ref, b_ref, o_ref):
            k = pl.program_id(2)
            p = lax.dot_general(a_ref[...].astype(BF16), b_ref[...].astype(BF16), dims, preferred_element_type=F32)

            @pl.when(k == 0)
            def _():
                o_ref[...] = p

            @pl.when(k != 0)
            def _():
                o_ref[...] += p

        return pl.pallas_call(
            body, name=name, grid=(M // tm, N // tn, K // tk),
            in_specs=[pl.BlockSpec((tk, tm), lambda i, j, k: (k, i)), pl.BlockSpec((tk, tn), lambda i, j, k: (k, j))],
            out_specs=pl.BlockSpec((tm, tn), lambda i, j, k: (i, j)),
            out_shape=jax.ShapeDtypeStruct((M, N), F32),
            compiler_params=_params("parallel", "parallel", "arbitrary"),
        )(a, b)

    if mode == "nn":
        (M, K), (K2, N) = a.shape, b.shape
    else:
        (M, K), (N, K2) = a.shape, b.shape
    assert K == K2, (a.shape, b.shape, mode)
    tm, tn = _pick(M, MM_ROWS), _pick(N, MM_COLS)

    def body(*refs):
        a_ref, b_ref = refs[:2]
        o_ref = refs[-1]
        r = lax.dot_general(a_ref[...].astype(BF16), b_ref[...].astype(BF16), dims, preferred_element_type=F32)
        if residual is not None:
            r = r + refs[2][...].astype(F32)
        o_ref[...] = r.astype(o_ref.dtype)

    a_spec = pl.BlockSpec((tm, K), lambda j, i: (i, 0))
    b_spec = pl.BlockSpec((K, tn), lambda j, i: (0, j)) if mode == "nn" else pl.BlockSpec((tn, K), lambda j, i: (j, 0))
    o_spec = pl.BlockSpec((tm, tn), lambda j, i: (i, j))
    in_specs = [a_spec, b_spec] + ([o_spec] if residual is not None else [])
    args = (a, b) + ((residual,) if residual is not None else ())
    return pl.pallas_call(
        body, name=name, grid=(N // tn, M // tm),
        in_specs=in_specs, out_specs=o_spec,
        out_shape=jax.ShapeDtypeStruct((M, N), out_dtype),
        compiler_params=_params("parallel", "parallel"),
    )(*args)


def rowwise(f, rows, params, outs, *, name, tm=256):
    S = rows[0].shape[0]
    tm = _row_tile(S, tm)
    nr, npar = len(rows), len(params)

    def body(*refs):
        vals = [r[...].astype(F32) for r in refs[:nr + npar]]
        res = f(*vals)
        for o_ref, r in zip(refs[nr + npar:], res):
            o_ref[...] = r.astype(o_ref.dtype)

    in_specs = [pl.BlockSpec((tm, r.shape[1]), lambda i: (i, 0)) for r in rows]
    in_specs += [pl.BlockSpec(p.shape, lambda i: (0, 0)) for p in params]
    out_specs = [pl.BlockSpec((tm, c), lambda i: (i, 0)) for c, _ in outs]
    out_shape = [jax.ShapeDtypeStruct((S, c), dt) for c, dt in outs]
    return pl.pallas_call(body, name=name, grid=(S // tm,), in_specs=in_specs, out_specs=out_specs,
                          out_shape=out_shape, compiler_params=_params("parallel"))(*rows, *params)


def rowwise_vjp(f, rows, params, cots, drow, dpar, *, name, tm=256, cot_skip=None):
    S = rows[0].shape[0]
    tm = _row_tile(S, tm)
    nr, npar, nc = len(rows), len(params), len(cots)
    skip = [0] * nc if cot_skip is None else [s // tm for s in cot_skip]
    assert cot_skip is None or all(s % tm == 0 for s in cot_skip)

    def body(*refs):
        vals = [r[...].astype(F32) for r in refs[:nr + npar]]
        cvals = [r[...].astype(F32) for r in refs[nr + npar:nr + npar + nc]]
        o_refs = refs[nr + npar + nc:]
        want = [ri for ri, _ in drow] + [nr + pi for pi in dpar]

        def f_want(*d):
            full = list(vals)
            for k, v in zip(want, d):
                full[k] = v
            return f(*full)

        _, vjp = jax.vjp(f_want, *[vals[k] for k in want])
        grads = vjp(tuple(cvals))
        for o_ref, g in zip(o_refs[:len(drow)], grads):
            o_ref[...] = g.astype(o_ref.dtype)
        first = pl.program_id(0) == 0
        for o_ref, g in zip(o_refs[len(drow):], grads[len(drow):]):
            g = g.astype(F32)

            @pl.when(first)
            def _(o_ref=o_ref, g=g):
                o_ref[...] = g

            @pl.when(jnp.logical_not(first))
            def _(o_ref=o_ref, g=g):
                o_ref[...] += g

    in_specs = [pl.BlockSpec((tm, r.shape[1]), lambda i: (i, 0)) for r in rows]
    in_specs += [pl.BlockSpec(p.shape, lambda i: (0, 0)) for p in params]
    in_specs += [pl.BlockSpec((tm, c.shape[1]), lambda i, s=s: (i + s, 0)) for c, s in zip(cots, skip)]
    out_specs = [pl.BlockSpec((tm, rows[ri].shape[1]), lambda i: (i, 0)) for ri, _ in drow]
    out_specs += [pl.BlockSpec(params[pi].shape, lambda i: (0, 0)) for pi in dpar]
    out_shape = [jax.ShapeDtypeStruct(rows[ri].shape, dt) for ri, dt in drow]
    out_shape += [jax.ShapeDtypeStruct(params[pi].shape, F32) for pi in dpar]
    return pl.pallas_call(body, name=name, grid=(S // tm,), in_specs=in_specs, out_specs=out_specs,
                          out_shape=out_shape, compiler_params=_params("arbitrary"))(*rows, *params, *cots)


HALO = 2 * SUBLANES


def dwconv_fwd(x, w, b, post, extra, outs, *, name, tm=256):
    S, C = x.shape
    K = w.shape[0]
    tm = min(tm, S)
    hb = tm // HALO
    ne = len(extra)

    def body(*refs):
        x_ref, halo_ref, w_ref, b_ref = refs[:4]
        e_refs = refs[4:4 + ne]
        o_refs = refs[4 + ne:4 + ne + len(outs)]
        buf = refs[-1]
        i = pl.program_id(0)
        buf[0:HALO, :] = jnp.where(i == 0, 0.0, halo_ref[...].astype(F32))
        buf[HALO:HALO + tm, :] = x_ref[...].astype(F32)
        for c0 in range(0, C, LANES):
            cs = slice(c0, c0 + LANES)
            acc = jnp.broadcast_to(b_ref[:, cs], (tm, LANES))
            for k in range(K):
                acc = acc + w_ref[k:k + 1, cs] * buf[pl.ds(HALO - (K - 1) + k, tm), cs]
            for o_ref, r in zip(o_refs, post(acc, *[e[:, cs].astype(F32) for e in e_refs])):
                o_ref[:, cs] = r.astype(o_ref.dtype)

    row = pl.BlockSpec((tm, C), lambda i: (i, 0))
    return pl.pallas_call(
        body, name=name, grid=(S // tm,),
        in_specs=[row,
                  pl.BlockSpec((HALO, C), lambda i: (jnp.maximum(i * hb - 1, 0), 0)),
                  pl.BlockSpec((K, C), lambda i: (0, 0)),
                  pl.BlockSpec((1, C), lambda i: (0, 0))] + [row] * ne,
        out_specs=[row] * len(outs),
        out_shape=[jax.ShapeDtypeStruct((S, C), dt) for dt in outs],
        scratch_shapes=[pltpu.VMEM((HALO + tm, C), F32)],
        compiler_params=_params("parallel"),
    )(x, x, w, b, *extra)


def dwconv_bwd(x, w, srcs, dy_fn, extra_outs, *, name, tm=256):
    S, C = x.shape
    K = w.shape[0]
    tm = min(tm, S)
    hb = tm // HALO
    n = S // tm
    groups = [s if isinstance(s, tuple) else (s,) for s in srcs]
    flat = [a for g in groups for a in g]
    nf = len(flat)

    def body(*refs):
        x_ref, xh_ref, w_ref = refs[:3]
        dx_ref, dw_ref, db_ref = refs[3 + 2 * nf:6 + 2 * nf]
        e_refs = refs[6 + 2 * nf:6 + 2 * nf + len(extra_outs)]
        bx, bd = refs[-2:]

        def strips(first, c0):
            out, at = [], first
            for g in groups:
                off = 0
                for a in g:
                    if off <= c0 < off + a.shape[1]:
                        out.append(refs[at][:, c0 - off:c0 - off + LANES].astype(F32))
                    off += a.shape[1]
                    at += 1
            return out

        i = pl.program_id(0)
        bx[0:HALO, :] = jnp.where(i == 0, 0.0, xh_ref[...].astype(F32))
        bx[HALO:HALO + tm, :] = x_ref[...].astype(F32)

        @pl.when(i == 0)
        def _():
            dw_ref[...] = jnp.zeros_like(dw_ref)
            db_ref[...] = jnp.zeros_like(db_ref)

        for c0 in range(0, C, LANES):
            cs = slice(c0, c0 + LANES)
            res = dy_fn(*strips(3, c0))
            dyv = res[0]
            for e_ref, r in zip(e_refs, res[1:]):
                e_ref[:, cs] = r.astype(e_ref.dtype)
            bd[0:tm, cs] = dyv
            bd[tm:tm + HALO, cs] = jnp.where(i == n - 1, 0.0, dy_fn(*strips(3 + nf, c0))[0])
            acc = jnp.zeros((tm, LANES), F32)
            for k in range(K):
                acc = acc + w_ref[k:k + 1, cs] * bd[pl.ds((K - 1) - k, tm), cs]
            dx_ref[:, cs] = acc.astype(dx_ref.dtype)
            for k in range(K):
                dw_ref[k:k + 1, cs] += jnp.sum(dyv * bx[pl.ds(HALO - (K - 1) + k, tm), cs], axis=0, keepdims=True)
            db_ref[:, cs] += jnp.sum(dyv, axis=0, keepdims=True)

    row = lambda c: pl.BlockSpec((tm, c), lambda i: (i, 0))
    nxt = lambda c: pl.BlockSpec((HALO, c), lambda i: (jnp.minimum((i + 1) * hb, S // HALO - 1), 0))
    return pl.pallas_call(
        body, name=name, grid=(n,),
        in_specs=[row(C), pl.BlockSpec((HALO, C), lambda i: (jnp.maximum(i * hb - 1, 0), 0)),
                  pl.BlockSpec((K, C), lambda i: (0, 0))]
                 + [row(a.shape[1]) for a in flat] + [nxt(a.shape[1]) for a in flat],
        out_specs=[row(C), pl.BlockSpec((K, C), lambda i: (0, 0)), pl.BlockSpec((1, C), lambda i: (0, 0))]
                  + [row(C)] * len(extra_outs),
        out_shape=[jax.ShapeDtypeStruct((S, C), BF16), jax.ShapeDtypeStruct((K, C), F32),
                   jax.ShapeDtypeStruct((1, C), F32)] + [jax.ShapeDtypeStruct((S, C), dt) for dt in extra_outs],
        scratch_shapes=[pltpu.VMEM((HALO + tm, C), F32), pltpu.VMEM((tm + HALO, C), F32)],
        compiler_params=_params("arbitrary"),
    )(x, x, w, *flat, *flat)


def _sigmoid(x):
    return 0.5 * jnp.tanh(0.5 * x) + 0.5


def _silu(x):
    return x * _sigmoid(x)


def _dsilu(x):
    s = _sigmoid(x)
    return s * (1.0 + x * (1.0 - s))


def f_rmsnorm(x, g):
    return (x * lax.rsqrt(jnp.mean(x * x, axis=-1, keepdims=True) + EPS) * g,)


SEL = lax.Precision.HIGH


def _group_norm(x, bd, width):
    ms = jnp.dot(x * x, bd, precision=SEL, preferred_element_type=F32) * (1.0 / width)
    return x * lax.rsqrt(ms + EPS)


def f_qknorm(qkv, gqa, gka, gqb, gkb, bd512, bd128, fold, expand):
    dq = N_HEADS * HEAD_DIM
    qa, ka, va, qb = (qkv[:, i * dq:(i + 1) * dq] for i in range(4))
    kb = qkv[:, 4 * dq:4 * dq + LANES]
    vb = qkv[:, 4 * dq + LANES:4 * dq + 2 * LANES]
    tile8 = lambda g: jnp.dot(g, fold, precision=HI, preferred_element_type=F32)
    qa = _group_norm(qa, bd512, HEAD_DIM) * tile8(gqa)
    ka = _group_norm(ka, bd512, HEAD_DIM) * tile8(gka)
    qb = _group_norm(qb, bd512, HEAD_DIM) * tile8(gqb)
    kb = _group_norm(kb, bd128, HEAD_DIM) * tile8(gkb)[:, :LANES]
    kb = jnp.dot(kb, expand, precision=SEL, preferred_element_type=F32)
    vb = jnp.dot(vb, expand, precision=SEL, preferred_element_type=F32)
    return qa, ka, va, qb, kb, vb


def f_gate_norm(y, z, nw):
    v = y * _silu(z)
    gw = D_INNER // SSM_GROUPS
    parts = []
    for g in range(SSM_GROUPS):
        vg = v[:, g * gw:(g + 1) * gw]
        parts.append(vg * lax.rsqrt(jnp.mean(vg * vg, axis=-1, keepdims=True) + EPS))
    return (jnp.concatenate(parts, axis=-1) * nw,)


ATT_TQ = 256
_NT = (((1,), (1,)), ((), ()))
_TN = (((0,), (0,)), ((), ()))


def _stack_heads(t, head0):
    return jnp.concatenate([jnp.where(head0, t, 0.0), jnp.where(head0, 0.0, t)], axis=0).astype(BF16)


def _attn_probs(qk, bias, valid, snk):
    s = qk * (HEAD_DIM ** -0.5) + bias
    s = jnp.where(valid, s, -jnp.inf)
    m = jnp.max(s, axis=1, keepdims=True)
    if snk is not None:
        m = jnp.maximum(m, snk)
    e = jnp.exp(s - m)
    den = jnp.sum(e, axis=1, keepdims=True)
    if snk is None:
        return e / den, None
    es = jnp.exp(snk - m)
    den = den + es
    return e / den, es / den


def widen_bias(bias, n_prev, nj):
    band = (n_prev + 1) * CHUNK
    wk = (nj + n_prev) * CHUNK
    rows = [jnp.pad(bias, ((0, 0), (0, 0), (j * CHUNK, wk - band - j * CHUNK)), constant_values=-jnp.inf)
            for j in range(nj)]
    return jnp.concatenate(rows, axis=1)


def fold_bias(dbw, n_prev, nj):
    band = (n_prev + 1) * CHUNK
    acc = dbw[:, :CHUNK, :band]
    for j in range(1, nj):
        acc = acc + dbw[:, j * CHUNK:(j + 1) * CHUNK, j * CHUNK:j * CHUNK + band]
    return acc


def attn_fwd(q, k, v, bias_w, sinks, *, n_prev, name, rider=None):
    S = q.shape[0]
    pad = n_prev * CHUNK
    tq = min(ATT_TQ, S)
    wk = tq + pad
    assert bias_w.shape == (N_HEADS, tq, wk), bias_w.shape
    has_sink = sinks is not None

    r_in, r_out, r_shapes, r_sems, r_args = _rider_parts(rider)
    n_own = 5 if has_sink else 4
    n_p, n_i = N_HEADS // 2, S // tq

    def body(*refs):
        q_ref, k_ref, v_ref, bias_ref = refs[:4]
        sink_ref = refs[4] if has_sink else None
        o_ref = refs[n_own + len(r_in)]
        if rider is not None:
            p_id, i_id = pl.program_id(0), pl.program_id(1)
            _ride(rider, refs[n_own:n_own + len(r_in)], refs[n_own + len(r_in) + 1:n_own + len(r_in) + 1 + len(r_out)],
                  refs[n_own + len(r_in) + 1 + len(r_out):],
                  jnp.logical_and(p_id == 0, i_id == 0), jnp.logical_and(p_id == n_p - 1, i_id == 0),
                  jnp.logical_and(p_id == n_p - 1, i_id == n_i - 1))
        start = pl.multiple_of(pl.program_id(1) * tq, tq)
        head0 = lax.broadcasted_iota(jnp.int32, (1, LANES), 1) < HEAD_DIM
        valid = lax.broadcasted_iota(jnp.int32, (1, wk), 1) + start >= pad
        kb = k_ref[pl.ds(start, wk), :]
        vb = v_ref[pl.ds(start, wk), :]
        qk = lax.dot_general(_stack_heads(q_ref[...].astype(F32), head0), kb, _NT, preferred_element_type=F32)
        ps = []
        for r in range(2):
            snk = sink_ref[0, r:r + 1, 0:1] if has_sink else None
            ps.append(_attn_probs(qk[r * tq:(r + 1) * tq, :], bias_ref[r], valid, snk)[0].astype(BF16))
        o2 = jnp.dot(jnp.concatenate(ps, axis=0), vb, preferred_element_type=F32)
        o_ref[...] = jnp.where(head0, o2[:tq, :], o2[tq:, :]).astype(o_ref.dtype)

    in_specs = [pl.BlockSpec((tq, LANES), lambda p, i: (i, p)),
                pl.BlockSpec((pad + S, LANES), lambda p, i: (0, p)),
                pl.BlockSpec((pad + S, LANES), lambda p, i: (0, p)),
                pl.BlockSpec((2, tq, wk), lambda p, i: (p, 0, 0))]
    args = [q, k, v, bias_w]
    if has_sink:
        in_specs.append(pl.BlockSpec((1, 2, LANES), lambda p, i: (p, 0, 0)))
        args.append(sinks)
    res = pl.pallas_call(
        body, name=name, grid=(n_p, n_i), in_specs=in_specs + r_in,
        out_specs=[pl.BlockSpec((tq, LANES), lambda p, i: (i, p))] + r_out,
        out_shape=[jax.ShapeDtypeStruct((S, N_HEADS * HEAD_DIM), BF16)] + r_shapes,
        scratch_shapes=r_sems,
        compiler_params=pltpu.CompilerParams(dimension_semantics=("arbitrary", "arbitrary"), vmem_limit_bytes=VMEM_LIMIT,
                                             has_side_effects=rider is not None),
    )(*args, *r_args)
    return res[0], res[1:]


def attn_bwd(q, k, v, do, bias_w, sinks, *, n_prev, name, rider=None):
    S = q.shape[0]
    pad = n_prev * CHUNK
    tq = min(ATT_TQ, S)
    wk = tq + pad
    assert bias_w.shape == (N_HEADS, tq, wk), bias_w.shape
    has_sink = sinks is not None
    scale = HEAD_DIM ** -0.5

    r_in, r_out, r_shapes, r_sems, r_args = _rider_parts(rider)
    n_own_in = 6 if has_sink else 5
    n_own_out = 5 if has_sink else 4
    n_p, n_i = N_HEADS // 2, S // tq

    def body(*refs):
        q_ref, k_ref, v_ref, do_ref, bias_ref = refs[:5]
        sink_ref = refs[5] if has_sink else None
        o0 = n_own_in + len(r_in)
        dq_ref, dk_ref, dv_ref, db_ref = refs[o0:o0 + 4]
        dsk_ref = refs[o0 + 4] if has_sink else None
        i = pl.program_id(1)
        if rider is not None:
            p_id = pl.program_id(0)
            _ride(rider, refs[n_own_in:o0], refs[o0 + n_own_out:o0 + n_own_out + len(r_out)],
                  refs[o0 + n_own_out + len(r_out):],
                  jnp.logical_and(p_id == 0, i == 0), jnp.logical_and(p_id == n_p // 2, i == 0),
                  jnp.logical_and(p_id == n_p - 1, i == n_i - 1))

        @pl.when(i == 0)
        def _():
            dk_ref[...] = jnp.zeros_like(dk_ref)
            dv_ref[...] = jnp.zeros_like(dv_ref)
            db_ref[...] = jnp.zeros_like(db_ref)
            if has_sink:
                dsk_ref[...] = jnp.zeros_like(dsk_ref)

        start = pl.multiple_of(i * tq, tq)
        head0 = lax.broadcasted_iota(jnp.int32, (1, LANES), 1) < HEAD_DIM
        valid = lax.broadcasted_iota(jnp.int32, (1, wk), 1) + start >= pad
        kb = k_ref[pl.ds(start, wk), :]
        vb = v_ref[pl.ds(start, wk), :]
        q2 = _stack_heads(q_ref[...].astype(F32), head0)
        do2 = _stack_heads(do_ref[...].astype(F32), head0)
        qk = lax.dot_general(q2, kb, _NT, preferred_element_type=F32)
        dp2 = lax.dot_general(do2, vb, _NT, preferred_element_type=F32)
        pbs, dss = [], []
        for r in range(2):
            rows = slice(r * tq, (r + 1) * tq)
            snk = sink_ref[0, r:r + 1, 0:1] if has_sink else None
            p, ps = _attn_probs(qk[rows, :], bias_ref[r], valid, snk)
            dp = dp2[rows, :]
            delta = jnp.sum(p * dp, axis=1, keepdims=True)
            ds = p * (dp - delta)
            db_ref[r] += ds
            if has_sink:
                dsk = -jnp.sum(ps * delta, axis=0, keepdims=True)
                dsk_ref[0, r:r + 1, :] += jnp.broadcast_to(dsk, (1, LANES))
            pbs.append(p.astype(BF16))
            dss.append(ds.astype(BF16))
        ds2 = jnp.concatenate(dss, axis=0)
        dq2 = jnp.dot(ds2, kb, preferred_element_type=F32) * scale
        dq_ref[...] = jnp.where(head0, dq2[:tq, :], dq2[tq:, :])
        dk_ref[pl.ds(start, wk), :] += lax.dot_general(ds2, q2, _TN, preferred_element_type=F32) * scale
        dv_ref[pl.ds(start, wk), :] += lax.dot_general(jnp.concatenate(pbs, axis=0), do2, _TN,
                                                       preferred_element_type=F32)

    row_spec = pl.BlockSpec((tq, LANES), lambda p, i: (i, p))
    kv_spec = pl.BlockSpec((pad + S, LANES), lambda p, i: (0, p))
    bias_spec = pl.BlockSpec((2, tq, wk), lambda p, i: (p, 0, 0))
    sink_spec = pl.BlockSpec((1, 2, LANES), lambda p, i: (p, 0, 0))
    in_specs = [row_spec, kv_spec, kv_spec, row_spec, bias_spec]
    args = [q, k, v, do, bias_w]
    out_specs = [row_spec, kv_spec, kv_spec, bias_spec]
    W = N_HEADS * HEAD_DIM
    out_shape = [jax.ShapeDtypeStruct((S, W), F32), jax.ShapeDtypeStruct((pad + S, W), F32),
                 jax.ShapeDtypeStruct((pad + S, W), F32), jax.ShapeDtypeStruct((N_HEADS, tq, wk), F32)]
    if has_sink:
        in_specs.append(sink_spec)
        args.append(sinks)
        out_specs.append(sink_spec)
        out_shape.append(jax.ShapeDtypeStruct((N_HEADS // 2, 2, LANES), F32))
    res = pl.pallas_call(
        body, name=name, grid=(n_p, n_i), in_specs=in_specs + r_in, out_specs=out_specs + r_out,
        out_shape=out_shape + r_shapes, scratch_shapes=r_sems,
        compiler_params=pltpu.CompilerParams(dimension_semantics=("arbitrary", "arbitrary"), vmem_limit_bytes=VMEM_LIMIT,
                                             has_side_effects=rider is not None),
    )(*args, *r_args)
    return res[:n_own_out], res[n_own_out:]


HP = SSM_HEADS // 2
PAIRS_PER_GROUP = HP // SSM_GROUPS
HEADS_PER_GROUP = SSM_HEADS // SSM_GROUPS
GW = HEADS_PER_GROUP * 64


def _ssd_dt(dtraw, dtb, A, tril):
    lane = lax.broadcasted_iota(jnp.int32, (1, LANES), 1)
    u = dtraw + dtb
    eu = jnp.exp(-jnp.abs(u))
    w1 = 1.0 + eu
    l1p = jnp.where(w1 == 1.0, eu, jnp.log(w1) * eu / jnp.where(w1 == 1.0, 1.0, w1 - 1.0))
    dt = jnp.where(lane < SSM_HEADS, jnp.maximum(u, 0.0) + l1p, 0.0)
    acs = jnp.dot(tril, dt * A, precision=HI, preferred_element_type=F32)
    return u, dt, acs


def _head_expander():
    hw = D_INNER // SSM_HEADS
    return (np.arange(LANES)[:, None] == np.arange(D_INNER)[None, :] // hw).astype(np.float32)


def _select_dot(t, sel):
    hi = t.astype(BF16)
    lo = (t - hi.astype(F32)).astype(BF16)
    return jnp.dot(hi, sel, preferred_element_type=F32) + jnp.dot(lo, sel, preferred_element_type=F32)


def ssd_fwd(xbc, dtraw, dtb, A, dexp, *, name):
    S = xbc.shape[0]
    L = min(SSD_L, S)
    nc = S // L
    N = SSM_STATE
    e_mat = jnp.asarray(_head_expander(), dtype=BF16)

    def body(xs_ref, b_ref, c_ref, dtr_ref, dtb_ref, a_ref, d_ref, e_ref, y_ref, st_out_ref, st_ref, xw_ref):
        c = pl.program_id(0)

        @pl.when(c == 0)
        def _():
            st_ref[...] = jnp.zeros_like(st_ref)

        st_out_ref[0] = st_ref[...]
        ri = lax.broadcasted_iota(jnp.int32, (L, L), 0)
        ci = lax.broadcasted_iota(jnp.int32, (L, L), 1)
        trilb = ri >= ci
        head0 = lax.broadcasted_iota(jnp.int32, (1, LANES), 1) < 64
        _, dt, acs = _ssd_dt(dtr_ref[...], dtb_ref[...], a_ref[...], trilb.astype(F32))
        acsT = acs.T
        last = acs[L - 1:L, :]
        expand = lambda t: _select_dot(t, e_ref[...])
        dte, eae, wte = expand(dt), expand(jnp.exp(acs)), expand(jnp.exp(last - acs) * dt)
        lasts = [last[:, h:h + 1] for h in range(SSM_HEADS)]
        for g in range(SSM_GROUPS):
            Bg = b_ref[:, g * N:(g + 1) * N].astype(BF16)
            Cg = c_ref[:, g * N:(g + 1) * N].astype(BF16)
            CB = lax.dot_general(Cg, Bg, _NT, preferred_element_type=F32)
            Z = lax.dot_general(Cg, st_ref[g * GW:(g + 1) * GW, :].astype(BF16), _NT, preferred_element_type=F32)
            for q in range(PAIRS_PER_GROUP):
                hp = g * PAIRS_PER_GROUP + q
                sl = slice(hp * LANES, (hp + 1) * LANES)
                xs = xs_ref[:, sl]
                xd = xs * dte[:, sl]
                ms, xh = [], []
                for r in range(2):
                    h = 2 * hp + r
                    dec = jnp.exp(jnp.where(trilb, acs[:, h:h + 1] - acsT[h:h + 1, :], -jnp.inf))
                    ms.append((CB * dec).astype(BF16))
                    xh.append(jnp.where(head0 if r == 0 else jnp.logical_not(head0), xd, 0.0).astype(BF16))
                yi = jnp.dot(jnp.concatenate(ms, axis=1), jnp.concatenate(xh, axis=0), preferred_element_type=F32)
                y_ref[:, sl] = yi + Z[:, q * LANES:(q + 1) * LANES] * eae[:, sl] + d_ref[:, sl] * xs
                xw_ref[:, sl] = (xs * wte[:, sl]).astype(BF16)
        for g in range(SSM_GROUPS):
            Bg = b_ref[:, g * N:(g + 1) * N].astype(BF16)
            sn = lax.dot_general(xw_ref[:, g * GW:(g + 1) * GW], Bg, _TN, preferred_element_type=F32)
            for k in range(HEADS_PER_GROUP):
                h = g * HEADS_PER_GROUP + k
                rows = slice(h * 64, (h + 1) * 64)
                st_ref[rows, :] = st_ref[rows, :] * jnp.exp(lasts[h]) + sn[k * 64:(k + 1) * 64, :]

    return pl.pallas_call(
        body, name=name, grid=(nc,),
        in_specs=[pl.BlockSpec((L, D_INNER), lambda c: (c, 0)),
                  pl.BlockSpec((L, SSM_GROUPS * N), lambda c: (c, D_INNER // (SSM_GROUPS * N))),
                  pl.BlockSpec((L, SSM_GROUPS * N), lambda c: (c, D_INNER // (SSM_GROUPS * N) + 1)),
                  pl.BlockSpec((L, LANES), lambda c: (c, 0)),
                  pl.BlockSpec((1, LANES), lambda c: (0, 0)),
                  pl.BlockSpec((1, LANES), lambda c: (0, 0)),
                  pl.BlockSpec((1, D_INNER), lambda c: (0, 0)),
                  pl.BlockSpec((LANES, D_INNER), lambda c: (0, 0))],
        out_specs=[pl.BlockSpec((L, D_INNER), lambda c: (c, 0)),
                   pl.BlockSpec((1, D_INNER, N), lambda c: (c, 0, 0))],
        out_shape=[jax.ShapeDtypeStruct((S, D_INNER), F32), jax.ShapeDtypeStruct((nc, D_INNER, N), F32)],
        scratch_shapes=[pltpu.VMEM((D_INNER, N), F32), pltpu.VMEM((L, D_INNER), BF16)],
        compiler_params=_params("arbitrary"),
    )(xbc, xbc, xbc, dtraw, dtb, A, dexp, e_mat)


def ssd_bwd(xbc, dtraw, dtb, A, dexp, states, dy, *, name, rider=None):
    S = xbc.shape[0]
    L = min(SSD_L, S)
    nc = S // L
    N = SSM_STATE
    e_np = _head_expander()
    e_mat, et_mat = jnp.asarray(e_np, dtype=BF16), jnp.asarray(e_np.T, dtype=BF16)

    r_in, r_out, r_shapes, r_sems, r_args = _rider_parts(rider)

    def body(*refs):
        xs_ref, b_ref, c_ref, dtr_ref, dtb_ref, a_ref, d_ref, e_ref, et_ref, st_in_ref, dy_ref = refs[:11]
        o0 = 11 + len(r_in)
        dxs_ref, db_ref, dc_ref, ddtr_ref, da_ref, ddtb_ref, dd_ref = refs[o0:o0 + 7]
        s0 = o0 + 7 + len(r_out)
        dst_ref, xw_ref, dz_ref, r_ref, dsr_ref, dsc_ref = refs[s0:s0 + 6]
        step = pl.program_id(0)
        if rider is not None:
            _ride(rider, refs[11:o0], refs[o0 + 7:s0], refs[s0 + 6:], step == 0, step == nc // 2, step == nc - 1)

        @pl.when(step == 0)
        def _():
            dst_ref[...] = jnp.zeros_like(dst_ref)
            dsr_ref[...] = jnp.zeros_like(dsr_ref)
            dsc_ref[...] = jnp.zeros_like(dsc_ref)
            da_ref[...] = jnp.zeros_like(da_ref)
            ddtb_ref[...] = jnp.zeros_like(ddtb_ref)
            dd_ref[...] = jnp.zeros_like(dd_ref)

        ri = lax.broadcasted_iota(jnp.int32, (L, L), 0)
        ci = lax.broadcasted_iota(jnp.int32, (L, L), 1)
        trilb = ri >= ci
        lane = lax.broadcasted_iota(jnp.int32, (1, LANES), 1)
        sub = lax.broadcasted_iota(jnp.int32, (LANES, 1), 0)
        head0 = lane < 64
        A = a_ref[...]
        u, dt, acs = _ssd_dt(dtr_ref[...], dtb_ref[...], A, trilb.astype(F32))
        acsT = acs.T
        last = acs[L - 1:L, :]
        elast = jnp.exp(last)
        er = jnp.exp(last - acs)
        wt = er * dt
        expand = lambda t: _select_dot(t, e_ref[...])
        dte, eae, wte = expand(dt), expand(jnp.exp(acs)), expand(wt)
        dlast = jnp.zeros((1, LANES), F32)
        dcbs = []
        for g in range(SSM_GROUPS):
            Bg = b_ref[:, g * N:(g + 1) * N].astype(BF16)
            Cg = c_ref[:, g * N:(g + 1) * N].astype(BF16)
            stg = st_in_ref[0, g * GW:(g + 1) * GW, :]
            dstg = dst_ref[g * GW:(g + 1) * GW, :]
            CB = lax.dot_general(Cg, Bg, _NT, preferred_element_type=F32)
            CBT = lax.dot_general(Bg, Cg, _NT, preferred_element_type=F32)
            Z = lax.dot_general(Cg, stg.astype(BF16), _NT, preferred_element_type=F32)
            U = lax.dot_general(Bg, dstg.astype(BF16), _NT, preferred_element_type=F32)
            dcb = jnp.zeros((L, L), F32)
            for q in range(PAIRS_PER_GROUP):
                hp = g * PAIRS_PER_GROUP + q
                sl = slice(hp * LANES, (hp + 1) * LANES)
                qs = slice(q * LANES, (q + 1) * LANES)
                xs = xs_ref[:, sl]
                dyp = dy_ref[:, sl]
                dtp, eap, wp, Dp = dte[:, sl], eae[:, sl], wte[:, sl], d_ref[:, sl]
                xd = xs * dtp
                dy2 = jnp.concatenate([jnp.where(head0, dyp, 0.0), jnp.where(head0, 0.0, dyp)], axis=0).astype(BF16)
                G2 = lax.dot_general(dy2, xd.astype(BF16), _NT, preferred_element_type=F32)
                mts = []
                for r in range(2):
                    h = 2 * hp + r
                    seg = acs[:, h:h + 1] - acsT[h:h + 1, :]
                    dec = jnp.exp(jnp.where(trilb, seg, -jnp.inf))
                    decT = jnp.exp(jnp.where(ri <= ci, -seg, -jnp.inf))
                    gd = G2[r * L:(r + 1) * L, :] * dec
                    dcb = dcb + gd
                    dseg = gd * CB
                    dsr_ref[:, h:h + 1] = jnp.sum(dseg, axis=1, keepdims=True)
                    dsc_ref[h:h + 1, :] = jnp.sum(dseg, axis=0, keepdims=True)
                    mts.append((CBT * decT).astype(BF16))
                dxd = jnp.dot(jnp.concatenate(mts, axis=1), dy2, preferred_element_type=F32)
                Up = U[:, qs]
                r_ref[0, :, sl] = dyp * Z[:, qs] * eap
                r_ref[1, :, sl] = dxd * xs
                r_ref[2, :, sl] = Up * xs
                dz_ref[:, sl] = (dyp * eap).astype(BF16)
                xw_ref[:, sl] = (xs * wp).astype(BF16)
                dxs_ref[:, sl] = dxd * dtp + Dp * dyp + Up * wp
                dd_ref[:, sl] += jnp.sum(dyp * xs, axis=0, keepdims=True)
            dcbs.append(dcb)
            t = dstg * stg
            for k in range(HEADS_PER_GROUP):
                dlast = dlast + jnp.where(lane == g * HEADS_PER_GROUP + k,
                                          jnp.sum(t[k * 64:(k + 1) * 64, :], keepdims=True), 0.0)
        fold = lambda k: _select_dot(r_ref[k], et_ref[...])
        r1, r2, dws = fold(0), fold(1), fold(2)
        dww = dws * wt
        ddt = r2 + dws * er
        dacs = r1 - dww + dsr_ref[...] - dsc_ref[...].T
        dlast = dlast * elast + jnp.sum(dww, axis=0, keepdims=True)
        lasts = [last[:, h:h + 1] for h in range(SSM_HEADS)]
        for g in range(SSM_GROUPS):
            Bg = b_ref[:, g * N:(g + 1) * N].astype(BF16)
            Cg = c_ref[:, g * N:(g + 1) * N].astype(BF16)
            gs = slice(g * GW, (g + 1) * GW)
            stb = st_in_ref[0, gs, :].astype(BF16)
            dstb = dst_ref[gs, :].astype(BF16)
            dcbb = dcbs[g].astype(BF16)
            dzg = dz_ref[:, gs]
            dc_ref[:, g * N:(g + 1) * N] = (jnp.dot(dzg, stb, preferred_element_type=F32)
                                            + jnp.dot(dcbb, Bg, preferred_element_type=F32))
            db_ref[:, g * N:(g + 1) * N] = (jnp.dot(xw_ref[:, gs], dstb, preferred_element_type=F32)
                                            + lax.dot_general(dcbb, Cg, _TN, preferred_element_type=F32))
            dsn = lax.dot_general(dzg, Cg, _TN, preferred_element_type=F32)
            for k in range(HEADS_PER_GROUP):
                h = g * HEADS_PER_GROUP + k
                rows = slice(h * 64, (h + 1) * 64)
                dst_ref[rows, :] = dst_ref[rows, :] * jnp.exp(lasts[h]) + dsn[k * 64:(k + 1) * 64, :]
        rowi = lax.broadcasted_iota(jnp.int32, (L, 1), 0)
        dacs = dacs + jnp.where(rowi == L - 1, dlast, 0.0)
        da = jnp.dot((ci >= ri).astype(F32), dacs, precision=HI, preferred_element_type=F32)
        ddt = ddt + da * A
        da_ref[...] += jnp.sum(da * dt, axis=0, keepdims=True)
        ddtr = jnp.where(lane < SSM_HEADS, ddt * _sigmoid(u), 0.0)
        ddtr_ref[...] = ddtr
        ddtb_ref[...] += jnp.sum(ddtr, axis=0, keepdims=True)

    rev = lambda c: nc - 1 - c
    gn = SSM_GROUPS * N
    res = pl.pallas_call(
        body, name=name, grid=(nc,),
        in_specs=[pl.BlockSpec((L, D_INNER), lambda c: (rev(c), 0)),
                  pl.BlockSpec((L, gn), lambda c: (rev(c), D_INNER // gn)),
                  pl.BlockSpec((L, gn), lambda c: (rev(c), D_INNER // gn + 1)),
                  pl.BlockSpec((L, LANES), lambda c: (rev(c), 0)),
                  pl.BlockSpec((1, LANES), lambda c: (0, 0)),
                  pl.BlockSpec((1, LANES), lambda c: (0, 0)),
                  pl.BlockSpec((1, D_INNER), lambda c: (0, 0)),
                  pl.BlockSpec((LANES, D_INNER), lambda c: (0, 0)),
                  pl.BlockSpec((D_INNER, LANES), lambda c: (0, 0)),
                  pl.BlockSpec((1, D_INNER, N), lambda c: (rev(c), 0, 0)),
                  pl.BlockSpec((L, D_INNER), lambda c: (rev(c), 0))] + r_in,
        out_specs=[pl.BlockSpec((L, D_INNER), lambda c: (rev(c), 0)),
                   pl.BlockSpec((L, gn), lambda c: (rev(c), 0)),
                   pl.BlockSpec((L, gn), lambda c: (rev(c), 0)),
                   pl.BlockSpec((L, LANES), lambda c: (rev(c), 0)),
                   pl.BlockSpec((1, LANES), lambda c: (0, 0)),
                   pl.BlockSpec((1, LANES), lambda c: (0, 0)),
                   pl.BlockSpec((1, D_INNER), lambda c: (0, 0))] + r_out,
        out_shape=[jax.ShapeDtypeStruct((S, D_INNER), F32), jax.ShapeDtypeStruct((S, gn), F32),
                   jax.ShapeDtypeStruct((S, gn), F32), jax.ShapeDtypeStruct((S, LANES), F32),
                   jax.ShapeDtypeStruct((1, LANES), F32), jax.ShapeDtypeStruct((1, LANES), F32),
                   jax.ShapeDtypeStruct((1, D_INNER), F32)] + r_shapes,
        scratch_shapes=[pltpu.VMEM((D_INNER, N), F32), pltpu.VMEM((L, D_INNER), BF16), pltpu.VMEM((L, D_INNER), BF16),
                        pltpu.VMEM((3, L, D_INNER), F32), pltpu.VMEM((L, LANES), F32), pltpu.VMEM((LANES, L), F32)]
                       + r_sems,
        compiler_params=pltpu.CompilerParams(dimension_semantics=("arbitrary",), vmem_limit_bytes=VMEM_LIMIT,
                                             has_side_effects=rider is not None),
    )(xbc, xbc, xbc, dtraw, dtb, A, dexp, e_mat, et_mat, states, dy, *r_args)
    return res[:7], res[7:]


BAND_A = (A_PREV + 1) * CHUNK
REL_W = 640


def _relpos_select():
    k = np.arange(REL_W)
    rel = np.where(k < BAND_A, A_PREV * CHUNK - k, A_PREV * CHUNK - (k - REL_W))
    idx = np.clip(rel, -MAX_REL, MAX_REL) + MAX_REL
    sel = (np.arange(REL_W)[:, None] == idx[None, :]) & (k != BAND_A)[None, :]
    return sel.astype(np.float32)


def relpos_bias(table_pad, *, name):
    def body(t_ref, s_ref, o_ref):
        v = jnp.dot(t_ref[...], s_ref[...], precision=HI, preferred_element_type=F32)
        for h in range(N_HEADS):
            o_ref[h] = pltpu.roll(jnp.broadcast_to(v[h:h + 1, :], (CHUNK, REL_W)), 0, 1, stride=1, stride_axis=0)

    return pl.pallas_call(body, name=name, out_shape=jax.ShapeDtypeStruct((N_HEADS, CHUNK, REL_W), F32),
                          compiler_params=pltpu.CompilerParams(vmem_limit_bytes=VMEM_LIMIT),
                          )(table_pad, jnp.asarray(_relpos_select()))


def relpos_grad(dbias_rev, *, name):
    def body(d_ref, s_ref, o_ref):
        head = lax.broadcasted_iota(jnp.int32, (N_HEADS, 1), 0)
        dv = jnp.zeros((N_HEADS, REL_W), F32)
        for h in range(N_HEADS):
            back = pltpu.roll(d_ref[h], REL_W - (CHUNK - 1), 1, stride=1, stride_axis=0)
            dv = dv + jnp.where(head == h, jnp.sum(back, axis=0, keepdims=True), 0.0)
        o_ref[...] = lax.dot_general(dv, s_ref[...], _NT, precision=HI, preferred_element_type=F32)

    return pl.pallas_call(body, name=name, out_shape=jax.ShapeDtypeStruct((N_HEADS, REL_W), F32),
                          compiler_params=pltpu.CompilerParams(vmem_limit_bytes=VMEM_LIMIT),
                          )(dbias_rev, jnp.asarray(_relpos_select()))


def loss_head(y, t, *, name, tm=256):
    S, D = y.shape
    tm = min(tm, S)

    def body(y_ref, t_ref, dy_ref, l_ref):
        e = y_ref[...] - t_ref[...]
        dy_ref[...] = e * (1.0 / D)

        @pl.when(pl.program_id(0) == 0)
        def _():
            l_ref[...] = jnp.zeros_like(l_ref)

        part = jnp.sum(jnp.sum(e * e, axis=1, keepdims=True), axis=0, keepdims=True) * (0.5 / D)
        l_ref[...] += jnp.broadcast_to(part, l_ref.shape)

    return pl.pallas_call(
        body, name=name, grid=(S // tm,),
        in_specs=[pl.BlockSpec((tm, D), lambda i: (i, 0))] * 2,
        out_specs=[pl.BlockSpec((tm, D), lambda i: (i, 0)), pl.BlockSpec((1, LANES), lambda i: (0, 0))],
        out_shape=[jax.ShapeDtypeStruct((S, D), F32), jax.ShapeDtypeStruct((1, LANES), F32)],
        compiler_params=_params("arbitrary"),
    )(y, t)


def f_adamw(w, g, m, v):
    m = ADAM_B1 * m + (1.0 - ADAM_B1) * g
    v = ADAM_B2 * v + (1.0 - ADAM_B2) * (g * g)
    m_hat = m / (1.0 - ADAM_B1 ** ADAM_STEP)
    v_hat = v / (1.0 - ADAM_B2 ** ADAM_STEP)
    delta = -ADAM_LR * (m_hat / (jnp.sqrt(v_hat) + ADAM_EPS) + ADAM_WD * w)
    return delta, m, v


def f_norm_id(x, g):
    return f_rmsnorm(x, g)[0], x


ANY = pl.BlockSpec(memory_space=pl.ANY)


def _pos():
    return lax.axis_index("x"), lax.axis_index("y"), lax.axis_index("c")


def _other_chips(x, y):
    return [(1 - x, y), (x, 1 - y), (1 - x, 1 - y)]


class Rider:
    def __init__(self, ins, outs, sems, start, mid, finish):
        self.ins, self.outs, self.sems = list(ins), list(outs), list(sems)
        self.start, self.mid, self.finish = start, mid, finish


def _rider_parts(rider):
    if rider is None:
        return [], [], [], [], []
    return [ANY] * len(rider.ins), [ANY] * len(rider.outs), rider.outs, rider.sems, rider.ins


def _ride(rider, ins, outs, sems, first, mid, last):
    pos = _pos()

    @pl.when(first)
    def _():
        rider.start(pos, ins, outs, sems)

    if rider.mid is not None:
        @pl.when(mid)
        def _():
            rider.mid(pos, ins, outs, sems)

    @pl.when(last)
    def _():
        rider.finish(pos, ins, outs, sems)


def run_rider(rider, *, name):
    n_in, n_out = len(rider.ins), len(rider.outs)

    def body(*refs):
        ins, outs, sems = refs[:n_in], refs[n_in:n_in + n_out], refs[n_in + n_out:]
        pos = _pos()
        rider.start(pos, ins, outs, sems)
        if rider.mid is not None:
            rider.mid(pos, ins, outs, sems)
        rider.finish(pos, ins, outs, sems)

    return pl.pallas_call(
        body, name=name, in_specs=[ANY] * n_in, out_specs=[ANY] * n_out, out_shape=rider.outs,
        scratch_shapes=rider.sems, compiler_params=pltpu.CompilerParams(has_side_effects=True),
    )(*rider.ins)


def gather_rider(shards):
    n = len(shards)

    def copies(pos, ins, outs, sems):
        x, y, c = pos
        send, recv, fsend, frecv = sems
        me = 2 * x + y
        sib = (x, y, 1 - c)
        first, arrive, passed, theirs = [], [], [], []
        for i in range(n):
            for j, (px, py) in enumerate(_other_chips(x, y)):
                k = 3 * i + j
                far = dict(device_id=(px, py, c), device_id_type=MESH)
                near = dict(device_id=sib, device_id_type=MESH)
                got = outs[i].at[2 * px + py, c]
                his = outs[i].at[2 * px + py, 1 - c]
                first.append(pltpu.make_async_remote_copy(ins[i].at[c], outs[i].at[me, c], send.at[k], recv.at[k], **far))
                arrive.append(pltpu.make_async_remote_copy(ins[i].at[c], got, send.at[k], recv.at[k], **far))
                passed.append(pltpu.make_async_remote_copy(got, got, fsend.at[k], frecv.at[k], **near))
                theirs.append(pltpu.make_async_remote_copy(his, his, fsend.at[k], frecv.at[k], **near))
        return first, arrive, passed, theirs

    def start(*a):
        for cp in copies(*a)[0]:
            cp.start()

    def mid(*a):
        _, arrive, passed, _ = copies(*a)
        for got, cp in zip(arrive, passed):
            got.wait_recv()
            cp.start()

    def finish(*a):
        first, _, passed, theirs = copies(*a)
        for cp in theirs:
            cp.wait_recv()
        for cp in first + passed:
            cp.wait_send()

    return Rider(shards, [jax.ShapeDtypeStruct((4,) + s.shape, s.dtype) for s in shards],
                 [pltpu.SemaphoreType.DMA((3 * n,))] * 4, start, mid, finish)


def scatter_rider(ps):
    n = len(ps)

    def copies(pos, ins, outs, sems):
        x, y, c = pos
        send, recv = sems
        return [pltpu.make_async_remote_copy(ins[i].at[2 * px + py], outs[i].at[j], send.at[3 * i + j], recv.at[3 * i + j],
                                             device_id=(px, py, c), device_id_type=MESH)
                for i in range(n) for j, (px, py) in enumerate(_other_chips(x, y))]

    def start(*a):
        for cp in copies(*a):
            cp.start()

    def finish(*a):
        for cp in copies(*a):
            cp.wait()

    return Rider(ps, [jax.ShapeDtypeStruct((3,) + p.shape[1:], p.dtype) for p in ps],
                 [pltpu.SemaphoreType.DMA((3 * n,))] * 2, start, None, finish)


def pair_swap_halves(gs, *, name):
    n = len(gs)

    def body(*refs):
        ins, outs = refs[:n], refs[n:2 * n]
        send, recv = refs[2 * n:]
        x, y, c = _pos()
        cps = []
        for i in range(n):
            cp = pltpu.make_async_remote_copy(ins[i].at[1 - c], outs[i], send.at[i], recv.at[i],
                                              device_id=(x, y, 1 - c), device_id_type=MESH)
            cp.start()
            cps.append(cp)
        for cp in cps:
            cp.wait()

    return pl.pallas_call(
        body, name=name, in_specs=[ANY] * n, out_specs=[ANY] * n,
        out_shape=[jax.ShapeDtypeStruct(g.shape[1:], g.dtype) for g in gs],
        scratch_shapes=[pltpu.SemaphoreType.DMA((n,)), pltpu.SemaphoreType.DMA((n,))],
        compiler_params=pltpu.CompilerParams(has_side_effects=True),
    )(*gs)


def pair_share(hs, *, name):
    n = len(hs)

    def body(*refs):
        ins, outs = refs[:n], refs[n:2 * n]
        send, recv = refs[2 * n:]
        x, y, c = _pos()
        cps = []
        for i in range(n):
            cp = pltpu.make_async_remote_copy(ins[i], outs[i], send.at[i], recv.at[i],
                                              device_id=(x, y, 1 - c), device_id_type=MESH)
            cp.start()
            cps.append(cp)
        for cp in cps:
            cp.wait()

    return pl.pallas_call(
        body, name=name, in_specs=[ANY] * n, out_specs=[ANY] * n,
        out_shape=[jax.ShapeDtypeStruct(h.shape, h.dtype) for h in hs],
        scratch_shapes=[pltpu.SemaphoreType.DMA((n,)), pltpu.SemaphoreType.DMA((n,))],
        compiler_params=pltpu.CompilerParams(has_side_effects=True),
    )(*hs)


def gather_all(buf, *, name):
    def body(in_ref, out_ref, send, recv, loc):
        x, y, c = _pos()
        lid = 4 * x + 2 * y + c
        lc = pltpu.make_async_copy(in_ref, out_ref.at[lid], loc.at[0])
        lc.start()
        cps = []
        for k in range(1, 8):
            px = 1 - x if k & 4 else x
            py = 1 - y if k & 2 else y
            pc = 1 - c if k & 1 else c
            cp = pltpu.make_async_remote_copy(in_ref, out_ref.at[lid], send.at[k - 1], recv.at[k - 1],
                                              device_id=(px, py, pc), device_id_type=MESH)
            cp.start()
            cps.append((cp, 4 * px + 2 * py + pc, (px, py, pc)))
        for k, (cp, plid, peer) in enumerate(cps):
            cp.wait_send()
            pltpu.make_async_remote_copy(in_ref, out_ref.at[plid], send.at[k], recv.at[k],
                                         device_id=peer, device_id_type=MESH).wait_recv()
        lc.wait()

    return pl.pallas_call(
        body, name=name, in_specs=[ANY], out_specs=ANY,
        out_shape=jax.ShapeDtypeStruct((8,) + buf.shape, buf.dtype),
        scratch_shapes=[pltpu.SemaphoreType.DMA((7,)), pltpu.SemaphoreType.DMA((7,)), pltpu.SemaphoreType.DMA((1,))],
        compiler_params=pltpu.CompilerParams(has_side_effects=True),
    )(buf)


def sum_slots(a, *, name):
    n = a.shape[0]

    def body(a_ref, o_ref):
        acc = a_ref[0]
        for k in range(1, n):
            acc = acc + a_ref[k]
        o_ref[...] = acc

    return pl.pallas_call(body, name=name, out_shape=jax.ShapeDtypeStruct(a.shape[1:], a.dtype),
                          compiler_params=pltpu.CompilerParams(vmem_limit_bytes=VMEM_LIMIT))(a)


def _row_tile(r, want, mult=16):
    t = (min(want, r) // mult) * mult
    while t >= mult:
        if r % t == 0:
            return t
        t -= mult
    return r


def pair_add(g, r1, csel, *, name):
    _, _, r, C = g.shape
    tr = _row_tile(r, 256)

    def body(g_ref, r_ref, c_ref, p32_ref, pb_ref):
        south = c_ref[0:1, 0:1] == 0.0
        p = jnp.where(south, g_ref[0, 0], g_ref[1, 0]) + r_ref[0]
        p32_ref[0] = p
        pb_ref[0] = p.astype(BF16)

    return pl.pallas_call(
        body, name=name, grid=(4, r // tr),
        in_specs=[pl.BlockSpec((2, 1, tr, C), lambda j, t: (0, j, t, 0)), pl.BlockSpec((1, tr, C), lambda j, t: (j, t, 0)),
                  pl.BlockSpec((1, LANES), lambda j, t: (0, 0))],
        out_specs=[pl.BlockSpec((1, tr, C), lambda j, t: (j, t, 0))] * 2,
        out_shape=[jax.ShapeDtypeStruct((4, r, C), F32), jax.ShapeDtypeStruct((4, r, C), BF16)],
        compiler_params=_params("parallel", "parallel"),
    )(g, r1, csel)


def chip_add(p32, r3, msel, *, name):
    _, r, C = p32.shape
    tr = _row_tile(r, 128)

    def body(p_ref, r_ref, m_ref, o_ref):
        me = m_ref[0:1, 0:1]
        acc = jnp.where(me == 0.0, p_ref[0], jnp.where(me == 1.0, p_ref[1], jnp.where(me == 2.0, p_ref[2], p_ref[3])))
        for j in range(3):
            acc = acc + r_ref[j].astype(F32)
        o_ref[...] = acc

    return pl.pallas_call(
        body, name=name, grid=(r // tr,),
        in_specs=[pl.BlockSpec((4, tr, C), lambda t: (0, t, 0)), pl.BlockSpec((3, tr, C), lambda t: (0, t, 0)),
                  pl.BlockSpec((1, LANES), lambda t: (0, 0))],
        out_specs=pl.BlockSpec((tr, C), lambda t: (t, 0)),
        out_shape=jax.ShapeDtypeStruct((r, C), F32),
        compiler_params=_params("parallel"),
    )(p32, r3, msel)


def _consts():
    i512 = np.arange(N_HEADS * HEAD_DIM)
    i128 = np.arange(LANES)
    bd512 = (i512[:, None] // HEAD_DIM == i512[None, :] // HEAD_DIM).astype(np.float32)
    bd128 = (i128[:, None] // HEAD_DIM == i128[None, :] // HEAD_DIM).astype(np.float32)
    fold = (np.arange(HEAD_DIM)[:, None] == (i512[None, :] % HEAD_DIM)).astype(np.float32)
    grp = N_HEADS // 2 * HEAD_DIM
    expand = ((i128[:, None] // HEAD_DIM == i512[None, :] // grp)
              & (i128[:, None] % HEAD_DIM == i512[None, :] % HEAD_DIM)).astype(np.float32)
    band = (B_PREV + 1) * CHUNK
    rel = np.arange(CHUNK)[:, None] - (np.arange(band)[None, :] - B_PREV * CHUNK)
    slopes = 2.0 ** (-8.0 * np.arange(1, N_HEADS + 1, dtype=np.float32) / N_HEADS)
    bias_b = (-slopes[:, None, None] * np.abs(rel).astype(np.float32)[None]).astype(np.float32)
    return [jnp.asarray(a) for a in (bd512, bd128, fold, expand)], jnp.asarray(bias_b)


def _ffn_fwd(xin, l, W, P):
    g = P["norm_ffn"][l:l + 1]
    (h,) = rowwise(f_rmsnorm, [xin], [g], [(D_MODEL, BF16)], name=f"ffn{l}_norm")
    Wi = W["ffn_in"][l]
    gate = matmul(h, Wi[:, :D_FF], mode="nn", name=f"ffn{l}_gate", out_dtype=BF16)
    up = matmul(h, Wi[:, D_FF:], mode="nn", name=f"ffn{l}_up", out_dtype=BF16)
    gc, act = dwconv_fwd(gate, P["ffn_conv_w"][l], P["ffn_conv_b"][l:l + 1], lambda y, u: (y, _silu(y) * u), [up],
                         [BF16, BF16], name=f"ffn{l}_conv")
    xout = matmul(act, W["ffn_out"][l], mode="nn", name=f"ffn{l}_out", residual=xin)
    return xout, (xin, h, gate, gc, up, act)


def _ffn_bwd(dxout, l, saved, W, P):
    xin, h, gate, gc, up, act = saved
    g = P["norm_ffn"][l:l + 1]
    Wi = W["ffn_in"][l]
    dact = matmul(dxout, W["ffn_out"][l], mode="nt", name=f"ffn{l}_dact", out_dtype=BF16)
    dWo = matmul(act, dxout, mode="tn", name=f"ffn{l}_dwout")
    dgate, dcw, dcb, dup = dwconv_bwd(gate, P["ffn_conv_w"][l], [gc, up, dact],
                                      lambda c, u, da: (da * u * _dsilu(c), da * _silu(c)), [BF16],
                                      name=f"ffn{l}_dconv")
    dh = matmul(dgate, Wi[:, :D_FF], mode="nt", name=f"ffn{l}_dh_gate")
    dh = matmul(dup, Wi[:, D_FF:], mode="nt", name=f"ffn{l}_dh_up", residual=dh)
    dWi = jnp.concatenate([matmul(h, dgate, mode="tn", name=f"ffn{l}_dw_gate"),
                           matmul(h, dup, mode="tn", name=f"ffn{l}_dw_up")], axis=1)
    dxin, dg = rowwise_vjp(f_norm_id, [xin], [g], [dh, dxout], [(0, F32)], [0], name=f"ffn{l}_dnorm")
    return dxin, dWi, dWo, dg, dcw, dcb


class NoComm:
    def fwd_rider(self, tag):
        return None

    def fwd_done(self, tag, outs, W, P):
        pass

    def grads(self, tag, cols, rows):
        return None

    def bwd_done(self, tag, outs):
        pass


def local_step(x, tgt, W, P, comm):
    qk_consts, bias_b = _consts()
    pad_rows = lambda t, n: jnp.pad(t, ((n * CHUNK, 0), (0, 0)))
    DQ = N_HEADS * HEAD_DIM

    g_mix0 = P["norm_mix"][0:1]
    (h0,) = rowwise(f_rmsnorm, [x], [g_mix0], [(D_MODEL, BF16)], name="attn_norm")
    qkv = matmul(h0, W["attn_in"], mode="nn", name="attn_qkv")
    qk_par = [P["q_norm_a"], P["k_norm_a"], P["q_norm_b"], P["k_norm_b"]] + qk_consts
    qa, ka, va, qb, kb, vb = rowwise(f_qknorm, [qkv], qk_par, [(DQ, BF16)] * 6, name="attn_qknorm")
    ka, va, kb, vb = pad_rows(ka, A_PREV), pad_rows(va, A_PREV), pad_rows(kb, B_PREV), pad_rows(vb, B_PREV)
    table = jnp.pad(P["relpos_table"], ((0, 0), (0, REL_W - (2 * MAX_REL + 1))))
    nj = min(ATT_TQ, x.shape[0]) // CHUNK
    bias_a = widen_bias(relpos_bias(table, name="relpos_bias")[:, :, :BAND_A], A_PREV, nj)
    bias_b = widen_bias(bias_b, B_PREV, nj)
    sinks = jnp.broadcast_to(P["sinks"].reshape(N_HEADS // 2, 2, 1), (N_HEADS // 2, 2, LANES))
    oa, late = attn_fwd(qa, ka, va, bias_a, None, n_prev=A_PREV, name="attn_a", rider=comm.fwd_rider("a"))
    comm.fwd_done("a", late, W, P)
    ob, late = attn_fwd(qb, kb, vb, bias_b, sinks, n_prev=B_PREV, name="attn_b", rider=comm.fwd_rider("b"))
    comm.fwd_done("b", late, W, P)
    Wao = W["attn_out"]
    x1 = matmul(oa, Wao[:DQ], mode="nn", name="attn_out_a", residual=x)
    x1 = matmul(ob, Wao[DQ:], mode="nn", name="attn_out_b", residual=x1)
    x2, ffn0 = _ffn_fwd(x1, 0, W, P)

    g_mix1 = P["norm_mix"][1:2]
    (h2,) = rowwise(f_rmsnorm, [x2], [g_mix1], [(D_MODEL, BF16)], name="ssm_norm_in")
    Ws = W["ssm_in"]
    CC = D_INNER + 2 * SSM_GROUPS * SSM_STATE
    Wz, Wx = Ws[:, :D_INNER], Ws[:, D_INNER:D_INNER + CC]
    Wdt = jnp.pad(Ws[:, D_INNER + CC:], ((0, 0), (0, LANES - SSM_HEADS)))
    z = matmul(h2, Wz, mode="nn", name="ssm_z", out_dtype=BF16)
    xr = matmul(h2, Wx, mode="nn", name="ssm_xbc", out_dtype=BF16)
    dtraw = matmul(h2, Wdt, mode="nn", name="ssm_dt")
    xc, xbc = dwconv_fwd(xr, P["ssm_conv_w"], P["ssm_conv_b"], lambda y: (y, _silu(y)), [], [BF16, F32],
                         name="ssm_conv")
    pad32 = lambda v: jnp.pad(v, ((0, 0), (0, LANES - SSM_HEADS)))
    A = pad32(-jnp.exp(P["ssm_a_log"]))
    dtb = pad32(P["ssm_dt_bias"])
    dexp = jnp.repeat(P["ssm_d"], D_INNER // SSM_HEADS, axis=1)
    y, states = ssd_fwd(xbc, dtraw, dtb, A, dexp, name="ssd_fwd")
    (y2,) = rowwise(f_gate_norm, [y, z], [P["ssm_norm"]], [(D_INNER, BF16)], name="ssm_gate_norm")
    x3 = matmul(y2, W["ssm_out"], mode="nn", name="ssm_out", residual=x2)
    x4, ffn1 = _ffn_fwd(x3, 1, W, P)

    dx4, lpart = loss_head(x4, tgt, name="loss_head")

    dx3, dWfi1, dWfo1, dgf1, dfcw1, dfcb1 = _ffn_bwd(dx4, 1, ffn1, W, P)
    out_f1 = comm.grads("f1", dWfi1, dWfo1)
    dy2 = matmul(dx3, W["ssm_out"], mode="nt", name="ssm_dy")
    dWso = matmul(y2, dx3, mode="tn", name="ssm_dwout")
    dy, dz, dnw = rowwise_vjp(f_gate_norm, [y, z], [P["ssm_norm"]], [dy2], [(0, F32), (1, BF16)], [0],
                              name="ssm_dgate_norm")
    (dxs, dB, dC, ddtraw, dA, ddtb, dDl), sent = ssd_bwd(xbc, dtraw, dtb, A, dexp, states, dy, name="ssd_bwd",
                                                          rider=out_f1)
    comm.bwd_done("f1", sent)
    dxr, dscw, dscb = dwconv_bwd(xr, P["ssm_conv_w"], [xc, (dxs, dB, dC)], lambda c, g: (g * _dsilu(c),), [],
                                 name="ssm_dconv")
    dh2 = matmul(dz, Wz, mode="nt", name="ssm_dh_z")
    dh2 = matmul(dxr, Wx, mode="nt", name="ssm_dh_x", residual=dh2)
    dh2 = matmul(ddtraw, Wdt, mode="nt", name="ssm_dh_dt", residual=dh2)
    dWs = jnp.concatenate([matmul(h2, dz, mode="tn", name="ssm_dw_z"),
                           matmul(h2, dxr, mode="tn", name="ssm_dw_x"),
                           matmul(h2, ddtraw, mode="tn", name="ssm_dw_dt")[:, :SSM_HEADS]], axis=1)
    dx2, dgm1 = rowwise_vjp(f_norm_id, [x2], [g_mix1], [dh2, dx3], [(0, F32)], [0], name="ssm_dnorm_in")
    out_s = comm.grads("s", dWs, dWso)

    dx1, dWfi0, dWfo0, dgf0, dfcw0, dfcb0 = _ffn_bwd(dx2, 0, ffn0, W, P)
    out_f0 = comm.grads("f0", dWfi0, dWfo0)
    doa = matmul(dx1, Wao[:DQ], mode="nt", name="attn_do_a", out_dtype=BF16)
    dob = matmul(dx1, Wao[DQ:], mode="nt", name="attn_do_b", out_dtype=BF16)
    dWao = jnp.concatenate([matmul(oa, dx1, mode="tn", name="attn_dwout_a"),
                            matmul(ob, dx1, mode="tn", name="attn_dwout_b")], axis=0)
    (dqa, dka, dva, dbias_a), sent = attn_bwd(qa, ka, va, doa, bias_a, None, n_prev=A_PREV, name="attn_a_bwd",
                                              rider=out_s)
    comm.bwd_done("s", sent)
    (dqb, dkb, dvb, _, dsk), sent = attn_bwd(qb, kb, vb, dob, bias_b, sinks, n_prev=B_PREV, name="attn_b_bwd",
                                             rider=out_f0)
    comm.bwd_done("f0", sent)
    pa, pb = A_PREV * CHUNK, B_PREV * CHUNK
    dqkv, dgqa, dgka, dgqb, dgkb = rowwise_vjp(f_qknorm, [qkv], qk_par, [dqa, dka, dva, dqb, dkb[pb:], dvb[pb:]],
                                               [(0, BF16)], [0, 1, 2, 3], name="attn_dqknorm",
                                               cot_skip=[0, pa, pa, 0, 0, 0])
    dh0 = matmul(dqkv, W["attn_in"], mode="nt", name="attn_dh")
    dWai = matmul(h0, dqkv, mode="tn", name="attn_dwin")
    dx, dgm0 = rowwise_vjp(f_norm_id, [x], [g_mix0], [dh0, dx1], [(0, F32)], [0], name="attn_dnorm")
    dbias_a = fold_bias(dbias_a, A_PREV, nj)
    dbias_rev = jnp.pad(dbias_a[:, ::-1, :], ((0, 0), (0, 0), (0, REL_W - BAND_A)))
    dtable = relpos_grad(dbias_rev, name="relpos_grad")[:, :2 * MAX_REL + 1]

    gW = {"attn_in": dWai, "attn_out": dWao, "ssm_in": dWs, "ssm_out": dWso,
          "ffn_in": [dWfi0, dWfi1], "ffn_out": [dWfo0, dWfo1]}
    gP = {"norm_mix": jnp.concatenate([dgm0, dgm1], axis=0),
          "norm_ffn": jnp.concatenate([dgf0, dgf1], axis=0),
          "relpos_table": dtable, "q_norm_a": dgqa, "k_norm_a": dgka, "q_norm_b": dgqb, "k_norm_b": dgkb,
          "sinks": dsk[:, :, 0].reshape(1, N_HEADS),
          "ssm_conv_w": dscw, "ssm_conv_b": dscb,
          "ssm_dt_bias": ddtb[:, :SSM_HEADS], "ssm_a_log": dA[:, :SSM_HEADS] * A[:, :SSM_HEADS],
          "ssm_d": dDl.reshape(SSM_HEADS, D_INNER // SSM_HEADS).sum(axis=1).reshape(1, SSM_HEADS),
          "ssm_norm": dnw,
          "ffn_conv_w": jnp.stack([dfcw0, dfcw1]), "ffn_conv_b": jnp.concatenate([dfcb0, dfcb1], axis=0)}
    return lpart, dx, gW, gP


WEIGHTS = ["norm_mix", "norm_ffn", "attn_w_in", "attn_w_out", "relpos_table", "q_norm_a", "k_norm_a", "q_norm_b",
           "k_norm_b", "sinks", "ssm_w_in", "ssm_conv_w", "ssm_conv_b", "ssm_dt_bias", "ssm_a_log", "ssm_d",
           "ssm_norm", "ssm_w_out", "ffn_w_in", "ffn_conv_w", "ffn_conv_b", "ffn_w_out"]
ARGS = ["x"] + WEIGHTS + ["loss_target"] + ["m_" + w for w in WEIGHTS] + ["v_" + w for w in WEIGHTS]
N_CHIPS = 4
SMALL_ROWS = 384
SMALL_ORDER = ["norm_mix", "norm_ffn", "relpos_table", "q_norm_a", "k_norm_a", "q_norm_b", "k_norm_b", "sinks",
               "ssm_dt_bias", "ssm_a_log", "ssm_d", "ffn_conv_b", "ssm_conv_w", "ssm_conv_b", "ssm_norm", "ffn_conv_w"]


def _cols_to_slabs(g):
    K, N = g.shape
    return g.reshape(2, K // 2, N_CHIPS, N // N_CHIPS).transpose(0, 2, 1, 3)


def _rows_to_slabs(g):
    R, C = g.shape
    return g.reshape(N_CHIPS, 2, R // (2 * N_CHIPS), C).transpose(1, 0, 2, 3)


class MeshComm:
    def __init__(self, d, xi, yi, ci):
        self.d, self.ci, self.me = d, ci, 2 * xi + yi
        self.csel = jnp.full((1, LANES), ci, F32)
        self.msel = jnp.full((1, LANES), self.me, F32)
        halves = lambda w: w.reshape((2, -1, w.shape[-1]))
        small = jnp.concatenate([d[k].reshape(-1) for k in ("ssm_conv_w", "ssm_conv_b", "ssm_norm", "ffn_conv_w")])
        small = jnp.pad(small, (0, 2 * 40 * LANES - small.shape[0])).reshape(2, 40, LANES)
        self.shards = {"attn": [halves(d["attn_w_in"][0].astype(BF16)), halves(d["attn_w_out"][0].astype(BF16))],
                       "a": [d["ffn_w_in"].astype(BF16), small],
                       "b": [d["ffn_w_out"].astype(BF16), halves(d["ssm_w_in"][0].astype(BF16)),
                             halves(d["ssm_w_out"][0].astype(BF16))]}
        self.p32, self.mine = {}, {}

    def _whole(self, tag, outs):
        return [lax.dynamic_update_slice_in_dim(g, s[None], self.me, axis=0) for g, s in zip(outs, self.shards[tag])]

    @staticmethod
    def _cat_cols(g):
        return jnp.concatenate([g[j].reshape((-1, g.shape[-1])) for j in range(N_CHIPS)], axis=1)

    def first_weights(self):
        g_ai, g_ao = self._whole("attn", run_rider(gather_rider(self.shards["attn"]), name="gather_attn"))
        return {"attn_in": self._cat_cols(g_ai), "attn_out": g_ao.reshape(-1, D_MODEL)}

    def fwd_rider(self, tag):
        return gather_rider(self.shards[tag])

    def fwd_done(self, tag, outs, W, P):
        if tag == "b":
            g_fo, g_si, g_so = self._whole("b", outs)
            W["ffn_out"] = [g_fo[:, l].reshape(-1, D_MODEL) for l in range(2)]
            W["ssm_in"], W["ssm_out"] = self._cat_cols(g_si), g_so.reshape(-1, D_MODEL)
            return
        g_fi, g_sm = self._whole("a", outs)
        W["ffn_in"] = [jnp.concatenate([g_fi[j, l] for j in range(N_CHIPS)], axis=1) for l in range(2)]
        sm = g_sm.reshape(N_CHIPS, -1)
        CC = D_INNER + 2 * SSM_GROUPS * SSM_STATE
        c4, f4 = CC // N_CHIPS, D_FF // N_CHIPS
        o1 = SSM_CONV * c4
        o2 = o1 + c4
        o3 = o2 + D_INNER // N_CHIPS
        o4 = o3 + 2 * FFN_CONV * f4
        P["ssm_conv_w"] = sm[:, :o1].reshape(N_CHIPS, SSM_CONV, c4).transpose(1, 0, 2).reshape(SSM_CONV, CC)
        P["ssm_conv_b"] = sm[:, o1:o2].reshape(1, CC)
        P["ssm_norm"] = sm[:, o2:o3].reshape(1, D_INNER)
        P["ffn_conv_w"] = sm[:, o3:o4].reshape(N_CHIPS, 2, FFN_CONV, f4).transpose(1, 2, 0, 3).reshape(2, FFN_CONV, D_FF)

    def grads(self, tag, cols, rows):
        slabs = [_cols_to_slabs(cols), _rows_to_slabs(rows)]
        from_sib = pair_swap_halves(slabs, name="grad_pair_swap_" + tag)
        pairs = [pair_add(g, r, self.csel, name=f"grad_pair_add_{tag}{i}") for i, (g, r) in enumerate(zip(slabs, from_sib))]
        self.p32[tag] = [p[0] for p in pairs]
        return scatter_rider([p[1] for p in pairs])

    def bwd_done(self, tag, outs):
        self.mine[tag] = [chip_add(p, r, self.msel, name=f"grad_chip_add_{tag}{i}")
                          for i, (p, r) in enumerate(zip(self.p32[tag], outs))]

    def finish(self, d_attn_in, d_attn_out):
        self.bwd_done("at", run_rider(self.grads("at", d_attn_in, d_attn_out), name="grad_scatter_at"))
        order = ["at", "s", "f0", "f1"]
        mine = [m for t in order for m in self.mine[t]]
        theirs = pair_share(mine, name="grad_pair_share")
        full = [jnp.where(self.ci == 0, jnp.stack([a, b]), jnp.stack([b, a])).reshape((-1, a.shape[-1]))
                for a, b in zip(mine, theirs)]
        ai, ao, si, so, fi0, fo0, fi1, fo1 = full
        return {"attn_w_in": ai[None], "attn_w_out": ao[None], "ssm_w_in": si[None], "ssm_w_out": so[None],
                "ffn_w_in": jnp.stack([fi0, fi1]), "ffn_w_out": jnp.stack([fo0, fo1])}


def _adamw(w, g, m, v, name):
    shp = w.shape
    two = lambda a: a.reshape((-1, shp[-1]))
    outs = [(shp[-1], F32)] * 3
    d, nm, nv = rowwise(f_adamw, [two(w), two(g), two(m), two(v)], [], outs, name="adamw_" + name)
    return d.reshape(shp), nm.reshape(shp), nv.reshape(shp)


def kernel(x, norm_mix, norm_ffn, attn_w_in, attn_w_out, relpos_table, q_norm_a, k_norm_a, q_norm_b, k_norm_b, sinks, ssm_w_in, ssm_conv_w, ssm_conv_b, ssm_dt_bias, ssm_a_log, ssm_d, ssm_norm, ssm_w_out, ffn_w_in, ffn_conv_w, ffn_conv_b, ffn_w_out, loss_target, m_norm_mix, m_norm_ffn, m_attn_w_in, m_attn_w_out, m_relpos_table, m_q_norm_a, m_k_norm_a, m_q_norm_b, m_k_norm_b, m_sinks, m_ssm_w_in, m_ssm_conv_w, m_ssm_conv_b, m_ssm_dt_bias, m_ssm_a_log, m_ssm_d, m_ssm_norm, m_ssm_w_out, m_ffn_w_in, m_ffn_conv_w, m_ffn_conv_b, m_ffn_w_out, v_norm_mix, v_norm_ffn, v_attn_w_in, v_attn_w_out, v_relpos_table, v_q_norm_a, v_k_norm_a, v_q_norm_b, v_k_norm_b, v_sinks, v_ssm_w_in, v_ssm_conv_w, v_ssm_conv_b, v_ssm_dt_bias, v_ssm_a_log, v_ssm_d, v_ssm_norm, v_ssm_w_out, v_ffn_w_in, v_ffn_conv_w, v_ffn_conv_b, v_ffn_w_out):
    d = dict(zip(ARGS, (x, norm_mix, norm_ffn, attn_w_in, attn_w_out, relpos_table, q_norm_a, k_norm_a, q_norm_b, k_norm_b, sinks, ssm_w_in, ssm_conv_w, ssm_conv_b, ssm_dt_bias, ssm_a_log, ssm_d, ssm_norm, ssm_w_out, ffn_w_in, ffn_conv_w, ffn_conv_b, ffn_w_out, loss_target, m_norm_mix, m_norm_ffn, m_attn_w_in, m_attn_w_out, m_relpos_table, m_q_norm_a, m_k_norm_a, m_q_norm_b, m_k_norm_b, m_sinks, m_ssm_w_in, m_ssm_conv_w, m_ssm_conv_b, m_ssm_dt_bias, m_ssm_a_log, m_ssm_d, m_ssm_norm, m_ssm_w_out, m_ffn_w_in, m_ffn_conv_w, m_ffn_conv_b, m_ffn_w_out, v_norm_mix, v_norm_ffn, v_attn_w_in, v_attn_w_out, v_relpos_table, v_q_norm_a, v_k_norm_a, v_q_norm_b, v_k_norm_b, v_sinks, v_ssm_w_in, v_ssm_conv_w, v_ssm_conv_b, v_ssm_dt_bias, v_ssm_a_log, v_ssm_d, v_ssm_norm, v_ssm_w_out, v_ffn_w_in, v_ffn_conv_w, v_ffn_conv_b, v_ffn_w_out)))
    xi, yi, ci = _pos()
    me = 2 * xi + yi
    CC = D_INNER + 2 * SSM_GROUPS * SSM_STATE
    c4, f4 = CC // N_CHIPS, D_FF // N_CHIPS

    P = {k: d[k] for k in ["norm_mix", "norm_ffn", "q_norm_a", "k_norm_a", "q_norm_b", "k_norm_b", "sinks",
                           "ssm_dt_bias", "ssm_a_log", "ssm_d", "ffn_conv_b"]}
    P["relpos_table"] = d["relpos_table"][0]
    comm = MeshComm(d, xi, yi, ci)
    W = comm.first_weights()
    lpart, dx, gW, gP = local_step(d["x"][0], d["loss_target"][0], W, P, comm)
    loss = lax.psum(lpart[0, 0], ("x", "y", "c"))
    grads = comm.finish(gW["attn_in"], gW["attn_out"])

    flat = jnp.concatenate([gP[k].reshape(-1) for k in SMALL_ORDER])
    flat = jnp.pad(flat, (0, SMALL_ROWS * LANES - flat.shape[0])).reshape(SMALL_ROWS, LANES)
    tot = sum_slots(gather_all(flat, name="small_gather"), name="small_sum").reshape(-1)
    off = 0
    for k in SMALL_ORDER:
        n = int(np.prod(gP[k].shape))
        g = tot[off:off + n].reshape(gP[k].shape)
        off += n
        if k == "ssm_conv_w":
            g = lax.dynamic_slice_in_dim(g, me * c4, c4, axis=1)[None]
        elif k == "ssm_conv_b":
            g = lax.dynamic_slice_in_dim(g, me * c4, c4, axis=1)
        elif k == "ssm_norm":
            g = lax.dynamic_slice_in_dim(g, me * (D_INNER // N_CHIPS), D_INNER // N_CHIPS, axis=1)
        elif k == "ffn_conv_w":
            g = lax.dynamic_slice_in_dim(g, me * f4, f4, axis=2)
        elif k == "relpos_table":
            g = g[None]
        grads[k] = g

    deltas, new_m, new_v = {}, {}, {}
    for k in WEIGHTS:
        deltas[k], new_m[k], new_v[k] = _adamw(d[k], grads[k], d["m_" + k], d["v_" + k], k)
    return (loss, dx[None], *[grads[k] for k in WEIGHTS], *[deltas[k] for k in WEIGHTS],
            *[new_m[k] for k in WEIGHTS], *[new_v[k] for k in WEIGHTS])
```

```python
import functools

import numpy as np
import jax
import jax.numpy as jnp
from jax import lax
from jax.experimental import pallas as pl
from jax.experimental.pallas import tpu as pltpu

F32 = jnp.float32
BF16 = jnp.bfloat16
HI = lax.Precision.HIGHEST

D_MODEL = 1024
CHUNK = 64
EPS = 1e-6
HEAD_DIM = 64
N_HEADS = 8
A_PREV = 8
B_PREV = 2
MAX_REL = 256
D_INNER = 2048
SSM_HEADS = 32
SSM_GROUPS = 4
SSM_STATE = 128
SSM_CONV = 4
D_FF = 2816
FFN_CONV = 3
LANES = 128
SUBLANES = 8
VMEM_LIMIT = 56 * 1024 * 1024
SSD_L = 128

ADAM_LR = 0.001
ADAM_B1 = 0.9
ADAM_B2 = 0.999
ADAM_EPS = 1e-08
ADAM_WD = 0.01
ADAM_STEP = 10

MESH = pl.DeviceIdType.MESH


def _params(*sem):
    return pltpu.CompilerParams(dimension_semantics=sem, vmem_limit_bytes=VMEM_LIMIT)


def _pick(n, want):
    if n <= want:
        return n
    t = (want // LANES) * LANES
    while t >= LANES:
        if n % t == 0:
            return t
        t -= LANES
    return n


MM_ROWS = 512
MM_COLS = 1536
MM_RED = 2048


def matmul(a, b, *, mode, name, out_dtype=F32, residual=None):
    dims = {"nn": (((1,), (0,)), ((), ())), "nt": (((1,), (1,)), ((), ())), "tn": (((0,), (0,)), ((), ()))}[mode]
    if mode == "tn":
        assert residual is None and out_dtype == F32
        (K, M), (K2, N) = a.shape, b.shape
        assert K == K2, (a.shape, b.shape)
        tm, tn, tk = _pick(M, MM_COLS), _pick(N, MM_COLS), _pick(K, MM_RED)

        def body(a_ref, b_ref, o_ref):
            k = pl.program_id(2)
            p = lax.dot_general(a_ref[...].astype(BF16), b_ref[...].astype(BF16), dims, preferred_element_type=F32)

            @pl.when(k == 0)
            def _():
                o_ref[...] = p

            @pl.when(k != 0)
            def _():
                o_ref[...] += p

        return pl.pallas_call(
            body, name=name, grid=(M // tm, N // tn, K // tk),
            in_specs=[pl.BlockSpec((tk, tm), lambda i, j, k: (k, i)), pl.BlockSpec((tk, tn), lambda i, j, k: (k, j))],
            out_specs=pl.BlockSpec((tm, tn), lambda i, j, k: (i, j)),
            out_shape=jax.ShapeDtypeStruct((M, N), F32),
            compiler_params=_params("parallel", "parallel", "arbitrary"),
        )(a, b)

    if mode == "nn":
        (M, K), (K2, N) = a.shape, b.shape
    else:
        (M, K), (N, K2) = a.shape, b.shape
    assert K == K2, (a.shape, b.shape, mode)
    tm, tn = _pick(M, MM_ROWS), _pick(N, MM_COLS)

    def body(*refs):
        a_ref, b_ref = refs[:2]
        o_ref = refs[-1]
        r = lax.dot_general(a_ref[...].astype(BF16), b_ref[...].astype(BF16), dims, preferred_element_type=F32)
        if residual is not None:
            r = r + refs[2][...].astype(F32)
        o_ref[...] = r.astype(o_ref.dtype)

    a_spec = pl.BlockSpec((tm, K), lambda j, i: (i, 0))
    b_spec = pl.BlockSpec((K, tn), lambda j, i: (0, j)) if mode == "nn" else pl.BlockSpec((tn, K), lambda j, i: (j, 0))
    o_spec = pl.BlockSpec((tm, tn), lambda j, i: (i, j))
    in_specs = [a_spec, b_spec] + ([o_spec] if residual is not None else [])
    args = (a, b) + ((residual,) if residual is not None else ())
    return pl.pallas_call(
        body, name=name, grid=(N // tn, M // tm),
        in_specs=in_specs, out_specs=o_spec,
        out_shape=jax.ShapeDtypeStruct((M, N), out_dtype),
        compiler_params=_params("parallel", "parallel"),
    )(*args)


def matmul_norm(a, b, residual, g, *, name):
    (M, K), (_, N) = a.shape, b.shape
    tm = _pick(M, MM_ROWS)

    def body(a_ref, b_ref, r_ref, g_ref, x_ref, h_ref):
        x = jnp.dot(a_ref[...].astype(BF16), b_ref[...].astype(BF16), preferred_element_type=F32) + r_ref[...]
        x_ref[...] = x
        h_ref[...] = f_rmsnorm(x, g_ref[...])[0].astype(BF16)

    row = pl.BlockSpec((tm, N), lambda i: (i, 0))
    return pl.pallas_call(
        body, name=name, grid=(M // tm,),
        in_specs=[pl.BlockSpec((tm, K), lambda i: (i, 0)), pl.BlockSpec((K, N), lambda i: (0, 0)), row,
                  pl.BlockSpec((1, N), lambda i: (0, 0))],
        out_specs=[row, row],
        out_shape=[jax.ShapeDtypeStruct((M, N), F32), jax.ShapeDtypeStruct((M, N), BF16)],
        compiler_params=_params("parallel"),
    )(a, b, residual, g)


def matmul_dnorm(a, b, partial, x, g, dres, *, name):
    (M, K), (N, _) = a.shape, b.shape
    tm = _pick(M, MM_ROWS)
    has_part = partial is not None

    def body(*refs):
        a_ref, b_ref = refs[:2]
        x_ref, g_ref, dres_ref, dx_ref, dg_ref = refs[-5:]
        dh = lax.dot_general(a_ref[...].astype(BF16), b_ref[...].astype(BF16), _NT, preferred_element_type=F32)
        if has_part:
            dh = dh + refs[2][...]
        xv = x_ref[...]
        r = lax.rsqrt(jnp.mean(xv * xv, axis=-1, keepdims=True) + EPS)
        xhat = xv * r
        dxh = dh * g_ref[...]
        dx_ref[...] = dres_ref[...] + r * (dxh - xhat * jnp.mean(dxh * xhat, axis=-1, keepdims=True))
        dg = jnp.sum(dh * xhat, axis=0, keepdims=True)

        @pl.when(pl.program_id(0) == 0)
        def _():
            dg_ref[...] = dg

        @pl.when(pl.program_id(0) != 0)
        def _():
            dg_ref[...] += dg

    row = pl.BlockSpec((tm, N), lambda i: (i, 0))
    vec = pl.BlockSpec((1, N), lambda i: (0, 0))
    in_specs = [pl.BlockSpec((tm, K), lambda i: (i, 0)), pl.BlockSpec((N, K), lambda i: (0, 0))]
    args = [a, b]
    if has_part:
        in_specs.append(row)
        args.append(partial)
    return pl.pallas_call(
        body, name=name, grid=(M // tm,), in_specs=in_specs + [row, vec, row], out_specs=[row, vec],
        out_shape=[jax.ShapeDtypeStruct((M, N), F32), jax.ShapeDtypeStruct((1, N), F32)],
        compiler_params=_params("arbitrary"),
    )(*args, x, g, dres)


def rowwise(f, rows, params, outs, *, name, tm=256):
    S = rows[0].shape[0]
    tm = _row_tile(S, tm)
    nr, npar = len(rows), len(params)

    def body(*refs):
        vals = [r[...].astype(F32) for r in refs[:nr + npar]]
        res = f(*vals)
        for o_ref, r in zip(refs[nr + npar:], res):
            o_ref[...] = r.astype(o_ref.dtype)

    in_specs = [pl.BlockSpec((tm, r.shape[1]), lambda i: (i, 0)) for r in rows]
    in_specs += [pl.BlockSpec(p.shape, lambda i: (0, 0)) for p in params]
    out_specs = [pl.BlockSpec((tm, c), lambda i: (i, 0)) for c, _ in outs]
    out_shape = [jax.ShapeDtypeStruct((S, c), dt) for c, dt in outs]
    return pl.pallas_call(body, name=name, grid=(S // tm,), in_specs=in_specs, out_specs=out_specs,
                          out_shape=out_shape, compiler_params=_params("parallel"))(*rows, *params)


def rowwise_vjp(f, rows, params, cots, drow, dpar, *, name, tm=256, cot_skip=None):
    S = rows[0].shape[0]
    tm = _row_tile(S, tm)
    nr, npar, nc = len(rows), len(params), len(cots)
    skip = [0] * nc if cot_skip is None else [s // tm for s in cot_skip]
    assert cot_skip is None or all(s % tm == 0 for s in cot_skip)

    def body(*refs):
        vals = [r[...].astype(F32) for r in refs[:nr + npar]]
        cvals = [r[...].astype(F32) for r in refs[nr + npar:nr + npar + nc]]
        o_refs = refs[nr + npar + nc:]
        want = [ri for ri, _ in drow] + [nr + pi for pi in dpar]

        def f_want(*d):
            full = list(vals)
            for k, v in zip(want, d):
                full[k] = v
            return f(*full)

        _, vjp = jax.vjp(f_want, *[vals[k] for k in want])
        grads = vjp(tuple(cvals))
        for o_ref, g in zip(o_refs[:len(drow)], grads):
            o_ref[...] = g.astype(o_ref.dtype)
        first = pl.program_id(0) == 0
        for o_ref, g in zip(o_refs[len(drow):], grads[len(drow):]):
            g = g.astype(F32)

            @pl.when(first)
            def _(o_ref=o_ref, g=g):
                o_ref[...] = g

            @pl.when(jnp.logical_not(first))
            def _(o_ref=o_ref, g=g):
                o_ref[...] += g

    in_specs = [pl.BlockSpec((tm, r.shape[1]), lambda i: (i, 0)) for r in rows]
    in_specs += [pl.BlockSpec(p.shape, lambda i: (0, 0)) for p in params]
    in_specs += [pl.BlockSpec((tm, c.shape[1]), lambda i, s=s: (i + s, 0)) for c, s in zip(cots, skip)]
    out_specs = [pl.BlockSpec((tm, rows[ri].shape[1]), lambda i: (i, 0)) for ri, _ in drow]
    out_specs += [pl.BlockSpec(params[pi].shape, lambda i: (0, 0)) for pi in dpar]
    out_shape = [jax.ShapeDtypeStruct(rows[ri].shape, dt) for ri, dt in drow]
    out_shape += [jax.ShapeDtypeStruct(params[pi].shape, F32) for pi in dpar]
    return pl.pallas_call(body, name=name, grid=(S // tm,), in_specs=in_specs, out_specs=out_specs,
                          out_shape=out_shape, compiler_params=_params("arbitrary"))(*rows, *params, *cots)


HALO = 2 * SUBLANES


def dwconv_fwd(x, w, b, post, extra, outs, *, name, tm=256):
    S, C = x.shape
    K = w.shape[0]
    tm = min(tm, S)
    hb = tm // HALO
    ne = len(extra)

    def body(*refs):
        x_ref, halo_ref, w_ref, b_ref = refs[:4]
        e_refs = refs[4:4 + ne]
        o_refs = refs[4 + ne:4 + ne + len(outs)]
        buf = refs[-1]
        i = pl.program_id(0)
        buf[0:HALO, :] = jnp.where(i == 0, 0.0, halo_ref[...].astype(F32))
        buf[HALO:HALO + tm, :] = x_ref[...].astype(F32)
        for c0 in range(0, C, LANES):
            cs = slice(c0, c0 + LANES)
            acc = jnp.broadcast_to(b_ref[:, cs], (tm, LANES))
            for k in range(K):
                acc = acc + w_ref[k:k + 1, cs] * buf[pl.ds(HALO - (K - 1) + k, tm), cs]
            for o_ref, r in zip(o_refs, post(acc, *[e[:, cs].astype(F32) for e in e_refs])):
                o_ref[:, cs] = r.astype(o_ref.dtype)

    row = pl.BlockSpec((tm, C), lambda i: (i, 0))
    return pl.pallas_call(
        body, name=name, grid=(S // tm,),
        in_specs=[row,
                  pl.BlockSpec((HALO, C), lambda i: (jnp.maximum(i * hb - 1, 0), 0)),
                  pl.BlockSpec((K, C), lambda i: (0, 0)),
                  pl.BlockSpec((1, C), lambda i: (0, 0))] + [row] * ne,
        out_specs=[row] * len(outs),
        out_shape=[jax.ShapeDtypeStruct((S, C), dt) for dt in outs],
        scratch_shapes=[pltpu.VMEM((HALO + tm, C), F32)],
        compiler_params=_params("parallel"),
    )(x, x, w, b, *extra)


def dwconv_bwd(x, w, srcs, dy_fn, extra_outs, *, name, tm=256):
    S, C = x.shape
    K = w.shape[0]
    tm = min(tm, S)
    hb = tm // HALO
    n = S // tm
    groups = [s if isinstance(s, tuple) else (s,) for s in srcs]
    flat = [a for g in groups for a in g]
    nf = len(flat)

    def body(*refs):
        x_ref, xh_ref, w_ref = refs[:3]
        dx_ref, dw_ref, db_ref = refs[3 + 2 * nf:6 + 2 * nf]
        e_refs = refs[6 + 2 * nf:6 + 2 * nf + len(extra_outs)]
        bx, bd = refs[-2:]

        def strips(first, c0):
            out, at = [], first
            for g in groups:
                off = 0
                for a in g:
                    if off <= c0 < off + a.shape[1]:
                        out.append(refs[at][:, c0 - off:c0 - off + LANES].astype(F32))
                    off += a.shape[1]
                    at += 1
            return out

        i = pl.program_id(0)
        bx[0:HALO, :] = jnp.where(i == 0, 0.0, xh_ref[...].astype(F32))
        bx[HALO:HALO + tm, :] = x_ref[...].astype(F32)

        @pl.when(i == 0)
        def _():
            dw_ref[...] = jnp.zeros_like(dw_ref)
            db_ref[...] = jnp.zeros_like(db_ref)

        for c0 in range(0, C, LANES):
            cs = slice(c0, c0 + LANES)
            res = dy_fn(*strips(3, c0))
            dyv = res[0]
            for e_ref, r in zip(e_refs, res[1:]):
                e_ref[:, cs] = r.astype(e_ref.dtype)
            bd[0:tm, cs] = dyv
            bd[tm:tm + HALO, cs] = jnp.where(i == n - 1, 0.0, dy_fn(*strips(3 + nf, c0))[0])
            acc = jnp.zeros((tm, LANES), F32)
            for k in range(K):
                acc = acc + w_ref[k:k + 1, cs] * bd[pl.ds((K - 1) - k, tm), cs]
            dx_ref[:, cs] = acc.astype(dx_ref.dtype)
            for k in range(K):
                dw_ref[k:k + 1, cs] += jnp.sum(dyv * bx[pl.ds(HALO - (K - 1) + k, tm), cs], axis=0, keepdims=True)
            db_ref[:, cs] += jnp.sum(dyv, axis=0, keepdims=True)

    row = lambda c: pl.BlockSpec((tm, c), lambda i: (i, 0))
    nxt = lambda c: pl.BlockSpec((HALO, c), lambda i: (jnp.minimum((i + 1) * hb, S // HALO - 1), 0))
    return pl.pallas_call(
        body, name=name, grid=(n,),
        in_specs=[row(C), pl.BlockSpec((HALO, C), lambda i: (jnp.maximum(i * hb - 1, 0), 0)),
                  pl.BlockSpec((K, C), lambda i: (0, 0))]
                 + [row(a.shape[1]) for a in flat] + [nxt(a.shape[1]) for a in flat],
        out_specs=[row(C), pl.BlockSpec((K, C), lambda i: (0, 0)), pl.BlockSpec((1, C), lambda i: (0, 0))]
                  + [row(C)] * len(extra_outs),
        out_shape=[jax.ShapeDtypeStruct((S, C), BF16), jax.ShapeDtypeStruct((K, C), F32),
                   jax.ShapeDtypeStruct((1, C), F32)] + [jax.ShapeDtypeStruct((S, C), dt) for dt in extra_outs],
        scratch_shapes=[pltpu.VMEM((HALO + tm, C), F32), pltpu.VMEM((tm + HALO, C), F32)],
        compiler_params=_params("arbitrary"),
    )(x, x, w, *flat, *flat)


def _sigmoid(x):
    return 0.5 * jnp.tanh(0.5 * x) + 0.5


def _silu(x):
    return x * _sigmoid(x)


def _dsilu(x):
    s = _sigmoid(x)
    return s * (1.0 + x * (1.0 - s))


def f_rmsnorm(x, g):
    return (x * lax.rsqrt(jnp.mean(x * x, axis=-1, keepdims=True) + EPS) * g,)


SEL = lax.Precision.HIGH


def _group_norm(x, bd, width):
    ms = jnp.dot(x * x, bd, precision=SEL, preferred_element_type=F32) * (1.0 / width)
    return x * lax.rsqrt(ms + EPS)


def f_qknorm(qkv, gqa, gka, gqb, gkb, bd512, bd128, fold, expand):
    dq = N_HEADS * HEAD_DIM
    qa, ka, va, qb = (qkv[:, i * dq:(i + 1) * dq] for i in range(4))
    kb = qkv[:, 4 * dq:4 * dq + LANES]
    vb = qkv[:, 4 * dq + LANES:4 * dq + 2 * LANES]
    tile8 = lambda g: jnp.dot(g, fold, precision=HI, preferred_element_type=F32)
    qa = _group_norm(qa, bd512, HEAD_DIM) * tile8(gqa)
    ka = _group_norm(ka, bd512, HEAD_DIM) * tile8(gka)
    qb = _group_norm(qb, bd512, HEAD_DIM) * tile8(gqb)
    kb = _group_norm(kb, bd128, HEAD_DIM) * tile8(gkb)[:, :LANES]
    kb = jnp.dot(kb, expand, precision=SEL, preferred_element_type=F32)
    vb = jnp.dot(vb, expand, precision=SEL, preferred_element_type=F32)
    return qa, ka, va, qb, kb, vb


def f_gate_norm(y, z, nw):
    v = y * _silu(z)
    gw = D_INNER // SSM_GROUPS
    parts = []
    for g in range(SSM_GROUPS):
        vg = v[:, g * gw:(g + 1) * gw]
        parts.append(vg * lax.rsqrt(jnp.mean(vg * vg, axis=-1, keepdims=True) + EPS))
    return (jnp.concatenate(parts, axis=-1) * nw,)


ATT_TQ = 256
_NT = (((1,), (1,)), ((), ()))
_TN = (((0,), (0,)), ((), ()))


def _stack_heads(t, head0):
    return jnp.concatenate([jnp.where(head0, t, 0.0), jnp.where(head0, 0.0, t)], axis=0).astype(BF16)


def _attn_probs(qk, bias, valid, snk):
    s = qk * (HEAD_DIM ** -0.5) + bias
    s = jnp.where(valid, s, -jnp.inf)
    m = jnp.max(s, axis=1, keepdims=True)
    if snk is not None:
        m = jnp.maximum(m, snk)
    e = jnp.exp(s - m)
    den = jnp.sum(e, axis=1, keepdims=True)
    if snk is None:
        return e / den, None
    es = jnp.exp(snk - m)
    den = den + es
    return e / den, es / den


def widen_bias(bias, n_prev, nj):
    band = (n_prev + 1) * CHUNK
    wk = (nj + n_prev) * CHUNK
    rows = [jnp.pad(bias, ((0, 0), (0, 0), (j * CHUNK, wk - band - j * CHUNK)), constant_values=-jnp.inf)
            for j in range(nj)]
    return jnp.concatenate(rows, axis=1)


def fold_bias(dbw, n_prev, nj):
    band = (n_prev + 1) * CHUNK
    acc = dbw[:, :CHUNK, :band]
    for j in range(1, nj):
        acc = acc + dbw[:, j * CHUNK:(j + 1) * CHUNK, j * CHUNK:j * CHUNK + band]
    return acc


def attn_fwd(q, k, v, bias_w, sinks, *, n_prev, name, rider=None):
    S = q.shape[0]
    pad = n_prev * CHUNK
    tq = min(ATT_TQ, S)
    wk = tq + pad
    assert bias_w.shape == (N_HEADS, tq, wk), bias_w.shape
    has_sink = sinks is not None

    r_in, r_out, r_shapes, r_sems, r_args = _rider_parts(rider)
    n_own = 5 if has_sink else 4
    n_p, n_i = N_HEADS // 2, S // tq

    def body(*refs):
        q_ref, k_ref, v_ref, bias_ref = refs[:4]
        sink_ref = refs[4] if has_sink else None
        o_ref = refs[n_own + len(r_in)]
        if rider is not None:
            p_id, i_id = pl.program_id(0), pl.program_id(1)
            _ride(rider, refs[n_own:n_own + len(r_in)], refs[n_own + len(r_in) + 1:n_own + len(r_in) + 1 + len(r_out)],
                  refs[n_own + len(r_in) + 1 + len(r_out):],
                  jnp.logical_and(p_id == 0, i_id == 0), jnp.logical_and(p_id == n_p - 1, i_id == 0),
                  jnp.logical_and(p_id == n_p - 1, i_id == n_i - 1))
        start = pl.multiple_of(pl.program_id(1) * tq, tq)
        head0 = lax.broadcasted_iota(jnp.int32, (1, LANES), 1) < HEAD_DIM
        valid = lax.broadcasted_iota(jnp.int32, (1, wk), 1) + start >= pad
        kb = k_ref[pl.ds(start, wk), :]
        vb = v_ref[pl.ds(start, wk), :]
        qk = lax.dot_general(_stack_heads(q_ref[...].astype(F32), head0), kb, _NT, preferred_element_type=F32)
        ps = []
        for r in range(2):
            snk = sink_ref[0, r:r + 1, 0:1] if has_sink else None
            ps.append(_attn_probs(qk[r * tq:(r + 1) * tq, :], bias_ref[r], valid, snk)[0].astype(BF16))
        o2 = jnp.dot(jnp.concatenate(ps, axis=0), vb, preferred_element_type=F32)
        o_ref[...] = jnp.where(head0, o2[:tq, :], o2[tq:, :]).astype(o_ref.dtype)

    in_specs = [pl.BlockSpec((tq, LANES), lambda p, i: (i, p)),
                pl.BlockSpec((pad + S, LANES), lambda p, i: (0, p)),
                pl.BlockSpec((pad + S, LANES), lambda p, i: (0, p)),
                pl.BlockSpec((2, tq, wk), lambda p, i: (p, 0, 0))]
    args = [q, k, v, bias_w]
    if has_sink:
        in_specs.append(pl.BlockSpec((1, 2, LANES), lambda p, i: (p, 0, 0)))
        args.append(sinks)
    res = pl.pallas_call(
        body, name=name, grid=(n_p, n_i), in_specs=in_specs + r_in,
        out_specs=[pl.BlockSpec((tq, LANES), lambda p, i: (i, p))] + r_out,
        out_shape=[jax.ShapeDtypeStruct((S, N_HEADS * HEAD_DIM), BF16)] + r_shapes,
        scratch_shapes=r_sems,
        compiler_params=pltpu.CompilerParams(dimension_semantics=("arbitrary", "arbitrary"), vmem_limit_bytes=VMEM_LIMIT,
                                             has_side_effects=rider is not None),
    )(*args, *r_args)
    return res[0], res[1:]


def attn_bwd(q, k, v, do, bias_w, sinks, *, n_prev, name, rider=None):
    S = q.shape[0]
    pad = n_prev * CHUNK
    tq = min(ATT_TQ, S)
    wk = tq + pad
    assert bias_w.shape == (N_HEADS, tq, wk), bias_w.shape
    has_sink = sinks is not None
    scale = HEAD_DIM ** -0.5

    r_in, r_out, r_shapes, r_sems, r_args = _rider_parts(rider)
    n_own_in = 6 if has_sink else 5
    n_own_out = 5 if has_sink else 4
    n_p, n_i = N_HEADS // 2, S // tq

    def body(*refs):
        q_ref, k_ref, v_ref, do_ref, bias_ref = refs[:5]
        sink_ref = refs[5] if has_sink else None
        o0 = n_own_in + len(r_in)
        dq_ref, dk_ref, dv_ref, db_ref = refs[o0:o0 + 4]
        dsk_ref = refs[o0 + 4] if has_sink else None
        i = pl.program_id(1)
        if rider is not None:
            p_id = pl.program_id(0)
            _ride(rider, refs[n_own_in:o0], refs[o0 + n_own_out:o0 + n_own_out + len(r_out)],
                  refs[o0 + n_own_out + len(r_out):],
                  jnp.logical_and(p_id == 0, i == 0), jnp.logical_and(p_id == n_p // 2, i == 0),
                  jnp.logical_and(p_id == n_p - 1, i == n_i - 1))

        @pl.when(i == 0)
        def _():
            dk_ref[...] = jnp.zeros_like(dk_ref)
            dv_ref[...] = jnp.zeros_like(dv_ref)
            db_ref[...] = jnp.zeros_like(db_ref)
            if has_sink:
                dsk_ref[...] = jnp.zeros_like(dsk_ref)

        start = pl.multiple_of(i * tq, tq)
        head0 = lax.broadcasted_iota(jnp.int32, (1, LANES), 1) < HEAD_DIM
        valid = lax.broadcasted_iota(jnp.int32, (1, wk), 1) + start >= pad
        kb = k_ref[pl.ds(start, wk), :]
        vb = v_ref[pl.ds(start, wk), :]
        q2 = _stack_heads(q_ref[...].astype(F32), head0)
        do2 = _stack_heads(do_ref[...].astype(F32), head0)
        qk = lax.dot_general(q2, kb, _NT, preferred_element_type=F32)
        dp2 = lax.dot_general(do2, vb, _NT, preferred_element_type=F32)
        pbs, dss = [], []
        for r in range(2):
            rows = slice(r * tq, (r + 1) * tq)
            snk = sink_ref[0, r:r + 1, 0:1] if has_sink else None
            p, ps = _attn_probs(qk[rows, :], bias_ref[r], valid, snk)
            dp = dp2[rows, :]
            delta = jnp.sum(p * dp, axis=1, keepdims=True)
            ds = p * (dp - delta)
            db_ref[r] += ds
            if has_sink:
                dsk = -jnp.sum(ps * delta, axis=0, keepdims=True)
                dsk_ref[0, r:r + 1, :] += jnp.broadcast_to(dsk, (1, LANES))
            pbs.append(p.astype(BF16))
            dss.append(ds.astype(BF16))
        ds2 = jnp.concatenate(dss, axis=0)
        dq2 = jnp.dot(ds2, kb, preferred_element_type=F32) * scale
        dq_ref[...] = jnp.where(head0, dq2[:tq, :], dq2[tq:, :])
        dk_ref[pl.ds(start, wk), :] += lax.dot_general(ds2, q2, _TN, preferred_element_type=F32) * scale
        dv_ref[pl.ds(start, wk), :] += lax.dot_general(jnp.concatenate(pbs, axis=0), do2, _TN,
                                                       preferred_element_type=F32)

    row_spec = pl.BlockSpec((tq, LANES), lambda p, i: (i, p))
    kv_spec = pl.BlockSpec((pad + S, LANES), lambda p, i: (0, p))
    bias_spec = pl.BlockSpec((2, tq, wk), lambda p, i: (p, 0, 0))
    sink_spec = pl.BlockSpec((1, 2, LANES), lambda p, i: (p, 0, 0))
    in_specs = [row_spec, kv_spec, kv_spec, row_spec, bias_spec]
    args = [q, k, v, do, bias_w]
    out_specs = [row_spec, kv_spec, kv_spec, bias_spec]
    W = N_HEADS * HEAD_DIM
    out_shape = [jax.ShapeDtypeStruct((S, W), F32), jax.ShapeDtypeStruct((pad + S, W), F32),
                 jax.ShapeDtypeStruct((pad + S, W), F32), jax.ShapeDtypeStruct((N_HEADS, tq, wk), F32)]
    if has_sink:
        in_specs.append(sink_spec)
        args.append(sinks)
        out_specs.append(sink_spec)
        out_shape.append(jax.ShapeDtypeStruct((N_HEADS // 2, 2, LANES), F32))
    res = pl.pallas_call(
        body, name=name, grid=(n_p, n_i), in_specs=in_specs + r_in, out_specs=out_specs + r_out,
        out_shape=out_shape + r_shapes, scratch_shapes=r_sems,
        compiler_params=pltpu.CompilerParams(dimension_semantics=("arbitrary", "arbitrary"), vmem_limit_bytes=VMEM_LIMIT,
                                             has_side_effects=rider is not None),
    )(*args, *r_args)
    return res[:n_own_out], res[n_own_out:]


HP = SSM_HEADS // 2
PAIRS_PER_GROUP = HP // SSM_GROUPS
HEADS_PER_GROUP = SSM_HEADS // SSM_GROUPS
GW = HEADS_PER_GROUP * 64


def _ssd_dt(dtraw, dtb, A, tril):
    lane = lax.broadcasted_iota(jnp.int32, (1, LANES), 1)
    u = dtraw + dtb
    eu = jnp.exp(-jnp.abs(u))
    w1 = 1.0 + eu
    l1p = jnp.where(w1 == 1.0, eu, jnp.log(w1) * eu / jnp.where(w1 == 1.0, 1.0, w1 - 1.0))
    dt = jnp.where(lane < SSM_HEADS, jnp.maximum(u, 0.0) + l1p, 0.0)
    acs = jnp.dot(tril, dt * A, precision=HI, preferred_element_type=F32)
    return u, dt, acs


def _head_expander():
    hw = D_INNER // SSM_HEADS
    return (np.arange(LANES)[:, None] == np.arange(D_INNER)[None, :] // hw).astype(np.float32)


def _select_dot(t, sel):
    hi = t.astype(BF16)
    lo = (t - hi.astype(F32)).astype(BF16)
    return jnp.dot(hi, sel, preferred_element_type=F32) + jnp.dot(lo, sel, preferred_element_type=F32)


def ssd_fwd(xbc, dtraw, dtb, A, dexp, *, name):
    S = xbc.shape[0]
    L = min(SSD_L, S)
    nc = S // L
    N = SSM_STATE
    e_mat = jnp.asarray(_head_expander(), dtype=BF16)

    def body(xs_ref, b_ref, c_ref, dtr_ref, dtb_ref, a_ref, d_ref, e_ref, y_ref, st_out_ref, st_ref, xw_ref):
        c = pl.program_id(0)

        @pl.when(c == 0)
        def _():
            st_ref[...] = jnp.zeros_like(st_ref)

        st_out_ref[0] = st_ref[...]
        ri = lax.broadcasted_iota(jnp.int32, (L, L), 0)
        ci = lax.broadcasted_iota(jnp.int32, (L, L), 1)
        trilb = ri >= ci
        head0 = lax.broadcasted_iota(jnp.int32, (1, LANES), 1) < 64
        _, dt, acs = _ssd_dt(dtr_ref[...], dtb_ref[...], a_ref[...], trilb.astype(F32))
        acsT = acs.T
        last = acs[L - 1:L, :]
        expand = lambda t: _select_dot(t, e_ref[...])
        dte, eae, wte = expand(dt), expand(jnp.exp(acs)), expand(jnp.exp(last - acs) * dt)
        lasts = [last[:, h:h + 1] for h in range(SSM_HEADS)]
        for g in range(SSM_GROUPS):
            Bg = b_ref[:, g * N:(g + 1) * N].astype(BF16)
            Cg = c_ref[:, g * N:(g + 1) * N].astype(BF16)
            CB = lax.dot_general(Cg, Bg, _NT, preferred_element_type=F32)
            Z = lax.dot_general(Cg, st_ref[g * GW:(g + 1) * GW, :].astype(BF16), _NT, preferred_element_type=F32)
            for q in range(PAIRS_PER_GROUP):
                hp = g * PAIRS_PER_GROUP + q
                sl = slice(hp * LANES, (hp + 1) * LANES)
                xs = xs_ref[:, sl]
                xd = xs * dte[:, sl]
                ms, xh = [], []
                for r in range(2):
                    h = 2 * hp + r
                    dec = jnp.exp(jnp.where(trilb, acs[:, h:h + 1] - acsT[h:h + 1, :], -jnp.inf))
                    ms.append((CB * dec).astype(BF16))
                    xh.append(jnp.where(head0 if r == 0 else jnp.logical_not(head0), xd, 0.0).astype(BF16))
                yi = jnp.dot(jnp.concatenate(ms, axis=1), jnp.concatenate(xh, axis=0), preferred_element_type=F32)
                y_ref[:, sl] = yi + Z[:, q * LANES:(q + 1) * LANES] * eae[:, sl] + d_ref[:, sl] * xs
                xw_ref[:, sl] = (xs * wte[:, sl]).astype(BF16)
        for g in range(SSM_GROUPS):
            Bg = b_ref[:, g * N:(g + 1) * N].astype(BF16)
            sn = lax.dot_general(xw_ref[:, g * GW:(g + 1) * GW], Bg, _TN, preferred_element_type=F32)
            for k in range(HEADS_PER_GROUP):
                h = g * HEADS_PER_GROUP + k
                rows = slice(h * 64, (h + 1) * 64)
                st_ref[rows, :] = st_ref[rows, :] * jnp.exp(lasts[h]) + sn[k * 64:(k + 1) * 64, :]

    return pl.pallas_call(
        body, name=name, grid=(nc,),
        in_specs=[pl.BlockSpec((L, D_INNER), lambda c: (c, 0)),
                  pl.BlockSpec((L, SSM_GROUPS * N), lambda c: (c, D_INNER // (SSM_GROUPS * N))),
                  pl.BlockSpec((L, SSM_GROUPS * N), lambda c: (c, D_INNER // (SSM_GROUPS * N) + 1)),
                  pl.BlockSpec((L, LANES), lambda c: (c, 0)),
                  pl.BlockSpec((1, LANES), lambda c: (0, 0)),
                  pl.BlockSpec((1, LANES), lambda c: (0, 0)),
                  pl.BlockSpec((1, D_INNER), lambda c: (0, 0)),
                  pl.BlockSpec((LANES, D_INNER), lambda c: (0, 0))],
        out_specs=[pl.BlockSpec((L, D_INNER), lambda c: (c, 0)),
                   pl.BlockSpec((1, D_INNER, N), lambda c: (c, 0, 0))],
        out_shape=[jax.ShapeDtypeStruct((S, D_INNER), F32), jax.ShapeDtypeStruct((nc, D_INNER, N), F32)],
        scratch_shapes=[pltpu.VMEM((D_INNER, N), F32), pltpu.VMEM((L, D_INNER), BF16)],
        compiler_params=_params("arbitrary"),
    )(xbc, xbc, xbc, dtraw, dtb, A, dexp, e_mat)


def ssd_bwd(xbc, dtraw, dtb, A, dexp, states, dy, *, name, rider=None):
    S = xbc.shape[0]
    L = min(SSD_L, S)
    nc = S // L
    N = SSM_STATE
    e_np = _head_expander()
    e_mat, et_mat = jnp.asarray(e_np, dtype=BF16), jnp.asarray(e_np.T, dtype=BF16)

    r_in, r_out, r_shapes, r_sems, r_args = _rider_parts(rider)

    def body(*refs):
        xs_ref, b_ref, c_ref, dtr_ref, dtb_ref, a_ref, d_ref, e_ref, et_ref, st_in_ref, dy_ref = refs[:11]
        o0 = 11 + len(r_in)
        dxs_ref, db_ref, dc_ref, ddtr_ref, da_ref, ddtb_ref, dd_ref = refs[o0:o0 + 7]
        s0 = o0 + 7 + len(r_out)
        dst_ref, xw_ref, dz_ref, r_ref, dsr_ref, dsc_ref = refs[s0:s0 + 6]
        step = pl.program_id(0)
        if rider is not None:
            _ride(rider, refs[11:o0], refs[o0 + 7:s0], refs[s0 + 6:], step == 0, step == nc // 2, step == nc - 1)

        @pl.when(step == 0)
        def _():
            dst_ref[...] = jnp.zeros_like(dst_ref)
            dsr_ref[...] = jnp.zeros_like(dsr_ref)
            dsc_ref[...] = jnp.zeros_like(dsc_ref)
            da_ref[...] = jnp.zeros_like(da_ref)
            ddtb_ref[...] = jnp.zeros_like(ddtb_ref)
            dd_ref[...] = jnp.zeros_like(dd_ref)

        ri = lax.broadcasted_iota(jnp.int32, (L, L), 0)
        ci = lax.broadcasted_iota(jnp.int32, (L, L), 1)
        trilb = ri >= ci
        lane = lax.broadcasted_iota(jnp.int32, (1, LANES), 1)
        sub = lax.broadcasted_iota(jnp.int32, (LANES, 1), 0)
        head0 = lane < 64
        A = a_ref[...]
        u, dt, acs = _ssd_dt(dtr_ref[...], dtb_ref[...], A, trilb.astype(F32))
        acsT = acs.T
        last = acs[L - 1:L, :]
        elast = jnp.exp(last)
        er = jnp.exp(last - acs)
        wt = er * dt
        expand = lambda t: _select_dot(t, e_ref[...])
        dte, eae, wte = expand(dt), expand(jnp.exp(acs)), expand(wt)
        dlast = jnp.zeros((1, LANES), F32)
        dcbs = []
        for g in range(SSM_GROUPS):
            Bg = b_ref[:, g * N:(g + 1) * N].astype(BF16)
            Cg = c_ref[:, g * N:(g + 1) * N].astype(BF16)
            stg = st_in_ref[0, g * GW:(g + 1) * GW, :]
            dstg = dst_ref[g * GW:(g + 1) * GW, :]
            CB = lax.dot_general(Cg, Bg, _NT, preferred_element_type=F32)
            CBT = lax.dot_general(Bg, Cg, _NT, preferred_element_type=F32)
            Z = lax.dot_general(Cg, stg.astype(BF16), _NT, preferred_element_type=F32)
            U = lax.dot_general(Bg, dstg.astype(BF16), _NT, preferred_element_type=F32)
            dcb = jnp.zeros((L, L), F32)
            for q in range(PAIRS_PER_GROUP):
                hp = g * PAIRS_PER_GROUP + q
                sl = slice(hp * LANES, (hp + 1) * LANES)
                qs = slice(q * LANES, (q + 1) * LANES)
                xs = xs_ref[:, sl]
                dyp = dy_ref[:, sl]
                dtp, eap, wp, Dp = dte[:, sl], eae[:, sl], wte[:, sl], d_ref[:, sl]
                xd = xs * dtp
                dy2 = jnp.concatenate([jnp.where(head0, dyp, 0.0), jnp.where(head0, 0.0, dyp)], axis=0).astype(BF16)
                G2 = lax.dot_general(dy2, xd.astype(BF16), _NT, preferred_element_type=F32)
                mts = []
                for r in range(2):
                    h = 2 * hp + r
                    seg = acs[:, h:h + 1] - acsT[h:h + 1, :]
                    dec = jnp.exp(jnp.where(trilb, seg, -jnp.inf))
                    decT = jnp.exp(jnp.where(ri <= ci, -seg, -jnp.inf))
                    gd = G2[r * L:(r + 1) * L, :] * dec
                    dcb = dcb + gd
                    dseg = gd * CB
                    dsr_ref[:, h:h + 1] = jnp.sum(dseg, axis=1, keepdims=True)
                    dsc_ref[h:h + 1, :] = jnp.sum(dseg, axis=0, keepdims=True)
                    mts.append((CBT * decT).astype(BF16))
                dxd = jnp.dot(jnp.concatenate(mts, axis=1), dy2, preferred_element_type=F32)
                Up = U[:, qs]
                r_ref[0, :, sl] = dyp * Z[:, qs] * eap
                r_ref[1, :, sl] = dxd * xs
                r_ref[2, :, sl] = Up * xs
                dz_ref[:, sl] = (dyp * eap).astype(BF16)
                xw_ref[:, sl] = (xs * wp).astype(BF16)
                dxs_ref[:, sl] = dxd * dtp + Dp * dyp + Up * wp
                dd_ref[:, sl] += jnp.sum(dyp * xs, axis=0, keepdims=True)
            dcbs.append(dcb)
            t = dstg * stg
            for k in range(HEADS_PER_GROUP):
                dlast = dlast + jnp.where(lane == g * HEADS_PER_GROUP + k,
                                          jnp.sum(t[k * 64:(k + 1) * 64, :], keepdims=True), 0.0)
        fold = lambda k: _select_dot(r_ref[k], et_ref[...])
        r1, r2, dws = fold(0), fold(1), fold(2)
        dww = dws * wt
        ddt = r2 + dws * er
        dacs = r1 - dww + dsr_ref[...] - dsc_ref[...].T
        dlast = dlast * elast + jnp.sum(dww, axis=0, keepdims=True)
        lasts = [last[:, h:h + 1] for h in range(SSM_HEADS)]
        for g in range(SSM_GROUPS):
            Bg = b_ref[:, g * N:(g + 1) * N].astype(BF16)
            Cg = c_ref[:, g * N:(g + 1) * N].astype(BF16)
            gs = slice(g * GW, (g + 1) * GW)
            stb = st_in_ref[0, gs, :].astype(BF16)
            dstb = dst_ref[gs, :].astype(BF16)
            dcbb = dcbs[g].astype(BF16)
            dzg = dz_ref[:, gs]
            dc_ref[:, g * N:(g + 1) * N] = (jnp.dot(dzg, stb, preferred_element_type=F32)
                                            + jnp.dot(dcbb, Bg, preferred_element_type=F32))
            db_ref[:, g * N:(g + 1) * N] = (jnp.dot(xw_ref[:, gs], dstb, preferred_element_type=F32)
                                            + lax.dot_general(dcbb, Cg, _TN, preferred_element_type=F32))
            dsn = lax.dot_general(dzg, Cg, _TN, preferred_element_type=F32)
            for k in range(HEADS_PER_GROUP):
                h = g * HEADS_PER_GROUP + k
                rows = slice(h * 64, (h + 1) * 64)
                dst_ref[rows, :] = dst_ref[rows, :] * jnp.exp(lasts[h]) + dsn[k * 64:(k + 1) * 64, :]
        rowi = lax.broadcasted_iota(jnp.int32, (L, 1), 0)
        dacs = dacs + jnp.where(rowi == L - 1, dlast, 0.0)
        da = jnp.dot((ci >= ri).astype(F32), dacs, precision=HI, preferred_element_type=F32)
        ddt = ddt + da * A
        da_ref[...] += jnp.sum(da * dt, axis=0, keepdims=True)
        ddtr = jnp.where(lane < SSM_HEADS, ddt * _sigmoid(u), 0.0)
        ddtr_ref[...] = ddtr
        ddtb_ref[...] += jnp.sum(ddtr, axis=0, keepdims=True)

    rev = lambda c: nc - 1 - c
    gn = SSM_GROUPS * N
    res = pl.pallas_call(
        body, name=name, grid=(nc,),
        in_specs=[pl.BlockSpec((L, D_INNER), lambda c: (rev(c), 0)),
                  pl.BlockSpec((L, gn), lambda c: (rev(c), D_INNER // gn)),
                  pl.BlockSpec((L, gn), lambda c: (rev(c), D_INNER // gn + 1)),
                  pl.BlockSpec((L, LANES), lambda c: (rev(c), 0)),
                  pl.BlockSpec((1, LANES), lambda c: (0, 0)),
                  pl.BlockSpec((1, LANES), lambda c: (0, 0)),
                  pl.BlockSpec((1, D_INNER), lambda c: (0, 0)),
                  pl.BlockSpec((LANES, D_INNER), lambda c: (0, 0)),
                  pl.BlockSpec((D_INNER, LANES), lambda c: (0, 0)),
                  pl.BlockSpec((1, D_INNER, N), lambda c: (rev(c), 0, 0)),
                  pl.BlockSpec((L, D_INNER), lambda c: (rev(c), 0))] + r_in,
        out_specs=[pl.BlockSpec((L, D_INNER), lambda c: (rev(c), 0)),
                   pl.BlockSpec((L, gn), lambda c: (rev(c), 0)),
                   pl.BlockSpec((L, gn), lambda c: (rev(c), 0)),
                   pl.BlockSpec((L, LANES), lambda c: (rev(c), 0)),
                   pl.BlockSpec((1, LANES), lambda c: (0, 0)),
                   pl.BlockSpec((1, LANES), lambda c: (0, 0)),
                   pl.BlockSpec((1, D_INNER), lambda c: (0, 0))] + r_out,
        out_shape=[jax.ShapeDtypeStruct((S, D_INNER), F32), jax.ShapeDtypeStruct((S, gn), F32),
                   jax.ShapeDtypeStruct((S, gn), F32), jax.ShapeDtypeStruct((S, LANES), F32),
                   jax.ShapeDtypeStruct((1, LANES), F32), jax.ShapeDtypeStruct((1, LANES), F32),
                   jax.ShapeDtypeStruct((1, D_INNER), F32)] + r_shapes,
        scratch_shapes=[pltpu.VMEM((D_INNER, N), F32), pltpu.VMEM((L, D_INNER), BF16), pltpu.VMEM((L, D_INNER), BF16),
                        pltpu.VMEM((3, L, D_INNER), F32), pltpu.VMEM((L, LANES), F32), pltpu.VMEM((LANES, L), F32)]
                       + r_sems,
        compiler_params=pltpu.CompilerParams(dimension_semantics=("arbitrary",), vmem_limit_bytes=VMEM_LIMIT,
                                             has_side_effects=rider is not None),
    )(xbc, xbc, xbc, dtraw, dtb, A, dexp, e_mat, et_mat, states, dy, *r_args)
    return res[:7], res[7:]


BAND_A = (A_PREV + 1) * CHUNK
REL_W = 640


def _relpos_select():
    k = np.arange(REL_W)
    rel = np.where(k < BAND_A, A_PREV * CHUNK - k, A_PREV * CHUNK - (k - REL_W))
    idx = np.clip(rel, -MAX_REL, MAX_REL) + MAX_REL
    sel = (np.arange(REL_W)[:, None] == idx[None, :]) & (k != BAND_A)[None, :]
    return sel.astype(np.float32)


def relpos_bias(table_pad, *, name):
    def body(t_ref, s_ref, o_ref):
        v = jnp.dot(t_ref[...], s_ref[...], precision=HI, preferred_element_type=F32)
        for h in range(N_HEADS):
            o_ref[h] = pltpu.roll(jnp.broadcast_to(v[h:h + 1, :], (CHUNK, REL_W)), 0, 1, stride=1, stride_axis=0)

    return pl.pallas_call(body, name=name, out_shape=jax.ShapeDtypeStruct((N_HEADS, CHUNK, REL_W), F32),
                          compiler_params=pltpu.CompilerParams(vmem_limit_bytes=VMEM_LIMIT),
                          )(table_pad, jnp.asarray(_relpos_select()))


def relpos_grad(dbias_rev, *, name):
    def body(d_ref, s_ref, o_ref):
        head = lax.broadcasted_iota(jnp.int32, (N_HEADS, 1), 0)
        dv = jnp.zeros((N_HEADS, REL_W), F32)
        for h in range(N_HEADS):
            back = pltpu.roll(d_ref[h], REL_W - (CHUNK - 1), 1, stride=1, stride_axis=0)
            dv = dv + jnp.where(head == h, jnp.sum(back, axis=0, keepdims=True), 0.0)
        o_ref[...] = lax.dot_general(dv, s_ref[...], _NT, precision=HI, preferred_element_type=F32)

    return pl.pallas_call(body, name=name, out_shape=jax.ShapeDtypeStruct((N_HEADS, REL_W), F32),
                          compiler_params=pltpu.CompilerParams(vmem_limit_bytes=VMEM_LIMIT),
                          )(dbias_rev, jnp.asarray(_relpos_select()))


def loss_head(y, t, *, name, tm=256):
    S, D = y.shape
    tm = min(tm, S)

    def body(y_ref, t_ref, dy_ref, l_ref):
        e = y_ref[...] - t_ref[...]
        dy_ref[...] = e * (1.0 / D)

        @pl.when(pl.program_id(0) == 0)
        def _():
            l_ref[...] = jnp.zeros_like(l_ref)

        part = jnp.sum(jnp.sum(e * e, axis=1, keepdims=True), axis=0, keepdims=True) * (0.5 / D)
        l_ref[...] += jnp.broadcast_to(part, l_ref.shape)

    return pl.pallas_call(
        body, name=name, grid=(S // tm,),
        in_specs=[pl.BlockSpec((tm, D), lambda i: (i, 0))] * 2,
        out_specs=[pl.BlockSpec((tm, D), lambda i: (i, 0)), pl.BlockSpec((1, LANES), lambda i: (0, 0))],
        out_shape=[jax.ShapeDtypeStruct((S, D), F32), jax.ShapeDtypeStruct((1, LANES), F32)],
        compiler_params=_params("arbitrary"),
    )(y, t)


def f_adamw(w, g, m, v):
    m = ADAM_B1 * m + (1.0 - ADAM_B1) * g
    v = ADAM_B2 * v + (1.0 - ADAM_B2) * (g * g)
    m_hat = m / (1.0 - ADAM_B1 ** ADAM_STEP)
    v_hat = v / (1.0 - ADAM_B2 ** ADAM_STEP)
    delta = -ADAM_LR * (m_hat / (jnp.sqrt(v_hat) + ADAM_EPS) + ADAM_WD * w)
    return delta, m, v


ANY = pl.BlockSpec(memory_space=pl.ANY)


def _pos():
    return lax.axis_index("x"), lax.axis_index("y"), lax.axis_index("c")


def _other_chips(x, y):
    return [(1 - x, y), (x, 1 - y), (1 - x, 1 - y)]


class Rider:
    def __init__(self, ins, outs, sems, start, mid, finish):
        self.ins, self.outs, self.sems = list(ins), list(outs), list(sems)
        self.start, self.mid, self.finish = start, mid, finish


def _rider_parts(rider):
    if rider is None:
        return [], [], [], [], []
    return [ANY] * len(rider.ins), [ANY] * len(rider.outs), rider.outs, rider.sems, rider.ins


def _ride(rider, ins, outs, sems, first, mid, last):
    pos = _pos()

    @pl.when(first)
    def _():
        rider.start(pos, ins, outs, sems)

    if rider.mid is not None:
        @pl.when(mid)
        def _():
            rider.mid(pos, ins, outs, sems)

    @pl.when(last)
    def _():
        rider.finish(pos, ins, outs, sems)


def run_rider(rider, *, name):
    n_in, n_out = len(rider.ins), len(rider.outs)

    def body(*refs):
        ins, outs, sems = refs[:n_in], refs[n_in:n_in + n_out], refs[n_in + n_out:]
        pos = _pos()
        rider.start(pos, ins, outs, sems)
        if rider.mid is not None:
            rider.mid(pos, ins, outs, sems)
        rider.finish(pos, ins, outs, sems)

    return pl.pallas_call(
        body, name=name, in_specs=[ANY] * n_in, out_specs=[ANY] * n_out, out_shape=rider.outs,
        scratch_shapes=rider.sems, compiler_params=pltpu.CompilerParams(has_side_effects=True),
    )(*rider.ins)


def gather_rider(shards):
    n = len(shards)

    def copies(pos, ins, outs, sems):
        x, y, c = pos
        send, recv, fsend, frecv = sems
        me = 2 * x + y
        sib = (x, y, 1 - c)
        first, arrive, passed, theirs = [], [], [], []
        for i in range(n):
            for j, (px, py) in enumerate(_other_chips(x, y)):
                k = 3 * i + j
                far = dict(device_id=(px, py, c), device_id_type=MESH)
                near = dict(device_id=sib, device_id_type=MESH)
                got = outs[i].at[2 * px + py, c]
                his = outs[i].at[2 * px + py, 1 - c]
                first.append(pltpu.make_async_remote_copy(ins[i].at[c], outs[i].at[me, c], send.at[k], recv.at[k], **far))
                arrive.append(pltpu.make_async_remote_copy(ins[i].at[c], got, send.at[k], recv.at[k], **far))
                passed.append(pltpu.make_async_remote_copy(got, got, fsend.at[k], frecv.at[k], **near))
                theirs.append(pltpu.make_async_remote_copy(his, his, fsend.at[k], frecv.at[k], **near))
        return first, arrive, passed, theirs

    def start(*a):
        for cp in copies(*a)[0]:
            cp.start()

    def mid(*a):
        _, arrive, passed, _ = copies(*a)
        for got, cp in zip(arrive, passed):
            got.wait_recv()
            cp.start()

    def finish(*a):
        first, _, passed, theirs = copies(*a)
        for cp in theirs:
            cp.wait_recv()
        for cp in first + passed:
            cp.wait_send()

    return Rider(shards, [jax.ShapeDtypeStruct((4,) + s.shape, s.dtype) for s in shards],
                 [pltpu.SemaphoreType.DMA((3 * n,))] * 4, start, mid, finish)


def scatter_rider(ps):
    n = len(ps)

    def copies(pos, ins, outs, sems):
        x, y, c = pos
        send, recv = sems
        return [pltpu.make_async_remote_copy(ins[i].at[2 * px + py], outs[i].at[j], send.at[3 * i + j], recv.at[3 * i + j],
                                             device_id=(px, py, c), device_id_type=MESH)
                for i in range(n) for j, (px, py) in enumerate(_other_chips(x, y))]

    def start(*a):
        for cp in copies(*a):
            cp.start()

    def finish(*a):
        for cp in copies(*a):
            cp.wait()

    return Rider(ps, [jax.ShapeDtypeStruct((3,) + p.shape[1:], p.dtype) for p in ps],
                 [pltpu.SemaphoreType.DMA((3 * n,))] * 2, start, None, finish)


def pair_swap_halves(gs, *, name):
    n = len(gs)

    def body(*refs):
        ins, outs = refs[:n], refs[n:2 * n]
        send, recv = refs[2 * n:]
        x, y, c = _pos()
        cps = []
        for i in range(n):
            cp = pltpu.make_async_remote_copy(ins[i].at[1 - c], outs[i], send.at[i], recv.at[i],
                                              device_id=(x, y, 1 - c), device_id_type=MESH)
            cp.start()
            cps.append(cp)
        for cp in cps:
            cp.wait()

    return pl.pallas_call(
        body, name=name, in_specs=[ANY] * n, out_specs=[ANY] * n,
        out_shape=[jax.ShapeDtypeStruct(g.shape[1:], g.dtype) for g in gs],
        scratch_shapes=[pltpu.SemaphoreType.DMA((n,)), pltpu.SemaphoreType.DMA((n,))],
        compiler_params=pltpu.CompilerParams(has_side_effects=True),
    )(*gs)


def pair_share(hs, *, name):
    n = len(hs)

    def body(*refs):
        ins, outs = refs[:n], refs[n:2 * n]
        send, recv = refs[2 * n:]
        x, y, c = _pos()
        cps = []
        for i in range(n):
            cp = pltpu.make_async_remote_copy(ins[i], outs[i], send.at[i], recv.at[i],
                                              device_id=(x, y, 1 - c), device_id_type=MESH)
            cp.start()
            cps.append(cp)
        for cp in cps:
            cp.wait()

    return pl.pallas_call(
        body, name=name, in_specs=[ANY] * n, out_specs=[ANY] * n,
        out_shape=[jax.ShapeDtypeStruct(h.shape, h.dtype) for h in hs],
        scratch_shapes=[pltpu.SemaphoreType.DMA((n,)), pltpu.SemaphoreType.DMA((n,))],
        compiler_params=pltpu.CompilerParams(has_side_effects=True),
    )(*hs)


def gather_all(buf, *, name):
    def body(in_ref, out_ref, send, recv, loc):
        x, y, c = _pos()
        lid = 4 * x + 2 * y + c
        lc = pltpu.make_async_copy(in_ref, out_ref.at[lid], loc.at[0])
        lc.start()
        cps = []
        for k in range(1, 8):
            px = 1 - x if k & 4 else x
            py = 1 - y if k & 2 else y
            pc = 1 - c if k & 1 else c
            cp = pltpu.make_async_remote_copy(in_ref, out_ref.at[lid], send.at[k - 1], recv.at[k - 1],
                                              device_id=(px, py, pc), device_id_type=MESH)
            cp.start()
            cps.append((cp, 4 * px + 2 * py + pc, (px, py, pc)))
        for k, (cp, plid, peer) in enumerate(cps):
            cp.wait_send()
            pltpu.make_async_remote_copy(in_ref, out_ref.at[plid], send.at[k], recv.at[k],
                                         device_id=peer, device_id_type=MESH).wait_recv()
        lc.wait()

    return pl.pallas_call(
        body, name=name, in_specs=[ANY], out_specs=ANY,
        out_shape=jax.ShapeDtypeStruct((8,) + buf.shape, buf.dtype),
        scratch_shapes=[pltpu.SemaphoreType.DMA((7,)), pltpu.SemaphoreType.DMA((7,)), pltpu.SemaphoreType.DMA((1,))],
        compiler_params=pltpu.CompilerParams(has_side_effects=True),
    )(buf)


def sum_slots(a, *, name):
    n = a.shape[0]

    def body(a_ref, o_ref):
        acc = a_ref[0]
        for k in range(1, n):
            acc = acc + a_ref[k]
        o_ref[...] = acc

    return pl.pallas_call(body, name=name, out_shape=jax.ShapeDtypeStruct(a.shape[1:], a.dtype),
                          compiler_params=pltpu.CompilerParams(vmem_limit_bytes=VMEM_LIMIT))(a)


def _row_tile(r, want, mult=16):
    t = (min(want, r) // mult) * mult
    while t >= mult:
        if r % t == 0:
            return t
        t -= mult
    return r


def pair_add(g, r1, csel, *, name):
    _, _, r, C = g.shape
    tr = _row_tile(r, 256)

    def body(g_ref, r_ref, c_ref, p32_ref, pb_ref):
        south = c_ref[0:1, 0:1] == 0.0
        p = jnp.where(south, g_ref[0, 0], g_ref[1, 0]) + r_ref[0]
        p32_ref[0] = p
        pb_ref[0] = p.astype(BF16)

    return pl.pallas_call(
        body, name=name, grid=(4, r // tr),
        in_specs=[pl.BlockSpec((2, 1, tr, C), lambda j, t: (0, j, t, 0)), pl.BlockSpec((1, tr, C), lambda j, t: (j, t, 0)),
                  pl.BlockSpec((1, LANES), lambda j, t: (0, 0))],
        out_specs=[pl.BlockSpec((1, tr, C), lambda j, t: (j, t, 0))] * 2,
        out_shape=[jax.ShapeDtypeStruct((4, r, C), F32), jax.ShapeDtypeStruct((4, r, C), BF16)],
        compiler_params=_params("parallel", "parallel"),
    )(g, r1, csel)


def chip_add(p32, r3, msel, *, name):
    _, r, C = p32.shape
    tr = _row_tile(r, 128)

    def body(p_ref, r_ref, m_ref, o_ref):
        me = m_ref[0:1, 0:1]
        acc = jnp.where(me == 0.0, p_ref[0], jnp.where(me == 1.0, p_ref[1], jnp.where(me == 2.0, p_ref[2], p_ref[3])))
        for j in range(3):
            acc = acc + r_ref[j].astype(F32)
        o_ref[...] = acc

    return pl.pallas_call(
        body, name=name, grid=(r // tr,),
        in_specs=[pl.BlockSpec((4, tr, C), lambda t: (0, t, 0)), pl.BlockSpec((3, tr, C), lambda t: (0, t, 0)),
                  pl.BlockSpec((1, LANES), lambda t: (0, 0))],
        out_specs=pl.BlockSpec((tr, C), lambda t: (t, 0)),
        out_shape=jax.ShapeDtypeStruct((r, C), F32),
        compiler_params=_params("parallel"),
    )(p32, r3, msel)


def _consts():
    i512 = np.arange(N_HEADS * HEAD_DIM)
    i128 = np.arange(LANES)
    bd512 = (i512[:, None] // HEAD_DIM == i512[None, :] // HEAD_DIM).astype(np.float32)
    bd128 = (i128[:, None] // HEAD_DIM == i128[None, :] // HEAD_DIM).astype(np.float32)
    fold = (np.arange(HEAD_DIM)[:, None] == (i512[None, :] % HEAD_DIM)).astype(np.float32)
    grp = N_HEADS // 2 * HEAD_DIM
    expand = ((i128[:, None] // HEAD_DIM == i512[None, :] // grp)
              & (i128[:, None] % HEAD_DIM == i512[None, :] % HEAD_DIM)).astype(np.float32)
    band = (B_PREV + 1) * CHUNK
    rel = np.arange(CHUNK)[:, None] - (np.arange(band)[None, :] - B_PREV * CHUNK)
    slopes = 2.0 ** (-8.0 * np.arange(1, N_HEADS + 1, dtype=np.float32) / N_HEADS)
    bias_b = (-slopes[:, None, None] * np.abs(rel).astype(np.float32)[None]).astype(np.float32)
    return [jnp.asarray(a) for a in (bd512, bd128, fold, expand)], jnp.asarray(bias_b)


def _ffn_fwd(xin, h, l, W, P, next_gain):
    Wi = W["ffn_in"][l]
    gate = matmul(h, Wi[:, :D_FF], mode="nn", name=f"ffn{l}_gate", out_dtype=BF16)
    up = matmul(h, Wi[:, D_FF:], mode="nn", name=f"ffn{l}_up", out_dtype=BF16)
    gc, act = dwconv_fwd(gate, P["ffn_conv_w"][l], P["ffn_conv_b"][l:l + 1], lambda y, u: (y, _silu(y) * u), [up],
                         [BF16, BF16], name=f"ffn{l}_conv")
    saved = (xin, h, gate, gc, up, act)
    if next_gain is None:
        return matmul(act, W["ffn_out"][l], mode="nn", name=f"ffn{l}_out", residual=xin), saved, None
    xout, h_next = matmul_norm(act, W["ffn_out"][l], xin, next_gain, name=f"ffn{l}_out")
    return xout, saved, h_next


def _ffn_bwd(dxout, l, saved, W, P):
    xin, h, gate, gc, up, act = saved
    g = P["norm_ffn"][l:l + 1]
    Wi = W["ffn_in"][l]
    dact = matmul(dxout, W["ffn_out"][l], mode="nt", name=f"ffn{l}_dact", out_dtype=BF16)
    dWo = matmul(act, dxout, mode="tn", name=f"ffn{l}_dwout")
    dgate, dcw, dcb, dup = dwconv_bwd(gate, P["ffn_conv_w"][l], [gc, up, dact],
                                      lambda c, u, da: (da * u * _dsilu(c), da * _silu(c)), [BF16],
                                      name=f"ffn{l}_dconv")
    dh = matmul(dgate, Wi[:, :D_FF], mode="nt", name=f"ffn{l}_dh_gate")
    dxin, dg = matmul_dnorm(dup, Wi[:, D_FF:], dh, xin, g, dxout, name=f"ffn{l}_dh_up")
    dWi = jnp.concatenate([matmul(h, dgate, mode="tn", name=f"ffn{l}_dw_gate"),
                           matmul(h, dup, mode="tn", name=f"ffn{l}_dw_up")], axis=1)
    return dxin, dWi, dWo, dg, dcw, dcb


class NoComm:
    def fwd_rider(self, tag):
        return None

    def fwd_done(self, tag, outs, W, P):
        pass

    def grads(self, tag, cols, rows):
        return None

    def bwd_done(self, tag, outs):
        pass


def local_step(x, tgt, W, P, comm):
    qk_consts, bias_b = _consts()
    pad_rows = lambda t, n: jnp.pad(t, ((n * CHUNK, 0), (0, 0)))
    DQ = N_HEADS * HEAD_DIM

    g_mix0 = P["norm_mix"][0:1]
    (h0,) = rowwise(f_rmsnorm, [x], [g_mix0], [(D_MODEL, BF16)], name="attn_norm")
    qkv = matmul(h0, W["attn_in"], mode="nn", name="attn_qkv")
    qk_par = [P["q_norm_a"], P["k_norm_a"], P["q_norm_b"], P["k_norm_b"]] + qk_consts
    qa, ka, va, qb, kb, vb = rowwise(f_qknorm, [qkv], qk_par, [(DQ, BF16)] * 6, name="attn_qknorm")
    ka, va, kb, vb = pad_rows(ka, A_PREV), pad_rows(va, A_PREV), pad_rows(kb, B_PREV), pad_rows(vb, B_PREV)
    table = jnp.pad(P["relpos_table"], ((0, 0), (0, REL_W - (2 * MAX_REL + 1))))
    nj = min(ATT_TQ, x.shape[0]) // CHUNK
    bias_a = widen_bias(relpos_bias(table, name="relpos_bias")[:, :, :BAND_A], A_PREV, nj)
    bias_b = widen_bias(bias_b, B_PREV, nj)
    sinks = jnp.broadcast_to(P["sinks"].reshape(N_HEADS // 2, 2, 1), (N_HEADS // 2, 2, LANES))
    oa, late = attn_fwd(qa, ka, va, bias_a, None, n_prev=A_PREV, name="attn_a", rider=comm.fwd_rider("a"))
    comm.fwd_done("a", late, W, P)
    ob, late = attn_fwd(qb, kb, vb, bias_b, sinks, n_prev=B_PREV, name="attn_b", rider=comm.fwd_rider("b"))
    comm.fwd_done("b", late, W, P)
    Wao = W["attn_out"]
    x1 = matmul(oa, Wao[:DQ], mode="nn", name="attn_out_a", residual=x)
    x1, hf0 = matmul_norm(ob, Wao[DQ:], x1, P["norm_ffn"][0:1], name="attn_out_b")
    g_mix1 = P["norm_mix"][1:2]
    x2, ffn0, h2 = _ffn_fwd(x1, hf0, 0, W, P, g_mix1)

    Ws = W["ssm_in"]
    CC = D_INNER + 2 * SSM_GROUPS * SSM_STATE
    Wz, Wx = Ws[:, :D_INNER], Ws[:, D_INNER:D_INNER + CC]
    Wdt = jnp.pad(Ws[:, D_INNER + CC:], ((0, 0), (0, LANES - SSM_HEADS)))
    z = matmul(h2, Wz, mode="nn", name="ssm_z", out_dtype=BF16)
    xr = matmul(h2, Wx, mode="nn", name="ssm_xbc", out_dtype=BF16)
    dtraw = matmul(h2, Wdt, mode="nn", name="ssm_dt")
    xc, xbc = dwconv_fwd(xr, P["ssm_conv_w"], P["ssm_conv_b"], lambda y: (y, _silu(y)), [], [BF16, F32],
                         name="ssm_conv")
    pad32 = lambda v: jnp.pad(v, ((0, 0), (0, LANES - SSM_HEADS)))
    A = pad32(-jnp.exp(P["ssm_a_log"]))
    dtb = pad32(P["ssm_dt_bias"])
    dexp = jnp.repeat(P["ssm_d"], D_INNER // SSM_HEADS, axis=1)
    y, states = ssd_fwd(xbc, dtraw, dtb, A, dexp, name="ssd_fwd")
    (y2,) = rowwise(f_gate_norm, [y, z], [P["ssm_norm"]], [(D_INNER, BF16)], name="ssm_gate_norm")
    x3, hf1 = matmul_norm(y2, W["ssm_out"], x2, P["norm_ffn"][1:2], name="ssm_out")
    x4, ffn1, _ = _ffn_fwd(x3, hf1, 1, W, P, None)

    dx4, lpart = loss_head(x4, tgt, name="loss_head")

    dx3, dWfi1, dWfo1, dgf1, dfcw1, dfcb1 = _ffn_bwd(dx4, 1, ffn1, W, P)
    out_f1 = comm.grads("f1", dWfi1, dWfo1)
    dy2 = matmul(dx3, W["ssm_out"], mode="nt", name="ssm_dy")
    dWso = matmul(y2, dx3, mode="tn", name="ssm_dwout")
    dy, dz, dnw = rowwise_vjp(f_gate_norm, [y, z], [P["ssm_norm"]], [dy2], [(0, F32), (1, BF16)], [0],
                              name="ssm_dgate_norm")
    (dxs, dB, dC, ddtraw, dA, ddtb, dDl), sent = ssd_bwd(xbc, dtraw, dtb, A, dexp, states, dy, name="ssd_bwd",
                                                          rider=out_f1)
    comm.bwd_done("f1", sent)
    dxr, dscw, dscb = dwconv_bwd(xr, P["ssm_conv_w"], [xc, (dxs, dB, dC)], lambda c, g: (g * _dsilu(c),), [],
                                 name="ssm_dconv")
    dh2 = matmul(dz, Wz, mode="nt", name="ssm_dh_z")
    dh2 = matmul(dxr, Wx, mode="nt", name="ssm_dh_x", residual=dh2)
    dx2, dgm1 = matmul_dnorm(ddtraw, Wdt, dh2, x2, g_mix1, dx3, name="ssm_dh_dt")
    dWs = jnp.concatenate([matmul(h2, dz, mode="tn", name="ssm_dw_z"),
                           matmul(h2, dxr, mode="tn", name="ssm_dw_x"),
                           matmul(h2, ddtraw, mode="tn", name="ssm_dw_dt")[:, :SSM_HEADS]], axis=1)
    out_s = comm.grads("s", dWs, dWso)

    dx1, dWfi0, dWfo0, dgf0, dfcw0, dfcb0 = _ffn_bwd(dx2, 0, ffn0, W, P)
    out_f0 = comm.grads("f0", dWfi0, dWfo0)
    doa = matmul(dx1, Wao[:DQ], mode="nt", name="attn_do_a", out_dtype=BF16)
    dob = matmul(dx1, Wao[DQ:], mode="nt", name="attn_do_b", out_dtype=BF16)
    dWao = jnp.concatenate([matmul(oa, dx1, mode="tn", name="attn_dwout_a"),
                            matmul(ob, dx1, mode="tn", name="attn_dwout_b")], axis=0)
    (dqa, dka, dva, dbias_a), sent = attn_bwd(qa, ka, va, doa, bias_a, None, n_prev=A_PREV, name="attn_a_bwd",
                                              rider=out_s)
    comm.bwd_done("s", sent)
    (dqb, dkb, dvb, _, dsk), sent = attn_bwd(qb, kb, vb, dob, bias_b, sinks, n_prev=B_PREV, name="attn_b_bwd",
                                             rider=out_f0)
    comm.bwd_done("f0", sent)
    pa, pb = A_PREV * CHUNK, B_PREV * CHUNK
    dqkv, dgqa, dgka, dgqb, dgkb = rowwise_vjp(f_qknorm, [qkv], qk_par, [dqa, dka, dva, dqb, dkb[pb:], dvb[pb:]],
                                               [(0, BF16)], [0, 1, 2, 3], name="attn_dqknorm",
                                               cot_skip=[0, pa, pa, 0, 0, 0])
    dx, dgm0 = matmul_dnorm(dqkv, W["attn_in"], None, x, g_mix0, dx1, name="attn_dh")
    dWai = matmul(h0, dqkv, mode="tn", name="attn_dwin")
    dbias_a = fold_bias(dbias_a, A_PREV, nj)
    dbias_rev = jnp.pad(dbias_a[:, ::-1, :], ((0, 0), (0, 0), (0, REL_W - BAND_A)))
    dtable = relpos_grad(dbias_rev, name="relpos_grad")[:, :2 * MAX_REL + 1]

    gW = {"attn_in": dWai, "attn_out": dWao, "ssm_in": dWs, "ssm_out": dWso,
          "ffn_in": [dWfi0, dWfi1], "ffn_out": [dWfo0, dWfo1]}
    gP = {"norm_mix": jnp.concatenate([dgm0, dgm1], axis=0),
          "norm_ffn": jnp.concatenate([dgf0, dgf1], axis=0),
          "relpos_table": dtable, "q_norm_a": dgqa, "k_norm_a": dgka, "q_norm_b": dgqb, "k_norm_b": dgkb,
          "sinks": dsk[:, :, 0].reshape(1, N_HEADS),
          "ssm_conv_w": dscw, "ssm_conv_b": dscb,
          "ssm_dt_bias": ddtb[:, :SSM_HEADS], "ssm_a_log": dA[:, :SSM_HEADS] * A[:, :SSM_HEADS],
          "ssm_d": dDl.reshape(SSM_HEADS, D_INNER // SSM_HEADS).sum(axis=1).reshape(1, SSM_HEADS),
          "ssm_norm": dnw,
          "ffn_conv_w": jnp.stack([dfcw0, dfcw1]), "ffn_conv_b": jnp.concatenate([dfcb0, dfcb1], axis=0)}
    return lpart, dx, gW, gP


WEIGHTS = ["norm_mix", "norm_ffn", "attn_w_in", "attn_w_out", "relpos_table", "q_norm_a", "k_norm_a", "q_norm_b",
           "k_norm_b", "sinks", "ssm_w_in", "ssm_conv_w", "ssm_conv_b", "ssm_dt_bias", "ssm_a_log", "ssm_d",
           "ssm_norm", "ssm_w_out", "ffn_w_in", "ffn_conv_w", "ffn_conv_b", "ffn_w_out"]
ARGS = ["x"] + WEIGHTS + ["loss_target"] + ["m_" + w for w in WEIGHTS] + ["v_" + w for w in WEIGHTS]
N_CHIPS = 4
SMALL_ROWS = 384
SMALL_ORDER = ["norm_mix", "norm_ffn", "relpos_table", "q_norm_a", "k_norm_a", "q_norm_b", "k_norm_b", "sinks",
               "ssm_dt_bias", "ssm_a_log", "ssm_d", "ffn_conv_b", "ssm_conv_w", "ssm_conv_b", "ssm_norm", "ffn_conv_w"]


def _cols_to_slabs(g):
    K, N = g.shape
    return g.reshape(2, K // 2, N_CHIPS, N // N_CHIPS).transpose(0, 2, 1, 3)


def _rows_to_slabs(g):
    R, C = g.shape
    return g.reshape(N_CHIPS, 2, R // (2 * N_CHIPS), C).transpose(1, 0, 2, 3)


class MeshComm:
    def __init__(self, d, xi, yi, ci):
        self.d, self.ci, self.me = d, ci, 2 * xi + yi
        self.csel = jnp.full((1, LANES), ci, F32)
        self.msel = jnp.full((1, LANES), self.me, F32)
        halves = lambda w: w.reshape((2, -1, w.shape[-1]))
        small = jnp.concatenate([d[k].reshape(-1) for k in ("ssm_conv_w", "ssm_conv_b", "ssm_norm", "ffn_conv_w")])
        small = jnp.pad(small, (0, 2 * 40 * LANES - small.shape[0])).reshape(2, 40, LANES)
        self.shards = {"attn": [halves(d["attn_w_in"][0].astype(BF16)), halves(d["attn_w_out"][0].astype(BF16))],
                       "a": [d["ffn_w_in"].astype(BF16), small],
                       "b": [d["ffn_w_out"].astype(BF16), halves(d["ssm_w_in"][0].astype(BF16)),
                             halves(d["ssm_w_out"][0].astype(BF16))]}
        self.p32, self.mine = {}, {}

    def _whole(self, tag, outs):
        return [lax.dynamic_update_slice_in_dim(g, s[None], self.me, axis=0) for g, s in zip(outs, self.shards[tag])]

    @staticmethod
    def _cat_cols(g):
        return jnp.concatenate([g[j].reshape((-1, g.shape[-1])) for j in range(N_CHIPS)], axis=1)

    def first_weights(self):
        g_ai, g_ao = self._whole("attn", run_rider(gather_rider(self.shards["attn"]), name="gather_attn"))
        return {"attn_in": self._cat_cols(g_ai), "attn_out": g_ao.reshape(-1, D_MODEL)}

    def fwd_rider(self, tag):
        return gather_rider(self.shards[tag])

    def fwd_done(self, tag, outs, W, P):
        if tag == "b":
            g_fo, g_si, g_so = self._whole("b", outs)
            W["ffn_out"] = [g_fo[:, l].reshape(-1, D_MODEL) for l in range(2)]
            W["ssm_in"], W["ssm_out"] = self._cat_cols(g_si), g_so.reshape(-1, D_MODEL)
            return
        g_fi, g_sm = self._whole("a", outs)
        W["ffn_in"] = [jnp.concatenate([g_fi[j, l] for j in range(N_CHIPS)], axis=1) for l in range(2)]
        sm = g_sm.reshape(N_CHIPS, -1)
        CC = D_INNER + 2 * SSM_GROUPS * SSM_STATE
        c4, f4 = CC // N_CHIPS, D_FF // N_CHIPS
        o1 = SSM_CONV * c4
        o2 = o1 + c4
        o3 = o2 + D_INNER // N_CHIPS
        o4 = o3 + 2 * FFN_CONV * f4
        P["ssm_conv_w"] = sm[:, :o1].reshape(N_CHIPS, SSM_CONV, c4).transpose(1, 0, 2).reshape(SSM_CONV, CC)
        P["ssm_conv_b"] = sm[:, o1:o2].reshape(1, CC)
        P["ssm_norm"] = sm[:, o2:o3].reshape(1, D_INNER)
        P["ffn_conv_w"] = sm[:, o3:o4].reshape(N_CHIPS, 2, FFN_CONV, f4).transpose(1, 2, 0, 3).reshape(2, FFN_CONV, D_FF)

    def grads(self, tag, cols, rows):
        slabs = [_cols_to_slabs(cols), _rows_to_slabs(rows)]
        from_sib = pair_swap_halves(slabs, name="grad_pair_swap_" + tag)
        pairs = [pair_add(g, r, self.csel, name=f"grad_pair_add_{tag}{i}") for i, (g, r) in enumerate(zip(slabs, from_sib))]
        self.p32[tag] = [p[0] for p in pairs]
        return scatter_rider([p[1] for p in pairs])

    def bwd_done(self, tag, outs):
        self.mine[tag] = [chip_add(p, r, self.msel, name=f"grad_chip_add_{tag}{i}")
                          for i, (p, r) in enumerate(zip(self.p32[tag], outs))]

    def finish(self, d_attn_in, d_attn_out):
        self.bwd_done("at", run_rider(self.grads("at", d_attn_in, d_attn_out), name="grad_scatter_at"))
        order = ["at", "s", "f0", "f1"]
        mine = [m for t in order for m in self.mine[t]]
        theirs = pair_share(mine, name="grad_pair_share")
        full = [jnp.where(self.ci == 0, jnp.stack([a, b]), jnp.stack([b, a])).reshape((-1, a.shape[-1]))
                for a, b in zip(mine, theirs)]
        ai, ao, si, so, fi0, fo0, fi1, fo1 = full
        return {"attn_w_in": ai[None], "attn_w_out": ao[None], "ssm_w_in": si[None], "ssm_w_out": so[None],
                "ffn_w_in": jnp.stack([fi0, fi1]), "ffn_w_out": jnp.stack([fo0, fo1])}


def _adamw(w, g, m, v, name):
    shp = w.shape
    two = lambda a: a.reshape((-1, shp[-1]))
    outs = [(shp[-1], F32)] * 3
    d, nm, nv = rowwise(f_adamw, [two(w), two(g), two(m), two(v)], [], outs, name="adamw_" + name)
    return d.reshape(shp), nm.reshape(shp), nv.reshape(shp)


def kernel(x, norm_mix, norm_ffn, attn_w_in, attn_w_out, relpos_table, q_norm_a, k_norm_a, q_norm_b, k_norm_b, sinks, ssm_w_in, ssm_conv_w, ssm_conv_b, ssm_dt_bias, ssm_a_log, ssm_d, ssm_norm, ssm_w_out, ffn_w_in, ffn_conv_w, ffn_conv_b, ffn_w_out, loss_target, m_norm_mix, m_norm_ffn, m_attn_w_in, m_attn_w_out, m_relpos_table, m_q_norm_a, m_k_norm_a, m_q_norm_b, m_k_norm_b, m_sinks, m_ssm_w_in, m_ssm_conv_w, m_ssm_conv_b, m_ssm_dt_bias, m_ssm_a_log, m_ssm_d, m_ssm_norm, m_ssm_w_out, m_ffn_w_in, m_ffn_conv_w, m_ffn_conv_b, m_ffn_w_out, v_norm_mix, v_norm_ffn, v_attn_w_in, v_attn_w_out, v_relpos_table, v_q_norm_a, v_k_norm_a, v_q_norm_b, v_k_norm_b, v_sinks, v_ssm_w_in, v_ssm_conv_w, v_ssm_conv_b, v_ssm_dt_bias, v_ssm_a_log, v_ssm_d, v_ssm_norm, v_ssm_w_out, v_ffn_w_in, v_ffn_conv_w, v_ffn_conv_b, v_ffn_w_out):
    d = dict(zip(ARGS, (x, norm_mix, norm_ffn, attn_w_in, attn_w_out, relpos_table, q_norm_a, k_norm_a, q_norm_b, k_norm_b, sinks, ssm_w_in, ssm_conv_w, ssm_conv_b, ssm_dt_bias, ssm_a_log, ssm_d, ssm_norm, ssm_w_out, ffn_w_in, ffn_conv_w, ffn_conv_b, ffn_w_out, loss_target, m_norm_mix, m_norm_ffn, m_attn_w_in, m_attn_w_out, m_relpos_table, m_q_norm_a, m_k_norm_a, m_q_norm_b, m_k_norm_b, m_sinks, m_ssm_w_in, m_ssm_conv_w, m_ssm_conv_b, m_ssm_dt_bias, m_ssm_a_log, m_ssm_d, m_ssm_norm, m_ssm_w_out, m_ffn_w_in, m_ffn_conv_w, m_ffn_conv_b, m_ffn_w_out, v_norm_mix, v_norm_ffn, v_attn_w_in, v_attn_w_out, v_relpos_table, v_q_norm_a, v_k_norm_a, v_q_norm_b, v_k_norm_b, v_sinks, v_ssm_w_in, v_ssm_conv_w, v_ssm_conv_b, v_ssm_dt_bias, v_ssm_a_log, v_ssm_d, v_ssm_norm, v_ssm_w_out, v_ffn_w_in, v_ffn_conv_w, v_ffn_conv_b, v_ffn_w_out)))
    xi, yi, ci = _pos()
    me = 2 * xi + yi
    CC = D_INNER + 2 * SSM_GROUPS * SSM_STATE
    c4, f4 = CC // N_CHIPS, D_FF // N_CHIPS

    P = {k: d[k] for k in ["norm_mix", "norm_ffn", "q_norm_a", "k_norm_a", "q_norm_b", "k_norm_b", "sinks",
                           "ssm_dt_bias", "ssm_a_log", "ssm_d", "ffn_conv_b"]}
    P["relpos_table"] = d["relpos_table"][0]
    comm = MeshComm(d, xi, yi, ci)
    W = comm.first_weights()
    lpart, dx, gW, gP = local_step(d["x"][0], d["loss_target"][0], W, P, comm)
    loss = lax.psum(lpart[0, 0], ("x", "y", "c"))
    grads = comm.finish(gW["attn_in"], gW["attn_out"])

    flat = jnp.concatenate([gP[k].reshape(-1) for k in SMALL_ORDER])
    flat = jnp.pad(flat, (0, SMALL_ROWS * LANES - flat.shape[0])).reshape(SMALL_ROWS, LANES)
    tot = sum_slots(gather_all(flat, name="small_gather"), name="small_sum").reshape(-1)
    off = 0
    for k in SMALL_ORDER:
        n = int(np.prod(gP[k].shape))
        g = tot[off:off + n].reshape(gP[k].shape)
        off += n
        if k == "ssm_conv_w":
            g = lax.dynamic_slice_in_dim(g, me * c4, c4, axis=1)[None]
        elif k == "ssm_conv_b":
            g = lax.dynamic_slice_in_dim(g, me * c4, c4, axis=1)
        elif k == "ssm_norm":
            g = lax.dynamic_slice_in_dim(g, me * (D_INNER // N_CHIPS), D_INNER // N_CHIPS, axis=1)
        elif k == "ffn_conv_w":
            g = lax.dynamic_slice_in_dim(g, me * f4, f4, axis=2)
        elif k == "relpos_table":
            g = g[None]
        grads[k] = g

    deltas, new_m, new_v = {}, {}, {}
    for k in WEIGHTS:
        deltas[k], new_m[k], new_v[k] = _adamw(d[k], grads[k], d["m_" + k], d["v_" + k], k)
    return (loss, dx[None], *[grads[k] for k in WEIGHTS], *[deltas[k] for k in WEIGHTS],
            *[new_m[k] for k in WEIGHTS], *[new_v[k] for k in WEIGHTS])
```

```python
import functools

import numpy as np
import jax
import jax.numpy as jnp
from jax import lax
from jax.experimental import pallas as pl
from jax.experimental.pallas import tpu as pltpu

F32 = jnp.float32
BF16 = jnp.bfloat16
HI = lax.Precision.HIGHEST

D_MODEL = 1024
CHUNK = 64
EPS = 1e-6
HEAD_DIM = 64
N_HEADS = 8
A_PREV = 8
B_PREV = 2
MAX_REL = 256
D_INNER = 2048
SSM_HEADS = 32
SSM_GROUPS = 4
SSM_STATE = 128
SSM_CONV = 4
D_FF = 2816
FFN_CONV = 3
LANES = 128
SUBLANES = 8
VMEM_LIMIT = 56 * 1024 * 1024
SSD_L = 128

ADAM_LR = 0.001
ADAM_B1 = 0.9
ADAM_B2 = 0.999
ADAM_EPS = 1e-08
ADAM_WD = 0.01
ADAM_STEP = 10

MESH = pl.DeviceIdType.MESH


def _params(*sem):
    return pltpu.CompilerParams(dimension_semantics=sem, vmem_limit_bytes=VMEM_LIMIT)


def _pick(n, want):
    if n <= want:
        return n
    t = (want // LANES) * LANES
    while t >= LANES:
        if n % t == 0:
            return t
        t -= LANES
    return n


MM_ROWS = 512
MM_COLS = 1536
MM_RED = 2048
MM_SHORT = 1024


def matmul(a, b, *, mode, name, out_dtype=F32, residual=None):
    dims = {"nn": (((1,), (0,)), ((), ())), "nt": (((1,), (1,)), ((), ())), "tn": (((0,), (0,)), ((), ()))}[mode]
    if mode == "tn":
        assert residual is None and out_dtype == F32
        (K, M), (K2, N) = a.shape, b.shape
        assert K == K2, (a.shape, b.shape)
        tm, tn, tk = _pick(M, MM_COLS), _pick(N, MM_COLS), _pick(K, MM_RED)

        def body(a_ref, b_ref, o_ref):
            k = pl.program_id(2)
            p = lax.dot_general(a_ref[...].astype(BF16), b_ref[...].astype(BF16), dims, preferred_element_type=F32)

            @pl.when(k == 0)
            def _():
                o_ref[...] = p

            @pl.when(k != 0)
            def _():
                o_ref[...] += p

        return pl.pallas_call(
            body, name=name, grid=(M // tm, N // tn, K // tk),
            in_specs=[pl.BlockSpec((tk, tm), lambda i, j, k: (k, i)), pl.BlockSpec((tk, tn), lambda i, j, k: (k, j))],
            out_specs=pl.BlockSpec((tm, tn), lambda i, j, k: (i, j)),
            out_shape=jax.ShapeDtypeStruct((M, N), F32),
            compiler_params=_params("parallel", "parallel", "arbitrary"),
        )(a, b)

    if mode == "nn":
        (M, K), (K2, N) = a.shape, b.shape
    else:
        (M, K), (N, K2) = a.shape, b.shape
    assert K == K2, (a.shape, b.shape, mode)
    tm, tn = _pick(M, MM_ROWS if K > MM_SHORT else 2 * MM_ROWS), _pick(N, MM_COLS)

    def body(*refs):
        a_ref, b_ref = refs[:2]
        o_ref = refs[-1]
        r = lax.dot_general(a_ref[...].astype(BF16), b_ref[...].astype(BF16), dims, preferred_element_type=F32)
        if residual is not None:
            r = r + refs[2][...].astype(F32)
        o_ref[...] = r.astype(o_ref.dtype)

    a_spec = pl.BlockSpec((tm, K), lambda j, i: (i, 0))
    b_spec = pl.BlockSpec((K, tn), lambda j, i: (0, j)) if mode == "nn" else pl.BlockSpec((tn, K), lambda j, i: (j, 0))
    o_spec = pl.BlockSpec((tm, tn), lambda j, i: (i, j))
    in_specs = [a_spec, b_spec] + ([o_spec] if residual is not None else [])
    args = (a, b) + ((residual,) if residual is not None else ())
    return pl.pallas_call(
        body, name=name, grid=(N // tn, M // tm),
        in_specs=in_specs, out_specs=o_spec,
        out_shape=jax.ShapeDtypeStruct((M, N), out_dtype),
        compiler_params=_params("parallel", "parallel"),
    )(*args)


def matmul_norm(a, b, residual, g, *, name):
    (M, K), (_, N) = a.shape, b.shape
    tm = _pick(M, MM_ROWS)

    def body(a_ref, b_ref, r_ref, g_ref, x_ref, h_ref):
        x = jnp.dot(a_ref[...].astype(BF16), b_ref[...].astype(BF16), preferred_element_type=F32) + r_ref[...]
        x_ref[...] = x
        h_ref[...] = f_rmsnorm(x, g_ref[...])[0].astype(BF16)

    row = pl.BlockSpec((tm, N), lambda i: (i, 0))
    return pl.pallas_call(
        body, name=name, grid=(M // tm,),
        in_specs=[pl.BlockSpec((tm, K), lambda i: (i, 0)), pl.BlockSpec((K, N), lambda i: (0, 0)), row,
                  pl.BlockSpec((1, N), lambda i: (0, 0))],
        out_specs=[row, row],
        out_shape=[jax.ShapeDtypeStruct((M, N), F32), jax.ShapeDtypeStruct((M, N), BF16)],
        compiler_params=_params("parallel"),
    )(a, b, residual, g)


def matmul_dnorm(a, b, partial, x, g, dres, *, name):
    (M, K), (N, _) = a.shape, b.shape
    tm = _pick(M, MM_ROWS)
    has_part = partial is not None

    def body(*refs):
        a_ref, b_ref = refs[:2]
        x_ref, g_ref, dres_ref, dx_ref, dg_ref = refs[-5:]
        dh = lax.dot_general(a_ref[...].astype(BF16), b_ref[...].astype(BF16), _NT, preferred_element_type=F32)
        if has_part:
            dh = dh + refs[2][...]
        xv = x_ref[...]
        r = lax.rsqrt(jnp.mean(xv * xv, axis=-1, keepdims=True) + EPS)
        xhat = xv * r
        dxh = dh * g_ref[...]
        dx_ref[...] = dres_ref[...] + r * (dxh - xhat * jnp.mean(dxh * xhat, axis=-1, keepdims=True))
        dg = jnp.sum(dh * xhat, axis=0, keepdims=True)

        @pl.when(pl.program_id(0) == 0)
        def _():
            dg_ref[...] = dg

        @pl.when(pl.program_id(0) != 0)
        def _():
            dg_ref[...] += dg

    row = pl.BlockSpec((tm, N), lambda i: (i, 0))
    vec = pl.BlockSpec((1, N), lambda i: (0, 0))
    in_specs = [pl.BlockSpec((tm, K), lambda i: (i, 0)), pl.BlockSpec((N, K), lambda i: (0, 0))]
    args = [a, b]
    if has_part:
        in_specs.append(row)
        args.append(partial)
    return pl.pallas_call(
        body, name=name, grid=(M // tm,), in_specs=in_specs + [row, vec, row], out_specs=[row, vec],
        out_shape=[jax.ShapeDtypeStruct((M, N), F32), jax.ShapeDtypeStruct((1, N), F32)],
        compiler_params=_params("arbitrary"),
    )(*args, x, g, dres)


def rowwise(f, rows, params, outs, *, name, tm=256):
    S = rows[0].shape[0]
    tm = _row_tile(S, tm)
    nr, npar = len(rows), len(params)

    def body(*refs):
        vals = [r[...].astype(F32) for r in refs[:nr + npar]]
        res = f(*vals)
        for o_ref, r in zip(refs[nr + npar:], res):
            o_ref[...] = r.astype(o_ref.dtype)

    in_specs = [pl.BlockSpec((tm, r.shape[1]), lambda i: (i, 0)) for r in rows]
    in_specs += [pl.BlockSpec(p.shape, lambda i: (0, 0)) for p in params]
    out_specs = [pl.BlockSpec((tm, c), lambda i: (i, 0)) for c, _ in outs]
    out_shape = [jax.ShapeDtypeStruct((S, c), dt) for c, dt in outs]
    return pl.pallas_call(body, name=name, grid=(S // tm,), in_specs=in_specs, out_specs=out_specs,
                          out_shape=out_shape, compiler_params=_params("parallel"))(*rows, *params)


def rowwise_vjp(f, rows, params, cots, drow, dpar, *, name, tm=256, cot_skip=None):
    S = rows[0].shape[0]
    tm = _row_tile(S, tm)
    nr, npar, nc = len(rows), len(params), len(cots)
    skip = [0] * nc if cot_skip is None else [s // tm for s in cot_skip]
    assert cot_skip is None or all(s % tm == 0 for s in cot_skip)

    def body(*refs):
        vals = [r[...].astype(F32) for r in refs[:nr + npar]]
        cvals = [r[...].astype(F32) for r in refs[nr + npar:nr + npar + nc]]
        o_refs = refs[nr + npar + nc:]
        want = [ri for ri, _ in drow] + [nr + pi for pi in dpar]

        def f_want(*d):
            full = list(vals)
            for k, v in zip(want, d):
                full[k] = v
            return f(*full)

        _, vjp = jax.vjp(f_want, *[vals[k] for k in want])
        grads = vjp(tuple(cvals))
        for o_ref, g in zip(o_refs[:len(drow)], grads):
            o_ref[...] = g.astype(o_ref.dtype)
        first = pl.program_id(0) == 0
        for o_ref, g in zip(o_refs[len(drow):], grads[len(drow):]):
            g = g.astype(F32)

            @pl.when(first)
            def _(o_ref=o_ref, g=g):
                o_ref[...] = g

            @pl.when(jnp.logical_not(first))
            def _(o_ref=o_ref, g=g):
                o_ref[...] += g

    in_specs = [pl.BlockSpec((tm, r.shape[1]), lambda i: (i, 0)) for r in rows]
    in_specs += [pl.BlockSpec(p.shape, lambda i: (0, 0)) for p in params]
    in_specs += [pl.BlockSpec((tm, c.shape[1]), lambda i, s=s: (i + s, 0)) for c, s in zip(cots, skip)]
    out_specs = [pl.BlockSpec((tm, rows[ri].shape[1]), lambda i: (i, 0)) for ri, _ in drow]
    out_specs += [pl.BlockSpec(params[pi].shape, lambda i: (0, 0)) for pi in dpar]
    out_shape = [jax.ShapeDtypeStruct(rows[ri].shape, dt) for ri, dt in drow]
    out_shape += [jax.ShapeDtypeStruct(params[pi].shape, F32) for pi in dpar]
    return pl.pallas_call(body, name=name, grid=(S // tm,), in_specs=in_specs, out_specs=out_specs,
                          out_shape=out_shape, compiler_params=_params("arbitrary"))(*rows, *params, *cots)


HALO = 2 * SUBLANES


def dwconv_fwd(x, w, b, post, extra, outs, *, name, tm=256):
    S, C = x.shape
    K = w.shape[0]
    tm = min(tm, S)
    hb = tm // HALO
    ne = len(extra)

    def body(*refs):
        x_ref, halo_ref, w_ref, b_ref = refs[:4]
        e_refs = refs[4:4 + ne]
        o_refs = refs[4 + ne:4 + ne + len(outs)]
        buf = refs[-1]
        i = pl.program_id(0)
        buf[0:HALO, :] = jnp.where(i == 0, 0.0, halo_ref[...].astype(F32))
        buf[HALO:HALO + tm, :] = x_ref[...].astype(F32)
        for c0 in range(0, C, LANES):
            cs = slice(c0, c0 + LANES)
            acc = jnp.broadcast_to(b_ref[:, cs], (tm, LANES))
            for k in range(K):
                acc = acc + w_ref[k:k + 1, cs] * buf[pl.ds(HALO - (K - 1) + k, tm), cs]
            for o_ref, r in zip(o_refs, post(acc, *[e[:, cs].astype(F32) for e in e_refs])):
                o_ref[:, cs] = r.astype(o_ref.dtype)

    row = pl.BlockSpec((tm, C), lambda i: (i, 0))
    return pl.pallas_call(
        body, name=name, grid=(S // tm,),
        in_specs=[row,
                  pl.BlockSpec((HALO, C), lambda i: (jnp.maximum(i * hb - 1, 0), 0)),
                  pl.BlockSpec((K, C), lambda i: (0, 0)),
                  pl.BlockSpec((1, C), lambda i: (0, 0))] + [row] * ne,
        out_specs=[row] * len(outs),
        out_shape=[jax.ShapeDtypeStruct((S, C), dt) for dt in outs],
        scratch_shapes=[pltpu.VMEM((HALO + tm, C), F32)],
        compiler_params=_params("parallel"),
    )(x, x, w, b, *extra)


def dwconv_bwd(x, w, srcs, dy_fn, extra_outs, *, name, tm=256):
    S, C = x.shape
    K = w.shape[0]
    tm = min(tm, S)
    hb = tm // HALO
    n = S // tm
    groups = [s if isinstance(s, tuple) else (s,) for s in srcs]
    flat = [a for g in groups for a in g]
    nf = len(flat)

    def body(*refs):
        x_ref, xh_ref, w_ref = refs[:3]
        dx_ref, dw_ref, db_ref = refs[3 + 2 * nf:6 + 2 * nf]
        e_refs = refs[6 + 2 * nf:6 + 2 * nf + len(extra_outs)]
        bx, bd = refs[-2:]

        def strips(first, c0):
            out, at = [], first
            for g in groups:
                off = 0
                for a in g:
                    if off <= c0 < off + a.shape[1]:
                        out.append(refs[at][:, c0 - off:c0 - off + LANES].astype(F32))
                    off += a.shape[1]
                    at += 1
            return out

        i = pl.program_id(0)
        bx[0:HALO, :] = jnp.where(i == 0, 0.0, xh_ref[...].astype(F32))
        bx[HALO:HALO + tm, :] = x_ref[...].astype(F32)

        @pl.when(i == 0)
        def _():
            dw_ref[...] = jnp.zeros_like(dw_ref)
            db_ref[...] = jnp.zeros_like(db_ref)

        for c0 in range(0, C, LANES):
            cs = slice(c0, c0 + LANES)
            res = dy_fn(*strips(3, c0))
            dyv = res[0]
            for e_ref, r in zip(e_refs, res[1:]):
                e_ref[:, cs] = r.astype(e_ref.dtype)
            bd[0:tm, cs] = dyv
            bd[tm:tm + HALO, cs] = jnp.where(i == n - 1, 0.0, dy_fn(*strips(3 + nf, c0))[0])
            acc = jnp.zeros((tm, LANES), F32)
            for k in range(K):
                acc = acc + w_ref[k:k + 1, cs] * bd[pl.ds((K - 1) - k, tm), cs]
            dx_ref[:, cs] = acc.astype(dx_ref.dtype)
            for k in range(K):
                dw_ref[k:k + 1, cs] += jnp.sum(dyv * bx[pl.ds(HALO - (K - 1) + k, tm), cs], axis=0, keepdims=True)
            db_ref[:, cs] += jnp.sum(dyv, axis=0, keepdims=True)

    row = lambda c: pl.BlockSpec((tm, c), lambda i: (i, 0))
    nxt = lambda c: pl.BlockSpec((HALO, c), lambda i: (jnp.minimum((i + 1) * hb, S // HALO - 1), 0))
    return pl.pallas_call(
        body, name=name, grid=(n,),
        in_specs=[row(C), pl.BlockSpec((HALO, C), lambda i: (jnp.maximum(i * hb - 1, 0), 0)),
                  pl.BlockSpec((K, C), lambda i: (0, 0))]
                 + [row(a.shape[1]) for a in flat] + [nxt(a.shape[1]) for a in flat],
        out_specs=[row(C), pl.BlockSpec((K, C), lambda i: (0, 0)), pl.BlockSpec((1, C), lambda i: (0, 0))]
                  + [row(C)] * len(extra_outs),
        out_shape=[jax.ShapeDtypeStruct((S, C), BF16), jax.ShapeDtypeStruct((K, C), F32),
                   jax.ShapeDtypeStruct((1, C), F32)] + [jax.ShapeDtypeStruct((S, C), dt) for dt in extra_outs],
        scratch_shapes=[pltpu.VMEM((HALO + tm, C), F32), pltpu.VMEM((tm + HALO, C), F32)],
        compiler_params=_params("arbitrary"),
    )(x, x, w, *flat, *flat)


def _sigmoid(x):
    return 0.5 * jnp.tanh(0.5 * x) + 0.5


def _silu(x):
    return x * _sigmoid(x)


def _dsilu(x):
    s = _sigmoid(x)
    return s * (1.0 + x * (1.0 - s))


def f_rmsnorm(x, g):
    return (x * lax.rsqrt(jnp.mean(x * x, axis=-1, keepdims=True) + EPS) * g,)


SEL = lax.Precision.HIGH


def _group_norm(x, bd, width):
    ms = jnp.dot(x * x, bd, precision=SEL, preferred_element_type=F32) * (1.0 / width)
    return x * lax.rsqrt(ms + EPS)


def f_qknorm(qkv, gqa, gka, gqb, gkb, bd512, bd128, fold, expand):
    dq = N_HEADS * HEAD_DIM
    qa, ka, va, qb = (qkv[:, i * dq:(i + 1) * dq] for i in range(4))
    kb = qkv[:, 4 * dq:4 * dq + LANES]
    vb = qkv[:, 4 * dq + LANES:4 * dq + 2 * LANES]
    tile8 = lambda g: jnp.dot(g, fold, precision=HI, preferred_element_type=F32)
    qa = _group_norm(qa, bd512, HEAD_DIM) * tile8(gqa)
    ka = _group_norm(ka, bd512, HEAD_DIM) * tile8(gka)
    qb = _group_norm(qb, bd512, HEAD_DIM) * tile8(gqb)
    kb = _group_norm(kb, bd128, HEAD_DIM) * tile8(gkb)[:, :LANES]
    kb = jnp.dot(kb, expand, precision=SEL, preferred_element_type=F32)
    vb = jnp.dot(vb, expand, precision=SEL, preferred_element_type=F32)
    return qa, ka, va, qb, kb, vb


def f_gate_norm(y, z, nw):
    v = y * _silu(z)
    gw = D_INNER // SSM_GROUPS
    parts = []
    for g in range(SSM_GROUPS):
        vg = v[:, g * gw:(g + 1) * gw]
        parts.append(vg * lax.rsqrt(jnp.mean(vg * vg, axis=-1, keepdims=True) + EPS))
    return (jnp.concatenate(parts, axis=-1) * nw,)


ATT_TQ = 256
_NT = (((1,), (1,)), ((), ()))
_TN = (((0,), (0,)), ((), ()))


def _stack_heads(t, head0):
    return jnp.concatenate([jnp.where(head0, t, 0.0), jnp.where(head0, 0.0, t)], axis=0).astype(BF16)


def _attn_probs(qk, bias, valid, snk):
    s = qk * (HEAD_DIM ** -0.5) + bias
    s = jnp.where(valid, s, -jnp.inf)
    m = jnp.max(s, axis=1, keepdims=True)
    if snk is not None:
        m = jnp.maximum(m, snk)
    e = jnp.exp(s - m)
    den = jnp.sum(e, axis=1, keepdims=True)
    if snk is None:
        return e / den, None
    es = jnp.exp(snk - m)
    den = den + es
    return e / den, es / den


def widen_bias(bias, n_prev, nj):
    band = (n_prev + 1) * CHUNK
    wk = (nj + n_prev) * CHUNK
    rows = [jnp.pad(bias, ((0, 0), (0, 0), (j * CHUNK, wk - band - j * CHUNK)), constant_values=-jnp.inf)
            for j in range(nj)]
    return jnp.concatenate(rows, axis=1)


def fold_bias(dbw, n_prev, nj):
    band = (n_prev + 1) * CHUNK
    acc = dbw[:, :CHUNK, :band]
    for j in range(1, nj):
        acc = acc + dbw[:, j * CHUNK:(j + 1) * CHUNK, j * CHUNK:j * CHUNK + band]
    return acc


def attn_fwd(q, k, v, bias_w, sinks, *, n_prev, name, rider=None):
    S = q.shape[0]
    pad = n_prev * CHUNK
    tq = min(ATT_TQ, S)
    wk = tq + pad
    assert bias_w.shape == (N_HEADS, tq, wk), bias_w.shape
    has_sink = sinks is not None

    r_in, r_out, r_shapes, r_sems, r_args = _rider_parts(rider)
    n_own = 5 if has_sink else 4
    n_p, n_i = N_HEADS // 2, S // tq

    def body(*refs):
        q_ref, k_ref, v_ref, bias_ref = refs[:4]
        sink_ref = refs[4] if has_sink else None
        o_ref = refs[n_own + len(r_in)]
        if rider is not None:
            p_id, i_id = pl.program_id(0), pl.program_id(1)
            _ride(rider, refs[n_own:n_own + len(r_in)], refs[n_own + len(r_in) + 1:n_own + len(r_in) + 1 + len(r_out)],
                  refs[n_own + len(r_in) + 1 + len(r_out):],
                  jnp.logical_and(p_id == 0, i_id == 0), jnp.logical_and(p_id == n_p - 1, i_id == 0),
                  jnp.logical_and(p_id == n_p - 1, i_id == n_i - 1))
        start = pl.multiple_of(pl.program_id(1) * tq, tq)
        head0 = lax.broadcasted_iota(jnp.int32, (1, LANES), 1) < HEAD_DIM
        valid = lax.broadcasted_iota(jnp.int32, (1, wk), 1) + start >= pad
        kb = k_ref[pl.ds(start, wk), :]
        vb = v_ref[pl.ds(start, wk), :]
        qk = lax.dot_general(_stack_heads(q_ref[...].astype(F32), head0), kb, _NT, preferred_element_type=F32)
        ps = []
        for r in range(2):
            snk = sink_ref[0, r:r + 1, 0:1] if has_sink else None
            ps.append(_attn_probs(qk[r * tq:(r + 1) * tq, :], bias_ref[r], valid, snk)[0].astype(BF16))
        o2 = jnp.dot(jnp.concatenate(ps, axis=0), vb, preferred_element_type=F32)
        o_ref[...] = jnp.where(head0, o2[:tq, :], o2[tq:, :]).astype(o_ref.dtype)

    in_specs = [pl.BlockSpec((tq, LANES), lambda p, i: (i, p)),
                pl.BlockSpec((pad + S, LANES), lambda p, i: (0, p)),
                pl.BlockSpec((pad + S, LANES), lambda p, i: (0, p)),
                pl.BlockSpec((2, tq, wk), lambda p, i: (p, 0, 0))]
    args = [q, k, v, bias_w]
    if has_sink:
        in_specs.append(pl.BlockSpec((1, 2, LANES), lambda p, i: (p, 0, 0)))
        args.append(sinks)
    res = pl.pallas_call(
        body, name=name, grid=(n_p, n_i), in_specs=in_specs + r_in,
        out_specs=[pl.BlockSpec((tq, LANES), lambda p, i: (i, p))] + r_out,
        out_shape=[jax.ShapeDtypeStruct((S, N_HEADS * HEAD_DIM), BF16)] + r_shapes,
        scratch_shapes=r_sems,
        compiler_params=pltpu.CompilerParams(dimension_semantics=("arbitrary", "arbitrary"), vmem_limit_bytes=VMEM_LIMIT,
                                             has_side_effects=rider is not None),
    )(*args, *r_args)
    return res[0], res[1:]


def attn_bwd(q, k, v, do, bias_w, sinks, *, n_prev, name, rider=None):
    S = q.shape[0]
    pad = n_prev * CHUNK
    tq = min(ATT_TQ, S)
    wk = tq + pad
    assert bias_w.shape == (N_HEADS, tq, wk), bias_w.shape
    has_sink = sinks is not None
    scale = HEAD_DIM ** -0.5

    r_in, r_out, r_shapes, r_sems, r_args = _rider_parts(rider)
    n_own_in = 6 if has_sink else 5
    n_own_out = 5 if has_sink else 4
    n_p, n_i = N_HEADS // 2, S // tq

    def body(*refs):
        q_ref, k_ref, v_ref, do_ref, bias_ref = refs[:5]
        sink_ref = refs[5] if has_sink else None
        o0 = n_own_in + len(r_in)
        dq_ref, dk_ref, dv_ref, db_ref = refs[o0:o0 + 4]
        dsk_ref = refs[o0 + 4] if has_sink else None
        i = pl.program_id(1)
        if rider is not None:
            p_id = pl.program_id(0)
            _ride(rider, refs[n_own_in:o0], refs[o0 + n_own_out:o0 + n_own_out + len(r_out)],
                  refs[o0 + n_own_out + len(r_out):],
                  jnp.logical_and(p_id == 0, i == 0), jnp.logical_and(p_id == n_p // 2, i == 0),
                  jnp.logical_and(p_id == n_p - 1, i == n_i - 1))

        @pl.when(i == 0)
        def _():
            dk_ref[...] = jnp.zeros_like(dk_ref)
            dv_ref[...] = jnp.zeros_like(dv_ref)
            db_ref[...] = jnp.zeros_like(db_ref)
            if has_sink:
                dsk_ref[...] = jnp.zeros_like(dsk_ref)

        start = pl.multiple_of(i * tq, tq)
        head0 = lax.broadcasted_iota(jnp.int32, (1, LANES), 1) < HEAD_DIM
        valid = lax.broadcasted_iota(jnp.int32, (1, wk), 1) + start >= pad
        kb = k_ref[pl.ds(start, wk), :]
        vb = v_ref[pl.ds(start, wk), :]
        q2 = _stack_heads(q_ref[...].astype(F32), head0)
        do2 = _stack_heads(do_ref[...].astype(F32), head0)
        qk = lax.dot_general(q2, kb, _NT, preferred_element_type=F32)
        dp2 = lax.dot_general(do2, vb, _NT, preferred_element_type=F32)
        pbs, dss = [], []
        for r in range(2):
            rows = slice(r * tq, (r + 1) * tq)
            snk = sink_ref[0, r:r + 1, 0:1] if has_sink else None
            p, ps = _attn_probs(qk[rows, :], bias_ref[r], valid, snk)
            dp = dp2[rows, :]
            delta = jnp.sum(p * dp, axis=1, keepdims=True)
            ds = p * (dp - delta)
            db_ref[r] += ds
            if has_sink:
                dsk = -jnp.sum(ps * delta, axis=0, keepdims=True)
                dsk_ref[0, r:r + 1, :] += jnp.broadcast_to(dsk, (1, LANES))
            pbs.append(p.astype(BF16))
            dss.append(ds.astype(BF16))
        ds2 = jnp.concatenate(dss, axis=0)
        dq2 = jnp.dot(ds2, kb, preferred_element_type=F32) * scale
        dq_ref[...] = jnp.where(head0, dq2[:tq, :], dq2[tq:, :])
        dk_ref[pl.ds(start, wk), :] += lax.dot_general(ds2, q2, _TN, preferred_element_type=F32) * scale
        dv_ref[pl.ds(start, wk), :] += lax.dot_general(jnp.concatenate(pbs, axis=0), do2, _TN,
                                                       preferred_element_type=F32)

    row_spec = pl.BlockSpec((tq, LANES), lambda p, i: (i, p))
    kv_spec = pl.BlockSpec((pad + S, LANES), lambda p, i: (0, p))
    bias_spec = pl.BlockSpec((2, tq, wk), lambda p, i: (p, 0, 0))
    sink_spec = pl.BlockSpec((1, 2, LANES), lambda p, i: (p, 0, 0))
    in_specs = [row_spec, kv_spec, kv_spec, row_spec, bias_spec]
    args = [q, k, v, do, bias_w]
    out_specs = [row_spec, kv_spec, kv_spec, bias_spec]
    W = N_HEADS * HEAD_DIM
    out_shape = [jax.ShapeDtypeStruct((S, W), F32), jax.ShapeDtypeStruct((pad + S, W), F32),
                 jax.ShapeDtypeStruct((pad + S, W), F32), jax.ShapeDtypeStruct((N_HEADS, tq, wk), F32)]
    if has_sink:
        in_specs.append(sink_spec)
        args.append(sinks)
        out_specs.append(sink_spec)
        out_shape.append(jax.ShapeDtypeStruct((N_HEADS // 2, 2, LANES), F32))
    res = pl.pallas_call(
        body, name=name, grid=(n_p, n_i), in_specs=in_specs + r_in, out_specs=out_specs + r_out,
        out_shape=out_shape + r_shapes, scratch_shapes=r_sems,
        compiler_params=pltpu.CompilerParams(dimension_semantics=("arbitrary", "arbitrary"), vmem_limit_bytes=VMEM_LIMIT,
                                             has_side_effects=rider is not None),
    )(*args, *r_args)
    return res[:n_own_out], res[n_own_out:]


HP = SSM_HEADS // 2
PAIRS_PER_GROUP = HP // SSM_GROUPS
HEADS_PER_GROUP = SSM_HEADS // SSM_GROUPS
GW = HEADS_PER_GROUP * 64


def _ssd_dt(dtraw, dtb, A, tril):
    lane = lax.broadcasted_iota(jnp.int32, (1, LANES), 1)
    u = dtraw + dtb
    eu = jnp.exp(-jnp.abs(u))
    w1 = 1.0 + eu
    l1p = jnp.where(w1 == 1.0, eu, jnp.log(w1) * eu / jnp.where(w1 == 1.0, 1.0, w1 - 1.0))
    dt = jnp.where(lane < SSM_HEADS, jnp.maximum(u, 0.0) + l1p, 0.0)
    acs = jnp.dot(tril, dt * A, precision=HI, preferred_element_type=F32)
    return u, dt, acs


def _head_expander():
    hw = D_INNER // SSM_HEADS
    return (np.arange(LANES)[:, None] == np.arange(D_INNER)[None, :] // hw).astype(np.float32)


def _select_dot(t, sel):
    hi = t.astype(BF16)
    lo = (t - hi.astype(F32)).astype(BF16)
    return jnp.dot(hi, sel, preferred_element_type=F32) + jnp.dot(lo, sel, preferred_element_type=F32)


def ssd_fwd(xbc, dtraw, dtb, A, dexp, *, name):
    S = xbc.shape[0]
    L = min(SSD_L, S)
    nc = S // L
    N = SSM_STATE
    e_mat = jnp.asarray(_head_expander(), dtype=BF16)

    def body(xs_ref, b_ref, c_ref, dtr_ref, dtb_ref, a_ref, d_ref, e_ref, y_ref, st_out_ref, st_ref, xw_ref):
        c = pl.program_id(0)

        @pl.when(c == 0)
        def _():
            st_ref[...] = jnp.zeros_like(st_ref)

        st_out_ref[0] = st_ref[...]
        ri = lax.broadcasted_iota(jnp.int32, (L, L), 0)
        ci = lax.broadcasted_iota(jnp.int32, (L, L), 1)
        trilb = ri >= ci
        head0 = lax.broadcasted_iota(jnp.int32, (1, LANES), 1) < 64
        _, dt, acs = _ssd_dt(dtr_ref[...], dtb_ref[...], a_ref[...], trilb.astype(F32))
        acsT = acs.T
        last = acs[L - 1:L, :]
        expand = lambda t: _select_dot(t, e_ref[...])
        tables = expand(jnp.concatenate([dt, jnp.exp(acs), jnp.exp(last - acs) * dt], axis=0))
        dte, eae, wte = tables[:L], tables[L:2 * L], tables[2 * L:]
        lasts = [last[:, h:h + 1] for h in range(SSM_HEADS)]
        for g in range(SSM_GROUPS):
            Bg = b_ref[:, g * N:(g + 1) * N].astype(BF16)
            Cg = c_ref[:, g * N:(g + 1) * N].astype(BF16)
            CB = lax.dot_general(Cg, Bg, _NT, preferred_element_type=F32)
            Z = lax.dot_general(Cg, st_ref[g * GW:(g + 1) * GW, :].astype(BF16), _NT, preferred_element_type=F32)
            for q in range(PAIRS_PER_GROUP):
                hp = g * PAIRS_PER_GROUP + q
                sl = slice(hp * LANES, (hp + 1) * LANES)
                xs = xs_ref[:, sl]
                xd = xs * dte[:, sl]
                ms, xh = [], []
                for r in range(2):
                    h = 2 * hp + r
                    dec = jnp.exp(jnp.where(trilb, acs[:, h:h + 1] - acsT[h:h + 1, :], -jnp.inf))
                    ms.append((CB * dec).astype(BF16))
                    xh.append(jnp.where(head0 if r == 0 else jnp.logical_not(head0), xd, 0.0).astype(BF16))
                yi = jnp.dot(jnp.concatenate(ms, axis=1), jnp.concatenate(xh, axis=0), preferred_element_type=F32)
                y_ref[:, sl] = yi + Z[:, q * LANES:(q + 1) * LANES] * eae[:, sl] + d_ref[:, sl] * xs
                xw_ref[:, sl] = (xs * wte[:, sl]).astype(BF16)
        for g in range(SSM_GROUPS):
            Bg = b_ref[:, g * N:(g + 1) * N].astype(BF16)
            sn = lax.dot_general(xw_ref[:, g * GW:(g + 1) * GW], Bg, _TN, preferred_element_type=F32)
            for k in range(HEADS_PER_GROUP):
                h = g * HEADS_PER_GROUP + k
                rows = slice(h * 64, (h + 1) * 64)
                st_ref[rows, :] = st_ref[rows, :] * jnp.exp(lasts[h]) + sn[k * 64:(k + 1) * 64, :]

    return pl.pallas_call(
        body, name=name, grid=(nc,),
        in_specs=[pl.BlockSpec((L, D_INNER), lambda c: (c, 0)),
                  pl.BlockSpec((L, SSM_GROUPS * N), lambda c: (c, D_INNER // (SSM_GROUPS * N))),
                  pl.BlockSpec((L, SSM_GROUPS * N), lambda c: (c, D_INNER // (SSM_GROUPS * N) + 1)),
                  pl.BlockSpec((L, LANES), lambda c: (c, 0)),
                  pl.BlockSpec((1, LANES), lambda c: (0, 0)),
                  pl.BlockSpec((1, LANES), lambda c: (0, 0)),
                  pl.BlockSpec((1, D_INNER), lambda c: (0, 0)),
                  pl.BlockSpec((LANES, D_INNER), lambda c: (0, 0))],
        out_specs=[pl.BlockSpec((L, D_INNER), lambda c: (c, 0)),
                   pl.BlockSpec((1, D_INNER, N), lambda c: (c, 0, 0))],
        out_shape=[jax.ShapeDtypeStruct((S, D_INNER), F32), jax.ShapeDtypeStruct((nc, D_INNER, N), F32)],
        scratch_shapes=[pltpu.VMEM((D_INNER, N), F32), pltpu.VMEM((L, D_INNER), BF16)],
        compiler_params=_params("arbitrary"),
    )(xbc, xbc, xbc, dtraw, dtb, A, dexp, e_mat)


def ssd_bwd(xbc, dtraw, dtb, A, dexp, states, dy, *, name, rider=None):
    S = xbc.shape[0]
    L = min(SSD_L, S)
    nc = S // L
    N = SSM_STATE
    e_np = _head_expander()
    e_mat, et_mat = jnp.asarray(e_np, dtype=BF16), jnp.asarray(e_np.T, dtype=BF16)

    r_in, r_out, r_shapes, r_sems, r_args = _rider_parts(rider)

    def body(*refs):
        xs_ref, b_ref, c_ref, dtr_ref, dtb_ref, a_ref, d_ref, e_ref, et_ref, st_in_ref, dy_ref = refs[:11]
        o0 = 11 + len(r_in)
        dxs_ref, db_ref, dc_ref, ddtr_ref, da_ref, ddtb_ref, dd_ref = refs[o0:o0 + 7]
        s0 = o0 + 7 + len(r_out)
        dst_ref, xw_ref, dz_ref, r_ref, dsr_ref, dsc_ref = refs[s0:s0 + 6]
        step = pl.program_id(0)
        if rider is not None:
            _ride(rider, refs[11:o0], refs[o0 + 7:s0], refs[s0 + 6:], step == 0, step == nc // 2, step == nc - 1)

        @pl.when(step == 0)
        def _():
            dst_ref[...] = jnp.zeros_like(dst_ref)
            dsr_ref[...] = jnp.zeros_like(dsr_ref)
            dsc_ref[...] = jnp.zeros_like(dsc_ref)
            da_ref[...] = jnp.zeros_like(da_ref)
            ddtb_ref[...] = jnp.zeros_like(ddtb_ref)
            dd_ref[...] = jnp.zeros_like(dd_ref)

        ri = lax.broadcasted_iota(jnp.int32, (L, L), 0)
        ci = lax.broadcasted_iota(jnp.int32, (L, L), 1)
        trilb = ri >= ci
        lane = lax.broadcasted_iota(jnp.int32, (1, LANES), 1)
        sub = lax.broadcasted_iota(jnp.int32, (LANES, 1), 0)
        head0 = lane < 64
        A = a_ref[...]
        u, dt, acs = _ssd_dt(dtr_ref[...], dtb_ref[...], A, trilb.astype(F32))
        acsT = acs.T
        last = acs[L - 1:L, :]
        elast = jnp.exp(last)
        er = jnp.exp(last - acs)
        wt = er * dt
        expand = lambda t: _select_dot(t, e_ref[...])
        tables = expand(jnp.concatenate([dt, jnp.exp(acs), wt], axis=0))
        dte, eae, wte = tables[:L], tables[L:2 * L], tables[2 * L:]
        dlast = jnp.zeros((1, LANES), F32)
        dcbs = []
        for g in range(SSM_GROUPS):
            Bg = b_ref[:, g * N:(g + 1) * N].astype(BF16)
            Cg = c_ref[:, g * N:(g + 1) * N].astype(BF16)
            stg = st_in_ref[0, g * GW:(g + 1) * GW, :]
            dstg = dst_ref[g * GW:(g + 1) * GW, :]
            CB = lax.dot_general(Cg, Bg, _NT, preferred_element_type=F32)
            CBT = lax.dot_general(Bg, Cg, _NT, preferred_element_type=F32)
            Z = lax.dot_general(Cg, stg.astype(BF16), _NT, preferred_element_type=F32)
            U = lax.dot_general(Bg, dstg.astype(BF16), _NT, preferred_element_type=F32)
            dcb = jnp.zeros((L, L), F32)
            for q in range(PAIRS_PER_GROUP):
                hp = g * PAIRS_PER_GROUP + q
                sl = slice(hp * LANES, (hp + 1) * LANES)
                qs = slice(q * LANES, (q + 1) * LANES)
                xs = xs_ref[:, sl]
                dyp = dy_ref[:, sl]
                dtp, eap, wp, Dp = dte[:, sl], eae[:, sl], wte[:, sl], d_ref[:, sl]
                xd = xs * dtp
                dy2 = jnp.concatenate([jnp.where(head0, dyp, 0.0), jnp.where(head0, 0.0, dyp)], axis=0).astype(BF16)
                G2 = lax.dot_general(dy2, xd.astype(BF16), _NT, preferred_element_type=F32)
                mts = []
                for r in range(2):
                    h = 2 * hp + r
                    seg = acs[:, h:h + 1] - acsT[h:h + 1, :]
                    dec = jnp.exp(jnp.where(trilb, seg, -jnp.inf))
                    decT = jnp.exp(jnp.where(ri <= ci, -seg, -jnp.inf))
                    gd = G2[r * L:(r + 1) * L, :] * dec
                    dcb = dcb + gd
                    dseg = gd * CB
                    dsr_ref[:, h:h + 1] = jnp.sum(dseg, axis=1, keepdims=True)
                    dsc_ref[h:h + 1, :] = jnp.sum(dseg, axis=0, keepdims=True)
                    mts.append((CBT * decT).astype(BF16))
                dxd = jnp.dot(jnp.concatenate(mts, axis=1), dy2, preferred_element_type=F32)
                Up = U[:, qs]
                r_ref[0:L, sl] = dyp * Z[:, qs] * eap
                r_ref[L:2 * L, sl] = dxd * xs
                r_ref[2 * L:3 * L, sl] = Up * xs
                dz_ref[:, sl] = (dyp * eap).astype(BF16)
                xw_ref[:, sl] = (xs * wp).astype(BF16)
                dxs_ref[:, sl] = dxd * dtp + Dp * dyp + Up * wp
                dd_ref[:, sl] += jnp.sum(dyp * xs, axis=0, keepdims=True)
            dcbs.append(dcb)
            t = dstg * stg
            for k in range(HEADS_PER_GROUP):
                dlast = dlast + jnp.where(lane == g * HEADS_PER_GROUP + k,
                                          jnp.sum(t[k * 64:(k + 1) * 64, :], keepdims=True), 0.0)
        sums = _select_dot(r_ref[...], et_ref[...])
        r1, r2, dws = sums[:L], sums[L:2 * L], sums[2 * L:]
        dww = dws * wt
        ddt = r2 + dws * er
        dacs = r1 - dww + dsr_ref[...] - dsc_ref[...].T
        dlast = dlast * elast + jnp.sum(dww, axis=0, keepdims=True)
        lasts = [last[:, h:h + 1] for h in range(SSM_HEADS)]
        for g in range(SSM_GROUPS):
            Bg = b_ref[:, g * N:(g + 1) * N].astype(BF16)
            Cg = c_ref[:, g * N:(g + 1) * N].astype(BF16)
            gs = slice(g * GW, (g + 1) * GW)
            stb = st_in_ref[0, gs, :].astype(BF16)
            dstb = dst_ref[gs, :].astype(BF16)
            dcbb = dcbs[g].astype(BF16)
            dzg = dz_ref[:, gs]
            dc_ref[:, g * N:(g + 1) * N] = (jnp.dot(dzg, stb, preferred_element_type=F32)
                                            + jnp.dot(dcbb, Bg, preferred_element_type=F32))
            db_ref[:, g * N:(g + 1) * N] = (jnp.dot(xw_ref[:, gs], dstb, preferred_element_type=F32)
                                            + lax.dot_general(dcbb, Cg, _TN, preferred_element_type=F32))
            dsn = lax.dot_general(dzg, Cg, _TN, preferred_element_type=F32)
            for k in range(HEADS_PER_GROUP):
                h = g * HEADS_PER_GROUP + k
                rows = slice(h * 64, (h + 1) * 64)
                dst_ref[rows, :] = dst_ref[rows, :] * jnp.exp(lasts[h]) + dsn[k * 64:(k + 1) * 64, :]
        rowi = lax.broadcasted_iota(jnp.int32, (L, 1), 0)
        dacs = dacs + jnp.where(rowi == L - 1, dlast, 0.0)
        da = jnp.dot((ci >= ri).astype(F32), dacs, precision=HI, preferred_element_type=F32)
        ddt = ddt + da * A
        da_ref[...] += jnp.sum(da * dt, axis=0, keepdims=True)
        ddtr = jnp.where(lane < SSM_HEADS, ddt * _sigmoid(u), 0.0)
        ddtr_ref[...] = ddtr
        ddtb_ref[...] += jnp.sum(ddtr, axis=0, keepdims=True)

    rev = lambda c: nc - 1 - c
    gn = SSM_GROUPS * N
    res = pl.pallas_call(
        body, name=name, grid=(nc,),
        in_specs=[pl.BlockSpec((L, D_INNER), lambda c: (rev(c), 0)),
                  pl.BlockSpec((L, gn), lambda c: (rev(c), D_INNER // gn)),
                  pl.BlockSpec((L, gn), lambda c: (rev(c), D_INNER // gn + 1)),
                  pl.BlockSpec((L, LANES), lambda c: (rev(c), 0)),
                  pl.BlockSpec((1, LANES), lambda c: (0, 0)),
                  pl.BlockSpec((1, LANES), lambda c: (0, 0)),
                  pl.BlockSpec((1, D_INNER), lambda c: (0, 0)),
                  pl.BlockSpec((LANES, D_INNER), lambda c: (0, 0)),
                  pl.BlockSpec((D_INNER, LANES), lambda c: (0, 0)),
                  pl.BlockSpec((1, D_INNER, N), lambda c: (rev(c), 0, 0)),
                  pl.BlockSpec((L, D_INNER), lambda c: (rev(c), 0))] + r_in,
        out_specs=[pl.BlockSpec((L, D_INNER), lambda c: (rev(c), 0)),
                   pl.BlockSpec((L, gn), lambda c: (rev(c), 0)),
                   pl.BlockSpec((L, gn), lambda c: (rev(c), 0)),
                   pl.BlockSpec((L, LANES), lambda c: (rev(c), 0)),
                   pl.BlockSpec((1, LANES), lambda c: (0, 0)),
                   pl.BlockSpec((1, LANES), lambda c: (0, 0)),
                   pl.BlockSpec((1, D_INNER), lambda c: (0, 0))] + r_out,
        out_shape=[jax.ShapeDtypeStruct((S, D_INNER), F32), jax.ShapeDtypeStruct((S, gn), F32),
                   jax.ShapeDtypeStruct((S, gn), F32), jax.ShapeDtypeStruct((S, LANES), F32),
                   jax.ShapeDtypeStruct((1, LANES), F32), jax.ShapeDtypeStruct((1, LANES), F32),
                   jax.ShapeDtypeStruct((1, D_INNER), F32)] + r_shapes,
        scratch_shapes=[pltpu.VMEM((D_INNER, N), F32), pltpu.VMEM((L, D_INNER), BF16), pltpu.VMEM((L, D_INNER), BF16),
                        pltpu.VMEM((3 * L, D_INNER), F32), pltpu.VMEM((L, LANES), F32), pltpu.VMEM((LANES, L), F32)]
                       + r_sems,
        compiler_params=pltpu.CompilerParams(dimension_semantics=("arbitrary",), vmem_limit_bytes=VMEM_LIMIT,
                                             has_side_effects=rider is not None),
    )(xbc, xbc, xbc, dtraw, dtb, A, dexp, e_mat, et_mat, states, dy, *r_args)
    return res[:7], res[7:]


BAND_A = (A_PREV + 1) * CHUNK
REL_W = 640


def _relpos_select():
    k = np.arange(REL_W)
    rel = np.where(k < BAND_A, A_PREV * CHUNK - k, A_PREV * CHUNK - (k - REL_W))
    idx = np.clip(rel, -MAX_REL, MAX_REL) + MAX_REL
    sel = (np.arange(REL_W)[:, None] == idx[None, :]) & (k != BAND_A)[None, :]
    return sel.astype(np.float32)


def relpos_bias(table_pad, *, name):
    def body(t_ref, s_ref, o_ref):
        v = jnp.dot(t_ref[...], s_ref[...], precision=HI, preferred_element_type=F32)
        for h in range(N_HEADS):
            o_ref[h] = pltpu.roll(jnp.broadcast_to(v[h:h + 1, :], (CHUNK, REL_W)), 0, 1, stride=1, stride_axis=0)

    return pl.pallas_call(body, name=name, out_shape=jax.ShapeDtypeStruct((N_HEADS, CHUNK, REL_W), F32),
                          compiler_params=pltpu.CompilerParams(vmem_limit_bytes=VMEM_LIMIT),
                          )(table_pad, jnp.asarray(_relpos_select()))


def relpos_grad(dbias_rev, *, name):
    def body(d_ref, s_ref, o_ref):
        head = lax.broadcasted_iota(jnp.int32, (N_HEADS, 1), 0)
        dv = jnp.zeros((N_HEADS, REL_W), F32)
        for h in range(N_HEADS):
            back = pltpu.roll(d_ref[h], REL_W - (CHUNK - 1), 1, stride=1, stride_axis=0)
            dv = dv + jnp.where(head == h, jnp.sum(back, axis=0, keepdims=True), 0.0)
        o_ref[...] = lax.dot_general(dv, s_ref[...], _NT, precision=HI, preferred_element_type=F32)

    return pl.pallas_call(body, name=name, out_shape=jax.ShapeDtypeStruct((N_HEADS, REL_W), F32),
                          compiler_params=pltpu.CompilerParams(vmem_limit_bytes=VMEM_LIMIT),
                          )(dbias_rev, jnp.asarray(_relpos_select()))


def loss_head(y, t, *, name, tm=256):
    S, D = y.shape
    tm = min(tm, S)

    def body(y_ref, t_ref, dy_ref, l_ref):
        e = y_ref[...] - t_ref[...]
        dy_ref[...] = e * (1.0 / D)

        @pl.when(pl.program_id(0) == 0)
        def _():
            l_ref[...] = jnp.zeros_like(l_ref)

        part = jnp.sum(jnp.sum(e * e, axis=1, keepdims=True), axis=0, keepdims=True) * (0.5 / D)
        l_ref[...] += jnp.broadcast_to(part, l_ref.shape)

    return pl.pallas_call(
        body, name=name, grid=(S // tm,),
        in_specs=[pl.BlockSpec((tm, D), lambda i: (i, 0))] * 2,
        out_specs=[pl.BlockSpec((tm, D), lambda i: (i, 0)), pl.BlockSpec((1, LANES), lambda i: (0, 0))],
        out_shape=[jax.ShapeDtypeStruct((S, D), F32), jax.ShapeDtypeStruct((1, LANES), F32)],
        compiler_params=_params("arbitrary"),
    )(y, t)


def f_adamw(w, g, m, v):
    m = ADAM_B1 * m + (1.0 - ADAM_B1) * g
    v = ADAM_B2 * v + (1.0 - ADAM_B2) * (g * g)
    m_hat = m / (1.0 - ADAM_B1 ** ADAM_STEP)
    v_hat = v / (1.0 - ADAM_B2 ** ADAM_STEP)
    delta = -ADAM_LR * (m_hat / (jnp.sqrt(v_hat) + ADAM_EPS) + ADAM_WD * w)
    return delta, m, v


ANY = pl.BlockSpec(memory_space=pl.ANY)


def _pos():
    return lax.axis_index("x"), lax.axis_index("y"), lax.axis_index("c")


def _other_chips(x, y):
    return [(1 - x, y), (x, 1 - y), (1 - x, 1 - y)]


class Rider:
    def __init__(self, ins, outs, sems, start, mid, finish):
        self.ins, self.outs, self.sems = list(ins), list(outs), list(sems)
        self.start, self.mid, self.finish = start, mid, finish


def _rider_parts(rider):
    if rider is None:
        return [], [], [], [], []
    return [ANY] * len(rider.ins), [ANY] * len(rider.outs), rider.outs, rider.sems, rider.ins


def _ride(rider, ins, outs, sems, first, mid, last):
    pos = _pos()

    @pl.when(first)
    def _():
        rider.start(pos, ins, outs, sems)

    if rider.mid is not None:
        @pl.when(mid)
        def _():
            rider.mid(pos, ins, outs, sems)

    @pl.when(last)
    def _():
        rider.finish(pos, ins, outs, sems)


def run_rider(rider, *, name):
    n_in, n_out = len(rider.ins), len(rider.outs)

    def body(*refs):
        ins, outs, sems = refs[:n_in], refs[n_in:n_in + n_out], refs[n_in + n_out:]
        pos = _pos()
        rider.start(pos, ins, outs, sems)
        if rider.mid is not None:
            rider.mid(pos, ins, outs, sems)
        rider.finish(pos, ins, outs, sems)

    return pl.pallas_call(
        body, name=name, in_specs=[ANY] * n_in, out_specs=[ANY] * n_out, out_shape=rider.outs,
        scratch_shapes=rider.sems, compiler_params=pltpu.CompilerParams(has_side_effects=True),
    )(*rider.ins)


def gather_rider(shards):
    n = len(shards)

    def copies(pos, ins, outs, sems):
        x, y, c = pos
        send, recv, fsend, frecv = sems
        me = 2 * x + y
        sib = (x, y, 1 - c)
        first, arrive, passed, theirs = [], [], [], []
        for i in range(n):
            for j, (px, py) in enumerate(_other_chips(x, y)):
                k = 3 * i + j
                far = dict(device_id=(px, py, c), device_id_type=MESH)
                near = dict(device_id=sib, device_id_type=MESH)
                got = outs[i].at[2 * px + py, c]
                his = outs[i].at[2 * px + py, 1 - c]
                first.append(pltpu.make_async_remote_copy(ins[i].at[c], outs[i].at[me, c], send.at[k], recv.at[k], **far))
                arrive.append(pltpu.make_async_remote_copy(ins[i].at[c], got, send.at[k], recv.at[k], **far))
                passed.append(pltpu.make_async_remote_copy(got, got, fsend.at[k], frecv.at[k], **near))
                theirs.append(pltpu.make_async_remote_copy(his, his, fsend.at[k], frecv.at[k], **near))
        return first, arrive, passed, theirs

    def start(*a):
        for cp in copies(*a)[0]:
            cp.start()

    def mid(*a):
        _, arrive, passed, _ = copies(*a)
        for got, cp in zip(arrive, passed):
            got.wait_recv()
            cp.start()

    def finish(*a):
        first, _, passed, theirs = copies(*a)
        for cp in theirs:
            cp.wait_recv()
        for cp in first + passed:
            cp.wait_send()

    return Rider(shards, [jax.ShapeDtypeStruct((4,) + s.shape, s.dtype) for s in shards],
                 [pltpu.SemaphoreType.DMA((3 * n,))] * 4, start, mid, finish)


def scatter_rider(ps):
    n = len(ps)

    def copies(pos, ins, outs, sems):
        x, y, c = pos
        send, recv = sems
        return [pltpu.make_async_remote_copy(ins[i].at[2 * px + py], outs[i].at[j], send.at[3 * i + j], recv.at[3 * i + j],
                                             device_id=(px, py, c), device_id_type=MESH)
                for i in range(n) for j, (px, py) in enumerate(_other_chips(x, y))]

    def start(*a):
        for cp in copies(*a):
            cp.start()

    def finish(*a):
        for cp in copies(*a):
            cp.wait()

    return Rider(ps, [jax.ShapeDtypeStruct((3,) + p.shape[1:], p.dtype) for p in ps],
                 [pltpu.SemaphoreType.DMA((3 * n,))] * 2, start, None, finish)


def pair_swap_halves(gs, *, name):
    n = len(gs)

    def body(*refs):
        ins, outs = refs[:n], refs[n:2 * n]
        send, recv = refs[2 * n:]
        x, y, c = _pos()
        cps = []
        for i in range(n):
            cp = pltpu.make_async_remote_copy(ins[i].at[1 - c], outs[i], send.at[i], recv.at[i],
                                              device_id=(x, y, 1 - c), device_id_type=MESH)
            cp.start()
            cps.append(cp)
        for cp in cps:
            cp.wait()

    return pl.pallas_call(
        body, name=name, in_specs=[ANY] * n, out_specs=[ANY] * n,
        out_shape=[jax.ShapeDtypeStruct(g.shape[1:], g.dtype) for g in gs],
        scratch_shapes=[pltpu.SemaphoreType.DMA((n,)), pltpu.SemaphoreType.DMA((n,))],
        compiler_params=pltpu.CompilerParams(has_side_effects=True),
    )(*gs)


def pair_share(hs, *, name):
    n = len(hs)

    def body(*refs):
        ins, outs = refs[:n], refs[n:2 * n]
        send, recv = refs[2 * n:]
        x, y, c = _pos()
        cps = []
        for i in range(n):
            cp = pltpu.make_async_remote_copy(ins[i], outs[i], send.at[i], recv.at[i],
                                              device_id=(x, y, 1 - c), device_id_type=MESH)
            cp.start()
            cps.append(cp)
        for cp in cps:
            cp.wait()

    return pl.pallas_call(
        body, name=name, in_specs=[ANY] * n, out_specs=[ANY] * n,
        out_shape=[jax.ShapeDtypeStruct(h.shape, h.dtype) for h in hs],
        scratch_shapes=[pltpu.SemaphoreType.DMA((n,)), pltpu.SemaphoreType.DMA((n,))],
        compiler_params=pltpu.CompilerParams(has_side_effects=True),
    )(*hs)


def gather_all(buf, *, name):
    def body(in_ref, out_ref, send, recv, loc):
        x, y, c = _pos()
        lid = 4 * x + 2 * y + c
        lc = pltpu.make_async_copy(in_ref, out_ref.at[lid], loc.at[0])
        lc.start()
        cps = []
        for k in range(1, 8):
            px = 1 - x if k & 4 else x
            py = 1 - y if k & 2 else y
            pc = 1 - c if k & 1 else c
            cp = pltpu.make_async_remote_copy(in_ref, out_ref.at[lid], send.at[k - 1], recv.at[k - 1],
                                              device_id=(px, py, pc), device_id_type=MESH)
            cp.start()
            cps.append((cp, 4 * px + 2 * py + pc, (px, py, pc)))
        for k, (cp, plid, peer) in enumerate(cps):
            cp.wait_send()
            pltpu.make_async_remote_copy(in_ref, out_ref.at[plid], send.at[k], recv.at[k],
                                         device_id=peer, device_id_type=MESH).wait_recv()
        lc.wait()

    return pl.pallas_call(
        body, name=name, in_specs=[ANY], out_specs=ANY,
        out_shape=jax.ShapeDtypeStruct((8,) + buf.shape, buf.dtype),
        scratch_shapes=[pltpu.SemaphoreType.DMA((7,)), pltpu.SemaphoreType.DMA((7,)), pltpu.SemaphoreType.DMA((1,))],
        compiler_params=pltpu.CompilerParams(has_side_effects=True),
    )(buf)


def sum_slots(a, *, name):
    n = a.shape[0]

    def body(a_ref, o_ref):
        acc = a_ref[0]
        for k in range(1, n):
            acc = acc + a_ref[k]
        o_ref[...] = acc

    return pl.pallas_call(body, name=name, out_shape=jax.ShapeDtypeStruct(a.shape[1:], a.dtype),
                          compiler_params=pltpu.CompilerParams(vmem_limit_bytes=VMEM_LIMIT))(a)


def _row_tile(r, want, mult=16):
    t = (min(want, r) // mult) * mult
    while t >= mult:
        if r % t == 0:
            return t
        t -= mult
    return r


def pair_add(g, r1, csel, *, name):
    _, _, r, C = g.shape
    tr = _row_tile(r, 256)

    def body(g_ref, r_ref, c_ref, p32_ref, pb_ref):
        south = c_ref[0:1, 0:1] == 0.0
        p = jnp.where(south, g_ref[0, 0], g_ref[1, 0]) + r_ref[0]
        p32_ref[0] = p
        pb_ref[0] = p.astype(BF16)

    return pl.pallas_call(
        body, name=name, grid=(4, r // tr),
        in_specs=[pl.BlockSpec((2, 1, tr, C), lambda j, t: (0, j, t, 0)), pl.BlockSpec((1, tr, C), lambda j, t: (j, t, 0)),
                  pl.BlockSpec((1, LANES), lambda j, t: (0, 0))],
        out_specs=[pl.BlockSpec((1, tr, C), lambda j, t: (j, t, 0))] * 2,
        out_shape=[jax.ShapeDtypeStruct((4, r, C), F32), jax.ShapeDtypeStruct((4, r, C), BF16)],
        compiler_params=_params("parallel", "parallel"),
    )(g, r1, csel)


def chip_add(p32, r3, msel, *, name):
    _, r, C = p32.shape
    tr = _row_tile(r, 128)

    def body(p_ref, r_ref, m_ref, o_ref):
        me = m_ref[0:1, 0:1]
        acc = jnp.where(me == 0.0, p_ref[0], jnp.where(me == 1.0, p_ref[1], jnp.where(me == 2.0, p_ref[2], p_ref[3])))
        for j in range(3):
            acc = acc + r_ref[j].astype(F32)
        o_ref[...] = acc

    return pl.pallas_call(
        body, name=name, grid=(r // tr,),
        in_specs=[pl.BlockSpec((4, tr, C), lambda t: (0, t, 0)), pl.BlockSpec((3, tr, C), lambda t: (0, t, 0)),
                  pl.BlockSpec((1, LANES), lambda t: (0, 0))],
        out_specs=pl.BlockSpec((tr, C), lambda t: (t, 0)),
        out_shape=jax.ShapeDtypeStruct((r, C), F32),
        compiler_params=_params("parallel"),
    )(p32, r3, msel)


def _consts():
    i512 = np.arange(N_HEADS * HEAD_DIM)
    i128 = np.arange(LANES)
    bd512 = (i512[:, None] // HEAD_DIM == i512[None, :] // HEAD_DIM).astype(np.float32)
    bd128 = (i128[:, None] // HEAD_DIM == i128[None, :] // HEAD_DIM).astype(np.float32)
    fold = (np.arange(HEAD_DIM)[:, None] == (i512[None, :] % HEAD_DIM)).astype(np.float32)
    grp = N_HEADS // 2 * HEAD_DIM
    expand = ((i128[:, None] // HEAD_DIM == i512[None, :] // grp)
              & (i128[:, None] % HEAD_DIM == i512[None, :] % HEAD_DIM)).astype(np.float32)
    band = (B_PREV + 1) * CHUNK
    rel = np.arange(CHUNK)[:, None] - (np.arange(band)[None, :] - B_PREV * CHUNK)
    slopes = 2.0 ** (-8.0 * np.arange(1, N_HEADS + 1, dtype=np.float32) / N_HEADS)
    bias_b = (-slopes[:, None, None] * np.abs(rel).astype(np.float32)[None]).astype(np.float32)
    return [jnp.asarray(a) for a in (bd512, bd128, fold, expand)], jnp.asarray(bias_b)


def _ffn_fwd(xin, h, l, W, P, next_gain):
    Wi = W["ffn_in"][l]
    gate = matmul(h, Wi[:, :D_FF], mode="nn", name=f"ffn{l}_gate", out_dtype=BF16)
    up = matmul(h, Wi[:, D_FF:], mode="nn", name=f"ffn{l}_up", out_dtype=BF16)
    gc, act = dwconv_fwd(gate, P["ffn_conv_w"][l], P["ffn_conv_b"][l:l + 1], lambda y, u: (y, _silu(y) * u), [up],
                         [BF16, BF16], name=f"ffn{l}_conv")
    saved = (xin, h, gate, gc, up, act)
    if next_gain is None:
        return matmul(act, W["ffn_out"][l], mode="nn", name=f"ffn{l}_out", residual=xin), saved, None
    xout, h_next = matmul_norm(act, W["ffn_out"][l], xin, next_gain, name=f"ffn{l}_out")
    return xout, saved, h_next


def _ffn_bwd(dxout, l, saved, W, P):
    xin, h, gate, gc, up, act = saved
    g = P["norm_ffn"][l:l + 1]
    Wi = W["ffn_in"][l]
    dact = matmul(dxout, W["ffn_out"][l], mode="nt", name=f"ffn{l}_dact", out_dtype=BF16)
    dWo = matmul(act, dxout, mode="tn", name=f"ffn{l}_dwout")
    dgate, dcw, dcb, dup = dwconv_bwd(gate, P["ffn_conv_w"][l], [gc, up, dact],
                                      lambda c, u, da: (da * u * _dsilu(c), da * _silu(c)), [BF16],
                                      name=f"ffn{l}_dconv")
    dh = matmul(dgate, Wi[:, :D_FF], mode="nt", name=f"ffn{l}_dh_gate")
    dxin, dg = matmul_dnorm(dup, Wi[:, D_FF:], dh, xin, g, dxout, name=f"ffn{l}_dh_up")
    dWi = jnp.concatenate([matmul(h, dgate, mode="tn", name=f"ffn{l}_dw_gate"),
                           matmul(h, dup, mode="tn", name=f"ffn{l}_dw_up")], axis=1)
    return dxin, dWi, dWo, dg, dcw, dcb


class NoComm:
    def fwd_rider(self, tag):
        return None

    def fwd_done(self, tag, outs, W, P):
        pass

    def grads(self, tag, cols, rows):
        return None

    def bwd_done(self, tag, outs):
        pass


def local_step(x, tgt, W, P, comm):
    qk_consts, bias_b = _consts()
    pad_rows = lambda t, n: jnp.pad(t, ((n * CHUNK, 0), (0, 0)))
    DQ = N_HEADS * HEAD_DIM

    g_mix0 = P["norm_mix"][0:1]
    (h0,) = rowwise(f_rmsnorm, [x], [g_mix0], [(D_MODEL, BF16)], name="attn_norm")
    qkv = matmul(h0, W["attn_in"], mode="nn", name="attn_qkv")
    qk_par = [P["q_norm_a"], P["k_norm_a"], P["q_norm_b"], P["k_norm_b"]] + qk_consts
    qa, ka, va, qb, kb, vb = rowwise(f_qknorm, [qkv], qk_par, [(DQ, BF16)] * 6, name="attn_qknorm")
    ka, va, kb, vb = pad_rows(ka, A_PREV), pad_rows(va, A_PREV), pad_rows(kb, B_PREV), pad_rows(vb, B_PREV)
    table = jnp.pad(P["relpos_table"], ((0, 0), (0, REL_W - (2 * MAX_REL + 1))))
    nj = min(ATT_TQ, x.shape[0]) // CHUNK
    bias_a = widen_bias(relpos_bias(table, name="relpos_bias")[:, :, :BAND_A], A_PREV, nj)
    bias_b = widen_bias(bias_b, B_PREV, nj)
    sinks = jnp.broadcast_to(P["sinks"].reshape(N_HEADS // 2, 2, 1), (N_HEADS // 2, 2, LANES))
    oa, late = attn_fwd(qa, ka, va, bias_a, None, n_prev=A_PREV, name="attn_a", rider=comm.fwd_rider("a"))
    comm.fwd_done("a", late, W, P)
    ob, late = attn_fwd(qb, kb, vb, bias_b, sinks, n_prev=B_PREV, name="attn_b", rider=comm.fwd_rider("b"))
    comm.fwd_done("b", late, W, P)
    Wao = W["attn_out"]
    x1 = matmul(oa, Wao[:DQ], mode="nn", name="attn_out_a", residual=x)
    x1, hf0 = matmul_norm(ob, Wao[DQ:], x1, P["norm_ffn"][0:1], name="attn_out_b")
    g_mix1 = P["norm_mix"][1:2]
    x2, ffn0, h2 = _ffn_fwd(x1, hf0, 0, W, P, g_mix1)

    Ws = W["ssm_in"]
    CC = D_INNER + 2 * SSM_GROUPS * SSM_STATE
    Wz, Wx = Ws[:, :D_INNER], Ws[:, D_INNER:D_INNER + CC]
    Wdt = jnp.pad(Ws[:, D_INNER + CC:], ((0, 0), (0, LANES - SSM_HEADS)))
    z = matmul(h2, Wz, mode="nn", name="ssm_z", out_dtype=BF16)
    xr = matmul(h2, Wx, mode="nn", name="ssm_xbc", out_dtype=BF16)
    dtraw = matmul(h2, Wdt, mode="nn", name="ssm_dt")
    xc, xbc = dwconv_fwd(xr, P["ssm_conv_w"], P["ssm_conv_b"], lambda y: (y, _silu(y)), [], [BF16, F32],
                         name="ssm_conv")
    pad32 = lambda v: jnp.pad(v, ((0, 0), (0, LANES - SSM_HEADS)))
    A = pad32(-jnp.exp(P["ssm_a_log"]))
    dtb = pad32(P["ssm_dt_bias"])
    dexp = jnp.repeat(P["ssm_d"], D_INNER // SSM_HEADS, axis=1)
    y, states = ssd_fwd(xbc, dtraw, dtb, A, dexp, name="ssd_fwd")
    (y2,) = rowwise(f_gate_norm, [y, z], [P["ssm_norm"]], [(D_INNER, BF16)], name="ssm_gate_norm")
    x3, hf1 = matmul_norm(y2, W["ssm_out"], x2, P["norm_ffn"][1:2], name="ssm_out")
    x4, ffn1, _ = _ffn_fwd(x3, hf1, 1, W, P, None)

    dx4, lpart = loss_head(x4, tgt, name="loss_head")

    dx3, dWfi1, dWfo1, dgf1, dfcw1, dfcb1 = _ffn_bwd(dx4, 1, ffn1, W, P)
    out_f1 = comm.grads("f1", dWfi1, dWfo1)
    dy2 = matmul(dx3, W["ssm_out"], mode="nt", name="ssm_dy")
    dWso = matmul(y2, dx3, mode="tn", name="ssm_dwout")
    dy, dz, dnw = rowwise_vjp(f_gate_norm, [y, z], [P["ssm_norm"]], [dy2], [(0, F32), (1, BF16)], [0],
                              name="ssm_dgate_norm")
    (dxs, dB, dC, ddtraw, dA, ddtb, dDl), sent = ssd_bwd(xbc, dtraw, dtb, A, dexp, states, dy, name="ssd_bwd",
                                                          rider=out_f1)
    comm.bwd_done("f1", sent)
    dxr, dscw, dscb = dwconv_bwd(xr, P["ssm_conv_w"], [xc, (dxs, dB, dC)], lambda c, g: (g * _dsilu(c),), [],
                                 name="ssm_dconv")
    dh2 = matmul(dz, Wz, mode="nt", name="ssm_dh_z")
    dh2 = matmul(dxr, Wx, mode="nt", name="ssm_dh_x", residual=dh2)
    dx2, dgm1 = matmul_dnorm(ddtraw, Wdt, dh2, x2, g_mix1, dx3, name="ssm_dh_dt")
    dWs = jnp.concatenate([matmul(h2, dz, mode="tn", name="ssm_dw_z"),
                           matmul(h2, dxr, mode="tn", name="ssm_dw_x"),
                           matmul(h2, ddtraw, mode="tn", name="ssm_dw_dt")[:, :SSM_HEADS]], axis=1)
    out_s = comm.grads("s", dWs, dWso)

    dx1, dWfi0, dWfo0, dgf0, dfcw0, dfcb0 = _ffn_bwd(dx2, 0, ffn0, W, P)
    out_f0 = comm.grads("f0", dWfi0, dWfo0)
    doa = matmul(dx1, Wao[:DQ], mode="nt", name="attn_do_a", out_dtype=BF16)
    dob = matmul(dx1, Wao[DQ:], mode="nt", name="attn_do_b", out_dtype=BF16)
    dWao = jnp.concatenate([matmul(oa, dx1, mode="tn", name="attn_dwout_a"),
                            matmul(ob, dx1, mode="tn", name="attn_dwout_b")], axis=0)
    (dqa, dka, dva, dbias_a), sent = attn_bwd(qa, ka, va, doa, bias_a, None, n_prev=A_PREV, name="attn_a_bwd",
                                              rider=out_s)
    comm.bwd_done("s", sent)
    (dqb, dkb, dvb, _, dsk), sent = attn_bwd(qb, kb, vb, dob, bias_b, sinks, n_prev=B_PREV, name="attn_b_bwd",
                                             rider=out_f0)
    comm.bwd_done("f0", sent)
    pa, pb = A_PREV * CHUNK, B_PREV * CHUNK
    dqkv, dgqa, dgka, dgqb, dgkb = rowwise_vjp(f_qknorm, [qkv], qk_par, [dqa, dka, dva, dqb, dkb[pb:], dvb[pb:]],
                                               [(0, BF16)], [0, 1, 2, 3], name="attn_dqknorm",
                                               cot_skip=[0, pa, pa, 0, 0, 0])
    dx, dgm0 = matmul_dnorm(dqkv, W["attn_in"], None, x, g_mix0, dx1, name="attn_dh")
    dWai = matmul(h0, dqkv, mode="tn", name="attn_dwin")
    dbias_a = fold_bias(dbias_a, A_PREV, nj)
    dbias_rev = jnp.pad(dbias_a[:, ::-1, :], ((0, 0), (0, 0), (0, REL_W - BAND_A)))
    dtable = relpos_grad(dbias_rev, name="relpos_grad")[:, :2 * MAX_REL + 1]

    gW = {"attn_in": dWai, "attn_out": dWao, "ssm_in": dWs, "ssm_out": dWso,
          "ffn_in": [dWfi0, dWfi1], "ffn_out": [dWfo0, dWfo1]}
    gP = {"norm_mix": jnp.concatenate([dgm0, dgm1], axis=0),
          "norm_ffn": jnp.concatenate([dgf0, dgf1], axis=0),
          "relpos_table": dtable, "q_norm_a": dgqa, "k_norm_a": dgka, "q_norm_b": dgqb, "k_norm_b": dgkb,
          "sinks": dsk[:, :, 0].reshape(1, N_HEADS),
          "ssm_conv_w": dscw, "ssm_conv_b": dscb,
          "ssm_dt_bias": ddtb[:, :SSM_HEADS], "ssm_a_log": dA[:, :SSM_HEADS] * A[:, :SSM_HEADS],
          "ssm_d": dDl.reshape(SSM_HEADS, D_INNER // SSM_HEADS).sum(axis=1).reshape(1, SSM_HEADS),
          "ssm_norm": dnw,
          "ffn_conv_w": jnp.stack([dfcw0, dfcw1]), "ffn_conv_b": jnp.concatenate([dfcb0, dfcb1], axis=0)}
    return lpart, dx, gW, gP


WEIGHTS = ["norm_mix", "norm_ffn", "attn_w_in", "attn_w_out", "relpos_table", "q_norm_a", "k_norm_a", "q_norm_b",
           "k_norm_b", "sinks", "ssm_w_in", "ssm_conv_w", "ssm_conv_b", "ssm_dt_bias", "ssm_a_log", "ssm_d",
           "ssm_norm", "ssm_w_out", "ffn_w_in", "ffn_conv_w", "ffn_conv_b", "ffn_w_out"]
ARGS = ["x"] + WEIGHTS + ["loss_target"] + ["m_" + w for w in WEIGHTS] + ["v_" + w for w in WEIGHTS]
N_CHIPS = 4
SMALL_ROWS = 384
SMALL_ORDER = ["norm_mix", "norm_ffn", "relpos_table", "q_norm_a", "k_norm_a", "q_norm_b", "k_norm_b", "sinks",
               "ssm_dt_bias", "ssm_a_log", "ssm_d", "ffn_conv_b", "ssm_conv_w", "ssm_conv_b", "ssm_norm", "ffn_conv_w"]


def _cols_to_slabs(g):
    K, N = g.shape
    return g.reshape(2, K // 2, N_CHIPS, N // N_CHIPS).transpose(0, 2, 1, 3)


def _rows_to_slabs(g):
    R, C = g.shape
    return g.reshape(N_CHIPS, 2, R // (2 * N_CHIPS), C).transpose(1, 0, 2, 3)


class MeshComm:
    def __init__(self, d, xi, yi, ci):
        self.d, self.ci, self.me = d, ci, 2 * xi + yi
        self.csel = jnp.full((1, LANES), ci, F32)
        self.msel = jnp.full((1, LANES), self.me, F32)
        halves = lambda w: w.reshape((2, -1, w.shape[-1]))
        small = jnp.concatenate([d[k].reshape(-1) for k in ("ssm_conv_w", "ssm_conv_b", "ssm_norm", "ffn_conv_w")])
        small = jnp.pad(small, (0, 2 * 40 * LANES - small.shape[0])).reshape(2, 40, LANES)
        self.shards = {"attn": [halves(d["attn_w_in"][0].astype(BF16)), halves(d["attn_w_out"][0].astype(BF16))],
                       "a": [d["ffn_w_in"].astype(BF16), small],
                       "b": [d["ffn_w_out"].astype(BF16), halves(d["ssm_w_in"][0].astype(BF16)),
                             halves(d["ssm_w_out"][0].astype(BF16))]}
        self.p32, self.mine = {}, {}

    def _whole(self, tag, outs):
        return [lax.dynamic_update_slice_in_dim(g, s[None], self.me, axis=0) for g, s in zip(outs, self.shards[tag])]

    @staticmethod
    def _cat_cols(g):
        return jnp.concatenate([g[j].reshape((-1, g.shape[-1])) for j in range(N_CHIPS)], axis=1)

    def first_weights(self):
        g_ai, g_ao = self._whole("attn", run_rider(gather_rider(self.shards["attn"]), name="gather_attn"))
        return {"attn_in": self._cat_cols(g_ai), "attn_out": g_ao.reshape(-1, D_MODEL)}

    def fwd_rider(self, tag):
        return gather_rider(self.shards[tag])

    def fwd_done(self, tag, outs, W, P):
        if tag == "b":
            g_fo, g_si, g_so = self._whole("b", outs)
            W["ffn_out"] = [g_fo[:, l].reshape(-1, D_MODEL) for l in range(2)]
            W["ssm_in"], W["ssm_out"] = self._cat_cols(g_si), g_so.reshape(-1, D_MODEL)
            return
        g_fi, g_sm = self._whole("a", outs)
        W["ffn_in"] = [jnp.concatenate([g_fi[j, l] for j in range(N_CHIPS)], axis=1) for l in range(2)]
        sm = g_sm.reshape(N_CHIPS, -1)
        CC = D_INNER + 2 * SSM_GROUPS * SSM_STATE
        c4, f4 = CC // N_CHIPS, D_FF // N_CHIPS
        o1 = SSM_CONV * c4
        o2 = o1 + c4
        o3 = o2 + D_INNER // N_CHIPS
        o4 = o3 + 2 * FFN_CONV * f4
        P["ssm_conv_w"] = sm[:, :o1].reshape(N_CHIPS, SSM_CONV, c4).transpose(1, 0, 2).reshape(SSM_CONV, CC)
        P["ssm_conv_b"] = sm[:, o1:o2].reshape(1, CC)
        P["ssm_norm"] = sm[:, o2:o3].reshape(1, D_INNER)
        P["ffn_conv_w"] = sm[:, o3:o4].reshape(N_CHIPS, 2, FFN_CONV, f4).transpose(1, 2, 0, 3).reshape(2, FFN_CONV, D_FF)

    def grads(self, tag, cols, rows):
        slabs = [_cols_to_slabs(cols), _rows_to_slabs(rows)]
        from_sib = pair_swap_halves(slabs, name="grad_pair_swap_" + tag)
        pairs = [pair_add(g, r, self.csel, name=f"grad_pair_add_{tag}{i}") for i, (g, r) in enumerate(zip(slabs, from_sib))]
        self.p32[tag] = [p[0] for p in pairs]
        return scatter_rider([p[1] for p in pairs])

    def bwd_done(self, tag, outs):
        self.mine[tag] = [chip_add(p, r, self.msel, name=f"grad_chip_add_{tag}{i}")
                          for i, (p, r) in enumerate(zip(self.p32[tag], outs))]

    def finish(self, d_attn_in, d_attn_out):
        self.bwd_done("at", run_rider(self.grads("at", d_attn_in, d_attn_out), name="grad_scatter_at"))
        order = ["at", "s", "f0", "f1"]
        mine = [m for t in order for m in self.mine[t]]
        theirs = pair_share(mine, name="grad_pair_share")
        full = [jnp.where(self.ci == 0, jnp.stack([a, b]), jnp.stack([b, a])).reshape((-1, a.shape[-1]))
                for a, b in zip(mine, theirs)]
        ai, ao, si, so, fi0, fo0, fi1, fo1 = full
        return {"attn_w_in": ai[None], "attn_w_out": ao[None], "ssm_w_in": si[None], "ssm_w_out": so[None],
                "ffn_w_in": jnp.stack([fi0, fi1]), "ffn_w_out": jnp.stack([fo0, fo1])}


def _adamw(w, g, m, v, name):
    shp = w.shape
    two = lambda a: a.reshape((-1, shp[-1]))
    outs = [(shp[-1], F32)] * 3
    d, nm, nv = rowwise(f_adamw, [two(w), two(g), two(m), two(v)], [], outs, name="adamw_" + name)
    return d.reshape(shp), nm.reshape(shp), nv.reshape(shp)


def kernel(x, norm_mix, norm_ffn, attn_w_in, attn_w_out, relpos_table, q_norm_a, k_norm_a, q_norm_b, k_norm_b, sinks, ssm_w_in, ssm_conv_w, ssm_conv_b, ssm_dt_bias, ssm_a_log, ssm_d, ssm_norm, ssm_w_out, ffn_w_in, ffn_conv_w, ffn_conv_b, ffn_w_out, loss_target, m_norm_mix, m_norm_ffn, m_attn_w_in, m_attn_w_out, m_relpos_table, m_q_norm_a, m_k_norm_a, m_q_norm_b, m_k_norm_b, m_sinks, m_ssm_w_in, m_ssm_conv_w, m_ssm_conv_b, m_ssm_dt_bias, m_ssm_a_log, m_ssm_d, m_ssm_norm, m_ssm_w_out, m_ffn_w_in, m_ffn_conv_w, m_ffn_conv_b, m_ffn_w_out, v_norm_mix, v_norm_ffn, v_attn_w_in, v_attn_w_out, v_relpos_table, v_q_norm_a, v_k_norm_a, v_q_norm_b, v_k_norm_b, v_sinks, v_ssm_w_in, v_ssm_conv_w, v_ssm_conv_b, v_ssm_dt_bias, v_ssm_a_log, v_ssm_d, v_ssm_norm, v_ssm_w_out, v_ffn_w_in, v_ffn_conv_w, v_ffn_conv_b, v_ffn_w_out):
    d = dict(zip(ARGS, (x, norm_mix, norm_ffn, attn_w_in, attn_w_out, relpos_table, q_norm_a, k_norm_a, q_norm_b, k_norm_b, sinks, ssm_w_in, ssm_conv_w, ssm_conv_b, ssm_dt_bias, ssm_a_log, ssm_d, ssm_norm, ssm_w_out, ffn_w_in, ffn_conv_w, ffn_conv_b, ffn_w_out, loss_target, m_norm_mix, m_norm_ffn, m_attn_w_in, m_attn_w_out, m_relpos_table, m_q_norm_a, m_k_norm_a, m_q_norm_b, m_k_norm_b, m_sinks, m_ssm_w_in, m_ssm_conv_w, m_ssm_conv_b, m_ssm_dt_bias, m_ssm_a_log, m_ssm_d, m_ssm_norm, m_ssm_w_out, m_ffn_w_in, m_ffn_conv_w, m_ffn_conv_b, m_ffn_w_out, v_norm_mix, v_norm_ffn, v_attn_w_in, v_attn_w_out, v_relpos_table, v_q_norm_a, v_k_norm_a, v_q_norm_b, v_k_norm_b, v_sinks, v_ssm_w_in, v_ssm_conv_w, v_ssm_conv_b, v_ssm_dt_bias, v_ssm_a_log, v_ssm_d, v_ssm_norm, v_ssm_w_out, v_ffn_w_in, v_ffn_conv_w, v_ffn_conv_b, v_ffn_w_out)))
    xi, yi, ci = _pos()
    me = 2 * xi + yi
    CC = D_INNER + 2 * SSM_GROUPS * SSM_STATE
    c4, f4 = CC // N_CHIPS, D_FF // N_CHIPS

    P = {k: d[k] for k in ["norm_mix", "norm_ffn", "q_norm_a", "k_norm_a", "q_norm_b", "k_norm_b", "sinks",
                           "ssm_dt_bias", "ssm_a_log", "ssm_d", "ffn_conv_b"]}
    P["relpos_table"] = d["relpos_table"][0]
    comm = MeshComm(d, xi, yi, ci)
    W = comm.first_weights()
    lpart, dx, gW, gP = local_step(d["x"][0], d["loss_target"][0], W, P, comm)
    loss = lax.psum(lpart[0, 0], ("x", "y", "c"))
    grads = comm.finish(gW["attn_in"], gW["attn_out"])

    flat = jnp.concatenate([gP[k].reshape(-1) for k in SMALL_ORDER])
    flat = jnp.pad(flat, (0, SMALL_ROWS * LANES - flat.shape[0])).reshape(SMALL_ROWS, LANES)
    tot = sum_slots(gather_all(flat, name="small_gather"), name="small_sum").reshape(-1)
    off = 0
    for k in SMALL_ORDER:
        n = int(np.prod(gP[k].shape))
        g = tot[off:off + n].reshape(gP[k].shape)
        off += n
        if k == "ssm_conv_w":
            g = lax.dynamic_slice_in_dim(g, me * c4, c4, axis=1)[None]
        elif k == "ssm_conv_b":
            g = lax.dynamic_slice_in_dim(g, me * c4, c4, axis=1)
        elif k == "ssm_norm":
            g = lax.dynamic_slice_in_dim(g, me * (D_INNER // N_CHIPS), D_INNER // N_CHIPS, axis=1)
        elif k == "ffn_conv_w":
            g = lax.dynamic_slice_in_dim(g, me * f4, f4, axis=2)
        elif k == "relpos_table":
            g = g[None]
        grads[k] = g

    deltas, new_m, new_v = {}, {}, {}
    for k in WEIGHTS:
        deltas[k], new_m[k], new_v[k] = _adamw(d[k], grads[k], d["m_" + k], d["v_" + k], k)
    return (loss, dx[None], *[grads[k] for k in WEIGHTS], *[deltas[k] for k in WEIGHTS],
            *[new_m[k] for k in WEIGHTS], *[new_v[k] for k in WEIGHTS])
```

```python
import functools

import numpy as np
import jax
import jax.numpy as jnp
from jax import lax
from jax.experimental import pallas as pl
from jax.experimental.pallas import tpu as pltpu

F32 = jnp.float32
BF16 = jnp.bfloat16
HI = lax.Precision.HIGHEST

D_MODEL = 1024
CHUNK = 64
EPS = 1e-6
HEAD_DIM = 64
N_HEADS = 8
A_PREV = 8
B_PREV = 2
MAX_REL = 256
D_INNER = 2048
SSM_HEADS = 32
SSM_GROUPS = 4
SSM_STATE = 128
SSM_CONV = 4
D_FF = 2816
FFN_CONV = 3
LANES = 128
SUBLANES = 8
VMEM_LIMIT = 56 * 1024 * 1024
SSD_L = 128

ADAM_LR = 0.001
ADAM_B1 = 0.9
ADAM_B2 = 0.999
ADAM_EPS = 1e-08
ADAM_WD = 0.01
ADAM_STEP = 10

MESH = pl.DeviceIdType.MESH


def _params(*sem):
    return pltpu.CompilerParams(dimension_semantics=sem, vmem_limit_bytes=VMEM_LIMIT)


def _pick(n, want):
    if n <= want:
        return n
    t = (want // LANES) * LANES
    while t >= LANES:
        if n % t == 0:
            return t
        t -= LANES
    return n


MM_ROWS = 512
MM_COLS = 1536
MM_RED = 2048
MM_SHORT = 1024


def matmul(a, b, *, mode, name, out_dtype=F32, residual=None):
    dims = {"nn": (((1,), (0,)), ((), ())), "nt": (((1,), (1,)), ((), ())), "tn": (((0,), (0,)), ((), ()))}[mode]
    if mode == "tn":
        assert residual is None and out_dtype == F32
        (K, M), (K2, N) = a.shape, b.shape
        assert K == K2, (a.shape, b.shape)
        tm, tn, tk = _pick(M, MM_COLS), _pick(N, MM_COLS), _pick(K, MM_RED)

        def body(a_ref, b_ref, o_ref):
            k = pl.program_id(2)
            p = lax.dot_general(a_ref[...].astype(BF16), b_ref[...].astype(BF16), dims, preferred_element_type=F32)

            @pl.when(k == 0)
            def _():
                o_ref[...] = p

            @pl.when(k != 0)
            def _():
                o_ref[...] += p

        return pl.pallas_call(
            body, name=name, grid=(M // tm, N // tn, K // tk),
            in_specs=[pl.BlockSpec((tk, tm), lambda i, j, k: (k, i)), pl.BlockSpec((tk, tn), lambda i, j, k: (k, j))],
            out_specs=pl.BlockSpec((tm, tn), lambda i, j, k: (i, j)),
            out_shape=jax.ShapeDtypeStruct((M, N), F32),
            compiler_params=_params("parallel", "parallel", "arbitrary"),
        )(a, b)

    if mode == "nn":
        (M, K), (K2, N) = a.shape, b.shape
    else:
        (M, K), (N, K2) = a.shape, b.shape
    assert K == K2, (a.shape, b.shape, mode)
    tm, tn = _pick(M, MM_ROWS if K > MM_SHORT else 2 * MM_ROWS), _pick(N, MM_COLS)

    def body(*refs):
        a_ref, b_ref = refs[:2]
        o_ref = refs[-1]
        r = lax.dot_general(a_ref[...].astype(BF16), b_ref[...].astype(BF16), dims, preferred_element_type=F32)
        if residual is not None:
            r = r + refs[2][...].astype(F32)
        o_ref[...] = r.astype(o_ref.dtype)

    a_spec = pl.BlockSpec((tm, K), lambda j, i: (i, 0))
    b_spec = pl.BlockSpec((K, tn), lambda j, i: (0, j)) if mode == "nn" else pl.BlockSpec((tn, K), lambda j, i: (j, 0))
    o_spec = pl.BlockSpec((tm, tn), lambda j, i: (i, j))
    in_specs = [a_spec, b_spec] + ([o_spec] if residual is not None else [])
    args = (a, b) + ((residual,) if residual is not None else ())
    return pl.pallas_call(
        body, name=name, grid=(N // tn, M // tm),
        in_specs=in_specs, out_specs=o_spec,
        out_shape=jax.ShapeDtypeStruct((M, N), out_dtype),
        compiler_params=_params("parallel", "parallel"),
    )(*args)


def matmul_norm(a, b, residual, g, *, name):
    (M, K), (_, N) = a.shape, b.shape
    tm = _pick(M, MM_ROWS)

    def body(a_ref, b_ref, r_ref, g_ref, x_ref, h_ref):
        x = jnp.dot(a_ref[...].astype(BF16), b_ref[...].astype(BF16), preferred_element_type=F32) + r_ref[...]
        x_ref[...] = x
        h_ref[...] = f_rmsnorm(x, g_ref[...])[0].astype(BF16)

    row = pl.BlockSpec((tm, N), lambda i: (i, 0))
    return pl.pallas_call(
        body, name=name, grid=(M // tm,),
        in_specs=[pl.BlockSpec((tm, K), lambda i: (i, 0)), pl.BlockSpec((K, N), lambda i: (0, 0)), row,
                  pl.BlockSpec((1, N), lambda i: (0, 0))],
        out_specs=[row, row],
        out_shape=[jax.ShapeDtypeStruct((M, N), F32), jax.ShapeDtypeStruct((M, N), BF16)],
        compiler_params=_params("parallel"),
    )(a, b, residual, g)


def matmul_dnorm(a, b, partial, x, g, dres, *, name):
    (M, K), (N, _) = a.shape, b.shape
    tm = _pick(M, MM_ROWS)
    has_part = partial is not None

    def body(*refs):
        a_ref, b_ref = refs[:2]
        x_ref, g_ref, dres_ref, dx_ref, dg_ref = refs[-5:]
        dh = lax.dot_general(a_ref[...].astype(BF16), b_ref[...].astype(BF16), _NT, preferred_element_type=F32)
        if has_part:
            dh = dh + refs[2][...]
        xv = x_ref[...]
        r = lax.rsqrt(jnp.mean(xv * xv, axis=-1, keepdims=True) + EPS)
        xhat = xv * r
        dxh = dh * g_ref[...]
        dx_ref[...] = dres_ref[...] + r * (dxh - xhat * jnp.mean(dxh * xhat, axis=-1, keepdims=True))
        dg = jnp.sum(dh * xhat, axis=0, keepdims=True)

        @pl.when(pl.program_id(0) == 0)
        def _():
            dg_ref[...] = dg

        @pl.when(pl.program_id(0) != 0)
        def _():
            dg_ref[...] += dg

    row = pl.BlockSpec((tm, N), lambda i: (i, 0))
    vec = pl.BlockSpec((1, N), lambda i: (0, 0))
    in_specs = [pl.BlockSpec((tm, K), lambda i: (i, 0)), pl.BlockSpec((N, K), lambda i: (0, 0))]
    args = [a, b]
    if has_part:
        in_specs.append(row)
        args.append(partial)
    return pl.pallas_call(
        body, name=name, grid=(M // tm,), in_specs=in_specs + [row, vec, row], out_specs=[row, vec],
        out_shape=[jax.ShapeDtypeStruct((M, N), F32), jax.ShapeDtypeStruct((1, N), F32)],
        compiler_params=_params("arbitrary"),
    )(*args, x, g, dres)


def rowwise(f, rows, params, outs, *, name, tm=256):
    S = rows[0].shape[0]
    tm = _row_tile(S, tm)
    nr, npar = len(rows), len(params)

    def body(*refs):
        vals = [r[...].astype(F32) for r in refs[:nr + npar]]
        res = f(*vals)
        for o_ref, r in zip(refs[nr + npar:], res):
            o_ref[...] = r.astype(o_ref.dtype)

    in_specs = [pl.BlockSpec((tm, r.shape[1]), lambda i: (i, 0)) for r in rows]
    in_specs += [pl.BlockSpec(p.shape, lambda i: (0, 0)) for p in params]
    out_specs = [pl.BlockSpec((tm, c), lambda i: (i, 0)) for c, _ in outs]
    out_shape = [jax.ShapeDtypeStruct((S, c), dt) for c, dt in outs]
    return pl.pallas_call(body, name=name, grid=(S // tm,), in_specs=in_specs, out_specs=out_specs,
                          out_shape=out_shape, compiler_params=_params("parallel"))(*rows, *params)


def rowwise_vjp(f, rows, params, cots, drow, dpar, *, name, tm=256, cot_skip=None):
    S = rows[0].shape[0]
    tm = _row_tile(S, tm)
    nr, npar, nc = len(rows), len(params), len(cots)
    skip = [0] * nc if cot_skip is None else [s // tm for s in cot_skip]
    assert cot_skip is None or all(s % tm == 0 for s in cot_skip)

    def body(*refs):
        vals = [r[...].astype(F32) for r in refs[:nr + npar]]
        cvals = [r[...].astype(F32) for r in refs[nr + npar:nr + npar + nc]]
        o_refs = refs[nr + npar + nc:]
        want = [ri for ri, _ in drow] + [nr + pi for pi in dpar]

        def f_want(*d):
            full = list(vals)
            for k, v in zip(want, d):
                full[k] = v
            return f(*full)

        _, vjp = jax.vjp(f_want, *[vals[k] for k in want])
        grads = vjp(tuple(cvals))
        for o_ref, g in zip(o_refs[:len(drow)], grads):
            o_ref[...] = g.astype(o_ref.dtype)
        first = pl.program_id(0) == 0
        for o_ref, g in zip(o_refs[len(drow):], grads[len(drow):]):
            g = g.astype(F32)

            @pl.when(first)
            def _(o_ref=o_ref, g=g):
                o_ref[...] = g

            @pl.when(jnp.logical_not(first))
            def _(o_ref=o_ref, g=g):
                o_ref[...] += g

    in_specs = [pl.BlockSpec((tm, r.shape[1]), lambda i: (i, 0)) for r in rows]
    in_specs += [pl.BlockSpec(p.shape, lambda i: (0, 0)) for p in params]
    in_specs += [pl.BlockSpec((tm, c.shape[1]), lambda i, s=s: (i + s, 0)) for c, s in zip(cots, skip)]
    out_specs = [pl.BlockSpec((tm, rows[ri].shape[1]), lambda i: (i, 0)) for ri, _ in drow]
    out_specs += [pl.BlockSpec(params[pi].shape, lambda i: (0, 0)) for pi in dpar]
    out_shape = [jax.ShapeDtypeStruct(rows[ri].shape, dt) for ri, dt in drow]
    out_shape += [jax.ShapeDtypeStruct(params[pi].shape, F32) for pi in dpar]
    return pl.pallas_call(body, name=name, grid=(S // tm,), in_specs=in_specs, out_specs=out_specs,
                          out_shape=out_shape, compiler_params=_params("arbitrary"))(*rows, *params, *cots)


HALO = 2 * SUBLANES
CONV_ROWS = 64


def dwconv_fwd(x, w, b, post, extra, outs, *, name, tm=256):
    S, C = x.shape
    K = w.shape[0]
    tm = min(tm, S)
    hb = tm // HALO
    ne = len(extra)

    def body(*refs):
        x_ref, halo_ref, w_ref, b_ref = refs[:4]
        e_refs = refs[4:4 + ne]
        o_refs = refs[4 + ne:4 + ne + len(outs)]
        buf = refs[-1]
        i = pl.program_id(0)
        buf[0:HALO, :] = jnp.where(i == 0, 0.0, halo_ref[...].astype(F32))
        buf[HALO:HALO + tm, :] = x_ref[...].astype(F32)
        for c0 in range(0, C, LANES):
            cs = slice(c0, c0 + LANES)
            for r0 in range(0, tm, CONV_ROWS):
                rs = slice(r0, r0 + CONV_ROWS)
                acc = jnp.broadcast_to(b_ref[:, cs], (CONV_ROWS, LANES))
                for k in range(K):
                    acc = acc + w_ref[k:k + 1, cs] * buf[pl.ds(HALO - (K - 1) + k + r0, CONV_ROWS), cs]
                for o_ref, r in zip(o_refs, post(acc, *[e[rs, cs].astype(F32) for e in e_refs])):
                    o_ref[rs, cs] = r.astype(o_ref.dtype)

    row = pl.BlockSpec((tm, C), lambda i: (i, 0))
    return pl.pallas_call(
        body, name=name, grid=(S // tm,),
        in_specs=[row,
                  pl.BlockSpec((HALO, C), lambda i: (jnp.maximum(i * hb - 1, 0), 0)),
                  pl.BlockSpec((K, C), lambda i: (0, 0)),
                  pl.BlockSpec((1, C), lambda i: (0, 0))] + [row] * ne,
        out_specs=[row] * len(outs),
        out_shape=[jax.ShapeDtypeStruct((S, C), dt) for dt in outs],
        scratch_shapes=[pltpu.VMEM((HALO + tm, C), F32)],
        compiler_params=_params("parallel"),
    )(x, x, w, b, *extra)


def dwconv_bwd(x, w, srcs, dy_fn, extra_outs, *, name, tm=256):
    S, C = x.shape
    K = w.shape[0]
    tm = min(tm, S)
    hb = tm // HALO
    n = S // tm
    groups = [s if isinstance(s, tuple) else (s,) for s in srcs]
    flat = [a for g in groups for a in g]
    nf = len(flat)

    def body(*refs):
        x_ref, xh_ref, w_ref = refs[:3]
        dx_ref, dw_ref, db_ref = refs[3 + 2 * nf:6 + 2 * nf]
        e_refs = refs[6 + 2 * nf:6 + 2 * nf + len(extra_outs)]
        bx, bd = refs[-2:]

        def strips(first, c0, rs):
            out, at = [], first
            for g in groups:
                off = 0
                for a in g:
                    if off <= c0 < off + a.shape[1]:
                        out.append(refs[at][rs, c0 - off:c0 - off + LANES].astype(F32))
                    off += a.shape[1]
                    at += 1
            return out

        i = pl.program_id(0)
        bx[0:HALO, :] = jnp.where(i == 0, 0.0, xh_ref[...].astype(F32))
        bx[HALO:HALO + tm, :] = x_ref[...].astype(F32)

        @pl.when(i == 0)
        def _():
            dw_ref[...] = jnp.zeros_like(dw_ref)
            db_ref[...] = jnp.zeros_like(db_ref)

        for c0 in range(0, C, LANES):
            cs = slice(c0, c0 + LANES)
            dws = [jnp.zeros((1, LANES), F32) for _ in range(K)]
            dbs = jnp.zeros((1, LANES), F32)
            for r0 in range(0, tm, CONV_ROWS):
                rs = slice(r0, r0 + CONV_ROWS)
                res = dy_fn(*strips(3, c0, rs))
                dyv = res[0]
                for e_ref, r in zip(e_refs, res[1:]):
                    e_ref[rs, cs] = r.astype(e_ref.dtype)
                bd[rs, cs] = dyv
                for k in range(K):
                    dws[k] = dws[k] + jnp.sum(dyv * bx[pl.ds(HALO - (K - 1) + k + r0, CONV_ROWS), cs], axis=0,
                                              keepdims=True)
                dbs = dbs + jnp.sum(dyv, axis=0, keepdims=True)
            bd[tm:tm + HALO, cs] = jnp.where(i == n - 1, 0.0, dy_fn(*strips(3 + nf, c0, slice(None)))[0])
            for k in range(K):
                dw_ref[k:k + 1, cs] += dws[k]
            db_ref[:, cs] += dbs
            for r0 in range(0, tm, CONV_ROWS):
                acc = jnp.zeros((CONV_ROWS, LANES), F32)
                for k in range(K):
                    acc = acc + w_ref[k:k + 1, cs] * bd[pl.ds((K - 1) - k + r0, CONV_ROWS), cs]
                dx_ref[r0:r0 + CONV_ROWS, cs] = acc.astype(dx_ref.dtype)

    row = lambda c: pl.BlockSpec((tm, c), lambda i: (i, 0))
    nxt = lambda c: pl.BlockSpec((HALO, c), lambda i: (jnp.minimum((i + 1) * hb, S // HALO - 1), 0))
    return pl.pallas_call(
        body, name=name, grid=(n,),
        in_specs=[row(C), pl.BlockSpec((HALO, C), lambda i: (jnp.maximum(i * hb - 1, 0), 0)),
                  pl.BlockSpec((K, C), lambda i: (0, 0))]
                 + [row(a.shape[1]) for a in flat] + [nxt(a.shape[1]) for a in flat],
        out_specs=[row(C), pl.BlockSpec((K, C), lambda i: (0, 0)), pl.BlockSpec((1, C), lambda i: (0, 0))]
                  + [row(C)] * len(extra_outs),
        out_shape=[jax.ShapeDtypeStruct((S, C), BF16), jax.ShapeDtypeStruct((K, C), F32),
                   jax.ShapeDtypeStruct((1, C), F32)] + [jax.ShapeDtypeStruct((S, C), dt) for dt in extra_outs],
        scratch_shapes=[pltpu.VMEM((HALO + tm, C), F32), pltpu.VMEM((tm + HALO, C), F32)],
        compiler_params=_params("arbitrary"),
    )(x, x, w, *flat, *flat)


def _sigmoid(x):
    return 0.5 * jnp.tanh(0.5 * x) + 0.5


def _silu(x):
    return x * _sigmoid(x)


def _dsilu(x):
    s = _sigmoid(x)
    return s * (1.0 + x * (1.0 - s))


def f_rmsnorm(x, g):
    return (x * lax.rsqrt(jnp.mean(x * x, axis=-1, keepdims=True) + EPS) * g,)


SEL = lax.Precision.HIGH


def _group_norm(x, bd, width):
    ms = jnp.dot(x * x, bd, precision=SEL, preferred_element_type=F32) * (1.0 / width)
    return x * lax.rsqrt(ms + EPS)


def f_qknorm(qkv, gqa, gka, gqb, gkb, bd512, bd128, fold, expand):
    dq = N_HEADS * HEAD_DIM
    qa, ka, va, qb = (qkv[:, i * dq:(i + 1) * dq] for i in range(4))
    kb = qkv[:, 4 * dq:4 * dq + LANES]
    vb = qkv[:, 4 * dq + LANES:4 * dq + 2 * LANES]
    tile8 = lambda g: jnp.dot(g, fold, precision=HI, preferred_element_type=F32)
    qa = _group_norm(qa, bd512, HEAD_DIM) * tile8(gqa)
    ka = _group_norm(ka, bd512, HEAD_DIM) * tile8(gka)
    qb = _group_norm(qb, bd512, HEAD_DIM) * tile8(gqb)
    kb = _group_norm(kb, bd128, HEAD_DIM) * tile8(gkb)[:, :LANES]
    kb = jnp.dot(kb, expand, precision=SEL, preferred_element_type=F32)
    vb = jnp.dot(vb, expand, precision=SEL, preferred_element_type=F32)
    return qa, ka, va, qb, kb, vb


def f_gate_norm(y, z, nw):
    v = y * _silu(z)
    gw = D_INNER // SSM_GROUPS
    parts = []
    for g in range(SSM_GROUPS):
        vg = v[:, g * gw:(g + 1) * gw]
        parts.append(vg * lax.rsqrt(jnp.mean(vg * vg, axis=-1, keepdims=True) + EPS))
    return (jnp.concatenate(parts, axis=-1) * nw,)


ATT_TQ = 256
_NT = (((1,), (1,)), ((), ()))
_TN = (((0,), (0,)), ((), ()))


def _stack_heads(t, head0):
    return jnp.concatenate([jnp.where(head0, t, 0.0), jnp.where(head0, 0.0, t)], axis=0).astype(BF16)


def _attn_probs(qk, bias, valid, snk):
    s = qk * (HEAD_DIM ** -0.5) + bias
    s = jnp.where(valid, s, -jnp.inf)
    m = jnp.max(s, axis=1, keepdims=True)
    if snk is not None:
        m = jnp.maximum(m, snk)
    e = jnp.exp(s - m)
    den = jnp.sum(e, axis=1, keepdims=True)
    if snk is None:
        return e / den, None
    es = jnp.exp(snk - m)
    den = den + es
    return e / den, es / den


def widen_bias(bias, n_prev, nj):
    band = (n_prev + 1) * CHUNK
    wk = (nj + n_prev) * CHUNK
    rows = [jnp.pad(bias, ((0, 0), (0, 0), (j * CHUNK, wk - band - j * CHUNK)), constant_values=-jnp.inf)
            for j in range(nj)]
    return jnp.concatenate(rows, axis=1)


def fold_bias(dbw, n_prev, nj):
    band = (n_prev + 1) * CHUNK
    acc = dbw[:, :CHUNK, :band]
    for j in range(1, nj):
        acc = acc + dbw[:, j * CHUNK:(j + 1) * CHUNK, j * CHUNK:j * CHUNK + band]
    return acc


def attn_fwd(q, k, v, bias_w, sinks, *, n_prev, name, rider=None):
    S = q.shape[0]
    pad = n_prev * CHUNK
    tq = min(ATT_TQ, S)
    wk = tq + pad
    assert bias_w.shape == (N_HEADS, tq, wk), bias_w.shape
    has_sink = sinks is not None

    r_in, r_out, r_shapes, r_sems, r_args = _rider_parts(rider)
    n_own = 5 if has_sink else 4
    n_p, n_i = N_HEADS // 2, S // tq

    def body(*refs):
        q_ref, k_ref, v_ref, bias_ref = refs[:4]
        sink_ref = refs[4] if has_sink else None
        o_ref = refs[n_own + len(r_in)]
        if rider is not None:
            p_id, i_id = pl.program_id(0), pl.program_id(1)
            _ride(rider, refs[n_own:n_own + len(r_in)], refs[n_own + len(r_in) + 1:n_own + len(r_in) + 1 + len(r_out)],
                  refs[n_own + len(r_in) + 1 + len(r_out):],
                  jnp.logical_and(p_id == 0, i_id == 0), jnp.logical_and(p_id == n_p - 1, i_id == 0),
                  jnp.logical_and(p_id == n_p - 1, i_id == n_i - 1))
        start = pl.multiple_of(pl.program_id(1) * tq, tq)
        head0 = lax.broadcasted_iota(jnp.int32, (1, LANES), 1) < HEAD_DIM
        valid = lax.broadcasted_iota(jnp.int32, (1, wk), 1) + start >= pad
        kb = k_ref[pl.ds(start, wk), :]
        vb = v_ref[pl.ds(start, wk), :]
        qk = lax.dot_general(_stack_heads(q_ref[...].astype(F32), head0), kb, _NT, preferred_element_type=F32)
        ps = []
        for r in range(2):
            snk = sink_ref[0, r:r + 1, 0:1] if has_sink else None
            ps.append(_attn_probs(qk[r * tq:(r + 1) * tq, :], bias_ref[r], valid, snk)[0].astype(BF16))
        o2 = jnp.dot(jnp.concatenate(ps, axis=0), vb, preferred_element_type=F32)
        o_ref[...] = jnp.where(head0, o2[:tq, :], o2[tq:, :]).astype(o_ref.dtype)

    in_specs = [pl.BlockSpec((tq, LANES), lambda p, i: (i, p)),
                pl.BlockSpec((pad + S, LANES), lambda p, i: (0, p)),
                pl.BlockSpec((pad + S, LANES), lambda p, i: (0, p)),
                pl.BlockSpec((2, tq, wk), lambda p, i: (p, 0, 0))]
    args = [q, k, v, bias_w]
    if has_sink:
        in_specs.append(pl.BlockSpec((1, 2, LANES), lambda p, i: (p, 0, 0)))
        args.append(sinks)
    res = pl.pallas_call(
        body, name=name, grid=(n_p, n_i), in_specs=in_specs + r_in,
        out_specs=[pl.BlockSpec((tq, LANES), lambda p, i: (i, p))] + r_out,
        out_shape=[jax.ShapeDtypeStruct((S, N_HEADS * HEAD_DIM), BF16)] + r_shapes,
        scratch_shapes=r_sems,
        compiler_params=pltpu.CompilerParams(dimension_semantics=("arbitrary", "arbitrary"), vmem_limit_bytes=VMEM_LIMIT,
                                             has_side_effects=rider is not None),
    )(*args, *r_args)
    return res[0], res[1:]


def attn_bwd(q, k, v, do, bias_w, sinks, *, n_prev, name, rider=None):
    S = q.shape[0]
    pad = n_prev * CHUNK
    tq = min(ATT_TQ, S)
    wk = tq + pad
    assert bias_w.shape == (N_HEADS, tq, wk), bias_w.shape
    has_sink = sinks is not None
    scale = HEAD_DIM ** -0.5

    r_in, r_out, r_shapes, r_sems, r_args = _rider_parts(rider)
    n_own_in = 6 if has_sink else 5
    n_own_out = 5 if has_sink else 4
    n_p, n_i = N_HEADS // 2, S // tq

    def body(*refs):
        q_ref, k_ref, v_ref, do_ref, bias_ref = refs[:5]
        sink_ref = refs[5] if has_sink else None
        o0 = n_own_in + len(r_in)
        dq_ref, dk_ref, dv_ref, db_ref = refs[o0:o0 + 4]
        dsk_ref = refs[o0 + 4] if has_sink else None
        i = pl.program_id(1)
        if rider is not None:
            p_id = pl.program_id(0)
            _ride(rider, refs[n_own_in:o0], refs[o0 + n_own_out:o0 + n_own_out + len(r_out)],
                  refs[o0 + n_own_out + len(r_out):],
                  jnp.logical_and(p_id == 0, i == 0), jnp.logical_and(p_id == n_p // 2, i == 0),
                  jnp.logical_and(p_id == n_p - 1, i == n_i - 1))

        @pl.when(i == 0)
        def _():
            dk_ref[...] = jnp.zeros_like(dk_ref)
            dv_ref[...] = jnp.zeros_like(dv_ref)
            db_ref[...] = jnp.zeros_like(db_ref)
            if has_sink:
                dsk_ref[...] = jnp.zeros_like(dsk_ref)

        start = pl.multiple_of(i * tq, tq)
        head0 = lax.broadcasted_iota(jnp.int32, (1, LANES), 1) < HEAD_DIM
        valid = lax.broadcasted_iota(jnp.int32, (1, wk), 1) + start >= pad
        kb = k_ref[pl.ds(start, wk), :]
        vb = v_ref[pl.ds(start, wk), :]
        q2 = _stack_heads(q_ref[...].astype(F32), head0)
        do2 = _stack_heads(do_ref[...].astype(F32), head0)
        qk = lax.dot_general(q2, kb, _NT, preferred_element_type=F32)
        dp2 = lax.dot_general(do2, vb, _NT, preferred_element_type=F32)
        pbs, dss = [], []
        for r in range(2):
            rows = slice(r * tq, (r + 1) * tq)
            snk = sink_ref[0, r:r + 1, 0:1] if has_sink else None
            p, ps = _attn_probs(qk[rows, :], bias_ref[r], valid, snk)
            dp = dp2[rows, :]
            delta = jnp.sum(p * dp, axis=1, keepdims=True)
            ds = p * (dp - delta)
            db_ref[r] += ds
            if has_sink:
                dsk = -jnp.sum(ps * delta, axis=0, keepdims=True)
                dsk_ref[0, r:r + 1, :] += jnp.broadcast_to(dsk, (1, LANES))
            pbs.append(p.astype(BF16))
            dss.append(ds.astype(BF16))
        ds2 = jnp.concatenate(dss, axis=0)
        dq2 = jnp.dot(ds2, kb, preferred_element_type=F32) * scale
        dq_ref[...] = jnp.where(head0, dq2[:tq, :], dq2[tq:, :])
        dk_ref[pl.ds(start, wk), :] += lax.dot_general(ds2, q2, _TN, preferred_element_type=F32) * scale
        dv_ref[pl.ds(start, wk), :] += lax.dot_general(jnp.concatenate(pbs, axis=0), do2, _TN,
                                                       preferred_element_type=F32)

    row_spec = pl.BlockSpec((tq, LANES), lambda p, i: (i, p))
    kv_spec = pl.BlockSpec((pad + S, LANES), lambda p, i: (0, p))
    bias_spec = pl.BlockSpec((2, tq, wk), lambda p, i: (p, 0, 0))
    sink_spec = pl.BlockSpec((1, 2, LANES), lambda p, i: (p, 0, 0))
    in_specs = [row_spec, kv_spec, kv_spec, row_spec, bias_spec]
    args = [q, k, v, do, bias_w]
    out_specs = [row_spec, kv_spec, kv_spec, bias_spec]
    W = N_HEADS * HEAD_DIM
    out_shape = [jax.ShapeDtypeStruct((S, W), F32), jax.ShapeDtypeStruct((pad + S, W), F32),
                 jax.ShapeDtypeStruct((pad + S, W), F32), jax.ShapeDtypeStruct((N_HEADS, tq, wk), F32)]
    if has_sink:
        in_specs.append(sink_spec)
        args.append(sinks)
        out_specs.append(sink_spec)
        out_shape.append(jax.ShapeDtypeStruct((N_HEADS // 2, 2, LANES), F32))
    res = pl.pallas_call(
        body, name=name, grid=(n_p, n_i), in_specs=in_specs + r_in, out_specs=out_specs + r_out,
        out_shape=out_shape + r_shapes, scratch_shapes=r_sems,
        compiler_params=pltpu.CompilerParams(dimension_semantics=("arbitrary", "arbitrary"), vmem_limit_bytes=VMEM_LIMIT,
                                             has_side_effects=rider is not None),
    )(*args, *r_args)
    return res[:n_own_out], res[n_own_out:]


HP = SSM_HEADS // 2
PAIRS_PER_GROUP = HP // SSM_GROUPS
HEADS_PER_GROUP = SSM_HEADS // SSM_GROUPS
GW = HEADS_PER_GROUP * 64


def _ssd_dt(dtraw, dtb, A, tril):
    lane = lax.broadcasted_iota(jnp.int32, (1, LANES), 1)
    u = dtraw + dtb
    eu = jnp.exp(-jnp.abs(u))
    w1 = 1.0 + eu
    l1p = jnp.where(w1 == 1.0, eu, jnp.log(w1) * eu / jnp.where(w1 == 1.0, 1.0, w1 - 1.0))
    dt = jnp.where(lane < SSM_HEADS, jnp.maximum(u, 0.0) + l1p, 0.0)
    acs = jnp.dot(tril, dt * A, precision=HI, preferred_element_type=F32)
    return u, dt, acs


def _head_expander():
    hw = D_INNER // SSM_HEADS
    return (np.arange(LANES)[:, None] == np.arange(D_INNER)[None, :] // hw).astype(np.float32)


def _select_dot(t, sel):
    hi = t.astype(BF16)
    lo = (t - hi.astype(F32)).astype(BF16)
    return jnp.dot(hi, sel, preferred_element_type=F32) + jnp.dot(lo, sel, preferred_element_type=F32)


def ssd_fwd(xbc, dtraw, dtb, A, dexp, *, name):
    S = xbc.shape[0]
    L = min(SSD_L, S)
    nc = S // L
    N = SSM_STATE
    e_mat = jnp.asarray(_head_expander(), dtype=BF16)

    def body(xs_ref, b_ref, c_ref, dtr_ref, dtb_ref, a_ref, d_ref, e_ref, y_ref, st_out_ref, st_ref, xw_ref):
        c = pl.program_id(0)

        @pl.when(c == 0)
        def _():
            st_ref[...] = jnp.zeros_like(st_ref)

        st_out_ref[0] = st_ref[...]
        ri = lax.broadcasted_iota(jnp.int32, (L, L), 0)
        ci = lax.broadcasted_iota(jnp.int32, (L, L), 1)
        trilb = ri >= ci
        head0 = lax.broadcasted_iota(jnp.int32, (1, LANES), 1) < 64
        _, dt, acs = _ssd_dt(dtr_ref[...], dtb_ref[...], a_ref[...], trilb.astype(F32))
        acsT = acs.T
        last = acs[L - 1:L, :]
        expand = lambda t: _select_dot(t, e_ref[...])
        dte, eae, wte = expand(dt), expand(jnp.exp(acs)), expand(jnp.exp(last - acs) * dt)
        lasts = [last[:, h:h + 1] for h in range(SSM_HEADS)]
        for g in range(SSM_GROUPS):
            Bg = b_ref[:, g * N:(g + 1) * N].astype(BF16)
            Cg = c_ref[:, g * N:(g + 1) * N].astype(BF16)
            CB = lax.dot_general(Cg, Bg, _NT, preferred_element_type=F32)
            Z = lax.dot_general(Cg, st_ref[g * GW:(g + 1) * GW, :].astype(BF16), _NT, preferred_element_type=F32)
            for q in range(PAIRS_PER_GROUP):
                hp = g * PAIRS_PER_GROUP + q
                sl = slice(hp * LANES, (hp + 1) * LANES)
                xs = xs_ref[:, sl]
                xd = xs * dte[:, sl]
                ms, xh = [], []
                for r in range(2):
                    h = 2 * hp + r
                    dec = jnp.exp(jnp.where(trilb, acs[:, h:h + 1] - acsT[h:h + 1, :], -jnp.inf))
                    ms.append((CB * dec).astype(BF16))
                    xh.append(jnp.where(head0 if r == 0 else jnp.logical_not(head0), xd, 0.0).astype(BF16))
                yi = jnp.dot(jnp.concatenate(ms, axis=1), jnp.concatenate(xh, axis=0), preferred_element_type=F32)
                y_ref[:, sl] = yi + Z[:, q * LANES:(q + 1) * LANES] * eae[:, sl] + d_ref[:, sl] * xs
                xw_ref[:, sl] = (xs * wte[:, sl]).astype(BF16)
        for g in range(SSM_GROUPS):
            Bg = b_ref[:, g * N:(g + 1) * N].astype(BF16)
            sn = lax.dot_general(xw_ref[:, g * GW:(g + 1) * GW], Bg, _TN, preferred_element_type=F32)
            for k in range(HEADS_PER_GROUP):
                h = g * HEADS_PER_GROUP + k
                rows = slice(h * 64, (h + 1) * 64)
                st_ref[rows, :] = st_ref[rows, :] * jnp.exp(lasts[h]) + sn[k * 64:(k + 1) * 64, :]

    return pl.pallas_call(
        body, name=name, grid=(nc,),
        in_specs=[pl.BlockSpec((L, D_INNER), lambda c: (c, 0)),
                  pl.BlockSpec((L, SSM_GROUPS * N), lambda c: (c, D_INNER // (SSM_GROUPS * N))),
                  pl.BlockSpec((L, SSM_GROUPS * N), lambda c: (c, D_INNER // (SSM_GROUPS * N) + 1)),
                  pl.BlockSpec((L, LANES), lambda c: (c, 0)),
                  pl.BlockSpec((1, LANES), lambda c: (0, 0)),
                  pl.BlockSpec((1, LANES), lambda c: (0, 0)),
                  pl.BlockSpec((1, D_INNER), lambda c: (0, 0)),
                  pl.BlockSpec((LANES, D_INNER), lambda c: (0, 0))],
        out_specs=[pl.BlockSpec((L, D_INNER), lambda c: (c, 0)),
                   pl.BlockSpec((1, D_INNER, N), lambda c: (c, 0, 0))],
        out_shape=[jax.ShapeDtypeStruct((S, D_INNER), F32), jax.ShapeDtypeStruct((nc, D_INNER, N), F32)],
        scratch_shapes=[pltpu.VMEM((D_INNER, N), F32), pltpu.VMEM((L, D_INNER), BF16)],
        compiler_params=_params("arbitrary"),
    )(xbc, xbc, xbc, dtraw, dtb, A, dexp, e_mat)


def ssd_bwd(xbc, dtraw, dtb, A, dexp, states, dy, *, name, rider=None):
    S = xbc.shape[0]
    L = min(SSD_L, S)
    nc = S // L
    N = SSM_STATE
    e_np = _head_expander()
    e_mat, et_mat = jnp.asarray(e_np, dtype=BF16), jnp.asarray(e_np.T, dtype=BF16)

    r_in, r_out, r_shapes, r_sems, r_args = _rider_parts(rider)

    def body(*refs):
        xs_ref, b_ref, c_ref, dtr_ref, dtb_ref, a_ref, d_ref, e_ref, et_ref, st_in_ref, dy_ref = refs[:11]
        o0 = 11 + len(r_in)
        dxs_ref, db_ref, dc_ref, ddtr_ref, da_ref, ddtb_ref, dd_ref = refs[o0:o0 + 7]
        s0 = o0 + 7 + len(r_out)
        dst_ref, xw_ref, dz_ref, r_ref, dsr_ref, dsc_ref = refs[s0:s0 + 6]
        step = pl.program_id(0)
        if rider is not None:
            _ride(rider, refs[11:o0], refs[o0 + 7:s0], refs[s0 + 6:], step == 0, step == nc // 2, step == nc - 1)

        @pl.when(step == 0)
        def _():
            dst_ref[...] = jnp.zeros_like(dst_ref)
            dsr_ref[...] = jnp.zeros_like(dsr_ref)
            dsc_ref[...] = jnp.zeros_like(dsc_ref)
            da_ref[...] = jnp.zeros_like(da_ref)
            ddtb_ref[...] = jnp.zeros_like(ddtb_ref)
            dd_ref[...] = jnp.zeros_like(dd_ref)

        ri = lax.broadcasted_iota(jnp.int32, (L, L), 0)
        ci = lax.broadcasted_iota(jnp.int32, (L, L), 1)
        trilb = ri >= ci
        lane = lax.broadcasted_iota(jnp.int32, (1, LANES), 1)
        sub = lax.broadcasted_iota(jnp.int32, (LANES, 1), 0)
        head0 = lane < 64
        A = a_ref[...]
        u, dt, acs = _ssd_dt(dtr_ref[...], dtb_ref[...], A, trilb.astype(F32))
        acsT = acs.T
        last = acs[L - 1:L, :]
        elast = jnp.exp(last)
        er = jnp.exp(last - acs)
        wt = er * dt
        expand = lambda t: _select_dot(t, e_ref[...])
        dte, eae, wte = expand(dt), expand(jnp.exp(acs)), expand(wt)
        dlast = jnp.zeros((1, LANES), F32)
        dcbs = []
        for g in range(SSM_GROUPS):
            Bg = b_ref[:, g * N:(g + 1) * N].astype(BF16)
            Cg = c_ref[:, g * N:(g + 1) * N].astype(BF16)
            stg = st_in_ref[0, g * GW:(g + 1) * GW, :]
            dstg = dst_ref[g * GW:(g + 1) * GW, :]
            CB = lax.dot_general(Cg, Bg, _NT, preferred_element_type=F32)
            CBT = lax.dot_general(Bg, Cg, _NT, preferred_element_type=F32)
            Z = lax.dot_general(Cg, stg.astype(BF16), _NT, preferred_element_type=F32)
            U = lax.dot_general(Bg, dstg.astype(BF16), _NT, preferred_element_type=F32)
            dcb = jnp.zeros((L, L), F32)
            for q in range(PAIRS_PER_GROUP):
                hp = g * PAIRS_PER_GROUP + q
                sl = slice(hp * LANES, (hp + 1) * LANES)
                qs = slice(q * LANES, (q + 1) * LANES)
                xs = xs_ref[:, sl]
                dyp = dy_ref[:, sl]
                dtp, eap, wp, Dp = dte[:, sl], eae[:, sl], wte[:, sl], d_ref[:, sl]
                xd = xs * dtp
                dy2 = jnp.concatenate([jnp.where(head0, dyp, 0.0), jnp.where(head0, 0.0, dyp)], axis=0).astype(BF16)
                G2 = lax.dot_general(dy2, xd.astype(BF16), _NT, preferred_element_type=F32)
                mts = []
                for r in range(2):
                    h = 2 * hp + r
                    seg = acs[:, h:h + 1] - acsT[h:h + 1, :]
                    dec = jnp.exp(jnp.where(trilb, seg, -jnp.inf))
                    decT = jnp.exp(jnp.where(ri <= ci, -seg, -jnp.inf))
                    gd = G2[r * L:(r + 1) * L, :] * dec
                    dcb = dcb + gd
                    dseg = gd * CB
                    dsr_ref[:, h:h + 1] = jnp.sum(dseg, axis=1, keepdims=True)
                    dsc_ref[h:h + 1, :] = jnp.sum(dseg, axis=0, keepdims=True)
                    mts.append((CBT * decT).astype(BF16))
                dxd = jnp.dot(jnp.concatenate(mts, axis=1), dy2, preferred_element_type=F32)
                Up = U[:, qs]
                r_ref[0:L, sl] = dyp * Z[:, qs] * eap
                r_ref[L:2 * L, sl] = dxd * xs
                r_ref[2 * L:3 * L, sl] = Up * xs
                dz_ref[:, sl] = (dyp * eap).astype(BF16)
                xw_ref[:, sl] = (xs * wp).astype(BF16)
                dxs_ref[:, sl] = dxd * dtp + Dp * dyp + Up * wp
                dd_ref[:, sl] += jnp.sum(dyp * xs, axis=0, keepdims=True)
            dcbs.append(dcb)
            t = dstg * stg
            for k in range(HEADS_PER_GROUP):
                dlast = dlast + jnp.where(lane == g * HEADS_PER_GROUP + k,
                                          jnp.sum(t[k * 64:(k + 1) * 64, :], keepdims=True), 0.0)
        fold = lambda k: _select_dot(r_ref[k * L:(k + 1) * L, :], et_ref[...])
        r1, r2, dws = fold(0), fold(1), fold(2)
        dww = dws * wt
        ddt = r2 + dws * er
        dacs = r1 - dww + dsr_ref[...] - dsc_ref[...].T
        dlast = dlast * elast + jnp.sum(dww, axis=0, keepdims=True)
        lasts = [last[:, h:h + 1] for h in range(SSM_HEADS)]
        for g in range(SSM_GROUPS):
            Bg = b_ref[:, g * N:(g + 1) * N].astype(BF16)
            Cg = c_ref[:, g * N:(g + 1) * N].astype(BF16)
            gs = slice(g * GW, (g + 1) * GW)
            stb = st_in_ref[0, gs, :].astype(BF16)
            dstb = dst_ref[gs, :].astype(BF16)
            dcbb = dcbs[g].astype(BF16)
            dzg = dz_ref[:, gs]
            dc_ref[:, g * N:(g + 1) * N] = (jnp.dot(dzg, stb, preferred_element_type=F32)
                                            + jnp.dot(dcbb, Bg, preferred_element_type=F32))
            db_ref[:, g * N:(g + 1) * N] = (jnp.dot(xw_ref[:, gs], dstb, preferred_element_type=F32)
                                            + lax.dot_general(dcbb, Cg, _TN, preferred_element_type=F32))
            dsn = lax.dot_general(dzg, Cg, _TN, preferred_element_type=F32)
            for k in range(HEADS_PER_GROUP):
                h = g * HEADS_PER_GROUP + k
                rows = slice(h * 64, (h + 1) * 64)
                dst_ref[rows, :] = dst_ref[rows, :] * jnp.exp(lasts[h]) + dsn[k * 64:(k + 1) * 64, :]
        rowi = lax.broadcasted_iota(jnp.int32, (L, 1), 0)
        dacs = dacs + jnp.where(rowi == L - 1, dlast, 0.0)
        da = jnp.dot((ci >= ri).astype(F32), dacs, precision=HI, preferred_element_type=F32)
        ddt = ddt + da * A
        da_ref[...] += jnp.sum(da * dt, axis=0, keepdims=True)
        ddtr = jnp.where(lane < SSM_HEADS, ddt * _sigmoid(u), 0.0)
        ddtr_ref[...] = ddtr
        ddtb_ref[...] += jnp.sum(ddtr, axis=0, keepdims=True)

    rev = lambda c: nc - 1 - c
    gn = SSM_GROUPS * N
    res = pl.pallas_call(
        body, name=name, grid=(nc,),
        in_specs=[pl.BlockSpec((L, D_INNER), lambda c: (rev(c), 0)),
                  pl.BlockSpec((L, gn), lambda c: (rev(c), D_INNER // gn)),
                  pl.BlockSpec((L, gn), lambda c: (rev(c), D_INNER // gn + 1)),
                  pl.BlockSpec((L, LANES), lambda c: (rev(c), 0)),
                  pl.BlockSpec((1, LANES), lambda c: (0, 0)),
                  pl.BlockSpec((1, LANES), lambda c: (0, 0)),
                  pl.BlockSpec((1, D_INNER), lambda c: (0, 0)),
                  pl.BlockSpec((LANES, D_INNER), lambda c: (0, 0)),
                  pl.BlockSpec((D_INNER, LANES), lambda c: (0, 0)),
                  pl.BlockSpec((1, D_INNER, N), lambda c: (rev(c), 0, 0)),
                  pl.BlockSpec((L, D_INNER), lambda c: (rev(c), 0))] + r_in,
        out_specs=[pl.BlockSpec((L, D_INNER), lambda c: (rev(c), 0)),
                   pl.BlockSpec((L, gn), lambda c: (rev(c), 0)),
                   pl.BlockSpec((L, gn), lambda c: (rev(c), 0)),
                   pl.BlockSpec((L, LANES), lambda c: (rev(c), 0)),
                   pl.BlockSpec((1, LANES), lambda c: (0, 0)),
                   pl.BlockSpec((1, LANES), lambda c: (0, 0)),
                   pl.BlockSpec((1, D_INNER), lambda c: (0, 0))] + r_out,
        out_shape=[jax.ShapeDtypeStruct((S, D_INNER), F32), jax.ShapeDtypeStruct((S, gn), F32),
                   jax.ShapeDtypeStruct((S, gn), F32), jax.ShapeDtypeStruct((S, LANES), F32),
                   jax.ShapeDtypeStruct((1, LANES), F32), jax.ShapeDtypeStruct((1, LANES), F32),
                   jax.ShapeDtypeStruct((1, D_INNER), F32)] + r_shapes,
        scratch_shapes=[pltpu.VMEM((D_INNER, N), F32), pltpu.VMEM((L, D_INNER), BF16), pltpu.VMEM((L, D_INNER), BF16),
                        pltpu.VMEM((3 * L, D_INNER), F32), pltpu.VMEM((L, LANES), F32), pltpu.VMEM((LANES, L), F32)]
                       + r_sems,
        compiler_params=pltpu.CompilerParams(dimension_semantics=("arbitrary",), vmem_limit_bytes=VMEM_LIMIT,
                                             has_side_effects=rider is not None),
    )(xbc, xbc, xbc, dtraw, dtb, A, dexp, e_mat, et_mat, states, dy, *r_args)
    return res[:7], res[7:]


BAND_A = (A_PREV + 1) * CHUNK
REL_W = 640


def _relpos_select():
    k = np.arange(REL_W)
    rel = np.where(k < BAND_A, A_PREV * CHUNK - k, A_PREV * CHUNK - (k - REL_W))
    idx = np.clip(rel, -MAX_REL, MAX_REL) + MAX_REL
    sel = (np.arange(REL_W)[:, None] == idx[None, :]) & (k != BAND_A)[None, :]
    return sel.astype(np.float32)


def relpos_bias(table_pad, *, name):
    def body(t_ref, s_ref, o_ref):
        v = jnp.dot(t_ref[...], s_ref[...], precision=HI, preferred_element_type=F32)
        for h in range(N_HEADS):
            o_ref[h] = pltpu.roll(jnp.broadcast_to(v[h:h + 1, :], (CHUNK, REL_W)), 0, 1, stride=1, stride_axis=0)

    return pl.pallas_call(body, name=name, out_shape=jax.ShapeDtypeStruct((N_HEADS, CHUNK, REL_W), F32),
                          compiler_params=pltpu.CompilerParams(vmem_limit_bytes=VMEM_LIMIT),
                          )(table_pad, jnp.asarray(_relpos_select()))


def relpos_grad(dbias_rev, *, name):
    def body(d_ref, s_ref, o_ref):
        head = lax.broadcasted_iota(jnp.int32, (N_HEADS, 1), 0)
        dv = jnp.zeros((N_HEADS, REL_W), F32)
        for h in range(N_HEADS):
            back = pltpu.roll(d_ref[h], REL_W - (CHUNK - 1), 1, stride=1, stride_axis=0)
            dv = dv + jnp.where(head == h, jnp.sum(back, axis=0, keepdims=True), 0.0)
        o_ref[...] = lax.dot_general(dv, s_ref[...], _NT, precision=HI, preferred_element_type=F32)

    return pl.pallas_call(body, name=name, out_shape=jax.ShapeDtypeStruct((N_HEADS, REL_W), F32),
                          compiler_params=pltpu.CompilerParams(vmem_limit_bytes=VMEM_LIMIT),
                          )(dbias_rev, jnp.asarray(_relpos_select()))


def loss_head(y, t, *, name, tm=256):
    S, D = y.shape
    tm = min(tm, S)

    def body(y_ref, t_ref, dy_ref, l_ref):
        e = y_ref[...] - t_ref[...]
        dy_ref[...] = e * (1.0 / D)

        @pl.when(pl.program_id(0) == 0)
        def _():
            l_ref[...] = jnp.zeros_like(l_ref)

        part = jnp.sum(jnp.sum(e * e, axis=1, keepdims=True), axis=0, keepdims=True) * (0.5 / D)
        l_ref[...] += jnp.broadcast_to(part, l_ref.shape)

    return pl.pallas_call(
        body, name=name, grid=(S // tm,),
        in_specs=[pl.BlockSpec((tm, D), lambda i: (i, 0))] * 2,
        out_specs=[pl.BlockSpec((tm, D), lambda i: (i, 0)), pl.BlockSpec((1, LANES), lambda i: (0, 0))],
        out_shape=[jax.ShapeDtypeStruct((S, D), F32), jax.ShapeDtypeStruct((1, LANES), F32)],
        compiler_params=_params("arbitrary"),
    )(y, t)


def f_adamw(w, g, m, v):
    m = ADAM_B1 * m + (1.0 - ADAM_B1) * g
    v = ADAM_B2 * v + (1.0 - ADAM_B2) * (g * g)
    m_hat = m / (1.0 - ADAM_B1 ** ADAM_STEP)
    v_hat = v / (1.0 - ADAM_B2 ** ADAM_STEP)
    delta = -ADAM_LR * (m_hat / (jnp.sqrt(v_hat) + ADAM_EPS) + ADAM_WD * w)
    return delta, m, v


ANY = pl.BlockSpec(memory_space=pl.ANY)


def _pos():
    return lax.axis_index("x"), lax.axis_index("y"), lax.axis_index("c")


def _other_chips(x, y):
    return [(1 - x, y), (x, 1 - y), (1 - x, 1 - y)]


class Rider:
    def __init__(self, ins, outs, sems, start, mid, finish):
        self.ins, self.outs, self.sems = list(ins), list(outs), list(sems)
        self.start, self.mid, self.finish = start, mid, finish


def _rider_parts(rider):
    if rider is None:
        return [], [], [], [], []
    return [ANY] * len(rider.ins), [ANY] * len(rider.outs), rider.outs, rider.sems, rider.ins


def _ride(rider, ins, outs, sems, first, mid, last):
    pos = _pos()

    @pl.when(first)
    def _():
        rider.start(pos, ins, outs, sems)

    if rider.mid is not None:
        @pl.when(mid)
        def _():
            rider.mid(pos, ins, outs, sems)

    @pl.when(last)
    def _():
        rider.finish(pos, ins, outs, sems)


def run_rider(rider, *, name):
    n_in, n_out = len(rider.ins), len(rider.outs)

    def body(*refs):
        ins, outs, sems = refs[:n_in], refs[n_in:n_in + n_out], refs[n_in + n_out:]
        pos = _pos()
        rider.start(pos, ins, outs, sems)
        if rider.mid is not None:
            rider.mid(pos, ins, outs, sems)
        rider.finish(pos, ins, outs, sems)

    return pl.pallas_call(
        body, name=name, in_specs=[ANY] * n_in, out_specs=[ANY] * n_out, out_shape=rider.outs,
        scratch_shapes=rider.sems, compiler_params=pltpu.CompilerParams(has_side_effects=True),
    )(*rider.ins)


def gather_rider(shards):
    n = len(shards)

    def copies(pos, ins, outs, sems):
        x, y, c = pos
        send, recv, fsend, frecv = sems
        me = 2 * x + y
        sib = (x, y, 1 - c)
        first, arrive, passed, theirs = [], [], [], []
        for i in range(n):
            for j, (px, py) in enumerate(_other_chips(x, y)):
                k = 3 * i + j
                far = dict(device_id=(px, py, c), device_id_type=MESH)
                near = dict(device_id=sib, device_id_type=MESH)
                got = outs[i].at[2 * px + py, c]
                his = outs[i].at[2 * px + py, 1 - c]
                first.append(pltpu.make_async_remote_copy(ins[i].at[c], outs[i].at[me, c], send.at[k], recv.at[k], **far))
                arrive.append(pltpu.make_async_remote_copy(ins[i].at[c], got, send.at[k], recv.at[k], **far))
                passed.append(pltpu.make_async_remote_copy(got, got, fsend.at[k], frecv.at[k], **near))
                theirs.append(pltpu.make_async_remote_copy(his, his, fsend.at[k], frecv.at[k], **near))
        return first, arrive, passed, theirs

    def start(*a):
        for cp in copies(*a)[0]:
            cp.start()

    def mid(*a):
        _, arrive, passed, _ = copies(*a)
        for got, cp in zip(arrive, passed):
            got.wait_recv()
            cp.start()

    def finish(*a):
        first, _, passed, theirs = copies(*a)
        for cp in theirs:
            cp.wait_recv()
        for cp in first + passed:
            cp.wait_send()

    return Rider(shards, [jax.ShapeDtypeStruct((4,) + s.shape, s.dtype) for s in shards],
                 [pltpu.SemaphoreType.DMA((3 * n,))] * 4, start, mid, finish)


def scatter_rider(ps):
    n = len(ps)

    def copies(pos, ins, outs, sems):
        x, y, c = pos
        send, recv = sems
        return [pltpu.make_async_remote_copy(ins[i].at[2 * px + py], outs[i].at[j], send.at[3 * i + j], recv.at[3 * i + j],
                                             device_id=(px, py, c), device_id_type=MESH)
                for i in range(n) for j, (px, py) in enumerate(_other_chips(x, y))]

    def start(*a):
        for cp in copies(*a):
            cp.start()

    def finish(*a):
        for cp in copies(*a):
            cp.wait()

    return Rider(ps, [jax.ShapeDtypeStruct((3,) + p.shape[1:], p.dtype) for p in ps],
                 [pltpu.SemaphoreType.DMA((3 * n,))] * 2, start, None, finish)


def pair_swap_halves(gs, *, name):
    n = len(gs)

    def body(*refs):
        ins, outs = refs[:n], refs[n:2 * n]
        send, recv = refs[2 * n:]
        x, y, c = _pos()
        cps = []
        for i in range(n):
            cp = pltpu.make_async_remote_copy(ins[i].at[1 - c], outs[i], send.at[i], recv.at[i],
                                              device_id=(x, y, 1 - c), device_id_type=MESH)
            cp.start()
            cps.append(cp)
        for cp in cps:
            cp.wait()

    return pl.pallas_call(
        body, name=name, in_specs=[ANY] * n, out_specs=[ANY] * n,
        out_shape=[jax.ShapeDtypeStruct(g.shape[1:], g.dtype) for g in gs],
        scratch_shapes=[pltpu.SemaphoreType.DMA((n,)), pltpu.SemaphoreType.DMA((n,))],
        compiler_params=pltpu.CompilerParams(has_side_effects=True),
    )(*gs)


def pair_share(hs, *, name):
    n = len(hs)

    def body(*refs):
        ins, outs = refs[:n], refs[n:2 * n]
        send, recv = refs[2 * n:]
        x, y, c = _pos()
        cps = []
        for i in range(n):
            cp = pltpu.make_async_remote_copy(ins[i], outs[i], send.at[i], recv.at[i],
                                              device_id=(x, y, 1 - c), device_id_type=MESH)
            cp.start()
            cps.append(cp)
        for cp in cps:
            cp.wait()

    return pl.pallas_call(
        body, name=name, in_specs=[ANY] * n, out_specs=[ANY] * n,
        out_shape=[jax.ShapeDtypeStruct(h.shape, h.dtype) for h in hs],
        scratch_shapes=[pltpu.SemaphoreType.DMA((n,)), pltpu.SemaphoreType.DMA((n,))],
        compiler_params=pltpu.CompilerParams(has_side_effects=True),
    )(*hs)


def gather_all(buf, *, name):
    def body(in_ref, out_ref, send, recv, loc):
        x, y, c = _pos()
        lid = 4 * x + 2 * y + c
        lc = pltpu.make_async_copy(in_ref, out_ref.at[lid], loc.at[0])
        lc.start()
        cps = []
        for k in range(1, 8):
            px = 1 - x if k & 4 else x
            py = 1 - y if k & 2 else y
            pc = 1 - c if k & 1 else c
            cp = pltpu.make_async_remote_copy(in_ref, out_ref.at[lid], send.at[k - 1], recv.at[k - 1],
                                              device_id=(px, py, pc), device_id_type=MESH)
            cp.start()
            cps.append((cp, 4 * px + 2 * py + pc, (px, py, pc)))
        for k, (cp, plid, peer) in enumerate(cps):
            cp.wait_send()
            pltpu.make_async_remote_copy(in_ref, out_ref.at[plid], send.at[k], recv.at[k],
                                         device_id=peer, device_id_type=MESH).wait_recv()
        lc.wait()

    return pl.pallas_call(
        body, name=name, in_specs=[ANY], out_specs=ANY,
        out_shape=jax.ShapeDtypeStruct((8,) + buf.shape, buf.dtype),
        scratch_shapes=[pltpu.SemaphoreType.DMA((7,)), pltpu.SemaphoreType.DMA((7,)), pltpu.SemaphoreType.DMA((1,))],
        compiler_params=pltpu.CompilerParams(has_side_effects=True),
    )(buf)


def sum_slots(a, *, name):
    n = a.shape[0]

    def body(a_ref, o_ref):
        acc = a_ref[0]
        for k in range(1, n):
            acc = acc + a_ref[k]
        o_ref[...] = acc

    return pl.pallas_call(body, name=name, out_shape=jax.ShapeDtypeStruct(a.shape[1:], a.dtype),
                          compiler_params=pltpu.CompilerParams(vmem_limit_bytes=VMEM_LIMIT))(a)


def _row_tile(r, want, mult=16):
    t = (min(want, r) // mult) * mult
    while t >= mult:
        if r % t == 0:
            return t
        t -= mult
    return r


def pair_add(g, r1, csel, *, name):
    _, _, r, C = g.shape
    tr = _row_tile(r, 256)

    def body(g_ref, r_ref, c_ref, p32_ref, pb_ref):
        south = c_ref[0:1, 0:1] == 0.0
        p = jnp.where(south, g_ref[0, 0], g_ref[1, 0]) + r_ref[0]
        p32_ref[0] = p
        pb_ref[0] = p.astype(BF16)

    return pl.pallas_call(
        body, name=name, grid=(4, r // tr),
        in_specs=[pl.BlockSpec((2, 1, tr, C), lambda j, t: (0, j, t, 0)), pl.BlockSpec((1, tr, C), lambda j, t: (j, t, 0)),
                  pl.BlockSpec((1, LANES), lambda j, t: (0, 0))],
        out_specs=[pl.BlockSpec((1, tr, C), lambda j, t: (j, t, 0))] * 2,
        out_shape=[jax.ShapeDtypeStruct((4, r, C), F32), jax.ShapeDtypeStruct((4, r, C), BF16)],
        compiler_params=_params("parallel", "parallel"),
    )(g, r1, csel)


def chip_add(p32, r3, msel, *, name):
    _, r, C = p32.shape
    tr = _row_tile(r, 128)

    def body(p_ref, r_ref, m_ref, o_ref):
        me = m_ref[0:1, 0:1]
        acc = jnp.where(me == 0.0, p_ref[0], jnp.where(me == 1.0, p_ref[1], jnp.where(me == 2.0, p_ref[2], p_ref[3])))
        for j in range(3):
            acc = acc + r_ref[j].astype(F32)
        o_ref[...] = acc

    return pl.pallas_call(
        body, name=name, grid=(r // tr,),
        in_specs=[pl.BlockSpec((4, tr, C), lambda t: (0, t, 0)), pl.BlockSpec((3, tr, C), lambda t: (0, t, 0)),
                  pl.BlockSpec((1, LANES), lambda t: (0, 0))],
        out_specs=pl.BlockSpec((tr, C), lambda t: (t, 0)),
        out_shape=jax.ShapeDtypeStruct((r, C), F32),
        compiler_params=_params("parallel"),
    )(p32, r3, msel)


def _consts():
    i512 = np.arange(N_HEADS * HEAD_DIM)
    i128 = np.arange(LANES)
    bd512 = (i512[:, None] // HEAD_DIM == i512[None, :] // HEAD_DIM).astype(np.float32)
    bd128 = (i128[:, None] // HEAD_DIM == i128[None, :] // HEAD_DIM).astype(np.float32)
    fold = (np.arange(HEAD_DIM)[:, None] == (i512[None, :] % HEAD_DIM)).astype(np.float32)
    grp = N_HEADS // 2 * HEAD_DIM
    expand = ((i128[:, None] // HEAD_DIM == i512[None, :] // grp)
              & (i128[:, None] % HEAD_DIM == i512[None, :] % HEAD_DIM)).astype(np.float32)
    band = (B_PREV + 1) * CHUNK
    rel = np.arange(CHUNK)[:, None] - (np.arange(band)[None, :] - B_PREV * CHUNK)
    slopes = 2.0 ** (-8.0 * np.arange(1, N_HEADS + 1, dtype=np.float32) / N_HEADS)
    bias_b = (-slopes[:, None, None] * np.abs(rel).astype(np.float32)[None]).astype(np.float32)
    return [jnp.asarray(a) for a in (bd512, bd128, fold, expand)], jnp.asarray(bias_b)


def _ffn_fwd(xin, h, l, W, P, next_gain):
    Wi = W["ffn_in"][l]
    gate = matmul(h, Wi[:, :D_FF], mode="nn", name=f"ffn{l}_gate", out_dtype=BF16)
    up = matmul(h, Wi[:, D_FF:], mode="nn", name=f"ffn{l}_up", out_dtype=BF16)
    gc, act = dwconv_fwd(gate, P["ffn_conv_w"][l], P["ffn_conv_b"][l:l + 1], lambda y, u: (y, _silu(y) * u), [up],
                         [BF16, BF16], name=f"ffn{l}_conv")
    saved = (xin, h, gate, gc, up, act)
    if next_gain is None:
        return matmul(act, W["ffn_out"][l], mode="nn", name=f"ffn{l}_out", residual=xin), saved, None
    xout, h_next = matmul_norm(act, W["ffn_out"][l], xin, next_gain, name=f"ffn{l}_out")
    return xout, saved, h_next


def _ffn_bwd(dxout, l, saved, W, P):
    xin, h, gate, gc, up, act = saved
    g = P["norm_ffn"][l:l + 1]
    Wi = W["ffn_in"][l]
    dact = matmul(dxout, W["ffn_out"][l], mode="nt", name=f"ffn{l}_dact", out_dtype=BF16)
    dWo = matmul(act, dxout, mode="tn", name=f"ffn{l}_dwout")
    dgate, dcw, dcb, dup = dwconv_bwd(gate, P["ffn_conv_w"][l], [gc, up, dact],
                                      lambda c, u, da: (da * u * _dsilu(c), da * _silu(c)), [BF16],
                                      name=f"ffn{l}_dconv")
    dh = matmul(dgate, Wi[:, :D_FF], mode="nt", name=f"ffn{l}_dh_gate")
    dxin, dg = matmul_dnorm(dup, Wi[:, D_FF:], dh, xin, g, dxout, name=f"ffn{l}_dh_up")
    dWi = jnp.concatenate([matmul(h, dgate, mode="tn", name=f"ffn{l}_dw_gate"),
                           matmul(h, dup, mode="tn", name=f"ffn{l}_dw_up")], axis=1)
    return dxin, dWi, dWo, dg, dcw, dcb


class NoComm:
    def fwd_rider(self, tag):
        return None

    def fwd_done(self, tag, outs, W, P):
        pass

    def grads(self, tag, cols, rows):
        return None

    def bwd_done(self, tag, outs):
        pass


def local_step(x, tgt, W, P, comm):
    qk_consts, bias_b = _consts()
    pad_rows = lambda t, n: jnp.pad(t, ((n * CHUNK, 0), (0, 0)))
    DQ = N_HEADS * HEAD_DIM

    g_mix0 = P["norm_mix"][0:1]
    (h0,) = rowwise(f_rmsnorm, [x], [g_mix0], [(D_MODEL, BF16)], name="attn_norm")
    qkv = matmul(h0, W["attn_in"], mode="nn", name="attn_qkv")
    qk_par = [P["q_norm_a"], P["k_norm_a"], P["q_norm_b"], P["k_norm_b"]] + qk_consts
    qa, ka, va, qb, kb, vb = rowwise(f_qknorm, [qkv], qk_par, [(DQ, BF16)] * 6, name="attn_qknorm")
    ka, va, kb, vb = pad_rows(ka, A_PREV), pad_rows(va, A_PREV), pad_rows(kb, B_PREV), pad_rows(vb, B_PREV)
    table = jnp.pad(P["relpos_table"], ((0, 0), (0, REL_W - (2 * MAX_REL + 1))))
    nj = min(ATT_TQ, x.shape[0]) // CHUNK
    bias_a = widen_bias(relpos_bias(table, name="relpos_bias")[:, :, :BAND_A], A_PREV, nj)
    bias_b = widen_bias(bias_b, B_PREV, nj)
    sinks = jnp.broadcast_to(P["sinks"].reshape(N_HEADS // 2, 2, 1), (N_HEADS // 2, 2, LANES))
    oa, late = attn_fwd(qa, ka, va, bias_a, None, n_prev=A_PREV, name="attn_a", rider=comm.fwd_rider("a"))
    comm.fwd_done("a", late, W, P)
    ob, late = attn_fwd(qb, kb, vb, bias_b, sinks, n_prev=B_PREV, name="attn_b", rider=comm.fwd_rider("b"))
    comm.fwd_done("b", late, W, P)
    Wao = W["attn_out"]
    x1 = matmul(oa, Wao[:DQ], mode="nn", name="attn_out_a", residual=x)
    x1, hf0 = matmul_norm(ob, Wao[DQ:], x1, P["norm_ffn"][0:1], name="attn_out_b")
    g_mix1 = P["norm_mix"][1:2]
    x2, ffn0, h2 = _ffn_fwd(x1, hf0, 0, W, P, g_mix1)

    Ws = W["ssm_in"]
    CC = D_INNER + 2 * SSM_GROUPS * SSM_STATE
    Wz, Wx = Ws[:, :D_INNER], Ws[:, D_INNER:D_INNER + CC]
    Wdt = jnp.pad(Ws[:, D_INNER + CC:], ((0, 0), (0, LANES - SSM_HEADS)))
    z = matmul(h2, Wz, mode="nn", name="ssm_z", out_dtype=BF16)
    xr = matmul(h2, Wx, mode="nn", name="ssm_xbc", out_dtype=BF16)
    dtraw = matmul(h2, Wdt, mode="nn", name="ssm_dt")
    xc, xbc = dwconv_fwd(xr, P["ssm_conv_w"], P["ssm_conv_b"], lambda y: (y, _silu(y)), [], [BF16, F32],
                         name="ssm_conv")
    pad32 = lambda v: jnp.pad(v, ((0, 0), (0, LANES - SSM_HEADS)))
    A = pad32(-jnp.exp(P["ssm_a_log"]))
    dtb = pad32(P["ssm_dt_bias"])
    dexp = jnp.repeat(P["ssm_d"], D_INNER // SSM_HEADS, axis=1)
    y, states = ssd_fwd(xbc, dtraw, dtb, A, dexp, name="ssd_fwd")
    (y2,) = rowwise(f_gate_norm, [y, z], [P["ssm_norm"]], [(D_INNER, BF16)], name="ssm_gate_norm")
    x3, hf1 = matmul_norm(y2, W["ssm_out"], x2, P["norm_ffn"][1:2], name="ssm_out")
    x4, ffn1, _ = _ffn_fwd(x3, hf1, 1, W, P, None)

    dx4, lpart = loss_head(x4, tgt, name="loss_head")

    dx3, dWfi1, dWfo1, dgf1, dfcw1, dfcb1 = _ffn_bwd(dx4, 1, ffn1, W, P)
    out_f1 = comm.grads("f1", dWfi1, dWfo1)
    dy2 = matmul(dx3, W["ssm_out"], mode="nt", name="ssm_dy")
    dWso = matmul(y2, dx3, mode="tn", name="ssm_dwout")
    dy, dz, dnw = rowwise_vjp(f_gate_norm, [y, z], [P["ssm_norm"]], [dy2], [(0, F32), (1, BF16)], [0],
                              name="ssm_dgate_norm")
    (dxs, dB, dC, ddtraw, dA, ddtb, dDl), sent = ssd_bwd(xbc, dtraw, dtb, A, dexp, states, dy, name="ssd_bwd",
                                                          rider=out_f1)
    comm.bwd_done("f1", sent)
    dxr, dscw, dscb = dwconv_bwd(xr, P["ssm_conv_w"], [xc, (dxs, dB, dC)], lambda c, g: (g * _dsilu(c),), [],
                                 name="ssm_dconv")
    dh2 = matmul(dz, Wz, mode="nt", name="ssm_dh_z")
    dh2 = matmul(dxr, Wx, mode="nt", name="ssm_dh_x", residual=dh2)
    dx2, dgm1 = matmul_dnorm(ddtraw, Wdt, dh2, x2, g_mix1, dx3, name="ssm_dh_dt")
    dWs = jnp.concatenate([matmul(h2, dz, mode="tn", name="ssm_dw_z"),
                           matmul(h2, dxr, mode="tn", name="ssm_dw_x"),
                           matmul(h2, ddtraw, mode="tn", name="ssm_dw_dt")[:, :SSM_HEADS]], axis=1)
    out_s = comm.grads("s", dWs, dWso)

    dx1, dWfi0, dWfo0, dgf0, dfcw0, dfcb0 = _ffn_bwd(dx2, 0, ffn0, W, P)
    out_f0 = comm.grads("f0", dWfi0, dWfo0)
    doa = matmul(dx1, Wao[:DQ], mode="nt", name="attn_do_a", out_dtype=BF16)
    dob = matmul(dx1, Wao[DQ:], mode="nt", name="attn_do_b", out_dtype=BF16)
    dWao = jnp.concatenate([matmul(oa, dx1, mode="tn", name="attn_dwout_a"),
                            matmul(ob, dx1, mode="tn", name="attn_dwout_b")], axis=0)
    (dqa, dka, dva, dbias_a), sent = attn_bwd(qa, ka, va, doa, bias_a, None, n_prev=A_PREV, name="attn_a_bwd",
                                              rider=out_s)
    comm.bwd_done("s", sent)
    (dqb, dkb, dvb, _, dsk), sent = attn_bwd(qb, kb, vb, dob, bias_b, sinks, n_prev=B_PREV, name="attn_b_bwd",
                                             rider=out_f0)
    comm.bwd_done("f0", sent)
    pa, pb = A_PREV * CHUNK, B_PREV * CHUNK
    dqkv, dgqa, dgka, dgqb, dgkb = rowwise_vjp(f_qknorm, [qkv], qk_par, [dqa, dka, dva, dqb, dkb[pb:], dvb[pb:]],
                                               [(0, BF16)], [0, 1, 2, 3], name="attn_dqknorm",
                                               cot_skip=[0, pa, pa, 0, 0, 0])
    dx, dgm0 = matmul_dnorm(dqkv, W["attn_in"], None, x, g_mix0, dx1, name="attn_dh")
    dWai = matmul(h0, dqkv, mode="tn", name="attn_dwin")
    dbias_a = fold_bias(dbias_a, A_PREV, nj)
    dbias_rev = jnp.pad(dbias_a[:, ::-1, :], ((0, 0), (0, 0), (0, REL_W - BAND_A)))
    dtable = relpos_grad(dbias_rev, name="relpos_grad")[:, :2 * MAX_REL + 1]

    gW = {"attn_in": dWai, "attn_out": dWao, "ssm_in": dWs, "ssm_out": dWso,
          "ffn_in": [dWfi0, dWfi1], "ffn_out": [dWfo0, dWfo1]}
    gP = {"norm_mix": jnp.concatenate([dgm0, dgm1], axis=0),
          "norm_ffn": jnp.concatenate([dgf0, dgf1], axis=0),
          "relpos_table": dtable, "q_norm_a": dgqa, "k_norm_a": dgka, "q_norm_b": dgqb, "k_norm_b": dgkb,
          "sinks": dsk[:, :, 0].reshape(1, N_HEADS),
          "ssm_conv_w": dscw, "ssm_conv_b": dscb,
          "ssm_dt_bias": ddtb[:, :SSM_HEADS], "ssm_a_log": dA[:, :SSM_HEADS] * A[:, :SSM_HEADS],
          "ssm_d": dDl.reshape(SSM_HEADS, D_INNER // SSM_HEADS).sum(axis=1).reshape(1, SSM_HEADS),
          "ssm_norm": dnw,
          "ffn_conv_w": jnp.stack([dfcw0, dfcw1]), "ffn_conv_b": jnp.concatenate([dfcb0, dfcb1], axis=0)}
    return lpart, dx, gW, gP


WEIGHTS = ["norm_mix", "norm_ffn", "attn_w_in", "attn_w_out", "relpos_table", "q_norm_a", "k_norm_a", "q_norm_b",
           "k_norm_b", "sinks", "ssm_w_in", "ssm_conv_w", "ssm_conv_b", "ssm_dt_bias", "ssm_a_log", "ssm_d",
           "ssm_norm", "ssm_w_out", "ffn_w_in", "ffn_conv_w", "ffn_conv_b", "ffn_w_out"]
ARGS = ["x"] + WEIGHTS + ["loss_target"] + ["m_" + w for w in WEIGHTS] + ["v_" + w for w in WEIGHTS]
N_CHIPS = 4
SMALL_ROWS = 384
SMALL_ORDER = ["norm_mix", "norm_ffn", "relpos_table", "q_norm_a", "k_norm_a", "q_norm_b", "k_norm_b", "sinks",
               "ssm_dt_bias", "ssm_a_log", "ssm_d", "ffn_conv_b", "ssm_conv_w", "ssm_conv_b", "ssm_norm", "ffn_conv_w"]


def _cols_to_slabs(g):
    K, N = g.shape
    return g.reshape(2, K // 2, N_CHIPS, N // N_CHIPS).transpose(0, 2, 1, 3)


def _rows_to_slabs(g):
    R, C = g.shape
    return g.reshape(N_CHIPS, 2, R // (2 * N_CHIPS), C).transpose(1, 0, 2, 3)


class MeshComm:
    def __init__(self, d, xi, yi, ci):
        self.d, self.ci, self.me = d, ci, 2 * xi + yi
        self.csel = jnp.full((1, LANES), ci, F32)
        self.msel = jnp.full((1, LANES), self.me, F32)
        halves = lambda w: w.reshape((2, -1, w.shape[-1]))
        small = jnp.concatenate([d[k].reshape(-1) for k in ("ssm_conv_w", "ssm_conv_b", "ssm_norm", "ffn_conv_w")])
        small = jnp.pad(small, (0, 2 * 40 * LANES - small.shape[0])).reshape(2, 40, LANES)
        self.shards = {"attn": [halves(d["attn_w_in"][0].astype(BF16)), halves(d["attn_w_out"][0].astype(BF16))],
                       "a": [d["ffn_w_in"].astype(BF16), small],
                       "b": [d["ffn_w_out"].astype(BF16), halves(d["ssm_w_in"][0].astype(BF16)),
                             halves(d["ssm_w_out"][0].astype(BF16))]}
        self.p32, self.mine = {}, {}

    def _whole(self, tag, outs):
        return [lax.dynamic_update_slice_in_dim(g, s[None], self.me, axis=0) for g, s in zip(outs, self.shards[tag])]

    @staticmethod
    def _cat_cols(g):
        return jnp.concatenate([g[j].reshape((-1, g.shape[-1])) for j in range(N_CHIPS)], axis=1)

    def first_weights(self):
        g_ai, g_ao = self._whole("attn", run_rider(gather_rider(self.shards["attn"]), name="gather_attn"))
        return {"attn_in": self._cat_cols(g_ai), "attn_out": g_ao.reshape(-1, D_MODEL)}

    def fwd_rider(self, tag):
        return gather_rider(self.shards[tag])

    def fwd_done(self, tag, outs, W, P):
        if tag == "b":
            g_fo, g_si, g_so = self._whole("b", outs)
            W["ffn_out"] = [g_fo[:, l].reshape(-1, D_MODEL) for l in range(2)]
            W["ssm_in"], W["ssm_out"] = self._cat_cols(g_si), g_so.reshape(-1, D_MODEL)
            return
        g_fi, g_sm = self._whole("a", outs)
        W["ffn_in"] = [jnp.concatenate([g_fi[j, l] for j in range(N_CHIPS)], axis=1) for l in range(2)]
        sm = g_sm.reshape(N_CHIPS, -1)
        CC = D_INNER + 2 * SSM_GROUPS * SSM_STATE
        c4, f4 = CC // N_CHIPS, D_FF // N_CHIPS
        o1 = SSM_CONV * c4
        o2 = o1 + c4
        o3 = o2 + D_INNER // N_CHIPS
        o4 = o3 + 2 * FFN_CONV * f4
        P["ssm_conv_w"] = sm[:, :o1].reshape(N_CHIPS, SSM_CONV, c4).transpose(1, 0, 2).reshape(SSM_CONV, CC)
        P["ssm_conv_b"] = sm[:, o1:o2].reshape(1, CC)
        P["ssm_norm"] = sm[:, o2:o3].reshape(1, D_INNER)
        P["ffn_conv_w"] = sm[:, o3:o4].reshape(N_CHIPS, 2, FFN_CONV, f4).transpose(1, 2, 0, 3).reshape(2, FFN_CONV, D_FF)

    def grads(self, tag, cols, rows):
        slabs = [_cols_to_slabs(cols), _rows_to_slabs(rows)]
        from_sib = pair_swap_halves(slabs, name="grad_pair_swap_" + tag)
        pairs = [pair_add(g, r, self.csel, name=f"grad_pair_add_{tag}{i}") for i, (g, r) in enumerate(zip(slabs, from_sib))]
        self.p32[tag] = [p[0] for p in pairs]
        return scatter_rider([p[1] for p in pairs])

    def bwd_done(self, tag, outs):
        self.mine[tag] = [chip_add(p, r, self.msel, name=f"grad_chip_add_{tag}{i}")
                          for i, (p, r) in enumerate(zip(self.p32[tag], outs))]

    def finish(self, d_attn_in, d_attn_out):
        self.bwd_done("at", run_rider(self.grads("at", d_attn_in, d_attn_out), name="grad_scatter_at"))
        order = ["at", "s", "f0", "f1"]
        mine = [m for t in order for m in self.mine[t]]
        theirs = pair_share(mine, name="grad_pair_share")
        full = [jnp.where(self.ci == 0, jnp.stack([a, b]), jnp.stack([b, a])).reshape((-1, a.shape[-1]))
                for a, b in zip(mine, theirs)]
        ai, ao, si, so, fi0, fo0, fi1, fo1 = full
        return {"attn_w_in": ai[None], "attn_w_out": ao[None], "ssm_w_in": si[None], "ssm_w_out": so[None],
                "ffn_w_in": jnp.stack([fi0, fi1]), "ffn_w_out": jnp.stack([fo0, fo1])}


def _adamw(w, g, m, v, name):
    shp = w.shape
    two = lambda a: a.reshape((-1, shp[-1]))
    outs = [(shp[-1], F32)] * 3
    d, nm, nv = rowwise(f_adamw, [two(w), two(g), two(m), two(v)], [], outs, name="adamw_" + name)
    return d.reshape(shp), nm.reshape(shp), nv.reshape(shp)


def kernel(x, norm_mix, norm_ffn, attn_w_in, attn_w_out, relpos_table, q_norm_a, k_norm_a, q_norm_b, k_norm_b, sinks, ssm_w_in, ssm_conv_w, ssm_conv_b, ssm_dt_bias, ssm_a_log, ssm_d, ssm_norm, ssm_w_out, ffn_w_in, ffn_conv_w, ffn_conv_b, ffn_w_out, loss_target, m_norm_mix, m_norm_ffn, m_attn_w_in, m_attn_w_out, m_relpos_table, m_q_norm_a, m_k_norm_a, m_q_norm_b, m_k_norm_b, m_sinks, m_ssm_w_in, m_ssm_conv_w, m_ssm_conv_b, m_ssm_dt_bias, m_ssm_a_log, m_ssm_d, m_ssm_norm, m_ssm_w_out, m_ffn_w_in, m_ffn_conv_w, m_ffn_conv_b, m_ffn_w_out, v_norm_mix, v_norm_ffn, v_attn_w_in, v_attn_w_out, v_relpos_table, v_q_norm_a, v_k_norm_a, v_q_norm_b, v_k_norm_b, v_sinks, v_ssm_w_in, v_ssm_conv_w, v_ssm_conv_b, v_ssm_dt_bias, v_ssm_a_log, v_ssm_d, v_ssm_norm, v_ssm_w_out, v_ffn_w_in, v_ffn_conv_w, v_ffn_conv_b, v_ffn_w_out):
    d = dict(zip(ARGS, (x, norm_mix, norm_ffn, attn_w_in, attn_w_out, relpos_table, q_norm_a, k_norm_a, q_norm_b, k_norm_b, sinks, ssm_w_in, ssm_conv_w, ssm_conv_b, ssm_dt_bias, ssm_a_log, ssm_d, ssm_norm, ssm_w_out, ffn_w_in, ffn_conv_w, ffn_conv_b, ffn_w_out, loss_target, m_norm_mix, m_norm_ffn, m_attn_w_in, m_attn_w_out, m_relpos_table, m_q_norm_a, m_k_norm_a, m_q_norm_b, m_k_norm_b, m_sinks, m_ssm_w_in, m_ssm_conv_w, m_ssm_conv_b, m_ssm_dt_bias, m_ssm_a_log, m_ssm_d, m_ssm_norm, m_ssm_w_out, m_ffn_w_in, m_ffn_conv_w, m_ffn_conv_b, m_ffn_w_out, v_norm_mix, v_norm_ffn, v_attn_w_in, v_attn_w_out, v_relpos_table, v_q_norm_a, v_k_norm_a, v_q_norm_b, v_k_norm_b, v_sinks, v_ssm_w_in, v_ssm_conv_w, v_ssm_conv_b, v_ssm_dt_bias, v_ssm_a_log, v_ssm_d, v_ssm_norm, v_ssm_w_out, v_ffn_w_in, v_ffn_conv_w, v_ffn_conv_b, v_ffn_w_out)))
    xi, yi, ci = _pos()
    me = 2 * xi + yi
    CC = D_INNER + 2 * SSM_GROUPS * SSM_STATE
    c4, f4 = CC // N_CHIPS, D_FF // N_CHIPS

    P = {k: d[k] for k in ["norm_mix", "norm_ffn", "q_norm_a", "k_norm_a", "q_norm_b", "k_norm_b", "sinks",
                           "ssm_dt_bias", "ssm_a_log", "ssm_d", "ffn_conv_b"]}
    P["relpos_table"] = d["relpos_table"][0]
    comm = MeshComm(d, xi, yi, ci)
    W = comm.first_weights()
    lpart, dx, gW, gP = local_step(d["x"][0], d["loss_target"][0], W, P, comm)
    loss = lax.psum(lpart[0, 0], ("x", "y", "c"))
    grads = comm.finish(gW["attn_in"], gW["attn_out"])

    flat = jnp.concatenate([gP[k].reshape(-1) for k in SMALL_ORDER])
    flat = jnp.pad(flat, (0, SMALL_ROWS * LANES - flat.shape[0])).reshape(SMALL_ROWS, LANES)
    tot = sum_slots(gather_all(flat, name="small_gather"), name="small_sum").reshape(-1)
    off = 0
    for k in SMALL_ORDER:
        n = int(np.prod(gP[k].shape))
        g = tot[off:off + n].reshape(gP[k].shape)
        off += n
        if k == "ssm_conv_w":
            g = lax.dynamic_slice_in_dim(g, me * c4, c4, axis=1)[None]
        elif k == "ssm_conv_b":
            g = lax.dynamic_slice_in_dim(g, me * c4, c4, axis=1)
        elif k == "ssm_norm":
            g = lax.dynamic_slice_in_dim(g, me * (D_INNER // N_CHIPS), D_INNER // N_CHIPS, axis=1)
        elif k == "ffn_conv_w":
            g = lax.dynamic_slice_in_dim(g, me * f4, f4, axis=2)
        elif k == "relpos_table":
            g = g[None]
        grads[k] = g

    deltas, new_m, new_v = {}, {}, {}
    for k in WEIGHTS:
        deltas[k], new_m[k], new_v[k] = _adamw(d[k], grads[k], d["m_" + k], d["v_" + k], k)
    return (loss, dx[None], *[grads[k] for k in WEIGHTS], *[deltas[k] for k in WEIGHTS],
            *[new_m[k] for k in WEIGHTS], *[new_v[k] for k in WEIGHTS])
```

```python
import functools

import numpy as np
import jax
import jax.numpy as jnp
from jax import lax
from jax.experimental import pallas as pl
from jax.experimental.pallas import tpu as pltpu

F32 = jnp.float32
BF16 = jnp.bfloat16
HI = lax.Precision.HIGHEST

D_MODEL = 1024
CHUNK = 64
EPS = 1e-6
HEAD_DIM = 64
N_HEADS = 8
A_PREV = 8
B_PREV = 2
MAX_REL = 256
D_INNER = 2048
SSM_HEADS = 32
SSM_GROUPS = 4
SSM_STATE = 128
SSM_CONV = 4
D_FF = 2816
FFN_CONV = 3
LANES = 128
SUBLANES = 8
VMEM_LIMIT = 56 * 1024 * 1024
SSD_L = 128

ADAM_LR = 0.001
ADAM_B1 = 0.9
ADAM_B2 = 0.999
ADAM_EPS = 1e-08
ADAM_WD = 0.01
ADAM_STEP = 10

MESH = pl.DeviceIdType.MESH


def _params(*sem):
    return pltpu.CompilerParams(dimension_semantics=sem, vmem_limit_bytes=VMEM_LIMIT)


def _pick(n, want):
    if n <= want:
        return n
    t = (want // LANES) * LANES
    while t >= LANES:
        if n % t == 0:
            return t
        t -= LANES
    return n


MM_ROWS = 512
MM_COLS = 1536
MM_RED = 2048
MM_SHORT = 1024


def matmul(a, b, *, mode, name, out_dtype=F32, residual=None):
    dims = {"nn": (((1,), (0,)), ((), ())), "nt": (((1,), (1,)), ((), ())), "tn": (((0,), (0,)), ((), ()))}[mode]
    if mode == "tn":
        assert residual is None and out_dtype == F32
        (K, M), (K2, N) = a.shape, b.shape
        assert K == K2, (a.shape, b.shape)
        tm, tn, tk = _pick(M, MM_COLS), _pick(N, MM_COLS), _pick(K, MM_RED)

        def body(a_ref, b_ref, o_ref):
            k = pl.program_id(2)
            p = lax.dot_general(a_ref[...].astype(BF16), b_ref[...].astype(BF16), dims, preferred_element_type=F32)

            @pl.when(k == 0)
            def _():
                o_ref[...] = p

            @pl.when(k != 0)
            def _():
                o_ref[...] += p

        return pl.pallas_call(
            body, name=name, grid=(M // tm, N // tn, K // tk),
            in_specs=[pl.BlockSpec((tk, tm), lambda i, j, k: (k, i)), pl.BlockSpec((tk, tn), lambda i, j, k: (k, j))],
            out_specs=pl.BlockSpec((tm, tn), lambda i, j, k: (i, j)),
            out_shape=jax.ShapeDtypeStruct((M, N), F32),
            compiler_params=_params("parallel", "parallel", "arbitrary"),
        )(a, b)

    if mode == "nn":
        (M, K), (K2, N) = a.shape, b.shape
    else:
        (M, K), (N, K2) = a.shape, b.shape
    assert K == K2, (a.shape, b.shape, mode)
    tm, tn = _pick(M, MM_ROWS if K > MM_SHORT else 2 * MM_ROWS), _pick(N, MM_COLS)

    def body(*refs):
        a_ref, b_ref = refs[:2]
        o_ref = refs[-1]
        r = lax.dot_general(a_ref[...].astype(BF16), b_ref[...].astype(BF16), dims, preferred_element_type=F32)
        if residual is not None:
            r = r + refs[2][...].astype(F32)
        o_ref[...] = r.astype(o_ref.dtype)

    a_spec = pl.BlockSpec((tm, K), lambda j, i: (i, 0))
    b_spec = pl.BlockSpec((K, tn), lambda j, i: (0, j)) if mode == "nn" else pl.BlockSpec((tn, K), lambda j, i: (j, 0))
    o_spec = pl.BlockSpec((tm, tn), lambda j, i: (i, j))
    in_specs = [a_spec, b_spec] + ([o_spec] if residual is not None else [])
    args = (a, b) + ((residual,) if residual is not None else ())
    return pl.pallas_call(
        body, name=name, grid=(N // tn, M // tm),
        in_specs=in_specs, out_specs=o_spec,
        out_shape=jax.ShapeDtypeStruct((M, N), out_dtype),
        compiler_params=_params("parallel", "parallel"),
    )(*args)


def matmul_norm(a, b, residual, g, *, name):
    (M, K), (_, N) = a.shape, b.shape
    tm = _pick(M, MM_ROWS)

    def body(a_ref, b_ref, r_ref, g_ref, x_ref, h_ref):
        x = jnp.dot(a_ref[...].astype(BF16), b_ref[...].astype(BF16), preferred_element_type=F32) + r_ref[...]
        x_ref[...] = x
        h_ref[...] = f_rmsnorm(x, g_ref[...])[0].astype(BF16)

    row = pl.BlockSpec((tm, N), lambda i: (i, 0))
    return pl.pallas_call(
        body, name=name, grid=(M // tm,),
        in_specs=[pl.BlockSpec((tm, K), lambda i: (i, 0)), pl.BlockSpec((K, N), lambda i: (0, 0)), row,
                  pl.BlockSpec((1, N), lambda i: (0, 0))],
        out_specs=[row, row],
        out_shape=[jax.ShapeDtypeStruct((M, N), F32), jax.ShapeDtypeStruct((M, N), BF16)],
        compiler_params=_params("parallel"),
    )(a, b, residual, g)


def matmul_dnorm(a, b, partial, x, g, dres, *, name):
    (M, K), (N, _) = a.shape, b.shape
    tm = _pick(M, MM_ROWS)
    has_part = partial is not None

    def body(*refs):
        a_ref, b_ref = refs[:2]
        x_ref, g_ref, dres_ref, dx_ref, dg_ref = refs[-5:]
        dh = lax.dot_general(a_ref[...].astype(BF16), b_ref[...].astype(BF16), _NT, preferred_element_type=F32)
        if has_part:
            dh = dh + refs[2][...]
        xv = x_ref[...]
        r = lax.rsqrt(jnp.mean(xv * xv, axis=-1, keepdims=True) + EPS)
        xhat = xv * r
        dxh = dh * g_ref[...]
        dx_ref[...] = dres_ref[...] + r * (dxh - xhat * jnp.mean(dxh * xhat, axis=-1, keepdims=True))
        dg = jnp.sum(dh * xhat, axis=0, keepdims=True)

        @pl.when(pl.program_id(0) == 0)
        def _():
            dg_ref[...] = dg

        @pl.when(pl.program_id(0) != 0)
        def _():
            dg_ref[...] += dg

    row = pl.BlockSpec((tm, N), lambda i: (i, 0))
    vec = pl.BlockSpec((1, N), lambda i: (0, 0))
    in_specs = [pl.BlockSpec((tm, K), lambda i: (i, 0)), pl.BlockSpec((N, K), lambda i: (0, 0))]
    args = [a, b]
    if has_part:
        in_specs.append(row)
        args.append(partial)
    return pl.pallas_call(
        body, name=name, grid=(M // tm,), in_specs=in_specs + [row, vec, row], out_specs=[row, vec],
        out_shape=[jax.ShapeDtypeStruct((M, N), F32), jax.ShapeDtypeStruct((1, N), F32)],
        compiler_params=_params("arbitrary"),
    )(*args, x, g, dres)


def rowwise(f, rows, params, outs, *, name, tm=256, lead=None):
    S = rows[0].shape[0]
    tm = _row_tile(S, tm)
    nr, npar = len(rows), len(params)
    lead = [0] * len(outs) if lead is None else lead
    assert all(ld % tm == 0 for ld in lead)
    padded = [k for k, ld in enumerate(lead) if ld]

    def body(*refs):
        vals = [r[...].astype(F32) for r in refs[:nr + npar]]
        res = f(*vals)
        for o_ref, r in zip(refs[nr + npar + len(padded):], res):
            o_ref[...] = r.astype(o_ref.dtype)

    in_specs = [pl.BlockSpec((tm, r.shape[1]), lambda i: (i, 0)) for r in rows]
    in_specs += [pl.BlockSpec(p.shape, lambda i: (0, 0)) for p in params]
    in_specs += [pl.BlockSpec(memory_space=pl.ANY)] * len(padded)
    zeros = [jnp.zeros((S + lead[k], outs[k][0]), outs[k][1]) for k in padded]
    out_specs = [pl.BlockSpec((tm, c), lambda i, s=ld // tm: (i + s, 0)) for (c, _), ld in zip(outs, lead)]
    out_shape = [jax.ShapeDtypeStruct((S + ld, c), dt) for (c, dt), ld in zip(outs, lead)]
    return pl.pallas_call(body, name=name, grid=(S // tm,), in_specs=in_specs, out_specs=out_specs,
                          out_shape=out_shape, input_output_aliases={nr + npar + n: k for n, k in enumerate(padded)},
                          compiler_params=_params("parallel"))(*rows, *params, *zeros)


def rowwise_vjp(f, rows, params, cots, drow, dpar, *, name, tm=256, cot_skip=None):
    S = rows[0].shape[0]
    tm = _row_tile(S, tm)
    nr, npar, nc = len(rows), len(params), len(cots)
    skip = [0] * nc if cot_skip is None else [s // tm for s in cot_skip]
    assert cot_skip is None or all(s % tm == 0 for s in cot_skip)

    def body(*refs):
        vals = [r[...].astype(F32) for r in refs[:nr + npar]]
        cvals = [r[...].astype(F32) for r in refs[nr + npar:nr + npar + nc]]
        o_refs = refs[nr + npar + nc:]
        want = [ri for ri, _ in drow] + [nr + pi for pi in dpar]

        def f_want(*d):
            full = list(vals)
            for k, v in zip(want, d):
                full[k] = v
            return f(*full)

        _, vjp = jax.vjp(f_want, *[vals[k] for k in want])
        grads = vjp(tuple(cvals))
        for o_ref, g in zip(o_refs[:len(drow)], grads):
            o_ref[...] = g.astype(o_ref.dtype)
        first = pl.program_id(0) == 0
        for o_ref, g in zip(o_refs[len(drow):], grads[len(drow):]):
            g = g.astype(F32)

            @pl.when(first)
            def _(o_ref=o_ref, g=g):
                o_ref[...] = g

            @pl.when(jnp.logical_not(first))
            def _(o_ref=o_ref, g=g):
                o_ref[...] += g

    in_specs = [pl.BlockSpec((tm, r.shape[1]), lambda i: (i, 0)) for r in rows]
    in_specs += [pl.BlockSpec(p.shape, lambda i: (0, 0)) for p in params]
    in_specs += [pl.BlockSpec((tm, c.shape[1]), lambda i, s=s: (i + s, 0)) for c, s in zip(cots, skip)]
    out_specs = [pl.BlockSpec((tm, rows[ri].shape[1]), lambda i: (i, 0)) for ri, _ in drow]
    out_specs += [pl.BlockSpec(params[pi].shape, lambda i: (0, 0)) for pi in dpar]
    out_shape = [jax.ShapeDtypeStruct(rows[ri].shape, dt) for ri, dt in drow]
    out_shape += [jax.ShapeDtypeStruct(params[pi].shape, F32) for pi in dpar]
    return pl.pallas_call(body, name=name, grid=(S // tm,), in_specs=in_specs, out_specs=out_specs,
                          out_shape=out_shape, compiler_params=_params("arbitrary"))(*rows, *params, *cots)


HALO = 2 * SUBLANES
CONV_ROWS = 64


def dwconv_fwd(x, w, b, post, extra, outs, *, name, tm=256):
    S, C = x.shape
    K = w.shape[0]
    tm = min(tm, S)
    hb = tm // HALO
    ne = len(extra)

    def body(*refs):
        x_ref, halo_ref, w_ref, b_ref = refs[:4]
        e_refs = refs[4:4 + ne]
        o_refs = refs[4 + ne:4 + ne + len(outs)]
        buf = refs[-1]
        i = pl.program_id(0)
        buf[0:HALO, :] = jnp.where(i == 0, 0.0, halo_ref[...].astype(F32))
        buf[HALO:HALO + tm, :] = x_ref[...].astype(F32)
        for c0 in range(0, C, LANES):
            cs = slice(c0, c0 + LANES)
            for r0 in range(0, tm, CONV_ROWS):
                rs = slice(r0, r0 + CONV_ROWS)
                acc = jnp.broadcast_to(b_ref[:, cs], (CONV_ROWS, LANES))
                for k in range(K):
                    acc = acc + w_ref[k:k + 1, cs] * buf[pl.ds(HALO - (K - 1) + k + r0, CONV_ROWS), cs]
                for o_ref, r in zip(o_refs, post(acc, *[e[rs, cs].astype(F32) for e in e_refs])):
                    o_ref[rs, cs] = r.astype(o_ref.dtype)

    row = pl.BlockSpec((tm, C), lambda i: (i, 0))
    return pl.pallas_call(
        body, name=name, grid=(S // tm,),
        in_specs=[row,
                  pl.BlockSpec((HALO, C), lambda i: (jnp.maximum(i * hb - 1, 0), 0)),
                  pl.BlockSpec((K, C), lambda i: (0, 0)),
                  pl.BlockSpec((1, C), lambda i: (0, 0))] + [row] * ne,
        out_specs=[row] * len(outs),
        out_shape=[jax.ShapeDtypeStruct((S, C), dt) for dt in outs],
        scratch_shapes=[pltpu.VMEM((HALO + tm, C), F32)],
        compiler_params=_params("parallel"),
    )(x, x, w, b, *extra)


def dwconv_bwd(x, w, srcs, dy_fn, extra_outs, *, name, tm=256):
    S, C = x.shape
    K = w.shape[0]
    tm = min(tm, S)
    hb = tm // HALO
    n = S // tm
    groups = [s if isinstance(s, tuple) else (s,) for s in srcs]
    flat = [a for g in groups for a in g]
    nf = len(flat)

    def body(*refs):
        x_ref, xh_ref, w_ref = refs[:3]
        dx_ref, dw_ref, db_ref = refs[3 + 2 * nf:6 + 2 * nf]
        e_refs = refs[6 + 2 * nf:6 + 2 * nf + len(extra_outs)]
        bx, bd = refs[-2:]

        def strips(first, c0, rs):
            out, at = [], first
            for g in groups:
                off = 0
                for a in g:
                    if off <= c0 < off + a.shape[1]:
                        out.append(refs[at][rs, c0 - off:c0 - off + LANES].astype(F32))
                    off += a.shape[1]
                    at += 1
            return out

        i = pl.program_id(0)
        bx[0:HALO, :] = jnp.where(i == 0, 0.0, xh_ref[...].astype(F32))
        bx[HALO:HALO + tm, :] = x_ref[...].astype(F32)

        @pl.when(i == 0)
        def _():
            dw_ref[...] = jnp.zeros_like(dw_ref)
            db_ref[...] = jnp.zeros_like(db_ref)

        for c0 in range(0, C, LANES):
            cs = slice(c0, c0 + LANES)
            dws = [jnp.zeros((1, LANES), F32) for _ in range(K)]
            dbs = jnp.zeros((1, LANES), F32)
            for r0 in range(0, tm, CONV_ROWS):
                rs = slice(r0, r0 + CONV_ROWS)
                res = dy_fn(*strips(3, c0, rs))
                dyv = res[0]
                for e_ref, r in zip(e_refs, res[1:]):
                    e_ref[rs, cs] = r.astype(e_ref.dtype)
                bd[rs, cs] = dyv
                for k in range(K):
                    dws[k] = dws[k] + jnp.sum(dyv * bx[pl.ds(HALO - (K - 1) + k + r0, CONV_ROWS), cs], axis=0,
                                              keepdims=True)
                dbs = dbs + jnp.sum(dyv, axis=0, keepdims=True)
            bd[tm:tm + HALO, cs] = jnp.where(i == n - 1, 0.0, dy_fn(*strips(3 + nf, c0, slice(None)))[0])
            for k in range(K):
                dw_ref[k:k + 1, cs] += dws[k]
            db_ref[:, cs] += dbs
            for r0 in range(0, tm, CONV_ROWS):
                acc = jnp.zeros((CONV_ROWS, LANES), F32)
                for k in range(K):
                    acc = acc + w_ref[k:k + 1, cs] * bd[pl.ds((K - 1) - k + r0, CONV_ROWS), cs]
                dx_ref[r0:r0 + CONV_ROWS, cs] = acc.astype(dx_ref.dtype)

    row = lambda c: pl.BlockSpec((tm, c), lambda i: (i, 0))
    nxt = lambda c: pl.BlockSpec((HALO, c), lambda i: (jnp.minimum((i + 1) * hb, S // HALO - 1), 0))
    return pl.pallas_call(
        body, name=name, grid=(n,),
        in_specs=[row(C), pl.BlockSpec((HALO, C), lambda i: (jnp.maximum(i * hb - 1, 0), 0)),
                  pl.BlockSpec((K, C), lambda i: (0, 0))]
                 + [row(a.shape[1]) for a in flat] + [nxt(a.shape[1]) for a in flat],
        out_specs=[row(C), pl.BlockSpec((K, C), lambda i: (0, 0)), pl.BlockSpec((1, C), lambda i: (0, 0))]
                  + [row(C)] * len(extra_outs),
        out_shape=[jax.ShapeDtypeStruct((S, C), BF16), jax.ShapeDtypeStruct((K, C), F32),
                   jax.ShapeDtypeStruct((1, C), F32)] + [jax.ShapeDtypeStruct((S, C), dt) for dt in extra_outs],
        scratch_shapes=[pltpu.VMEM((HALO + tm, C), F32), pltpu.VMEM((tm + HALO, C), F32)],
        compiler_params=_params("arbitrary"),
    )(x, x, w, *flat, *flat)


def _sigmoid(x):
    return 0.5 * jnp.tanh(0.5 * x) + 0.5


def _silu(x):
    return x * _sigmoid(x)


def _dsilu(x):
    s = _sigmoid(x)
    return s * (1.0 + x * (1.0 - s))


def f_rmsnorm(x, g):
    return (x * lax.rsqrt(jnp.mean(x * x, axis=-1, keepdims=True) + EPS) * g,)


SEL = lax.Precision.HIGH


def _group_norm(x, bd, width):
    ms = jnp.dot(x * x, bd, precision=SEL, preferred_element_type=F32) * (1.0 / width)
    return x * lax.rsqrt(ms + EPS)


def f_qknorm(qkv, gqa, gka, gqb, gkb, bd512, bd128, fold, expand):
    dq = N_HEADS * HEAD_DIM
    qa, ka, va, qb = (qkv[:, i * dq:(i + 1) * dq] for i in range(4))
    kb = qkv[:, 4 * dq:4 * dq + LANES]
    vb = qkv[:, 4 * dq + LANES:4 * dq + 2 * LANES]
    tile8 = lambda g: jnp.dot(g, fold, precision=HI, preferred_element_type=F32)
    qa = _group_norm(qa, bd512, HEAD_DIM) * tile8(gqa)
    ka = _group_norm(ka, bd512, HEAD_DIM) * tile8(gka)
    qb = _group_norm(qb, bd512, HEAD_DIM) * tile8(gqb)
    kb = _group_norm(kb, bd128, HEAD_DIM) * tile8(gkb)[:, :LANES]
    kb = jnp.dot(kb, expand, precision=SEL, preferred_element_type=F32)
    vb = jnp.dot(vb, expand, precision=SEL, preferred_element_type=F32)
    return qa, ka, va, qb, kb, vb


def f_gate_norm(y, z, nw):
    v = y * _silu(z)
    gw = D_INNER // SSM_GROUPS
    parts = []
    for g in range(SSM_GROUPS):
        vg = v[:, g * gw:(g + 1) * gw]
        parts.append(vg * lax.rsqrt(jnp.mean(vg * vg, axis=-1, keepdims=True) + EPS))
    return (jnp.concatenate(parts, axis=-1) * nw,)


ATT_TQ = 256
_NT = (((1,), (1,)), ((), ()))
_TN = (((0,), (0,)), ((), ()))


def _stack_heads(t, head0):
    return jnp.concatenate([jnp.where(head0, t, 0.0), jnp.where(head0, 0.0, t)], axis=0).astype(BF16)


def _attn_probs(qk, bias, valid, snk):
    s = qk * (HEAD_DIM ** -0.5) + bias
    s = jnp.where(valid, s, -jnp.inf)
    m = jnp.max(s, axis=1, keepdims=True)
    if snk is not None:
        m = jnp.maximum(m, snk)
    e = jnp.exp(s - m)
    den = jnp.sum(e, axis=1, keepdims=True)
    if snk is None:
        return e / den, None
    es = jnp.exp(snk - m)
    den = den + es
    return e / den, es / den


def widen_bias(bias, n_prev, nj):
    band = (n_prev + 1) * CHUNK
    wk = (nj + n_prev) * CHUNK
    rows = [jnp.pad(bias, ((0, 0), (0, 0), (j * CHUNK, wk - band - j * CHUNK)), constant_values=-jnp.inf)
            for j in range(nj)]
    return jnp.concatenate(rows, axis=1)


def fold_bias(dbw, n_prev, nj):
    band = (n_prev + 1) * CHUNK
    acc = dbw[:, :CHUNK, :band]
    for j in range(1, nj):
        acc = acc + dbw[:, j * CHUNK:(j + 1) * CHUNK, j * CHUNK:j * CHUNK + band]
    return acc


def attn_fwd(q, k, v, bias_w, sinks, *, n_prev, name, rider=None):
    S = q.shape[0]
    pad = n_prev * CHUNK
    kv_rows = k.shape[0]
    lead = kv_rows - S - pad
    tq = min(ATT_TQ, S)
    wk = tq + pad
    assert bias_w.shape == (N_HEADS, tq, wk), bias_w.shape
    has_sink = sinks is not None

    r_in, r_out, r_shapes, r_sems, r_args = _rider_parts(rider)
    n_own = 5 if has_sink else 4
    n_p, n_i = N_HEADS // 2, S // tq

    def body(*refs):
        q_ref, k_ref, v_ref, bias_ref = refs[:4]
        sink_ref = refs[4] if has_sink else None
        o_ref = refs[n_own + len(r_in)]
        if rider is not None:
            p_id, i_id = pl.program_id(0), pl.program_id(1)
            _ride(rider, refs[n_own:n_own + len(r_in)], refs[n_own + len(r_in) + 1:n_own + len(r_in) + 1 + len(r_out)],
                  refs[n_own + len(r_in) + 1 + len(r_out):],
                  jnp.logical_and(p_id == 0, i_id == 0), jnp.logical_and(p_id == n_p - 1, i_id == 0),
                  jnp.logical_and(p_id == n_p - 1, i_id == n_i - 1))
        start = pl.multiple_of(pl.program_id(1) * tq, tq)
        head0 = lax.broadcasted_iota(jnp.int32, (1, LANES), 1) < HEAD_DIM
        valid = lax.broadcasted_iota(jnp.int32, (1, wk), 1) + start >= pad
        kb = k_ref[pl.ds(pl.multiple_of(start + lead, CHUNK), wk), :]
        vb = v_ref[pl.ds(pl.multiple_of(start + lead, CHUNK), wk), :]
        qk = lax.dot_general(_stack_heads(q_ref[...].astype(F32), head0), kb, _NT, preferred_element_type=F32)
        ps = []
        for r in range(2):
            snk = sink_ref[0, r:r + 1, 0:1] if has_sink else None
            ps.append(_attn_probs(qk[r * tq:(r + 1) * tq, :], bias_ref[r], valid, snk)[0].astype(BF16))
        o2 = jnp.dot(jnp.concatenate(ps, axis=0), vb, preferred_element_type=F32)
        o_ref[...] = jnp.where(head0, o2[:tq, :], o2[tq:, :]).astype(o_ref.dtype)

    in_specs = [pl.BlockSpec((tq, LANES), lambda p, i: (i, p)),
                pl.BlockSpec((kv_rows, LANES), lambda p, i: (0, p)),
                pl.BlockSpec((kv_rows, LANES), lambda p, i: (0, p)),
                pl.BlockSpec((2, tq, wk), lambda p, i: (p, 0, 0))]
    args = [q, k, v, bias_w]
    if has_sink:
        in_specs.append(pl.BlockSpec((1, 2, LANES), lambda p, i: (p, 0, 0)))
        args.append(sinks)
    res = pl.pallas_call(
        body, name=name, grid=(n_p, n_i), in_specs=in_specs + r_in,
        out_specs=[pl.BlockSpec((tq, LANES), lambda p, i: (i, p))] + r_out,
        out_shape=[jax.ShapeDtypeStruct((S, N_HEADS * HEAD_DIM), BF16)] + r_shapes,
        scratch_shapes=r_sems,
        compiler_params=pltpu.CompilerParams(dimension_semantics=("arbitrary", "arbitrary"), vmem_limit_bytes=VMEM_LIMIT,
                                             has_side_effects=rider is not None),
    )(*args, *r_args)
    return res[0], res[1:]


def attn_bwd(q, k, v, do, bias_w, sinks, *, n_prev, name, rider=None):
    S = q.shape[0]
    pad = n_prev * CHUNK
    kv_rows = k.shape[0]
    lead = kv_rows - S - pad
    tq = min(ATT_TQ, S)
    wk = tq + pad
    assert bias_w.shape == (N_HEADS, tq, wk), bias_w.shape
    has_sink = sinks is not None
    scale = HEAD_DIM ** -0.5

    r_in, r_out, r_shapes, r_sems, r_args = _rider_parts(rider)
    n_own_in = 6 if has_sink else 5
    n_own_out = 5 if has_sink else 4
    n_p, n_i = N_HEADS // 2, S // tq

    def body(*refs):
        q_ref, k_ref, v_ref, do_ref, bias_ref = refs[:5]
        sink_ref = refs[5] if has_sink else None
        o0 = n_own_in + len(r_in)
        dq_ref, dk_ref, dv_ref, db_ref = refs[o0:o0 + 4]
        dsk_ref = refs[o0 + 4] if has_sink else None
        i = pl.program_id(1)
        if rider is not None:
            p_id = pl.program_id(0)
            _ride(rider, refs[n_own_in:o0], refs[o0 + n_own_out:o0 + n_own_out + len(r_out)],
                  refs[o0 + n_own_out + len(r_out):],
                  jnp.logical_and(p_id == 0, i == 0), jnp.logical_and(p_id == n_p // 2, i == 0),
                  jnp.logical_and(p_id == n_p - 1, i == n_i - 1))

        @pl.when(i == 0)
        def _():
            dk_ref[...] = jnp.zeros_like(dk_ref)
            dv_ref[...] = jnp.zeros_like(dv_ref)
            db_ref[...] = jnp.zeros_like(db_ref)
            if has_sink:
                dsk_ref[...] = jnp.zeros_like(dsk_ref)

        start = pl.multiple_of(i * tq, tq)
        head0 = lax.broadcasted_iota(jnp.int32, (1, LANES), 1) < HEAD_DIM
        valid = lax.broadcasted_iota(jnp.int32, (1, wk), 1) + start >= pad
        kb = k_ref[pl.ds(pl.multiple_of(start + lead, CHUNK), wk), :]
        vb = v_ref[pl.ds(pl.multiple_of(start + lead, CHUNK), wk), :]
        q2 = _stack_heads(q_ref[...].astype(F32), head0)
        do2 = _stack_heads(do_ref[...].astype(F32), head0)
        qk = lax.dot_general(q2, kb, _NT, preferred_element_type=F32)
        dp2 = lax.dot_general(do2, vb, _NT, preferred_element_type=F32)
        pbs, dss = [], []
        for r in range(2):
            rows = slice(r * tq, (r + 1) * tq)
            snk = sink_ref[0, r:r + 1, 0:1] if has_sink else None
            p, ps = _attn_probs(qk[rows, :], bias_ref[r], valid, snk)
            dp = dp2[rows, :]
            delta = jnp.sum(p * dp, axis=1, keepdims=True)
            ds = p * (dp - delta)
            db_ref[r] += ds
            if has_sink:
                dsk = -jnp.sum(ps * delta, axis=0, keepdims=True)
                dsk_ref[0, r:r + 1, :] += jnp.broadcast_to(dsk, (1, LANES))
            pbs.append(p.astype(BF16))
            dss.append(ds.astype(BF16))
        ds2 = jnp.concatenate(dss, axis=0)
        dq2 = jnp.dot(ds2, kb, preferred_element_type=F32) * scale
        dq_ref[...] = jnp.where(head0, dq2[:tq, :], dq2[tq:, :])
        dk_ref[pl.ds(pl.multiple_of(start + lead, CHUNK), wk), :] += lax.dot_general(ds2, q2, _TN, preferred_element_type=F32) * scale
        dv_ref[pl.ds(pl.multiple_of(start + lead, CHUNK), wk), :] += lax.dot_general(jnp.concatenate(pbs, axis=0), do2, _TN,
                                                       preferred_element_type=F32)

    row_spec = pl.BlockSpec((tq, LANES), lambda p, i: (i, p))
    kv_spec = pl.BlockSpec((kv_rows, LANES), lambda p, i: (0, p))
    bias_spec = pl.BlockSpec((2, tq, wk), lambda p, i: (p, 0, 0))
    sink_spec = pl.BlockSpec((1, 2, LANES), lambda p, i: (p, 0, 0))
    in_specs = [row_spec, kv_spec, kv_spec, row_spec, bias_spec]
    args = [q, k, v, do, bias_w]
    out_specs = [row_spec, kv_spec, kv_spec, bias_spec]
    W = N_HEADS * HEAD_DIM
    out_shape = [jax.ShapeDtypeStruct((S, W), F32), jax.ShapeDtypeStruct((kv_rows, W), F32),
                 jax.ShapeDtypeStruct((kv_rows, W), F32), jax.ShapeDtypeStruct((N_HEADS, tq, wk), F32)]
    if has_sink:
        in_specs.append(sink_spec)
        args.append(sinks)
        out_specs.append(sink_spec)
        out_shape.append(jax.ShapeDtypeStruct((N_HEADS // 2, 2, LANES), F32))
    res = pl.pallas_call(
        body, name=name, grid=(n_p, n_i), in_specs=in_specs + r_in, out_specs=out_specs + r_out,
        out_shape=out_shape + r_shapes, scratch_shapes=r_sems,
        compiler_params=pltpu.CompilerParams(dimension_semantics=("arbitrary", "arbitrary"), vmem_limit_bytes=VMEM_LIMIT,
                                             has_side_effects=rider is not None),
    )(*args, *r_args)
    return res[:n_own_out], res[n_own_out:]


HP = SSM_HEADS // 2
PAIRS_PER_GROUP = HP // SSM_GROUPS
HEADS_PER_GROUP = SSM_HEADS // SSM_GROUPS
GW = HEADS_PER_GROUP * 64


def _ssd_dt(dtraw, dtb, A, tril):
    lane = lax.broadcasted_iota(jnp.int32, (1, LANES), 1)
    u = dtraw + dtb
    eu = jnp.exp(-jnp.abs(u))
    w1 = 1.0 + eu
    l1p = jnp.where(w1 == 1.0, eu, jnp.log(w1) * eu / jnp.where(w1 == 1.0, 1.0, w1 - 1.0))
    dt = jnp.where(lane < SSM_HEADS, jnp.maximum(u, 0.0) + l1p, 0.0)
    acs = jnp.dot(tril, dt * A, precision=HI, preferred_element_type=F32)
    return u, dt, acs


def _head_expander():
    hw = D_INNER // SSM_HEADS
    return (np.arange(LANES)[:, None] == np.arange(D_INNER)[None, :] // hw).astype(np.float32)


def _select_dot(t, sel):
    hi = t.astype(BF16)
    lo = (t - hi.astype(F32)).astype(BF16)
    return jnp.dot(hi, sel, preferred_element_type=F32) + jnp.dot(lo, sel, preferred_element_type=F32)


def ssd_fwd(xbc, dtraw, dtb, A, dexp, *, name):
    S = xbc.shape[0]
    L = min(SSD_L, S)
    nc = S // L
    N = SSM_STATE
    e_mat = jnp.asarray(_head_expander(), dtype=BF16)

    def body(xs_ref, b_ref, c_ref, dtr_ref, dtb_ref, a_ref, d_ref, e_ref, y_ref, st_out_ref, st_ref, xw_ref):
        c = pl.program_id(0)

        @pl.when(c == 0)
        def _():
            st_ref[...] = jnp.zeros_like(st_ref)

        st_out_ref[0] = st_ref[...]
        ri = lax.broadcasted_iota(jnp.int32, (L, L), 0)
        ci = lax.broadcasted_iota(jnp.int32, (L, L), 1)
        trilb = ri >= ci
        head0 = lax.broadcasted_iota(jnp.int32, (1, LANES), 1) < 64
        _, dt, acs = _ssd_dt(dtr_ref[...], dtb_ref[...], a_ref[...], trilb.astype(F32))
        acsT = acs.T
        last = acs[L - 1:L, :]
        expand = lambda t: _select_dot(t, e_ref[...])
        dte, eae, wte = expand(dt), expand(jnp.exp(acs)), expand(jnp.exp(last - acs) * dt)
        lasts = [last[:, h:h + 1] for h in range(SSM_HEADS)]
        for g in range(SSM_GROUPS):
            Bg = b_ref[:, g * N:(g + 1) * N].astype(BF16)
            Cg = c_ref[:, g * N:(g + 1) * N].astype(BF16)
            CB = lax.dot_general(Cg, Bg, _NT, preferred_element_type=F32)
            Z = lax.dot_general(Cg, st_ref[g * GW:(g + 1) * GW, :].astype(BF16), _NT, preferred_element_type=F32)
            for q in range(PAIRS_PER_GROUP):
                hp = g * PAIRS_PER_GROUP + q
                sl = slice(hp * LANES, (hp + 1) * LANES)
                xs = xs_ref[:, sl]
                xd = xs * dte[:, sl]
                ms, xh = [], []
                for r in range(2):
                    h = 2 * hp + r
                    dec = jnp.exp(jnp.where(trilb, acs[:, h:h + 1] - acsT[h:h + 1, :], -jnp.inf))
                    ms.append((CB * dec).astype(BF16))
                    xh.append(jnp.where(head0 if r == 0 else jnp.logical_not(head0), xd, 0.0).astype(BF16))
                yi = jnp.dot(jnp.concatenate(ms, axis=1), jnp.concatenate(xh, axis=0), preferred_element_type=F32)
                y_ref[:, sl] = yi + Z[:, q * LANES:(q + 1) * LANES] * eae[:, sl] + d_ref[:, sl] * xs
                xw_ref[:, sl] = (xs * wte[:, sl]).astype(BF16)
        for g in range(SSM_GROUPS):
            Bg = b_ref[:, g * N:(g + 1) * N].astype(BF16)
            sn = lax.dot_general(xw_ref[:, g * GW:(g + 1) * GW], Bg, _TN, preferred_element_type=F32)
            for k in range(HEADS_PER_GROUP):
                h = g * HEADS_PER_GROUP + k
                rows = slice(h * 64, (h + 1) * 64)
                st_ref[rows, :] = st_ref[rows, :] * jnp.exp(lasts[h]) + sn[k * 64:(k + 1) * 64, :]

    return pl.pallas_call(
        body, name=name, grid=(nc,),
        in_specs=[pl.BlockSpec((L, D_INNER), lambda c: (c, 0)),
                  pl.BlockSpec((L, SSM_GROUPS * N), lambda c: (c, D_INNER // (SSM_GROUPS * N))),
                  pl.BlockSpec((L, SSM_GROUPS * N), lambda c: (c, D_INNER // (SSM_GROUPS * N) + 1)),
                  pl.BlockSpec((L, LANES), lambda c: (c, 0)),
                  pl.BlockSpec((1, LANES), lambda c: (0, 0)),
                  pl.BlockSpec((1, LANES), lambda c: (0, 0)),
                  pl.BlockSpec((1, D_INNER), lambda c: (0, 0)),
                  pl.BlockSpec((LANES, D_INNER), lambda c: (0, 0))],
        out_specs=[pl.BlockSpec((L, D_INNER), lambda c: (c, 0)),
                   pl.BlockSpec((1, D_INNER, N), lambda c: (c, 0, 0))],
        out_shape=[jax.ShapeDtypeStruct((S, D_INNER), F32), jax.ShapeDtypeStruct((nc, D_INNER, N), F32)],
        scratch_shapes=[pltpu.VMEM((D_INNER, N), F32), pltpu.VMEM((L, D_INNER), BF16)],
        compiler_params=_params("arbitrary"),
    )(xbc, xbc, xbc, dtraw, dtb, A, dexp, e_mat)


def ssd_bwd(xbc, dtraw, dtb, A, dexp, states, dy, *, name, rider=None):
    S = xbc.shape[0]
    L = min(SSD_L, S)
    nc = S // L
    N = SSM_STATE
    e_np = _head_expander()
    e_mat, et_mat = jnp.asarray(e_np, dtype=BF16), jnp.asarray(e_np.T, dtype=BF16)

    r_in, r_out, r_shapes, r_sems, r_args = _rider_parts(rider)

    def body(*refs):
        xs_ref, b_ref, c_ref, dtr_ref, dtb_ref, a_ref, d_ref, e_ref, et_ref, st_in_ref, dy_ref = refs[:11]
        o0 = 11 + len(r_in)
        dxs_ref, db_ref, dc_ref, ddtr_ref, da_ref, ddtb_ref, dd_ref = refs[o0:o0 + 7]
        s0 = o0 + 7 + len(r_out)
        dst_ref, xw_ref, dz_ref, r_ref, dsr_ref, dsc_ref = refs[s0:s0 + 6]
        step = pl.program_id(0)
        if rider is not None:
            _ride(rider, refs[11:o0], refs[o0 + 7:s0], refs[s0 + 6:], step == 0, step == nc // 2, step == nc - 1)

        @pl.when(step == 0)
        def _():
            dst_ref[...] = jnp.zeros_like(dst_ref)
            dsr_ref[...] = jnp.zeros_like(dsr_ref)
            dsc_ref[...] = jnp.zeros_like(dsc_ref)
            da_ref[...] = jnp.zeros_like(da_ref)
            ddtb_ref[...] = jnp.zeros_like(ddtb_ref)
            dd_ref[...] = jnp.zeros_like(dd_ref)

        ri = lax.broadcasted_iota(jnp.int32, (L, L), 0)
        ci = lax.broadcasted_iota(jnp.int32, (L, L), 1)
        trilb = ri >= ci
        lane = lax.broadcasted_iota(jnp.int32, (1, LANES), 1)
        sub = lax.broadcasted_iota(jnp.int32, (LANES, 1), 0)
        head0 = lane < 64
        A = a_ref[...]
        u, dt, acs = _ssd_dt(dtr_ref[...], dtb_ref[...], A, trilb.astype(F32))
        acsT = acs.T
        last = acs[L - 1:L, :]
        elast = jnp.exp(last)
        er = jnp.exp(last - acs)
        wt = er * dt
        expand = lambda t: _select_dot(t, e_ref[...])
        dte, eae, wte = expand(dt), expand(jnp.exp(acs)), expand(wt)
        dlast = jnp.zeros((1, LANES), F32)
        dcbs = []
        for g in range(SSM_GROUPS):
            Bg = b_ref[:, g * N:(g + 1) * N].astype(BF16)
            Cg = c_ref[:, g * N:(g + 1) * N].astype(BF16)
            stg = st_in_ref[0, g * GW:(g + 1) * GW, :]
            dstg = dst_ref[g * GW:(g + 1) * GW, :]
            CB = lax.dot_general(Cg, Bg, _NT, preferred_element_type=F32)
            CBT = lax.dot_general(Bg, Cg, _NT, preferred_element_type=F32)
            Z = lax.dot_general(Cg, stg.astype(BF16), _NT, preferred_element_type=F32)
            U = lax.dot_general(Bg, dstg.astype(BF16), _NT, preferred_element_type=F32)
            dcb = jnp.zeros((L, L), F32)
            for q in range(PAIRS_PER_GROUP):
                hp = g * PAIRS_PER_GROUP + q
                sl = slice(hp * LANES, (hp + 1) * LANES)
                qs = slice(q * LANES, (q + 1) * LANES)
                xs = xs_ref[:, sl]
                dyp = dy_ref[:, sl]
                dtp, eap, wp, Dp = dte[:, sl], eae[:, sl], wte[:, sl], d_ref[:, sl]
                xd = xs * dtp
                dy2 = jnp.concatenate([jnp.where(head0, dyp, 0.0), jnp.where(head0, 0.0, dyp)], axis=0).astype(BF16)
                G2 = lax.dot_general(dy2, xd.astype(BF16), _NT, preferred_element_type=F32)
                mts = []
                for r in range(2):
                    h = 2 * hp + r
                    seg = acs[:, h:h + 1] - acsT[h:h + 1, :]
                    dec = jnp.exp(jnp.where(trilb, seg, -jnp.inf))
                    decT = jnp.exp(jnp.where(ri <= ci, -seg, -jnp.inf))
                    gd = G2[r * L:(r + 1) * L, :] * dec
                    dcb = dcb + gd
                    dseg = gd * CB
                    dsr_ref[:, h:h + 1] = jnp.sum(dseg, axis=1, keepdims=True)
                    dsc_ref[h:h + 1, :] = jnp.sum(dseg, axis=0, keepdims=True)
                    mts.append((CBT * decT).astype(BF16))
                dxd = jnp.dot(jnp.concatenate(mts, axis=1), dy2, preferred_element_type=F32)
                Up = U[:, qs]
                r_ref[0:L, sl] = dyp * Z[:, qs] * eap
                r_ref[L:2 * L, sl] = dxd * xs
                r_ref[2 * L:3 * L, sl] = Up * xs
                dz_ref[:, sl] = (dyp * eap).astype(BF16)
                xw_ref[:, sl] = (xs * wp).astype(BF16)
                dxs_ref[:, sl] = dxd * dtp + Dp * dyp + Up * wp
                dd_ref[:, sl] += jnp.sum(dyp * xs, axis=0, keepdims=True)
            dcbs.append(dcb)
            t = dstg * stg
            for k in range(HEADS_PER_GROUP):
                dlast = dlast + jnp.where(lane == g * HEADS_PER_GROUP + k,
                                          jnp.sum(t[k * 64:(k + 1) * 64, :], keepdims=True), 0.0)
        fold = lambda k: _select_dot(r_ref[k * L:(k + 1) * L, :], et_ref[...])
        r1, r2, dws = fold(0), fold(1), fold(2)
        dww = dws * wt
        ddt = r2 + dws * er
        dacs = r1 - dww + dsr_ref[...] - dsc_ref[...].T
        dlast = dlast * elast + jnp.sum(dww, axis=0, keepdims=True)
        lasts = [last[:, h:h + 1] for h in range(SSM_HEADS)]
        for g in range(SSM_GROUPS):
            Bg = b_ref[:, g * N:(g + 1) * N].astype(BF16)
            Cg = c_ref[:, g * N:(g + 1) * N].astype(BF16)
            gs = slice(g * GW, (g + 1) * GW)
            stb = st_in_ref[0, gs, :].astype(BF16)
            dstb = dst_ref[gs, :].astype(BF16)
            dcbb = dcbs[g].astype(BF16)
            dzg = dz_ref[:, gs]
            dc_ref[:, g * N:(g + 1) * N] = (jnp.dot(dzg, stb, preferred_element_type=F32)
                                            + jnp.dot(dcbb, Bg, preferred_element_type=F32))
            db_ref[:, g * N:(g + 1) * N] = (jnp.dot(xw_ref[:, gs], dstb, preferred_element_type=F32)
                                            + lax.dot_general(dcbb, Cg, _TN, preferred_element_type=F32))
            dsn = lax.dot_general(dzg, Cg, _TN, preferred_element_type=F32)
            for k in range(HEADS_PER_GROUP):
                h = g * HEADS_PER_GROUP + k
                rows = slice(h * 64, (h + 1) * 64)
                dst_ref[rows, :] = dst_ref[rows, :] * jnp.exp(lasts[h]) + dsn[k * 64:(k + 1) * 64, :]
        rowi = lax.broadcasted_iota(jnp.int32, (L, 1), 0)
        dacs = dacs + jnp.where(rowi == L - 1, dlast, 0.0)
        da = jnp.dot((ci >= ri).astype(F32), dacs, precision=HI, preferred_element_type=F32)
        ddt = ddt + da * A
        da_ref[...] += jnp.sum(da * dt, axis=0, keepdims=True)
        ddtr = jnp.where(lane < SSM_HEADS, ddt * _sigmoid(u), 0.0)
        ddtr_ref[...] = ddtr
        ddtb_ref[...] += jnp.sum(ddtr, axis=0, keepdims=True)

    rev = lambda c: nc - 1 - c
    gn = SSM_GROUPS * N
    res = pl.pallas_call(
        body, name=name, grid=(nc,),
        in_specs=[pl.BlockSpec((L, D_INNER), lambda c: (rev(c), 0)),
                  pl.BlockSpec((L, gn), lambda c: (rev(c), D_INNER // gn)),
                  pl.BlockSpec((L, gn), lambda c: (rev(c), D_INNER // gn + 1)),
                  pl.BlockSpec((L, LANES), lambda c: (rev(c), 0)),
                  pl.BlockSpec((1, LANES), lambda c: (0, 0)),
                  pl.BlockSpec((1, LANES), lambda c: (0, 0)),
                  pl.BlockSpec((1, D_INNER), lambda c: (0, 0)),
                  pl.BlockSpec((LANES, D_INNER), lambda c: (0, 0)),
                  pl.BlockSpec((D_INNER, LANES), lambda c: (0, 0)),
                  pl.BlockSpec((1, D_INNER, N), lambda c: (rev(c), 0, 0)),
                  pl.BlockSpec((L, D_INNER), lambda c: (rev(c), 0))] + r_in,
        out_specs=[pl.BlockSpec((L, D_INNER), lambda c: (rev(c), 0)),
                   pl.BlockSpec((L, gn), lambda c: (rev(c), 0)),
                   pl.BlockSpec((L, gn), lambda c: (rev(c), 0)),
                   pl.BlockSpec((L, LANES), lambda c: (rev(c), 0)),
                   pl.BlockSpec((1, LANES), lambda c: (0, 0)),
                   pl.BlockSpec((1, LANES), lambda c: (0, 0)),
                   pl.BlockSpec((1, D_INNER), lambda c: (0, 0))] + r_out,
        out_shape=[jax.ShapeDtypeStruct((S, D_INNER), F32), jax.ShapeDtypeStruct((S, gn), F32),
                   jax.ShapeDtypeStruct((S, gn), F32), jax.ShapeDtypeStruct((S, LANES), F32),
                   jax.ShapeDtypeStruct((1, LANES), F32), jax.ShapeDtypeStruct((1, LANES), F32),
                   jax.ShapeDtypeStruct((1, D_INNER), F32)] + r_shapes,
        scratch_shapes=[pltpu.VMEM((D_INNER, N), F32), pltpu.VMEM((L, D_INNER), BF16), pltpu.VMEM((L, D_INNER), BF16),
                        pltpu.VMEM((3 * L, D_INNER), F32), pltpu.VMEM((L, LANES), F32), pltpu.VMEM((LANES, L), F32)]
                       + r_sems,
        compiler_params=pltpu.CompilerParams(dimension_semantics=("arbitrary",), vmem_limit_bytes=VMEM_LIMIT,
                                             has_side_effects=rider is not None),
    )(xbc, xbc, xbc, dtraw, dtb, A, dexp, e_mat, et_mat, states, dy, *r_args)
    return res[:7], res[7:]


BAND_A = (A_PREV + 1) * CHUNK
REL_W = 640


def _relpos_select():
    k = np.arange(REL_W)
    rel = np.where(k < BAND_A, A_PREV * CHUNK - k, A_PREV * CHUNK - (k - REL_W))
    idx = np.clip(rel, -MAX_REL, MAX_REL) + MAX_REL
    sel = (np.arange(REL_W)[:, None] == idx[None, :]) & (k != BAND_A)[None, :]
    return sel.astype(np.float32)


def relpos_bias(table_pad, *, name):
    def body(t_ref, s_ref, o_ref):
        v = jnp.dot(t_ref[...], s_ref[...], precision=HI, preferred_element_type=F32)
        for h in range(N_HEADS):
            o_ref[h] = pltpu.roll(jnp.broadcast_to(v[h:h + 1, :], (CHUNK, REL_W)), 0, 1, stride=1, stride_axis=0)

    return pl.pallas_call(body, name=name, out_shape=jax.ShapeDtypeStruct((N_HEADS, CHUNK, REL_W), F32),
                          compiler_params=pltpu.CompilerParams(vmem_limit_bytes=VMEM_LIMIT),
                          )(table_pad, jnp.asarray(_relpos_select()))


def relpos_grad(dbias_rev, *, name):
    def body(d_ref, s_ref, o_ref):
        head = lax.broadcasted_iota(jnp.int32, (N_HEADS, 1), 0)
        dv = jnp.zeros((N_HEADS, REL_W), F32)
        for h in range(N_HEADS):
            back = pltpu.roll(d_ref[h], REL_W - (CHUNK - 1), 1, stride=1, stride_axis=0)
            dv = dv + jnp.where(head == h, jnp.sum(back, axis=0, keepdims=True), 0.0)
        o_ref[...] = lax.dot_general(dv, s_ref[...], _NT, precision=HI, preferred_element_type=F32)

    return pl.pallas_call(body, name=name, out_shape=jax.ShapeDtypeStruct((N_HEADS, REL_W), F32),
                          compiler_params=pltpu.CompilerParams(vmem_limit_bytes=VMEM_LIMIT),
                          )(dbias_rev, jnp.asarray(_relpos_select()))


def loss_head(y, t, *, name, tm=256):
    S, D = y.shape
    tm = min(tm, S)

    def body(y_ref, t_ref, dy_ref, l_ref):
        e = y_ref[...] - t_ref[...]
        dy_ref[...] = e * (1.0 / D)

        @pl.when(pl.program_id(0) == 0)
        def _():
            l_ref[...] = jnp.zeros_like(l_ref)

        part = jnp.sum(jnp.sum(e * e, axis=1, keepdims=True), axis=0, keepdims=True) * (0.5 / D)
        l_ref[...] += jnp.broadcast_to(part, l_ref.shape)

    return pl.pallas_call(
        body, name=name, grid=(S // tm,),
        in_specs=[pl.BlockSpec((tm, D), lambda i: (i, 0))] * 2,
        out_specs=[pl.BlockSpec((tm, D), lambda i: (i, 0)), pl.BlockSpec((1, LANES), lambda i: (0, 0))],
        out_shape=[jax.ShapeDtypeStruct((S, D), F32), jax.ShapeDtypeStruct((1, LANES), F32)],
        compiler_params=_params("arbitrary"),
    )(y, t)


def f_adamw(w, g, m, v):
    m = ADAM_B1 * m + (1.0 - ADAM_B1) * g
    v = ADAM_B2 * v + (1.0 - ADAM_B2) * (g * g)
    m_hat = m / (1.0 - ADAM_B1 ** ADAM_STEP)
    v_hat = v / (1.0 - ADAM_B2 ** ADAM_STEP)
    delta = -ADAM_LR * (m_hat / (jnp.sqrt(v_hat) + ADAM_EPS) + ADAM_WD * w)
    return delta, m, v


ANY = pl.BlockSpec(memory_space=pl.ANY)


def _pos():
    return lax.axis_index("x"), lax.axis_index("y"), lax.axis_index("c")


def _other_chips(x, y):
    return [(1 - x, y), (x, 1 - y), (1 - x, 1 - y)]


class Rider:
    def __init__(self, ins, outs, sems, start, mid, finish):
        self.ins, self.outs, self.sems = list(ins), list(outs), list(sems)
        self.start, self.mid, self.finish = start, mid, finish


def _rider_parts(rider):
    if rider is None:
        return [], [], [], [], []
    return [ANY] * len(rider.ins), [ANY] * len(rider.outs), rider.outs, rider.sems, rider.ins


def _ride(rider, ins, outs, sems, first, mid, last):
    pos = _pos()

    @pl.when(first)
    def _():
        rider.start(pos, ins, outs, sems)

    if rider.mid is not None:
        @pl.when(mid)
        def _():
            rider.mid(pos, ins, outs, sems)

    @pl.when(last)
    def _():
        rider.finish(pos, ins, outs, sems)


def run_rider(rider, *, name):
    n_in, n_out = len(rider.ins), len(rider.outs)

    def body(*refs):
        ins, outs, sems = refs[:n_in], refs[n_in:n_in + n_out], refs[n_in + n_out:]
        pos = _pos()
        rider.start(pos, ins, outs, sems)
        if rider.mid is not None:
            rider.mid(pos, ins, outs, sems)
        rider.finish(pos, ins, outs, sems)

    return pl.pallas_call(
        body, name=name, in_specs=[ANY] * n_in, out_specs=[ANY] * n_out, out_shape=rider.outs,
        scratch_shapes=rider.sems, compiler_params=pltpu.CompilerParams(has_side_effects=True),
    )(*rider.ins)


def gather_rider(shards):
    n = len(shards)

    def copies(pos, ins, outs, sems):
        x, y, c = pos
        send, recv, fsend, frecv = sems
        me = 2 * x + y
        sib = (x, y, 1 - c)
        first, arrive, passed, theirs = [], [], [], []
        for i in range(n):
            for j, (px, py) in enumerate(_other_chips(x, y)):
                k = 3 * i + j
                far = dict(device_id=(px, py, c), device_id_type=MESH)
                near = dict(device_id=sib, device_id_type=MESH)
                got = outs[i].at[2 * px + py, c]
                his = outs[i].at[2 * px + py, 1 - c]
                first.append(pltpu.make_async_remote_copy(ins[i].at[c], outs[i].at[me, c], send.at[k], recv.at[k], **far))
                arrive.append(pltpu.make_async_remote_copy(ins[i].at[c], got, send.at[k], recv.at[k], **far))
                passed.append(pltpu.make_async_remote_copy(got, got, fsend.at[k], frecv.at[k], **near))
                theirs.append(pltpu.make_async_remote_copy(his, his, fsend.at[k], frecv.at[k], **near))
        return first, arrive, passed, theirs

    def start(*a):
        for cp in copies(*a)[0]:
            cp.start()

    def mid(*a):
        _, arrive, passed, _ = copies(*a)
        for got, cp in zip(arrive, passed):
            got.wait_recv()
            cp.start()

    def finish(*a):
        first, _, passed, theirs = copies(*a)
        for cp in theirs:
            cp.wait_recv()
        for cp in first + passed:
            cp.wait_send()

    return Rider(shards, [jax.ShapeDtypeStruct((4,) + s.shape, s.dtype) for s in shards],
                 [pltpu.SemaphoreType.DMA((3 * n,))] * 4, start, mid, finish)


def scatter_rider(ps):
    n = len(ps)

    def copies(pos, ins, outs, sems):
        x, y, c = pos
        send, recv = sems
        return [pltpu.make_async_remote_copy(ins[i].at[2 * px + py], outs[i].at[j], send.at[3 * i + j], recv.at[3 * i + j],
                                             device_id=(px, py, c), device_id_type=MESH)
                for i in range(n) for j, (px, py) in enumerate(_other_chips(x, y))]

    def start(*a):
        for cp in copies(*a):
            cp.start()

    def finish(*a):
        for cp in copies(*a):
            cp.wait()

    return Rider(ps, [jax.ShapeDtypeStruct((3,) + p.shape[1:], p.dtype) for p in ps],
                 [pltpu.SemaphoreType.DMA((3 * n,))] * 2, start, None, finish)


def pair_swap_halves(gs, *, name):
    n = len(gs)

    def body(*refs):
        ins, outs = refs[:n], refs[n:2 * n]
        send, recv = refs[2 * n:]
        x, y, c = _pos()
        cps = []
        for i in range(n):
            cp = pltpu.make_async_remote_copy(ins[i].at[1 - c], outs[i], send.at[i], recv.at[i],
                                              device_id=(x, y, 1 - c), device_id_type=MESH)
            cp.start()
            cps.append(cp)
        for cp in cps:
            cp.wait()

    return pl.pallas_call(
        body, name=name, in_specs=[ANY] * n, out_specs=[ANY] * n,
        out_shape=[jax.ShapeDtypeStruct(g.shape[1:], g.dtype) for g in gs],
        scratch_shapes=[pltpu.SemaphoreType.DMA((n,)), pltpu.SemaphoreType.DMA((n,))],
        compiler_params=pltpu.CompilerParams(has_side_effects=True),
    )(*gs)


def pair_share(hs, *, name):
    n = len(hs)

    def body(*refs):
        ins, outs = refs[:n], refs[n:2 * n]
        send, recv = refs[2 * n:]
        x, y, c = _pos()
        cps = []
        for i in range(n):
            cp = pltpu.make_async_remote_copy(ins[i], outs[i], send.at[i], recv.at[i],
                                              device_id=(x, y, 1 - c), device_id_type=MESH)
            cp.start()
            cps.append(cp)
        for cp in cps:
            cp.wait()

    return pl.pallas_call(
        body, name=name, in_specs=[ANY] * n, out_specs=[ANY] * n,
        out_shape=[jax.ShapeDtypeStruct(h.shape, h.dtype) for h in hs],
        scratch_shapes=[pltpu.SemaphoreType.DMA((n,)), pltpu.SemaphoreType.DMA((n,))],
        compiler_params=pltpu.CompilerParams(has_side_effects=True),
    )(*hs)


def gather_all(buf, *, name):
    def body(in_ref, out_ref, send, recv, loc):
        x, y, c = _pos()
        lid = 4 * x + 2 * y + c
        lc = pltpu.make_async_copy(in_ref, out_ref.at[lid], loc.at[0])
        lc.start()
        cps = []
        for k in range(1, 8):
            px = 1 - x if k & 4 else x
            py = 1 - y if k & 2 else y
            pc = 1 - c if k & 1 else c
            cp = pltpu.make_async_remote_copy(in_ref, out_ref.at[lid], send.at[k - 1], recv.at[k - 1],
                                              device_id=(px, py, pc), device_id_type=MESH)
            cp.start()
            cps.append((cp, 4 * px + 2 * py + pc, (px, py, pc)))
        for k, (cp, plid, peer) in enumerate(cps):
            cp.wait_send()
            pltpu.make_async_remote_copy(in_ref, out_ref.at[plid], send.at[k], recv.at[k],
                                         device_id=peer, device_id_type=MESH).wait_recv()
        lc.wait()

    return pl.pallas_call(
        body, name=name, in_specs=[ANY], out_specs=ANY,
        out_shape=jax.ShapeDtypeStruct((8,) + buf.shape, buf.dtype),
        scratch_shapes=[pltpu.SemaphoreType.DMA((7,)), pltpu.SemaphoreType.DMA((7,)), pltpu.SemaphoreType.DMA((1,))],
        compiler_params=pltpu.CompilerParams(has_side_effects=True),
    )(buf)


def sum_slots(a, *, name):
    n = a.shape[0]

    def body(a_ref, o_ref):
        acc = a_ref[0]
        for k in range(1, n):
            acc = acc + a_ref[k]
        o_ref[...] = acc

    return pl.pallas_call(body, name=name, out_shape=jax.ShapeDtypeStruct(a.shape[1:], a.dtype),
                          compiler_params=pltpu.CompilerParams(vmem_limit_bytes=VMEM_LIMIT))(a)


def _row_tile(r, want, mult=16):
    t = (min(want, r) // mult) * mult
    while t >= mult:
        if r % t == 0:
            return t
        t -= mult
    return r


def pair_add(g, r1, csel, *, name):
    _, _, r, C = g.shape
    tr = _row_tile(r, 256)

    def body(g_ref, r_ref, c_ref, p32_ref, pb_ref):
        south = c_ref[0:1, 0:1] == 0.0
        p = jnp.where(south, g_ref[0, 0], g_ref[1, 0]) + r_ref[0]
        p32_ref[0] = p
        pb_ref[0] = p.astype(BF16)

    return pl.pallas_call(
        body, name=name, grid=(4, r // tr),
        in_specs=[pl.BlockSpec((2, 1, tr, C), lambda j, t: (0, j, t, 0)), pl.BlockSpec((1, tr, C), lambda j, t: (j, t, 0)),
                  pl.BlockSpec((1, LANES), lambda j, t: (0, 0))],
        out_specs=[pl.BlockSpec((1, tr, C), lambda j, t: (j, t, 0))] * 2,
        out_shape=[jax.ShapeDtypeStruct((4, r, C), F32), jax.ShapeDtypeStruct((4, r, C), BF16)],
        compiler_params=_params("parallel", "parallel"),
    )(g, r1, csel)


def chip_add(p32, r3, msel, *, name):
    _, r, C = p32.shape
    tr = _row_tile(r, 128)

    def body(p_ref, r_ref, m_ref, o_ref):
        me = m_ref[0:1, 0:1]
        acc = jnp.where(me == 0.0, p_ref[0], jnp.where(me == 1.0, p_ref[1], jnp.where(me == 2.0, p_ref[2], p_ref[3])))
        for j in range(3):
            acc = acc + r_ref[j].astype(F32)
        o_ref[...] = acc

    return pl.pallas_call(
        body, name=name, grid=(r // tr,),
        in_specs=[pl.BlockSpec((4, tr, C), lambda t: (0, t, 0)), pl.BlockSpec((3, tr, C), lambda t: (0, t, 0)),
                  pl.BlockSpec((1, LANES), lambda t: (0, 0))],
        out_specs=pl.BlockSpec((tr, C), lambda t: (t, 0)),
        out_shape=jax.ShapeDtypeStruct((r, C), F32),
        compiler_params=_params("parallel"),
    )(p32, r3, msel)


def _consts():
    i512 = np.arange(N_HEADS * HEAD_DIM)
    i128 = np.arange(LANES)
    bd512 = (i512[:, None] // HEAD_DIM == i512[None, :] // HEAD_DIM).astype(np.float32)
    bd128 = (i128[:, None] // HEAD_DIM == i128[None, :] // HEAD_DIM).astype(np.float32)
    fold = (np.arange(HEAD_DIM)[:, None] == (i512[None, :] % HEAD_DIM)).astype(np.float32)
    grp = N_HEADS // 2 * HEAD_DIM
    expand = ((i128[:, None] // HEAD_DIM == i512[None, :] // grp)
              & (i128[:, None] % HEAD_DIM == i512[None, :] % HEAD_DIM)).astype(np.float32)
    band = (B_PREV + 1) * CHUNK
    rel = np.arange(CHUNK)[:, None] - (np.arange(band)[None, :] - B_PREV * CHUNK)
    slopes = 2.0 ** (-8.0 * np.arange(1, N_HEADS + 1, dtype=np.float32) / N_HEADS)
    bias_b = (-slopes[:, None, None] * np.abs(rel).astype(np.float32)[None]).astype(np.float32)
    return [jnp.asarray(a) for a in (bd512, bd128, fold, expand)], jnp.asarray(bias_b)


def _ffn_fwd(xin, h, l, W, P, next_gain):
    Wi = W["ffn_in"][l]
    gate = matmul(h, Wi[:, :D_FF], mode="nn", name=f"ffn{l}_gate", out_dtype=BF16)
    up = matmul(h, Wi[:, D_FF:], mode="nn", name=f"ffn{l}_up", out_dtype=BF16)
    gc, act = dwconv_fwd(gate, P["ffn_conv_w"][l], P["ffn_conv_b"][l:l + 1], lambda y, u: (y, _silu(y) * u), [up],
                         [BF16, BF16], name=f"ffn{l}_conv")
    saved = (xin, h, gate, gc, up, act)
    if next_gain is None:
        return matmul(act, W["ffn_out"][l], mode="nn", name=f"ffn{l}_out", residual=xin), saved, None
    xout, h_next = matmul_norm(act, W["ffn_out"][l], xin, next_gain, name=f"ffn{l}_out")
    return xout, saved, h_next


def _ffn_bwd(dxout, l, saved, W, P):
    xin, h, gate, gc, up, act = saved
    g = P["norm_ffn"][l:l + 1]
    Wi = W["ffn_in"][l]
    dact = matmul(dxout, W["ffn_out"][l], mode="nt", name=f"ffn{l}_dact", out_dtype=BF16)
    dWo = matmul(act, dxout, mode="tn", name=f"ffn{l}_dwout")
    dgate, dcw, dcb, dup = dwconv_bwd(gate, P["ffn_conv_w"][l], [gc, up, dact],
                                      lambda c, u, da: (da * u * _dsilu(c), da * _silu(c)), [BF16],
                                      name=f"ffn{l}_dconv")
    dh = matmul(dgate, Wi[:, :D_FF], mode="nt", name=f"ffn{l}_dh_gate")
    dxin, dg = matmul_dnorm(dup, Wi[:, D_FF:], dh, xin, g, dxout, name=f"ffn{l}_dh_up")
    dWi = jnp.concatenate([matmul(h, dgate, mode="tn", name=f"ffn{l}_dw_gate"),
                           matmul(h, dup, mode="tn", name=f"ffn{l}_dw_up")], axis=1)
    return dxin, dWi, dWo, dg, dcw, dcb


class NoComm:
    def fwd_rider(self, tag):
        return None

    def fwd_done(self, tag, outs, W, P):
        pass

    def grads(self, tag, cols, rows):
        return None

    def bwd_done(self, tag, outs):
        pass


def local_step(x, tgt, W, P, comm):
    qk_consts, bias_b = _consts()
    DQ = N_HEADS * HEAD_DIM
    tile = _row_tile(x.shape[0], 256)
    kv_lead = [-(-n * CHUNK // tile) * tile for n in (A_PREV, B_PREV)]

    g_mix0 = P["norm_mix"][0:1]
    (h0,) = rowwise(f_rmsnorm, [x], [g_mix0], [(D_MODEL, BF16)], name="attn_norm")
    qkv = matmul(h0, W["attn_in"], mode="nn", name="attn_qkv")
    qk_par = [P["q_norm_a"], P["k_norm_a"], P["q_norm_b"], P["k_norm_b"]] + qk_consts
    qk_lead = [0, kv_lead[0], kv_lead[0], 0, kv_lead[1], kv_lead[1]]
    qa, ka, va, qb, kb, vb = rowwise(f_qknorm, [qkv], qk_par, [(DQ, BF16)] * 6, name="attn_qknorm", lead=qk_lead)
    table = jnp.pad(P["relpos_table"], ((0, 0), (0, REL_W - (2 * MAX_REL + 1))))
    nj = min(ATT_TQ, x.shape[0]) // CHUNK
    bias_a = widen_bias(relpos_bias(table, name="relpos_bias")[:, :, :BAND_A], A_PREV, nj)
    bias_b = widen_bias(bias_b, B_PREV, nj)
    sinks = jnp.broadcast_to(P["sinks"].reshape(N_HEADS // 2, 2, 1), (N_HEADS // 2, 2, LANES))
    oa, late = attn_fwd(qa, ka, va, bias_a, None, n_prev=A_PREV, name="attn_a", rider=comm.fwd_rider("a"))
    comm.fwd_done("a", late, W, P)
    ob, late = attn_fwd(qb, kb, vb, bias_b, sinks, n_prev=B_PREV, name="attn_b", rider=comm.fwd_rider("b"))
    comm.fwd_done("b", late, W, P)
    Wao = W["attn_out"]
    x1 = matmul(oa, Wao[:DQ], mode="nn", name="attn_out_a", residual=x)
    x1, hf0 = matmul_norm(ob, Wao[DQ:], x1, P["norm_ffn"][0:1], name="attn_out_b")
    g_mix1 = P["norm_mix"][1:2]
    x2, ffn0, h2 = _ffn_fwd(x1, hf0, 0, W, P, g_mix1)

    Ws = W["ssm_in"]
    CC = D_INNER + 2 * SSM_GROUPS * SSM_STATE
    Wz, Wx = Ws[:, :D_INNER], Ws[:, D_INNER:D_INNER + CC]
    Wdt = jnp.pad(Ws[:, D_INNER + CC:], ((0, 0), (0, LANES - SSM_HEADS)))
    z = matmul(h2, Wz, mode="nn", name="ssm_z", out_dtype=BF16)
    xr = matmul(h2, Wx, mode="nn", name="ssm_xbc", out_dtype=BF16)
    dtraw = matmul(h2, Wdt, mode="nn", name="ssm_dt")
    xc, xbc = dwconv_fwd(xr, P["ssm_conv_w"], P["ssm_conv_b"], lambda y: (y, _silu(y)), [], [BF16, F32],
                         name="ssm_conv")
    pad32 = lambda v: jnp.pad(v, ((0, 0), (0, LANES - SSM_HEADS)))
    A = pad32(-jnp.exp(P["ssm_a_log"]))
    dtb = pad32(P["ssm_dt_bias"])
    dexp = jnp.repeat(P["ssm_d"], D_INNER // SSM_HEADS, axis=1)
    y, states = ssd_fwd(xbc, dtraw, dtb, A, dexp, name="ssd_fwd")
    (y2,) = rowwise(f_gate_norm, [y, z], [P["ssm_norm"]], [(D_INNER, BF16)], name="ssm_gate_norm")
    x3, hf1 = matmul_norm(y2, W["ssm_out"], x2, P["norm_ffn"][1:2], name="ssm_out")
    x4, ffn1, _ = _ffn_fwd(x3, hf1, 1, W, P, None)

    dx4, lpart = loss_head(x4, tgt, name="loss_head")

    dx3, dWfi1, dWfo1, dgf1, dfcw1, dfcb1 = _ffn_bwd(dx4, 1, ffn1, W, P)
    out_f1 = comm.grads("f1", dWfi1, dWfo1)
    dy2 = matmul(dx3, W["ssm_out"], mode="nt", name="ssm_dy")
    dWso = matmul(y2, dx3, mode="tn", name="ssm_dwout")
    dy, dz, dnw = rowwise_vjp(f_gate_norm, [y, z], [P["ssm_norm"]], [dy2], [(0, F32), (1, BF16)], [0],
                              name="ssm_dgate_norm")
    (dxs, dB, dC, ddtraw, dA, ddtb, dDl), sent = ssd_bwd(xbc, dtraw, dtb, A, dexp, states, dy, name="ssd_bwd",
                                                          rider=out_f1)
    comm.bwd_done("f1", sent)
    dxr, dscw, dscb = dwconv_bwd(xr, P["ssm_conv_w"], [xc, (dxs, dB, dC)], lambda c, g: (g * _dsilu(c),), [],
                                 name="ssm_dconv")
    dh2 = matmul(dz, Wz, mode="nt", name="ssm_dh_z")
    dh2 = matmul(dxr, Wx, mode="nt", name="ssm_dh_x", residual=dh2)
    dx2, dgm1 = matmul_dnorm(ddtraw, Wdt, dh2, x2, g_mix1, dx3, name="ssm_dh_dt")
    dWs = jnp.concatenate([matmul(h2, dz, mode="tn", name="ssm_dw_z"),
                           matmul(h2, dxr, mode="tn", name="ssm_dw_x"),
                           matmul(h2, ddtraw, mode="tn", name="ssm_dw_dt")[:, :SSM_HEADS]], axis=1)
    out_s = comm.grads("s", dWs, dWso)

    dx1, dWfi0, dWfo0, dgf0, dfcw0, dfcb0 = _ffn_bwd(dx2, 0, ffn0, W, P)
    out_f0 = comm.grads("f0", dWfi0, dWfo0)
    doa = matmul(dx1, Wao[:DQ], mode="nt", name="attn_do_a", out_dtype=BF16)
    dob = matmul(dx1, Wao[DQ:], mode="nt", name="attn_do_b", out_dtype=BF16)
    dWao = jnp.concatenate([matmul(oa, dx1, mode="tn", name="attn_dwout_a"),
                            matmul(ob, dx1, mode="tn", name="attn_dwout_b")], axis=0)
    (dqa, dka, dva, dbias_a), sent = attn_bwd(qa, ka, va, doa, bias_a, None, n_prev=A_PREV, name="attn_a_bwd",
                                              rider=out_s)
    comm.bwd_done("s", sent)
    (dqb, dkb, dvb, _, dsk), sent = attn_bwd(qb, kb, vb, dob, bias_b, sinks, n_prev=B_PREV, name="attn_b_bwd",
                                             rider=out_f0)
    comm.bwd_done("f0", sent)
    dqkv, dgqa, dgka, dgqb, dgkb = rowwise_vjp(f_qknorm, [qkv], qk_par, [dqa, dka, dva, dqb, dkb, dvb],
                                               [(0, BF16)], [0, 1, 2, 3], name="attn_dqknorm", cot_skip=qk_lead)
    dx, dgm0 = matmul_dnorm(dqkv, W["attn_in"], None, x, g_mix0, dx1, name="attn_dh")
    dWai = matmul(h0, dqkv, mode="tn", name="attn_dwin")
    dbias_a = fold_bias(dbias_a, A_PREV, nj)
    dbias_rev = jnp.pad(dbias_a[:, ::-1, :], ((0, 0), (0, 0), (0, REL_W - BAND_A)))
    dtable = relpos_grad(dbias_rev, name="relpos_grad")[:, :2 * MAX_REL + 1]

    gW = {"attn_in": dWai, "attn_out": dWao, "ssm_in": dWs, "ssm_out": dWso,
          "ffn_in": [dWfi0, dWfi1], "ffn_out": [dWfo0, dWfo1]}
    gP = {"norm_mix": jnp.concatenate([dgm0, dgm1], axis=0),
          "norm_ffn": jnp.concatenate([dgf0, dgf1], axis=0),
          "relpos_table": dtable, "q_norm_a": dgqa, "k_norm_a": dgka, "q_norm_b": dgqb, "k_norm_b": dgkb,
          "sinks": dsk[:, :, 0].reshape(1, N_HEADS),
          "ssm_conv_w": dscw, "ssm_conv_b": dscb,
          "ssm_dt_bias": ddtb[:, :SSM_HEADS], "ssm_a_log": dA[:, :SSM_HEADS] * A[:, :SSM_HEADS],
          "ssm_d": dDl.reshape(SSM_HEADS, D_INNER // SSM_HEADS).sum(axis=1).reshape(1, SSM_HEADS),
          "ssm_norm": dnw,
          "ffn_conv_w": jnp.stack([dfcw0, dfcw1]), "ffn_conv_b": jnp.concatenate([dfcb0, dfcb1], axis=0)}
    return lpart, dx, gW, gP


WEIGHTS = ["norm_mix", "norm_ffn", "attn_w_in", "attn_w_out", "relpos_table", "q_norm_a", "k_norm_a", "q_norm_b",
           "k_norm_b", "sinks", "ssm_w_in", "ssm_conv_w", "ssm_conv_b", "ssm_dt_bias", "ssm_a_log", "ssm_d",
           "ssm_norm", "ssm_w_out", "ffn_w_in", "ffn_conv_w", "ffn_conv_b", "ffn_w_out"]
ARGS = ["x"] + WEIGHTS + ["loss_target"] + ["m_" + w for w in WEIGHTS] + ["v_" + w for w in WEIGHTS]
N_CHIPS = 4
SMALL_ROWS = 384
SMALL_ORDER = ["norm_mix", "norm_ffn", "relpos_table", "q_norm_a", "k_norm_a", "q_norm_b", "k_norm_b", "sinks",
               "ssm_dt_bias", "ssm_a_log", "ssm_d", "ffn_conv_b", "ssm_conv_w", "ssm_conv_b", "ssm_norm", "ffn_conv_w"]


def _cols_to_slabs(g):
    K, N = g.shape
    return g.reshape(2, K // 2, N_CHIPS, N // N_CHIPS).transpose(0, 2, 1, 3)


def _rows_to_slabs(g):
    R, C = g.shape
    return g.reshape(N_CHIPS, 2, R // (2 * N_CHIPS), C).transpose(1, 0, 2, 3)


class MeshComm:
    def __init__(self, d, xi, yi, ci):
        self.d, self.ci, self.me = d, ci, 2 * xi + yi
        self.csel = jnp.full((1, LANES), ci, F32)
        self.msel = jnp.full((1, LANES), self.me, F32)
        halves = lambda w: w.reshape((2, -1, w.shape[-1]))
        small = jnp.concatenate([d[k].reshape(-1) for k in ("ssm_conv_w", "ssm_conv_b", "ssm_norm", "ffn_conv_w")])
        small = jnp.pad(small, (0, 2 * 40 * LANES - small.shape[0])).reshape(2, 40, LANES)
        self.shards = {"attn": [halves(d["attn_w_in"][0].astype(BF16)), halves(d["attn_w_out"][0].astype(BF16))],
                       "a": [d["ffn_w_in"].astype(BF16), small],
                       "b": [d["ffn_w_out"].astype(BF16), halves(d["ssm_w_in"][0].astype(BF16)),
                             halves(d["ssm_w_out"][0].astype(BF16))]}
        self.p32, self.mine = {}, {}

    def _whole(self, tag, outs):
        return [lax.dynamic_update_slice_in_dim(g, s[None], self.me, axis=0) for g, s in zip(outs, self.shards[tag])]

    @staticmethod
    def _cat_cols(g):
        return jnp.concatenate([g[j].reshape((-1, g.shape[-1])) for j in range(N_CHIPS)], axis=1)

    def first_weights(self):
        g_ai, g_ao = self._whole("attn", run_rider(gather_rider(self.shards["attn"]), name="gather_attn"))
        return {"attn_in": self._cat_cols(g_ai), "attn_out": g_ao.reshape(-1, D_MODEL)}

    def fwd_rider(self, tag):
        return gather_rider(self.shards[tag])

    def fwd_done(self, tag, outs, W, P):
        if tag == "b":
            g_fo, g_si, g_so = self._whole("b", outs)
            W["ffn_out"] = [g_fo[:, l].reshape(-1, D_MODEL) for l in range(2)]
            W["ssm_in"], W["ssm_out"] = self._cat_cols(g_si), g_so.reshape(-1, D_MODEL)
            return
        g_fi, g_sm = self._whole("a", outs)
        W["ffn_in"] = [jnp.concatenate([g_fi[j, l] for j in range(N_CHIPS)], axis=1) for l in range(2)]
        sm = g_sm.reshape(N_CHIPS, -1)
        CC = D_INNER + 2 * SSM_GROUPS * SSM_STATE
        c4, f4 = CC // N_CHIPS, D_FF // N_CHIPS
        o1 = SSM_CONV * c4
        o2 = o1 + c4
        o3 = o2 + D_INNER // N_CHIPS
        o4 = o3 + 2 * FFN_CONV * f4
        P["ssm_conv_w"] = sm[:, :o1].reshape(N_CHIPS, SSM_CONV, c4).transpose(1, 0, 2).reshape(SSM_CONV, CC)
        P["ssm_conv_b"] = sm[:, o1:o2].reshape(1, CC)
        P["ssm_norm"] = sm[:, o2:o3].reshape(1, D_INNER)
        P["ffn_conv_w"] = sm[:, o3:o4].reshape(N_CHIPS, 2, FFN_CONV, f4).transpose(1, 2, 0, 3).reshape(2, FFN_CONV, D_FF)

    def grads(self, tag, cols, rows):
        slabs = [_cols_to_slabs(cols), _rows_to_slabs(rows)]
        from_sib = pair_swap_halves(slabs, name="grad_pair_swap_" + tag)
        pairs = [pair_add(g, r, self.csel, name=f"grad_pair_add_{tag}{i}") for i, (g, r) in enumerate(zip(slabs, from_sib))]
        self.p32[tag] = [p[0] for p in pairs]
        return scatter_rider([p[1] for p in pairs])

    def bwd_done(self, tag, outs):
        self.mine[tag] = [chip_add(p, r, self.msel, name=f"grad_chip_add_{tag}{i}")
                          for i, (p, r) in enumerate(zip(self.p32[tag], outs))]

    def finish(self, d_attn_in, d_attn_out):
        self.bwd_done("at", run_rider(self.grads("at", d_attn_in, d_attn_out), name="grad_scatter_at"))
        order = ["at", "s", "f0", "f1"]
        mine = [m for t in order for m in self.mine[t]]
        theirs = pair_share(mine, name="grad_pair_share")
        full = [jnp.where(self.ci == 0, jnp.stack([a, b]), jnp.stack([b, a])).reshape((-1, a.shape[-1]))
                for a, b in zip(mine, theirs)]
        ai, ao, si, so, fi0, fo0, fi1, fo1 = full
        return {"attn_w_in": ai[None], "attn_w_out": ao[None], "ssm_w_in": si[None], "ssm_w_out": so[None],
                "ffn_w_in": jnp.stack([fi0, fi1]), "ffn_w_out": jnp.stack([fo0, fo1])}


def _adamw(w, g, m, v, name):
    shp = w.shape
    two = lambda a: a.reshape((-1, shp[-1]))
    outs = [(shp[-1], F32)] * 3
    d, nm, nv = rowwise(f_adamw, [two(w), two(g), two(m), two(v)], [], outs, name="adamw_" + name)
    return d.reshape(shp), nm.reshape(shp), nv.reshape(shp)


def kernel(x, norm_mix, norm_ffn, attn_w_in, attn_w_out, relpos_table, q_norm_a, k_norm_a, q_norm_b, k_norm_b, sinks, ssm_w_in, ssm_conv_w, ssm_conv_b, ssm_dt_bias, ssm_a_log, ssm_d, ssm_norm, ssm_w_out, ffn_w_in, ffn_conv_w, ffn_conv_b, ffn_w_out, loss_target, m_norm_mix, m_norm_ffn, m_attn_w_in, m_attn_w_out, m_relpos_table, m_q_norm_a, m_k_norm_a, m_q_norm_b, m_k_norm_b, m_sinks, m_ssm_w_in, m_ssm_conv_w, m_ssm_conv_b, m_ssm_dt_bias, m_ssm_a_log, m_ssm_d, m_ssm_norm, m_ssm_w_out, m_ffn_w_in, m_ffn_conv_w, m_ffn_conv_b, m_ffn_w_out, v_norm_mix, v_norm_ffn, v_attn_w_in, v_attn_w_out, v_relpos_table, v_q_norm_a, v_k_norm_a, v_q_norm_b, v_k_norm_b, v_sinks, v_ssm_w_in, v_ssm_conv_w, v_ssm_conv_b, v_ssm_dt_bias, v_ssm_a_log, v_ssm_d, v_ssm_norm, v_ssm_w_out, v_ffn_w_in, v_ffn_conv_w, v_ffn_conv_b, v_ffn_w_out):
    d = dict(zip(ARGS, (x, norm_mix, norm_ffn, attn_w_in, attn_w_out, relpos_table, q_norm_a, k_norm_a, q_norm_b, k_norm_b, sinks, ssm_w_in, ssm_conv_w, ssm_conv_b, ssm_dt_bias, ssm_a_log, ssm_d, ssm_norm, ssm_w_out, ffn_w_in, ffn_conv_w, ffn_conv_b, ffn_w_out, loss_target, m_norm_mix, m_norm_ffn, m_attn_w_in, m_attn_w_out, m_relpos_table, m_q_norm_a, m_k_norm_a, m_q_norm_b, m_k_norm_b, m_sinks, m_ssm_w_in, m_ssm_conv_w, m_ssm_conv_b, m_ssm_dt_bias, m_ssm_a_log, m_ssm_d, m_ssm_norm, m_ssm_w_out, m_ffn_w_in, m_ffn_conv_w, m_ffn_conv_b, m_ffn_w_out, v_norm_mix, v_norm_ffn, v_attn_w_in, v_attn_w_out, v_relpos_table, v_q_norm_a, v_k_norm_a, v_q_norm_b, v_k_norm_b, v_sinks, v_ssm_w_in, v_ssm_conv_w, v_ssm_conv_b, v_ssm_dt_bias, v_ssm_a_log, v_ssm_d, v_ssm_norm, v_ssm_w_out, v_ffn_w_in, v_ffn_conv_w, v_ffn_conv_b, v_ffn_w_out)))
    xi, yi, ci = _pos()
    me = 2 * xi + yi
    CC = D_INNER + 2 * SSM_GROUPS * SSM_STATE
    c4, f4 = CC // N_CHIPS, D_FF // N_CHIPS

    P = {k: d[k] for k in ["norm_mix", "norm_ffn", "q_norm_a", "k_norm_a", "q_norm_b", "k_norm_b", "sinks",
                           "ssm_dt_bias", "ssm_a_log", "ssm_d", "ffn_conv_b"]}
    P["relpos_table"] = d["relpos_table"][0]
    comm = MeshComm(d, xi, yi, ci)
    W = comm.first_weights()
    lpart, dx, gW, gP = local_step(d["x"][0], d["loss_target"][0], W, P, comm)
    loss = lax.psum(lpart[0, 0], ("x", "y", "c"))
    grads = comm.finish(gW["attn_in"], gW["attn_out"])

    flat = jnp.concatenate([gP[k].reshape(-1) for k in SMALL_ORDER])
    flat = jnp.pad(flat, (0, SMALL_ROWS * LANES - flat.shape[0])).reshape(SMALL_ROWS, LANES)
    tot = sum_slots(gather_all(flat, name="small_gather"), name="small_sum").reshape(-1)
    off = 0
    for k in SMALL_ORDER:
        n = int(np.prod(gP[k].shape))
        g = tot[off:off + n].reshape(gP[k].shape)
        off += n
        if k == "ssm_conv_w":
            g = lax.dynamic_slice_in_dim(g, me * c4, c4, axis=1)[None]
        elif k == "ssm_conv_b":
            g = lax.dynamic_slice_in_dim(g, me * c4, c4, axis=1)
        elif k == "ssm_norm":
            g = lax.dynamic_slice_in_dim(g, me * (D_INNER // N_CHIPS), D_INNER // N_CHIPS, axis=1)
        elif k == "ffn_conv_w":
            g = lax.dynamic_slice_in_dim(g, me * f4, f4, axis=2)
        elif k == "relpos_table":
            g = g[None]
        grads[k] = g

    deltas, new_m, new_v = {}, {}, {}
    for k in WEIGHTS:
        deltas[k], new_m[k], new_v[k] = _adamw(d[k], grads[k], d["m_" + k], d["v_" + k], k)
    return (loss, dx[None], *[grads[k] for k in WEIGHTS], *[deltas[k] for k in WEIGHTS],
            *[new_m[k] for k in WEIGHTS], *[new_v[k] for k in WEIGHTS])
```

```python
import functools

import numpy as np
import jax
import jax.numpy as jnp
from jax import lax
from jax.experimental import pallas as pl
from jax.experimental.pallas import tpu as pltpu

F32 = jnp.float32
BF16 = jnp.bfloat16
HI = lax.Precision.HIGHEST

D_MODEL = 1024
CHUNK = 64
EPS = 1e-6
HEAD_DIM = 64
N_HEADS = 8
A_PREV = 8
B_PREV = 2
MAX_REL = 256
D_INNER = 2048
SSM_HEADS = 32
SSM_GROUPS = 4
SSM_STATE = 128
SSM_CONV = 4
D_FF = 2816
FFN_CONV = 3
LANES = 128
SUBLANES = 8
VMEM_LIMIT = 56 * 1024 * 1024
SSD_L = 128

ADAM_LR = 0.001
ADAM_B1 = 0.9
ADAM_B2 = 0.999
ADAM_EPS = 1e-08
ADAM_WD = 0.01
ADAM_STEP = 10

MESH = pl.DeviceIdType.MESH


def _params(*sem):
    return pltpu.CompilerParams(dimension_semantics=sem, vmem_limit_bytes=VMEM_LIMIT)


def _pick(n, want):
    if n <= want:
        return n
    t = (want // LANES) * LANES
    while t >= LANES:
        if n % t == 0:
            return t
        t -= LANES
    return n


MM_ROWS = 512
MM_COLS = 1536
MM_RED = 2048
MM_SHORT = 1024


def matmul(a, b, *, mode, name, out_dtype=F32, residual=None):
    dims = {"nn": (((1,), (0,)), ((), ())), "nt": (((1,), (1,)), ((), ())), "tn": (((0,), (0,)), ((), ()))}[mode]
    if mode == "tn":
        assert residual is None and out_dtype == F32
        (K, M), (K2, N) = a.shape, b.shape
        assert K == K2, (a.shape, b.shape)
        tm, tn, tk = _pick(M, MM_COLS), _pick(N, MM_COLS), _pick(K, MM_RED)

        def body(a_ref, b_ref, o_ref):
            k = pl.program_id(2)
            p = lax.dot_general(a_ref[...].astype(BF16), b_ref[...].astype(BF16), dims, preferred_element_type=F32)

            @pl.when(k == 0)
            def _():
                o_ref[...] = p

            @pl.when(k != 0)
            def _():
                o_ref[...] += p

        return pl.pallas_call(
            body, name=name, grid=(M // tm, N // tn, K // tk),
            in_specs=[pl.BlockSpec((tk, tm), lambda i, j, k: (k, i)), pl.BlockSpec((tk, tn), lambda i, j, k: (k, j))],
            out_specs=pl.BlockSpec((tm, tn), lambda i, j, k: (i, j)),
            out_shape=jax.ShapeDtypeStruct((M, N), F32),
            compiler_params=_params("parallel", "parallel", "arbitrary"),
        )(a, b)

    if mode == "nn":
        (M, K), (K2, N) = a.shape, b.shape
    else:
        (M, K), (N, K2) = a.shape, b.shape
    assert K == K2, (a.shape, b.shape, mode)
    tm, tn = _pick(M, MM_ROWS if K > MM_SHORT else 2 * MM_ROWS), _pick(N, MM_COLS)

    def body(*refs):
        a_ref, b_ref = refs[:2]
        o_ref = refs[-1]
        r = lax.dot_general(a_ref[...].astype(BF16), b_ref[...].astype(BF16), dims, preferred_element_type=F32)
        if residual is not None:
            r = r + refs[2][...].astype(F32)
        o_ref[...] = r.astype(o_ref.dtype)

    a_spec = pl.BlockSpec((tm, K), lambda j, i: (i, 0))
    b_spec = pl.BlockSpec((K, tn), lambda j, i: (0, j)) if mode == "nn" else pl.BlockSpec((tn, K), lambda j, i: (j, 0))
    o_spec = pl.BlockSpec((tm, tn), lambda j, i: (i, j))
    in_specs = [a_spec, b_spec] + ([o_spec] if residual is not None else [])
    args = (a, b) + ((residual,) if residual is not None else ())
    return pl.pallas_call(
        body, name=name, grid=(N // tn, M // tm),
        in_specs=in_specs, out_specs=o_spec,
        out_shape=jax.ShapeDtypeStruct((M, N), out_dtype),
        compiler_params=_params("parallel", "parallel"),
    )(*args)


def matmul_norm(a, b, residual, g, *, name):
    (M, K), (_, N) = a.shape, b.shape
    tm = _pick(M, MM_ROWS)

    def body(a_ref, b_ref, r_ref, g_ref, x_ref, h_ref):
        x = jnp.dot(a_ref[...].astype(BF16), b_ref[...].astype(BF16), preferred_element_type=F32) + r_ref[...]
        x_ref[...] = x
        h_ref[...] = f_rmsnorm(x, g_ref[...])[0].astype(BF16)

    row = pl.BlockSpec((tm, N), lambda i: (i, 0))
    return pl.pallas_call(
        body, name=name, grid=(M // tm,),
        in_specs=[pl.BlockSpec((tm, K), lambda i: (i, 0)), pl.BlockSpec((K, N), lambda i: (0, 0)), row,
                  pl.BlockSpec((1, N), lambda i: (0, 0))],
        out_specs=[row, row],
        out_shape=[jax.ShapeDtypeStruct((M, N), F32), jax.ShapeDtypeStruct((M, N), BF16)],
        compiler_params=_params("parallel"),
    )(a, b, residual, g)


def matmul_dnorm(a, b, partial, x, g, dres, *, name):
    (M, K), (N, _) = a.shape, b.shape
    tm = _pick(M, MM_ROWS)
    has_part = partial is not None

    def body(*refs):
        a_ref, b_ref = refs[:2]
        x_ref, g_ref, dres_ref, dx_ref, dg_ref = refs[-5:]
        dh = lax.dot_general(a_ref[...].astype(BF16), b_ref[...].astype(BF16), _NT, preferred_element_type=F32)
        if has_part:
            dh = dh + refs[2][...]
        xv = x_ref[...]
        r = lax.rsqrt(jnp.mean(xv * xv, axis=-1, keepdims=True) + EPS)
        xhat = xv * r
        dxh = dh * g_ref[...]
        dx_ref[...] = dres_ref[...] + r * (dxh - xhat * jnp.mean(dxh * xhat, axis=-1, keepdims=True))
        dg = jnp.sum(dh * xhat, axis=0, keepdims=True)

        @pl.when(pl.program_id(0) == 0)
        def _():
            dg_ref[...] = dg

        @pl.when(pl.program_id(0) != 0)
        def _():
            dg_ref[...] += dg

    row = pl.BlockSpec((tm, N), lambda i: (i, 0))
    vec = pl.BlockSpec((1, N), lambda i: (0, 0))
    in_specs = [pl.BlockSpec((tm, K), lambda i: (i, 0)), pl.BlockSpec((N, K), lambda i: (0, 0))]
    args = [a, b]
    if has_part:
        in_specs.append(row)
        args.append(partial)
    return pl.pallas_call(
        body, name=name, grid=(M // tm,), in_specs=in_specs + [row, vec, row], out_specs=[row, vec],
        out_shape=[jax.ShapeDtypeStruct((M, N), F32), jax.ShapeDtypeStruct((1, N), F32)],
        compiler_params=_params("arbitrary"),
    )(*args, x, g, dres)


def rowwise(f, rows, params, outs, *, name, tm=256, lead=None):
    S = rows[0].shape[0]
    tm = _row_tile(S, tm)
    nr, npar = len(rows), len(params)
    lead = [0] * len(outs) if lead is None else lead
    assert all(ld % tm == 0 for ld in lead)
    padded = [k for k, ld in enumerate(lead) if ld]

    def body(*refs):
        vals = [r[...].astype(F32) for r in refs[:nr + npar]]
        res = f(*vals)
        for o_ref, r in zip(refs[nr + npar + len(padded):], res):
            o_ref[...] = r.astype(o_ref.dtype)

    in_specs = [pl.BlockSpec((tm, r.shape[1]), lambda i: (i, 0)) for r in rows]
    in_specs += [pl.BlockSpec(p.shape, lambda i: (0, 0)) for p in params]
    in_specs += [pl.BlockSpec(memory_space=pl.ANY)] * len(padded)
    zeros = [jnp.zeros((S + lead[k], outs[k][0]), outs[k][1]) for k in padded]
    out_specs = [pl.BlockSpec((tm, c), lambda i, s=ld // tm: (i + s, 0)) for (c, _), ld in zip(outs, lead)]
    out_shape = [jax.ShapeDtypeStruct((S + ld, c), dt) for (c, dt), ld in zip(outs, lead)]
    return pl.pallas_call(body, name=name, grid=(S // tm,), in_specs=in_specs, out_specs=out_specs,
                          out_shape=out_shape, input_output_aliases={nr + npar + n: k for n, k in enumerate(padded)},
                          compiler_params=_params("parallel"))(*rows, *params, *zeros)


def rowwise_vjp(f, rows, params, cots, drow, dpar, *, name, tm=256, cot_skip=None):
    S = rows[0].shape[0]
    tm = _row_tile(S, tm)
    nr, npar, nc = len(rows), len(params), len(cots)
    skip = [0] * nc if cot_skip is None else [s // tm for s in cot_skip]
    assert cot_skip is None or all(s % tm == 0 for s in cot_skip)

    def body(*refs):
        vals = [r[...].astype(F32) for r in refs[:nr + npar]]
        cvals = [r[...].astype(F32) for r in refs[nr + npar:nr + npar + nc]]
        o_refs = refs[nr + npar + nc:]
        want = [ri for ri, _ in drow] + [nr + pi for pi in dpar]

        def f_want(*d):
            full = list(vals)
            for k, v in zip(want, d):
                full[k] = v
            return f(*full)

        _, vjp = jax.vjp(f_want, *[vals[k] for k in want])
        grads = vjp(tuple(cvals))
        for o_ref, g in zip(o_refs[:len(drow)], grads):
            o_ref[...] = g.astype(o_ref.dtype)
        first = pl.program_id(0) == 0
        for o_ref, g in zip(o_refs[len(drow):], grads[len(drow):]):
            g = g.astype(F32)

            @pl.when(first)
            def _(o_ref=o_ref, g=g):
                o_ref[...] = g

            @pl.when(jnp.logical_not(first))
            def _(o_ref=o_ref, g=g):
                o_ref[...] += g

    in_specs = [pl.BlockSpec((tm, r.shape[1]), lambda i: (i, 0)) for r in rows]
    in_specs += [pl.BlockSpec(p.shape, lambda i: (0, 0)) for p in params]
    in_specs += [pl.BlockSpec((tm, c.shape[1]), lambda i, s=s: (i + s, 0)) for c, s in zip(cots, skip)]
    out_specs = [pl.BlockSpec((tm, rows[ri].shape[1]), lambda i: (i, 0)) for ri, _ in drow]
    out_specs += [pl.BlockSpec(params[pi].shape, lambda i: (0, 0)) for pi in dpar]
    out_shape = [jax.ShapeDtypeStruct(rows[ri].shape, dt) for ri, dt in drow]
    out_shape += [jax.ShapeDtypeStruct(params[pi].shape, F32) for pi in dpar]
    return pl.pallas_call(body, name=name, grid=(S // tm,), in_specs=in_specs, out_specs=out_specs,
                          out_shape=out_shape, compiler_params=_params("arbitrary"))(*rows, *params, *cots)


HALO = 2 * SUBLANES
CONV_ROWS = 64


def dwconv_fwd(x, w, b, post, extra, outs, *, name, tm=256):
    S, C = x.shape
    K = w.shape[0]
    tm = min(tm, S)
    hb = tm // HALO
    ne = len(extra)

    def body(*refs):
        x_ref, halo_ref, w_ref, b_ref = refs[:4]
        e_refs = refs[4:4 + ne]
        o_refs = refs[4 + ne:4 + ne + len(outs)]
        buf = refs[-1]
        i = pl.program_id(0)
        buf[0:HALO, :] = jnp.where(i == 0, 0.0, halo_ref[...].astype(F32))
        buf[HALO:HALO + tm, :] = x_ref[...].astype(F32)
        for c0 in range(0, C, LANES):
            cs = slice(c0, c0 + LANES)
            for r0 in range(0, tm, CONV_ROWS):
                rs = slice(r0, r0 + CONV_ROWS)
                acc = jnp.broadcast_to(b_ref[:, cs], (CONV_ROWS, LANES))
                for k in range(K):
                    acc = acc + w_ref[k:k + 1, cs] * buf[pl.ds(HALO - (K - 1) + k + r0, CONV_ROWS), cs]
                for o_ref, r in zip(o_refs, post(acc, *[e[rs, cs].astype(F32) for e in e_refs])):
                    o_ref[rs, cs] = r.astype(o_ref.dtype)

    row = pl.BlockSpec((tm, C), lambda i: (i, 0))
    return pl.pallas_call(
        body, name=name, grid=(S // tm,),
        in_specs=[row,
                  pl.BlockSpec((HALO, C), lambda i: (jnp.maximum(i * hb - 1, 0), 0)),
                  pl.BlockSpec((K, C), lambda i: (0, 0)),
                  pl.BlockSpec((1, C), lambda i: (0, 0))] + [row] * ne,
        out_specs=[row] * len(outs),
        out_shape=[jax.ShapeDtypeStruct((S, C), dt) for dt in outs],
        scratch_shapes=[pltpu.VMEM((HALO + tm, C), F32)],
        compiler_params=_params("parallel"),
    )(x, x, w, b, *extra)


def dwconv_bwd(x, w, srcs, dy_fn, extra_outs, *, name, tm=256):
    S, C = x.shape
    K = w.shape[0]
    tm = min(tm, S)
    hb = tm // HALO
    n = S // tm
    groups = [s if isinstance(s, tuple) else (s,) for s in srcs]
    flat = [a for g in groups for a in g]
    nf = len(flat)

    def body(*refs):
        x_ref, xh_ref, w_ref = refs[:3]
        dx_ref, dw_ref, db_ref = refs[3 + 2 * nf:6 + 2 * nf]
        e_refs = refs[6 + 2 * nf:6 + 2 * nf + len(extra_outs)]
        bx, bd = refs[-2:]

        def strips(first, c0, rs):
            out, at = [], first
            for g in groups:
                off = 0
                for a in g:
                    if off <= c0 < off + a.shape[1]:
                        out.append(refs[at][rs, c0 - off:c0 - off + LANES].astype(F32))
                    off += a.shape[1]
                    at += 1
            return out

        i = pl.program_id(0)
        bx[0:HALO, :] = jnp.where(i == 0, 0.0, xh_ref[...].astype(F32))
        bx[HALO:HALO + tm, :] = x_ref[...].astype(F32)

        @pl.when(i == 0)
        def _():
            dw_ref[...] = jnp.zeros_like(dw_ref)
            db_ref[...] = jnp.zeros_like(db_ref)

        for c0 in range(0, C, LANES):
            cs = slice(c0, c0 + LANES)
            dws = [jnp.zeros((1, LANES), F32) for _ in range(K)]
            dbs = jnp.zeros((1, LANES), F32)
            for r0 in range(0, tm, CONV_ROWS):
                rs = slice(r0, r0 + CONV_ROWS)
                res = dy_fn(*strips(3, c0, rs))
                dyv = res[0]
                for e_ref, r in zip(e_refs, res[1:]):
                    e_ref[rs, cs] = r.astype(e_ref.dtype)
                bd[rs, cs] = dyv
                for k in range(K):
                    dws[k] = dws[k] + jnp.sum(dyv * bx[pl.ds(HALO - (K - 1) + k + r0, CONV_ROWS), cs], axis=0,
                                              keepdims=True)
                dbs = dbs + jnp.sum(dyv, axis=0, keepdims=True)
            bd[tm:tm + HALO, cs] = jnp.where(i == n - 1, 0.0, dy_fn(*strips(3 + nf, c0, slice(None)))[0])
            for k in range(K):
                dw_ref[k:k + 1, cs] += dws[k]
            db_ref[:, cs] += dbs
            for r0 in range(0, tm, CONV_ROWS):
                acc = jnp.zeros((CONV_ROWS, LANES), F32)
                for k in range(K):
                    acc = acc + w_ref[k:k + 1, cs] * bd[pl.ds((K - 1) - k + r0, CONV_ROWS), cs]
                dx_ref[r0:r0 + CONV_ROWS, cs] = acc.astype(dx_ref.dtype)

    row = lambda c: pl.BlockSpec((tm, c), lambda i: (i, 0))
    nxt = lambda c: pl.BlockSpec((HALO, c), lambda i: (jnp.minimum((i + 1) * hb, S // HALO - 1), 0))
    return pl.pallas_call(
        body, name=name, grid=(n,),
        in_specs=[row(C), pl.BlockSpec((HALO, C), lambda i: (jnp.maximum(i * hb - 1, 0), 0)),
                  pl.BlockSpec((K, C), lambda i: (0, 0))]
                 + [row(a.shape[1]) for a in flat] + [nxt(a.shape[1]) for a in flat],
        out_specs=[row(C), pl.BlockSpec((K, C), lambda i: (0, 0)), pl.BlockSpec((1, C), lambda i: (0, 0))]
                  + [row(C)] * len(extra_outs),
        out_shape=[jax.ShapeDtypeStruct((S, C), BF16), jax.ShapeDtypeStruct((K, C), F32),
                   jax.ShapeDtypeStruct((1, C), F32)] + [jax.ShapeDtypeStruct((S, C), dt) for dt in extra_outs],
        scratch_shapes=[pltpu.VMEM((HALO + tm, C), F32), pltpu.VMEM((tm + HALO, C), F32)],
        compiler_params=_params("arbitrary"),
    )(x, x, w, *flat, *flat)


def _sigmoid(x):
    return 0.5 * jnp.tanh(0.5 * x) + 0.5


def _silu(x):
    return x * _sigmoid(x)


def _dsilu(x):
    s = _sigmoid(x)
    return s * (1.0 + x * (1.0 - s))


def f_rmsnorm(x, g):
    return (x * lax.rsqrt(jnp.mean(x * x, axis=-1, keepdims=True) + EPS) * g,)


SEL = lax.Precision.HIGH


def _group_norm(x, bd, width):
    ms = jnp.dot(x * x, bd, precision=SEL, preferred_element_type=F32) * (1.0 / width)
    return x * lax.rsqrt(ms + EPS)


def f_qknorm(qkv, gqa, gka, gqb, gkb, bd512, bd128, fold, expand):
    dq = N_HEADS * HEAD_DIM
    qa, ka, va, qb = (qkv[:, i * dq:(i + 1) * dq] for i in range(4))
    kb = qkv[:, 4 * dq:4 * dq + LANES]
    vb = qkv[:, 4 * dq + LANES:4 * dq + 2 * LANES]
    tile8 = lambda g: jnp.dot(g, fold, precision=HI, preferred_element_type=F32)
    qa = _group_norm(qa, bd512, HEAD_DIM) * tile8(gqa)
    ka = _group_norm(ka, bd512, HEAD_DIM) * tile8(gka)
    qb = _group_norm(qb, bd512, HEAD_DIM) * tile8(gqb)
    kb = _group_norm(kb, bd128, HEAD_DIM) * tile8(gkb)[:, :LANES]
    kb = jnp.dot(kb, expand, precision=SEL, preferred_element_type=F32)
    vb = jnp.dot(vb, expand, precision=SEL, preferred_element_type=F32)
    return qa, ka, va, qb, kb, vb


def f_gate_norm(y, z, nw):
    v = y * _silu(z)
    gw = D_INNER // SSM_GROUPS
    parts = []
    for g in range(SSM_GROUPS):
        vg = v[:, g * gw:(g + 1) * gw]
        parts.append(vg * lax.rsqrt(jnp.mean(vg * vg, axis=-1, keepdims=True) + EPS))
    return (jnp.concatenate(parts, axis=-1) * nw,)


ATT_TQ = 256
_NT = (((1,), (1,)), ((), ()))
_TN = (((0,), (0,)), ((), ()))


def _stack_heads(t, head0):
    return jnp.concatenate([jnp.where(head0, t, 0.0), jnp.where(head0, 0.0, t)], axis=0).astype(BF16)


def _attn_probs(qk, bias, valid, snk):
    s = qk * (HEAD_DIM ** -0.5) + bias
    s = jnp.where(valid, s, -jnp.inf)
    m = jnp.max(s, axis=1, keepdims=True)
    if snk is not None:
        m = jnp.maximum(m, snk)
    e = jnp.exp(s - m)
    den = jnp.sum(e, axis=1, keepdims=True)
    if snk is None:
        return e / den, None
    es = jnp.exp(snk - m)
    den = den + es
    return e / den, es / den


def widen_bias(bias, n_prev, nj):
    band = (n_prev + 1) * CHUNK
    wk = (nj + n_prev) * CHUNK
    rows = [jnp.pad(bias, ((0, 0), (0, 0), (j * CHUNK, wk - band - j * CHUNK)), constant_values=-jnp.inf)
            for j in range(nj)]
    return jnp.concatenate(rows, axis=1)


def fold_bias(dbw, n_prev, nj):
    band = (n_prev + 1) * CHUNK
    acc = dbw[:, :CHUNK, :band]
    for j in range(1, nj):
        acc = acc + dbw[:, j * CHUNK:(j + 1) * CHUNK, j * CHUNK:j * CHUNK + band]
    return acc


def attn_fwd(q, k, v, bias_w, sinks, *, n_prev, name, rider=None):
    S = q.shape[0]
    pad = n_prev * CHUNK
    kv_rows = k.shape[0]
    lead = kv_rows - S - pad
    tq = min(ATT_TQ, S)
    wk = tq + pad
    assert bias_w.shape == (N_HEADS, tq, wk), bias_w.shape
    has_sink = sinks is not None

    r_in, r_out, r_shapes, r_sems, r_args = _rider_parts(rider)
    n_own = 5 if has_sink else 4
    n_p, n_i = N_HEADS // 2, S // tq

    def body(*refs):
        q_ref, k_ref, v_ref, bias_ref = refs[:4]
        sink_ref = refs[4] if has_sink else None
        o_ref = refs[n_own + len(r_in)]
        if rider is not None:
            p_id, i_id = pl.program_id(0), pl.program_id(1)
            _ride(rider, refs[n_own:n_own + len(r_in)], refs[n_own + len(r_in) + 1:n_own + len(r_in) + 1 + len(r_out)],
                  refs[n_own + len(r_in) + 1 + len(r_out):],
                  jnp.logical_and(p_id == 0, i_id == 0), jnp.logical_and(p_id == n_p - 1, i_id == 0),
                  jnp.logical_and(p_id == n_p - 1, i_id == n_i - 1))
        start = pl.multiple_of(pl.program_id(1) * tq, tq)
        head0 = lax.broadcasted_iota(jnp.int32, (1, LANES), 1) < HEAD_DIM
        valid = lax.broadcasted_iota(jnp.int32, (1, wk), 1) + start >= pad
        kb = k_ref[pl.ds(pl.multiple_of(start + lead, CHUNK), wk), :]
        vb = v_ref[pl.ds(pl.multiple_of(start + lead, CHUNK), wk), :]
        qk = lax.dot_general(_stack_heads(q_ref[...].astype(F32), head0), kb, _NT, preferred_element_type=F32)
        ps = []
        for r in range(2):
            snk = sink_ref[0, r:r + 1, 0:1] if has_sink else None
            ps.append(_attn_probs(qk[r * tq:(r + 1) * tq, :], bias_ref[r], valid, snk)[0].astype(BF16))
        o2 = jnp.dot(jnp.concatenate(ps, axis=0), vb, preferred_element_type=F32)
        o_ref[...] = jnp.where(head0, o2[:tq, :], o2[tq:, :]).astype(o_ref.dtype)

    in_specs = [pl.BlockSpec((tq, LANES), lambda p, i: (i, p)),
                pl.BlockSpec((kv_rows, LANES), lambda p, i: (0, p)),
                pl.BlockSpec((kv_rows, LANES), lambda p, i: (0, p)),
                pl.BlockSpec((2, tq, wk), lambda p, i: (p, 0, 0))]
    args = [q, k, v, bias_w]
    if has_sink:
        in_specs.append(pl.BlockSpec((1, 2, LANES), lambda p, i: (p, 0, 0)))
        args.append(sinks)
    res = pl.pallas_call(
        body, name=name, grid=(n_p, n_i), in_specs=in_specs + r_in,
        out_specs=[pl.BlockSpec((tq, LANES), lambda p, i: (i, p))] + r_out,
        out_shape=[jax.ShapeDtypeStruct((S, N_HEADS * HEAD_DIM), BF16)] + r_shapes,
        scratch_shapes=r_sems,
        compiler_params=pltpu.CompilerParams(dimension_semantics=("arbitrary", "arbitrary"), vmem_limit_bytes=VMEM_LIMIT,
                                             has_side_effects=rider is not None),
    )(*args, *r_args)
    return res[0], res[1:]


def attn_bwd(q, k, v, do, bias_w, sinks, *, n_prev, name, rider=None):
    S = q.shape[0]
    pad = n_prev * CHUNK
    kv_rows = k.shape[0]
    lead = kv_rows - S - pad
    tq = min(ATT_TQ, S)
    wk = tq + pad
    assert bias_w.shape == (N_HEADS, tq, wk), bias_w.shape
    has_sink = sinks is not None
    scale = HEAD_DIM ** -0.5

    r_in, r_out, r_shapes, r_sems, r_args = _rider_parts(rider)
    n_own_in = 6 if has_sink else 5
    n_own_out = 5 if has_sink else 4
    n_p, n_i = N_HEADS // 2, S // tq

    def body(*refs):
        q_ref, k_ref, v_ref, do_ref, bias_ref = refs[:5]
        sink_ref = refs[5] if has_sink else None
        o0 = n_own_in + len(r_in)
        dq_ref, dk_ref, dv_ref, db_ref = refs[o0:o0 + 4]
        dsk_ref = refs[o0 + 4] if has_sink else None
        i = pl.program_id(1)
        if rider is not None:
            p_id = pl.program_id(0)
            _ride(rider, refs[n_own_in:o0], refs[o0 + n_own_out:o0 + n_own_out + len(r_out)],
                  refs[o0 + n_own_out + len(r_out):],
                  jnp.logical_and(p_id == 0, i == 0), jnp.logical_and(p_id == n_p // 2, i == 0),
                  jnp.logical_and(p_id == n_p - 1, i == n_i - 1))

        @pl.when(i == 0)
        def _():
            dk_ref[...] = jnp.zeros_like(dk_ref)
            dv_ref[...] = jnp.zeros_like(dv_ref)
            db_ref[...] = jnp.zeros_like(db_ref)
            if has_sink:
                dsk_ref[...] = jnp.zeros_like(dsk_ref)

        start = pl.multiple_of(i * tq, tq)
        head0 = lax.broadcasted_iota(jnp.int32, (1, LANES), 1) < HEAD_DIM
        valid = lax.broadcasted_iota(jnp.int32, (1, wk), 1) + start >= pad
        kb = k_ref[pl.ds(pl.multiple_of(start + lead, CHUNK), wk), :]
        vb = v_ref[pl.ds(pl.multiple_of(start + lead, CHUNK), wk), :]
        q2 = _stack_heads(q_ref[...].astype(F32), head0)
        do2 = _stack_heads(do_ref[...].astype(F32), head0)
        qk = lax.dot_general(q2, kb, _NT, preferred_element_type=F32)
        dp2 = lax.dot_general(do2, vb, _NT, preferred_element_type=F32)
        pbs, dss = [], []
        for r in range(2):
            rows = slice(r * tq, (r + 1) * tq)
            snk = sink_ref[0, r:r + 1, 0:1] if has_sink else None
            p, ps = _attn_probs(qk[rows, :], bias_ref[r], valid, snk)
            dp = dp2[rows, :]
            delta = jnp.sum(p * dp, axis=1, keepdims=True)
            ds = p * (dp - delta)
            db_ref[r] += ds
            if has_sink:
                dsk = -jnp.sum(ps * delta, axis=0, keepdims=True)
                dsk_ref[0, r:r + 1, :] += jnp.broadcast_to(dsk, (1, LANES))
            pbs.append(p.astype(BF16))
            dss.append(ds.astype(BF16))
        ds2 = jnp.concatenate(dss, axis=0)
        dq2 = jnp.dot(ds2, kb, preferred_element_type=F32) * scale
        dq_ref[...] = jnp.where(head0, dq2[:tq, :], dq2[tq:, :])
        dk_ref[pl.ds(pl.multiple_of(start + lead, CHUNK), wk), :] += lax.dot_general(ds2, q2, _TN, preferred_element_type=F32) * scale
        dv_ref[pl.ds(pl.multiple_of(start + lead, CHUNK), wk), :] += lax.dot_general(jnp.concatenate(pbs, axis=0), do2, _TN,
                                                       preferred_element_type=F32)

    row_spec = pl.BlockSpec((tq, LANES), lambda p, i: (i, p))
    kv_spec = pl.BlockSpec((kv_rows, LANES), lambda p, i: (0, p))
    bias_spec = pl.BlockSpec((2, tq, wk), lambda p, i: (p, 0, 0))
    sink_spec = pl.BlockSpec((1, 2, LANES), lambda p, i: (p, 0, 0))
    in_specs = [row_spec, kv_spec, kv_spec, row_spec, bias_spec]
    args = [q, k, v, do, bias_w]
    out_specs = [row_spec, kv_spec, kv_spec, bias_spec]
    W = N_HEADS * HEAD_DIM
    out_shape = [jax.ShapeDtypeStruct((S, W), F32), jax.ShapeDtypeStruct((kv_rows, W), F32),
                 jax.ShapeDtypeStruct((kv_rows, W), F32), jax.ShapeDtypeStruct((N_HEADS, tq, wk), F32)]
    if has_sink:
        in_specs.append(sink_spec)
        args.append(sinks)
        out_specs.append(sink_spec)
        out_shape.append(jax.ShapeDtypeStruct((N_HEADS // 2, 2, LANES), F32))
    res = pl.pallas_call(
        body, name=name, grid=(n_p, n_i), in_specs=in_specs + r_in, out_specs=out_specs + r_out,
        out_shape=out_shape + r_shapes, scratch_shapes=r_sems,
        compiler_params=pltpu.CompilerParams(dimension_semantics=("arbitrary", "arbitrary"), vmem_limit_bytes=VMEM_LIMIT,
                                             has_side_effects=rider is not None),
    )(*args, *r_args)
    return res[:n_own_out], res[n_own_out:]


HP = SSM_HEADS // 2
PAIRS_PER_GROUP = HP // SSM_GROUPS
HEADS_PER_GROUP = SSM_HEADS // SSM_GROUPS
GW = HEADS_PER_GROUP * 64


def _ssd_dt(dtraw, dtb, A, tril):
    lane = lax.broadcasted_iota(jnp.int32, (1, LANES), 1)
    u = dtraw + dtb
    eu = jnp.exp(-jnp.abs(u))
    w1 = 1.0 + eu
    l1p = jnp.where(w1 == 1.0, eu, jnp.log(w1) * eu / jnp.where(w1 == 1.0, 1.0, w1 - 1.0))
    dt = jnp.where(lane < SSM_HEADS, jnp.maximum(u, 0.0) + l1p, 0.0)
    acs = jnp.dot(tril, dt * A, precision=HI, preferred_element_type=F32)
    return u, dt, acs


def _head_expander():
    hw = D_INNER // SSM_HEADS
    return (np.arange(LANES)[:, None] == np.arange(D_INNER)[None, :] // hw).astype(np.float32)


def _select_dot(t, sel):
    hi = t.astype(BF16)
    lo = (t - hi.astype(F32)).astype(BF16)
    return jnp.dot(hi, sel, preferred_element_type=F32) + jnp.dot(lo, sel, preferred_element_type=F32)


def ssd_fwd(xbc, dtraw, dtb, A, dexp, *, name):
    S = xbc.shape[0]
    L = min(SSD_L, S)
    nc = S // L
    N = SSM_STATE
    e_mat = jnp.asarray(_head_expander(), dtype=BF16)

    def body(xs_ref, b_ref, c_ref, dtr_ref, dtb_ref, a_ref, d_ref, e_ref, y_ref, st_out_ref, st_ref, xw_ref):
        c = pl.program_id(0)

        @pl.when(c == 0)
        def _():
            st_ref[...] = jnp.zeros_like(st_ref)

        st_out_ref[0] = st_ref[...]
        ri = lax.broadcasted_iota(jnp.int32, (L, L), 0)
        ci = lax.broadcasted_iota(jnp.int32, (L, L), 1)
        trilb = ri >= ci
        head0 = lax.broadcasted_iota(jnp.int32, (1, LANES), 1) < 64
        _, dt, acs = _ssd_dt(dtr_ref[...], dtb_ref[...], a_ref[...], trilb.astype(F32))
        acsT = acs.T
        last = acs[L - 1:L, :]
        expand = lambda t: _select_dot(t, e_ref[...])
        dte, eae, wte = expand(dt), expand(jnp.exp(acs)), expand(jnp.exp(last - acs) * dt)
        lasts = [last[:, h:h + 1] for h in range(SSM_HEADS)]
        for g in range(SSM_GROUPS):
            Bg = b_ref[:, g * N:(g + 1) * N].astype(BF16)
            Cg = c_ref[:, g * N:(g + 1) * N].astype(BF16)
            CB = lax.dot_general(Cg, Bg, _NT, preferred_element_type=F32)
            Z = lax.dot_general(Cg, st_ref[g * GW:(g + 1) * GW, :].astype(BF16), _NT, preferred_element_type=F32)
            for q in range(PAIRS_PER_GROUP):
                hp = g * PAIRS_PER_GROUP + q
                sl = slice(hp * LANES, (hp + 1) * LANES)
                xs = xs_ref[:, sl]
                xd = xs * dte[:, sl]
                ms, xh = [], []
                for r in range(2):
                    h = 2 * hp + r
                    dec = jnp.exp(jnp.where(trilb, acs[:, h:h + 1] - acsT[h:h + 1, :], -jnp.inf))
                    ms.append((CB * dec).astype(BF16))
                    xh.append(jnp.where(head0 if r == 0 else jnp.logical_not(head0), xd, 0.0).astype(BF16))
                yi = jnp.dot(jnp.concatenate(ms, axis=1), jnp.concatenate(xh, axis=0), preferred_element_type=F32)
                y_ref[:, sl] = yi + Z[:, q * LANES:(q + 1) * LANES] * eae[:, sl] + d_ref[:, sl] * xs
                xw_ref[:, sl] = (xs * wte[:, sl]).astype(BF16)
        for g in range(SSM_GROUPS):
            Bg = b_ref[:, g * N:(g + 1) * N].astype(BF16)
            sn = lax.dot_general(xw_ref[:, g * GW:(g + 1) * GW], Bg, _TN, preferred_element_type=F32)
            for k in range(HEADS_PER_GROUP):
                h = g * HEADS_PER_GROUP + k
                rows = slice(h * 64, (h + 1) * 64)
                st_ref[rows, :] = st_ref[rows, :] * jnp.exp(lasts[h]) + sn[k * 64:(k + 1) * 64, :]

    return pl.pallas_call(
        body, name=name, grid=(nc,),
        in_specs=[pl.BlockSpec((L, D_INNER), lambda c: (c, 0)),
                  pl.BlockSpec((L, SSM_GROUPS * N), lambda c: (c, D_INNER // (SSM_GROUPS * N))),
                  pl.BlockSpec((L, SSM_GROUPS * N), lambda c: (c, D_INNER // (SSM_GROUPS * N) + 1)),
                  pl.BlockSpec((L, LANES), lambda c: (c, 0)),
                  pl.BlockSpec((1, LANES), lambda c: (0, 0)),
                  pl.BlockSpec((1, LANES), lambda c: (0, 0)),
                  pl.BlockSpec((1, D_INNER), lambda c: (0, 0)),
                  pl.BlockSpec((LANES, D_INNER), lambda c: (0, 0))],
        out_specs=[pl.BlockSpec((L, D_INNER), lambda c: (c, 0)),
                   pl.BlockSpec((1, D_INNER, N), lambda c: (c, 0, 0))],
        out_shape=[jax.ShapeDtypeStruct((S, D_INNER), F32), jax.ShapeDtypeStruct((nc, D_INNER, N), F32)],
        scratch_shapes=[pltpu.VMEM((D_INNER, N), F32), pltpu.VMEM((L, D_INNER), BF16)],
        compiler_params=_params("arbitrary"),
    )(xbc, xbc, xbc, dtraw, dtb, A, dexp, e_mat)


def ssd_bwd(xbc, dtraw, dtb, A, dexp, states, dy, *, name, rider=None):
    S = xbc.shape[0]
    L = min(SSD_L, S)
    nc = S // L
    N = SSM_STATE
    e_np = _head_expander()
    e_mat, et_mat = jnp.asarray(e_np, dtype=BF16), jnp.asarray(e_np.T, dtype=BF16)

    r_in, r_out, r_shapes, r_sems, r_args = _rider_parts(rider)

    def body(*refs):
        xs_ref, b_ref, c_ref, dtr_ref, dtb_ref, a_ref, d_ref, e_ref, et_ref, st_in_ref, dy_ref = refs[:11]
        o0 = 11 + len(r_in)
        dxs_ref, db_ref, dc_ref, ddtr_ref, da_ref, ddtb_ref, dd_ref = refs[o0:o0 + 7]
        s0 = o0 + 7 + len(r_out)
        dst_ref, xw_ref, dz_ref, r_ref, dsr_ref, dsc_ref = refs[s0:s0 + 6]
        step = pl.program_id(0)
        if rider is not None:
            _ride(rider, refs[11:o0], refs[o0 + 7:s0], refs[s0 + 6:], step == 0, step == nc // 2, step == nc - 1)

        @pl.when(step == 0)
        def _():
            dst_ref[...] = jnp.zeros_like(dst_ref)
            dsr_ref[...] = jnp.zeros_like(dsr_ref)
            dsc_ref[...] = jnp.zeros_like(dsc_ref)
            da_ref[...] = jnp.zeros_like(da_ref)
            ddtb_ref[...] = jnp.zeros_like(ddtb_ref)
            dd_ref[...] = jnp.zeros_like(dd_ref)

        ri = lax.broadcasted_iota(jnp.int32, (L, L), 0)
        ci = lax.broadcasted_iota(jnp.int32, (L, L), 1)
        trilb = ri >= ci
        lane = lax.broadcasted_iota(jnp.int32, (1, LANES), 1)
        sub = lax.broadcasted_iota(jnp.int32, (LANES, 1), 0)
        head0 = lane < 64
        A = a_ref[...]
        u, dt, acs = _ssd_dt(dtr_ref[...], dtb_ref[...], A, trilb.astype(F32))
        acsT = acs.T
        last = acs[L - 1:L, :]
        elast = jnp.exp(last)
        er = jnp.exp(last - acs)
        wt = er * dt
        expand = lambda t: _select_dot(t, e_ref[...])
        dte, eae, wte = expand(dt), expand(jnp.exp(acs)), expand(wt)
        dlast = jnp.zeros((1, LANES), F32)
        dcbs = []
        for g in range(SSM_GROUPS):
            Bg = b_ref[:, g * N:(g + 1) * N].astype(BF16)
            Cg = c_ref[:, g * N:(g + 1) * N].astype(BF16)
            stg = st_in_ref[0, g * GW:(g + 1) * GW, :]
            dstg = dst_ref[g * GW:(g + 1) * GW, :]
            CB = lax.dot_general(Cg, Bg, _NT, preferred_element_type=F32)
            CBT = lax.dot_general(Bg, Cg, _NT, preferred_element_type=F32)
            Z = lax.dot_general(Cg, stg.astype(BF16), _NT, preferred_element_type=F32)
            U = lax.dot_general(Bg, dstg.astype(BF16), _NT, preferred_element_type=F32)
            dcb = jnp.zeros((L, L), F32)
            for q in range(PAIRS_PER_GROUP):
                hp = g * PAIRS_PER_GROUP + q
                sl = slice(hp * LANES, (hp + 1) * LANES)
                qs = slice(q * LANES, (q + 1) * LANES)
                xs = xs_ref[:, sl]
                dyp = dy_ref[:, sl]
                dtp, eap, wp, Dp = dte[:, sl], eae[:, sl], wte[:, sl], d_ref[:, sl]
                xd = xs * dtp
                dy2 = jnp.concatenate([jnp.where(head0, dyp, 0.0), jnp.where(head0, 0.0, dyp)], axis=0).astype(BF16)
                G2 = lax.dot_general(dy2, xd.astype(BF16), _NT, preferred_element_type=F32)
                mts = []
                for r in range(2):
                    h = 2 * hp + r
                    seg = acs[:, h:h + 1] - acsT[h:h + 1, :]
                    dec = jnp.exp(jnp.where(trilb, seg, -jnp.inf))
                    decT = jnp.exp(jnp.where(ri <= ci, -seg, -jnp.inf))
                    gd = G2[r * L:(r + 1) * L, :] * dec
                    dcb = dcb + gd
                    dseg = gd * CB
                    dsr_ref[:, h:h + 1] = jnp.sum(dseg, axis=1, keepdims=True)
                    dsc_ref[h:h + 1, :] = jnp.sum(dseg, axis=0, keepdims=True)
                    mts.append((CBT * decT).astype(BF16))
                dxd = jnp.dot(jnp.concatenate(mts, axis=1), dy2, preferred_element_type=F32)
                Up = U[:, qs]
                r_ref[0:L, sl] = dyp * Z[:, qs] * eap
                r_ref[L:2 * L, sl] = dxd * xs
                r_ref[2 * L:3 * L, sl] = Up * xs
                dz_ref[:, sl] = (dyp * eap).astype(BF16)
                xw_ref[:, sl] = (xs * wp).astype(BF16)
                dxs_ref[:, sl] = dxd * dtp + Dp * dyp + Up * wp
                dd_ref[:, sl] += jnp.sum(dyp * xs, axis=0, keepdims=True)
            dcbs.append(dcb)
            t = dstg * stg
            for k in range(HEADS_PER_GROUP):
                dlast = dlast + jnp.where(lane == g * HEADS_PER_GROUP + k,
                                          jnp.sum(t[k * 64:(k + 1) * 64, :], keepdims=True), 0.0)
        fold = lambda k: _select_dot(r_ref[k * L:(k + 1) * L, :], et_ref[...])
        r1, r2, dws = fold(0), fold(1), fold(2)
        dww = dws * wt
        ddt = r2 + dws * er
        dacs = r1 - dww + dsr_ref[...] - dsc_ref[...].T
        dlast = dlast * elast + jnp.sum(dww, axis=0, keepdims=True)
        lasts = [last[:, h:h + 1] for h in range(SSM_HEADS)]
        for g in range(SSM_GROUPS):
            Bg = b_ref[:, g * N:(g + 1) * N].astype(BF16)
            Cg = c_ref[:, g * N:(g + 1) * N].astype(BF16)
            gs = slice(g * GW, (g + 1) * GW)
            stb = st_in_ref[0, gs, :].astype(BF16)
            dstb = dst_ref[gs, :].astype(BF16)
            dcbb = dcbs[g].astype(BF16)
            dzg = dz_ref[:, gs]
            dc_ref[:, g * N:(g + 1) * N] = (jnp.dot(dzg, stb, preferred_element_type=F32)
                                            + jnp.dot(dcbb, Bg, preferred_element_type=F32))
            db_ref[:, g * N:(g + 1) * N] = (jnp.dot(xw_ref[:, gs], dstb, preferred_element_type=F32)
                                            + lax.dot_general(dcbb, Cg, _TN, preferred_element_type=F32))
            dsn = lax.dot_general(dzg, Cg, _TN, preferred_element_type=F32)
            for k in range(HEADS_PER_GROUP):
                h = g * HEADS_PER_GROUP + k
                rows = slice(h * 64, (h + 1) * 64)
                dst_ref[rows, :] = dst_ref[rows, :] * jnp.exp(lasts[h]) + dsn[k * 64:(k + 1) * 64, :]
        rowi = lax.broadcasted_iota(jnp.int32, (L, 1), 0)
        dacs = dacs + jnp.where(rowi == L - 1, dlast, 0.0)
        da = jnp.dot((ci >= ri).astype(F32), dacs, precision=HI, preferred_element_type=F32)
        ddt = ddt + da * A
        da_ref[...] += jnp.sum(da * dt, axis=0, keepdims=True)
        ddtr = jnp.where(lane < SSM_HEADS, ddt * _sigmoid(u), 0.0)
        ddtr_ref[...] = ddtr
        ddtb_ref[...] += jnp.sum(ddtr, axis=0, keepdims=True)

    rev = lambda c: nc - 1 - c
    gn = SSM_GROUPS * N
    res = pl.pallas_call(
        body, name=name, grid=(nc,),
        in_specs=[pl.BlockSpec((L, D_INNER), lambda c: (rev(c), 0)),
                  pl.BlockSpec((L, gn), lambda c: (rev(c), D_INNER // gn)),
                  pl.BlockSpec((L, gn), lambda c: (rev(c), D_INNER // gn + 1)),
                  pl.BlockSpec((L, LANES), lambda c: (rev(c), 0)),
                  pl.BlockSpec((1, LANES), lambda c: (0, 0)),
                  pl.BlockSpec((1, LANES), lambda c: (0, 0)),
                  pl.BlockSpec((1, D_INNER), lambda c: (0, 0)),
                  pl.BlockSpec((LANES, D_INNER), lambda c: (0, 0)),
                  pl.BlockSpec((D_INNER, LANES), lambda c: (0, 0)),
                  pl.BlockSpec((1, D_INNER, N), lambda c: (rev(c), 0, 0)),
                  pl.BlockSpec((L, D_INNER), lambda c: (rev(c), 0))] + r_in,
        out_specs=[pl.BlockSpec((L, D_INNER), lambda c: (rev(c), 0)),
                   pl.BlockSpec((L, gn), lambda c: (rev(c), 0)),
                   pl.BlockSpec((L, gn), lambda c: (rev(c), 0)),
                   pl.BlockSpec((L, LANES), lambda c: (rev(c), 0)),
                   pl.BlockSpec((1, LANES), lambda c: (0, 0)),
                   pl.BlockSpec((1, LANES), lambda c: (0, 0)),
                   pl.BlockSpec((1, D_INNER), lambda c: (0, 0))] + r_out,
        out_shape=[jax.ShapeDtypeStruct((S, D_INNER), F32), jax.ShapeDtypeStruct((S, gn), F32),
                   jax.ShapeDtypeStruct((S, gn), F32), jax.ShapeDtypeStruct((S, LANES), F32),
                   jax.ShapeDtypeStruct((1, LANES), F32), jax.ShapeDtypeStruct((1, LANES), F32),
                   jax.ShapeDtypeStruct((1, D_INNER), F32)] + r_shapes,
        scratch_shapes=[pltpu.VMEM((D_INNER, N), F32), pltpu.VMEM((L, D_INNER), BF16), pltpu.VMEM((L, D_INNER), BF16),
                        pltpu.VMEM((3 * L, D_INNER), F32), pltpu.VMEM((L, LANES), F32), pltpu.VMEM((LANES, L), F32)]
                       + r_sems,
        compiler_params=pltpu.CompilerParams(dimension_semantics=("arbitrary",), vmem_limit_bytes=VMEM_LIMIT,
                                             has_side_effects=rider is not None),
    )(xbc, xbc, xbc, dtraw, dtb, A, dexp, e_mat, et_mat, states, dy, *r_args)
    return res[:7], res[7:]


BAND_A = (A_PREV + 1) * CHUNK
REL_W = 640


def _relpos_select():
    k = np.arange(REL_W)
    rel = np.where(k < BAND_A, A_PREV * CHUNK - k, A_PREV * CHUNK - (k - REL_W))
    idx = np.clip(rel, -MAX_REL, MAX_REL) + MAX_REL
    sel = (np.arange(REL_W)[:, None] == idx[None, :]) & (k != BAND_A)[None, :]
    return sel.astype(np.float32)


def relpos_bias(table_pad, *, name):
    def body(t_ref, s_ref, o_ref):
        v = jnp.dot(t_ref[...], s_ref[...], precision=HI, preferred_element_type=F32)
        for h in range(N_HEADS):
            o_ref[h] = pltpu.roll(jnp.broadcast_to(v[h:h + 1, :], (CHUNK, REL_W)), 0, 1, stride=1, stride_axis=0)

    return pl.pallas_call(body, name=name, out_shape=jax.ShapeDtypeStruct((N_HEADS, CHUNK, REL_W), F32),
                          compiler_params=pltpu.CompilerParams(vmem_limit_bytes=VMEM_LIMIT),
                          )(table_pad, jnp.asarray(_relpos_select()))


def relpos_grad(dbias_rev, *, name):
    def body(d_ref, s_ref, o_ref):
        head = lax.broadcasted_iota(jnp.int32, (N_HEADS, 1), 0)
        dv = jnp.zeros((N_HEADS, REL_W), F32)
        for h in range(N_HEADS):
            back = pltpu.roll(d_ref[h], REL_W - (CHUNK - 1), 1, stride=1, stride_axis=0)
            dv = dv + jnp.where(head == h, jnp.sum(back, axis=0, keepdims=True), 0.0)
        o_ref[...] = lax.dot_general(dv, s_ref[...], _NT, precision=HI, preferred_element_type=F32)

    return pl.pallas_call(body, name=name, out_shape=jax.ShapeDtypeStruct((N_HEADS, REL_W), F32),
                          compiler_params=pltpu.CompilerParams(vmem_limit_bytes=VMEM_LIMIT),
                          )(dbias_rev, jnp.asarray(_relpos_select()))


def matmul_loss(a, b, residual, t, *, name):
    (M, K), (_, D) = a.shape, b.shape
    tm = _pick(M, MM_ROWS)

    def body(a_ref, b_ref, r_ref, t_ref, dy_ref, l_ref):
        y = jnp.dot(a_ref[...].astype(BF16), b_ref[...].astype(BF16), preferred_element_type=F32) + r_ref[...]
        e = y - t_ref[...]
        dy_ref[...] = e * (1.0 / D)

        @pl.when(pl.program_id(0) == 0)
        def _():
            l_ref[...] = jnp.zeros_like(l_ref)

        part = jnp.sum(jnp.sum(e * e, axis=1, keepdims=True), axis=0, keepdims=True) * (0.5 / D)
        l_ref[...] += jnp.broadcast_to(part, l_ref.shape)

    row = pl.BlockSpec((tm, D), lambda i: (i, 0))
    return pl.pallas_call(
        body, name=name, grid=(M // tm,),
        in_specs=[pl.BlockSpec((tm, K), lambda i: (i, 0)), pl.BlockSpec((K, D), lambda i: (0, 0)), row, row],
        out_specs=[row, pl.BlockSpec((1, LANES), lambda i: (0, 0))],
        out_shape=[jax.ShapeDtypeStruct((M, D), F32), jax.ShapeDtypeStruct((1, LANES), F32)],
        compiler_params=_params("arbitrary"),
    )(a, b, residual, t)


def f_adamw(w, g, m, v):
    m = ADAM_B1 * m + (1.0 - ADAM_B1) * g
    v = ADAM_B2 * v + (1.0 - ADAM_B2) * (g * g)
    m_hat = m / (1.0 - ADAM_B1 ** ADAM_STEP)
    v_hat = v / (1.0 - ADAM_B2 ** ADAM_STEP)
    delta = -ADAM_LR * (m_hat / (jnp.sqrt(v_hat) + ADAM_EPS) + ADAM_WD * w)
    return delta, m, v


ANY = pl.BlockSpec(memory_space=pl.ANY)


def _pos():
    return lax.axis_index("x"), lax.axis_index("y"), lax.axis_index("c")


def _other_chips(x, y):
    return [(1 - x, y), (x, 1 - y), (1 - x, 1 - y)]


class Rider:
    def __init__(self, ins, outs, sems, start, mid, finish):
        self.ins, self.outs, self.sems = list(ins), list(outs), list(sems)
        self.start, self.mid, self.finish = start, mid, finish


def _rider_parts(rider):
    if rider is None:
        return [], [], [], [], []
    return [ANY] * len(rider.ins), [ANY] * len(rider.outs), rider.outs, rider.sems, rider.ins


def _ride(rider, ins, outs, sems, first, mid, last):
    pos = _pos()

    @pl.when(first)
    def _():
        rider.start(pos, ins, outs, sems)

    if rider.mid is not None:
        @pl.when(mid)
        def _():
            rider.mid(pos, ins, outs, sems)

    @pl.when(last)
    def _():
        rider.finish(pos, ins, outs, sems)


def run_rider(rider, *, name):
    n_in, n_out = len(rider.ins), len(rider.outs)

    def body(*refs):
        ins, outs, sems = refs[:n_in], refs[n_in:n_in + n_out], refs[n_in + n_out:]
        pos = _pos()
        rider.start(pos, ins, outs, sems)
        if rider.mid is not None:
            rider.mid(pos, ins, outs, sems)
        rider.finish(pos, ins, outs, sems)

    return pl.pallas_call(
        body, name=name, in_specs=[ANY] * n_in, out_specs=[ANY] * n_out, out_shape=rider.outs,
        scratch_shapes=rider.sems, compiler_params=pltpu.CompilerParams(has_side_effects=True),
    )(*rider.ins)


def gather_rider(shards):
    n = len(shards)

    def copies(pos, ins, outs, sems):
        x, y, c = pos
        send, recv, fsend, frecv = sems
        me = 2 * x + y
        sib = (x, y, 1 - c)
        first, arrive, passed, theirs = [], [], [], []
        for i in range(n):
            for j, (px, py) in enumerate(_other_chips(x, y)):
                k = 3 * i + j
                far = dict(device_id=(px, py, c), device_id_type=MESH)
                near = dict(device_id=sib, device_id_type=MESH)
                got = outs[i].at[2 * px + py, c]
                his = outs[i].at[2 * px + py, 1 - c]
                first.append(pltpu.make_async_remote_copy(ins[i].at[c], outs[i].at[me, c], send.at[k], recv.at[k], **far))
                arrive.append(pltpu.make_async_remote_copy(ins[i].at[c], got, send.at[k], recv.at[k], **far))
                passed.append(pltpu.make_async_remote_copy(got, got, fsend.at[k], frecv.at[k], **near))
                theirs.append(pltpu.make_async_remote_copy(his, his, fsend.at[k], frecv.at[k], **near))
        return first, arrive, passed, theirs

    def start(*a):
        for cp in copies(*a)[0]:
            cp.start()

    def mid(*a):
        _, arrive, passed, _ = copies(*a)
        for got, cp in zip(arrive, passed):
            got.wait_recv()
            cp.start()

    def finish(*a):
        first, _, passed, theirs = copies(*a)
        for cp in theirs:
            cp.wait_recv()
        for cp in first + passed:
            cp.wait_send()

    return Rider(shards, [jax.ShapeDtypeStruct((4,) + s.shape, s.dtype) for s in shards],
                 [pltpu.SemaphoreType.DMA((3 * n,))] * 4, start, mid, finish)


def scatter_rider(ps):
    n = len(ps)

    def copies(pos, ins, outs, sems):
        x, y, c = pos
        send, recv = sems
        return [pltpu.make_async_remote_copy(ins[i].at[2 * px + py], outs[i].at[j], send.at[3 * i + j], recv.at[3 * i + j],
                                             device_id=(px, py, c), device_id_type=MESH)
                for i in range(n) for j, (px, py) in enumerate(_other_chips(x, y))]

    def start(*a):
        for cp in copies(*a):
            cp.start()

    def finish(*a):
        for cp in copies(*a):
            cp.wait()

    return Rider(ps, [jax.ShapeDtypeStruct((3,) + p.shape[1:], p.dtype) for p in ps],
                 [pltpu.SemaphoreType.DMA((3 * n,))] * 2, start, None, finish)


def pair_swap_halves(gs, *, name):
    n = len(gs)

    def body(*refs):
        ins, outs = refs[:n], refs[n:2 * n]
        send, recv = refs[2 * n:]
        x, y, c = _pos()
        cps = []
        for i in range(n):
            cp = pltpu.make_async_remote_copy(ins[i].at[1 - c], outs[i], send.at[i], recv.at[i],
                                              device_id=(x, y, 1 - c), device_id_type=MESH)
            cp.start()
            cps.append(cp)
        for cp in cps:
            cp.wait()

    return pl.pallas_call(
        body, name=name, in_specs=[ANY] * n, out_specs=[ANY] * n,
        out_shape=[jax.ShapeDtypeStruct(g.shape[1:], g.dtype) for g in gs],
        scratch_shapes=[pltpu.SemaphoreType.DMA((n,)), pltpu.SemaphoreType.DMA((n,))],
        compiler_params=pltpu.CompilerParams(has_side_effects=True),
    )(*gs)


def pair_share(hs, *, name):
    n = len(hs)

    def body(*refs):
        ins, outs = refs[:n], refs[n:2 * n]
        send, recv = refs[2 * n:]
        x, y, c = _pos()
        cps = []
        for i in range(n):
            cp = pltpu.make_async_remote_copy(ins[i], outs[i], send.at[i], recv.at[i],
                                              device_id=(x, y, 1 - c), device_id_type=MESH)
            cp.start()
            cps.append(cp)
        for cp in cps:
            cp.wait()

    return pl.pallas_call(
        body, name=name, in_specs=[ANY] * n, out_specs=[ANY] * n,
        out_shape=[jax.ShapeDtypeStruct(h.shape, h.dtype) for h in hs],
        scratch_shapes=[pltpu.SemaphoreType.DMA((n,)), pltpu.SemaphoreType.DMA((n,))],
        compiler_params=pltpu.CompilerParams(has_side_effects=True),
    )(*hs)


def gather_all(buf, *, name):
    def body(in_ref, out_ref, send, recv, loc):
        x, y, c = _pos()
        lid = 4 * x + 2 * y + c
        lc = pltpu.make_async_copy(in_ref, out_ref.at[lid], loc.at[0])
        lc.start()
        cps = []
        for k in range(1, 8):
            px = 1 - x if k & 4 else x
            py = 1 - y if k & 2 else y
            pc = 1 - c if k & 1 else c
            cp = pltpu.make_async_remote_copy(in_ref, out_ref.at[lid], send.at[k - 1], recv.at[k - 1],
                                              device_id=(px, py, pc), device_id_type=MESH)
            cp.start()
            cps.append((cp, 4 * px + 2 * py + pc, (px, py, pc)))
        for k, (cp, plid, peer) in enumerate(cps):
            cp.wait_send()
            pltpu.make_async_remote_copy(in_ref, out_ref.at[plid], send.at[k], recv.at[k],
                                         device_id=peer, device_id_type=MESH).wait_recv()
        lc.wait()

    return pl.pallas_call(
        body, name=name, in_specs=[ANY], out_specs=ANY,
        out_shape=jax.ShapeDtypeStruct((8,) + buf.shape, buf.dtype),
        scratch_shapes=[pltpu.SemaphoreType.DMA((7,)), pltpu.SemaphoreType.DMA((7,)), pltpu.SemaphoreType.DMA((1,))],
        compiler_params=pltpu.CompilerParams(has_side_effects=True),
    )(buf)


def sum_slots(a, *, name):
    n = a.shape[0]

    def body(a_ref, o_ref):
        acc = a_ref[0]
        for k in range(1, n):
            acc = acc + a_ref[k]
        o_ref[...] = acc

    return pl.pallas_call(body, name=name, out_shape=jax.ShapeDtypeStruct(a.shape[1:], a.dtype),
                          compiler_params=pltpu.CompilerParams(vmem_limit_bytes=VMEM_LIMIT))(a)


def _row_tile(r, want, mult=16):
    t = (min(want, r) // mult) * mult
    while t >= mult:
        if r % t == 0:
            return t
        t -= mult
    return r


def pair_add(g, r1, csel, *, name):
    _, _, r, C = g.shape
    tr = _row_tile(r, 256)

    def body(g_ref, r_ref, c_ref, p32_ref, pb_ref):
        south = c_ref[0:1, 0:1] == 0.0
        p = jnp.where(south, g_ref[0, 0], g_ref[1, 0]) + r_ref[0]
        p32_ref[0] = p
        pb_ref[0] = p.astype(BF16)

    return pl.pallas_call(
        body, name=name, grid=(4, r // tr),
        in_specs=[pl.BlockSpec((2, 1, tr, C), lambda j, t: (0, j, t, 0)), pl.BlockSpec((1, tr, C), lambda j, t: (j, t, 0)),
                  pl.BlockSpec((1, LANES), lambda j, t: (0, 0))],
        out_specs=[pl.BlockSpec((1, tr, C), lambda j, t: (j, t, 0))] * 2,
        out_shape=[jax.ShapeDtypeStruct((4, r, C), F32), jax.ShapeDtypeStruct((4, r, C), BF16)],
        compiler_params=_params("parallel", "parallel"),
    )(g, r1, csel)


def chip_add(p32, r3, msel, *, name):
    _, r, C = p32.shape
    tr = _row_tile(r, 128)

    def body(p_ref, r_ref, m_ref, o_ref):
        me = m_ref[0:1, 0:1]
        acc = jnp.where(me == 0.0, p_ref[0], jnp.where(me == 1.0, p_ref[1], jnp.where(me == 2.0, p_ref[2], p_ref[3])))
        for j in range(3):
            acc = acc + r_ref[j].astype(F32)
        o_ref[...] = acc

    return pl.pallas_call(
        body, name=name, grid=(r // tr,),
        in_specs=[pl.BlockSpec((4, tr, C), lambda t: (0, t, 0)), pl.BlockSpec((3, tr, C), lambda t: (0, t, 0)),
                  pl.BlockSpec((1, LANES), lambda t: (0, 0))],
        out_specs=pl.BlockSpec((tr, C), lambda t: (t, 0)),
        out_shape=jax.ShapeDtypeStruct((r, C), F32),
        compiler_params=_params("parallel"),
    )(p32, r3, msel)


def _consts():
    i512 = np.arange(N_HEADS * HEAD_DIM)
    i128 = np.arange(LANES)
    bd512 = (i512[:, None] // HEAD_DIM == i512[None, :] // HEAD_DIM).astype(np.float32)
    bd128 = (i128[:, None] // HEAD_DIM == i128[None, :] // HEAD_DIM).astype(np.float32)
    fold = (np.arange(HEAD_DIM)[:, None] == (i512[None, :] % HEAD_DIM)).astype(np.float32)
    grp = N_HEADS // 2 * HEAD_DIM
    expand = ((i128[:, None] // HEAD_DIM == i512[None, :] // grp)
              & (i128[:, None] % HEAD_DIM == i512[None, :] % HEAD_DIM)).astype(np.float32)
    band = (B_PREV + 1) * CHUNK
    rel = np.arange(CHUNK)[:, None] - (np.arange(band)[None, :] - B_PREV * CHUNK)
    slopes = 2.0 ** (-8.0 * np.arange(1, N_HEADS + 1, dtype=np.float32) / N_HEADS)
    bias_b = (-slopes[:, None, None] * np.abs(rel).astype(np.float32)[None]).astype(np.float32)
    return [jnp.asarray(a) for a in (bd512, bd128, fold, expand)], jnp.asarray(bias_b)


def _ffn_fwd(xin, h, l, W, P, next_gain=None, target=None):
    Wi = W["ffn_in"][l]
    gate = matmul(h, Wi[:, :D_FF], mode="nn", name=f"ffn{l}_gate", out_dtype=BF16)
    up = matmul(h, Wi[:, D_FF:], mode="nn", name=f"ffn{l}_up", out_dtype=BF16)
    gc, act = dwconv_fwd(gate, P["ffn_conv_w"][l], P["ffn_conv_b"][l:l + 1], lambda y, u: (y, _silu(y) * u), [up],
                         [BF16, BF16], name=f"ffn{l}_conv")
    saved = (xin, h, gate, gc, up, act)
    if target is not None:
        dxout, lpart = matmul_loss(act, W["ffn_out"][l], xin, target, name=f"ffn{l}_out")
        return dxout, saved, lpart
    xout, h_next = matmul_norm(act, W["ffn_out"][l], xin, next_gain, name=f"ffn{l}_out")
    return xout, saved, h_next


def _ffn_bwd(dxout, l, saved, W, P):
    xin, h, gate, gc, up, act = saved
    g = P["norm_ffn"][l:l + 1]
    Wi = W["ffn_in"][l]
    dact = matmul(dxout, W["ffn_out"][l], mode="nt", name=f"ffn{l}_dact", out_dtype=BF16)
    dWo = matmul(act, dxout, mode="tn", name=f"ffn{l}_dwout")
    dgate, dcw, dcb, dup = dwconv_bwd(gate, P["ffn_conv_w"][l], [gc, up, dact],
                                      lambda c, u, da: (da * u * _dsilu(c), da * _silu(c)), [BF16],
                                      name=f"ffn{l}_dconv")
    dh = matmul(dgate, Wi[:, :D_FF], mode="nt", name=f"ffn{l}_dh_gate")
    dxin, dg = matmul_dnorm(dup, Wi[:, D_FF:], dh, xin, g, dxout, name=f"ffn{l}_dh_up")
    dWi = jnp.concatenate([matmul(h, dgate, mode="tn", name=f"ffn{l}_dw_gate"),
                           matmul(h, dup, mode="tn", name=f"ffn{l}_dw_up")], axis=1)
    return dxin, dWi, dWo, dg, dcw, dcb


class NoComm:
    def fwd_rider(self, tag):
        return None

    def fwd_done(self, tag, outs, W, P):
        pass

    def grads(self, tag, cols, rows):
        return None

    def bwd_done(self, tag, outs):
        pass


def local_step(x, tgt, W, P, comm):
    qk_consts, bias_b = _consts()
    DQ = N_HEADS * HEAD_DIM
    tile = _row_tile(x.shape[0], 256)
    kv_lead = [-(-n * CHUNK // tile) * tile for n in (A_PREV, B_PREV)]

    g_mix0 = P["norm_mix"][0:1]
    (h0,) = rowwise(f_rmsnorm, [x], [g_mix0], [(D_MODEL, BF16)], name="attn_norm")
    qkv = matmul(h0, W["attn_in"], mode="nn", name="attn_qkv")
    qk_par = [P["q_norm_a"], P["k_norm_a"], P["q_norm_b"], P["k_norm_b"]] + qk_consts
    qk_lead = [0, kv_lead[0], kv_lead[0], 0, kv_lead[1], kv_lead[1]]
    qa, ka, va, qb, kb, vb = rowwise(f_qknorm, [qkv], qk_par, [(DQ, BF16)] * 6, name="attn_qknorm", lead=qk_lead)
    table = jnp.pad(P["relpos_table"], ((0, 0), (0, REL_W - (2 * MAX_REL + 1))))
    nj = min(ATT_TQ, x.shape[0]) // CHUNK
    bias_a = widen_bias(relpos_bias(table, name="relpos_bias")[:, :, :BAND_A], A_PREV, nj)
    bias_b = widen_bias(bias_b, B_PREV, nj)
    sinks = jnp.broadcast_to(P["sinks"].reshape(N_HEADS // 2, 2, 1), (N_HEADS // 2, 2, LANES))
    oa, late = attn_fwd(qa, ka, va, bias_a, None, n_prev=A_PREV, name="attn_a", rider=comm.fwd_rider("a"))
    comm.fwd_done("a", late, W, P)
    ob, late = attn_fwd(qb, kb, vb, bias_b, sinks, n_prev=B_PREV, name="attn_b", rider=comm.fwd_rider("b"))
    comm.fwd_done("b", late, W, P)
    Wao = W["attn_out"]
    x1 = matmul(oa, Wao[:DQ], mode="nn", name="attn_out_a", residual=x)
    x1, hf0 = matmul_norm(ob, Wao[DQ:], x1, P["norm_ffn"][0:1], name="attn_out_b")
    g_mix1 = P["norm_mix"][1:2]
    x2, ffn0, h2 = _ffn_fwd(x1, hf0, 0, W, P, g_mix1)

    Ws = W["ssm_in"]
    CC = D_INNER + 2 * SSM_GROUPS * SSM_STATE
    Wz, Wx = Ws[:, :D_INNER], Ws[:, D_INNER:D_INNER + CC]
    Wdt = jnp.pad(Ws[:, D_INNER + CC:], ((0, 0), (0, LANES - SSM_HEADS)))
    z = matmul(h2, Wz, mode="nn", name="ssm_z", out_dtype=BF16)
    xr = matmul(h2, Wx, mode="nn", name="ssm_xbc", out_dtype=BF16)
    dtraw = matmul(h2, Wdt, mode="nn", name="ssm_dt")
    xc, xbc = dwconv_fwd(xr, P["ssm_conv_w"], P["ssm_conv_b"], lambda y: (y, _silu(y)), [], [BF16, F32],
                         name="ssm_conv")
    pad32 = lambda v: jnp.pad(v, ((0, 0), (0, LANES - SSM_HEADS)))
    A = pad32(-jnp.exp(P["ssm_a_log"]))
    dtb = pad32(P["ssm_dt_bias"])
    dexp = jnp.repeat(P["ssm_d"], D_INNER // SSM_HEADS, axis=1)
    y, states = ssd_fwd(xbc, dtraw, dtb, A, dexp, name="ssd_fwd")
    (y2,) = rowwise(f_gate_norm, [y, z], [P["ssm_norm"]], [(D_INNER, BF16)], name="ssm_gate_norm")
    x3, hf1 = matmul_norm(y2, W["ssm_out"], x2, P["norm_ffn"][1:2], name="ssm_out")
    dx4, ffn1, lpart = _ffn_fwd(x3, hf1, 1, W, P, target=tgt)

    dx3, dWfi1, dWfo1, dgf1, dfcw1, dfcb1 = _ffn_bwd(dx4, 1, ffn1, W, P)
    out_f1 = comm.grads("f1", dWfi1, dWfo1)
    dy2 = matmul(dx3, W["ssm_out"], mode="nt", name="ssm_dy")
    dWso = matmul(y2, dx3, mode="tn", name="ssm_dwout")
    dy, dz, dnw = rowwise_vjp(f_gate_norm, [y, z], [P["ssm_norm"]], [dy2], [(0, F32), (1, BF16)], [0],
                              name="ssm_dgate_norm")
    (dxs, dB, dC, ddtraw, dA, ddtb, dDl), sent = ssd_bwd(xbc, dtraw, dtb, A, dexp, states, dy, name="ssd_bwd",
                                                          rider=out_f1)
    comm.bwd_done("f1", sent)
    dxr, dscw, dscb = dwconv_bwd(xr, P["ssm_conv_w"], [xc, (dxs, dB, dC)], lambda c, g: (g * _dsilu(c),), [],
                                 name="ssm_dconv")
    dh2 = matmul(dz, Wz, mode="nt", name="ssm_dh_z")
    dh2 = matmul(dxr, Wx, mode="nt", name="ssm_dh_x", residual=dh2)
    dx2, dgm1 = matmul_dnorm(ddtraw, Wdt, dh2, x2, g_mix1, dx3, name="ssm_dh_dt")
    dWs = jnp.concatenate([matmul(h2, dz, mode="tn", name="ssm_dw_z"),
                           matmul(h2, dxr, mode="tn", name="ssm_dw_x"),
                           matmul(h2, ddtraw, mode="tn", name="ssm_dw_dt")[:, :SSM_HEADS]], axis=1)
    out_s = comm.grads("s", dWs, dWso)

    dx1, dWfi0, dWfo0, dgf0, dfcw0, dfcb0 = _ffn_bwd(dx2, 0, ffn0, W, P)
    out_f0 = comm.grads("f0", dWfi0, dWfo0)
    doa = matmul(dx1, Wao[:DQ], mode="nt", name="attn_do_a", out_dtype=BF16)
    dob = matmul(dx1, Wao[DQ:], mode="nt", name="attn_do_b", out_dtype=BF16)
    dWao = jnp.concatenate([matmul(oa, dx1, mode="tn", name="attn_dwout_a"),
                            matmul(ob, dx1, mode="tn", name="attn_dwout_b")], axis=0)
    (dqa, dka, dva, dbias_a), sent = attn_bwd(qa, ka, va, doa, bias_a, None, n_prev=A_PREV, name="attn_a_bwd",
                                              rider=out_s)
    comm.bwd_done("s", sent)
    (dqb, dkb, dvb, _, dsk), sent = attn_bwd(qb, kb, vb, dob, bias_b, sinks, n_prev=B_PREV, name="attn_b_bwd",
                                             rider=out_f0)
    comm.bwd_done("f0", sent)
    dqkv, dgqa, dgka, dgqb, dgkb = rowwise_vjp(f_qknorm, [qkv], qk_par, [dqa, dka, dva, dqb, dkb, dvb],
                                               [(0, BF16)], [0, 1, 2, 3], name="attn_dqknorm", cot_skip=qk_lead)
    dx, dgm0 = matmul_dnorm(dqkv, W["attn_in"], None, x, g_mix0, dx1, name="attn_dh")
    dWai = matmul(h0, dqkv, mode="tn", name="attn_dwin")
    dbias_a = fold_bias(dbias_a, A_PREV, nj)
    dbias_rev = jnp.pad(dbias_a[:, ::-1, :], ((0, 0), (0, 0), (0, REL_W - BAND_A)))
    dtable = relpos_grad(dbias_rev, name="relpos_grad")[:, :2 * MAX_REL + 1]

    gW = {"attn_in": dWai, "attn_out": dWao, "ssm_in": dWs, "ssm_out": dWso,
          "ffn_in": [dWfi0, dWfi1], "ffn_out": [dWfo0, dWfo1]}
    gP = {"norm_mix": jnp.concatenate([dgm0, dgm1], axis=0),
          "norm_ffn": jnp.concatenate([dgf0, dgf1], axis=0),
          "relpos_table": dtable, "q_norm_a": dgqa, "k_norm_a": dgka, "q_norm_b": dgqb, "k_norm_b": dgkb,
          "sinks": dsk[:, :, 0].reshape(1, N_HEADS),
          "ssm_conv_w": dscw, "ssm_conv_b": dscb,
          "ssm_dt_bias": ddtb[:, :SSM_HEADS], "ssm_a_log": dA[:, :SSM_HEADS] * A[:, :SSM_HEADS],
          "ssm_d": dDl.reshape(SSM_HEADS, D_INNER // SSM_HEADS).sum(axis=1).reshape(1, SSM_HEADS),
          "ssm_norm": dnw,
          "ffn_conv_w": jnp.stack([dfcw0, dfcw1]), "ffn_conv_b": jnp.concatenate([dfcb0, dfcb1], axis=0)}
    return lpart, dx, gW, gP


WEIGHTS = ["norm_mix", "norm_ffn", "attn_w_in", "attn_w_out", "relpos_table", "q_norm_a", "k_norm_a", "q_norm_b",
           "k_norm_b", "sinks", "ssm_w_in", "ssm_conv_w", "ssm_conv_b", "ssm_dt_bias", "ssm_a_log", "ssm_d",
           "ssm_norm", "ssm_w_out", "ffn_w_in", "ffn_conv_w", "ffn_conv_b", "ffn_w_out"]
ARGS = ["x"] + WEIGHTS + ["loss_target"] + ["m_" + w for w in WEIGHTS] + ["v_" + w for w in WEIGHTS]
N_CHIPS = 4
SMALL_ROWS = 384
SMALL_ORDER = ["norm_mix", "norm_ffn", "relpos_table", "q_norm_a", "k_norm_a", "q_norm_b", "k_norm_b", "sinks",
               "ssm_dt_bias", "ssm_a_log", "ssm_d", "ffn_conv_b", "ssm_conv_w", "ssm_conv_b", "ssm_norm", "ffn_conv_w"]


def _cols_to_slabs(g):
    K, N = g.shape
    return g.reshape(2, K // 2, N_CHIPS, N // N_CHIPS).transpose(0, 2, 1, 3)


def _rows_to_slabs(g):
    R, C = g.shape
    return g.reshape(N_CHIPS, 2, R // (2 * N_CHIPS), C).transpose(1, 0, 2, 3)


class MeshComm:
    def __init__(self, d, xi, yi, ci):
        self.d, self.ci, self.me = d, ci, 2 * xi + yi
        self.csel = jnp.full((1, LANES), ci, F32)
        self.msel = jnp.full((1, LANES), self.me, F32)
        halves = lambda w: w.reshape((2, -1, w.shape[-1]))
        small = jnp.concatenate([d[k].reshape(-1) for k in ("ssm_conv_w", "ssm_conv_b", "ssm_norm", "ffn_conv_w")])
        small = jnp.pad(small, (0, 2 * 40 * LANES - small.shape[0])).reshape(2, 40, LANES)
        self.shards = {"attn": [halves(d["attn_w_in"][0].astype(BF16)), halves(d["attn_w_out"][0].astype(BF16))],
                       "a": [d["ffn_w_in"].astype(BF16), small],
                       "b": [d["ffn_w_out"].astype(BF16), halves(d["ssm_w_in"][0].astype(BF16)),
                             halves(d["ssm_w_out"][0].astype(BF16))]}
        self.p32, self.mine = {}, {}

    def _whole(self, tag, outs):
        return [lax.dynamic_update_slice_in_dim(g, s[None], self.me, axis=0) for g, s in zip(outs, self.shards[tag])]

    @staticmethod
    def _cat_cols(g):
        return jnp.concatenate([g[j].reshape((-1, g.shape[-1])) for j in range(N_CHIPS)], axis=1)

    def first_weights(self):
        g_ai, g_ao = self._whole("attn", run_rider(gather_rider(self.shards["attn"]), name="gather_attn"))
        return {"attn_in": self._cat_cols(g_ai), "attn_out": g_ao.reshape(-1, D_MODEL)}

    def fwd_rider(self, tag):
        return gather_rider(self.shards[tag])

    def fwd_done(self, tag, outs, W, P):
        if tag == "b":
            g_fo, g_si, g_so = self._whole("b", outs)
            W["ffn_out"] = [g_fo[:, l].reshape(-1, D_MODEL) for l in range(2)]
            W["ssm_in"], W["ssm_out"] = self._cat_cols(g_si), g_so.reshape(-1, D_MODEL)
            return
        g_fi, g_sm = self._whole("a", outs)
        W["ffn_in"] = [jnp.concatenate([g_fi[j, l] for j in range(N_CHIPS)], axis=1) for l in range(2)]
        sm = g_sm.reshape(N_CHIPS, -1)
        CC = D_INNER + 2 * SSM_GROUPS * SSM_STATE
        c4, f4 = CC // N_CHIPS, D_FF // N_CHIPS
        o1 = SSM_CONV * c4
        o2 = o1 + c4
        o3 = o2 + D_INNER // N_CHIPS
        o4 = o3 + 2 * FFN_CONV * f4
        P["ssm_conv_w"] = sm[:, :o1].reshape(N_CHIPS, SSM_CONV, c4).transpose(1, 0, 2).reshape(SSM_CONV, CC)
        P["ssm_conv_b"] = sm[:, o1:o2].reshape(1, CC)
        P["ssm_norm"] = sm[:, o2:o3].reshape(1, D_INNER)
        P["ffn_conv_w"] = sm[:, o3:o4].reshape(N_CHIPS, 2, FFN_CONV, f4).transpose(1, 2, 0, 3).reshape(2, FFN_CONV, D_FF)

    def grads(self, tag, cols, rows):
        slabs = [_cols_to_slabs(cols), _rows_to_slabs(rows)]
        from_sib = pair_swap_halves(slabs, name="grad_pair_swap_" + tag)
        pairs = [pair_add(g, r, self.csel, name=f"grad_pair_add_{tag}{i}") for i, (g, r) in enumerate(zip(slabs, from_sib))]
        self.p32[tag] = [p[0] for p in pairs]
        return scatter_rider([p[1] for p in pairs])

    def bwd_done(self, tag, outs):
        self.mine[tag] = [chip_add(p, r, self.msel, name=f"grad_chip_add_{tag}{i}")
                          for i, (p, r) in enumerate(zip(self.p32[tag], outs))]

    def finish(self, d_attn_in, d_attn_out):
        self.bwd_done("at", run_rider(self.grads("at", d_attn_in, d_attn_out), name="grad_scatter_at"))
        order = ["at", "s", "f0", "f1"]
        mine = [m for t in order for m in self.mine[t]]
        theirs = pair_share(mine, name="grad_pair_share")
        full = [jnp.where(self.ci == 0, jnp.stack([a, b]), jnp.stack([b, a])).reshape((-1, a.shape[-1]))
                for a, b in zip(mine, theirs)]
        ai, ao, si, so, fi0, fo0, fi1, fo1 = full
        return {"attn_w_in": ai[None], "attn_w_out": ao[None], "ssm_w_in": si[None], "ssm_w_out": so[None],
                "ffn_w_in": jnp.stack([fi0, fi1]), "ffn_w_out": jnp.stack([fo0, fo1])}


def _adamw(w, g, m, v, name):
    shp = w.shape
    two = lambda a: a.reshape((-1, shp[-1]))
    outs = [(shp[-1], F32)] * 3
    d, nm, nv = rowwise(f_adamw, [two(w), two(g), two(m), two(v)], [], outs, name="adamw_" + name)
    return d.reshape(shp), nm.reshape(shp), nv.reshape(shp)


def kernel(x, norm_mix, norm_ffn, attn_w_in, attn_w_out, relpos_table, q_norm_a, k_norm_a, q_norm_b, k_norm_b, sinks, ssm_w_in, ssm_conv_w, ssm_conv_b, ssm_dt_bias, ssm_a_log, ssm_d, ssm_norm, ssm_w_out, ffn_w_in, ffn_conv_w, ffn_conv_b, ffn_w_out, loss_target, m_norm_mix, m_norm_ffn, m_attn_w_in, m_attn_w_out, m_relpos_table, m_q_norm_a, m_k_norm_a, m_q_norm_b, m_k_norm_b, m_sinks, m_ssm_w_in, m_ssm_conv_w, m_ssm_conv_b, m_ssm_dt_bias, m_ssm_a_log, m_ssm_d, m_ssm_norm, m_ssm_w_out, m_ffn_w_in, m_ffn_conv_w, m_ffn_conv_b, m_ffn_w_out, v_norm_mix, v_norm_ffn, v_attn_w_in, v_attn_w_out, v_relpos_table, v_q_norm_a, v_k_norm_a, v_q_norm_b, v_k_norm_b, v_sinks, v_ssm_w_in, v_ssm_conv_w, v_ssm_conv_b, v_ssm_dt_bias, v_ssm_a_log, v_ssm_d, v_ssm_norm, v_ssm_w_out, v_ffn_w_in, v_ffn_conv_w, v_ffn_conv_b, v_ffn_w_out):
    d = dict(zip(ARGS, (x, norm_mix, norm_ffn, attn_w_in, attn_w_out, relpos_table, q_norm_a, k_norm_a, q_norm_b, k_norm_b, sinks, ssm_w_in, ssm_conv_w, ssm_conv_b, ssm_dt_bias, ssm_a_log, ssm_d, ssm_norm, ssm_w_out, ffn_w_in, ffn_conv_w, ffn_conv_b, ffn_w_out, loss_target, m_norm_mix, m_norm_ffn, m_attn_w_in, m_attn_w_out, m_relpos_table, m_q_norm_a, m_k_norm_a, m_q_norm_b, m_k_norm_b, m_sinks, m_ssm_w_in, m_ssm_conv_w, m_ssm_conv_b, m_ssm_dt_bias, m_ssm_a_log, m_ssm_d, m_ssm_norm, m_ssm_w_out, m_ffn_w_in, m_ffn_conv_w, m_ffn_conv_b, m_ffn_w_out, v_norm_mix, v_norm_ffn, v_attn_w_in, v_attn_w_out, v_relpos_table, v_q_norm_a, v_k_norm_a, v_q_norm_b, v_k_norm_b, v_sinks, v_ssm_w_in, v_ssm_conv_w, v_ssm_conv_b, v_ssm_dt_bias, v_ssm_a_log, v_ssm_d, v_ssm_norm, v_ssm_w_out, v_ffn_w_in, v_ffn_conv_w, v_ffn_conv_b, v_ffn_w_out)))
    xi, yi, ci = _pos()
    me = 2 * xi + yi
    CC = D_INNER + 2 * SSM_GROUPS * SSM_STATE
    c4, f4 = CC // N_CHIPS, D_FF // N_CHIPS

    P = {k: d[k] for k in ["norm_mix", "norm_ffn", "q_norm_a", "k_norm_a", "q_norm_b", "k_norm_b", "sinks",
                           "ssm_dt_bias", "ssm_a_log", "ssm_d", "ffn_conv_b"]}
    P["relpos_table"] = d["relpos_table"][0]
    comm = MeshComm(d, xi, yi, ci)
    W = comm.first_weights()
    lpart, dx, gW, gP = local_step(d["x"][0], d["loss_target"][0], W, P, comm)
    loss = lax.psum(lpart[0, 0], ("x", "y", "c"))
    grads = comm.finish(gW["attn_in"], gW["attn_out"])

    flat = jnp.concatenate([gP[k].reshape(-1) for k in SMALL_ORDER])
    flat = jnp.pad(flat, (0, SMALL_ROWS * LANES - flat.shape[0])).reshape(SMALL_ROWS, LANES)
    tot = sum_slots(gather_all(flat, name="small_gather"), name="small_sum").reshape(-1)
    off = 0
    for k in SMALL_ORDER:
        n = int(np.prod(gP[k].shape))
        g = tot[off:off + n].reshape(gP[k].shape)
        off += n
        if k == "ssm_conv_w":
            g = lax.dynamic_slice_in_dim(g, me * c4, c4, axis=1)[None]
        elif k == "ssm_conv_b":
            g = lax.dynamic_slice_in_dim(g, me * c4, c4, axis=1)
        elif k == "ssm_norm":
            g = lax.dynamic_slice_in_dim(g, me * (D_INNER // N_CHIPS), D_INNER // N_CHIPS, axis=1)
        elif k == "ffn_conv_w":
            g = lax.dynamic_slice_in_dim(g, me * f4, f4, axis=2)
        elif k == "relpos_table":
            g = g[None]
        grads[k] = g

    deltas, new_m, new_v = {}, {}, {}
    for k in WEIGHTS:
        deltas[k], new_m[k], new_v[k] = _adamw(d[k], grads[k], d["m_" + k], d["v_" + k], k)
    return (loss, dx[None], *[grads[k] for k in WEIGHTS], *[deltas[k] for k in WEIGHTS],
            *[new_m[k] for k in WEIGHTS], *[new_v[k] for k in WEIGHTS])
```

```python
import functools

import numpy as np
import jax
import jax.numpy as jnp
from jax import lax
from jax.experimental import pallas as pl
from jax.experimental.pallas import tpu as pltpu

F32 = jnp.float32
BF16 = jnp.bfloat16
HI = lax.Precision.HIGHEST

D_MODEL = 1024
CHUNK = 64
EPS = 1e-6
HEAD_DIM = 64
N_HEADS = 8
A_PREV = 8
B_PREV = 2
MAX_REL = 256
D_INNER = 2048
SSM_HEADS = 32
SSM_GROUPS = 4
SSM_STATE = 128
SSM_CONV = 4
D_FF = 2816
FFN_CONV = 3
LANES = 128
SUBLANES = 8
VMEM_LIMIT = 56 * 1024 * 1024
SSD_L = 128

ADAM_LR = 0.001
ADAM_B1 = 0.9
ADAM_B2 = 0.999
ADAM_EPS = 1e-08
ADAM_WD = 0.01
ADAM_STEP = 10

MESH = pl.DeviceIdType.MESH


def _params(*sem):
    return pltpu.CompilerParams(dimension_semantics=sem, vmem_limit_bytes=VMEM_LIMIT)


def _pick(n, want):
    if n <= want:
        return n
    t = (want // LANES) * LANES
    while t >= LANES:
        if n % t == 0:
            return t
        t -= LANES
    return n


MM_ROWS = 512
MM_COLS = 1536
MM_RED = 2048
MM_SHORT = 1024


def matmul(a, b, *, mode, name, out_dtype=F32, residual=None):
    dims = {"nn": (((1,), (0,)), ((), ())), "nt": (((1,), (1,)), ((), ())), "tn": (((0,), (0,)), ((), ()))}[mode]
    if mode == "tn":
        assert residual is None and out_dtype == F32
        (K, M), (K2, N) = a.shape, b.shape
        assert K == K2, (a.shape, b.shape)
        tm, tn, tk = _pick(M, MM_COLS), _pick(N, MM_COLS), _pick(K, MM_RED)

        def body(a_ref, b_ref, o_ref):
            k = pl.program_id(2)
            p = lax.dot_general(a_ref[...].astype(BF16), b_ref[...].astype(BF16), dims, preferred_element_type=F32)

            @pl.when(k == 0)
            def _():
                o_ref[...] = p

            @pl.when(k != 0)
            def _():
                o_ref[...] += p

        return pl.pallas_call(
            body, name=name, grid=(M // tm, N // tn, K // tk),
            in_specs=[pl.BlockSpec((tk, tm), lambda i, j, k: (k, i)), pl.BlockSpec((tk, tn), lambda i, j, k: (k, j))],
            out_specs=pl.BlockSpec((tm, tn), lambda i, j, k: (i, j)),
            out_shape=jax.ShapeDtypeStruct((M, N), F32),
            compiler_params=_params("parallel", "parallel", "arbitrary"),
        )(a, b)

    if mode == "nn":
        (M, K), (K2, N) = a.shape, b.shape
    else:
        (M, K), (N, K2) = a.shape, b.shape
    assert K == K2, (a.shape, b.shape, mode)
    tm, tn = _pick(M, MM_ROWS if K > MM_SHORT else 2 * MM_ROWS), _pick(N, MM_COLS)

    def body(*refs):
        a_ref, b_ref = refs[:2]
        o_ref = refs[-1]
        r = lax.dot_general(a_ref[...].astype(BF16), b_ref[...].astype(BF16), dims, preferred_element_type=F32)
        if residual is not None:
            r = r + refs[2][...].astype(F32)
        o_ref[...] = r.astype(o_ref.dtype)

    a_spec = pl.BlockSpec((tm, K), lambda j, i: (i, 0))
    b_spec = pl.BlockSpec((K, tn), lambda j, i: (0, j)) if mode == "nn" else pl.BlockSpec((tn, K), lambda j, i: (j, 0))
    o_spec = pl.BlockSpec((tm, tn), lambda j, i: (i, j))
    in_specs = [a_spec, b_spec] + ([o_spec] if residual is not None else [])
    args = (a, b) + ((residual,) if residual is not None else ())
    return pl.pallas_call(
        body, name=name, grid=(N // tn, M // tm),
        in_specs=in_specs, out_specs=o_spec,
        out_shape=jax.ShapeDtypeStruct((M, N), out_dtype),
        compiler_params=_params("parallel", "parallel"),
    )(*args)


def matmul_norm(a, b, residual, g, *, name):
    (M, K), (_, N) = a.shape, b.shape
    tm = _pick(M, MM_ROWS)

    def body(a_ref, b_ref, r_ref, g_ref, x_ref, h_ref):
        x = jnp.dot(a_ref[...].astype(BF16), b_ref[...].astype(BF16), preferred_element_type=F32) + r_ref[...]
        x_ref[...] = x
        h_ref[...] = f_rmsnorm(x, g_ref[...])[0].astype(BF16)

    row = pl.BlockSpec((tm, N), lambda i: (i, 0))
    return pl.pallas_call(
        body, name=name, grid=(M // tm,),
        in_specs=[pl.BlockSpec((tm, K), lambda i: (i, 0)), pl.BlockSpec((K, N), lambda i: (0, 0)), row,
                  pl.BlockSpec((1, N), lambda i: (0, 0))],
        out_specs=[row, row],
        out_shape=[jax.ShapeDtypeStruct((M, N), F32), jax.ShapeDtypeStruct((M, N), BF16)],
        compiler_params=_params("parallel"),
    )(a, b, residual, g)


def matmul_dnorm(a, b, partial, x, g, dres, *, name):
    (M, K), (N, _) = a.shape, b.shape
    tm = _pick(M, MM_ROWS)
    has_part = partial is not None

    def body(*refs):
        a_ref, b_ref = refs[:2]
        x_ref, g_ref, dres_ref, dx_ref, dg_ref = refs[-5:]
        dh = lax.dot_general(a_ref[...].astype(BF16), b_ref[...].astype(BF16), _NT, preferred_element_type=F32)
        if has_part:
            dh = dh + refs[2][...]
        xv = x_ref[...]
        r = lax.rsqrt(jnp.mean(xv * xv, axis=-1, keepdims=True) + EPS)
        xhat = xv * r
        dxh = dh * g_ref[...]
        dx_ref[...] = dres_ref[...] + r * (dxh - xhat * jnp.mean(dxh * xhat, axis=-1, keepdims=True))
        dg = jnp.sum(dh * xhat, axis=0, keepdims=True)

        @pl.when(pl.program_id(0) == 0)
        def _():
            dg_ref[...] = dg

        @pl.when(pl.program_id(0) != 0)
        def _():
            dg_ref[...] += dg

    row = pl.BlockSpec((tm, N), lambda i: (i, 0))
    vec = pl.BlockSpec((1, N), lambda i: (0, 0))
    in_specs = [pl.BlockSpec((tm, K), lambda i: (i, 0)), pl.BlockSpec((N, K), lambda i: (0, 0))]
    args = [a, b]
    if has_part:
        in_specs.append(row)
        args.append(partial)
    return pl.pallas_call(
        body, name=name, grid=(M // tm,), in_specs=in_specs + [row, vec, row], out_specs=[row, vec],
        out_shape=[jax.ShapeDtypeStruct((M, N), F32), jax.ShapeDtypeStruct((1, N), F32)],
        compiler_params=_params("arbitrary"),
    )(*args, x, g, dres)


def rowwise(f, rows, params, outs, *, name, tm=256, lead=None):
    S = rows[0].shape[0]
    tm = _row_tile(S, tm)
    nr, npar = len(rows), len(params)
    lead = [0] * len(outs) if lead is None else lead
    assert all(ld % tm == 0 for ld in lead)
    padded = [k for k, ld in enumerate(lead) if ld]

    def body(*refs):
        vals = [r[...].astype(F32) for r in refs[:nr + npar]]
        res = f(*vals)
        for o_ref, r in zip(refs[nr + npar + len(padded):], res):
            o_ref[...] = r.astype(o_ref.dtype)

    in_specs = [pl.BlockSpec((tm, r.shape[1]), lambda i: (i, 0)) for r in rows]
    in_specs += [pl.BlockSpec(p.shape, lambda i: (0, 0)) for p in params]
    in_specs += [pl.BlockSpec(memory_space=pl.ANY)] * len(padded)
    zeros = [jnp.zeros((S + lead[k], outs[k][0]), outs[k][1]) for k in padded]
    out_specs = [pl.BlockSpec((tm, c), lambda i, s=ld // tm: (i + s, 0)) for (c, _), ld in zip(outs, lead)]
    out_shape = [jax.ShapeDtypeStruct((S + ld, c), dt) for (c, dt), ld in zip(outs, lead)]
    return pl.pallas_call(body, name=name, grid=(S // tm,), in_specs=in_specs, out_specs=out_specs,
                          out_shape=out_shape, input_output_aliases={nr + npar + n: k for n, k in enumerate(padded)},
                          compiler_params=_params("parallel"))(*rows, *params, *zeros)


def rowwise_vjp(f, rows, params, cots, drow, dpar, *, name, tm=256, cot_skip=None):
    S = rows[0].shape[0]
    tm = _row_tile(S, tm)
    nr, npar, nc = len(rows), len(params), len(cots)
    skip = [0] * nc if cot_skip is None else [s // tm for s in cot_skip]
    assert cot_skip is None or all(s % tm == 0 for s in cot_skip)

    def body(*refs):
        vals = [r[...].astype(F32) for r in refs[:nr + npar]]
        cvals = [r[...].astype(F32) for r in refs[nr + npar:nr + npar + nc]]
        o_refs = refs[nr + npar + nc:]
        want = [ri for ri, _ in drow] + [nr + pi for pi in dpar]

        def f_want(*d):
            full = list(vals)
            for k, v in zip(want, d):
                full[k] = v
            return f(*full)

        _, vjp = jax.vjp(f_want, *[vals[k] for k in want])
        grads = vjp(tuple(cvals))
        for o_ref, g in zip(o_refs[:len(drow)], grads):
            o_ref[...] = g.astype(o_ref.dtype)
        first = pl.program_id(0) == 0
        for o_ref, g in zip(o_refs[len(drow):], grads[len(drow):]):
            g = g.astype(F32)

            @pl.when(first)
            def _(o_ref=o_ref, g=g):
                o_ref[...] = g

            @pl.when(jnp.logical_not(first))
            def _(o_ref=o_ref, g=g):
                o_ref[...] += g

    in_specs = [pl.BlockSpec((tm, r.shape[1]), lambda i: (i, 0)) for r in rows]
    in_specs += [pl.BlockSpec(p.shape, lambda i: (0, 0)) for p in params]
    in_specs += [pl.BlockSpec((tm, c.shape[1]), lambda i, s=s: (i + s, 0)) for c, s in zip(cots, skip)]
    out_specs = [pl.BlockSpec((tm, rows[ri].shape[1]), lambda i: (i, 0)) for ri, _ in drow]
    out_specs += [pl.BlockSpec(params[pi].shape, lambda i: (0, 0)) for pi in dpar]
    out_shape = [jax.ShapeDtypeStruct(rows[ri].shape, dt) for ri, dt in drow]
    out_shape += [jax.ShapeDtypeStruct(params[pi].shape, F32) for pi in dpar]
    return pl.pallas_call(body, name=name, grid=(S // tm,), in_specs=in_specs, out_specs=out_specs,
                          out_shape=out_shape, compiler_params=_params("arbitrary"))(*rows, *params, *cots)


HALO = 2 * SUBLANES
CONV_ROWS = 64


def dwconv_fwd(x, w, b, post, extra, outs, *, name, tm=256):
    S, C = x.shape
    K = w.shape[0]
    tm = min(tm, S)
    hb = tm // HALO
    ne = len(extra)

    def body(*refs):
        x_ref, halo_ref, w_ref, b_ref = refs[:4]
        e_refs = refs[4:4 + ne]
        o_refs = refs[4 + ne:4 + ne + len(outs)]
        buf = refs[-1]
        i = pl.program_id(0)
        buf[0:HALO, :] = jnp.where(i == 0, 0.0, halo_ref[...].astype(F32))
        buf[HALO:HALO + tm, :] = x_ref[...].astype(F32)
        for c0 in range(0, C, LANES):
            cs = slice(c0, c0 + LANES)
            for r0 in range(0, tm, CONV_ROWS):
                rs = slice(r0, r0 + CONV_ROWS)
                acc = jnp.broadcast_to(b_ref[:, cs], (CONV_ROWS, LANES))
                for k in range(K):
                    acc = acc + w_ref[k:k + 1, cs] * buf[pl.ds(HALO - (K - 1) + k + r0, CONV_ROWS), cs]
                for o_ref, r in zip(o_refs, post(acc, *[e[rs, cs].astype(F32) for e in e_refs])):
                    o_ref[rs, cs] = r.astype(o_ref.dtype)

    row = pl.BlockSpec((tm, C), lambda i: (i, 0))
    return pl.pallas_call(
        body, name=name, grid=(S // tm,),
        in_specs=[row,
                  pl.BlockSpec((HALO, C), lambda i: (jnp.maximum(i * hb - 1, 0), 0)),
                  pl.BlockSpec((K, C), lambda i: (0, 0)),
                  pl.BlockSpec((1, C), lambda i: (0, 0))] + [row] * ne,
        out_specs=[row] * len(outs),
        out_shape=[jax.ShapeDtypeStruct((S, C), dt) for dt in outs],
        scratch_shapes=[pltpu.VMEM((HALO + tm, C), F32)],
        compiler_params=_params("parallel"),
    )(x, x, w, b, *extra)


def dwconv_bwd(x, w, srcs, dy_fn, extra_outs, *, name, tm=256):
    S, C = x.shape
    K = w.shape[0]
    tm = min(tm, S)
    hb = tm // HALO
    n = S // tm
    groups = [s if isinstance(s, tuple) else (s,) for s in srcs]
    flat = [a for g in groups for a in g]
    nf = len(flat)

    def body(*refs):
        x_ref, xh_ref, w_ref = refs[:3]
        dx_ref, dw_ref, db_ref = refs[3 + 2 * nf:6 + 2 * nf]
        e_refs = refs[6 + 2 * nf:6 + 2 * nf + len(extra_outs)]
        bx, bd = refs[-2:]

        def strips(first, c0, rs):
            out, at = [], first
            for g in groups:
                off = 0
                for a in g:
                    if off <= c0 < off + a.shape[1]:
                        out.append(refs[at][rs, c0 - off:c0 - off + LANES].astype(F32))
                    off += a.shape[1]
                    at += 1
            return out

        i = pl.program_id(0)
        bx[0:HALO, :] = jnp.where(i == 0, 0.0, xh_ref[...].astype(F32))
        bx[HALO:HALO + tm, :] = x_ref[...].astype(F32)

        @pl.when(i == 0)
        def _():
            dw_ref[...] = jnp.zeros_like(dw_ref)
            db_ref[...] = jnp.zeros_like(db_ref)

        for c0 in range(0, C, LANES):
            cs = slice(c0, c0 + LANES)
            dws = [jnp.zeros((1, LANES), F32) for _ in range(K)]
            dbs = jnp.zeros((1, LANES), F32)
            for r0 in range(0, tm, CONV_ROWS):
                rs = slice(r0, r0 + CONV_ROWS)
                res = dy_fn(*strips(3, c0, rs))
                dyv = res[0]
                for e_ref, r in zip(e_refs, res[1:]):
                    e_ref[rs, cs] = r.astype(e_ref.dtype)
                bd[rs, cs] = dyv
                for k in range(K):
                    dws[k] = dws[k] + jnp.sum(dyv * bx[pl.ds(HALO - (K - 1) + k + r0, CONV_ROWS), cs], axis=0,
                                              keepdims=True)
                dbs = dbs + jnp.sum(dyv, axis=0, keepdims=True)
            bd[tm:tm + HALO, cs] = jnp.where(i == n - 1, 0.0, dy_fn(*strips(3 + nf, c0, slice(None)))[0])
            for k in range(K):
                dw_ref[k:k + 1, cs] += dws[k]
            db_ref[:, cs] += dbs
            for r0 in range(0, tm, CONV_ROWS):
                acc = jnp.zeros((CONV_ROWS, LANES), F32)
                for k in range(K):
                    acc = acc + w_ref[k:k + 1, cs] * bd[pl.ds((K - 1) - k + r0, CONV_ROWS), cs]
                dx_ref[r0:r0 + CONV_ROWS, cs] = acc.astype(dx_ref.dtype)

    row = lambda c: pl.BlockSpec((tm, c), lambda i: (i, 0))
    nxt = lambda c: pl.BlockSpec((HALO, c), lambda i: (jnp.minimum((i + 1) * hb, S // HALO - 1), 0))
    return pl.pallas_call(
        body, name=name, grid=(n,),
        in_specs=[row(C), pl.BlockSpec((HALO, C), lambda i: (jnp.maximum(i * hb - 1, 0), 0)),
                  pl.BlockSpec((K, C), lambda i: (0, 0))]
                 + [row(a.shape[1]) for a in flat] + [nxt(a.shape[1]) for a in flat],
        out_specs=[row(C), pl.BlockSpec((K, C), lambda i: (0, 0)), pl.BlockSpec((1, C), lambda i: (0, 0))]
                  + [row(C)] * len(extra_outs),
        out_shape=[jax.ShapeDtypeStruct((S, C), BF16), jax.ShapeDtypeStruct((K, C), F32),
                   jax.ShapeDtypeStruct((1, C), F32)] + [jax.ShapeDtypeStruct((S, C), dt) for dt in extra_outs],
        scratch_shapes=[pltpu.VMEM((HALO + tm, C), F32), pltpu.VMEM((tm + HALO, C), F32)],
        compiler_params=_params("arbitrary"),
    )(x, x, w, *flat, *flat)


def _sigmoid(x):
    return 0.5 * jnp.tanh(0.5 * x) + 0.5


def _silu(x):
    return x * _sigmoid(x)


def _dsilu(x):
    s = _sigmoid(x)
    return s * (1.0 + x * (1.0 - s))


def f_rmsnorm(x, g):
    return (x * lax.rsqrt(jnp.mean(x * x, axis=-1, keepdims=True) + EPS) * g,)


SEL = lax.Precision.HIGH


def _group_norm(x, bd, width):
    ms = jnp.dot(x * x, bd, precision=SEL, preferred_element_type=F32) * (1.0 / width)
    return x * lax.rsqrt(ms + EPS)


def f_qknorm(qkv, gqa, gka, gqb, gkb, bd512, bd128, fold, expand):
    dq = N_HEADS * HEAD_DIM
    qa, ka, va, qb = (qkv[:, i * dq:(i + 1) * dq] for i in range(4))
    kb = qkv[:, 4 * dq:4 * dq + LANES]
    vb = qkv[:, 4 * dq + LANES:4 * dq + 2 * LANES]
    tile8 = lambda g: jnp.dot(g, fold, precision=HI, preferred_element_type=F32)
    qa = _group_norm(qa, bd512, HEAD_DIM) * tile8(gqa)
    ka = _group_norm(ka, bd512, HEAD_DIM) * tile8(gka)
    qb = _group_norm(qb, bd512, HEAD_DIM) * tile8(gqb)
    kb = _group_norm(kb, bd128, HEAD_DIM) * tile8(gkb)[:, :LANES]
    kb = jnp.dot(kb, expand, precision=SEL, preferred_element_type=F32)
    vb = jnp.dot(vb, expand, precision=SEL, preferred_element_type=F32)
    return qa, ka, va, qb, kb, vb


def f_gate_norm(y, z, nw):
    v = y * _silu(z)
    gw = D_INNER // SSM_GROUPS
    parts = []
    for g in range(SSM_GROUPS):
        vg = v[:, g * gw:(g + 1) * gw]
        parts.append(vg * lax.rsqrt(jnp.mean(vg * vg, axis=-1, keepdims=True) + EPS))
    return (jnp.concatenate(parts, axis=-1) * nw,)


ATT_TQ = 256
QK_ROWS = 512
_NT =(((1,), (1,)), ((), ()))
_TN = (((0,), (0,)), ((), ()))


def _stack_heads(t, head0):
    return jnp.concatenate([jnp.where(head0, t, 0.0), jnp.where(head0, 0.0, t)], axis=0).astype(BF16)


def _attn_probs(qk, bias, valid, snk):
    s = qk * (HEAD_DIM ** -0.5) + bias
    s = jnp.where(valid, s, -jnp.inf)
    m = jnp.max(s, axis=1, keepdims=True)
    if snk is not None:
        m = jnp.maximum(m, snk)
    e = jnp.exp(s - m)
    den = jnp.sum(e, axis=1, keepdims=True)
    if snk is None:
        return e / den, None
    es = jnp.exp(snk - m)
    den = den + es
    return e / den, es / den


def widen_bias(bias, n_prev, nj):
    band = (n_prev + 1) * CHUNK
    wk = (nj + n_prev) * CHUNK
    rows = [jnp.pad(bias, ((0, 0), (0, 0), (j * CHUNK, wk - band - j * CHUNK)), constant_values=-jnp.inf)
            for j in range(nj)]
    return jnp.concatenate(rows, axis=1)


def fold_bias(dbw, n_prev, nj):
    band = (n_prev + 1) * CHUNK
    acc = dbw[:, :CHUNK, :band]
    for j in range(1, nj):
        acc = acc + dbw[:, j * CHUNK:(j + 1) * CHUNK, j * CHUNK:j * CHUNK + band]
    return acc


def attn_fwd(q, k, v, bias_w, sinks, *, n_prev, name, rider=None):
    S = q.shape[0]
    pad = n_prev * CHUNK
    kv_rows = k.shape[0]
    lead = kv_rows - S - pad
    tq = min(ATT_TQ, S)
    wk = tq + pad
    assert bias_w.shape == (N_HEADS, tq, wk), bias_w.shape
    has_sink = sinks is not None

    r_in, r_out, r_shapes, r_sems, r_args = _rider_parts(rider)
    n_own = 5 if has_sink else 4
    n_p, n_i = N_HEADS // 2, S // tq

    def body(*refs):
        q_ref, k_ref, v_ref, bias_ref = refs[:4]
        sink_ref = refs[4] if has_sink else None
        o_ref = refs[n_own + len(r_in)]
        if rider is not None:
            p_id, i_id = pl.program_id(0), pl.program_id(1)
            _ride(rider, refs[n_own:n_own + len(r_in)], refs[n_own + len(r_in) + 1:n_own + len(r_in) + 1 + len(r_out)],
                  refs[n_own + len(r_in) + 1 + len(r_out):],
                  jnp.logical_and(p_id == 0, i_id == 0), jnp.logical_and(p_id == n_p - 1, i_id == 0),
                  jnp.logical_and(p_id == n_p - 1, i_id == n_i - 1))
        start = pl.multiple_of(pl.program_id(1) * tq, tq)
        head0 = lax.broadcasted_iota(jnp.int32, (1, LANES), 1) < HEAD_DIM
        valid = lax.broadcasted_iota(jnp.int32, (1, wk), 1) + start >= pad
        kb = k_ref[pl.ds(pl.multiple_of(start + lead, CHUNK), wk), :]
        vb = v_ref[pl.ds(pl.multiple_of(start + lead, CHUNK), wk), :]
        qk = lax.dot_general(_stack_heads(q_ref[...].astype(F32), head0), kb, _NT, preferred_element_type=F32)
        ps = []
        for r in range(2):
            snk = sink_ref[0, r:r + 1, 0:1] if has_sink else None
            ps.append(_attn_probs(qk[r * tq:(r + 1) * tq, :], bias_ref[r], valid, snk)[0].astype(BF16))
        o2 = jnp.dot(jnp.concatenate(ps, axis=0), vb, preferred_element_type=F32)
        o_ref[...] = jnp.where(head0, o2[:tq, :], o2[tq:, :]).astype(o_ref.dtype)

    in_specs = [pl.BlockSpec((tq, LANES), lambda p, i: (i, p)),
                pl.BlockSpec((kv_rows, LANES), lambda p, i: (0, p)),
                pl.BlockSpec((kv_rows, LANES), lambda p, i: (0, p)),
                pl.BlockSpec((2, tq, wk), lambda p, i: (p, 0, 0))]
    args = [q, k, v, bias_w]
    if has_sink:
        in_specs.append(pl.BlockSpec((1, 2, LANES), lambda p, i: (p, 0, 0)))
        args.append(sinks)
    res = pl.pallas_call(
        body, name=name, grid=(n_p, n_i), in_specs=in_specs + r_in,
        out_specs=[pl.BlockSpec((tq, LANES), lambda p, i: (i, p))] + r_out,
        out_shape=[jax.ShapeDtypeStruct((S, N_HEADS * HEAD_DIM), BF16)] + r_shapes,
        scratch_shapes=r_sems,
        compiler_params=pltpu.CompilerParams(dimension_semantics=("arbitrary", "arbitrary"), vmem_limit_bytes=VMEM_LIMIT,
                                             has_side_effects=rider is not None),
    )(*args, *r_args)
    return res[0], res[1:]


def attn_bwd(q, k, v, do, bias_w, sinks, *, n_prev, name, rider=None):
    S = q.shape[0]
    pad = n_prev * CHUNK
    kv_rows = k.shape[0]
    lead = kv_rows - S - pad
    tq = min(ATT_TQ, S)
    wk = tq + pad
    assert bias_w.shape == (N_HEADS, tq, wk), bias_w.shape
    has_sink = sinks is not None
    scale = HEAD_DIM ** -0.5

    r_in, r_out, r_shapes, r_sems, r_args = _rider_parts(rider)
    n_own_in = 6 if has_sink else 5
    n_own_out = 5 if has_sink else 4
    n_p, n_i = N_HEADS // 2, S // tq

    def body(*refs):
        q_ref, k_ref, v_ref, do_ref, bias_ref = refs[:5]
        sink_ref = refs[5] if has_sink else None
        o0 = n_own_in + len(r_in)
        dq_ref, dk_ref, dv_ref, db_ref = refs[o0:o0 + 4]
        dsk_ref = refs[o0 + 4] if has_sink else None
        i = pl.program_id(1)
        if rider is not None:
            p_id = pl.program_id(0)
            _ride(rider, refs[n_own_in:o0], refs[o0 + n_own_out:o0 + n_own_out + len(r_out)],
                  refs[o0 + n_own_out + len(r_out):],
                  jnp.logical_and(p_id == 0, i == 0), jnp.logical_and(p_id == n_p // 2, i == 0),
                  jnp.logical_and(p_id == n_p - 1, i == n_i - 1))

        @pl.when(i == 0)
        def _():
            dk_ref[...] = jnp.zeros_like(dk_ref)
            dv_ref[...] = jnp.zeros_like(dv_ref)
            db_ref[...] = jnp.zeros_like(db_ref)
            if has_sink:
                dsk_ref[...] = jnp.zeros_like(dsk_ref)

        start = pl.multiple_of(i * tq, tq)
        head0 = lax.broadcasted_iota(jnp.int32, (1, LANES), 1) < HEAD_DIM
        valid = lax.broadcasted_iota(jnp.int32, (1, wk), 1) + start >= pad
        kb = k_ref[pl.ds(pl.multiple_of(start + lead, CHUNK), wk), :]
        vb = v_ref[pl.ds(pl.multiple_of(start + lead, CHUNK), wk), :]
        q2 = _stack_heads(q_ref[...].astype(F32), head0)
        do2 = _stack_heads(do_ref[...].astype(F32), head0)
        qk = lax.dot_general(q2, kb, _NT, preferred_element_type=F32)
        dp2 = lax.dot_general(do2, vb, _NT, preferred_element_type=F32)
        pbs, dss = [], []
        for r in range(2):
            rows = slice(r * tq, (r + 1) * tq)
            snk = sink_ref[0, r:r + 1, 0:1] if has_sink else None
            p, ps = _attn_probs(qk[rows, :], bias_ref[r], valid, snk)
            dp = dp2[rows, :]
            delta = jnp.sum(p * dp, axis=1, keepdims=True)
            ds = p * (dp - delta)
            db_ref[r] += ds
            if has_sink:
                dsk = -jnp.sum(ps * delta, axis=0, keepdims=True)
                dsk_ref[0, r:r + 1, :] += jnp.broadcast_to(dsk, (1, LANES))
            pbs.append(p.astype(BF16))
            dss.append(ds.astype(BF16))
        ds2 = jnp.concatenate(dss, axis=0)
        dq2 = jnp.dot(ds2, kb, preferred_element_type=F32) * scale
        dq_ref[...] = jnp.where(head0, dq2[:tq, :], dq2[tq:, :])
        dk_ref[pl.ds(pl.multiple_of(start + lead, CHUNK), wk), :] += lax.dot_general(ds2, q2, _TN, preferred_element_type=F32) * scale
        dv_ref[pl.ds(pl.multiple_of(start + lead, CHUNK), wk), :] += lax.dot_general(jnp.concatenate(pbs, axis=0), do2, _TN,
                                                       preferred_element_type=F32)

    row_spec = pl.BlockSpec((tq, LANES), lambda p, i: (i, p))
    kv_spec = pl.BlockSpec((kv_rows, LANES), lambda p, i: (0, p))
    bias_spec = pl.BlockSpec((2, tq, wk), lambda p, i: (p, 0, 0))
    sink_spec = pl.BlockSpec((1, 2, LANES), lambda p, i: (p, 0, 0))
    in_specs = [row_spec, kv_spec, kv_spec, row_spec, bias_spec]
    args = [q, k, v, do, bias_w]
    out_specs = [row_spec, kv_spec, kv_spec, bias_spec]
    W = N_HEADS * HEAD_DIM
    out_shape = [jax.ShapeDtypeStruct((S, W), F32), jax.ShapeDtypeStruct((kv_rows, W), F32),
                 jax.ShapeDtypeStruct((kv_rows, W), F32), jax.ShapeDtypeStruct((N_HEADS, tq, wk), F32)]
    if has_sink:
        in_specs.append(sink_spec)
        args.append(sinks)
        out_specs.append(sink_spec)
        out_shape.append(jax.ShapeDtypeStruct((N_HEADS // 2, 2, LANES), F32))
    res = pl.pallas_call(
        body, name=name, grid=(n_p, n_i), in_specs=in_specs + r_in, out_specs=out_specs + r_out,
        out_shape=out_shape + r_shapes, scratch_shapes=r_sems,
        compiler_params=pltpu.CompilerParams(dimension_semantics=("arbitrary", "arbitrary"), vmem_limit_bytes=VMEM_LIMIT,
                                             has_side_effects=rider is not None),
    )(*args, *r_args)
    return res[:n_own_out], res[n_own_out:]


HP = SSM_HEADS // 2
PAIRS_PER_GROUP = HP // SSM_GROUPS
HEADS_PER_GROUP = SSM_HEADS // SSM_GROUPS
GW = HEADS_PER_GROUP * 64


def _ssd_dt(dtraw, dtb, A, tril):
    lane = lax.broadcasted_iota(jnp.int32, (1, LANES), 1)
    u = dtraw + dtb
    eu = jnp.exp(-jnp.abs(u))
    w1 = 1.0 + eu
    l1p = jnp.where(w1 == 1.0, eu, jnp.log(w1) * eu / jnp.where(w1 == 1.0, 1.0, w1 - 1.0))
    dt = jnp.where(lane < SSM_HEADS, jnp.maximum(u, 0.0) + l1p, 0.0)
    acs = jnp.dot(tril, dt * A, precision=HI, preferred_element_type=F32)
    return u, dt, acs


def _head_expander():
    hw = D_INNER // SSM_HEADS
    return (np.arange(LANES)[:, None] == np.arange(D_INNER)[None, :] // hw).astype(np.float32)


def _select_dot(t, sel):
    hi = t.astype(BF16)
    lo = (t - hi.astype(F32)).astype(BF16)
    return jnp.dot(hi, sel, preferred_element_type=F32) + jnp.dot(lo, sel, preferred_element_type=F32)


def ssd_fwd(xbc, dtraw, dtb, A, dexp, *, name):
    S = xbc.shape[0]
    L = min(SSD_L, S)
    nc = S // L
    N = SSM_STATE
    e_mat = jnp.asarray(_head_expander(), dtype=BF16)

    def body(xs_ref, b_ref, c_ref, dtr_ref, dtb_ref, a_ref, d_ref, e_ref, y_ref, st_out_ref, st_ref, xw_ref):
        c = pl.program_id(0)

        @pl.when(c == 0)
        def _():
            st_ref[...] = jnp.zeros_like(st_ref)

        st_out_ref[0] = st_ref[...]
        ri = lax.broadcasted_iota(jnp.int32, (L, L), 0)
        ci = lax.broadcasted_iota(jnp.int32, (L, L), 1)
        trilb = ri >= ci
        head0 = lax.broadcasted_iota(jnp.int32, (1, LANES), 1) < 64
        _, dt, acs = _ssd_dt(dtr_ref[...], dtb_ref[...], a_ref[...], trilb.astype(F32))
        acsT = acs.T
        last = acs[L - 1:L, :]
        expand = lambda t: _select_dot(t, e_ref[...])
        dte, eae, wte = expand(dt), expand(jnp.exp(acs)), expand(jnp.exp(last - acs) * dt)
        lasts = [last[:, h:h + 1] for h in range(SSM_HEADS)]
        for g in range(SSM_GROUPS):
            Bg = b_ref[:, g * N:(g + 1) * N].astype(BF16)
            Cg = c_ref[:, g * N:(g + 1) * N].astype(BF16)
            CB = lax.dot_general(Cg, Bg, _NT, preferred_element_type=F32)
            Z = lax.dot_general(Cg, st_ref[g * GW:(g + 1) * GW, :].astype(BF16), _NT, preferred_element_type=F32)
            for q in range(PAIRS_PER_GROUP):
                hp = g * PAIRS_PER_GROUP + q
                sl = slice(hp * LANES, (hp + 1) * LANES)
                xs = xs_ref[:, sl]
                xd = xs * dte[:, sl]
                ms, xh = [], []
                for r in range(2):
                    h = 2 * hp + r
                    dec = jnp.exp(jnp.where(trilb, acs[:, h:h + 1] - acsT[h:h + 1, :], -jnp.inf))
                    ms.append((CB * dec).astype(BF16))
                    xh.append(jnp.where(head0 if r == 0 else jnp.logical_not(head0), xd, 0.0).astype(BF16))
                yi = jnp.dot(jnp.concatenate(ms, axis=1), jnp.concatenate(xh, axis=0), preferred_element_type=F32)
                y_ref[:, sl] = yi + Z[:, q * LANES:(q + 1) * LANES] * eae[:, sl] + d_ref[:, sl] * xs
                xw_ref[:, sl] = (xs * wte[:, sl]).astype(BF16)
        for g in range(SSM_GROUPS):
            Bg = b_ref[:, g * N:(g + 1) * N].astype(BF16)
            sn = lax.dot_general(xw_ref[:, g * GW:(g + 1) * GW], Bg, _TN, preferred_element_type=F32)
            for k in range(HEADS_PER_GROUP):
                h = g * HEADS_PER_GROUP + k
                rows = slice(h * 64, (h + 1) * 64)
                st_ref[rows, :] = st_ref[rows, :] * jnp.exp(lasts[h]) + sn[k * 64:(k + 1) * 64, :]

    return pl.pallas_call(
        body, name=name, grid=(nc,),
        in_specs=[pl.BlockSpec((L, D_INNER), lambda c: (c, 0)),
                  pl.BlockSpec((L, SSM_GROUPS * N), lambda c: (c, D_INNER // (SSM_GROUPS * N))),
                  pl.BlockSpec((L, SSM_GROUPS * N), lambda c: (c, D_INNER // (SSM_GROUPS * N) + 1)),
                  pl.BlockSpec((L, LANES), lambda c: (c, 0)),
                  pl.BlockSpec((1, LANES), lambda c: (0, 0)),
                  pl.BlockSpec((1, LANES), lambda c: (0, 0)),
                  pl.BlockSpec((1, D_INNER), lambda c: (0, 0)),
                  pl.BlockSpec((LANES, D_INNER), lambda c: (0, 0))],
        out_specs=[pl.BlockSpec((L, D_INNER), lambda c: (c, 0)),
                   pl.BlockSpec((1, D_INNER, N), lambda c: (c, 0, 0))],
        out_shape=[jax.ShapeDtypeStruct((S, D_INNER), F32), jax.ShapeDtypeStruct((nc, D_INNER, N), F32)],
        scratch_shapes=[pltpu.VMEM((D_INNER, N), F32), pltpu.VMEM((L, D_INNER), BF16)],
        compiler_params=_params("arbitrary"),
    )(xbc, xbc, xbc, dtraw, dtb, A, dexp, e_mat)


def ssd_bwd(xbc, dtraw, dtb, A, dexp, states, dy, *, name, rider=None):
    S = xbc.shape[0]
    L = min(SSD_L, S)
    nc = S // L
    N = SSM_STATE
    e_np = _head_expander()
    e_mat, et_mat = jnp.asarray(e_np, dtype=BF16), jnp.asarray(e_np.T, dtype=BF16)

    r_in, r_out, r_shapes, r_sems, r_args = _rider_parts(rider)

    def body(*refs):
        xs_ref, b_ref, c_ref, dtr_ref, dtb_ref, a_ref, d_ref, e_ref, et_ref, st_in_ref, dy_ref = refs[:11]
        o0 = 11 + len(r_in)
        dxs_ref, db_ref, dc_ref, ddtr_ref, da_ref, ddtb_ref, dd_ref = refs[o0:o0 + 7]
        s0 = o0 + 7 + len(r_out)
        dst_ref, xw_ref, dz_ref, r_ref, dsr_ref, dsc_ref = refs[s0:s0 + 6]
        step = pl.program_id(0)
        if rider is not None:
            _ride(rider, refs[11:o0], refs[o0 + 7:s0], refs[s0 + 6:], step == 0, step == nc // 2, step == nc - 1)

        @pl.when(step == 0)
        def _():
            dst_ref[...] = jnp.zeros_like(dst_ref)
            dsr_ref[...] = jnp.zeros_like(dsr_ref)
            dsc_ref[...] = jnp.zeros_like(dsc_ref)
            da_ref[...] = jnp.zeros_like(da_ref)
            ddtb_ref[...] = jnp.zeros_like(ddtb_ref)
            dd_ref[...] = jnp.zeros_like(dd_ref)

        ri = lax.broadcasted_iota(jnp.int32, (L, L), 0)
        ci = lax.broadcasted_iota(jnp.int32, (L, L), 1)
        trilb = ri >= ci
        lane = lax.broadcasted_iota(jnp.int32, (1, LANES), 1)
        sub = lax.broadcasted_iota(jnp.int32, (LANES, 1), 0)
        head0 = lane < 64
        A = a_ref[...]
        u, dt, acs = _ssd_dt(dtr_ref[...], dtb_ref[...], A, trilb.astype(F32))
        acsT = acs.T
        last = acs[L - 1:L, :]
        elast = jnp.exp(last)
        er = jnp.exp(last - acs)
        wt = er * dt
        expand = lambda t: _select_dot(t, e_ref[...])
        dte, eae, wte = expand(dt), expand(jnp.exp(acs)), expand(wt)
        dlast = jnp.zeros((1, LANES), F32)
        dcbs = []
        for g in range(SSM_GROUPS):
            Bg = b_ref[:, g * N:(g + 1) * N].astype(BF16)
            Cg = c_ref[:, g * N:(g + 1) * N].astype(BF16)
            stg = st_in_ref[0, g * GW:(g + 1) * GW, :]
            dstg = dst_ref[g * GW:(g + 1) * GW, :]
            CB = lax.dot_general(Cg, Bg, _NT, preferred_element_type=F32)
            CBT = lax.dot_general(Bg, Cg, _NT, preferred_element_type=F32)
            Z = lax.dot_general(Cg, stg.astype(BF16), _NT, preferred_element_type=F32)
            U = lax.dot_general(Bg, dstg.astype(BF16), _NT, preferred_element_type=F32)
            dcb = jnp.zeros((L, L), F32)
            for q in range(PAIRS_PER_GROUP):
                hp = g * PAIRS_PER_GROUP + q
                sl = slice(hp * LANES, (hp + 1) * LANES)
                qs = slice(q * LANES, (q + 1) * LANES)
                xs = xs_ref[:, sl]
                dyp = dy_ref[:, sl]
                dtp, eap, wp, Dp = dte[:, sl], eae[:, sl], wte[:, sl], d_ref[:, sl]
                xd = xs * dtp
                dy2 = jnp.concatenate([jnp.where(head0, dyp, 0.0), jnp.where(head0, 0.0, dyp)], axis=0).astype(BF16)
                G2 = lax.dot_general(dy2, xd.astype(BF16), _NT, preferred_element_type=F32)
                mts = []
                for r in range(2):
                    h = 2 * hp + r
                    seg = acs[:, h:h + 1] - acsT[h:h + 1, :]
                    dec = jnp.exp(jnp.where(trilb, seg, -jnp.inf))
                    decT = jnp.exp(jnp.where(ri <= ci, -seg, -jnp.inf))
                    gd = G2[r * L:(r + 1) * L, :] * dec
                    dcb = dcb + gd
                    dseg = gd * CB
                    dsr_ref[:, h:h + 1] = jnp.sum(dseg, axis=1, keepdims=True)
                    dsc_ref[h:h + 1, :] = jnp.sum(dseg, axis=0, keepdims=True)
                    mts.append((CBT * decT).astype(BF16))
                dxd = jnp.dot(jnp.concatenate(mts, axis=1), dy2, preferred_element_type=F32)
                Up = U[:, qs]
                r_ref[0:L, sl] = dyp * Z[:, qs] * eap
                r_ref[L:2 * L, sl] = dxd * xs
                r_ref[2 * L:3 * L, sl] = Up * xs
                dz_ref[:, sl] = (dyp * eap).astype(BF16)
                xw_ref[:, sl] = (xs * wp).astype(BF16)
                dxs_ref[:, sl] = dxd * dtp + Dp * dyp + Up * wp
                dd_ref[:, sl] += jnp.sum(dyp * xs, axis=0, keepdims=True)
            dcbs.append(dcb)
            t = dstg * stg
            for k in range(HEADS_PER_GROUP):
                dlast = dlast + jnp.where(lane == g * HEADS_PER_GROUP + k,
                                          jnp.sum(t[k * 64:(k + 1) * 64, :], keepdims=True), 0.0)
        fold = lambda k: _select_dot(r_ref[k * L:(k + 1) * L, :], et_ref[...])
        r1, r2, dws = fold(0), fold(1), fold(2)
        dww = dws * wt
        ddt = r2 + dws * er
        dacs = r1 - dww + dsr_ref[...] - dsc_ref[...].T
        dlast = dlast * elast + jnp.sum(dww, axis=0, keepdims=True)
        lasts = [last[:, h:h + 1] for h in range(SSM_HEADS)]
        for g in range(SSM_GROUPS):
            Bg = b_ref[:, g * N:(g + 1) * N].astype(BF16)
            Cg = c_ref[:, g * N:(g + 1) * N].astype(BF16)
            gs = slice(g * GW, (g + 1) * GW)
            stb = st_in_ref[0, gs, :].astype(BF16)
            dstb = dst_ref[gs, :].astype(BF16)
            dcbb = dcbs[g].astype(BF16)
            dzg = dz_ref[:, gs]
            dc_ref[:, g * N:(g + 1) * N] = (jnp.dot(dzg, stb, preferred_element_type=F32)
                                            + jnp.dot(dcbb, Bg, preferred_element_type=F32))
            db_ref[:, g * N:(g + 1) * N] = (jnp.dot(xw_ref[:, gs], dstb, preferred_element_type=F32)
                                            + lax.dot_general(dcbb, Cg, _TN, preferred_element_type=F32))
            dsn = lax.dot_general(dzg, Cg, _TN, preferred_element_type=F32)
            for k in range(HEADS_PER_GROUP):
                h = g * HEADS_PER_GROUP + k
                rows = slice(h * 64, (h + 1) * 64)
                dst_ref[rows, :] = dst_ref[rows, :] * jnp.exp(lasts[h]) + dsn[k * 64:(k + 1) * 64, :]
        rowi = lax.broadcasted_iota(jnp.int32, (L, 1), 0)
        dacs = dacs + jnp.where(rowi == L - 1, dlast, 0.0)
        da = jnp.dot((ci >= ri).astype(F32), dacs, precision=HI, preferred_element_type=F32)
        ddt = ddt + da * A
        da_ref[...] += jnp.sum(da * dt, axis=0, keepdims=True)
        ddtr = jnp.where(lane < SSM_HEADS, ddt * _sigmoid(u), 0.0)
        ddtr_ref[...] = ddtr
        ddtb_ref[...] += jnp.sum(ddtr, axis=0, keepdims=True)

    rev = lambda c: nc - 1 - c
    gn = SSM_GROUPS * N
    res = pl.pallas_call(
        body, name=name, grid=(nc,),
        in_specs=[pl.BlockSpec((L, D_INNER), lambda c: (rev(c), 0)),
                  pl.BlockSpec((L, gn), lambda c: (rev(c), D_INNER // gn)),
                  pl.BlockSpec((L, gn), lambda c: (rev(c), D_INNER // gn + 1)),
                  pl.BlockSpec((L, LANES), lambda c: (rev(c), 0)),
                  pl.BlockSpec((1, LANES), lambda c: (0, 0)),
                  pl.BlockSpec((1, LANES), lambda c: (0, 0)),
                  pl.BlockSpec((1, D_INNER), lambda c: (0, 0)),
                  pl.BlockSpec((LANES, D_INNER), lambda c: (0, 0)),
                  pl.BlockSpec((D_INNER, LANES), lambda c: (0, 0)),
                  pl.BlockSpec((1, D_INNER, N), lambda c: (rev(c), 0, 0)),
                  pl.BlockSpec((L, D_INNER), lambda c: (rev(c), 0))] + r_in,
        out_specs=[pl.BlockSpec((L, D_INNER), lambda c: (rev(c), 0)),
                   pl.BlockSpec((L, gn), lambda c: (rev(c), 0)),
                   pl.BlockSpec((L, gn), lambda c: (rev(c), 0)),
                   pl.BlockSpec((L, LANES), lambda c: (rev(c), 0)),
                   pl.BlockSpec((1, LANES), lambda c: (0, 0)),
                   pl.BlockSpec((1, LANES), lambda c: (0, 0)),
                   pl.BlockSpec((1, D_INNER), lambda c: (0, 0))] + r_out,
        out_shape=[jax.ShapeDtypeStruct((S, D_INNER), F32), jax.ShapeDtypeStruct((S, gn), F32),
                   jax.ShapeDtypeStruct((S, gn), F32), jax.ShapeDtypeStruct((S, LANES), F32),
                   jax.ShapeDtypeStruct((1, LANES), F32), jax.ShapeDtypeStruct((1, LANES), F32),
                   jax.ShapeDtypeStruct((1, D_INNER), F32)] + r_shapes,
        scratch_shapes=[pltpu.VMEM((D_INNER, N), F32), pltpu.VMEM((L, D_INNER), BF16), pltpu.VMEM((L, D_INNER), BF16),
                        pltpu.VMEM((3 * L, D_INNER), F32), pltpu.VMEM((L, LANES), F32), pltpu.VMEM((LANES, L), F32)]
                       + r_sems,
        compiler_params=pltpu.CompilerParams(dimension_semantics=("arbitrary",), vmem_limit_bytes=VMEM_LIMIT,
                                             has_side_effects=rider is not None),
    )(xbc, xbc, xbc, dtraw, dtb, A, dexp, e_mat, et_mat, states, dy, *r_args)
    return res[:7], res[7:]


BAND_A = (A_PREV + 1) * CHUNK
REL_W = 640


def _relpos_select():
    k = np.arange(REL_W)
    rel = np.where(k < BAND_A, A_PREV * CHUNK - k, A_PREV * CHUNK - (k - REL_W))
    idx = np.clip(rel, -MAX_REL, MAX_REL) + MAX_REL
    sel = (np.arange(REL_W)[:, None] == idx[None, :]) & (k != BAND_A)[None, :]
    return sel.astype(np.float32)


def relpos_bias(table_pad, *, name):
    def body(t_ref, s_ref, o_ref):
        v = jnp.dot(t_ref[...], s_ref[...], precision=HI, preferred_element_type=F32)
        for h in range(N_HEADS):
            o_ref[h] = pltpu.roll(jnp.broadcast_to(v[h:h + 1, :], (CHUNK, REL_W)), 0, 1, stride=1, stride_axis=0)

    return pl.pallas_call(body, name=name, out_shape=jax.ShapeDtypeStruct((N_HEADS, CHUNK, REL_W), F32),
                          compiler_params=pltpu.CompilerParams(vmem_limit_bytes=VMEM_LIMIT),
                          )(table_pad, jnp.asarray(_relpos_select()))


def relpos_grad(dbias_rev, *, name):
    def body(d_ref, s_ref, o_ref):
        head = lax.broadcasted_iota(jnp.int32, (N_HEADS, 1), 0)
        dv = jnp.zeros((N_HEADS, REL_W), F32)
        for h in range(N_HEADS):
            back = pltpu.roll(d_ref[h], REL_W - (CHUNK - 1), 1, stride=1, stride_axis=0)
            dv = dv + jnp.where(head == h, jnp.sum(back, axis=0, keepdims=True), 0.0)
        o_ref[...] = lax.dot_general(dv, s_ref[...], _NT, precision=HI, preferred_element_type=F32)

    return pl.pallas_call(body, name=name, out_shape=jax.ShapeDtypeStruct((N_HEADS, REL_W), F32),
                          compiler_params=pltpu.CompilerParams(vmem_limit_bytes=VMEM_LIMIT),
                          )(dbias_rev, jnp.asarray(_relpos_select()))


def matmul_loss(a, b, residual, t, *, name):
    (M, K), (_, D) = a.shape, b.shape
    tm = _pick(M, MM_ROWS)

    def body(a_ref, b_ref, r_ref, t_ref, dy_ref, l_ref):
        y = jnp.dot(a_ref[...].astype(BF16), b_ref[...].astype(BF16), preferred_element_type=F32) + r_ref[...]
        e = y - t_ref[...]
        dy_ref[...] = e * (1.0 / D)

        @pl.when(pl.program_id(0) == 0)
        def _():
            l_ref[...] = jnp.zeros_like(l_ref)

        part = jnp.sum(jnp.sum(e * e, axis=1, keepdims=True), axis=0, keepdims=True) * (0.5 / D)
        l_ref[...] += jnp.broadcast_to(part, l_ref.shape)

    row = pl.BlockSpec((tm, D), lambda i: (i, 0))
    return pl.pallas_call(
        body, name=name, grid=(M // tm,),
        in_specs=[pl.BlockSpec((tm, K), lambda i: (i, 0)), pl.BlockSpec((K, D), lambda i: (0, 0)), row, row],
        out_specs=[row, pl.BlockSpec((1, LANES), lambda i: (0, 0))],
        out_shape=[jax.ShapeDtypeStruct((M, D), F32), jax.ShapeDtypeStruct((1, LANES), F32)],
        compiler_params=_params("arbitrary"),
    )(a, b, residual, t)


def f_adamw(w, g, m, v):
    m = ADAM_B1 * m + (1.0 - ADAM_B1) * g
    v = ADAM_B2 * v + (1.0 - ADAM_B2) * (g * g)
    m_hat = m / (1.0 - ADAM_B1 ** ADAM_STEP)
    v_hat = v / (1.0 - ADAM_B2 ** ADAM_STEP)
    delta = -ADAM_LR * (m_hat / (jnp.sqrt(v_hat) + ADAM_EPS) + ADAM_WD * w)
    return delta, m, v


ANY = pl.BlockSpec(memory_space=pl.ANY)


def _pos():
    return lax.axis_index("x"), lax.axis_index("y"), lax.axis_index("c")


def _other_chips(x, y):
    return [(1 - x, y), (x, 1 - y), (1 - x, 1 - y)]


class Rider:
    def __init__(self, ins, outs, sems, start, mid, finish):
        self.ins, self.outs, self.sems = list(ins), list(outs), list(sems)
        self.start, self.mid, self.finish = start, mid, finish


def _rider_parts(rider):
    if rider is None:
        return [], [], [], [], []
    return [ANY] * len(rider.ins), [ANY] * len(rider.outs), rider.outs, rider.sems, rider.ins


def _ride(rider, ins, outs, sems, first, mid, last):
    pos = _pos()

    @pl.when(first)
    def _():
        rider.start(pos, ins, outs, sems)

    if rider.mid is not None:
        @pl.when(mid)
        def _():
            rider.mid(pos, ins, outs, sems)

    @pl.when(last)
    def _():
        rider.finish(pos, ins, outs, sems)


def run_rider(rider, *, name):
    n_in, n_out = len(rider.ins), len(rider.outs)

    def body(*refs):
        ins, outs, sems = refs[:n_in], refs[n_in:n_in + n_out], refs[n_in + n_out:]
        pos = _pos()
        rider.start(pos, ins, outs, sems)
        if rider.mid is not None:
            rider.mid(pos, ins, outs, sems)
        rider.finish(pos, ins, outs, sems)

    return pl.pallas_call(
        body, name=name, in_specs=[ANY] * n_in, out_specs=[ANY] * n_out, out_shape=rider.outs,
        scratch_shapes=rider.sems, compiler_params=pltpu.CompilerParams(has_side_effects=True),
    )(*rider.ins)


def gather_rider(shards):
    n = len(shards)

    def copies(pos, ins, outs, sems):
        x, y, c = pos
        send, recv, fsend, frecv = sems
        me = 2 * x + y
        sib = (x, y, 1 - c)
        first, arrive, passed, theirs = [], [], [], []
        for i in range(n):
            for j, (px, py) in enumerate(_other_chips(x, y)):
                k = 3 * i + j
                far = dict(device_id=(px, py, c), device_id_type=MESH)
                near = dict(device_id=sib, device_id_type=MESH)
                got = outs[i].at[2 * px + py, c]
                his = outs[i].at[2 * px + py, 1 - c]
                first.append(pltpu.make_async_remote_copy(ins[i].at[c], outs[i].at[me, c], send.at[k], recv.at[k], **far))
                arrive.append(pltpu.make_async_remote_copy(ins[i].at[c], got, send.at[k], recv.at[k], **far))
                passed.append(pltpu.make_async_remote_copy(got, got, fsend.at[k], frecv.at[k], **near))
                theirs.append(pltpu.make_async_remote_copy(his, his, fsend.at[k], frecv.at[k], **near))
        return first, arrive, passed, theirs

    def start(*a):
        for cp in copies(*a)[0]:
            cp.start()

    def mid(*a):
        _, arrive, passed, _ = copies(*a)
        for got, cp in zip(arrive, passed):
            got.wait_recv()
            cp.start()

    def finish(*a):
        first, _, passed, theirs = copies(*a)
        for cp in theirs:
            cp.wait_recv()
        for cp in first + passed:
            cp.wait_send()

    return Rider(shards, [jax.ShapeDtypeStruct((4,) + s.shape, s.dtype) for s in shards],
                 [pltpu.SemaphoreType.DMA((3 * n,))] * 4, start, mid, finish)


def scatter_rider(ps):
    n = len(ps)

    def copies(pos, ins, outs, sems):
        x, y, c = pos
        send, recv = sems
        return [pltpu.make_async_remote_copy(ins[i].at[2 * px + py], outs[i].at[j], send.at[3 * i + j], recv.at[3 * i + j],
                                             device_id=(px, py, c), device_id_type=MESH)
                for i in range(n) for j, (px, py) in enumerate(_other_chips(x, y))]

    def start(*a):
        for cp in copies(*a):
            cp.start()

    def finish(*a):
        for cp in copies(*a):
            cp.wait()

    return Rider(ps, [jax.ShapeDtypeStruct((3,) + p.shape[1:], p.dtype) for p in ps],
                 [pltpu.SemaphoreType.DMA((3 * n,))] * 2, start, None, finish)


def pair_swap_halves(gs, *, name):
    n = len(gs)

    def body(*refs):
        ins, outs = refs[:n], refs[n:2 * n]
        send, recv = refs[2 * n:]
        x, y, c = _pos()
        cps = []
        for i in range(n):
            cp = pltpu.make_async_remote_copy(ins[i].at[1 - c], outs[i], send.at[i], recv.at[i],
                                              device_id=(x, y, 1 - c), device_id_type=MESH)
            cp.start()
            cps.append(cp)
        for cp in cps:
            cp.wait()

    return pl.pallas_call(
        body, name=name, in_specs=[ANY] * n, out_specs=[ANY] * n,
        out_shape=[jax.ShapeDtypeStruct(g.shape[1:], g.dtype) for g in gs],
        scratch_shapes=[pltpu.SemaphoreType.DMA((n,)), pltpu.SemaphoreType.DMA((n,))],
        compiler_params=pltpu.CompilerParams(has_side_effects=True),
    )(*gs)


def pair_share(hs, *, name):
    n = len(hs)

    def body(*refs):
        ins, outs = refs[:n], refs[n:2 * n]
        send, recv = refs[2 * n:]
        x, y, c = _pos()
        cps = []
        for i in range(n):
            cp = pltpu.make_async_remote_copy(ins[i], outs[i].at[c], send.at[i], recv.at[i],
                                              device_id=(x, y, 1 - c), device_id_type=MESH)
            cp.start()
            cps.append(cp)
        for i, cp in enumerate(cps):
            cp.wait_send()
            pltpu.make_async_remote_copy(ins[i], outs[i].at[1 - c], send.at[i], recv.at[i],
                                         device_id=(x, y, 1 - c), device_id_type=MESH).wait_recv()

    return pl.pallas_call(
        body, name=name, in_specs=[ANY] * n, out_specs=[ANY] * n,
        out_shape=[jax.ShapeDtypeStruct((2,) + h.shape, h.dtype) for h in hs],
        scratch_shapes=[pltpu.SemaphoreType.DMA((n,)), pltpu.SemaphoreType.DMA((n,))],
        compiler_params=pltpu.CompilerParams(has_side_effects=True),
    )(*hs)


def gather_all(buf, *, name):
    def body(in_ref, out_ref, send, recv, loc):
        x, y, c = _pos()
        lid = 4 * x + 2 * y + c
        lc = pltpu.make_async_copy(in_ref, out_ref.at[lid], loc.at[0])
        lc.start()
        cps = []
        for k in range(1, 8):
            px = 1 - x if k & 4 else x
            py = 1 - y if k & 2 else y
            pc = 1 - c if k & 1 else c
            cp = pltpu.make_async_remote_copy(in_ref, out_ref.at[lid], send.at[k - 1], recv.at[k - 1],
                                              device_id=(px, py, pc), device_id_type=MESH)
            cp.start()
            cps.append((cp, 4 * px + 2 * py + pc, (px, py, pc)))
        for k, (cp, plid, peer) in enumerate(cps):
            cp.wait_send()
            pltpu.make_async_remote_copy(in_ref, out_ref.at[plid], send.at[k], recv.at[k],
                                         device_id=peer, device_id_type=MESH).wait_recv()
        lc.wait()

    return pl.pallas_call(
        body, name=name, in_specs=[ANY], out_specs=ANY,
        out_shape=jax.ShapeDtypeStruct((8,) + buf.shape, buf.dtype),
        scratch_shapes=[pltpu.SemaphoreType.DMA((7,)), pltpu.SemaphoreType.DMA((7,)), pltpu.SemaphoreType.DMA((1,))],
        compiler_params=pltpu.CompilerParams(has_side_effects=True),
    )(buf)


def sum_slots(a, *, name):
    n = a.shape[0]

    def body(a_ref, o_ref):
        acc = a_ref[0]
        for k in range(1, n):
            acc = acc + a_ref[k]
        o_ref[...] = acc

    return pl.pallas_call(body, name=name, out_shape=jax.ShapeDtypeStruct(a.shape[1:], a.dtype),
                          compiler_params=pltpu.CompilerParams(vmem_limit_bytes=VMEM_LIMIT))(a)


def _row_tile(r, want, mult=16):
    t = (min(want, r) // mult) * mult
    while t >= mult:
        if r % t == 0:
            return t
        t -= mult
    return r


def pair_add(g, r1, csel, *, name):
    _, _, r, C = g.shape
    tr = _row_tile(r, 256)

    def body(g_ref, r_ref, c_ref, p32_ref, pb_ref):
        south = c_ref[0:1, 0:1] == 0.0
        p = jnp.where(south, g_ref[0, 0], g_ref[1, 0]) + r_ref[0]
        p32_ref[0] = p
        pb_ref[0] = p.astype(BF16)

    return pl.pallas_call(
        body, name=name, grid=(4, r // tr),
        in_specs=[pl.BlockSpec((2, 1, tr, C), lambda j, t: (0, j, t, 0)), pl.BlockSpec((1, tr, C), lambda j, t: (j, t, 0)),
                  pl.BlockSpec((1, LANES), lambda j, t: (0, 0))],
        out_specs=[pl.BlockSpec((1, tr, C), lambda j, t: (j, t, 0))] * 2,
        out_shape=[jax.ShapeDtypeStruct((4, r, C), F32), jax.ShapeDtypeStruct((4, r, C), BF16)],
        compiler_params=_params("parallel", "parallel"),
    )(g, r1, csel)


def chip_add(p32, r3, msel, *, name):
    _, r, C = p32.shape
    tr = _row_tile(r, 128)

    def body(p_ref, r_ref, m_ref, o_ref):
        me = m_ref[0:1, 0:1]
        acc = jnp.where(me == 0.0, p_ref[0], jnp.where(me == 1.0, p_ref[1], jnp.where(me == 2.0, p_ref[2], p_ref[3])))
        for j in range(3):
            acc = acc + r_ref[j].astype(F32)
        o_ref[...] = acc

    return pl.pallas_call(
        body, name=name, grid=(r // tr,),
        in_specs=[pl.BlockSpec((4, tr, C), lambda t: (0, t, 0)), pl.BlockSpec((3, tr, C), lambda t: (0, t, 0)),
                  pl.BlockSpec((1, LANES), lambda t: (0, 0))],
        out_specs=pl.BlockSpec((tr, C), lambda t: (t, 0)),
        out_shape=jax.ShapeDtypeStruct((r, C), F32),
        compiler_params=_params("parallel"),
    )(p32, r3, msel)


def _consts():
    i512 = np.arange(N_HEADS * HEAD_DIM)
    i128 = np.arange(LANES)
    bd512 = (i512[:, None] // HEAD_DIM == i512[None, :] // HEAD_DIM).astype(np.float32)
    bd128 = (i128[:, None] // HEAD_DIM == i128[None, :] // HEAD_DIM).astype(np.float32)
    fold = (np.arange(HEAD_DIM)[:, None] == (i512[None, :] % HEAD_DIM)).astype(np.float32)
    grp = N_HEADS // 2 * HEAD_DIM
    expand = ((i128[:, None] // HEAD_DIM == i512[None, :] // grp)
              & (i128[:, None] % HEAD_DIM == i512[None, :] % HEAD_DIM)).astype(np.float32)
    band = (B_PREV + 1) * CHUNK
    rel = np.arange(CHUNK)[:, None] - (np.arange(band)[None, :] - B_PREV * CHUNK)
    slopes = 2.0 ** (-8.0 * np.arange(1, N_HEADS + 1, dtype=np.float32) / N_HEADS)
    bias_b = (-slopes[:, None, None] * np.abs(rel).astype(np.float32)[None]).astype(np.float32)
    return [jnp.asarray(a) for a in (bd512, bd128, fold, expand)], jnp.asarray(bias_b)


def _ffn_fwd(xin, h, l, W, P, next_gain=None, target=None):
    Wi = W["ffn_in"][l]
    gate = matmul(h, Wi[:, :D_FF], mode="nn", name=f"ffn{l}_gate", out_dtype=BF16)
    up = matmul(h, Wi[:, D_FF:], mode="nn", name=f"ffn{l}_up", out_dtype=BF16)
    gc, act = dwconv_fwd(gate, P["ffn_conv_w"][l], P["ffn_conv_b"][l:l + 1], lambda y, u: (y, _silu(y) * u), [up],
                         [BF16, BF16], name=f"ffn{l}_conv")
    saved = (xin, h, gate, gc, up, act)
    if target is not None:
        dxout, lpart = matmul_loss(act, W["ffn_out"][l], xin, target, name=f"ffn{l}_out")
        return dxout, saved, lpart
    xout, h_next = matmul_norm(act, W["ffn_out"][l], xin, next_gain, name=f"ffn{l}_out")
    return xout, saved, h_next


def _ffn_bwd(dxout, l, saved, W, P):
    xin, h, gate, gc, up, act = saved
    g = P["norm_ffn"][l:l + 1]
    Wi = W["ffn_in"][l]
    dact = matmul(dxout, W["ffn_out"][l], mode="nt", name=f"ffn{l}_dact", out_dtype=BF16)
    dWo = matmul(act, dxout, mode="tn", name=f"ffn{l}_dwout")
    dgate, dcw, dcb, dup = dwconv_bwd(gate, P["ffn_conv_w"][l], [gc, up, dact],
                                      lambda c, u, da: (da * u * _dsilu(c), da * _silu(c)), [BF16],
                                      name=f"ffn{l}_dconv")
    dh = matmul(dgate, Wi[:, :D_FF], mode="nt", name=f"ffn{l}_dh_gate")
    dxin, dg = matmul_dnorm(dup, Wi[:, D_FF:], dh, xin, g, dxout, name=f"ffn{l}_dh_up")
    dWi = jnp.concatenate([matmul(h, dgate, mode="tn", name=f"ffn{l}_dw_gate"),
                           matmul(h, dup, mode="tn", name=f"ffn{l}_dw_up")], axis=1)
    return dxin, dWi, dWo, dg, dcw, dcb


class NoComm:
    def fwd_rider(self, tag):
        return None

    def fwd_done(self, tag, outs, W, P):
        pass

    def grads(self, tag, cols, rows):
        return None

    def bwd_done(self, tag, outs):
        pass


def local_step(x, tgt, W, P, comm):
    qk_consts, bias_b = _consts()
    DQ = N_HEADS * HEAD_DIM
    tile = _row_tile(x.shape[0], QK_ROWS)
    kv_lead = [-(-n * CHUNK // tile) * tile for n in (A_PREV, B_PREV)]

    g_mix0 = P["norm_mix"][0:1]
    (h0,) = rowwise(f_rmsnorm, [x], [g_mix0], [(D_MODEL, BF16)], name="attn_norm")
    qkv = matmul(h0, W["attn_in"], mode="nn", name="attn_qkv")
    qk_par = [P["q_norm_a"], P["k_norm_a"], P["q_norm_b"], P["k_norm_b"]] + qk_consts
    qk_lead = [0, kv_lead[0], kv_lead[0], 0, kv_lead[1], kv_lead[1]]
    qa, ka, va, qb, kb, vb = rowwise(f_qknorm, [qkv], qk_par, [(DQ, BF16)] * 6, name="attn_qknorm", lead=qk_lead,
                                     tm=QK_ROWS)
    table = jnp.pad(P["relpos_table"], ((0, 0), (0, REL_W - (2 * MAX_REL + 1))))
    nj = min(ATT_TQ, x.shape[0]) // CHUNK
    bias_a = widen_bias(relpos_bias(table, name="relpos_bias")[:, :, :BAND_A], A_PREV, nj)
    bias_b = widen_bias(bias_b, B_PREV, nj)
    sinks = jnp.broadcast_to(P["sinks"].reshape(N_HEADS // 2, 2, 1), (N_HEADS // 2, 2, LANES))
    oa, late = attn_fwd(qa, ka, va, bias_a, None, n_prev=A_PREV, name="attn_a", rider=comm.fwd_rider("a"))
    comm.fwd_done("a", late, W, P)
    ob, late = attn_fwd(qb, kb, vb, bias_b, sinks, n_prev=B_PREV, name="attn_b", rider=comm.fwd_rider("b"))
    comm.fwd_done("b", late, W, P)
    Wao = W["attn_out"]
    x1 = matmul(oa, Wao[:DQ], mode="nn", name="attn_out_a", residual=x)
    x1, hf0 = matmul_norm(ob, Wao[DQ:], x1, P["norm_ffn"][0:1], name="attn_out_b")
    g_mix1 = P["norm_mix"][1:2]
    x2, ffn0, h2 = _ffn_fwd(x1, hf0, 0, W, P, g_mix1)

    Ws = W["ssm_in"]
    CC = D_INNER + 2 * SSM_GROUPS * SSM_STATE
    Wz, Wx = Ws[:, :D_INNER], Ws[:, D_INNER:D_INNER + CC]
    Wdt = jnp.pad(Ws[:, D_INNER + CC:], ((0, 0), (0, LANES - SSM_HEADS)))
    z = matmul(h2, Wz, mode="nn", name="ssm_z", out_dtype=BF16)
    xr = matmul(h2, Wx, mode="nn", name="ssm_xbc", out_dtype=BF16)
    dtraw = matmul(h2, Wdt, mode="nn", name="ssm_dt")
    xc, xbc = dwconv_fwd(xr, P["ssm_conv_w"], P["ssm_conv_b"], lambda y: (y, _silu(y)), [], [BF16, F32],
                         name="ssm_conv")
    pad32 = lambda v: jnp.pad(v, ((0, 0), (0, LANES - SSM_HEADS)))
    A = pad32(-jnp.exp(P["ssm_a_log"]))
    dtb = pad32(P["ssm_dt_bias"])
    dexp = jnp.repeat(P["ssm_d"], D_INNER // SSM_HEADS, axis=1)
    y, states = ssd_fwd(xbc, dtraw, dtb, A, dexp, name="ssd_fwd")
    (y2,) = rowwise(f_gate_norm, [y, z], [P["ssm_norm"]], [(D_INNER, BF16)], name="ssm_gate_norm")
    x3, hf1 = matmul_norm(y2, W["ssm_out"], x2, P["norm_ffn"][1:2], name="ssm_out")
    dx4, ffn1, lpart = _ffn_fwd(x3, hf1, 1, W, P, target=tgt)

    dx3, dWfi1, dWfo1, dgf1, dfcw1, dfcb1 = _ffn_bwd(dx4, 1, ffn1, W, P)
    out_f1 = comm.grads("f1", dWfi1, dWfo1)
    dy2 = matmul(dx3, W["ssm_out"], mode="nt", name="ssm_dy")
    dWso = matmul(y2, dx3, mode="tn", name="ssm_dwout")
    dy, dz, dnw = rowwise_vjp(f_gate_norm, [y, z], [P["ssm_norm"]], [dy2], [(0, F32), (1, BF16)], [0],
                              name="ssm_dgate_norm")
    (dxs, dB, dC, ddtraw, dA, ddtb, dDl), sent = ssd_bwd(xbc, dtraw, dtb, A, dexp, states, dy, name="ssd_bwd",
                                                          rider=out_f1)
    comm.bwd_done("f1", sent)
    dxr, dscw, dscb = dwconv_bwd(xr, P["ssm_conv_w"], [xc, (dxs, dB, dC)], lambda c, g: (g * _dsilu(c),), [],
                                 name="ssm_dconv")
    dh2 = matmul(dz, Wz, mode="nt", name="ssm_dh_z")
    dh2 = matmul(dxr, Wx, mode="nt", name="ssm_dh_x", residual=dh2)
    dx2, dgm1 = matmul_dnorm(ddtraw, Wdt, dh2, x2, g_mix1, dx3, name="ssm_dh_dt")
    dWs = jnp.concatenate([matmul(h2, dz, mode="tn", name="ssm_dw_z"),
                           matmul(h2, dxr, mode="tn", name="ssm_dw_x"),
                           matmul(h2, ddtraw, mode="tn", name="ssm_dw_dt")[:, :SSM_HEADS]], axis=1)
    out_s = comm.grads("s", dWs, dWso)

    dx1, dWfi0, dWfo0, dgf0, dfcw0, dfcb0 = _ffn_bwd(dx2, 0, ffn0, W, P)
    out_f0 = comm.grads("f0", dWfi0, dWfo0)
    doa = matmul(dx1, Wao[:DQ], mode="nt", name="attn_do_a", out_dtype=BF16)
    dob = matmul(dx1, Wao[DQ:], mode="nt", name="attn_do_b", out_dtype=BF16)
    dWao = jnp.concatenate([matmul(oa, dx1, mode="tn", name="attn_dwout_a"),
                            matmul(ob, dx1, mode="tn", name="attn_dwout_b")], axis=0)
    (dqa, dka, dva, dbias_a), sent = attn_bwd(qa, ka, va, doa, bias_a, None, n_prev=A_PREV, name="attn_a_bwd",
                                              rider=out_s)
    comm.bwd_done("s", sent)
    (dqb, dkb, dvb, _, dsk), sent = attn_bwd(qb, kb, vb, dob, bias_b, sinks, n_prev=B_PREV, name="attn_b_bwd",
                                             rider=out_f0)
    comm.bwd_done("f0", sent)
    dqkv, dgqa, dgka, dgqb, dgkb = rowwise_vjp(f_qknorm, [qkv], qk_par, [dqa, dka, dva, dqb, dkb, dvb],
                                               [(0, BF16)], [0, 1, 2, 3], name="attn_dqknorm", cot_skip=qk_lead,
                                               tm=QK_ROWS)
    dx, dgm0 = matmul_dnorm(dqkv, W["attn_in"], None, x, g_mix0, dx1, name="attn_dh")
    dWai = matmul(h0, dqkv, mode="tn", name="attn_dwin")
    dbias_a = fold_bias(dbias_a, A_PREV, nj)
    dbias_rev = jnp.pad(dbias_a[:, ::-1, :], ((0, 0), (0, 0), (0, REL_W - BAND_A)))
    dtable = relpos_grad(dbias_rev, name="relpos_grad")[:, :2 * MAX_REL + 1]

    gW = {"attn_in": dWai, "attn_out": dWao, "ssm_in": dWs, "ssm_out": dWso,
          "ffn_in": [dWfi0, dWfi1], "ffn_out": [dWfo0, dWfo1]}
    gP = {"norm_mix": jnp.concatenate([dgm0, dgm1], axis=0),
          "norm_ffn": jnp.concatenate([dgf0, dgf1], axis=0),
          "relpos_table": dtable, "q_norm_a": dgqa, "k_norm_a": dgka, "q_norm_b": dgqb, "k_norm_b": dgkb,
          "sinks": dsk[:, :, 0].reshape(1, N_HEADS),
          "ssm_conv_w": dscw, "ssm_conv_b": dscb,
          "ssm_dt_bias": ddtb[:, :SSM_HEADS], "ssm_a_log": dA[:, :SSM_HEADS] * A[:, :SSM_HEADS],
          "ssm_d": dDl.reshape(SSM_HEADS, D_INNER // SSM_HEADS).sum(axis=1).reshape(1, SSM_HEADS),
          "ssm_norm": dnw,
          "ffn_conv_w": jnp.stack([dfcw0, dfcw1]), "ffn_conv_b": jnp.concatenate([dfcb0, dfcb1], axis=0)}
    return lpart, dx, gW, gP


WEIGHTS = ["norm_mix", "norm_ffn", "attn_w_in", "attn_w_out", "relpos_table", "q_norm_a", "k_norm_a", "q_norm_b",
           "k_norm_b", "sinks", "ssm_w_in", "ssm_conv_w", "ssm_conv_b", "ssm_dt_bias", "ssm_a_log", "ssm_d",
           "ssm_norm", "ssm_w_out", "ffn_w_in", "ffn_conv_w", "ffn_conv_b", "ffn_w_out"]
ARGS = ["x"] + WEIGHTS + ["loss_target"] + ["m_" + w for w in WEIGHTS] + ["v_" + w for w in WEIGHTS]
N_CHIPS = 4
SMALL_ROWS = 384
SMALL_ORDER = ["norm_mix", "norm_ffn", "relpos_table", "q_norm_a", "k_norm_a", "q_norm_b", "k_norm_b", "sinks",
               "ssm_dt_bias", "ssm_a_log", "ssm_d", "ffn_conv_b", "ssm_conv_w", "ssm_conv_b", "ssm_norm", "ffn_conv_w"]


def _cols_to_slabs(g):
    K, N = g.shape
    return g.reshape(2, K // 2, N_CHIPS, N // N_CHIPS).transpose(0, 2, 1, 3)


def _rows_to_slabs(g):
    R, C = g.shape
    return g.reshape(N_CHIPS, 2, R // (2 * N_CHIPS), C).transpose(1, 0, 2, 3)


class MeshComm:
    def __init__(self, d, xi, yi, ci):
        self.d, self.ci, self.me = d, ci, 2 * xi + yi
        self.csel = jnp.full((1, LANES), ci, F32)
        self.msel = jnp.full((1, LANES), self.me, F32)
        halves = lambda w: w.reshape((2, -1, w.shape[-1]))
        small = jnp.concatenate([d[k].reshape(-1) for k in ("ssm_conv_w", "ssm_conv_b", "ssm_norm", "ffn_conv_w")])
        small = jnp.pad(small, (0, 2 * 40 * LANES - small.shape[0])).reshape(2, 40, LANES)
        self.shards = {"attn": [halves(d["attn_w_in"][0].astype(BF16)), halves(d["attn_w_out"][0].astype(BF16))],
                       "a": [d["ffn_w_in"].astype(BF16), small],
                       "b": [d["ffn_w_out"].astype(BF16), halves(d["ssm_w_in"][0].astype(BF16)),
                             halves(d["ssm_w_out"][0].astype(BF16))]}
        self.p32, self.mine = {}, {}

    def _whole(self, tag, outs):
        return [lax.dynamic_update_slice_in_dim(g, s[None], self.me, axis=0) for g, s in zip(outs, self.shards[tag])]

    @staticmethod
    def _cat_cols(g):
        return jnp.concatenate([g[j].reshape((-1, g.shape[-1])) for j in range(N_CHIPS)], axis=1)

    def first_weights(self):
        g_ai, g_ao = self._whole("attn", run_rider(gather_rider(self.shards["attn"]), name="gather_attn"))
        return {"attn_in": self._cat_cols(g_ai), "attn_out": g_ao.reshape(-1, D_MODEL)}

    def fwd_rider(self, tag):
        return gather_rider(self.shards[tag])

    def fwd_done(self, tag, outs, W, P):
        if tag == "b":
            g_fo, g_si, g_so = self._whole("b", outs)
            W["ffn_out"] = [g_fo[:, l].reshape(-1, D_MODEL) for l in range(2)]
            W["ssm_in"], W["ssm_out"] = self._cat_cols(g_si), g_so.reshape(-1, D_MODEL)
            return
        g_fi, g_sm = self._whole("a", outs)
        W["ffn_in"] = [jnp.concatenate([g_fi[j, l] for j in range(N_CHIPS)], axis=1) for l in range(2)]
        sm = g_sm.reshape(N_CHIPS, -1)
        CC = D_INNER + 2 * SSM_GROUPS * SSM_STATE
        c4, f4 = CC // N_CHIPS, D_FF // N_CHIPS
        o1 = SSM_CONV * c4
        o2 = o1 + c4
        o3 = o2 + D_INNER // N_CHIPS
        o4 = o3 + 2 * FFN_CONV * f4
        P["ssm_conv_w"] = sm[:, :o1].reshape(N_CHIPS, SSM_CONV, c4).transpose(1, 0, 2).reshape(SSM_CONV, CC)
        P["ssm_conv_b"] = sm[:, o1:o2].reshape(1, CC)
        P["ssm_norm"] = sm[:, o2:o3].reshape(1, D_INNER)
        P["ffn_conv_w"] = sm[:, o3:o4].reshape(N_CHIPS, 2, FFN_CONV, f4).transpose(1, 2, 0, 3).reshape(2, FFN_CONV, D_FF)

    def grads(self, tag, cols, rows):
        slabs = [_cols_to_slabs(cols), _rows_to_slabs(rows)]
        from_sib = pair_swap_halves(slabs, name="grad_pair_swap_" + tag)
        pairs = [pair_add(g, r, self.csel, name=f"grad_pair_add_{tag}{i}") for i, (g, r) in enumerate(zip(slabs, from_sib))]
        self.p32[tag] = [p[0] for p in pairs]
        return scatter_rider([p[1] for p in pairs])

    def bwd_done(self, tag, outs):
        self.mine[tag] = [chip_add(p, r, self.msel, name=f"grad_chip_add_{tag}{i}")
                          for i, (p, r) in enumerate(zip(self.p32[tag], outs))]

    def finish(self, d_attn_in, d_attn_out):
        self.bwd_done("at", run_rider(self.grads("at", d_attn_in, d_attn_out), name="grad_scatter_at"))
        order = ["at", "s", "f0", "f1"]
        mine = [m for t in order for m in self.mine[t]]
        theirs = pair_share(mine, name="grad_pair_share")
        full = [lax.dynamic_update_slice_in_dim(b, a[None], self.ci, axis=0).reshape((-1, a.shape[-1]))
                for a, b in zip(mine, theirs)]
        ai, ao, si, so, fi0, fo0, fi1, fo1 = full
        return {"attn_w_in": ai[None], "attn_w_out": ao[None], "ssm_w_in": si[None], "ssm_w_out": so[None],
                "ffn_w_in": jnp.stack([fi0, fi1]), "ffn_w_out": jnp.stack([fo0, fo1])}


def _adamw(w, g, m, v, name):
    shp = w.shape
    two = lambda a: a.reshape((-1, shp[-1]))
    outs = [(shp[-1], F32)] * 3
    d, nm, nv = rowwise(f_adamw, [two(w), two(g), two(m), two(v)], [], outs, name="adamw_" + name)
    return d.reshape(shp), nm.reshape(shp), nv.reshape(shp)


def kernel(x, norm_mix, norm_ffn, attn_w_in, attn_w_out, relpos_table, q_norm_a, k_norm_a, q_norm_b, k_norm_b, sinks, ssm_w_in, ssm_conv_w, ssm_conv_b, ssm_dt_bias, ssm_a_log, ssm_d, ssm_norm, ssm_w_out, ffn_w_in, ffn_conv_w, ffn_conv_b, ffn_w_out, loss_target, m_norm_mix, m_norm_ffn, m_attn_w_in, m_attn_w_out, m_relpos_table, m_q_norm_a, m_k_norm_a, m_q_norm_b, m_k_norm_b, m_sinks, m_ssm_w_in, m_ssm_conv_w, m_ssm_conv_b, m_ssm_dt_bias, m_ssm_a_log, m_ssm_d, m_ssm_norm, m_ssm_w_out, m_ffn_w_in, m_ffn_conv_w, m_ffn_conv_b, m_ffn_w_out, v_norm_mix, v_norm_ffn, v_attn_w_in, v_attn_w_out, v_relpos_table, v_q_norm_a, v_k_norm_a, v_q_norm_b, v_k_norm_b, v_sinks, v_ssm_w_in, v_ssm_conv_w, v_ssm_conv_b, v_ssm_dt_bias, v_ssm_a_log, v_ssm_d, v_ssm_norm, v_ssm_w_out, v_ffn_w_in, v_ffn_conv_w, v_ffn_conv_b, v_ffn_w_out):
    d = dict(zip(ARGS, (x, norm_mix, norm_ffn, attn_w_in, attn_w_out, relpos_table, q_norm_a, k_norm_a, q_norm_b, k_norm_b, sinks, ssm_w_in, ssm_conv_w, ssm_conv_b, ssm_dt_bias, ssm_a_log, ssm_d, ssm_norm, ssm_w_out, ffn_w_in, ffn_conv_w, ffn_conv_b, ffn_w_out, loss_target, m_norm_mix, m_norm_ffn, m_attn_w_in, m_attn_w_out, m_relpos_table, m_q_norm_a, m_k_norm_a, m_q_norm_b, m_k_norm_b, m_sinks, m_ssm_w_in, m_ssm_conv_w, m_ssm_conv_b, m_ssm_dt_bias, m_ssm_a_log, m_ssm_d, m_ssm_norm, m_ssm_w_out, m_ffn_w_in, m_ffn_conv_w, m_ffn_conv_b, m_ffn_w_out, v_norm_mix, v_norm_ffn, v_attn_w_in, v_attn_w_out, v_relpos_table, v_q_norm_a, v_k_norm_a, v_q_norm_b, v_k_norm_b, v_sinks, v_ssm_w_in, v_ssm_conv_w, v_ssm_conv_b, v_ssm_dt_bias, v_ssm_a_log, v_ssm_d, v_ssm_norm, v_ssm_w_out, v_ffn_w_in, v_ffn_conv_w, v_ffn_conv_b, v_ffn_w_out)))
    xi, yi, ci = _pos()
    me = 2 * xi + yi
    CC = D_INNER + 2 * SSM_GROUPS * SSM_STATE
    c4, f4 = CC // N_CHIPS, D_FF // N_CHIPS

    P = {k: d[k] for k in ["norm_mix", "norm_ffn", "q_norm_a", "k_norm_a", "q_norm_b", "k_norm_b", "sinks",
                           "ssm_dt_bias", "ssm_a_log", "ssm_d", "ffn_conv_b"]}
    P["relpos_table"] = d["relpos_table"][0]
    comm = MeshComm(d, xi, yi, ci)
    W = comm.first_weights()
    lpart, dx, gW, gP = local_step(d["x"][0], d["loss_target"][0], W, P, comm)
    loss = lax.psum(lpart[0, 0], ("x", "y", "c"))
    grads = comm.finish(gW["attn_in"], gW["attn_out"])

    flat = jnp.concatenate([gP[k].reshape(-1) for k in SMALL_ORDER])
    flat = jnp.pad(flat, (0, SMALL_ROWS * LANES - flat.shape[0])).reshape(SMALL_ROWS, LANES)
    tot = sum_slots(gather_all(flat, name="small_gather"), name="small_sum").reshape(-1)
    off = 0
    for k in SMALL_ORDER:
        n = int(np.prod(gP[k].shape))
        g = tot[off:off + n].reshape(gP[k].shape)
        off += n
        if k == "ssm_conv_w":
            g = lax.dynamic_slice_in_dim(g, me * c4, c4, axis=1)[None]
        elif k == "ssm_conv_b":
            g = lax.dynamic_slice_in_dim(g, me * c4, c4, axis=1)
        elif k == "ssm_norm":
            g = lax.dynamic_slice_in_dim(g, me * (D_INNER // N_CHIPS), D_INNER // N_CHIPS, axis=1)
        elif k == "ffn_conv_w":
            g = lax.dynamic_slice_in_dim(g, me * f4, f4, axis=2)
        elif k == "relpos_table":
            g = g[None]
        grads[k] = g

    deltas, new_m, new_v = {}, {}, {}
    for k in WEIGHTS:
        deltas[k], new_m[k], new_v[k] = _adamw(d[k], grads[k], d["m_" + k], d["v_" + k], k)
    return (loss, dx[None], *[grads[k] for k in WEIGHTS], *[deltas[k] for k in WEIGHTS],
            *[new_m[k] for k in WEIGHTS], *[new_v[k] for k in WEIGHTS])
```

```python
import functools

import numpy as np
import jax
import jax.numpy as jnp
from jax import lax
from jax.experimental import pallas as pl
from jax.experimental.pallas import tpu as pltpu

F32 = jnp.float32
BF16 = jnp.bfloat16
HI = lax.Precision.HIGHEST

D_MODEL = 1024
CHUNK = 64
EPS = 1e-6
HEAD_DIM = 64
N_HEADS = 8
A_PREV = 8
B_PREV = 2
MAX_REL = 256
D_INNER = 2048
SSM_HEADS = 32
SSM_GROUPS = 4
SSM_STATE = 128
SSM_CONV = 4
D_FF = 2816
FFN_CONV = 3
LANES = 128
SUBLANES = 8
VMEM_LIMIT = 56 * 1024 * 1024
SSD_L = 128

ADAM_LR = 0.001
ADAM_B1 = 0.9
ADAM_B2 = 0.999
ADAM_EPS = 1e-08
ADAM_WD = 0.01
ADAM_STEP = 10

MESH = pl.DeviceIdType.MESH


def _params(*sem):
    return pltpu.CompilerParams(dimension_semantics=sem, vmem_limit_bytes=VMEM_LIMIT)


def _pick(n, want):
    if n <= want:
        return n
    t = (want // LANES) * LANES
    while t >= LANES:
        if n % t == 0:
            return t
        t -= LANES
    return n


MM_ROWS = 512
MM_COLS = 1536
MM_RED = 2048
MM_SHORT = 1024


def matmul(a, b, *, mode, name, out_dtype=F32, residual=None):
    dims = {"nn": (((1,), (0,)), ((), ())), "nt": (((1,), (1,)), ((), ())), "tn": (((0,), (0,)), ((), ()))}[mode]
    if mode == "tn":
        assert residual is None and out_dtype == F32
        (K, M), (K2, N) = a.shape, b.shape
        assert K == K2, (a.shape, b.shape)
        tm, tn, tk = _pick(M, MM_COLS), _pick(N, MM_COLS), _pick(K, MM_RED)

        def body(a_ref, b_ref, o_ref):
            k = pl.program_id(2)
            p = lax.dot_general(a_ref[...].astype(BF16), b_ref[...].astype(BF16), dims, preferred_element_type=F32)

            @pl.when(k == 0)
            def _():
                o_ref[...] = p

            @pl.when(k != 0)
            def _():
                o_ref[...] += p

        return pl.pallas_call(
            body, name=name, grid=(M // tm, N // tn, K // tk),
            in_specs=[pl.BlockSpec((tk, tm), lambda i, j, k: (k, i)), pl.BlockSpec((tk, tn), lambda i, j, k: (k, j))],
            out_specs=pl.BlockSpec((tm, tn), lambda i, j, k: (i, j)),
            out_shape=jax.ShapeDtypeStruct((M, N), F32),
            compiler_params=_params("parallel", "parallel", "arbitrary"),
        )(a, b)

    if mode == "nn":
        (M, K), (K2, N) = a.shape, b.shape
    else:
        (M, K), (N, K2) = a.shape, b.shape
    assert K == K2, (a.shape, b.shape, mode)
    tm, tn = _pick(M, MM_ROWS if K > MM_SHORT else 2 * MM_ROWS), _pick(N, MM_COLS)

    def body(*refs):
        a_ref, b_ref = refs[:2]
        o_ref = refs[-1]
        r = lax.dot_general(a_ref[...].astype(BF16), b_ref[...].astype(BF16), dims, preferred_element_type=F32)
        if residual is not None:
            r = r + refs[2][...].astype(F32)
        o_ref[...] = r.astype(o_ref.dtype)

    a_spec = pl.BlockSpec((tm, K), lambda j, i: (i, 0))
    b_spec = pl.BlockSpec((K, tn), lambda j, i: (0, j)) if mode == "nn" else pl.BlockSpec((tn, K), lambda j, i: (j, 0))
    o_spec = pl.BlockSpec((tm, tn), lambda j, i: (i, j))
    in_specs = [a_spec, b_spec] + ([o_spec] if residual is not None else [])
    args = (a, b) + ((residual,) if residual is not None else ())
    return pl.pallas_call(
        body, name=name, grid=(N // tn, M // tm),
        in_specs=in_specs, out_specs=o_spec,
        out_shape=jax.ShapeDtypeStruct((M, N), out_dtype),
        compiler_params=_params("parallel", "parallel"),
    )(*args)


def matmul_norm(a, b, residual, g, *, name):
    (M, K), (_, N) = a.shape, b.shape
    tm = _pick(M, MM_ROWS)

    def body(a_ref, b_ref, r_ref, g_ref, x_ref, h_ref):
        x = jnp.dot(a_ref[...].astype(BF16), b_ref[...].astype(BF16), preferred_element_type=F32) + r_ref[...]
        x_ref[...] = x
        h_ref[...] = f_rmsnorm(x, g_ref[...])[0].astype(BF16)

    row = pl.BlockSpec((tm, N), lambda i: (i, 0))
    return pl.pallas_call(
        body, name=name, grid=(M // tm,),
        in_specs=[pl.BlockSpec((tm, K), lambda i: (i, 0)), pl.BlockSpec((K, N), lambda i: (0, 0)), row,
                  pl.BlockSpec((1, N), lambda i: (0, 0))],
        out_specs=[row, row],
        out_shape=[jax.ShapeDtypeStruct((M, N), F32), jax.ShapeDtypeStruct((M, N), BF16)],
        compiler_params=_params("parallel"),
    )(a, b, residual, g)


def matmul_dnorm(a, b, partial, x, g, dres, *, name):
    (M, K), (N, _) = a.shape, b.shape
    tm = _pick(M, MM_ROWS)
    has_part = partial is not None

    def body(*refs):
        a_ref, b_ref = refs[:2]
        x_ref, g_ref, dres_ref, dx_ref, dg_ref = refs[-5:]
        dh = lax.dot_general(a_ref[...].astype(BF16), b_ref[...].astype(BF16), _NT, preferred_element_type=F32)
        if has_part:
            dh = dh + refs[2][...]
        xv = x_ref[...]
        r = lax.rsqrt(jnp.mean(xv * xv, axis=-1, keepdims=True) + EPS)
        xhat = xv * r
        dxh = dh * g_ref[...]
        dx_ref[...] = dres_ref[...] + r * (dxh - xhat * jnp.mean(dxh * xhat, axis=-1, keepdims=True))
        dg = jnp.sum(dh * xhat, axis=0, keepdims=True)

        @pl.when(pl.program_id(0) == 0)
        def _():
            dg_ref[...] = dg

        @pl.when(pl.program_id(0) != 0)
        def _():
            dg_ref[...] += dg

    row = pl.BlockSpec((tm, N), lambda i: (i, 0))
    vec = pl.BlockSpec((1, N), lambda i: (0, 0))
    in_specs = [pl.BlockSpec((tm, K), lambda i: (i, 0)), pl.BlockSpec((N, K), lambda i: (0, 0))]
    args = [a, b]
    if has_part:
        in_specs.append(row)
        args.append(partial)
    return pl.pallas_call(
        body, name=name, grid=(M // tm,), in_specs=in_specs + [row, vec, row], out_specs=[row, vec],
        out_shape=[jax.ShapeDtypeStruct((M, N), F32), jax.ShapeDtypeStruct((1, N), F32)],
        compiler_params=_params("arbitrary"),
    )(*args, x, g, dres)


def rowwise(f, rows, params, outs, *, name, tm=256, lead=None):
    S = rows[0].shape[0]
    tm = _row_tile(S, tm)
    nr, npar = len(rows), len(params)
    lead = [0] * len(outs) if lead is None else lead
    assert all(ld % tm == 0 for ld in lead)
    padded = [k for k, ld in enumerate(lead) if ld]

    def body(*refs):
        vals = [r[...].astype(F32) for r in refs[:nr + npar]]
        res = f(*vals)
        for o_ref, r in zip(refs[nr + npar + len(padded):], res):
            o_ref[...] = r.astype(o_ref.dtype)

    in_specs = [pl.BlockSpec((tm, r.shape[1]), lambda i: (i, 0)) for r in rows]
    in_specs += [pl.BlockSpec(p.shape, lambda i: (0, 0)) for p in params]
    in_specs += [pl.BlockSpec(memory_space=pl.ANY)] * len(padded)
    zeros = [jnp.zeros((S + lead[k], outs[k][0]), outs[k][1]) for k in padded]
    out_specs = [pl.BlockSpec((tm, c), lambda i, s=ld // tm: (i + s, 0)) for (c, _), ld in zip(outs, lead)]
    out_shape = [jax.ShapeDtypeStruct((S + ld, c), dt) for (c, dt), ld in zip(outs, lead)]
    return pl.pallas_call(body, name=name, grid=(S // tm,), in_specs=in_specs, out_specs=out_specs,
                          out_shape=out_shape, input_output_aliases={nr + npar + n: k for n, k in enumerate(padded)},
                          compiler_params=_params("parallel"))(*rows, *params, *zeros)


def rowwise_vjp(f, rows, params, cots, drow, dpar, *, name, tm=256, cot_skip=None):
    S = rows[0].shape[0]
    tm = _row_tile(S, tm)
    nr, npar, nc = len(rows), len(params), len(cots)
    skip = [0] * nc if cot_skip is None else [s // tm for s in cot_skip]
    assert cot_skip is None or all(s % tm == 0 for s in cot_skip)

    def body(*refs):
        vals = [r[...].astype(F32) for r in refs[:nr + npar]]
        cvals = [r[...].astype(F32) for r in refs[nr + npar:nr + npar + nc]]
        o_refs = refs[nr + npar + nc:]
        want = [ri for ri, _ in drow] + [nr + pi for pi in dpar]

        def f_want(*d):
            full = list(vals)
            for k, v in zip(want, d):
                full[k] = v
            return f(*full)

        _, vjp = jax.vjp(f_want, *[vals[k] for k in want])
        grads = vjp(tuple(cvals))
        for o_ref, g in zip(o_refs[:len(drow)], grads):
            o_ref[...] = g.astype(o_ref.dtype)
        first = pl.program_id(0) == 0
        for o_ref, g in zip(o_refs[len(drow):], grads[len(drow):]):
            g = g.astype(F32)

            @pl.when(first)
            def _(o_ref=o_ref, g=g):
                o_ref[...] = g

            @pl.when(jnp.logical_not(first))
            def _(o_ref=o_ref, g=g):
                o_ref[...] += g

    in_specs = [pl.BlockSpec((tm, r.shape[1]), lambda i: (i, 0)) for r in rows]
    in_specs += [pl.BlockSpec(p.shape, lambda i: (0, 0)) for p in params]
    in_specs += [pl.BlockSpec((tm, c.shape[1]), lambda i, s=s: (i + s, 0)) for c, s in zip(cots, skip)]
    out_specs = [pl.BlockSpec((tm, rows[ri].shape[1]), lambda i: (i, 0)) for ri, _ in drow]
    out_specs += [pl.BlockSpec(params[pi].shape, lambda i: (0, 0)) for pi in dpar]
    out_shape = [jax.ShapeDtypeStruct(rows[ri].shape, dt) for ri, dt in drow]
    out_shape += [jax.ShapeDtypeStruct(params[pi].shape, F32) for pi in dpar]
    return pl.pallas_call(body, name=name, grid=(S // tm,), in_specs=in_specs, out_specs=out_specs,
                          out_shape=out_shape, compiler_params=_params("arbitrary"))(*rows, *params, *cots)


HALO = 2 * SUBLANES
CONV_ROWS = 64


def dwconv_fwd(x, w, b, post, extra, outs, *, name, tm=512):
    S, C = x.shape
    K = w.shape[0]
    tm = min(tm, S)
    hb = tm // HALO
    ne = len(extra)

    def body(*refs):
        x_ref, halo_ref, w_ref, b_ref = refs[:4]
        e_refs = refs[4:4 + ne]
        o_refs = refs[4 + ne:4 + ne + len(outs)]
        buf = refs[-1]
        i = pl.program_id(0)
        buf[0:HALO, :] = jnp.where(i == 0, 0.0, halo_ref[...].astype(F32))
        buf[HALO:HALO + tm, :] = x_ref[...].astype(F32)
        for c0 in range(0, C, LANES):
            cs = slice(c0, c0 + LANES)
            for r0 in range(0, tm, CONV_ROWS):
                rs = slice(r0, r0 + CONV_ROWS)
                acc = jnp.broadcast_to(b_ref[:, cs], (CONV_ROWS, LANES))
                for k in range(K):
                    acc = acc + w_ref[k:k + 1, cs] * buf[pl.ds(HALO - (K - 1) + k + r0, CONV_ROWS), cs]
                for o_ref, r in zip(o_refs, post(acc, *[e[rs, cs].astype(F32) for e in e_refs])):
                    o_ref[rs, cs] = r.astype(o_ref.dtype)

    row = pl.BlockSpec((tm, C), lambda i: (i, 0))
    return pl.pallas_call(
        body, name=name, grid=(S // tm,),
        in_specs=[row,
                  pl.BlockSpec((HALO, C), lambda i: (jnp.maximum(i * hb - 1, 0), 0)),
                  pl.BlockSpec((K, C), lambda i: (0, 0)),
                  pl.BlockSpec((1, C), lambda i: (0, 0))] + [row] * ne,
        out_specs=[row] * len(outs),
        out_shape=[jax.ShapeDtypeStruct((S, C), dt) for dt in outs],
        scratch_shapes=[pltpu.VMEM((HALO + tm, C), F32)],
        compiler_params=_params("parallel"),
    )(x, x, w, b, *extra)


def dwconv_bwd(x, w, srcs, dy_fn, extra_outs, *, name, tm=256):
    S, C = x.shape
    K = w.shape[0]
    tm = min(tm, S)
    hb = tm // HALO
    n = S // tm
    groups = [s if isinstance(s, tuple) else (s,) for s in srcs]
    flat = [a for g in groups for a in g]
    nf = len(flat)

    def body(*refs):
        x_ref, xh_ref, w_ref = refs[:3]
        dx_ref, dw_ref, db_ref = refs[3 + 2 * nf:6 + 2 * nf]
        e_refs = refs[6 + 2 * nf:6 + 2 * nf + len(extra_outs)]
        bx, bd = refs[-2:]

        def strips(first, c0, rs):
            out, at = [], first
            for g in groups:
                off = 0
                for a in g:
                    if off <= c0 < off + a.shape[1]:
                        out.append(refs[at][rs, c0 - off:c0 - off + LANES].astype(F32))
                    off += a.shape[1]
                    at += 1
            return out

        i = pl.program_id(0)
        bx[0:HALO, :] = jnp.where(i == 0, 0.0, xh_ref[...].astype(F32))
        bx[HALO:HALO + tm, :] = x_ref[...].astype(F32)

        @pl.when(i == 0)
        def _():
            dw_ref[...] = jnp.zeros_like(dw_ref)
            db_ref[...] = jnp.zeros_like(db_ref)

        for c0 in range(0, C, LANES):
            cs = slice(c0, c0 + LANES)
            dws = [jnp.zeros((1, LANES), F32) for _ in range(K)]
            dbs = jnp.zeros((1, LANES), F32)
            for r0 in range(0, tm, CONV_ROWS):
                rs = slice(r0, r0 + CONV_ROWS)
                res = dy_fn(*strips(3, c0, rs))
                dyv = res[0]
                for e_ref, r in zip(e_refs, res[1:]):
                    e_ref[rs, cs] = r.astype(e_ref.dtype)
                bd[rs, cs] = dyv
                for k in range(K):
                    dws[k] = dws[k] + jnp.sum(dyv * bx[pl.ds(HALO - (K - 1) + k + r0, CONV_ROWS), cs], axis=0,
                                              keepdims=True)
                dbs = dbs + jnp.sum(dyv, axis=0, keepdims=True)
            bd[tm:tm + HALO, cs] = jnp.where(i == n - 1, 0.0, dy_fn(*strips(3 + nf, c0, slice(None)))[0])
            for k in range(K):
                dw_ref[k:k + 1, cs] += dws[k]
            db_ref[:, cs] += dbs
            for r0 in range(0, tm, CONV_ROWS):
                acc = jnp.zeros((CONV_ROWS, LANES), F32)
                for k in range(K):
                    acc = acc + w_ref[k:k + 1, cs] * bd[pl.ds((K - 1) - k + r0, CONV_ROWS), cs]
                dx_ref[r0:r0 + CONV_ROWS, cs] = acc.astype(dx_ref.dtype)

    row = lambda c: pl.BlockSpec((tm, c), lambda i: (i, 0))
    nxt = lambda c: pl.BlockSpec((HALO, c), lambda i: (jnp.minimum((i + 1) * hb, S // HALO - 1), 0))
    return pl.pallas_call(
        body, name=name, grid=(n,),
        in_specs=[row(C), pl.BlockSpec((HALO, C), lambda i: (jnp.maximum(i * hb - 1, 0), 0)),
                  pl.BlockSpec((K, C), lambda i: (0, 0))]
                 + [row(a.shape[1]) for a in flat] + [nxt(a.shape[1]) for a in flat],
        out_specs=[row(C), pl.BlockSpec((K, C), lambda i: (0, 0)), pl.BlockSpec((1, C), lambda i: (0, 0))]
                  + [row(C)] * len(extra_outs),
        out_shape=[jax.ShapeDtypeStruct((S, C), BF16), jax.ShapeDtypeStruct((K, C), F32),
                   jax.ShapeDtypeStruct((1, C), F32)] + [jax.ShapeDtypeStruct((S, C), dt) for dt in extra_outs],
        scratch_shapes=[pltpu.VMEM((HALO + tm, C), F32), pltpu.VMEM((tm + HALO, C), F32)],
        compiler_params=_params("arbitrary"),
    )(x, x, w, *flat, *flat)


def _sigmoid(x):
    return 0.5 * jnp.tanh(0.5 * x) + 0.5


def _silu(x):
    return x * _sigmoid(x)


def _dsilu(x):
    s = _sigmoid(x)
    return s * (1.0 + x * (1.0 - s))


def f_rmsnorm(x, g):
    return (x * lax.rsqrt(jnp.mean(x * x, axis=-1, keepdims=True) + EPS) * g,)


SEL = lax.Precision.HIGH


def _group_norm(x, bd, width):
    ms = jnp.dot(x * x, bd, precision=SEL, preferred_element_type=F32) * (1.0 / width)
    return x * lax.rsqrt(ms + EPS)


def f_qknorm(qkv, gqa, gka, gqb, gkb, bd512, bd128, fold, expand):
    dq = N_HEADS * HEAD_DIM
    qa, ka, va, qb = (qkv[:, i * dq:(i + 1) * dq] for i in range(4))
    kb = qkv[:, 4 * dq:4 * dq + LANES]
    vb = qkv[:, 4 * dq + LANES:4 * dq + 2 * LANES]
    tile8 = lambda g: jnp.dot(g, fold, precision=HI, preferred_element_type=F32)
    qa = _group_norm(qa, bd512, HEAD_DIM) * tile8(gqa)
    ka = _group_norm(ka, bd512, HEAD_DIM) * tile8(gka)
    qb = _group_norm(qb, bd512, HEAD_DIM) * tile8(gqb)
    kb = _group_norm(kb, bd128, HEAD_DIM) * tile8(gkb)[:, :LANES]
    kb = jnp.dot(kb, expand, precision=SEL, preferred_element_type=F32)
    vb = jnp.dot(vb, expand, precision=SEL, preferred_element_type=F32)
    return qa, ka, va, qb, kb, vb


def f_gate_norm(y, z, nw):
    v = y * _silu(z)
    gw = D_INNER // SSM_GROUPS
    parts = []
    for g in range(SSM_GROUPS):
        vg = v[:, g * gw:(g + 1) * gw]
        parts.append(vg * lax.rsqrt(jnp.mean(vg * vg, axis=-1, keepdims=True) + EPS))
    return (jnp.concatenate(parts, axis=-1) * nw,)


ATT_TQ = 256
QK_ROWS = 512
_NT =(((1,), (1,)), ((), ()))
_TN = (((0,), (0,)), ((), ()))


def _stack_heads(t, head0):
    return jnp.concatenate([jnp.where(head0, t, 0.0), jnp.where(head0, 0.0, t)], axis=0).astype(BF16)


def _attn_probs(qk, bias, valid, snk):
    s = qk * (HEAD_DIM ** -0.5) + bias
    s = jnp.where(valid, s, -jnp.inf)
    m = jnp.max(s, axis=1, keepdims=True)
    if snk is not None:
        m = jnp.maximum(m, snk)
    e = jnp.exp(s - m)
    den = jnp.sum(e, axis=1, keepdims=True)
    if snk is None:
        return e / den, None
    es = jnp.exp(snk - m)
    den = den + es
    return e / den, es / den


def widen_bias(bias, n_prev, nj):
    band = (n_prev + 1) * CHUNK
    wk = (nj + n_prev) * CHUNK
    rows = [jnp.pad(bias, ((0, 0), (0, 0), (j * CHUNK, wk - band - j * CHUNK)), constant_values=-jnp.inf)
            for j in range(nj)]
    return jnp.concatenate(rows, axis=1)


def fold_bias(dbw, n_prev, nj):
    band = (n_prev + 1) * CHUNK
    acc = dbw[:, :CHUNK, :band]
    for j in range(1, nj):
        acc = acc + dbw[:, j * CHUNK:(j + 1) * CHUNK, j * CHUNK:j * CHUNK + band]
    return acc


def attn_fwd(q, k, v, bias_w, sinks, *, n_prev, name, rider=None):
    S = q.shape[0]
    pad = n_prev * CHUNK
    kv_rows = k.shape[0]
    lead = kv_rows - S - pad
    tq = min(ATT_TQ, S)
    wk = tq + pad
    assert bias_w.shape == (N_HEADS, tq, wk), bias_w.shape
    has_sink = sinks is not None

    r_in, r_out, r_shapes, r_sems, r_args = _rider_parts(rider)
    n_own = 5 if has_sink else 4
    n_p, n_i = N_HEADS // 2, S // tq

    def body(*refs):
        q_ref, k_ref, v_ref, bias_ref = refs[:4]
        sink_ref = refs[4] if has_sink else None
        o_ref = refs[n_own + len(r_in)]
        if rider is not None:
            p_id, i_id = pl.program_id(0), pl.program_id(1)
            _ride(rider, refs[n_own:n_own + len(r_in)], refs[n_own + len(r_in) + 1:n_own + len(r_in) + 1 + len(r_out)],
                  refs[n_own + len(r_in) + 1 + len(r_out):],
                  jnp.logical_and(p_id == 0, i_id == 0), jnp.logical_and(p_id == n_p - 1, i_id == 0),
                  jnp.logical_and(p_id == n_p - 1, i_id == n_i - 1))
        start = pl.multiple_of(pl.program_id(1) * tq, tq)
        head0 = lax.broadcasted_iota(jnp.int32, (1, LANES), 1) < HEAD_DIM
        valid = lax.broadcasted_iota(jnp.int32, (1, wk), 1) + start >= pad
        kb = k_ref[pl.ds(pl.multiple_of(start + lead, CHUNK), wk), :]
        vb = v_ref[pl.ds(pl.multiple_of(start + lead, CHUNK), wk), :]
        qk = lax.dot_general(_stack_heads(q_ref[...].astype(F32), head0), kb, _NT, preferred_element_type=F32)
        ps = []
        for r in range(2):
            snk = sink_ref[0, r:r + 1, 0:1] if has_sink else None
            ps.append(_attn_probs(qk[r * tq:(r + 1) * tq, :], bias_ref[r], valid, snk)[0].astype(BF16))
        o2 = jnp.dot(jnp.concatenate(ps, axis=0), vb, preferred_element_type=F32)
        o_ref[...] = jnp.where(head0, o2[:tq, :], o2[tq:, :]).astype(o_ref.dtype)

    in_specs = [pl.BlockSpec((tq, LANES), lambda p, i: (i, p)),
                pl.BlockSpec((kv_rows, LANES), lambda p, i: (0, p)),
                pl.BlockSpec((kv_rows, LANES), lambda p, i: (0, p)),
                pl.BlockSpec((2, tq, wk), lambda p, i: (p, 0, 0))]
    args = [q, k, v, bias_w]
    if has_sink:
        in_specs.append(pl.BlockSpec((1, 2, LANES), lambda p, i: (p, 0, 0)))
        args.append(sinks)
    res = pl.pallas_call(
        body, name=name, grid=(n_p, n_i), in_specs=in_specs + r_in,
        out_specs=[pl.BlockSpec((tq, LANES), lambda p, i: (i, p))] + r_out,
        out_shape=[jax.ShapeDtypeStruct((S, N_HEADS * HEAD_DIM), BF16)] + r_shapes,
        scratch_shapes=r_sems,
        compiler_params=pltpu.CompilerParams(dimension_semantics=("arbitrary", "arbitrary"), vmem_limit_bytes=VMEM_LIMIT,
                                             has_side_effects=rider is not None),
    )(*args, *r_args)
    return res[0], res[1:]


def attn_bwd(q, k, v, do, bias_w, sinks, *, n_prev, name, rider=None):
    S = q.shape[0]
    pad = n_prev * CHUNK
    kv_rows = k.shape[0]
    lead = kv_rows - S - pad
    tq = min(ATT_TQ, S)
    wk = tq + pad
    assert bias_w.shape == (N_HEADS, tq, wk), bias_w.shape
    has_sink = sinks is not None
    scale = HEAD_DIM ** -0.5

    r_in, r_out, r_shapes, r_sems, r_args = _rider_parts(rider)
    n_own_in = 6 if has_sink else 5
    n_own_out = 5 if has_sink else 4
    n_p, n_i = N_HEADS // 2, S // tq

    def body(*refs):
        q_ref, k_ref, v_ref, do_ref, bias_ref = refs[:5]
        sink_ref = refs[5] if has_sink else None
        o0 = n_own_in + len(r_in)
        dq_ref, dk_ref, dv_ref, db_ref = refs[o0:o0 + 4]
        dsk_ref = refs[o0 + 4] if has_sink else None
        i = pl.program_id(1)
        if rider is not None:
            p_id = pl.program_id(0)
            _ride(rider, refs[n_own_in:o0], refs[o0 + n_own_out:o0 + n_own_out + len(r_out)],
                  refs[o0 + n_own_out + len(r_out):],
                  jnp.logical_and(p_id == 0, i == 0), jnp.logical_and(p_id == n_p // 2, i == 0),
                  jnp.logical_and(p_id == n_p - 1, i == n_i - 1))

        @pl.when(i == 0)
        def _():
            dk_ref[...] = jnp.zeros_like(dk_ref)
            dv_ref[...] = jnp.zeros_like(dv_ref)
            db_ref[...] = jnp.zeros_like(db_ref)
            if has_sink:
                dsk_ref[...] = jnp.zeros_like(dsk_ref)

        start = pl.multiple_of(i * tq, tq)
        head0 = lax.broadcasted_iota(jnp.int32, (1, LANES), 1) < HEAD_DIM
        valid = lax.broadcasted_iota(jnp.int32, (1, wk), 1) + start >= pad
        kb = k_ref[pl.ds(pl.multiple_of(start + lead, CHUNK), wk), :]
        vb = v_ref[pl.ds(pl.multiple_of(start + lead, CHUNK), wk), :]
        q2 = _stack_heads(q_ref[...].astype(F32), head0)
        do2 = _stack_heads(do_ref[...].astype(F32), head0)
        qk = lax.dot_general(q2, kb, _NT, preferred_element_type=F32)
        dp2 = lax.dot_general(do2, vb, _NT, preferred_element_type=F32)
        pbs, dss = [], []
        for r in range(2):
            rows = slice(r * tq, (r + 1) * tq)
            snk = sink_ref[0, r:r + 1, 0:1] if has_sink else None
            p, ps = _attn_probs(qk[rows, :], bias_ref[r], valid, snk)
            dp = dp2[rows, :]
            delta = jnp.sum(p * dp, axis=1, keepdims=True)
            ds = p * (dp - delta)
            db_ref[r] += ds
            if has_sink:
                dsk = -jnp.sum(ps * delta, axis=0, keepdims=True)
                dsk_ref[0, r:r + 1, :] += jnp.broadcast_to(dsk, (1, LANES))
            pbs.append(p.astype(BF16))
            dss.append(ds.astype(BF16))
        ds2 = jnp.concatenate(dss, axis=0)
        dq2 = jnp.dot(ds2, kb, preferred_element_type=F32) * scale
        dq_ref[...] = jnp.where(head0, dq2[:tq, :], dq2[tq:, :])
        dk_ref[pl.ds(pl.multiple_of(start + lead, CHUNK), wk), :] += lax.dot_general(ds2, q2, _TN, preferred_element_type=F32) * scale
        dv_ref[pl.ds(pl.multiple_of(start + lead, CHUNK), wk), :] += lax.dot_general(jnp.concatenate(pbs, axis=0), do2, _TN,
                                                       preferred_element_type=F32)

    row_spec = pl.BlockSpec((tq, LANES), lambda p, i: (i, p))
    kv_spec = pl.BlockSpec((kv_rows, LANES), lambda p, i: (0, p))
    bias_spec = pl.BlockSpec((2, tq, wk), lambda p, i: (p, 0, 0))
    sink_spec = pl.BlockSpec((1, 2, LANES), lambda p, i: (p, 0, 0))
    in_specs = [row_spec, kv_spec, kv_spec, row_spec, bias_spec]
    args = [q, k, v, do, bias_w]
    out_specs = [row_spec, kv_spec, kv_spec, bias_spec]
    W = N_HEADS * HEAD_DIM
    out_shape = [jax.ShapeDtypeStruct((S, W), F32), jax.ShapeDtypeStruct((kv_rows, W), F32),
                 jax.ShapeDtypeStruct((kv_rows, W), F32), jax.ShapeDtypeStruct((N_HEADS, tq, wk), F32)]
    if has_sink:
        in_specs.append(sink_spec)
        args.append(sinks)
        out_specs.append(sink_spec)
        out_shape.append(jax.ShapeDtypeStruct((N_HEADS // 2, 2, LANES), F32))
    res = pl.pallas_call(
        body, name=name, grid=(n_p, n_i), in_specs=in_specs + r_in, out_specs=out_specs + r_out,
        out_shape=out_shape + r_shapes, scratch_shapes=r_sems,
        compiler_params=pltpu.CompilerParams(dimension_semantics=("arbitrary", "arbitrary"), vmem_limit_bytes=VMEM_LIMIT,
                                             has_side_effects=rider is not None),
    )(*args, *r_args)
    return res[:n_own_out], res[n_own_out:]


HP = SSM_HEADS // 2
PAIRS_PER_GROUP = HP // SSM_GROUPS
HEADS_PER_GROUP = SSM_HEADS // SSM_GROUPS
GW = HEADS_PER_GROUP * 64


def _ssd_dt(dtraw, dtb, A, tril):
    lane = lax.broadcasted_iota(jnp.int32, (1, LANES), 1)
    u = dtraw + dtb
    eu = jnp.exp(-jnp.abs(u))
    w1 = 1.0 + eu
    l1p = jnp.where(w1 == 1.0, eu, jnp.log(w1) * eu / jnp.where(w1 == 1.0, 1.0, w1 - 1.0))
    dt = jnp.where(lane < SSM_HEADS, jnp.maximum(u, 0.0) + l1p, 0.0)
    acs = jnp.dot(tril, dt * A, precision=HI, preferred_element_type=F32)
    return u, dt, acs


def _head_expander():
    hw = D_INNER // SSM_HEADS
    return (np.arange(LANES)[:, None] == np.arange(D_INNER)[None, :] // hw).astype(np.float32)


def _select_dot(t, sel):
    hi = t.astype(BF16)
    lo = (t - hi.astype(F32)).astype(BF16)
    return jnp.dot(hi, sel, preferred_element_type=F32) + jnp.dot(lo, sel, preferred_element_type=F32)


def ssd_fwd(xbc, dtraw, dtb, A, dexp, *, name):
    S = xbc.shape[0]
    L = min(SSD_L, S)
    nc = S // L
    N = SSM_STATE
    e_mat = jnp.asarray(_head_expander(), dtype=BF16)

    def body(xs_ref, b_ref, c_ref, dtr_ref, dtb_ref, a_ref, d_ref, e_ref, y_ref, st_out_ref, st_ref, xw_ref):
        c = pl.program_id(0)

        @pl.when(c == 0)
        def _():
            st_ref[...] = jnp.zeros_like(st_ref)

        st_out_ref[0] = st_ref[...]
        ri = lax.broadcasted_iota(jnp.int32, (L, L), 0)
        ci = lax.broadcasted_iota(jnp.int32, (L, L), 1)
        trilb = ri >= ci
        head0 = lax.broadcasted_iota(jnp.int32, (1, LANES), 1) < 64
        _, dt, acs = _ssd_dt(dtr_ref[...], dtb_ref[...], a_ref[...], trilb.astype(F32))
        acsT = acs.T
        last = acs[L - 1:L, :]
        expand = lambda t: _select_dot(t, e_ref[...])
        dte, eae, wte = expand(dt), expand(jnp.exp(acs)), expand(jnp.exp(last - acs) * dt)
        lasts = [last[:, h:h + 1] for h in range(SSM_HEADS)]
        for g in range(SSM_GROUPS):
            Bg = b_ref[:, g * N:(g + 1) * N].astype(BF16)
            Cg = c_ref[:, g * N:(g + 1) * N].astype(BF16)
            CB = lax.dot_general(Cg, Bg, _NT, preferred_element_type=F32)
            Z = lax.dot_general(Cg, st_ref[g * GW:(g + 1) * GW, :].astype(BF16), _NT, preferred_element_type=F32)
            for q in range(PAIRS_PER_GROUP):
                hp = g * PAIRS_PER_GROUP + q
                sl = slice(hp * LANES, (hp + 1) * LANES)
                xs = xs_ref[:, sl]
                xd = xs * dte[:, sl]
                ms, xh = [], []
                for r in range(2):
                    h = 2 * hp + r
                    dec = jnp.exp(jnp.where(trilb, acs[:, h:h + 1] - acsT[h:h + 1, :], -jnp.inf))
                    ms.append((CB * dec).astype(BF16))
                    xh.append(jnp.where(head0 if r == 0 else jnp.logical_not(head0), xd, 0.0).astype(BF16))
                yi = jnp.dot(jnp.concatenate(ms, axis=1), jnp.concatenate(xh, axis=0), preferred_element_type=F32)
                y_ref[:, sl] = yi + Z[:, q * LANES:(q + 1) * LANES] * eae[:, sl] + d_ref[:, sl] * xs
                xw_ref[:, sl] = (xs * wte[:, sl]).astype(BF16)
        for g in range(SSM_GROUPS):
            Bg = b_ref[:, g * N:(g + 1) * N].astype(BF16)
            sn = lax.dot_general(xw_ref[:, g * GW:(g + 1) * GW], Bg, _TN, preferred_element_type=F32)
            for k in range(HEADS_PER_GROUP):
                h = g * HEADS_PER_GROUP + k
                rows = slice(h * 64, (h + 1) * 64)
                st_ref[rows, :] = st_ref[rows, :] * jnp.exp(lasts[h]) + sn[k * 64:(k + 1) * 64, :]

    return pl.pallas_call(
        body, name=name, grid=(nc,),
        in_specs=[pl.BlockSpec((L, D_INNER), lambda c: (c, 0)),
                  pl.BlockSpec((L, SSM_GROUPS * N), lambda c: (c, D_INNER // (SSM_GROUPS * N))),
                  pl.BlockSpec((L, SSM_GROUPS * N), lambda c: (c, D_INNER // (SSM_GROUPS * N) + 1)),
                  pl.BlockSpec((L, LANES), lambda c: (c, 0)),
                  pl.BlockSpec((1, LANES), lambda c: (0, 0)),
                  pl.BlockSpec((1, LANES), lambda c: (0, 0)),
                  pl.BlockSpec((1, D_INNER), lambda c: (0, 0)),
                  pl.BlockSpec((LANES, D_INNER), lambda c: (0, 0))],
        out_specs=[pl.BlockSpec((L, D_INNER), lambda c: (c, 0)),
                   pl.BlockSpec((1, D_INNER, N), lambda c: (c, 0, 0))],
        out_shape=[jax.ShapeDtypeStruct((S, D_INNER), F32), jax.ShapeDtypeStruct((nc, D_INNER, N), F32)],
        scratch_shapes=[pltpu.VMEM((D_INNER, N), F32), pltpu.VMEM((L, D_INNER), BF16)],
        compiler_params=_params("arbitrary"),
    )(xbc, xbc, xbc, dtraw, dtb, A, dexp, e_mat)


def ssd_bwd(xbc, dtraw, dtb, A, dexp, states, dy, *, name, rider=None):
    S = xbc.shape[0]
    L = min(SSD_L, S)
    nc = S // L
    N = SSM_STATE
    e_np = _head_expander()
    e_mat, et_mat = jnp.asarray(e_np, dtype=BF16), jnp.asarray(e_np.T, dtype=BF16)

    r_in, r_out, r_shapes, r_sems, r_args = _rider_parts(rider)

    def body(*refs):
        xs_ref, b_ref, c_ref, dtr_ref, dtb_ref, a_ref, d_ref, e_ref, et_ref, st_in_ref, dy_ref = refs[:11]
        o0 = 11 + len(r_in)
        dxs_ref, db_ref, dc_ref, ddtr_ref, da_ref, ddtb_ref, dd_ref = refs[o0:o0 + 7]
        s0 = o0 + 7 + len(r_out)
        dst_ref, xw_ref, dz_ref, r_ref, dsr_ref, dsc_ref = refs[s0:s0 + 6]
        step = pl.program_id(0)
        if rider is not None:
            _ride(rider, refs[11:o0], refs[o0 + 7:s0], refs[s0 + 6:], step == 0, step == nc // 2, step == nc - 1)

        @pl.when(step == 0)
        def _():
            dst_ref[...] = jnp.zeros_like(dst_ref)
            dsr_ref[...] = jnp.zeros_like(dsr_ref)
            dsc_ref[...] = jnp.zeros_like(dsc_ref)
            da_ref[...] = jnp.zeros_like(da_ref)
            ddtb_ref[...] = jnp.zeros_like(ddtb_ref)
            dd_ref[...] = jnp.zeros_like(dd_ref)

        ri = lax.broadcasted_iota(jnp.int32, (L, L), 0)
        ci = lax.broadcasted_iota(jnp.int32, (L, L), 1)
        trilb = ri >= ci
        lane = lax.broadcasted_iota(jnp.int32, (1, LANES), 1)
        sub = lax.broadcasted_iota(jnp.int32, (LANES, 1), 0)
        head0 = lane < 64
        A = a_ref[...]
        u, dt, acs = _ssd_dt(dtr_ref[...], dtb_ref[...], A, trilb.astype(F32))
        acsT = acs.T
        last = acs[L - 1:L, :]
        elast = jnp.exp(last)
        er = jnp.exp(last - acs)
        wt = er * dt
        expand = lambda t: _select_dot(t, e_ref[...])
        dte, eae, wte = expand(dt), expand(jnp.exp(acs)), expand(wt)
        dlast = jnp.zeros((1, LANES), F32)
        dcbs = []
        for g in range(SSM_GROUPS):
            Bg = b_ref[:, g * N:(g + 1) * N].astype(BF16)
            Cg = c_ref[:, g * N:(g + 1) * N].astype(BF16)
            stg = st_in_ref[0, g * GW:(g + 1) * GW, :]
            dstg = dst_ref[g * GW:(g + 1) * GW, :]
            CB = lax.dot_general(Cg, Bg, _NT, preferred_element_type=F32)
            CBT = lax.dot_general(Bg, Cg, _NT, preferred_element_type=F32)
            Z = lax.dot_general(Cg, stg.astype(BF16), _NT, preferred_element_type=F32)
            U = lax.dot_general(Bg, dstg.astype(BF16), _NT, preferred_element_type=F32)
            dcb = jnp.zeros((L, L), F32)
            for q in range(PAIRS_PER_GROUP):
                hp = g * PAIRS_PER_GROUP + q
                sl = slice(hp * LANES, (hp + 1) * LANES)
                qs = slice(q * LANES, (q + 1) * LANES)
                xs = xs_ref[:, sl]
                dyp = dy_ref[:, sl]
                dtp, eap, wp, Dp = dte[:, sl], eae[:, sl], wte[:, sl], d_ref[:, sl]
                xd = xs * dtp
                dy2 = jnp.concatenate([jnp.where(head0, dyp, 0.0), jnp.where(head0, 0.0, dyp)], axis=0).astype(BF16)
                G2 = lax.dot_general(dy2, xd.astype(BF16), _NT, preferred_element_type=F32)
                mts = []
                for r in range(2):
                    h = 2 * hp + r
                    seg = acs[:, h:h + 1] - acsT[h:h + 1, :]
                    dec = jnp.exp(jnp.where(trilb, seg, -jnp.inf))
                    decT = jnp.exp(jnp.where(ri <= ci, -seg, -jnp.inf))
                    gd = G2[r * L:(r + 1) * L, :] * dec
                    dcb = dcb + gd
                    dseg = gd * CB
                    dsr_ref[:, h:h + 1] = jnp.sum(dseg, axis=1, keepdims=True)
                    dsc_ref[h:h + 1, :] = jnp.sum(dseg, axis=0, keepdims=True)
                    mts.append((CBT * decT).astype(BF16))
                dxd = jnp.dot(jnp.concatenate(mts, axis=1), dy2, preferred_element_type=F32)
                Up = U[:, qs]
                r_ref[0:L, sl] = dyp * Z[:, qs] * eap
                r_ref[L:2 * L, sl] = dxd * xs
                r_ref[2 * L:3 * L, sl] = Up * xs
                dz_ref[:, sl] = (dyp * eap).astype(BF16)
                xw_ref[:, sl] = (xs * wp).astype(BF16)
                dxs_ref[:, sl] = dxd * dtp + Dp * dyp + Up * wp
                dd_ref[:, sl] += jnp.sum(dyp * xs, axis=0, keepdims=True)
            dcbs.append(dcb)
            t = dstg * stg
            for k in range(HEADS_PER_GROUP):
                dlast = dlast + jnp.where(lane == g * HEADS_PER_GROUP + k,
                                          jnp.sum(t[k * 64:(k + 1) * 64, :], keepdims=True), 0.0)
        fold = lambda k: _select_dot(r_ref[k * L:(k + 1) * L, :], et_ref[...])
        r1, r2, dws = fold(0), fold(1), fold(2)
        dww = dws * wt
        ddt = r2 + dws * er
        dacs = r1 - dww + dsr_ref[...] - dsc_ref[...].T
        dlast = dlast * elast + jnp.sum(dww, axis=0, keepdims=True)
        lasts = [last[:, h:h + 1] for h in range(SSM_HEADS)]
        for g in range(SSM_GROUPS):
            Bg = b_ref[:, g * N:(g + 1) * N].astype(BF16)
            Cg = c_ref[:, g * N:(g + 1) * N].astype(BF16)
            gs = slice(g * GW, (g + 1) * GW)
            stb = st_in_ref[0, gs, :].astype(BF16)
            dstb = dst_ref[gs, :].astype(BF16)
            dcbb = dcbs[g].astype(BF16)
            dzg = dz_ref[:, gs]
            dc_ref[:, g * N:(g + 1) * N] = (jnp.dot(dzg, stb, preferred_element_type=F32)
                                            + jnp.dot(dcbb, Bg, preferred_element_type=F32))
            db_ref[:, g * N:(g + 1) * N] = (jnp.dot(xw_ref[:, gs], dstb, preferred_element_type=F32)
                                            + lax.dot_general(dcbb, Cg, _TN, preferred_element_type=F32))
            dsn = lax.dot_general(dzg, Cg, _TN, preferred_element_type=F32)
            for k in range(HEADS_PER_GROUP):
                h = g * HEADS_PER_GROUP + k
                rows = slice(h * 64, (h + 1) * 64)
                dst_ref[rows, :] = dst_ref[rows, :] * jnp.exp(lasts[h]) + dsn[k * 64:(k + 1) * 64, :]
        rowi = lax.broadcasted_iota(jnp.int32, (L, 1), 0)
        dacs = dacs + jnp.where(rowi == L - 1, dlast, 0.0)
        da = jnp.dot((ci >= ri).astype(F32), dacs, precision=HI, preferred_element_type=F32)
        ddt = ddt + da * A
        da_ref[...] += jnp.sum(da * dt, axis=0, keepdims=True)
        ddtr = jnp.where(lane < SSM_HEADS, ddt * _sigmoid(u), 0.0)
        ddtr_ref[...] = ddtr
        ddtb_ref[...] += jnp.sum(ddtr, axis=0, keepdims=True)

    rev = lambda c: nc - 1 - c
    gn = SSM_GROUPS * N
    res = pl.pallas_call(
        body, name=name, grid=(nc,),
        in_specs=[pl.BlockSpec((L, D_INNER), lambda c: (rev(c), 0)),
                  pl.BlockSpec((L, gn), lambda c: (rev(c), D_INNER // gn)),
                  pl.BlockSpec((L, gn), lambda c: (rev(c), D_INNER // gn + 1)),
                  pl.BlockSpec((L, LANES), lambda c: (rev(c), 0)),
                  pl.BlockSpec((1, LANES), lambda c: (0, 0)),
                  pl.BlockSpec((1, LANES), lambda c: (0, 0)),
                  pl.BlockSpec((1, D_INNER), lambda c: (0, 0)),
                  pl.BlockSpec((LANES, D_INNER), lambda c: (0, 0)),
                  pl.BlockSpec((D_INNER, LANES), lambda c: (0, 0)),
                  pl.BlockSpec((1, D_INNER, N), lambda c: (rev(c), 0, 0)),
                  pl.BlockSpec((L, D_INNER), lambda c: (rev(c), 0))] + r_in,
        out_specs=[pl.BlockSpec((L, D_INNER), lambda c: (rev(c), 0)),
                   pl.BlockSpec((L, gn), lambda c: (rev(c), 0)),
                   pl.BlockSpec((L, gn), lambda c: (rev(c), 0)),
                   pl.BlockSpec((L, LANES), lambda c: (rev(c), 0)),
                   pl.BlockSpec((1, LANES), lambda c: (0, 0)),
                   pl.BlockSpec((1, LANES), lambda c: (0, 0)),
                   pl.BlockSpec((1, D_INNER), lambda c: (0, 0))] + r_out,
        out_shape=[jax.ShapeDtypeStruct((S, D_INNER), F32), jax.ShapeDtypeStruct((S, gn), F32),
                   jax.ShapeDtypeStruct((S, gn), F32), jax.ShapeDtypeStruct((S, LANES), F32),
                   jax.ShapeDtypeStruct((1, LANES), F32), jax.ShapeDtypeStruct((1, LANES), F32),
                   jax.ShapeDtypeStruct((1, D_INNER), F32)] + r_shapes,
        scratch_shapes=[pltpu.VMEM((D_INNER, N), F32), pltpu.VMEM((L, D_INNER), BF16), pltpu.VMEM((L, D_INNER), BF16),
                        pltpu.VMEM((3 * L, D_INNER), F32), pltpu.VMEM((L, LANES), F32), pltpu.VMEM((LANES, L), F32)]
                       + r_sems,
        compiler_params=pltpu.CompilerParams(dimension_semantics=("arbitrary",), vmem_limit_bytes=VMEM_LIMIT,
                                             has_side_effects=rider is not None),
    )(xbc, xbc, xbc, dtraw, dtb, A, dexp, e_mat, et_mat, states, dy, *r_args)
    return res[:7], res[7:]


BAND_A = (A_PREV + 1) * CHUNK
REL_W = 640


def _relpos_select():
    k = np.arange(REL_W)
    rel = np.where(k < BAND_A, A_PREV * CHUNK - k, A_PREV * CHUNK - (k - REL_W))
    idx = np.clip(rel, -MAX_REL, MAX_REL) + MAX_REL
    sel = (np.arange(REL_W)[:, None] == idx[None, :]) & (k != BAND_A)[None, :]
    return sel.astype(np.float32)


def relpos_bias(table_pad, *, name):
    def body(t_ref, s_ref, o_ref):
        v = jnp.dot(t_ref[...], s_ref[...], precision=HI, preferred_element_type=F32)
        for h in range(N_HEADS):
            o_ref[h] = pltpu.roll(jnp.broadcast_to(v[h:h + 1, :], (CHUNK, REL_W)), 0, 1, stride=1, stride_axis=0)

    return pl.pallas_call(body, name=name, out_shape=jax.ShapeDtypeStruct((N_HEADS, CHUNK, REL_W), F32),
                          compiler_params=pltpu.CompilerParams(vmem_limit_bytes=VMEM_LIMIT),
                          )(table_pad, jnp.asarray(_relpos_select()))


def relpos_grad(dbias_rev, *, name):
    def body(d_ref, s_ref, o_ref):
        head = lax.broadcasted_iota(jnp.int32, (N_HEADS, 1), 0)
        dv = jnp.zeros((N_HEADS, REL_W), F32)
        for h in range(N_HEADS):
            back = pltpu.roll(d_ref[h], REL_W - (CHUNK - 1), 1, stride=1, stride_axis=0)
            dv = dv + jnp.where(head == h, jnp.sum(back, axis=0, keepdims=True), 0.0)
        o_ref[...] = lax.dot_general(dv, s_ref[...], _NT, precision=HI, preferred_element_type=F32)

    return pl.pallas_call(body, name=name, out_shape=jax.ShapeDtypeStruct((N_HEADS, REL_W), F32),
                          compiler_params=pltpu.CompilerParams(vmem_limit_bytes=VMEM_LIMIT),
                          )(dbias_rev, jnp.asarray(_relpos_select()))


def matmul_loss(a, b, residual, t, *, name):
    (M, K), (_, D) = a.shape, b.shape
    tm = _pick(M, MM_ROWS)

    def body(a_ref, b_ref, r_ref, t_ref, dy_ref, l_ref):
        y = jnp.dot(a_ref[...].astype(BF16), b_ref[...].astype(BF16), preferred_element_type=F32) + r_ref[...]
        e = y - t_ref[...]
        dy_ref[...] = e * (1.0 / D)

        @pl.when(pl.program_id(0) == 0)
        def _():
            l_ref[...] = jnp.zeros_like(l_ref)

        part = jnp.sum(jnp.sum(e * e, axis=1, keepdims=True), axis=0, keepdims=True) * (0.5 / D)
        l_ref[...] += jnp.broadcast_to(part, l_ref.shape)

    row = pl.BlockSpec((tm, D), lambda i: (i, 0))
    return pl.pallas_call(
        body, name=name, grid=(M // tm,),
        in_specs=[pl.BlockSpec((tm, K), lambda i: (i, 0)), pl.BlockSpec((K, D), lambda i: (0, 0)), row, row],
        out_specs=[row, pl.BlockSpec((1, LANES), lambda i: (0, 0))],
        out_shape=[jax.ShapeDtypeStruct((M, D), F32), jax.ShapeDtypeStruct((1, LANES), F32)],
        compiler_params=_params("arbitrary"),
    )(a, b, residual, t)


def f_adamw(w, g, m, v):
    m = ADAM_B1 * m + (1.0 - ADAM_B1) * g
    v = ADAM_B2 * v + (1.0 - ADAM_B2) * (g * g)
    m_hat = m / (1.0 - ADAM_B1 ** ADAM_STEP)
    v_hat = v / (1.0 - ADAM_B2 ** ADAM_STEP)
    delta = -ADAM_LR * (m_hat / (jnp.sqrt(v_hat) + ADAM_EPS) + ADAM_WD * w)
    return delta, m, v


ANY = pl.BlockSpec(memory_space=pl.ANY)


def _pos():
    return lax.axis_index("x"), lax.axis_index("y"), lax.axis_index("c")


def _other_chips(x, y):
    return [(1 - x, y), (x, 1 - y), (1 - x, 1 - y)]


class Rider:
    def __init__(self, ins, outs, sems, start, mid, finish):
        self.ins, self.outs, self.sems = list(ins), list(outs), list(sems)
        self.start, self.mid, self.finish = start, mid, finish


def _rider_parts(rider):
    if rider is None:
        return [], [], [], [], []
    return [ANY] * len(rider.ins), [ANY] * len(rider.outs), rider.outs, rider.sems, rider.ins


def _ride(rider, ins, outs, sems, first, mid, last):
    pos = _pos()

    @pl.when(first)
    def _():
        rider.start(pos, ins, outs, sems)

    if rider.mid is not None:
        @pl.when(mid)
        def _():
            rider.mid(pos, ins, outs, sems)

    @pl.when(last)
    def _():
        rider.finish(pos, ins, outs, sems)


def run_rider(rider, *, name):
    n_in, n_out = len(rider.ins), len(rider.outs)

    def body(*refs):
        ins, outs, sems = refs[:n_in], refs[n_in:n_in + n_out], refs[n_in + n_out:]
        pos = _pos()
        rider.start(pos, ins, outs, sems)
        if rider.mid is not None:
            rider.mid(pos, ins, outs, sems)
        rider.finish(pos, ins, outs, sems)

    return pl.pallas_call(
        body, name=name, in_specs=[ANY] * n_in, out_specs=[ANY] * n_out, out_shape=rider.outs,
        scratch_shapes=rider.sems, compiler_params=pltpu.CompilerParams(has_side_effects=True),
    )(*rider.ins)


def gather_rider(shards):
    n = len(shards)

    def copies(pos, ins, outs, sems):
        x, y, c = pos
        send, recv, fsend, frecv = sems
        me = 2 * x + y
        sib = (x, y, 1 - c)
        first, arrive, passed, theirs = [], [], [], []
        for i in range(n):
            for j, (px, py) in enumerate(_other_chips(x, y)):
                k = 3 * i + j
                far = dict(device_id=(px, py, c), device_id_type=MESH)
                near = dict(device_id=sib, device_id_type=MESH)
                got = outs[i].at[2 * px + py, c]
                his = outs[i].at[2 * px + py, 1 - c]
                first.append(pltpu.make_async_remote_copy(ins[i].at[c], outs[i].at[me, c], send.at[k], recv.at[k], **far))
                arrive.append(pltpu.make_async_remote_copy(ins[i].at[c], got, send.at[k], recv.at[k], **far))
                passed.append(pltpu.make_async_remote_copy(got, got, fsend.at[k], frecv.at[k], **near))
                theirs.append(pltpu.make_async_remote_copy(his, his, fsend.at[k], frecv.at[k], **near))
        return first, arrive, passed, theirs

    def start(*a):
        for cp in copies(*a)[0]:
            cp.start()

    def mid(*a):
        _, arrive, passed, _ = copies(*a)
        for got, cp in zip(arrive, passed):
            got.wait_recv()
            cp.start()

    def finish(*a):
        first, _, passed, theirs = copies(*a)
        for cp in theirs:
            cp.wait_recv()
        for cp in first + passed:
            cp.wait_send()

    return Rider(shards, [jax.ShapeDtypeStruct((4,) + s.shape, s.dtype) for s in shards],
                 [pltpu.SemaphoreType.DMA((3 * n,))] * 4, start, mid, finish)


def scatter_rider(ps):
    n = len(ps)

    def copies(pos, ins, outs, sems):
        x, y, c = pos
        send, recv = sems
        return [pltpu.make_async_remote_copy(ins[i].at[2 * px + py], outs[i].at[j], send.at[3 * i + j], recv.at[3 * i + j],
                                             device_id=(px, py, c), device_id_type=MESH)
                for i in range(n) for j, (px, py) in enumerate(_other_chips(x, y))]

    def start(*a):
        for cp in copies(*a):
            cp.start()

    def finish(*a):
        for cp in copies(*a):
            cp.wait()

    return Rider(ps, [jax.ShapeDtypeStruct((3,) + p.shape[1:], p.dtype) for p in ps],
                 [pltpu.SemaphoreType.DMA((3 * n,))] * 2, start, None, finish)


def pair_swap_halves(gs, *, name):
    n = len(gs)

    def body(*refs):
        ins, outs = refs[:n], refs[n:2 * n]
        send, recv = refs[2 * n:]
        x, y, c = _pos()
        cps = []
        for i in range(n):
            cp = pltpu.make_async_remote_copy(ins[i].at[1 - c], outs[i], send.at[i], recv.at[i],
                                              device_id=(x, y, 1 - c), device_id_type=MESH)
            cp.start()
            cps.append(cp)
        for cp in cps:
            cp.wait()

    return pl.pallas_call(
        body, name=name, in_specs=[ANY] * n, out_specs=[ANY] * n,
        out_shape=[jax.ShapeDtypeStruct(g.shape[1:], g.dtype) for g in gs],
        scratch_shapes=[pltpu.SemaphoreType.DMA((n,)), pltpu.SemaphoreType.DMA((n,))],
        compiler_params=pltpu.CompilerParams(has_side_effects=True),
    )(*gs)


def pair_share(hs, *, name):
    n = len(hs)

    def body(*refs):
        ins, outs = refs[:n], refs[n:2 * n]
        send, recv = refs[2 * n:]
        x, y, c = _pos()
        cps = []
        for i in range(n):
            cp = pltpu.make_async_remote_copy(ins[i], outs[i].at[c], send.at[i], recv.at[i],
                                              device_id=(x, y, 1 - c), device_id_type=MESH)
            cp.start()
            cps.append(cp)
        for i, cp in enumerate(cps):
            cp.wait_send()
            pltpu.make_async_remote_copy(ins[i], outs[i].at[1 - c], send.at[i], recv.at[i],
                                         device_id=(x, y, 1 - c), device_id_type=MESH).wait_recv()

    return pl.pallas_call(
        body, name=name, in_specs=[ANY] * n, out_specs=[ANY] * n,
        out_shape=[jax.ShapeDtypeStruct((2,) + h.shape, h.dtype) for h in hs],
        scratch_shapes=[pltpu.SemaphoreType.DMA((n,)), pltpu.SemaphoreType.DMA((n,))],
        compiler_params=pltpu.CompilerParams(has_side_effects=True),
    )(*hs)


def gather_all(buf, *, name):
    def body(in_ref, out_ref, send, recv, loc):
        x, y, c = _pos()
        lid = 4 * x + 2 * y + c
        lc = pltpu.make_async_copy(in_ref, out_ref.at[lid], loc.at[0])
        lc.start()
        cps = []
        for k in range(1, 8):
            px = 1 - x if k & 4 else x
            py = 1 - y if k & 2 else y
            pc = 1 - c if k & 1 else c
            cp = pltpu.make_async_remote_copy(in_ref, out_ref.at[lid], send.at[k - 1], recv.at[k - 1],
                                              device_id=(px, py, pc), device_id_type=MESH)
            cp.start()
            cps.append((cp, 4 * px + 2 * py + pc, (px, py, pc)))
        for k, (cp, plid, peer) in enumerate(cps):
            cp.wait_send()
            pltpu.make_async_remote_copy(in_ref, out_ref.at[plid], send.at[k], recv.at[k],
                                         device_id=peer, device_id_type=MESH).wait_recv()
        lc.wait()

    return pl.pallas_call(
        body, name=name, in_specs=[ANY], out_specs=ANY,
        out_shape=jax.ShapeDtypeStruct((8,) + buf.shape, buf.dtype),
        scratch_shapes=[pltpu.SemaphoreType.DMA((7,)), pltpu.SemaphoreType.DMA((7,)), pltpu.SemaphoreType.DMA((1,))],
        compiler_params=pltpu.CompilerParams(has_side_effects=True),
    )(buf)


def sum_slots(a, *, name):
    n = a.shape[0]

    def body(a_ref, o_ref):
        acc = a_ref[0]
        for k in range(1, n):
            acc = acc + a_ref[k]
        o_ref[...] = acc

    return pl.pallas_call(body, name=name, out_shape=jax.ShapeDtypeStruct(a.shape[1:], a.dtype),
                          compiler_params=pltpu.CompilerParams(vmem_limit_bytes=VMEM_LIMIT))(a)


def _row_tile(r, want, mult=16):
    t = (min(want, r) // mult) * mult
    while t >= mult:
        if r % t == 0:
            return t
        t -= mult
    return r


def pair_add(g, r1, csel, *, name):
    _, _, r, C = g.shape
    tr = _row_tile(r, 256)

    def body(g_ref, r_ref, c_ref, p32_ref, pb_ref):
        south = c_ref[0:1, 0:1] == 0.0
        p = jnp.where(south, g_ref[0, 0], g_ref[1, 0]) + r_ref[0]
        p32_ref[0] = p
        pb_ref[0] = p.astype(BF16)

    return pl.pallas_call(
        body, name=name, grid=(4, r // tr),
        in_specs=[pl.BlockSpec((2, 1, tr, C), lambda j, t: (0, j, t, 0)), pl.BlockSpec((1, tr, C), lambda j, t: (j, t, 0)),
                  pl.BlockSpec((1, LANES), lambda j, t: (0, 0))],
        out_specs=[pl.BlockSpec((1, tr, C), lambda j, t: (j, t, 0))] * 2,
        out_shape=[jax.ShapeDtypeStruct((4, r, C), F32), jax.ShapeDtypeStruct((4, r, C), BF16)],
        compiler_params=_params("parallel", "parallel"),
    )(g, r1, csel)


def chip_add(p32, r3, msel, *, name):
    _, r, C = p32.shape
    tr = _row_tile(r, 128)

    def body(p_ref, r_ref, m_ref, o_ref):
        me = m_ref[0:1, 0:1]
        acc = jnp.where(me == 0.0, p_ref[0], jnp.where(me == 1.0, p_ref[1], jnp.where(me == 2.0, p_ref[2], p_ref[3])))
        for j in range(3):
            acc = acc + r_ref[j].astype(F32)
        o_ref[...] = acc

    return pl.pallas_call(
        body, name=name, grid=(r // tr,),
        in_specs=[pl.BlockSpec((4, tr, C), lambda t: (0, t, 0)), pl.BlockSpec((3, tr, C), lambda t: (0, t, 0)),
                  pl.BlockSpec((1, LANES), lambda t: (0, 0))],
        out_specs=pl.BlockSpec((tr, C), lambda t: (t, 0)),
        out_shape=jax.ShapeDtypeStruct((r, C), F32),
        compiler_params=_params("parallel"),
    )(p32, r3, msel)


def _consts():
    i512 = np.arange(N_HEADS * HEAD_DIM)
    i128 = np.arange(LANES)
    bd512 = (i512[:, None] // HEAD_DIM == i512[None, :] // HEAD_DIM).astype(np.float32)
    bd128 = (i128[:, None] // HEAD_DIM == i128[None, :] // HEAD_DIM).astype(np.float32)
    fold = (np.arange(HEAD_DIM)[:, None] == (i512[None, :] % HEAD_DIM)).astype(np.float32)
    grp = N_HEADS // 2 * HEAD_DIM
    expand = ((i128[:, None] // HEAD_DIM == i512[None, :] // grp)
              & (i128[:, None] % HEAD_DIM == i512[None, :] % HEAD_DIM)).astype(np.float32)
    band = (B_PREV + 1) * CHUNK
    rel = np.arange(CHUNK)[:, None] - (np.arange(band)[None, :] - B_PREV * CHUNK)
    slopes = 2.0 ** (-8.0 * np.arange(1, N_HEADS + 1, dtype=np.float32) / N_HEADS)
    bias_b = (-slopes[:, None, None] * np.abs(rel).astype(np.float32)[None]).astype(np.float32)
    return [jnp.asarray(a) for a in (bd512, bd128, fold, expand)], jnp.asarray(bias_b)


def _ffn_fwd(xin, h, l, W, P, next_gain=None, target=None):
    Wi = W["ffn_in"][l]
    gate = matmul(h, Wi[:, :D_FF], mode="nn", name=f"ffn{l}_gate", out_dtype=BF16)
    up = matmul(h, Wi[:, D_FF:], mode="nn", name=f"ffn{l}_up", out_dtype=BF16)
    gc, act = dwconv_fwd(gate, P["ffn_conv_w"][l], P["ffn_conv_b"][l:l + 1], lambda y, u: (y, _silu(y) * u), [up],
                         [BF16, BF16], name=f"ffn{l}_conv")
    saved = (xin, h, gate, gc, up, act)
    if target is not None:
        dxout, lpart = matmul_loss(act, W["ffn_out"][l], xin, target, name=f"ffn{l}_out")
        return dxout, saved, lpart
    xout, h_next = matmul_norm(act, W["ffn_out"][l], xin, next_gain, name=f"ffn{l}_out")
    return xout, saved, h_next


def _ffn_bwd(dxout, l, saved, W, P):
    xin, h, gate, gc, up, act = saved
    g = P["norm_ffn"][l:l + 1]
    Wi = W["ffn_in"][l]
    dact = matmul(dxout, W["ffn_out"][l], mode="nt", name=f"ffn{l}_dact", out_dtype=BF16)
    dWo = matmul(act, dxout, mode="tn", name=f"ffn{l}_dwout")
    dgate, dcw, dcb, dup = dwconv_bwd(gate, P["ffn_conv_w"][l], [gc, up, dact],
                                      lambda c, u, da: (da * u * _dsilu(c), da * _silu(c)), [BF16],
                                      name=f"ffn{l}_dconv")
    dh = matmul(dgate, Wi[:, :D_FF], mode="nt", name=f"ffn{l}_dh_gate")
    dxin, dg = matmul_dnorm(dup, Wi[:, D_FF:], dh, xin, g, dxout, name=f"ffn{l}_dh_up")
    dWi = jnp.concatenate([matmul(h, dgate, mode="tn", name=f"ffn{l}_dw_gate"),
                           matmul(h, dup, mode="tn", name=f"ffn{l}_dw_up")], axis=1)
    return dxin, dWi, dWo, dg, dcw, dcb


class NoComm:
    def fwd_rider(self, tag):
        return None

    def fwd_done(self, tag, outs, W, P):
        pass

    def grads(self, tag, cols, rows):
        return None

    def bwd_done(self, tag, outs):
        pass


def local_step(x, tgt, W, P, comm):
    qk_consts, bias_b = _consts()
    DQ = N_HEADS * HEAD_DIM
    tile = _row_tile(x.shape[0], QK_ROWS)
    kv_lead = [-(-n * CHUNK // tile) * tile for n in (A_PREV, B_PREV)]

    g_mix0 = P["norm_mix"][0:1]
    (h0,) = rowwise(f_rmsnorm, [x], [g_mix0], [(D_MODEL, BF16)], name="attn_norm", tm=512)
    qkv = matmul(h0, W["attn_in"], mode="nn", name="attn_qkv")
    qk_par = [P["q_norm_a"], P["k_norm_a"], P["q_norm_b"], P["k_norm_b"]] + qk_consts
    qk_lead = [0, kv_lead[0], kv_lead[0], 0, kv_lead[1], kv_lead[1]]
    qa, ka, va, qb, kb, vb = rowwise(f_qknorm, [qkv], qk_par, [(DQ, BF16)] * 6, name="attn_qknorm", lead=qk_lead,
                                     tm=QK_ROWS)
    table = jnp.pad(P["relpos_table"], ((0, 0), (0, REL_W - (2 * MAX_REL + 1))))
    nj = min(ATT_TQ, x.shape[0]) // CHUNK
    bias_a = widen_bias(relpos_bias(table, name="relpos_bias")[:, :, :BAND_A], A_PREV, nj)
    bias_b = widen_bias(bias_b, B_PREV, nj)
    sinks = jnp.broadcast_to(P["sinks"].reshape(N_HEADS // 2, 2, 1), (N_HEADS // 2, 2, LANES))
    oa, late = attn_fwd(qa, ka, va, bias_a, None, n_prev=A_PREV, name="attn_a", rider=comm.fwd_rider("a"))
    comm.fwd_done("a", late, W, P)
    ob, late = attn_fwd(qb, kb, vb, bias_b, sinks, n_prev=B_PREV, name="attn_b", rider=comm.fwd_rider("b"))
    comm.fwd_done("b", late, W, P)
    Wao = W["attn_out"]
    x1 = matmul(oa, Wao[:DQ], mode="nn", name="attn_out_a", residual=x)
    x1, hf0 = matmul_norm(ob, Wao[DQ:], x1, P["norm_ffn"][0:1], name="attn_out_b")
    g_mix1 = P["norm_mix"][1:2]
    x2, ffn0, h2 = _ffn_fwd(x1, hf0, 0, W, P, g_mix1)

    Ws = W["ssm_in"]
    CC = D_INNER + 2 * SSM_GROUPS * SSM_STATE
    Wz, Wx = Ws[:, :D_INNER], Ws[:, D_INNER:D_INNER + CC]
    Wdt = jnp.pad(Ws[:, D_INNER + CC:], ((0, 0), (0, LANES - SSM_HEADS)))
    z = matmul(h2, Wz, mode="nn", name="ssm_z", out_dtype=BF16)
    xr = matmul(h2, Wx, mode="nn", name="ssm_xbc", out_dtype=BF16)
    dtraw = matmul(h2, Wdt, mode="nn", name="ssm_dt")
    xc, xbc = dwconv_fwd(xr, P["ssm_conv_w"], P["ssm_conv_b"], lambda y: (y, _silu(y)), [], [BF16, F32],
                         name="ssm_conv")
    pad32 = lambda v: jnp.pad(v, ((0, 0), (0, LANES - SSM_HEADS)))
    A = pad32(-jnp.exp(P["ssm_a_log"]))
    dtb = pad32(P["ssm_dt_bias"])
    dexp = jnp.repeat(P["ssm_d"], D_INNER // SSM_HEADS, axis=1)
    y, states = ssd_fwd(xbc, dtraw, dtb, A, dexp, name="ssd_fwd")
    (y2,) = rowwise(f_gate_norm, [y, z], [P["ssm_norm"]], [(D_INNER, BF16)], name="ssm_gate_norm", tm=512)
    x3, hf1 = matmul_norm(y2, W["ssm_out"], x2, P["norm_ffn"][1:2], name="ssm_out")
    dx4, ffn1, lpart = _ffn_fwd(x3, hf1, 1, W, P, target=tgt)

    dx3, dWfi1, dWfo1, dgf1, dfcw1, dfcb1 = _ffn_bwd(dx4, 1, ffn1, W, P)
    out_f1 = comm.grads("f1", dWfi1, dWfo1)
    dy2 = matmul(dx3, W["ssm_out"], mode="nt", name="ssm_dy")
    dWso = matmul(y2, dx3, mode="tn", name="ssm_dwout")
    dy, dz, dnw = rowwise_vjp(f_gate_norm, [y, z], [P["ssm_norm"]], [dy2], [(0, F32), (1, BF16)], [0],
                              name="ssm_dgate_norm")
    (dxs, dB, dC, ddtraw, dA, ddtb, dDl), sent = ssd_bwd(xbc, dtraw, dtb, A, dexp, states, dy, name="ssd_bwd",
                                                          rider=out_f1)
    comm.bwd_done("f1", sent)
    dxr, dscw, dscb = dwconv_bwd(xr, P["ssm_conv_w"], [xc, (dxs, dB, dC)], lambda c, g: (g * _dsilu(c),), [],
                                 name="ssm_dconv")
    dh2 = matmul(dz, Wz, mode="nt", name="ssm_dh_z")
    dh2 = matmul(dxr, Wx, mode="nt", name="ssm_dh_x", residual=dh2)
    dx2, dgm1 = matmul_dnorm(ddtraw, Wdt, dh2, x2, g_mix1, dx3, name="ssm_dh_dt")
    dWs = jnp.concatenate([matmul(h2, dz, mode="tn", name="ssm_dw_z"),
                           matmul(h2, dxr, mode="tn", name="ssm_dw_x"),
                           matmul(h2, ddtraw, mode="tn", name="ssm_dw_dt")[:, :SSM_HEADS]], axis=1)
    out_s = comm.grads("s", dWs, dWso)

    dx1, dWfi0, dWfo0, dgf0, dfcw0, dfcb0 = _ffn_bwd(dx2, 0, ffn0, W, P)
    out_f0 = comm.grads("f0", dWfi0, dWfo0)
    doa = matmul(dx1, Wao[:DQ], mode="nt", name="attn_do_a", out_dtype=BF16)
    dob = matmul(dx1, Wao[DQ:], mode="nt", name="attn_do_b", out_dtype=BF16)
    dWao = jnp.concatenate([matmul(oa, dx1, mode="tn", name="attn_dwout_a"),
                            matmul(ob, dx1, mode="tn", name="attn_dwout_b")], axis=0)
    (dqa, dka, dva, dbias_a), sent = attn_bwd(qa, ka, va, doa, bias_a, None, n_prev=A_PREV, name="attn_a_bwd",
                                              rider=out_s)
    comm.bwd_done("s", sent)
    (dqb, dkb, dvb, _, dsk), sent = attn_bwd(qb, kb, vb, dob, bias_b, sinks, n_prev=B_PREV, name="attn_b_bwd",
                                             rider=out_f0)
    comm.bwd_done("f0", sent)
    dqkv, dgqa, dgka, dgqb, dgkb = rowwise_vjp(f_qknorm, [qkv], qk_par, [dqa, dka, dva, dqb, dkb, dvb],
                                               [(0, BF16)], [0, 1, 2, 3], name="attn_dqknorm", cot_skip=qk_lead,
                                               tm=QK_ROWS)
    dx, dgm0 = matmul_dnorm(dqkv, W["attn_in"], None, x, g_mix0, dx1, name="attn_dh")
    dWai = matmul(h0, dqkv, mode="tn", name="attn_dwin")
    dbias_a = fold_bias(dbias_a, A_PREV, nj)
    dbias_rev = jnp.pad(dbias_a[:, ::-1, :], ((0, 0), (0, 0), (0, REL_W - BAND_A)))
    dtable = relpos_grad(dbias_rev, name="relpos_grad")[:, :2 * MAX_REL + 1]

    gW = {"attn_in": dWai, "attn_out": dWao, "ssm_in": dWs, "ssm_out": dWso,
          "ffn_in": [dWfi0, dWfi1], "ffn_out": [dWfo0, dWfo1]}
    gP = {"norm_mix": jnp.concatenate([dgm0, dgm1], axis=0),
          "norm_ffn": jnp.concatenate([dgf0, dgf1], axis=0),
          "relpos_table": dtable, "q_norm_a": dgqa, "k_norm_a": dgka, "q_norm_b": dgqb, "k_norm_b": dgkb,
          "sinks": dsk[:, :, 0].reshape(1, N_HEADS),
          "ssm_conv_w": dscw, "ssm_conv_b": dscb,
          "ssm_dt_bias": ddtb[:, :SSM_HEADS], "ssm_a_log": dA[:, :SSM_HEADS] * A[:, :SSM_HEADS],
          "ssm_d": dDl.reshape(SSM_HEADS, D_INNER // SSM_HEADS).sum(axis=1).reshape(1, SSM_HEADS),
          "ssm_norm": dnw,
          "ffn_conv_w": jnp.stack([dfcw0, dfcw1]), "ffn_conv_b": jnp.concatenate([dfcb0, dfcb1], axis=0)}
    return lpart, dx, gW, gP


WEIGHTS = ["norm_mix", "norm_ffn", "attn_w_in", "attn_w_out", "relpos_table", "q_norm_a", "k_norm_a", "q_norm_b",
           "k_norm_b", "sinks", "ssm_w_in", "ssm_conv_w", "ssm_conv_b", "ssm_dt_bias", "ssm_a_log", "ssm_d",
           "ssm_norm", "ssm_w_out", "ffn_w_in", "ffn_conv_w", "ffn_conv_b", "ffn_w_out"]
ARGS = ["x"] + WEIGHTS + ["loss_target"] + ["m_" + w for w in WEIGHTS] + ["v_" + w for w in WEIGHTS]
N_CHIPS = 4
SMALL_ROWS = 384
SMALL_ORDER = ["norm_mix", "norm_ffn", "relpos_table", "q_norm_a", "k_norm_a", "q_norm_b", "k_norm_b", "sinks",
               "ssm_dt_bias", "ssm_a_log", "ssm_d", "ffn_conv_b", "ssm_conv_w", "ssm_conv_b", "ssm_norm", "ffn_conv_w"]


def _cols_to_slabs(g):
    K, N = g.shape
    return g.reshape(2, K // 2, N_CHIPS, N // N_CHIPS).transpose(0, 2, 1, 3)


def _rows_to_slabs(g):
    R, C = g.shape
    return g.reshape(N_CHIPS, 2, R // (2 * N_CHIPS), C).transpose(1, 0, 2, 3)


class MeshComm:
    def __init__(self, d, xi, yi, ci):
        self.d, self.ci, self.me = d, ci, 2 * xi + yi
        self.csel = jnp.full((1, LANES), ci, F32)
        self.msel = jnp.full((1, LANES), self.me, F32)
        halves = lambda w: w.reshape((2, -1, w.shape[-1]))
        small = jnp.concatenate([d[k].reshape(-1) for k in ("ssm_conv_w", "ssm_conv_b", "ssm_norm", "ffn_conv_w")])
        small = jnp.pad(small, (0, 2 * 40 * LANES - small.shape[0])).reshape(2, 40, LANES)
        self.shards = {"attn": [halves(d["attn_w_in"][0].astype(BF16)), halves(d["attn_w_out"][0].astype(BF16))],
                       "a": [d["ffn_w_in"].astype(BF16), small],
                       "b": [d["ffn_w_out"].astype(BF16), halves(d["ssm_w_in"][0].astype(BF16)),
                             halves(d["ssm_w_out"][0].astype(BF16))]}
        self.p32, self.mine = {}, {}

    def _whole(self, tag, outs):
        return [lax.dynamic_update_slice_in_dim(g, s[None], self.me, axis=0) for g, s in zip(outs, self.shards[tag])]

    @staticmethod
    def _cat_cols(g):
        return jnp.concatenate([g[j].reshape((-1, g.shape[-1])) for j in range(N_CHIPS)], axis=1)

    def first_weights(self):
        g_ai, g_ao = self._whole("attn", run_rider(gather_rider(self.shards["attn"]), name="gather_attn"))
        return {"attn_in": self._cat_cols(g_ai), "attn_out": g_ao.reshape(-1, D_MODEL)}

    def fwd_rider(self, tag):
        return gather_rider(self.shards[tag])

    def fwd_done(self, tag, outs, W, P):
        if tag == "b":
            g_fo, g_si, g_so = self._whole("b", outs)
            W["ffn_out"] = [g_fo[:, l].reshape(-1, D_MODEL) for l in range(2)]
            W["ssm_in"], W["ssm_out"] = self._cat_cols(g_si), g_so.reshape(-1, D_MODEL)
            return
        g_fi, g_sm = self._whole("a", outs)
        W["ffn_in"] = [jnp.concatenate([g_fi[j, l] for j in range(N_CHIPS)], axis=1) for l in range(2)]
        sm = g_sm.reshape(N_CHIPS, -1)
        CC = D_INNER + 2 * SSM_GROUPS * SSM_STATE
        c4, f4 = CC // N_CHIPS, D_FF // N_CHIPS
        o1 = SSM_CONV * c4
        o2 = o1 + c4
        o3 = o2 + D_INNER // N_CHIPS
        o4 = o3 + 2 * FFN_CONV * f4
        P["ssm_conv_w"] = sm[:, :o1].reshape(N_CHIPS, SSM_CONV, c4).transpose(1, 0, 2).reshape(SSM_CONV, CC)
        P["ssm_conv_b"] = sm[:, o1:o2].reshape(1, CC)
        P["ssm_norm"] = sm[:, o2:o3].reshape(1, D_INNER)
        P["ffn_conv_w"] = sm[:, o3:o4].reshape(N_CHIPS, 2, FFN_CONV, f4).transpose(1, 2, 0, 3).reshape(2, FFN_CONV, D_FF)

    def grads(self, tag, cols, rows):
        slabs = [_cols_to_slabs(cols), _rows_to_slabs(rows)]
        from_sib = pair_swap_halves(slabs, name="grad_pair_swap_" + tag)
        pairs = [pair_add(g, r, self.csel, name=f"grad_pair_add_{tag}{i}") for i, (g, r) in enumerate(zip(slabs, from_sib))]
        self.p32[tag] = [p[0] for p in pairs]
        return scatter_rider([p[1] for p in pairs])

    def bwd_done(self, tag, outs):
        self.mine[tag] = [chip_add(p, r, self.msel, name=f"grad_chip_add_{tag}{i}")
                          for i, (p, r) in enumerate(zip(self.p32[tag], outs))]

    def finish(self, d_attn_in, d_attn_out):
        self.bwd_done("at", run_rider(self.grads("at", d_attn_in, d_attn_out), name="grad_scatter_at"))
        order = ["at", "s", "f0", "f1"]
        mine = [m for t in order for m in self.mine[t]]
        theirs = pair_share(mine, name="grad_pair_share")
        full = [lax.dynamic_update_slice_in_dim(b, a[None], self.ci, axis=0).reshape((-1, a.shape[-1]))
                for a, b in zip(mine, theirs)]
        ai, ao, si, so, fi0, fo0, fi1, fo1 = full
        return {"attn_w_in": ai[None], "attn_w_out": ao[None], "ssm_w_in": si[None], "ssm_w_out": so[None],
                "ffn_w_in": jnp.stack([fi0, fi1]), "ffn_w_out": jnp.stack([fo0, fo1])}


def _adamw(w, g, m, v, name):
    shp = w.shape
    two = lambda a: a.reshape((-1, shp[-1]))
    outs = [(shp[-1], F32)] * 3
    d, nm, nv = rowwise(f_adamw, [two(w), two(g), two(m), two(v)], [], outs, name="adamw_" + name)
    return d.reshape(shp), nm.reshape(shp), nv.reshape(shp)


def kernel(x, norm_mix, norm_ffn, attn_w_in, attn_w_out, relpos_table, q_norm_a, k_norm_a, q_norm_b, k_norm_b, sinks, ssm_w_in, ssm_conv_w, ssm_conv_b, ssm_dt_bias, ssm_a_log, ssm_d, ssm_norm, ssm_w_out, ffn_w_in, ffn_conv_w, ffn_conv_b, ffn_w_out, loss_target, m_norm_mix, m_norm_ffn, m_attn_w_in, m_attn_w_out, m_relpos_table, m_q_norm_a, m_k_norm_a, m_q_norm_b, m_k_norm_b, m_sinks, m_ssm_w_in, m_ssm_conv_w, m_ssm_conv_b, m_ssm_dt_bias, m_ssm_a_log, m_ssm_d, m_ssm_norm, m_ssm_w_out, m_ffn_w_in, m_ffn_conv_w, m_ffn_conv_b, m_ffn_w_out, v_norm_mix, v_norm_ffn, v_attn_w_in, v_attn_w_out, v_relpos_table, v_q_norm_a, v_k_norm_a, v_q_norm_b, v_k_norm_b, v_sinks, v_ssm_w_in, v_ssm_conv_w, v_ssm_conv_b, v_ssm_dt_bias, v_ssm_a_log, v_ssm_d, v_ssm_norm, v_ssm_w_out, v_ffn_w_in, v_ffn_conv_w, v_ffn_conv_b, v_ffn_w_out):
    d = dict(zip(ARGS, (x, norm_mix, norm_ffn, attn_w_in, attn_w_out, relpos_table, q_norm_a, k_norm_a, q_norm_b, k_norm_b, sinks, ssm_w_in, ssm_conv_w, ssm_conv_b, ssm_dt_bias, ssm_a_log, ssm_d, ssm_norm, ssm_w_out, ffn_w_in, ffn_conv_w, ffn_conv_b, ffn_w_out, loss_target, m_norm_mix, m_norm_ffn, m_attn_w_in, m_attn_w_out, m_relpos_table, m_q_norm_a, m_k_norm_a, m_q_norm_b, m_k_norm_b, m_sinks, m_ssm_w_in, m_ssm_conv_w, m_ssm_conv_b, m_ssm_dt_bias, m_ssm_a_log, m_ssm_d, m_ssm_norm, m_ssm_w_out, m_ffn_w_in, m_ffn_conv_w, m_ffn_conv_b, m_ffn_w_out, v_norm_mix, v_norm_ffn, v_attn_w_in, v_attn_w_out, v_relpos_table, v_q_norm_a, v_k_norm_a, v_q_norm_b, v_k_norm_b, v_sinks, v_ssm_w_in, v_ssm_conv_w, v_ssm_conv_b, v_ssm_dt_bias, v_ssm_a_log, v_ssm_d, v_ssm_norm, v_ssm_w_out, v_ffn_w_in, v_ffn_conv_w, v_ffn_conv_b, v_ffn_w_out)))
    xi, yi, ci = _pos()
    me = 2 * xi + yi
    CC = D_INNER + 2 * SSM_GROUPS * SSM_STATE
    c4, f4 = CC // N_CHIPS, D_FF // N_CHIPS

    P = {k: d[k] for k in ["norm_mix", "norm_ffn", "q_norm_a", "k_norm_a", "q_norm_b", "k_norm_b", "sinks",
                           "ssm_dt_bias", "ssm_a_log", "ssm_d", "ffn_conv_b"]}
    P["relpos_table"] = d["relpos_table"][0]
    comm = MeshComm(d, xi, yi, ci)
    W = comm.first_weights()
    lpart, dx, gW, gP = local_step(d["x"][0], d["loss_target"][0], W, P, comm)
    loss = lax.psum(lpart[0, 0], ("x", "y", "c"))
    grads = comm.finish(gW["attn_in"], gW["attn_out"])

    flat = jnp.concatenate([gP[k].reshape(-1) for k in SMALL_ORDER])
    flat = jnp.pad(flat, (0, SMALL_ROWS * LANES - flat.shape[0])).reshape(SMALL_ROWS, LANES)
    tot = sum_slots(gather_all(flat, name="small_gather"), name="small_sum").reshape(-1)
    off = 0
    for k in SMALL_ORDER:
        n = int(np.prod(gP[k].shape))
        g = tot[off:off + n].reshape(gP[k].shape)
        off += n
        if k == "ssm_conv_w":
            g = lax.dynamic_slice_in_dim(g, me * c4, c4, axis=1)[None]
        elif k == "ssm_conv_b":
            g = lax.dynamic_slice_in_dim(g, me * c4, c4, axis=1)
        elif k == "ssm_norm":
            g = lax.dynamic_slice_in_dim(g, me * (D_INNER // N_CHIPS), D_INNER // N_CHIPS, axis=1)
        elif k == "ffn_conv_w":
            g = lax.dynamic_slice_in_dim(g, me * f4, f4, axis=2)
        elif k == "relpos_table":
            g = g[None]
        grads[k] = g

    deltas, new_m, new_v = {}, {}, {}
    for k in WEIGHTS:
        deltas[k], new_m[k], new_v[k] = _adamw(d[k], grads[k], d["m_" + k], d["v_" + k], k)
    return (loss, dx[None], *[grads[k] for k in WEIGHTS], *[deltas[k] for k in WEIGHTS],
            *[new_m[k] for k in WEIGHTS], *[new_v[k] for k in WEIGHTS])
```

```python
import functools

import numpy as np
import jax
import jax.numpy as jnp
from jax import lax
from jax.experimental import pallas as pl
from jax.experimental.pallas import tpu as pltpu

F32 = jnp.float32
BF16 = jnp.bfloat16
HI = lax.Precision.HIGHEST

D_MODEL = 1024
CHUNK = 64
EPS = 1e-6
HEAD_DIM = 64
N_HEADS = 8
A_PREV = 8
B_PREV = 2
MAX_REL = 256
D_INNER = 2048
SSM_HEADS = 32
SSM_GROUPS = 4
SSM_STATE = 128
SSM_CONV = 4
D_FF = 2816
FFN_CONV = 3
LANES = 128
SUBLANES = 8
VMEM_LIMIT = 56 * 1024 * 1024
SSD_L = 128

ADAM_LR = 0.001
ADAM_B1 = 0.9
ADAM_B2 = 0.999
ADAM_EPS = 1e-08
ADAM_WD = 0.01
ADAM_STEP = 10

MESH = pl.DeviceIdType.MESH


def _params(*sem):
    return pltpu.CompilerParams(dimension_semantics=sem, vmem_limit_bytes=VMEM_LIMIT)


def _pick(n, want):
    if n <= want:
        return n
    t = (want // LANES) * LANES
    while t >= LANES:
        if n % t == 0:
            return t
        t -= LANES
    return n


MM_ROWS = 512
MM_COLS = 1536
MM_RED = 2048
MM_SHORT = 1024


def matmul(a, b, *, mode, name, out_dtype=F32, residual=None):
    dims = {"nn": (((1,), (0,)), ((), ())), "nt": (((1,), (1,)), ((), ())), "tn": (((0,), (0,)), ((), ()))}[mode]
    if mode == "tn":
        assert residual is None and out_dtype == F32
        (K, M), (K2, N) = a.shape, b.shape
        assert K == K2, (a.shape, b.shape)
        tm, tn, tk = _pick(M, MM_COLS), _pick(N, MM_COLS), _pick(K, MM_RED)

        def body(a_ref, b_ref, o_ref):
            k = pl.program_id(2)
            p = lax.dot_general(a_ref[...].astype(BF16), b_ref[...].astype(BF16), dims, preferred_element_type=F32)

            @pl.when(k == 0)
            def _():
                o_ref[...] = p

            @pl.when(k != 0)
            def _():
                o_ref[...] += p

        return pl.pallas_call(
            body, name=name, grid=(M // tm, N // tn, K // tk),
            in_specs=[pl.BlockSpec((tk, tm), lambda i, j, k: (k, i)), pl.BlockSpec((tk, tn), lambda i, j, k: (k, j))],
            out_specs=pl.BlockSpec((tm, tn), lambda i, j, k: (i, j)),
            out_shape=jax.ShapeDtypeStruct((M, N), F32),
            compiler_params=_params("parallel", "parallel", "arbitrary"),
        )(a, b)

    if mode == "nn":
        (M, K), (K2, N) = a.shape, b.shape
    else:
        (M, K), (N, K2) = a.shape, b.shape
    assert K == K2, (a.shape, b.shape, mode)
    tm, tn = _pick(M, MM_ROWS if K > MM_SHORT else 2 * MM_ROWS), _pick(N, MM_COLS)

    def body(*refs):
        a_ref, b_ref = refs[:2]
        o_ref = refs[-1]
        r = lax.dot_general(a_ref[...].astype(BF16), b_ref[...].astype(BF16), dims, preferred_element_type=F32)
        if residual is not None:
            r = r + refs[2][...].astype(F32)
        o_ref[...] = r.astype(o_ref.dtype)

    a_spec = pl.BlockSpec((tm, K), lambda j, i: (i, 0))
    b_spec = pl.BlockSpec((K, tn), lambda j, i: (0, j)) if mode == "nn" else pl.BlockSpec((tn, K), lambda j, i: (j, 0))
    o_spec = pl.BlockSpec((tm, tn), lambda j, i: (i, j))
    in_specs = [a_spec, b_spec] + ([o_spec] if residual is not None else [])
    args = (a, b) + ((residual,) if residual is not None else ())
    return pl.pallas_call(
        body, name=name, grid=(N // tn, M // tm),
        in_specs=in_specs, out_specs=o_spec,
        out_shape=jax.ShapeDtypeStruct((M, N), out_dtype),
        compiler_params=_params("parallel", "parallel"),
    )(*args)


def matmul_norm(a, b, residual, g, *, name):
    (M, K), (_, N) = a.shape, b.shape
    tm = _pick(M, MM_ROWS)

    def body(a_ref, b_ref, r_ref, g_ref, x_ref, h_ref):
        x = jnp.dot(a_ref[...].astype(BF16), b_ref[...].astype(BF16), preferred_element_type=F32) + r_ref[...]
        x_ref[...] = x
        h_ref[...] = f_rmsnorm(x, g_ref[...])[0].astype(BF16)

    row = pl.BlockSpec((tm, N), lambda i: (i, 0))
    return pl.pallas_call(
        body, name=name, grid=(M // tm,),
        in_specs=[pl.BlockSpec((tm, K), lambda i: (i, 0)), pl.BlockSpec((K, N), lambda i: (0, 0)), row,
                  pl.BlockSpec((1, N), lambda i: (0, 0))],
        out_specs=[row, row],
        out_shape=[jax.ShapeDtypeStruct((M, N), F32), jax.ShapeDtypeStruct((M, N), BF16)],
        compiler_params=_params("parallel"),
    )(a, b, residual, g)


def matmul_dnorm(a, b, partial, x, g, dres, *, name):
    (M, K), (N, _) = a.shape, b.shape
    tm = _pick(M, MM_ROWS)
    has_part = partial is not None

    def body(*refs):
        a_ref, b_ref = refs[:2]
        x_ref, g_ref, dres_ref, dx_ref, dxb_ref, dg_ref = refs[-6:]
        dh = lax.dot_general(a_ref[...].astype(BF16), b_ref[...].astype(BF16), _NT, preferred_element_type=F32)
        if has_part:
            dh = dh + refs[2][...]
        xv = x_ref[...]
        r = lax.rsqrt(jnp.mean(xv * xv, axis=-1, keepdims=True) + EPS)
        xhat = xv * r
        dxh = dh * g_ref[...]
        dx = dres_ref[...] + r * (dxh - xhat * jnp.mean(dxh * xhat, axis=-1, keepdims=True))
        dx_ref[...] = dx
        dxb_ref[...] = dx.astype(BF16)
        dg = jnp.sum(dh * xhat, axis=0, keepdims=True)

        @pl.when(pl.program_id(0) == 0)
        def _():
            dg_ref[...] = dg

        @pl.when(pl.program_id(0) != 0)
        def _():
            dg_ref[...] += dg

    row = pl.BlockSpec((tm, N), lambda i: (i, 0))
    vec = pl.BlockSpec((1, N), lambda i: (0, 0))
    in_specs = [pl.BlockSpec((tm, K), lambda i: (i, 0)), pl.BlockSpec((N, K), lambda i: (0, 0))]
    args = [a, b]
    if has_part:
        in_specs.append(row)
        args.append(partial)
    return pl.pallas_call(
        body, name=name, grid=(M // tm,), in_specs=in_specs + [row, vec, row], out_specs=[row, row, vec],
        out_shape=[jax.ShapeDtypeStruct((M, N), F32), jax.ShapeDtypeStruct((M, N), BF16),
                   jax.ShapeDtypeStruct((1, N), F32)],
        compiler_params=_params("arbitrary"),
    )(*args, x, g, dres)


def rowwise(f, rows, params, outs, *, name, tm=256, lead=None):
    S = rows[0].shape[0]
    tm = _row_tile(S, tm)
    nr, npar = len(rows), len(params)
    lead = [0] * len(outs) if lead is None else lead
    assert all(ld % tm == 0 for ld in lead)
    padded = [k for k, ld in enumerate(lead) if ld]

    def body(*refs):
        vals = [r[...].astype(F32) for r in refs[:nr + npar]]
        res = f(*vals)
        for o_ref, r in zip(refs[nr + npar + len(padded):], res):
            o_ref[...] = r.astype(o_ref.dtype)

    in_specs = [pl.BlockSpec((tm, r.shape[1]), lambda i: (i, 0)) for r in rows]
    in_specs += [pl.BlockSpec(p.shape, lambda i: (0, 0)) for p in params]
    in_specs += [pl.BlockSpec(memory_space=pl.ANY)] * len(padded)
    zeros = [jnp.zeros((S + lead[k], outs[k][0]), outs[k][1]) for k in padded]
    out_specs = [pl.BlockSpec((tm, c), lambda i, s=ld // tm: (i + s, 0)) for (c, _), ld in zip(outs, lead)]
    out_shape = [jax.ShapeDtypeStruct((S + ld, c), dt) for (c, dt), ld in zip(outs, lead)]
    return pl.pallas_call(body, name=name, grid=(S // tm,), in_specs=in_specs, out_specs=out_specs,
                          out_shape=out_shape, input_output_aliases={nr + npar + n: k for n, k in enumerate(padded)},
                          compiler_params=_params("parallel"))(*rows, *params, *zeros)


def rowwise_vjp(f, rows, params, cots, drow, dpar, *, name, tm=256, cot_skip=None):
    S = rows[0].shape[0]
    tm = _row_tile(S, tm)
    nr, npar, nc = len(rows), len(params), len(cots)
    skip = [0] * nc if cot_skip is None else [s // tm for s in cot_skip]
    assert cot_skip is None or all(s % tm == 0 for s in cot_skip)

    def body(*refs):
        vals = [r[...].astype(F32) for r in refs[:nr + npar]]
        cvals = [r[...].astype(F32) for r in refs[nr + npar:nr + npar + nc]]
        o_refs = refs[nr + npar + nc:]
        want = [ri for ri, _ in drow] + [nr + pi for pi in dpar]

        def f_want(*d):
            full = list(vals)
            for k, v in zip(want, d):
                full[k] = v
            return f(*full)

        _, vjp = jax.vjp(f_want, *[vals[k] for k in want])
        grads = vjp(tuple(cvals))
        for o_ref, g in zip(o_refs[:len(drow)], grads):
            o_ref[...] = g.astype(o_ref.dtype)
        first = pl.program_id(0) == 0
        for o_ref, g in zip(o_refs[len(drow):], grads[len(drow):]):
            g = g.astype(F32)

            @pl.when(first)
            def _(o_ref=o_ref, g=g):
                o_ref[...] = g

            @pl.when(jnp.logical_not(first))
            def _(o_ref=o_ref, g=g):
                o_ref[...] += g

    in_specs = [pl.BlockSpec((tm, r.shape[1]), lambda i: (i, 0)) for r in rows]
    in_specs += [pl.BlockSpec(p.shape, lambda i: (0, 0)) for p in params]
    in_specs += [pl.BlockSpec((tm, c.shape[1]), lambda i, s=s: (i + s, 0)) for c, s in zip(cots, skip)]
    out_specs = [pl.BlockSpec((tm, rows[ri].shape[1]), lambda i: (i, 0)) for ri, _ in drow]
    out_specs += [pl.BlockSpec(params[pi].shape, lambda i: (0, 0)) for pi in dpar]
    out_shape = [jax.ShapeDtypeStruct(rows[ri].shape, dt) for ri, dt in drow]
    out_shape += [jax.ShapeDtypeStruct(params[pi].shape, F32) for pi in dpar]
    return pl.pallas_call(body, name=name, grid=(S // tm,), in_specs=in_specs, out_specs=out_specs,
                          out_shape=out_shape, compiler_params=_params("arbitrary"))(*rows, *params, *cots)


HALO = 2 * SUBLANES
CONV_ROWS = 64


def dwconv_fwd(x, w, b, post, extra, outs, *, name, tm=512):
    S, C = x.shape
    K = w.shape[0]
    tm = min(tm, S)
    hb = tm // HALO
    ne = len(extra)

    def body(*refs):
        x_ref, halo_ref, w_ref, b_ref = refs[:4]
        e_refs = refs[4:4 + ne]
        o_refs = refs[4 + ne:4 + ne + len(outs)]
        buf = refs[-1]
        i = pl.program_id(0)
        buf[0:HALO, :] = jnp.where(i == 0, 0.0, halo_ref[...].astype(F32))
        buf[HALO:HALO + tm, :] = x_ref[...].astype(F32)
        for c0 in range(0, C, LANES):
            cs = slice(c0, c0 + LANES)
            for r0 in range(0, tm, CONV_ROWS):
                rs = slice(r0, r0 + CONV_ROWS)
                acc = jnp.broadcast_to(b_ref[:, cs], (CONV_ROWS, LANES))
                for k in range(K):
                    acc = acc + w_ref[k:k + 1, cs] * buf[pl.ds(HALO - (K - 1) + k + r0, CONV_ROWS), cs]
                for o_ref, r in zip(o_refs, post(acc, *[e[rs, cs].astype(F32) for e in e_refs])):
                    o_ref[rs, cs] = r.astype(o_ref.dtype)

    row = pl.BlockSpec((tm, C), lambda i: (i, 0))
    return pl.pallas_call(
        body, name=name, grid=(S // tm,),
        in_specs=[row,
                  pl.BlockSpec((HALO, C), lambda i: (jnp.maximum(i * hb - 1, 0), 0)),
                  pl.BlockSpec((K, C), lambda i: (0, 0)),
                  pl.BlockSpec((1, C), lambda i: (0, 0))] + [row] * ne,
        out_specs=[row] * len(outs),
        out_shape=[jax.ShapeDtypeStruct((S, C), dt) for dt in outs],
        scratch_shapes=[pltpu.VMEM((HALO + tm, C), F32)],
        compiler_params=_params("parallel"),
    )(x, x, w, b, *extra)


def dwconv_bwd(x, w, srcs, dy_fn, extra_outs, *, name, tm=256):
    S, C = x.shape
    K = w.shape[0]
    tm = min(tm, S)
    hb = tm // HALO
    n = S // tm
    groups = [s if isinstance(s, tuple) else (s,) for s in srcs]
    flat = [a for g in groups for a in g]
    nf = len(flat)

    def body(*refs):
        x_ref, xh_ref, w_ref = refs[:3]
        dx_ref, dw_ref, db_ref = refs[3 + 2 * nf:6 + 2 * nf]
        e_refs = refs[6 + 2 * nf:6 + 2 * nf + len(extra_outs)]
        bx, bd = refs[-2:]

        def strips(first, c0, rs):
            out, at = [], first
            for g in groups:
                off = 0
                for a in g:
                    if off <= c0 < off + a.shape[1]:
                        out.append(refs[at][rs, c0 - off:c0 - off + LANES].astype(F32))
                    off += a.shape[1]
                    at += 1
            return out

        i = pl.program_id(0)
        bx[0:HALO, :] = jnp.where(i == 0, 0.0, xh_ref[...].astype(F32))
        bx[HALO:HALO + tm, :] = x_ref[...].astype(F32)

        @pl.when(i == 0)
        def _():
            dw_ref[...] = jnp.zeros_like(dw_ref)
            db_ref[...] = jnp.zeros_like(db_ref)

        for c0 in range(0, C, LANES):
            cs = slice(c0, c0 + LANES)
            dws = [jnp.zeros((1, LANES), F32) for _ in range(K)]
            dbs = jnp.zeros((1, LANES), F32)
            for r0 in range(0, tm, CONV_ROWS):
                rs = slice(r0, r0 + CONV_ROWS)
                res = dy_fn(*strips(3, c0, rs))
                dyv = res[0]
                for e_ref, r in zip(e_refs, res[1:]):
                    e_ref[rs, cs] = r.astype(e_ref.dtype)
                bd[rs, cs] = dyv
                for k in range(K):
                    dws[k] = dws[k] + jnp.sum(dyv * bx[pl.ds(HALO - (K - 1) + k + r0, CONV_ROWS), cs], axis=0,
                                              keepdims=True)
                dbs = dbs + jnp.sum(dyv, axis=0, keepdims=True)
            bd[tm:tm + HALO, cs] = jnp.where(i == n - 1, 0.0, dy_fn(*strips(3 + nf, c0, slice(None)))[0])
            for k in range(K):
                dw_ref[k:k + 1, cs] += dws[k]
            db_ref[:, cs] += dbs
            for r0 in range(0, tm, CONV_ROWS):
                acc = jnp.zeros((CONV_ROWS, LANES), F32)
                for k in range(K):
                    acc = acc + w_ref[k:k + 1, cs] * bd[pl.ds((K - 1) - k + r0, CONV_ROWS), cs]
                dx_ref[r0:r0 + CONV_ROWS, cs] = acc.astype(dx_ref.dtype)

    row = lambda c: pl.BlockSpec((tm, c), lambda i: (i, 0))
    nxt = lambda c: pl.BlockSpec((HALO, c), lambda i: (jnp.minimum((i + 1) * hb, S // HALO - 1), 0))
    return pl.pallas_call(
        body, name=name, grid=(n,),
        in_specs=[row(C), pl.BlockSpec((HALO, C), lambda i: (jnp.maximum(i * hb - 1, 0), 0)),
                  pl.BlockSpec((K, C), lambda i: (0, 0))]
                 + [row(a.shape[1]) for a in flat] + [nxt(a.shape[1]) for a in flat],
        out_specs=[row(C), pl.BlockSpec((K, C), lambda i: (0, 0)), pl.BlockSpec((1, C), lambda i: (0, 0))]
                  + [row(C)] * len(extra_outs),
        out_shape=[jax.ShapeDtypeStruct((S, C), BF16), jax.ShapeDtypeStruct((K, C), F32),
                   jax.ShapeDtypeStruct((1, C), F32)] + [jax.ShapeDtypeStruct((S, C), dt) for dt in extra_outs],
        scratch_shapes=[pltpu.VMEM((HALO + tm, C), F32), pltpu.VMEM((tm + HALO, C), F32)],
        compiler_params=_params("arbitrary"),
    )(x, x, w, *flat, *flat)


def _sigmoid(x):
    return 0.5 * jnp.tanh(0.5 * x) + 0.5


def _silu(x):
    return x * _sigmoid(x)


def _dsilu(x):
    s = _sigmoid(x)
    return s * (1.0 + x * (1.0 - s))


def f_rmsnorm(x, g):
    return (x * lax.rsqrt(jnp.mean(x * x, axis=-1, keepdims=True) + EPS) * g,)


SEL = lax.Precision.HIGH


def _group_norm(x, bd, width):
    ms = jnp.dot(x * x, bd, precision=SEL, preferred_element_type=F32) * (1.0 / width)
    return x * lax.rsqrt(ms + EPS)


def f_qknorm(qkv, gqa, gka, gqb, gkb, bd512, bd128, fold, expand):
    dq = N_HEADS * HEAD_DIM
    qa, ka, va, qb = (qkv[:, i * dq:(i + 1) * dq] for i in range(4))
    kb = qkv[:, 4 * dq:4 * dq + LANES]
    vb = qkv[:, 4 * dq + LANES:4 * dq + 2 * LANES]
    tile8 = lambda g: jnp.dot(g, fold, precision=HI, preferred_element_type=F32)
    qa = _group_norm(qa, bd512, HEAD_DIM) * tile8(gqa)
    ka = _group_norm(ka, bd512, HEAD_DIM) * tile8(gka)
    qb = _group_norm(qb, bd512, HEAD_DIM) * tile8(gqb)
    kb = _group_norm(kb, bd128, HEAD_DIM) * tile8(gkb)[:, :LANES]
    kb = jnp.dot(kb, expand, precision=SEL, preferred_element_type=F32)
    vb = jnp.dot(vb, expand, precision=SEL, preferred_element_type=F32)
    return qa, ka, va, qb, kb, vb


def f_gate_norm(y, z, nw):
    v = y * _silu(z)
    gw = D_INNER // SSM_GROUPS
    parts = []
    for g in range(SSM_GROUPS):
        vg = v[:, g * gw:(g + 1) * gw]
        parts.append(vg * lax.rsqrt(jnp.mean(vg * vg, axis=-1, keepdims=True) + EPS))
    return (jnp.concatenate(parts, axis=-1) * nw,)


ATT_TQ = 256
QK_ROWS = 512
_NT =(((1,), (1,)), ((), ()))
_TN = (((0,), (0,)), ((), ()))


def _stack_heads(t, head0):
    return jnp.concatenate([jnp.where(head0, t, 0.0), jnp.where(head0, 0.0, t)], axis=0).astype(BF16)


def _attn_probs(qk, bias, valid, snk):
    s = qk * (HEAD_DIM ** -0.5) + bias
    s = jnp.where(valid, s, -jnp.inf)
    m = jnp.max(s, axis=1, keepdims=True)
    if snk is not None:
        m = jnp.maximum(m, snk)
    e = jnp.exp(s - m)
    den = jnp.sum(e, axis=1, keepdims=True)
    if snk is None:
        return e / den, None
    es = jnp.exp(snk - m)
    den = den + es
    return e / den, es / den


def widen_bias(bias, n_prev, nj):
    band = (n_prev + 1) * CHUNK
    wk = (nj + n_prev) * CHUNK
    rows = [jnp.pad(bias, ((0, 0), (0, 0), (j * CHUNK, wk - band - j * CHUNK)), constant_values=-jnp.inf)
            for j in range(nj)]
    return jnp.concatenate(rows, axis=1)


def fold_bias(dbw, n_prev, nj):
    band = (n_prev + 1) * CHUNK
    acc = dbw[:, :CHUNK, :band]
    for j in range(1, nj):
        acc = acc + dbw[:, j * CHUNK:(j + 1) * CHUNK, j * CHUNK:j * CHUNK + band]
    return acc


def attn_fwd(q, k, v, bias_w, sinks, *, n_prev, name, rider=None):
    S = q.shape[0]
    pad = n_prev * CHUNK
    kv_rows = k.shape[0]
    lead = kv_rows - S - pad
    tq = min(ATT_TQ, S)
    wk = tq + pad
    assert bias_w.shape == (N_HEADS, tq, wk), bias_w.shape
    has_sink = sinks is not None

    r_in, r_out, r_shapes, r_sems, r_args = _rider_parts(rider)
    n_own = 5 if has_sink else 4
    n_p, n_i = N_HEADS // 2, S // tq

    def body(*refs):
        q_ref, k_ref, v_ref, bias_ref = refs[:4]
        sink_ref = refs[4] if has_sink else None
        o_ref = refs[n_own + len(r_in)]
        if rider is not None:
            p_id, i_id = pl.program_id(0), pl.program_id(1)
            _ride(rider, refs[n_own:n_own + len(r_in)], refs[n_own + len(r_in) + 1:n_own + len(r_in) + 1 + len(r_out)],
                  refs[n_own + len(r_in) + 1 + len(r_out):],
                  jnp.logical_and(p_id == 0, i_id == 0), jnp.logical_and(p_id == n_p - 1, i_id == 0),
                  jnp.logical_and(p_id == n_p - 1, i_id == n_i - 1))
        start = pl.multiple_of(pl.program_id(1) * tq, tq)
        head0 = lax.broadcasted_iota(jnp.int32, (1, LANES), 1) < HEAD_DIM
        valid = lax.broadcasted_iota(jnp.int32, (1, wk), 1) + start >= pad
        kb = k_ref[pl.ds(pl.multiple_of(start + lead, CHUNK), wk), :]
        vb = v_ref[pl.ds(pl.multiple_of(start + lead, CHUNK), wk), :]
        qk = lax.dot_general(_stack_heads(q_ref[...].astype(F32), head0), kb, _NT, preferred_element_type=F32)
        ps = []
        for r in range(2):
            snk = sink_ref[0, r:r + 1, 0:1] if has_sink else None
            ps.append(_attn_probs(qk[r * tq:(r + 1) * tq, :], bias_ref[r], valid, snk)[0].astype(BF16))
        o2 = jnp.dot(jnp.concatenate(ps, axis=0), vb, preferred_element_type=F32)
        o_ref[...] = jnp.where(head0, o2[:tq, :], o2[tq:, :]).astype(o_ref.dtype)

    in_specs = [pl.BlockSpec((tq, LANES), lambda p, i: (i, p)),
                pl.BlockSpec((kv_rows, LANES), lambda p, i: (0, p)),
                pl.BlockSpec((kv_rows, LANES), lambda p, i: (0, p)),
                pl.BlockSpec((2, tq, wk), lambda p, i: (p, 0, 0))]
    args = [q, k, v, bias_w]
    if has_sink:
        in_specs.append(pl.BlockSpec((1, 2, LANES), lambda p, i: (p, 0, 0)))
        args.append(sinks)
    res = pl.pallas_call(
        body, name=name, grid=(n_p, n_i), in_specs=in_specs + r_in,
        out_specs=[pl.BlockSpec((tq, LANES), lambda p, i: (i, p))] + r_out,
        out_shape=[jax.ShapeDtypeStruct((S, N_HEADS * HEAD_DIM), BF16)] + r_shapes,
        scratch_shapes=r_sems,
        compiler_params=pltpu.CompilerParams(dimension_semantics=("arbitrary", "arbitrary"), vmem_limit_bytes=VMEM_LIMIT,
                                             has_side_effects=rider is not None),
    )(*args, *r_args)
    return res[0], res[1:]


def attn_bwd(q, k, v, do, bias_w, sinks, *, n_prev, name, rider=None):
    S = q.shape[0]
    pad = n_prev * CHUNK
    kv_rows = k.shape[0]
    lead = kv_rows - S - pad
    tq = min(ATT_TQ, S)
    wk = tq + pad
    assert bias_w.shape == (N_HEADS, tq, wk), bias_w.shape
    has_sink = sinks is not None
    scale = HEAD_DIM ** -0.5

    r_in, r_out, r_shapes, r_sems, r_args = _rider_parts(rider)
    n_own_in = 6 if has_sink else 5
    n_own_out = 5 if has_sink else 4
    n_p, n_i = N_HEADS // 2, S // tq

    def body(*refs):
        q_ref, k_ref, v_ref, do_ref, bias_ref = refs[:5]
        sink_ref = refs[5] if has_sink else None
        o0 = n_own_in + len(r_in)
        dq_ref, dk_ref, dv_ref, db_ref = refs[o0:o0 + 4]
        dsk_ref = refs[o0 + 4] if has_sink else None
        i = pl.program_id(1)
        if rider is not None:
            p_id = pl.program_id(0)
            _ride(rider, refs[n_own_in:o0], refs[o0 + n_own_out:o0 + n_own_out + len(r_out)],
                  refs[o0 + n_own_out + len(r_out):],
                  jnp.logical_and(p_id == 0, i == 0), jnp.logical_and(p_id == n_p // 2, i == 0),
                  jnp.logical_and(p_id == n_p - 1, i == n_i - 1))

        @pl.when(i == 0)
        def _():
            dk_ref[...] = jnp.zeros_like(dk_ref)
            dv_ref[...] = jnp.zeros_like(dv_ref)
            db_ref[...] = jnp.zeros_like(db_ref)
            if has_sink:
                dsk_ref[...] = jnp.zeros_like(dsk_ref)

        start = pl.multiple_of(i * tq, tq)
        head0 = lax.broadcasted_iota(jnp.int32, (1, LANES), 1) < HEAD_DIM
        valid = lax.broadcasted_iota(jnp.int32, (1, wk), 1) + start >= pad
        kb = k_ref[pl.ds(pl.multiple_of(start + lead, CHUNK), wk), :]
        vb = v_ref[pl.ds(pl.multiple_of(start + lead, CHUNK), wk), :]
        q2 = _stack_heads(q_ref[...].astype(F32), head0)
        do2 = _stack_heads(do_ref[...].astype(F32), head0)
        qk = lax.dot_general(q2, kb, _NT, preferred_element_type=F32)
        dp2 = lax.dot_general(do2, vb, _NT, preferred_element_type=F32)
        pbs, dss = [], []
        for r in range(2):
            rows = slice(r * tq, (r + 1) * tq)
            snk = sink_ref[0, r:r + 1, 0:1] if has_sink else None
            p, ps = _attn_probs(qk[rows, :], bias_ref[r], valid, snk)
            dp = dp2[rows, :]
            delta = jnp.sum(p * dp, axis=1, keepdims=True)
            ds = p * (dp - delta)
            db_ref[r] += ds
            if has_sink:
                dsk = -jnp.sum(ps * delta, axis=0, keepdims=True)
                dsk_ref[0, r:r + 1, :] += jnp.broadcast_to(dsk, (1, LANES))
            pbs.append(p.astype(BF16))
            dss.append(ds.astype(BF16))
        ds2 = jnp.concatenate(dss, axis=0)
        dq2 = jnp.dot(ds2, kb, preferred_element_type=F32) * scale
        dq_ref[...] = jnp.where(head0, dq2[:tq, :], dq2[tq:, :])
        dk_ref[pl.ds(pl.multiple_of(start + lead, CHUNK), wk), :] += lax.dot_general(ds2, q2, _TN, preferred_element_type=F32) * scale
        dv_ref[pl.ds(pl.multiple_of(start + lead, CHUNK), wk), :] += lax.dot_general(jnp.concatenate(pbs, axis=0), do2, _TN,
                                                       preferred_element_type=F32)

    row_spec = pl.BlockSpec((tq, LANES), lambda p, i: (i, p))
    kv_spec = pl.BlockSpec((kv_rows, LANES), lambda p, i: (0, p))
    bias_spec = pl.BlockSpec((2, tq, wk), lambda p, i: (p, 0, 0))
    sink_spec = pl.BlockSpec((1, 2, LANES), lambda p, i: (p, 0, 0))
    in_specs = [row_spec, kv_spec, kv_spec, row_spec, bias_spec]
    args = [q, k, v, do, bias_w]
    out_specs = [row_spec, kv_spec, kv_spec, bias_spec]
    W = N_HEADS * HEAD_DIM
    out_shape = [jax.ShapeDtypeStruct((S, W), F32), jax.ShapeDtypeStruct((kv_rows, W), F32),
                 jax.ShapeDtypeStruct((kv_rows, W), F32), jax.ShapeDtypeStruct((N_HEADS, tq, wk), F32)]
    if has_sink:
        in_specs.append(sink_spec)
        args.append(sinks)
        out_specs.append(sink_spec)
        out_shape.append(jax.ShapeDtypeStruct((N_HEADS // 2, 2, LANES), F32))
    res = pl.pallas_call(
        body, name=name, grid=(n_p, n_i), in_specs=in_specs + r_in, out_specs=out_specs + r_out,
        out_shape=out_shape + r_shapes, scratch_shapes=r_sems,
        compiler_params=pltpu.CompilerParams(dimension_semantics=("arbitrary", "arbitrary"), vmem_limit_bytes=VMEM_LIMIT,
                                             has_side_effects=rider is not None),
    )(*args, *r_args)
    return res[:n_own_out], res[n_own_out:]


HP = SSM_HEADS // 2
PAIRS_PER_GROUP = HP // SSM_GROUPS
HEADS_PER_GROUP = SSM_HEADS // SSM_GROUPS
GW = HEADS_PER_GROUP * 64


def _ssd_dt(dtraw, dtb, A, tril):
    lane = lax.broadcasted_iota(jnp.int32, (1, LANES), 1)
    u = dtraw + dtb
    eu = jnp.exp(-jnp.abs(u))
    w1 = 1.0 + eu
    l1p = jnp.where(w1 == 1.0, eu, jnp.log(w1) * eu / jnp.where(w1 == 1.0, 1.0, w1 - 1.0))
    dt = jnp.where(lane < SSM_HEADS, jnp.maximum(u, 0.0) + l1p, 0.0)
    acs = jnp.dot(tril, dt * A, precision=HI, preferred_element_type=F32)
    return u, dt, acs


def _head_expander():
    hw = D_INNER // SSM_HEADS
    return (np.arange(LANES)[:, None] == np.arange(D_INNER)[None, :] // hw).astype(np.float32)


def _select_dot(t, sel):
    hi = t.astype(BF16)
    lo = (t - hi.astype(F32)).astype(BF16)
    return jnp.dot(hi, sel, preferred_element_type=F32) + jnp.dot(lo, sel, preferred_element_type=F32)


def ssd_fwd(xbc, dtraw, dtb, A, dexp, *, name):
    S = xbc.shape[0]
    L = min(SSD_L, S)
    nc = S // L
    N = SSM_STATE
    e_mat = jnp.asarray(_head_expander(), dtype=BF16)

    def body(xs_ref, b_ref, c_ref, dtr_ref, dtb_ref, a_ref, d_ref, e_ref, y_ref, st_out_ref, st_ref, xw_ref):
        c = pl.program_id(0)

        @pl.when(c == 0)
        def _():
            st_ref[...] = jnp.zeros_like(st_ref)

        st_out_ref[0] = st_ref[...]
        ri = lax.broadcasted_iota(jnp.int32, (L, L), 0)
        ci = lax.broadcasted_iota(jnp.int32, (L, L), 1)
        trilb = ri >= ci
        head0 = lax.broadcasted_iota(jnp.int32, (1, LANES), 1) < 64
        _, dt, acs = _ssd_dt(dtr_ref[...], dtb_ref[...], a_ref[...], trilb.astype(F32))
        acsT = acs.T
        last = acs[L - 1:L, :]
        expand = lambda t: _select_dot(t, e_ref[...])
        dte, eae, wte = expand(dt), expand(jnp.exp(acs)), expand(jnp.exp(last - acs) * dt)
        lasts = [last[:, h:h + 1] for h in range(SSM_HEADS)]
        for g in range(SSM_GROUPS):
            Bg = b_ref[:, g * N:(g + 1) * N].astype(BF16)
            Cg = c_ref[:, g * N:(g + 1) * N].astype(BF16)
            CB = lax.dot_general(Cg, Bg, _NT, preferred_element_type=F32)
            Z = lax.dot_general(Cg, st_ref[g * GW:(g + 1) * GW, :].astype(BF16), _NT, preferred_element_type=F32)
            for q in range(PAIRS_PER_GROUP):
                hp = g * PAIRS_PER_GROUP + q
                sl = slice(hp * LANES, (hp + 1) * LANES)
                xs = xs_ref[:, sl]
                xd = xs * dte[:, sl]
                ms, xh = [], []
                for r in range(2):
                    h = 2 * hp + r
                    dec = jnp.exp(jnp.where(trilb, acs[:, h:h + 1] - acsT[h:h + 1, :], -jnp.inf))
                    ms.append((CB * dec).astype(BF16))
                    xh.append(jnp.where(head0 if r == 0 else jnp.logical_not(head0), xd, 0.0).astype(BF16))
                yi = jnp.dot(jnp.concatenate(ms, axis=1), jnp.concatenate(xh, axis=0), preferred_element_type=F32)
                y_ref[:, sl] = yi + Z[:, q * LANES:(q + 1) * LANES] * eae[:, sl] + d_ref[:, sl] * xs
                xw_ref[:, sl] = (xs * wte[:, sl]).astype(BF16)
        for g in range(SSM_GROUPS):
            Bg = b_ref[:, g * N:(g + 1) * N].astype(BF16)
            sn = lax.dot_general(xw_ref[:, g * GW:(g + 1) * GW], Bg, _TN, preferred_element_type=F32)
            for k in range(HEADS_PER_GROUP):
                h = g * HEADS_PER_GROUP + k
                rows = slice(h * 64, (h + 1) * 64)
                st_ref[rows, :] = st_ref[rows, :] * jnp.exp(lasts[h]) + sn[k * 64:(k + 1) * 64, :]

    return pl.pallas_call(
        body, name=name, grid=(nc,),
        in_specs=[pl.BlockSpec((L, D_INNER), lambda c: (c, 0)),
                  pl.BlockSpec((L, SSM_GROUPS * N), lambda c: (c, D_INNER // (SSM_GROUPS * N))),
                  pl.BlockSpec((L, SSM_GROUPS * N), lambda c: (c, D_INNER // (SSM_GROUPS * N) + 1)),
                  pl.BlockSpec((L, LANES), lambda c: (c, 0)),
                  pl.BlockSpec((1, LANES), lambda c: (0, 0)),
                  pl.BlockSpec((1, LANES), lambda c: (0, 0)),
                  pl.BlockSpec((1, D_INNER), lambda c: (0, 0)),
                  pl.BlockSpec((LANES, D_INNER), lambda c: (0, 0))],
        out_specs=[pl.BlockSpec((L, D_INNER), lambda c: (c, 0)),
                   pl.BlockSpec((1, D_INNER, N), lambda c: (c, 0, 0))],
        out_shape=[jax.ShapeDtypeStruct((S, D_INNER), F32), jax.ShapeDtypeStruct((nc, D_INNER, N), F32)],
        scratch_shapes=[pltpu.VMEM((D_INNER, N), F32), pltpu.VMEM((L, D_INNER), BF16)],
        compiler_params=_params("arbitrary"),
    )(xbc, xbc, xbc, dtraw, dtb, A, dexp, e_mat)


def ssd_bwd(xbc, dtraw, dtb, A, dexp, states, dy, *, name, rider=None):
    S = xbc.shape[0]
    L = min(SSD_L, S)
    nc = S // L
    N = SSM_STATE
    e_np = _head_expander()
    e_mat, et_mat = jnp.asarray(e_np, dtype=BF16), jnp.asarray(e_np.T, dtype=BF16)

    r_in, r_out, r_shapes, r_sems, r_args = _rider_parts(rider)

    def body(*refs):
        xs_ref, b_ref, c_ref, dtr_ref, dtb_ref, a_ref, d_ref, e_ref, et_ref, st_in_ref, dy_ref = refs[:11]
        o0 = 11 + len(r_in)
        dxs_ref, db_ref, dc_ref, ddtr_ref, da_ref, ddtb_ref, dd_ref = refs[o0:o0 + 7]
        s0 = o0 + 7 + len(r_out)
        dst_ref, xw_ref, dz_ref, r_ref, dsr_ref, dsc_ref = refs[s0:s0 + 6]
        step = pl.program_id(0)
        if rider is not None:
            _ride(rider, refs[11:o0], refs[o0 + 7:s0], refs[s0 + 6:], step == 0, step == nc // 2, step == nc - 1)

        @pl.when(step == 0)
        def _():
            dst_ref[...] = jnp.zeros_like(dst_ref)
            dsr_ref[...] = jnp.zeros_like(dsr_ref)
            dsc_ref[...] = jnp.zeros_like(dsc_ref)
            da_ref[...] = jnp.zeros_like(da_ref)
            ddtb_ref[...] = jnp.zeros_like(ddtb_ref)
            dd_ref[...] = jnp.zeros_like(dd_ref)

        ri = lax.broadcasted_iota(jnp.int32, (L, L), 0)
        ci = lax.broadcasted_iota(jnp.int32, (L, L), 1)
        trilb = ri >= ci
        lane = lax.broadcasted_iota(jnp.int32, (1, LANES), 1)
        sub = lax.broadcasted_iota(jnp.int32, (LANES, 1), 0)
        head0 = lane < 64
        A = a_ref[...]
        u, dt, acs = _ssd_dt(dtr_ref[...], dtb_ref[...], A, trilb.astype(F32))
        acsT = acs.T
        last = acs[L - 1:L, :]
        elast = jnp.exp(last)
        er = jnp.exp(last - acs)
        wt = er * dt
        expand = lambda t: _select_dot(t, e_ref[...])
        dte, eae, wte = expand(dt), expand(jnp.exp(acs)), expand(wt)
        dlast = jnp.zeros((1, LANES), F32)
        dcbs = []
        for g in range(SSM_GROUPS):
            Bg = b_ref[:, g * N:(g + 1) * N].astype(BF16)
            Cg = c_ref[:, g * N:(g + 1) * N].astype(BF16)
            stg = st_in_ref[0, g * GW:(g + 1) * GW, :]
            dstg = dst_ref[g * GW:(g + 1) * GW, :]
            CB = lax.dot_general(Cg, Bg, _NT, preferred_element_type=F32)
            CBT = lax.dot_general(Bg, Cg, _NT, preferred_element_type=F32)
            Z = lax.dot_general(Cg, stg.astype(BF16), _NT, preferred_element_type=F32)
            U = lax.dot_general(Bg, dstg.astype(BF16), _NT, preferred_element_type=F32)
            dcb = jnp.zeros((L, L), F32)
            for q in range(PAIRS_PER_GROUP):
                hp = g * PAIRS_PER_GROUP + q
                sl = slice(hp * LANES, (hp + 1) * LANES)
                qs = slice(q * LANES, (q + 1) * LANES)
                xs = xs_ref[:, sl]
                dyp = dy_ref[:, sl]
                dtp, eap, wp, Dp = dte[:, sl], eae[:, sl], wte[:, sl], d_ref[:, sl]
                xd = xs * dtp
                dy2 = jnp.concatenate([jnp.where(head0, dyp, 0.0), jnp.where(head0, 0.0, dyp)], axis=0).astype(BF16)
                G2 = lax.dot_general(dy2, xd.astype(BF16), _NT, preferred_element_type=F32)
                mts = []
                for r in range(2):
                    h = 2 * hp + r
                    seg = acs[:, h:h + 1] - acsT[h:h + 1, :]
                    dec = jnp.exp(jnp.where(trilb, seg, -jnp.inf))
                    decT = jnp.exp(jnp.where(ri <= ci, -seg, -jnp.inf))
                    gd = G2[r * L:(r + 1) * L, :] * dec
                    dcb = dcb + gd
                    dseg = gd * CB
                    dsr_ref[:, h:h + 1] = jnp.sum(dseg, axis=1, keepdims=True)
                    dsc_ref[h:h + 1, :] = jnp.sum(dseg, axis=0, keepdims=True)
                    mts.append((CBT * decT).astype(BF16))
                dxd = jnp.dot(jnp.concatenate(mts, axis=1), dy2, preferred_element_type=F32)
                Up = U[:, qs]
                r_ref[0:L, sl] = dyp * Z[:, qs] * eap
                r_ref[L:2 * L, sl] = dxd * xs
                r_ref[2 * L:3 * L, sl] = Up * xs
                dz_ref[:, sl] = (dyp * eap).astype(BF16)
                xw_ref[:, sl] = (xs * wp).astype(BF16)
                dxs_ref[:, sl] = dxd * dtp + Dp * dyp + Up * wp
                dd_ref[:, sl] += jnp.sum(dyp * xs, axis=0, keepdims=True)
            dcbs.append(dcb)
            t = dstg * stg
            for k in range(HEADS_PER_GROUP):
                dlast = dlast + jnp.where(lane == g * HEADS_PER_GROUP + k,
                                          jnp.sum(t[k * 64:(k + 1) * 64, :], keepdims=True), 0.0)
        fold = lambda k: _select_dot(r_ref[k * L:(k + 1) * L, :], et_ref[...])
        r1, r2, dws = fold(0), fold(1), fold(2)
        dww = dws * wt
        ddt = r2 + dws * er
        dacs = r1 - dww + dsr_ref[...] - dsc_ref[...].T
        dlast = dlast * elast + jnp.sum(dww, axis=0, keepdims=True)
        lasts = [last[:, h:h + 1] for h in range(SSM_HEADS)]
        for g in range(SSM_GROUPS):
            Bg = b_ref[:, g * N:(g + 1) * N].astype(BF16)
            Cg = c_ref[:, g * N:(g + 1) * N].astype(BF16)
            gs = slice(g * GW, (g + 1) * GW)
            stb = st_in_ref[0, gs, :].astype(BF16)
            dstb = dst_ref[gs, :].astype(BF16)
            dcbb = dcbs[g].astype(BF16)
            dzg = dz_ref[:, gs]
            dc_ref[:, g * N:(g + 1) * N] = (jnp.dot(dzg, stb, preferred_element_type=F32)
                                            + jnp.dot(dcbb, Bg, preferred_element_type=F32))
            db_ref[:, g * N:(g + 1) * N] = (jnp.dot(xw_ref[:, gs], dstb, preferred_element_type=F32)
                                            + lax.dot_general(dcbb, Cg, _TN, preferred_element_type=F32))
            dsn = lax.dot_general(dzg, Cg, _TN, preferred_element_type=F32)
            for k in range(HEADS_PER_GROUP):
                h = g * HEADS_PER_GROUP + k
                rows = slice(h * 64, (h + 1) * 64)
                dst_ref[rows, :] = dst_ref[rows, :] * jnp.exp(lasts[h]) + dsn[k * 64:(k + 1) * 64, :]
        rowi = lax.broadcasted_iota(jnp.int32, (L, 1), 0)
        dacs = dacs + jnp.where(rowi == L - 1, dlast, 0.0)
        da = jnp.dot((ci >= ri).astype(F32), dacs, precision=HI, preferred_element_type=F32)
        ddt = ddt + da * A
        da_ref[...] += jnp.sum(da * dt, axis=0, keepdims=True)
        ddtr = jnp.where(lane < SSM_HEADS, ddt * _sigmoid(u), 0.0)
        ddtr_ref[...] = ddtr
        ddtb_ref[...] += jnp.sum(ddtr, axis=0, keepdims=True)

    rev = lambda c: nc - 1 - c
    gn = SSM_GROUPS * N
    res = pl.pallas_call(
        body, name=name, grid=(nc,),
        in_specs=[pl.BlockSpec((L, D_INNER), lambda c: (rev(c), 0)),
                  pl.BlockSpec((L, gn), lambda c: (rev(c), D_INNER // gn)),
                  pl.BlockSpec((L, gn), lambda c: (rev(c), D_INNER // gn + 1)),
                  pl.BlockSpec((L, LANES), lambda c: (rev(c), 0)),
                  pl.BlockSpec((1, LANES), lambda c: (0, 0)),
                  pl.BlockSpec((1, LANES), lambda c: (0, 0)),
                  pl.BlockSpec((1, D_INNER), lambda c: (0, 0)),
                  pl.BlockSpec((LANES, D_INNER), lambda c: (0, 0)),
                  pl.BlockSpec((D_INNER, LANES), lambda c: (0, 0)),
                  pl.BlockSpec((1, D_INNER, N), lambda c: (rev(c), 0, 0)),
                  pl.BlockSpec((L, D_INNER), lambda c: (rev(c), 0))] + r_in,
        out_specs=[pl.BlockSpec((L, D_INNER), lambda c: (rev(c), 0)),
                   pl.BlockSpec((L, gn), lambda c: (rev(c), 0)),
                   pl.BlockSpec((L, gn), lambda c: (rev(c), 0)),
                   pl.BlockSpec((L, LANES), lambda c: (rev(c), 0)),
                   pl.BlockSpec((1, LANES), lambda c: (0, 0)),
                   pl.BlockSpec((1, LANES), lambda c: (0, 0)),
                   pl.BlockSpec((1, D_INNER), lambda c: (0, 0))] + r_out,
        out_shape=[jax.ShapeDtypeStruct((S, D_INNER), F32), jax.ShapeDtypeStruct((S, gn), F32),
                   jax.ShapeDtypeStruct((S, gn), F32), jax.ShapeDtypeStruct((S, LANES), F32),
                   jax.ShapeDtypeStruct((1, LANES), F32), jax.ShapeDtypeStruct((1, LANES), F32),
                   jax.ShapeDtypeStruct((1, D_INNER), F32)] + r_shapes,
        scratch_shapes=[pltpu.VMEM((D_INNER, N), F32), pltpu.VMEM((L, D_INNER), BF16), pltpu.VMEM((L, D_INNER), BF16),
                        pltpu.VMEM((3 * L, D_INNER), F32), pltpu.VMEM((L, LANES), F32), pltpu.VMEM((LANES, L), F32)]
                       + r_sems,
        compiler_params=pltpu.CompilerParams(dimension_semantics=("arbitrary",), vmem_limit_bytes=VMEM_LIMIT,
                                             has_side_effects=rider is not None),
    )(xbc, xbc, xbc, dtraw, dtb, A, dexp, e_mat, et_mat, states, dy, *r_args)
    return res[:7], res[7:]


BAND_A = (A_PREV + 1) * CHUNK
REL_W = 640


def _relpos_select():
    k = np.arange(REL_W)
    rel = np.where(k < BAND_A, A_PREV * CHUNK - k, A_PREV * CHUNK - (k - REL_W))
    idx = np.clip(rel, -MAX_REL, MAX_REL) + MAX_REL
    sel = (np.arange(REL_W)[:, None] == idx[None, :]) & (k != BAND_A)[None, :]
    return sel.astype(np.float32)


def relpos_bias(table_pad, *, name):
    def body(t_ref, s_ref, o_ref):
        v = jnp.dot(t_ref[...], s_ref[...], precision=HI, preferred_element_type=F32)
        for h in range(N_HEADS):
            o_ref[h] = pltpu.roll(jnp.broadcast_to(v[h:h + 1, :], (CHUNK, REL_W)), 0, 1, stride=1, stride_axis=0)

    return pl.pallas_call(body, name=name, out_shape=jax.ShapeDtypeStruct((N_HEADS, CHUNK, REL_W), F32),
                          compiler_params=pltpu.CompilerParams(vmem_limit_bytes=VMEM_LIMIT),
                          )(table_pad, jnp.asarray(_relpos_select()))


def relpos_grad(dbias_rev, *, name):
    def body(d_ref, s_ref, o_ref):
        head = lax.broadcasted_iota(jnp.int32, (N_HEADS, 1), 0)
        dv = jnp.zeros((N_HEADS, REL_W), F32)
        for h in range(N_HEADS):
            back = pltpu.roll(d_ref[h], REL_W - (CHUNK - 1), 1, stride=1, stride_axis=0)
            dv = dv + jnp.where(head == h, jnp.sum(back, axis=0, keepdims=True), 0.0)
        o_ref[...] = lax.dot_general(dv, s_ref[...], _NT, precision=HI, preferred_element_type=F32)

    return pl.pallas_call(body, name=name, out_shape=jax.ShapeDtypeStruct((N_HEADS, REL_W), F32),
                          compiler_params=pltpu.CompilerParams(vmem_limit_bytes=VMEM_LIMIT),
                          )(dbias_rev, jnp.asarray(_relpos_select()))


def matmul_loss(a, b, residual, t, *, name):
    (M, K), (_, D) = a.shape, b.shape
    tm = _pick(M, MM_ROWS)

    def body(a_ref, b_ref, r_ref, t_ref, dy_ref, dyb_ref, l_ref):
        y = jnp.dot(a_ref[...].astype(BF16), b_ref[...].astype(BF16), preferred_element_type=F32) + r_ref[...]
        e = y - t_ref[...]
        dy_ref[...] = e * (1.0 / D)
        dyb_ref[...] = (e * (1.0 / D)).astype(BF16)

        @pl.when(pl.program_id(0) == 0)
        def _():
            l_ref[...] = jnp.zeros_like(l_ref)

        part = jnp.sum(jnp.sum(e * e, axis=1, keepdims=True), axis=0, keepdims=True) * (0.5 / D)
        l_ref[...] += jnp.broadcast_to(part, l_ref.shape)

    row = pl.BlockSpec((tm, D), lambda i: (i, 0))
    return pl.pallas_call(
        body, name=name, grid=(M // tm,),
        in_specs=[pl.BlockSpec((tm, K), lambda i: (i, 0)), pl.BlockSpec((K, D), lambda i: (0, 0)), row, row],
        out_specs=[row, row, pl.BlockSpec((1, LANES), lambda i: (0, 0))],
        out_shape=[jax.ShapeDtypeStruct((M, D), F32), jax.ShapeDtypeStruct((M, D), BF16),
                   jax.ShapeDtypeStruct((1, LANES), F32)],
        compiler_params=_params("arbitrary"),
    )(a, b, residual, t)


def f_adamw(w, g, m, v):
    m = ADAM_B1 * m + (1.0 - ADAM_B1) * g
    v = ADAM_B2 * v + (1.0 - ADAM_B2) * (g * g)
    m_hat = m / (1.0 - ADAM_B1 ** ADAM_STEP)
    v_hat = v / (1.0 - ADAM_B2 ** ADAM_STEP)
    delta = -ADAM_LR * (m_hat / (jnp.sqrt(v_hat) + ADAM_EPS) + ADAM_WD * w)
    return delta, m, v


ANY = pl.BlockSpec(memory_space=pl.ANY)


def _pos():
    return lax.axis_index("x"), lax.axis_index("y"), lax.axis_index("c")


def _other_chips(x, y):
    return [(1 - x, y), (x, 1 - y), (1 - x, 1 - y)]


class Rider:
    def __init__(self, ins, outs, sems, start, mid, finish):
        self.ins, self.outs, self.sems = list(ins), list(outs), list(sems)
        self.start, self.mid, self.finish = start, mid, finish


def _rider_parts(rider):
    if rider is None:
        return [], [], [], [], []
    return [ANY] * len(rider.ins), [ANY] * len(rider.outs), rider.outs, rider.sems, rider.ins


def _ride(rider, ins, outs, sems, first, mid, last):
    pos = _pos()

    @pl.when(first)
    def _():
        rider.start(pos, ins, outs, sems)

    if rider.mid is not None:
        @pl.when(mid)
        def _():
            rider.mid(pos, ins, outs, sems)

    @pl.when(last)
    def _():
        rider.finish(pos, ins, outs, sems)


def run_rider(rider, *, name):
    n_in, n_out = len(rider.ins), len(rider.outs)

    def body(*refs):
        ins, outs, sems = refs[:n_in], refs[n_in:n_in + n_out], refs[n_in + n_out:]
        pos = _pos()
        rider.start(pos, ins, outs, sems)
        if rider.mid is not None:
            rider.mid(pos, ins, outs, sems)
        rider.finish(pos, ins, outs, sems)

    return pl.pallas_call(
        body, name=name, in_specs=[ANY] * n_in, out_specs=[ANY] * n_out, out_shape=rider.outs,
        scratch_shapes=rider.sems, compiler_params=pltpu.CompilerParams(has_side_effects=True),
    )(*rider.ins)


def gather_rider(shards):
    n = len(shards)

    def copies(pos, ins, outs, sems):
        x, y, c = pos
        send, recv, fsend, frecv = sems
        me = 2 * x + y
        sib = (x, y, 1 - c)
        first, arrive, passed, theirs = [], [], [], []
        for i in range(n):
            for j, (px, py) in enumerate(_other_chips(x, y)):
                k = 3 * i + j
                far = dict(device_id=(px, py, c), device_id_type=MESH)
                near = dict(device_id=sib, device_id_type=MESH)
                got = outs[i].at[2 * px + py, c]
                his = outs[i].at[2 * px + py, 1 - c]
                first.append(pltpu.make_async_remote_copy(ins[i].at[c], outs[i].at[me, c], send.at[k], recv.at[k], **far))
                arrive.append(pltpu.make_async_remote_copy(ins[i].at[c], got, send.at[k], recv.at[k], **far))
                passed.append(pltpu.make_async_remote_copy(got, got, fsend.at[k], frecv.at[k], **near))
                theirs.append(pltpu.make_async_remote_copy(his, his, fsend.at[k], frecv.at[k], **near))
        return first, arrive, passed, theirs

    def start(*a):
        for cp in copies(*a)[0]:
            cp.start()

    def mid(*a):
        _, arrive, passed, _ = copies(*a)
        for got, cp in zip(arrive, passed):
            got.wait_recv()
            cp.start()

    def finish(*a):
        first, _, passed, theirs = copies(*a)
        for cp in theirs:
            cp.wait_recv()
        for cp in first + passed:
            cp.wait_send()

    return Rider(shards, [jax.ShapeDtypeStruct((4,) + s.shape, s.dtype) for s in shards],
                 [pltpu.SemaphoreType.DMA((3 * n,))] * 4, start, mid, finish)


def scatter_rider(ps):
    n = len(ps)

    def copies(pos, ins, outs, sems):
        x, y, c = pos
        send, recv = sems
        return [pltpu.make_async_remote_copy(ins[i].at[2 * px + py], outs[i].at[j], send.at[3 * i + j], recv.at[3 * i + j],
                                             device_id=(px, py, c), device_id_type=MESH)
                for i in range(n) for j, (px, py) in enumerate(_other_chips(x, y))]

    def start(*a):
        for cp in copies(*a):
            cp.start()

    def finish(*a):
        for cp in copies(*a):
            cp.wait()

    return Rider(ps, [jax.ShapeDtypeStruct((3,) + p.shape[1:], p.dtype) for p in ps],
                 [pltpu.SemaphoreType.DMA((3 * n,))] * 2, start, None, finish)


def pair_swap_halves(gs, *, name):
    n = len(gs)

    def body(*refs):
        ins, outs = refs[:n], refs[n:2 * n]
        send, recv = refs[2 * n:]
        x, y, c = _pos()
        cps = []
        for i in range(n):
            cp = pltpu.make_async_remote_copy(ins[i].at[1 - c], outs[i], send.at[i], recv.at[i],
                                              device_id=(x, y, 1 - c), device_id_type=MESH)
            cp.start()
            cps.append(cp)
        for cp in cps:
            cp.wait()

    return pl.pallas_call(
        body, name=name, in_specs=[ANY] * n, out_specs=[ANY] * n,
        out_shape=[jax.ShapeDtypeStruct(g.shape[1:], g.dtype) for g in gs],
        scratch_shapes=[pltpu.SemaphoreType.DMA((n,)), pltpu.SemaphoreType.DMA((n,))],
        compiler_params=pltpu.CompilerParams(has_side_effects=True),
    )(*gs)


def pair_share(hs, *, name):
    n = len(hs)

    def body(*refs):
        ins, outs = refs[:n], refs[n:2 * n]
        send, recv = refs[2 * n:]
        x, y, c = _pos()
        cps = []
        for i in range(n):
            cp = pltpu.make_async_remote_copy(ins[i], outs[i].at[c], send.at[i], recv.at[i],
                                              device_id=(x, y, 1 - c), device_id_type=MESH)
            cp.start()
            cps.append(cp)
        for i, cp in enumerate(cps):
            cp.wait_send()
            pltpu.make_async_remote_copy(ins[i], outs[i].at[1 - c], send.at[i], recv.at[i],
                                         device_id=(x, y, 1 - c), device_id_type=MESH).wait_recv()

    return pl.pallas_call(
        body, name=name, in_specs=[ANY] * n, out_specs=[ANY] * n,
        out_shape=[jax.ShapeDtypeStruct((2,) + h.shape, h.dtype) for h in hs],
        scratch_shapes=[pltpu.SemaphoreType.DMA((n,)), pltpu.SemaphoreType.DMA((n,))],
        compiler_params=pltpu.CompilerParams(has_side_effects=True),
    )(*hs)


def gather_all(buf, *, name):
    def body(in_ref, out_ref, send, recv, loc):
        x, y, c = _pos()
        lid = 4 * x + 2 * y + c
        lc = pltpu.make_async_copy(in_ref, out_ref.at[lid], loc.at[0])
        lc.start()
        cps = []
        for k in range(1, 8):
            px = 1 - x if k & 4 else x
            py = 1 - y if k & 2 else y
            pc = 1 - c if k & 1 else c
            cp = pltpu.make_async_remote_copy(in_ref, out_ref.at[lid], send.at[k - 1], recv.at[k - 1],
                                              device_id=(px, py, pc), device_id_type=MESH)
            cp.start()
            cps.append((cp, 4 * px + 2 * py + pc, (px, py, pc)))
        for k, (cp, plid, peer) in enumerate(cps):
            cp.wait_send()
            pltpu.make_async_remote_copy(in_ref, out_ref.at[plid], send.at[k], recv.at[k],
                                         device_id=peer, device_id_type=MESH).wait_recv()
        lc.wait()

    return pl.pallas_call(
        body, name=name, in_specs=[ANY], out_specs=ANY,
        out_shape=jax.ShapeDtypeStruct((8,) + buf.shape, buf.dtype),
        scratch_shapes=[pltpu.SemaphoreType.DMA((7,)), pltpu.SemaphoreType.DMA((7,)), pltpu.SemaphoreType.DMA((1,))],
        compiler_params=pltpu.CompilerParams(has_side_effects=True),
    )(buf)


def sum_slots(a, *, name):
    n = a.shape[0]

    def body(a_ref, o_ref):
        acc = a_ref[0]
        for k in range(1, n):
            acc = acc + a_ref[k]
        o_ref[...] = acc

    return pl.pallas_call(body, name=name, out_shape=jax.ShapeDtypeStruct(a.shape[1:], a.dtype),
                          compiler_params=pltpu.CompilerParams(vmem_limit_bytes=VMEM_LIMIT))(a)


def _row_tile(r, want, mult=16):
    t = (min(want, r) // mult) * mult
    while t >= mult:
        if r % t == 0:
            return t
        t -= mult
    return r


def pair_add(g, r1, csel, *, name):
    _, _, r, C = g.shape
    tr = _row_tile(r, 256)

    def body(g_ref, r_ref, c_ref, p32_ref, pb_ref):
        south = c_ref[0:1, 0:1] == 0.0
        p = jnp.where(south, g_ref[0, 0], g_ref[1, 0]) + r_ref[0]
        p32_ref[0] = p
        pb_ref[0] = p.astype(BF16)

    return pl.pallas_call(
        body, name=name, grid=(4, r // tr),
        in_specs=[pl.BlockSpec((2, 1, tr, C), lambda j, t: (0, j, t, 0)), pl.BlockSpec((1, tr, C), lambda j, t: (j, t, 0)),
                  pl.BlockSpec((1, LANES), lambda j, t: (0, 0))],
        out_specs=[pl.BlockSpec((1, tr, C), lambda j, t: (j, t, 0))] * 2,
        out_shape=[jax.ShapeDtypeStruct((4, r, C), F32), jax.ShapeDtypeStruct((4, r, C), BF16)],
        compiler_params=_params("parallel", "parallel"),
    )(g, r1, csel)


def chip_add(p32, r3, msel, *, name):
    _, r, C = p32.shape
    tr = _row_tile(r, 128)

    def body(p_ref, r_ref, m_ref, o_ref):
        me = m_ref[0:1, 0:1]
        acc = jnp.where(me == 0.0, p_ref[0], jnp.where(me == 1.0, p_ref[1], jnp.where(me == 2.0, p_ref[2], p_ref[3])))
        for j in range(3):
            acc = acc + r_ref[j].astype(F32)
        o_ref[...] = acc

    return pl.pallas_call(
        body, name=name, grid=(r // tr,),
        in_specs=[pl.BlockSpec((4, tr, C), lambda t: (0, t, 0)), pl.BlockSpec((3, tr, C), lambda t: (0, t, 0)),
                  pl.BlockSpec((1, LANES), lambda t: (0, 0))],
        out_specs=pl.BlockSpec((tr, C), lambda t: (t, 0)),
        out_shape=jax.ShapeDtypeStruct((r, C), F32),
        compiler_params=_params("parallel"),
    )(p32, r3, msel)


def _consts():
    i512 = np.arange(N_HEADS * HEAD_DIM)
    i128 = np.arange(LANES)
    bd512 = (i512[:, None] // HEAD_DIM == i512[None, :] // HEAD_DIM).astype(np.float32)
    bd128 = (i128[:, None] // HEAD_DIM == i128[None, :] // HEAD_DIM).astype(np.float32)
    fold = (np.arange(HEAD_DIM)[:, None] == (i512[None, :] % HEAD_DIM)).astype(np.float32)
    grp = N_HEADS // 2 * HEAD_DIM
    expand = ((i128[:, None] // HEAD_DIM == i512[None, :] // grp)
              & (i128[:, None] % HEAD_DIM == i512[None, :] % HEAD_DIM)).astype(np.float32)
    band = (B_PREV + 1) * CHUNK
    rel = np.arange(CHUNK)[:, None] - (np.arange(band)[None, :] - B_PREV * CHUNK)
    slopes = 2.0 ** (-8.0 * np.arange(1, N_HEADS + 1, dtype=np.float32) / N_HEADS)
    bias_b = (-slopes[:, None, None] * np.abs(rel).astype(np.float32)[None]).astype(np.float32)
    return [jnp.asarray(a) for a in (bd512, bd128, fold, expand)], jnp.asarray(bias_b)


def _ffn_fwd(xin, h, l, W, P, next_gain=None, target=None):
    Wi = W["ffn_in"][l]
    gate = matmul(h, Wi[:, :D_FF], mode="nn", name=f"ffn{l}_gate", out_dtype=BF16)
    up = matmul(h, Wi[:, D_FF:], mode="nn", name=f"ffn{l}_up", out_dtype=BF16)
    gc, act = dwconv_fwd(gate, P["ffn_conv_w"][l], P["ffn_conv_b"][l:l + 1], lambda y, u: (y, _silu(y) * u), [up],
                         [BF16, BF16], name=f"ffn{l}_conv")
    saved = (xin, h, gate, gc, up, act)
    if target is not None:
        dxout, dxout_b, lpart = matmul_loss(act, W["ffn_out"][l], xin, target, name=f"ffn{l}_out")
        return (dxout, dxout_b), saved, lpart
    xout, h_next = matmul_norm(act, W["ffn_out"][l], xin, next_gain, name=f"ffn{l}_out")
    return xout, saved, h_next


def _ffn_bwd(dxout, l, saved, W, P):
    xin, h, gate, gc, up, act = saved
    g = P["norm_ffn"][l:l + 1]
    Wi = W["ffn_in"][l]
    dxout, dxout_b = dxout
    dact = matmul(dxout_b, W["ffn_out"][l], mode="nt", name=f"ffn{l}_dact", out_dtype=BF16)
    dWo = matmul(act, dxout_b, mode="tn", name=f"ffn{l}_dwout")
    dgate, dcw, dcb, dup = dwconv_bwd(gate, P["ffn_conv_w"][l], [gc, up, dact],
                                      lambda c, u, da: (da * u * _dsilu(c), da * _silu(c)), [BF16],
                                      name=f"ffn{l}_dconv")
    dh = matmul(dgate, Wi[:, :D_FF], mode="nt", name=f"ffn{l}_dh_gate")
    dxin_f, dxin_b, dg = matmul_dnorm(dup, Wi[:, D_FF:], dh, xin, g, dxout, name=f"ffn{l}_dh_up")
    dxin = (dxin_f, dxin_b)
    dWi = jnp.concatenate([matmul(h, dgate, mode="tn", name=f"ffn{l}_dw_gate"),
                           matmul(h, dup, mode="tn", name=f"ffn{l}_dw_up")], axis=1)
    return dxin, dWi, dWo, dg, dcw, dcb


class NoComm:
    def fwd_rider(self, tag):
        return None

    def fwd_done(self, tag, outs, W, P):
        pass

    def grads(self, tag, cols, rows):
        return None

    def bwd_done(self, tag, outs):
        pass


def local_step(x, tgt, W, P, comm):
    qk_consts, bias_b = _consts()
    DQ = N_HEADS * HEAD_DIM
    tile = _row_tile(x.shape[0], QK_ROWS)
    kv_lead = [-(-n * CHUNK // tile) * tile for n in (A_PREV, B_PREV)]

    g_mix0 = P["norm_mix"][0:1]
    (h0,) = rowwise(f_rmsnorm, [x], [g_mix0], [(D_MODEL, BF16)], name="attn_norm", tm=512)
    qkv = matmul(h0, W["attn_in"], mode="nn", name="attn_qkv")
    qk_par = [P["q_norm_a"], P["k_norm_a"], P["q_norm_b"], P["k_norm_b"]] + qk_consts
    qk_lead = [0, kv_lead[0], kv_lead[0], 0, kv_lead[1], kv_lead[1]]
    qa, ka, va, qb, kb, vb = rowwise(f_qknorm, [qkv], qk_par, [(DQ, BF16)] * 6, name="attn_qknorm", lead=qk_lead,
                                     tm=QK_ROWS)
    table = jnp.pad(P["relpos_table"], ((0, 0), (0, REL_W - (2 * MAX_REL + 1))))
    nj = min(ATT_TQ, x.shape[0]) // CHUNK
    bias_a = widen_bias(relpos_bias(table, name="relpos_bias")[:, :, :BAND_A], A_PREV, nj)
    bias_b = widen_bias(bias_b, B_PREV, nj)
    sinks = jnp.broadcast_to(P["sinks"].reshape(N_HEADS // 2, 2, 1), (N_HEADS // 2, 2, LANES))
    oa, late = attn_fwd(qa, ka, va, bias_a, None, n_prev=A_PREV, name="attn_a", rider=comm.fwd_rider("a"))
    comm.fwd_done("a", late, W, P)
    ob, late = attn_fwd(qb, kb, vb, bias_b, sinks, n_prev=B_PREV, name="attn_b", rider=comm.fwd_rider("b"))
    comm.fwd_done("b", late, W, P)
    Wao = W["attn_out"]
    x1 = matmul(oa, Wao[:DQ], mode="nn", name="attn_out_a", residual=x)
    x1, hf0 = matmul_norm(ob, Wao[DQ:], x1, P["norm_ffn"][0:1], name="attn_out_b")
    g_mix1 = P["norm_mix"][1:2]
    x2, ffn0, h2 = _ffn_fwd(x1, hf0, 0, W, P, g_mix1)

    Ws = W["ssm_in"]
    CC = D_INNER + 2 * SSM_GROUPS * SSM_STATE
    Wz, Wx = Ws[:, :D_INNER], Ws[:, D_INNER:D_INNER + CC]
    Wdt = jnp.pad(Ws[:, D_INNER + CC:], ((0, 0), (0, LANES - SSM_HEADS)))
    z = matmul(h2, Wz, mode="nn", name="ssm_z", out_dtype=BF16)
    xr = matmul(h2, Wx, mode="nn", name="ssm_xbc", out_dtype=BF16)
    dtraw = matmul(h2, Wdt, mode="nn", name="ssm_dt")
    xc, xbc = dwconv_fwd(xr, P["ssm_conv_w"], P["ssm_conv_b"], lambda y: (y, _silu(y)), [], [BF16, F32],
                         name="ssm_conv")
    pad32 = lambda v: jnp.pad(v, ((0, 0), (0, LANES - SSM_HEADS)))
    A = pad32(-jnp.exp(P["ssm_a_log"]))
    dtb = pad32(P["ssm_dt_bias"])
    dexp = jnp.repeat(P["ssm_d"], D_INNER // SSM_HEADS, axis=1)
    y, states = ssd_fwd(xbc, dtraw, dtb, A, dexp, name="ssd_fwd")
    (y2,) = rowwise(f_gate_norm, [y, z], [P["ssm_norm"]], [(D_INNER, BF16)], name="ssm_gate_norm", tm=512)
    x3, hf1 = matmul_norm(y2, W["ssm_out"], x2, P["norm_ffn"][1:2], name="ssm_out")
    dx4, ffn1, lpart = _ffn_fwd(x3, hf1, 1, W, P, target=tgt)

    dx3, dWfi1, dWfo1, dgf1, dfcw1, dfcb1 = _ffn_bwd(dx4, 1, ffn1, W, P)
    out_f1 = comm.grads("f1", dWfi1, dWfo1)
    dx3, dx3_b = dx3
    dy2 = matmul(dx3_b, W["ssm_out"], mode="nt", name="ssm_dy")
    dWso = matmul(y2, dx3_b, mode="tn", name="ssm_dwout")
    dy, dz, dnw = rowwise_vjp(f_gate_norm, [y, z], [P["ssm_norm"]], [dy2], [(0, F32), (1, BF16)], [0],
                              name="ssm_dgate_norm")
    (dxs, dB, dC, ddtraw, dA, ddtb, dDl), sent = ssd_bwd(xbc, dtraw, dtb, A, dexp, states, dy, name="ssd_bwd",
                                                          rider=out_f1)
    comm.bwd_done("f1", sent)
    dxr, dscw, dscb = dwconv_bwd(xr, P["ssm_conv_w"], [xc, (dxs, dB, dC)], lambda c, g: (g * _dsilu(c),), [],
                                 name="ssm_dconv")
    dh2 = matmul(dz, Wz, mode="nt", name="ssm_dh_z")
    dh2 = matmul(dxr, Wx, mode="nt", name="ssm_dh_x", residual=dh2)
    dx2_f, dx2_b, dgm1 = matmul_dnorm(ddtraw, Wdt, dh2, x2, g_mix1, dx3, name="ssm_dh_dt")
    dx2 = (dx2_f, dx2_b)
    dWs = jnp.concatenate([matmul(h2, dz, mode="tn", name="ssm_dw_z"),
                           matmul(h2, dxr, mode="tn", name="ssm_dw_x"),
                           matmul(h2, ddtraw, mode="tn", name="ssm_dw_dt")[:, :SSM_HEADS]], axis=1)
    out_s = comm.grads("s", dWs, dWso)

    dx1, dWfi0, dWfo0, dgf0, dfcw0, dfcb0 = _ffn_bwd(dx2, 0, ffn0, W, P)
    out_f0 = comm.grads("f0", dWfi0, dWfo0)
    dx1, dx1_b = dx1
    doa = matmul(dx1_b, Wao[:DQ], mode="nt", name="attn_do_a", out_dtype=BF16)
    dob = matmul(dx1_b, Wao[DQ:], mode="nt", name="attn_do_b", out_dtype=BF16)
    dWao = jnp.concatenate([matmul(oa, dx1_b, mode="tn", name="attn_dwout_a"),
                            matmul(ob, dx1_b, mode="tn", name="attn_dwout_b")], axis=0)
    (dqa, dka, dva, dbias_a), sent = attn_bwd(qa, ka, va, doa, bias_a, None, n_prev=A_PREV, name="attn_a_bwd",
                                              rider=out_s)
    comm.bwd_done("s", sent)
    (dqb, dkb, dvb, _, dsk), sent = attn_bwd(qb, kb, vb, dob, bias_b, sinks, n_prev=B_PREV, name="attn_b_bwd",
                                             rider=out_f0)
    comm.bwd_done("f0", sent)
    dqkv, dgqa, dgka, dgqb, dgkb = rowwise_vjp(f_qknorm, [qkv], qk_par, [dqa, dka, dva, dqb, dkb, dvb],
                                               [(0, BF16)], [0, 1, 2, 3], name="attn_dqknorm", cot_skip=qk_lead,
                                               tm=QK_ROWS)
    dx, _, dgm0 = matmul_dnorm(dqkv, W["attn_in"], None, x, g_mix0, dx1, name="attn_dh")
    dWai = matmul(h0, dqkv, mode="tn", name="attn_dwin")
    dbias_a = fold_bias(dbias_a, A_PREV, nj)
    dbias_rev = jnp.pad(dbias_a[:, ::-1, :], ((0, 0), (0, 0), (0, REL_W - BAND_A)))
    dtable = relpos_grad(dbias_rev, name="relpos_grad")[:, :2 * MAX_REL + 1]

    gW = {"attn_in": dWai, "attn_out": dWao, "ssm_in": dWs, "ssm_out": dWso,
          "ffn_in": [dWfi0, dWfi1], "ffn_out": [dWfo0, dWfo1]}
    gP = {"norm_mix": jnp.concatenate([dgm0, dgm1], axis=0),
          "norm_ffn": jnp.concatenate([dgf0, dgf1], axis=0),
          "relpos_table": dtable, "q_norm_a": dgqa, "k_norm_a": dgka, "q_norm_b": dgqb, "k_norm_b": dgkb,
          "sinks": dsk[:, :, 0].reshape(1, N_HEADS),
          "ssm_conv_w": dscw, "ssm_conv_b": dscb,
          "ssm_dt_bias": ddtb[:, :SSM_HEADS], "ssm_a_log": dA[:, :SSM_HEADS] * A[:, :SSM_HEADS],
          "ssm_d": dDl.reshape(SSM_HEADS, D_INNER // SSM_HEADS).sum(axis=1).reshape(1, SSM_HEADS),
          "ssm_norm": dnw,
          "ffn_conv_w": jnp.stack([dfcw0, dfcw1]), "ffn_conv_b": jnp.concatenate([dfcb0, dfcb1], axis=0)}
    return lpart, dx, gW, gP


WEIGHTS = ["norm_mix", "norm_ffn", "attn_w_in", "attn_w_out", "relpos_table", "q_norm_a", "k_norm_a", "q_norm_b",
           "k_norm_b", "sinks", "ssm_w_in", "ssm_conv_w", "ssm_conv_b", "ssm_dt_bias", "ssm_a_log", "ssm_d",
           "ssm_norm", "ssm_w_out", "ffn_w_in", "ffn_conv_w", "ffn_conv_b", "ffn_w_out"]
ARGS = ["x"] + WEIGHTS + ["loss_target"] + ["m_" + w for w in WEIGHTS] + ["v_" + w for w in WEIGHTS]
N_CHIPS = 4
SMALL_ROWS = 384
SMALL_ORDER = ["norm_mix", "norm_ffn", "relpos_table", "q_norm_a", "k_norm_a", "q_norm_b", "k_norm_b", "sinks",
               "ssm_dt_bias", "ssm_a_log", "ssm_d", "ffn_conv_b", "ssm_conv_w", "ssm_conv_b", "ssm_norm", "ffn_conv_w"]


def _cols_to_slabs(g):
    K, N = g.shape
    return g.reshape(2, K // 2, N_CHIPS, N // N_CHIPS).transpose(0, 2, 1, 3)


def _rows_to_slabs(g):
    R, C = g.shape
    return g.reshape(N_CHIPS, 2, R // (2 * N_CHIPS), C).transpose(1, 0, 2, 3)


class MeshComm:
    def __init__(self, d, xi, yi, ci):
        self.d, self.ci, self.me = d, ci, 2 * xi + yi
        self.csel = jnp.full((1, LANES), ci, F32)
        self.msel = jnp.full((1, LANES), self.me, F32)
        halves = lambda w: w.reshape((2, -1, w.shape[-1]))
        small = jnp.concatenate([d[k].reshape(-1) for k in ("ssm_conv_w", "ssm_conv_b", "ssm_norm", "ffn_conv_w")])
        small = jnp.pad(small, (0, 2 * 40 * LANES - small.shape[0])).reshape(2, 40, LANES)
        self.shards = {"attn": [halves(d["attn_w_in"][0].astype(BF16)), halves(d["attn_w_out"][0].astype(BF16))],
                       "a": [d["ffn_w_in"].astype(BF16), small],
                       "b": [d["ffn_w_out"].astype(BF16), halves(d["ssm_w_in"][0].astype(BF16)),
                             halves(d["ssm_w_out"][0].astype(BF16))]}
        self.p32, self.mine = {}, {}

    def _whole(self, tag, outs):
        return [lax.dynamic_update_slice_in_dim(g, s[None], self.me, axis=0) for g, s in zip(outs, self.shards[tag])]

    @staticmethod
    def _cat_cols(g):
        return jnp.concatenate([g[j].reshape((-1, g.shape[-1])) for j in range(N_CHIPS)], axis=1)

    def first_weights(self):
        g_ai, g_ao = self._whole("attn", run_rider(gather_rider(self.shards["attn"]), name="gather_attn"))
        return {"attn_in": self._cat_cols(g_ai), "attn_out": g_ao.reshape(-1, D_MODEL)}

    def fwd_rider(self, tag):
        return gather_rider(self.shards[tag])

    def fwd_done(self, tag, outs, W, P):
        if tag == "b":
            g_fo, g_si, g_so = self._whole("b", outs)
            W["ffn_out"] = [g_fo[:, l].reshape(-1, D_MODEL) for l in range(2)]
            W["ssm_in"], W["ssm_out"] = self._cat_cols(g_si), g_so.reshape(-1, D_MODEL)
            return
        g_fi, g_sm = self._whole("a", outs)
        W["ffn_in"] = [jnp.concatenate([g_fi[j, l] for j in range(N_CHIPS)], axis=1) for l in range(2)]
        sm = g_sm.reshape(N_CHIPS, -1)
        CC = D_INNER + 2 * SSM_GROUPS * SSM_STATE
        c4, f4 = CC // N_CHIPS, D_FF // N_CHIPS
        o1 = SSM_CONV * c4
        o2 = o1 + c4
        o3 = o2 + D_INNER // N_CHIPS
        o4 = o3 + 2 * FFN_CONV * f4
        P["ssm_conv_w"] = sm[:, :o1].reshape(N_CHIPS, SSM_CONV, c4).transpose(1, 0, 2).reshape(SSM_CONV, CC)
        P["ssm_conv_b"] = sm[:, o1:o2].reshape(1, CC)
        P["ssm_norm"] = sm[:, o2:o3].reshape(1, D_INNER)
        P["ffn_conv_w"] = sm[:, o3:o4].reshape(N_CHIPS, 2, FFN_CONV, f4).transpose(1, 2, 0, 3).reshape(2, FFN_CONV, D_FF)

    def grads(self, tag, cols, rows):
        slabs = [_cols_to_slabs(cols), _rows_to_slabs(rows)]
        from_sib = pair_swap_halves(slabs, name="grad_pair_swap_" + tag)
        pairs = [pair_add(g, r, self.csel, name=f"grad_pair_add_{tag}{i}") for i, (g, r) in enumerate(zip(slabs, from_sib))]
        self.p32[tag] = [p[0] for p in pairs]
        return scatter_rider([p[1] for p in pairs])

    def bwd_done(self, tag, outs):
        self.mine[tag] = [chip_add(p, r, self.msel, name=f"grad_chip_add_{tag}{i}")
                          for i, (p, r) in enumerate(zip(self.p32[tag], outs))]

    def finish(self, d_attn_in, d_attn_out):
        self.bwd_done("at", run_rider(self.grads("at", d_attn_in, d_attn_out), name="grad_scatter_at"))
        order = ["at", "s", "f0", "f1"]
        mine = [m for t in order for m in self.mine[t]]
        theirs = pair_share(mine, name="grad_pair_share")
        full = [lax.dynamic_update_slice_in_dim(b, a[None], self.ci, axis=0).reshape((-1, a.shape[-1]))
                for a, b in zip(mine, theirs)]
        ai, ao, si, so, fi0, fo0, fi1, fo1 = full
        return {"attn_w_in": ai[None], "attn_w_out": ao[None], "ssm_w_in": si[None], "ssm_w_out": so[None],
                "ffn_w_in": jnp.stack([fi0, fi1]), "ffn_w_out": jnp.stack([fo0, fo1])}


def _adamw(w, g, m, v, name):
    shp = w.shape
    two = lambda a: a.reshape((-1, shp[-1]))
    outs = [(shp[-1], F32)] * 3
    d, nm, nv = rowwise(f_adamw, [two(w), two(g), two(m), two(v)], [], outs, name="adamw_" + name)
    return d.reshape(shp), nm.reshape(shp), nv.reshape(shp)


def kernel(x, norm_mix, norm_ffn, attn_w_in, attn_w_out, relpos_table, q_norm_a, k_norm_a, q_norm_b, k_norm_b, sinks, ssm_w_in, ssm_conv_w, ssm_conv_b, ssm_dt_bias, ssm_a_log, ssm_d, ssm_norm, ssm_w_out, ffn_w_in, ffn_conv_w, ffn_conv_b, ffn_w_out, loss_target, m_norm_mix, m_norm_ffn, m_attn_w_in, m_attn_w_out, m_relpos_table, m_q_norm_a, m_k_norm_a, m_q_norm_b, m_k_norm_b, m_sinks, m_ssm_w_in, m_ssm_conv_w, m_ssm_conv_b, m_ssm_dt_bias, m_ssm_a_log, m_ssm_d, m_ssm_norm, m_ssm_w_out, m_ffn_w_in, m_ffn_conv_w, m_ffn_conv_b, m_ffn_w_out, v_norm_mix, v_norm_ffn, v_attn_w_in, v_attn_w_out, v_relpos_table, v_q_norm_a, v_k_norm_a, v_q_norm_b, v_k_norm_b, v_sinks, v_ssm_w_in, v_ssm_conv_w, v_ssm_conv_b, v_ssm_dt_bias, v_ssm_a_log, v_ssm_d, v_ssm_norm, v_ssm_w_out, v_ffn_w_in, v_ffn_conv_w, v_ffn_conv_b, v_ffn_w_out):
    d = dict(zip(ARGS, (x, norm_mix, norm_ffn, attn_w_in, attn_w_out, relpos_table, q_norm_a, k_norm_a, q_norm_b, k_norm_b, sinks, ssm_w_in, ssm_conv_w, ssm_conv_b, ssm_dt_bias, ssm_a_log, ssm_d, ssm_norm, ssm_w_out, ffn_w_in, ffn_conv_w, ffn_conv_b, ffn_w_out, loss_target, m_norm_mix, m_norm_ffn, m_attn_w_in, m_attn_w_out, m_relpos_table, m_q_norm_a, m_k_norm_a, m_q_norm_b, m_k_norm_b, m_sinks, m_ssm_w_in, m_ssm_conv_w, m_ssm_conv_b, m_ssm_dt_bias, m_ssm_a_log, m_ssm_d, m_ssm_norm, m_ssm_w_out, m_ffn_w_in, m_ffn_conv_w, m_ffn_conv_b, m_ffn_w_out, v_norm_mix, v_norm_ffn, v_attn_w_in, v_attn_w_out, v_relpos_table, v_q_norm_a, v_k_norm_a, v_q_norm_b, v_k_norm_b, v_sinks, v_ssm_w_in, v_ssm_conv_w, v_ssm_conv_b, v_ssm_dt_bias, v_ssm_a_log, v_ssm_d, v_ssm_norm, v_ssm_w_out, v_ffn_w_in, v_ffn_conv_w, v_ffn_conv_b, v_ffn_w_out)))
    xi, yi, ci = _pos()
    me = 2 * xi + yi
    CC = D_INNER + 2 * SSM_GROUPS * SSM_STATE
    c4, f4 = CC // N_CHIPS, D_FF // N_CHIPS

    P = {k: d[k] for k in ["norm_mix", "norm_ffn", "q_norm_a", "k_norm_a", "q_norm_b", "k_norm_b", "sinks",
                           "ssm_dt_bias", "ssm_a_log", "ssm_d", "ffn_conv_b"]}
    P["relpos_table"] = d["relpos_table"][0]
    comm = MeshComm(d, xi, yi, ci)
    W = comm.first_weights()
    lpart, dx, gW, gP = local_step(d["x"][0], d["loss_target"][0], W, P, comm)
    loss = lax.psum(lpart[0, 0], ("x", "y", "c"))
    grads = comm.finish(gW["attn_in"], gW["attn_out"])

    flat = jnp.concatenate([gP[k].reshape(-1) for k in SMALL_ORDER])
    flat = jnp.pad(flat, (0, SMALL_ROWS * LANES - flat.shape[0])).reshape(SMALL_ROWS, LANES)
    tot = sum_slots(gather_all(flat, name="small_gather"), name="small_sum").reshape(-1)
    off = 0
    for k in SMALL_ORDER:
        n = int(np.prod(gP[k].shape))
        g = tot[off:off + n].reshape(gP[k].shape)
        off += n
        if k == "ssm_conv_w":
            g = lax.dynamic_slice_in_dim(g, me * c4, c4, axis=1)[None]
        elif k == "ssm_conv_b":
            g = lax.dynamic_slice_in_dim(g, me * c4, c4, axis=1)
        elif k == "ssm_norm":
            g = lax.dynamic_slice_in_dim(g, me * (D_INNER // N_CHIPS), D_INNER // N_CHIPS, axis=1)
        elif k == "ffn_conv_w":
            g = lax.dynamic_slice_in_dim(g, me * f4, f4, axis=2)
        elif k == "relpos_table":
            g = g[None]
        grads[k] = g

    deltas, new_m, new_v = {}, {}, {}
    for k in WEIGHTS:
        deltas[k], new_m[k], new_v[k] = _adamw(d[k], grads[k], d["m_" + k], d["v_" + k], k)
    return (loss, dx[None], *[grads[k] for k in WEIGHTS], *[deltas[k] for k in WEIGHTS],
            *[new_m[k] for k in WEIGHTS], *[new_v[k] for k in WEIGHTS])
```
